```python
import jax, jax.numpy as jnp
from jax import lax
import numpy as np

D_MODEL = 2048
BATCH = 8
SEQ = 8192
DEPTH = 4

CHUNK = 64
DN_HEAD_DIM = 128
DN_WIDTH = D_MODEL // 2
DN_HEADS = DN_WIDTH // DN_HEAD_DIM
DN_CONV = 4
SG_WIDTH = D_MODEL // 4
SG_GROUPS = 4
SG_GROUP_DIM = SG_WIDTH // SG_GROUPS
SG_BLOCK = 128
CV_WIDTH = D_MODEL // 4
CV_GROUPS = 4
CV_KERNEL = 31
D_MIX = DN_WIDTH + SG_WIDTH + CV_WIDTH
IN_WIDTHS = (DN_WIDTH, DN_WIDTH, DN_WIDTH, DN_WIDTH, DN_HEADS, DN_HEADS,
             SG_WIDTH, SG_WIDTH, SG_WIDTH, CV_WIDTH, CV_WIDTH, CV_WIDTH)
D_IN = sum(IN_WIDTHS)
EPS = 1e-6
LN_EPS = 1e-5

kernel_name = "hybrid_deltanet_gmlp_conformer_trunk"


def rmsnorm(x, g):
    xf = x.astype(jnp.float32)
    y = xf * lax.rsqrt(jnp.mean(xf * xf, axis=-1, keepdims=True) + EPS)
    return (y * g.astype(jnp.float32)).astype(x.dtype)


def group_layernorm(x, g, b, groups):
    xf = x.astype(jnp.float32).reshape(x.shape[:-1] + (groups, -1))
    mu = jnp.mean(xf, axis=-1, keepdims=True)
    var = jnp.mean(jnp.square(xf - mu), axis=-1, keepdims=True)
    y = ((xf - mu) * lax.rsqrt(var + LN_EPS)).reshape(x.shape)
    return (y * g.astype(jnp.float32) + b.astype(jnp.float32)).astype(x.dtype)


def l2norm(t):
    return t * lax.rsqrt(jnp.sum(t * t, axis=-1, keepdims=True) + EPS)


def causal_dwconv(x, w):
    k = w.shape[0]
    return lax.conv_general_dilated(x, w[:, None, :], window_strides=(1,), padding=[(k - 1, 0)],
                                    dimension_numbers=('NWC', 'WIO', 'NWC'),
                                    feature_group_count=x.shape[-1])


def gated_delta_rule(q, k, v, g, beta):
    bsz, t_len, n_h, dk = q.shape
    dv = v.shape[-1]
    n = t_len // CHUNK

    def to_chunks(t):
        t = t.reshape((bsz, n, CHUNK, n_h) + t.shape[3:])
        return jnp.moveaxis(t, 3, 1)

    q = to_chunks(q) * (dk ** -0.5)
    k, v = to_chunks(k), to_chunks(v)
    beta, g = to_chunks(beta), to_chunks(g)
    gc = jnp.cumsum(g, axis=-1)
    tri_incl = jnp.tril(jnp.ones((CHUNK, CHUNK), dtype=bool))
    tri_strict = jnp.tril(jnp.ones((CHUNK, CHUNK), dtype=bool), k=-1)
    diff = gc[..., :, None] - gc[..., None, :]
    decay = jnp.where(tri_incl, jnp.exp(jnp.where(tri_incl, diff, 0.0)), 0.0)
    k_beta = k * beta[..., None]
    kk = jnp.einsum('bhncd,bhnsd->bhncs', k_beta, k) * decay
    a_mat = jnp.eye(CHUNK, dtype=jnp.float32) + jnp.where(tri_strict, kk, 0.0)
    rhs = jnp.concatenate([v * beta[..., None], k_beta * jnp.exp(gc)[..., None]], axis=-1)
    sol = lax.linalg.triangular_solve(a_mat, rhs, left_side=True, lower=True, unit_diagonal=True)
    u, w = sol[..., :dv], sol[..., dv:]
    qk = jnp.einsum('bhncd,bhnsd->bhncs', q, k) * decay

    xs = tuple(jnp.moveaxis(t, 2, 0) for t in (q, k, u, w, qk, gc))

    def step(state, inp):
        q_i, k_i, u_i, w_i, qk_i, g_i = inp
        v_new = u_i - jnp.einsum('bhck,bhkv->bhcv', w_i, state)
        o = (jnp.einsum('bhck,bhkv->bhcv', q_i * jnp.exp(g_i)[..., None], state)
             + jnp.einsum('bhcs,bhsv->bhcv', qk_i, v_new))
        g_last = g_i[..., -1:]
        state = (state * jnp.exp(g_last)[..., None]
                 + jnp.einsum('bhck,bhcv->bhkv', k_i * jnp.exp(g_last - g_i)[..., None], v_new))
        return state, o

    s0 = jnp.zeros((bsz, n_h, dk, dv), jnp.float32)
    _, o = lax.scan(step, s0, xs)
    return jnp.transpose(o, (1, 0, 3, 2, 4)).reshape(bsz, t_len, n_h, dv)


def hybrid_layer(x, mod, norm_g, w_in, conv_qkv, a_log, dt_bias, dn_norm_g,
                 sg_ln_g, sg_ln_b, sg_w, sg_b, cv_w, cv_b, cv_ln_g, cv_ln_b, w_out):
    bsz, t_len, _ = x.shape
    shift, scale, gate = jnp.split(mod, 3, axis=-1)
    h = rmsnorm(x, norm_g) * (1 + scale[:, None, :]) + shift[:, None, :]
    p = h @ w_in
    split_points = [int(s) for s in np.cumsum(IN_WIDTHS)[:-1]]
    (q, k, v, z, b_dn, a_dn, u_sg, v_sg, gate_sg, a_cv, b_cv, gate_cv) = jnp.split(p, split_points, axis=-1)

    qkv = jax.nn.silu(causal_dwconv(jnp.concatenate([q, k, v], axis=-1), conv_qkv))
    q, k, v = jnp.split(qkv, 3, axis=-1)
    heads = lambda t: t.reshape(bsz, t_len, DN_HEADS, DN_HEAD_DIM).astype(jnp.float32)
    q, k, v = l2norm(heads(q)), l2norm(heads(k)), heads(v)
    beta = jax.nn.sigmoid(b_dn.astype(jnp.float32))
    g = -jnp.exp(a_log.astype(jnp.float32)) * jax.nn.softplus(a_dn.astype(jnp.float32) + dt_bias.astype(jnp.float32))
    o = rmsnorm(gated_delta_rule(q, k, v, g, beta), dn_norm_g)
    y_dn = o.reshape(bsz, t_len, DN_WIDTH).astype(x.dtype) * jax.nn.silu(z)

    u_sg = jax.nn.gelu(u_sg, approximate=False)
    v_sg = group_layernorm(jax.nn.gelu(v_sg, approximate=False), sg_ln_g, sg_ln_b, SG_GROUPS)
    v_blk = v_sg.reshape(bsz, t_len // SG_BLOCK, SG_BLOCK, SG_GROUPS, SG_GROUP_DIM)
    pos_chunk = jnp.arange(SG_BLOCK) // CHUNK
    block_causal = pos_chunk[:, None] >= pos_chunk[None, :]
    w_s = jnp.where(block_causal, sg_w, 0)
    mixed = jnp.einsum('gts,bnsgc->bntgc', w_s, v_blk) + sg_b.T[None, None, :, :, None]
    y_sg = u_sg * mixed.reshape(bsz, t_len, SG_WIDTH) * jax.nn.silu(gate_sg)

    glu = a_cv * jax.nn.sigmoid(b_cv)
    dw = causal_dwconv(glu, cv_w) + cv_b
    y_cv = jax.nn.silu(group_layernorm(dw, cv_ln_g, cv_ln_b, CV_GROUPS)) * jax.nn.silu(gate_cv)

    y = jnp.concatenate([y_dn, y_sg, y_cv], axis=-1) @ w_out
    return x + gate[:, None, :] * y


def _fwd_setup_inputs(seed: int = 0) -> dict:
    key = jax.random.key(seed)
    ks = jax.random.split(key, 24)
    f32 = jnp.float32
    nrm = lambda k, shape, s: jax.random.normal(k, shape, f32) * s
    L, D = DEPTH, D_MODEL
    dt = jnp.exp(jax.random.uniform(ks[7], (L, DN_HEADS), f32, np.log(1e-3), np.log(1e-1)))
    return {
        'x': nrm(ks[0], (BATCH, SEQ, D), 1.0),
        'c': nrm(ks[1], (BATCH, D), 1.0),
        'norm_g': 1.0 + nrm(ks[2], (L, D), 0.02),
        'w_ada': nrm(ks[3], (L, D, 3 * D), 0.5 * D ** -0.5),
        'b_ada': nrm(ks[4], (L, 3 * D), 0.02),
        'w_in': nrm(ks[5], (L, D, D_IN), D ** -0.5),
        'conv_qkv': nrm(ks[6], (L, DN_CONV, 3 * DN_WIDTH), DN_CONV ** -0.5),
        'a_log': jnp.log(jax.random.uniform(ks[8], (L, DN_HEADS), f32, 1.0, 16.0)),
        'dt_bias': dt + jnp.log(-jnp.expm1(-dt)),
        'dn_norm_g': 1.0 + nrm(ks[9], (L, DN_HEAD_DIM), 0.02),
        'sg_ln_g': 1.0 + nrm(ks[10], (L, SG_WIDTH), 0.02),
        'sg_ln_b': nrm(ks[11], (L, SG_WIDTH), 0.02),
        'sg_w': nrm(ks[12], (L, SG_GROUPS, SG_BLOCK, SG_BLOCK), SG_BLOCK ** -0.5),
        'sg_b': 1.0 + nrm(ks[13], (L, SG_GROUPS, SG_BLOCK), 0.02),
        'cv_w': nrm(ks[14], (L, CV_KERNEL, CV_WIDTH), CV_KERNEL ** -0.5),
        'cv_b': nrm(ks[15], (L, CV_WIDTH), 0.02),
        'cv_ln_g': 1.0 + nrm(ks[16], (L, CV_WIDTH), 0.02),
        'cv_ln_b': nrm(ks[17], (L, CV_WIDTH), 0.02),
        'w_out': nrm(ks[18], (L, D_MIX, D), D_MIX ** -0.5),
        'final_g': 1.0 + nrm(ks[19], (D,), 0.02),
    }


def _fwd_reference(x, c, norm_g, w_ada, b_ada, w_in, conv_qkv, a_log, dt_bias, dn_norm_g,
              sg_ln_g, sg_ln_b, sg_w, sg_b, cv_w, cv_b, cv_ln_g, cv_ln_b, w_out, final_g):
    c_act = jax.nn.silu(c)
    for l in range(DEPTH):
        mod = c_act @ w_ada[l] + b_ada[l]
        x = hybrid_layer(x, mod, norm_g[l], w_in[l], conv_qkv[l], a_log[l], dt_bias[l], dn_norm_g[l],
                         sg_ln_g[l], sg_ln_b[l], sg_w[l], sg_b[l], cv_w[l], cv_b[l], cv_ln_g[l], cv_ln_b[l],
                         w_out[l])
    return rmsnorm(x, final_g)


import jax as _jax
import jax.numpy as _jnp

TWIN_FORMAT = 'train_step'
FWD_PARAMS = ['x', 'c', 'norm_g', 'w_ada', 'b_ada', 'w_in', 'conv_qkv', 'a_log', 'dt_bias', 'dn_norm_g', 'sg_ln_g', 'sg_ln_b', 'sg_w', 'sg_b', 'cv_w', 'cv_b', 'cv_ln_g', 'cv_ln_b', 'w_out', 'final_g']
TWIN_WEIGHTS = ['norm_g', 'w_ada', 'b_ada', 'w_in', 'conv_qkv', 'a_log', 'dt_bias', 'dn_norm_g', 'sg_ln_g', 'sg_ln_b', 'sg_w', 'sg_b', 'cv_w', 'cv_b', 'cv_ln_g', 'cv_ln_b', 'w_out', 'final_g']
TWIN_DIFF_INPUT = 'x'
TWIN_INPUTS = ['x', 'c', 'norm_g', 'w_ada', 'b_ada', 'w_in', 'conv_qkv', 'a_log', 'dt_bias', 'dn_norm_g', 'sg_ln_g', 'sg_ln_b', 'sg_w', 'sg_b', 'cv_w', 'cv_b', 'cv_ln_g', 'cv_ln_b', 'w_out', 'final_g', 'loss_target', 'm_norm_g', 'm_w_ada', 'm_b_ada', 'm_w_in', 'm_conv_qkv', 'm_a_log', 'm_dt_bias', 'm_dn_norm_g', 'm_sg_ln_g', 'm_sg_ln_b', 'm_sg_w', 'm_sg_b', 'm_cv_w', 'm_cv_b', 'm_cv_ln_g', 'm_cv_ln_b', 'm_w_out', 'm_final_g', 'v_norm_g', 'v_w_ada', 'v_b_ada', 'v_w_in', 'v_conv_qkv', 'v_a_log', 'v_dt_bias', 'v_dn_norm_g', 'v_sg_ln_g', 'v_sg_ln_b', 'v_sg_w', 'v_sg_b', 'v_cv_w', 'v_cv_b', 'v_cv_ln_g', 'v_cv_ln_b', 'v_w_out', 'v_final_g']
TWIN_OUTPUTS = ['loss', 'grad_x', 'grad_norm_g', 'grad_w_ada', 'grad_b_ada', 'grad_w_in', 'grad_conv_qkv', 'grad_a_log', 'grad_dt_bias', 'grad_dn_norm_g', 'grad_sg_ln_g', 'grad_sg_ln_b', 'grad_sg_w', 'grad_sg_b', 'grad_cv_w', 'grad_cv_b', 'grad_cv_ln_g', 'grad_cv_ln_b', 'grad_w_out', 'grad_final_g', 'delta_norm_g', 'delta_w_ada', 'delta_b_ada', 'delta_w_in', 'delta_conv_qkv', 'delta_a_log', 'delta_dt_bias', 'delta_dn_norm_g', 'delta_sg_ln_g', 'delta_sg_ln_b', 'delta_sg_w', 'delta_sg_b', 'delta_cv_w', 'delta_cv_b', 'delta_cv_ln_g', 'delta_cv_ln_b', 'delta_w_out', 'delta_final_g', 'new_m_norm_g', 'new_m_w_ada', 'new_m_b_ada', 'new_m_w_in', 'new_m_conv_qkv', 'new_m_a_log', 'new_m_dt_bias', 'new_m_dn_norm_g', 'new_m_sg_ln_g', 'new_m_sg_ln_b', 'new_m_sg_w', 'new_m_sg_b', 'new_m_cv_w', 'new_m_cv_b', 'new_m_cv_ln_g', 'new_m_cv_ln_b', 'new_m_w_out', 'new_m_final_g', 'new_v_norm_g', 'new_v_w_ada', 'new_v_b_ada', 'new_v_w_in', 'new_v_conv_qkv', 'new_v_a_log', 'new_v_dt_bias', 'new_v_dn_norm_g', 'new_v_sg_ln_g', 'new_v_sg_ln_b', 'new_v_sg_w', 'new_v_sg_b', 'new_v_cv_w', 'new_v_cv_b', 'new_v_cv_ln_g', 'new_v_cv_ln_b', 'new_v_w_out', 'new_v_final_g']
TWIN_LEAF_KINDS = {'loss': 'loss', 'grad_x': 'grad_x', 'grad_norm_g': 'grad_w', 'grad_w_ada': 'grad_w', 'grad_b_ada': 'grad_w', 'grad_w_in': 'grad_w', 'grad_conv_qkv': 'grad_w', 'grad_a_log': 'grad_w', 'grad_dt_bias': 'grad_w', 'grad_dn_norm_g': 'grad_w', 'grad_sg_ln_g': 'grad_w', 'grad_sg_ln_b': 'grad_w', 'grad_sg_w': 'grad_w', 'grad_sg_b': 'grad_w', 'grad_cv_w': 'grad_w', 'grad_cv_b': 'grad_w', 'grad_cv_ln_g': 'grad_w', 'grad_cv_ln_b': 'grad_w', 'grad_w_out': 'grad_w', 'grad_final_g': 'grad_w', 'delta_norm_g': 'delta_w', 'delta_w_ada': 'delta_w', 'delta_b_ada': 'delta_w', 'delta_w_in': 'delta_w', 'delta_conv_qkv': 'delta_w', 'delta_a_log': 'delta_w', 'delta_dt_bias': 'delta_w', 'delta_dn_norm_g': 'delta_w', 'delta_sg_ln_g': 'delta_w', 'delta_sg_ln_b': 'delta_w', 'delta_sg_w': 'delta_w', 'delta_sg_b': 'delta_w', 'delta_cv_w': 'delta_w', 'delta_cv_b': 'delta_w', 'delta_cv_ln_g': 'delta_w', 'delta_cv_ln_b': 'delta_w', 'delta_w_out': 'delta_w', 'delta_final_g': 'delta_w', 'new_m_norm_g': 'new_m', 'new_m_w_ada': 'new_m', 'new_m_b_ada': 'new_m', 'new_m_w_in': 'new_m', 'new_m_conv_qkv': 'new_m', 'new_m_a_log': 'new_m', 'new_m_dt_bias': 'new_m', 'new_m_dn_norm_g': 'new_m', 'new_m_sg_ln_g': 'new_m', 'new_m_sg_ln_b': 'new_m', 'new_m_sg_w': 'new_m', 'new_m_sg_b': 'new_m', 'new_m_cv_w': 'new_m', 'new_m_cv_b': 'new_m', 'new_m_cv_ln_g': 'new_m', 'new_m_cv_ln_b': 'new_m', 'new_m_w_out': 'new_m', 'new_m_final_g': 'new_m', 'new_v_norm_g': 'new_v', 'new_v_w_ada': 'new_v', 'new_v_b_ada': 'new_v', 'new_v_w_in': 'new_v', 'new_v_conv_qkv': 'new_v', 'new_v_a_log': 'new_v', 'new_v_dt_bias': 'new_v', 'new_v_dn_norm_g': 'new_v', 'new_v_sg_ln_g': 'new_v', 'new_v_sg_ln_b': 'new_v', 'new_v_sg_w': 'new_v', 'new_v_sg_b': 'new_v', 'new_v_cv_w': 'new_v', 'new_v_cv_b': 'new_v', 'new_v_cv_ln_g': 'new_v', 'new_v_cv_ln_b': 'new_v', 'new_v_w_out': 'new_v', 'new_v_final_g': 'new_v'}


def _forward(args):
    return _fwd_reference(*[args[k] for k in FWD_PARAMS])


def _output_shape():
    def fwd():
        inp = _fwd_setup_inputs(0)
        return _fwd_reference(*[inp[k] for k in FWD_PARAMS])
    out = _jax.eval_shape(fwd)
    return out.shape, out.dtype

N_MICROBATCH = 1
ADAM_LR = 0.001
ADAM_B1 = 0.9
ADAM_B2 = 0.999
ADAM_EPS = 1e-08
ADAM_WD = 0.01
ADAM_STEP = 10
PER_EXAMPLE_BATCH_AXIS = {'x': 0, 'c': 0, 'loss_target': 0}
SHARED_INPUTS = []
_WEIGHT_DTYPES = {'norm_g': _jnp.float32, 'w_ada': _jnp.float32, 'b_ada': _jnp.float32, 'w_in': _jnp.float32, 'conv_qkv': _jnp.float32, 'a_log': _jnp.float32, 'dt_bias': _jnp.float32, 'dn_norm_g': _jnp.float32, 'sg_ln_g': _jnp.float32, 'sg_ln_b': _jnp.float32, 'sg_w': _jnp.float32, 'sg_b': _jnp.float32, 'cv_w': _jnp.float32, 'cv_b': _jnp.float32, 'cv_ln_g': _jnp.float32, 'cv_ln_b': _jnp.float32, 'w_out': _jnp.float32, 'final_g': _jnp.float32}
MOMENT_SCALE = {'norm_g': 3.448435e-02, 'w_ada': 3.151745e-02, 'b_ada': 5.473730e-02, 'w_in': 1.942093e-02, 'conv_qkv': 1.840875e-02, 'a_log': 9.107454e-02, 'dt_bias': 8.845606e-02, 'dn_norm_g': 6.925649e-02, 'sg_ln_g': 1.609405e-02, 'sg_ln_b': 1.585620e-02, 'sg_w': 1.539471e-02, 'sg_b': 1.999954e-02, 'cv_w': 1.531356e-02, 'cv_b': 2.973076e-02, 'cv_ln_g': 1.785468e-02, 'cv_ln_b': 1.482902e-02, 'w_out': 2.209795e-02, 'final_g': 3.198005e+01}


def _to_microbatches(a, axis):
    t = _jnp.moveaxis(a, axis, 0)
    t = t.reshape((N_MICROBATCH, t.shape[0] // N_MICROBATCH) + t.shape[1:])
    return _jnp.moveaxis(t, 1, axis + 1)


def setup_inputs(seed: int = 0) -> dict:
    inp = _fwd_setup_inputs(seed)
    key = _jax.random.fold_in(_jax.random.key(seed), 7919)
    shape, _ = _output_shape()
    out = dict(inp)
    out["loss_target"] = _jax.random.normal(_jax.random.fold_in(key, 0), shape, _jnp.float32)
    for i, name in enumerate(TWIN_WEIGHTS):
        w = inp[name].astype(_jnp.float32)
        if MOMENT_SCALE is None:
            s = _jnp.sqrt(_jnp.mean(_jnp.square(w)) + 1e-30)
        else:
            s = MOMENT_SCALE[name]
        km, kv = _jax.random.split(_jax.random.fold_in(key, i + 1))
        out[name] = w
        out["m_" + name] = s * _jax.random.normal(km, w.shape, _jnp.float32)
        out["v_" + name] = (s * s) * _jax.random.uniform(kv, w.shape, _jnp.float32, 0.5, 1.5)
    if N_MICROBATCH > 1:
        for name, axis in PER_EXAMPLE_BATCH_AXIS.items():
            out[name] = _to_microbatches(out[name], axis)
    return {'x': out['x'], 'c': out['c'], 'norm_g': out['norm_g'], 'w_ada': out['w_ada'], 'b_ada': out['b_ada'], 'w_in': out['w_in'], 'conv_qkv': out['conv_qkv'], 'a_log': out['a_log'], 'dt_bias': out['dt_bias'], 'dn_norm_g': out['dn_norm_g'], 'sg_ln_g': out['sg_ln_g'], 'sg_ln_b': out['sg_ln_b'], 'sg_w': out['sg_w'], 'sg_b': out['sg_b'], 'cv_w': out['cv_w'], 'cv_b': out['cv_b'], 'cv_ln_g': out['cv_ln_g'], 'cv_ln_b': out['cv_ln_b'], 'w_out': out['w_out'], 'final_g': out['final_g'], 'loss_target': out['loss_target'], 'm_norm_g': out['m_norm_g'], 'm_w_ada': out['m_w_ada'], 'm_b_ada': out['m_b_ada'], 'm_w_in': out['m_w_in'], 'm_conv_qkv': out['m_conv_qkv'], 'm_a_log': out['m_a_log'], 'm_dt_bias': out['m_dt_bias'], 'm_dn_norm_g': out['m_dn_norm_g'], 'm_sg_ln_g': out['m_sg_ln_g'], 'm_sg_ln_b': out['m_sg_ln_b'], 'm_sg_w': out['m_sg_w'], 'm_sg_b': out['m_sg_b'], 'm_cv_w': out['m_cv_w'], 'm_cv_b': out['m_cv_b'], 'm_cv_ln_g': out['m_cv_ln_g'], 'm_cv_ln_b': out['m_cv_ln_b'], 'm_w_out': out['m_w_out'], 'm_final_g': out['m_final_g'], 'v_norm_g': out['v_norm_g'], 'v_w_ada': out['v_w_ada'], 'v_b_ada': out['v_b_ada'], 'v_w_in': out['v_w_in'], 'v_conv_qkv': out['v_conv_qkv'], 'v_a_log': out['v_a_log'], 'v_dt_bias': out['v_dt_bias'], 'v_dn_norm_g': out['v_dn_norm_g'], 'v_sg_ln_g': out['v_sg_ln_g'], 'v_sg_ln_b': out['v_sg_ln_b'], 'v_sg_w': out['v_sg_w'], 'v_sg_b': out['v_sg_b'], 'v_cv_w': out['v_cv_w'], 'v_cv_b': out['v_cv_b'], 'v_cv_ln_g': out['v_cv_ln_g'], 'v_cv_ln_b': out['v_cv_ln_b'], 'v_w_out': out['v_w_out'], 'v_final_g': out['v_final_g']}


def _loss(weights, diff, rest, loss_target):
    with _jax.named_scope("forward"):
        args = {**rest, TWIN_DIFF_INPUT: diff, **{k: w.astype(_WEIGHT_DTYPES[k]) for k, w in weights.items()}}
        y = _forward(args)
    with _jax.named_scope("loss_head"):
        err = _jnp.square(y.astype(_jnp.float32) - loss_target)
        return 0.5 * _jnp.sum(_jnp.mean(err, axis=-1)) if err.ndim else 0.5 * err


def _adamw(w, g, m, v):
    m = ADAM_B1 * m + (1.0 - ADAM_B1) * g
    v = ADAM_B2 * v + (1.0 - ADAM_B2) * _jnp.square(g)
    m_hat = m / (1.0 - ADAM_B1 ** ADAM_STEP)
    v_hat = v / (1.0 - ADAM_B2 ** ADAM_STEP)
    delta = -ADAM_LR * (m_hat / (_jnp.sqrt(v_hat) + ADAM_EPS) + ADAM_WD * w)
    return delta, m, v


def reference(x, c, norm_g, w_ada, b_ada, w_in, conv_qkv, a_log, dt_bias, dn_norm_g, sg_ln_g, sg_ln_b, sg_w, sg_b, cv_w, cv_b, cv_ln_g, cv_ln_b, w_out, final_g, loss_target, m_norm_g, m_w_ada, m_b_ada, m_w_in, m_conv_qkv, m_a_log, m_dt_bias, m_dn_norm_g, m_sg_ln_g, m_sg_ln_b, m_sg_w, m_sg_b, m_cv_w, m_cv_b, m_cv_ln_g, m_cv_ln_b, m_w_out, m_final_g, v_norm_g, v_w_ada, v_b_ada, v_w_in, v_conv_qkv, v_a_log, v_dt_bias, v_dn_norm_g, v_sg_ln_g, v_sg_ln_b, v_sg_w, v_sg_b, v_cv_w, v_cv_b, v_cv_ln_g, v_cv_ln_b, v_w_out, v_final_g):
    given = dict(x=x, c=c, norm_g=norm_g, w_ada=w_ada, b_ada=b_ada, w_in=w_in, conv_qkv=conv_qkv, a_log=a_log, dt_bias=dt_bias, dn_norm_g=dn_norm_g, sg_ln_g=sg_ln_g, sg_ln_b=sg_ln_b, sg_w=sg_w, sg_b=sg_b, cv_w=cv_w, cv_b=cv_b, cv_ln_g=cv_ln_g, cv_ln_b=cv_ln_b, w_out=w_out, final_g=final_g, loss_target=loss_target, m_norm_g=m_norm_g, m_w_ada=m_w_ada, m_b_ada=m_b_ada, m_w_in=m_w_in, m_conv_qkv=m_conv_qkv, m_a_log=m_a_log, m_dt_bias=m_dt_bias, m_dn_norm_g=m_dn_norm_g, m_sg_ln_g=m_sg_ln_g, m_sg_ln_b=m_sg_ln_b, m_sg_w=m_sg_w, m_sg_b=m_sg_b, m_cv_w=m_cv_w, m_cv_b=m_cv_b, m_cv_ln_g=m_cv_ln_g, m_cv_ln_b=m_cv_ln_b, m_w_out=m_w_out, m_final_g=m_final_g, v_norm_g=v_norm_g, v_w_ada=v_w_ada, v_b_ada=v_b_ada, v_w_in=v_w_in, v_conv_qkv=v_conv_qkv, v_a_log=v_a_log, v_dt_bias=v_dt_bias, v_dn_norm_g=v_dn_norm_g, v_sg_ln_g=v_sg_ln_g, v_sg_ln_b=v_sg_ln_b, v_sg_w=v_sg_w, v_sg_b=v_sg_b, v_cv_w=v_cv_w, v_cv_b=v_cv_b, v_cv_ln_g=v_cv_ln_g, v_cv_ln_b=v_cv_ln_b, v_w_out=v_w_out, v_final_g=v_final_g)
    weights = {n: given[n] for n in TWIN_WEIGHTS}
    shared = {n: given[n] for n in SHARED_INPUTS}
    per_example = {n: given[n] for n in ['x', 'c']}
    grad_fn = _jax.value_and_grad(_loss, argnums=(0, 1))

    def one_microbatch(ex, loss_target):
        ex = dict(ex)
        diff = ex.pop(TWIN_DIFF_INPUT)
        return grad_fn(weights, diff, {**shared, **ex}, loss_target)

    if N_MICROBATCH == 1:
        loss, (grad_w, grad_x) = one_microbatch(per_example, given["loss_target"])
    else:
        def body(carry, xs):
            loss_sum, grad_sum = carry
            l_k, (gw_k, gx_k) = one_microbatch(xs[0], xs[1])
            with _jax.named_scope("update"):
                return (loss_sum + l_k, _jax.tree.map(_jnp.add, grad_sum, gw_k)), gx_k

        init = (_jnp.zeros((), _jnp.float32), _jax.tree.map(_jnp.zeros_like, weights))
        (loss, grad_w), grad_x = _jax.lax.scan(body, init, (per_example, given["loss_target"]))
    with _jax.named_scope("update"):
        delta_w, new_m, new_v = {}, {}, {}
        for n in TWIN_WEIGHTS:
            delta_w[n], new_m[n], new_v[n] = _adamw(weights[n], grad_w[n], given["m_" + n], given["v_" + n])
    return (loss, grad_x, *[grad_w[n] for n in TWIN_WEIGHTS], *[delta_w[n] for n in TWIN_WEIGHTS],
            *[new_m[n] for n in TWIN_WEIGHTS], *[new_v[n] for n in TWIN_WEIGHTS])
```

```python
import functools
import math

import jax
import jax.numpy as jnp
from jax import lax
from jax.experimental import pallas as pl
from jax.experimental.pallas import tpu as pltpu

F32 = jnp.float32
BF16 = jnp.bfloat16
EPS = 1e-6
LN_EPS = 1e-5
LANES = 128
CHUNK = 64
HALO4 = 8
HALO31 = 32
NCHIP = 4
NDEV = 8
VMEM_LIMIT = 56 * 2 ** 20
ADAM_LR, ADAM_B1, ADAM_B2, ADAM_EPS, ADAM_WD, ADAM_STEP = 0.001, 0.9, 0.999, 1e-08, 0.01, 10
MESH = pl.DeviceIdType.MESH
ANY = pl.BlockSpec(memory_space=pl.ANY)


def _cp(sem=None, vmem=VMEM_LIMIT):
    return pltpu.CompilerParams(dimension_semantics=sem, vmem_limit_bytes=vmem)


def _tile(n, pref, mult):
    t = min(n, pref) // mult * mult
    while t > 0 and n % t:
        t -= mult
    return t if t > 0 else n


def _raw_dot(a, b, ca, cb, hi):
    dn = (((ca,), (cb,)), ((), ()))
    if hi:
        return lax.dot_general(a.astype(F32), b.astype(F32), dn, precision=lax.Precision.HIGHEST,
                               preferred_element_type=F32)
    return lax.dot_general(a.astype(BF16), b.astype(BF16), dn, preferred_element_type=F32)


@functools.partial(jax.custom_vjp, nondiff_argnums=(2, 3, 4))
def bdot(a, b, ca, cb, hi):
    return _raw_dot(a, b, ca, cb, hi)


def _bdot_fwd(a, b, ca, cb, hi):
    return _raw_dot(a, b, ca, cb, hi), (a, b)


def _bdot_bwd(ca, cb, hi, res, ct):
    a, b = res
    fa, fb = 1 - ca, 1 - cb
    da = _raw_dot(ct, b, 1, fb, hi) if ca == 1 else _raw_dot(b, ct, fb, 1, hi)
    db = _raw_dot(a, ct, fa, 0, hi) if cb == 0 else _raw_dot(ct, a, 0, fa, hi)
    return da.astype(a.dtype), db.astype(b.dtype)


bdot.defvjp(_bdot_fwd, _bdot_bwd)


def _sigmoid(x):
    return 1.0 / (1.0 + jnp.exp(-x))


def _silu(x):
    return x * _sigmoid(x)


def _gelu(x):
    return 0.5 * x * (1.0 + lax.erf(x * (2.0 ** -0.5)))


def _softplus(x):
    return jnp.maximum(x, 0.0) + jnp.log(1.0 + jnp.exp(-jnp.abs(x)))


def _modnorm(x, g, scale, shift):
    y = x * lax.rsqrt(jnp.mean(x * x, axis=-1, keepdims=True) + EPS)
    return (y * g) * (1.0 + scale) + shift


def _rmsnorm(x, g):
    return x * lax.rsqrt(jnp.mean(x * x, axis=-1, keepdims=True) + EPS) * g


def _layernorm(x, g, b):
    mu = jnp.mean(x, axis=-1, keepdims=True)
    xc = x - mu
    var = jnp.mean(xc * xc, axis=-1, keepdims=True)
    return xc * lax.rsqrt(var + LN_EPS) * g + b


def _l2norm(t):
    return t * lax.rsqrt(jnp.sum(t * t, axis=-1, keepdims=True) + EPS)


def _adamw_math(w, g, m, v):
    mn = ADAM_B1 * m + (1.0 - ADAM_B1) * g
    vn = ADAM_B2 * v + (1.0 - ADAM_B2) * (g * g)
    mh = mn / (1.0 - ADAM_B1 ** ADAM_STEP)
    vh = vn / (1.0 - ADAM_B2 ** ADAM_STEP)
    delta = -ADAM_LR * (mh / (jnp.sqrt(vh) + ADAM_EPS) + ADAM_WD * w)
    return delta, mn, vn


def _dn_chunk(q, k, v, bcol, acol, z, s, alog, dtb, ng):
    c = CHUNK
    r = lax.broadcasted_iota(jnp.int32, (c, c), 0)
    cc = lax.broadcasted_iota(jnp.int32, (c, c), 1)
    tri_incl, tri_strict, eye = r >= cc, r > cc, r == cc
    beta = _sigmoid(bcol)
    g = -jnp.exp(alog) * _softplus(acol + dtb)
    gb = jnp.broadcast_to(g, (c, c))
    g_row = jnp.sum(jnp.where(eye, gb, 0.0), axis=0, keepdims=True)
    gc_col = jnp.sum(jnp.where(tri_incl, jnp.broadcast_to(g_row, (c, c)), 0.0), axis=1, keepdims=True)
    gc_row = jnp.sum(jnp.where(r <= cc, gb, 0.0), axis=0, keepdims=True)
    decay = jnp.where(tri_incl, jnp.exp(jnp.where(tri_incl, gc_col - gc_row, 0.0)), 0.0)
    qs = q * (q.shape[-1] ** -0.5)
    kb = k * beta
    a = jnp.where(tri_strict, bdot(kb, k, 1, 1, False) * decay, 0.0)
    xv = v * beta
    xk = kb * jnp.exp(gc_col)
    xv = xv - bdot(a, xv, 1, 0, True)
    xk = xk - bdot(a, xk, 1, 0, True)
    p = a
    for _ in range(5):
        p = bdot(p, p, 1, 0, True)
        xv = xv + bdot(p, xv, 1, 0, True)
        xk = xk + bdot(p, xk, 1, 0, True)
    qk = bdot(qs, k, 1, 1, False) * decay
    v_new = xv - bdot(xk, s, 1, 0, False)
    o = bdot(qs * jnp.exp(gc_col), s, 1, 0, False) + bdot(qk, v_new, 1, 0, False)
    rr = lax.broadcasted_iota(jnp.int32, (c, 1), 0)
    g_last = jnp.sum(jnp.where(rr == c - 1, gc_col, 0.0), axis=0, keepdims=True)
    s_next = s * jnp.exp(g_last) + bdot(k * jnp.exp(g_last - gc_col), v_new, 0, 0, False)
    y = _rmsnorm(o, ng) * _silu(z)
    return y, s_next


def _sg_block(u, v, gt, lg, lb, w, bias):
    n = w.shape[0]
    pr = lax.broadcasted_iota(jnp.int32, (n, n), 0) // CHUNK
    pc = lax.broadcasted_iota(jnp.int32, (n, n), 1) // CHUNK
    wm = jnp.where(pr >= pc, w, 0.0)
    vl = _layernorm(_gelu(v), lg, lb)
    mixed = bdot(wm, vl, 1, 0, False) + bias
    return _gelu(u) * mixed * _silu(gt)


def _glu(a, b):
    return a * _sigmoid(b)


def _cv_post(conv, gate, cb, lg, lb):
    return _silu(_layernorm(conv + cb, lg, lb)) * _silu(gate)


def _qk_post(conv):
    return _l2norm(_silu(conv))


def _v_post(conv):
    return _silu(conv)


def in_proj(x, shift, scale, ng, wp):
    t, d = x.shape
    npc = wp.shape[1]
    tm, tn = _tile(t, 1024, LANES), _tile(npc, 384, LANES)

    def body(x_ref, sh_ref, sc_ref, g_ref, w_ref, p_ref, ht_ref, h_scr):
        @pl.when(pl.program_id(1) == 0)
        def _():
            h = _modnorm(x_ref[...], g_ref[...], sc_ref[...], sh_ref[...])
            h_scr[...] = h.astype(BF16)
            ht_ref[...] = h.T.astype(BF16)
        p_ref[...] = jnp.dot(h_scr[...], w_ref[...], preferred_element_type=F32)

    vec = pl.BlockSpec((1, d), lambda i, j: (0, 0))
    return pl.pallas_call(
        body, name="in_proj", grid=(t // tm, npc // tn),
        in_specs=[pl.BlockSpec((tm, d), lambda i, j: (i, 0)), vec, vec, vec,
                  pl.BlockSpec((d, tn), lambda i, j: (0, j))],
        out_specs=[pl.BlockSpec((tm, tn), lambda i, j: (i, j)), pl.BlockSpec((d, tm), lambda i, j: (0, i))],
        out_shape=[jax.ShapeDtypeStruct((t, npc), F32), jax.ShapeDtypeStruct((d, t), BF16)],
        scratch_shapes=[pltpu.VMEM((tm, d), BF16)],
        compiler_params=_cp(("parallel", "arbitrary")),
    )(x, shift, scale, ng, wp)


def conv_fwd(name, k, halo, pre_fn, pre, w, post_fns, extras, params, c_total, tc, tm_pref=512):
    t = pre[0][0].shape[0]
    tm = _tile(t, tm_pref, halo)
    npre, nex, npar = len(pre), len(extras), len(params)
    ngr = tc // LANES

    def body(*refs):
        prev = refs[:npre]
        cur = refs[npre:2 * npre]
        w_ref = refs[2 * npre]
        ex = refs[2 * npre + 1:2 * npre + 1 + nex]
        par = refs[2 * npre + 1 + nex:2 * npre + 1 + nex + npar]
        out_ref, buf = refs[-2], refs[-1]
        i = pl.program_id(1)
        pv = pre_fn(*[r[...] for r in prev])
        buf[0:halo, :] = jnp.where(i > 0, pv, 0.0)
        buf[halo:, :] = pre_fn(*[r[...] for r in cur])
        acc = w_ref[0:1, :] * buf[pl.ds(halo - (k - 1), tm), :]
        for j in range(1, k):
            acc = acc + w_ref[j:j + 1, :] * buf[pl.ds(halo - (k - 1) + j, tm), :]
        for gi in range(ngr):
            sl = slice(gi * LANES, (gi + 1) * LANES)
            out_ref[:, sl] = post_fns[gi](acc[:, sl], *[e[:, sl] for e in ex], *[p_[:, sl] for p_ in par])

    hb = tm // halo
    in_specs = ([pl.BlockSpec((halo, tc), functools.partial(lambda j, i, o: (jnp.maximum(i * hb - 1, 0), o + j), o=col // tc))
                 for _, col in pre]
                + [pl.BlockSpec((tm, tc), functools.partial(lambda j, i, o: (i, o + j), o=col // tc)) for _, col in pre]
                + [pl.BlockSpec((k, tc), lambda j, i: (0, j))]
                + [pl.BlockSpec((tm, tc), functools.partial(lambda j, i, o: (i, o + j), o=col // tc)) for _, col in extras]
                + [pl.BlockSpec((1, tc), lambda j, i: (0, j)) for _ in params])
    args = [a for a, _ in pre] * 2 + [w] + [a for a, _ in extras] + list(params)
    return pl.pallas_call(
        body, name=name, grid=(c_total // tc, t // tm), in_specs=in_specs,
        out_specs=pl.BlockSpec((tm, tc), lambda j, i: (i, j)),
        out_shape=jax.ShapeDtypeStruct((t, c_total), F32),
        scratch_shapes=[pltpu.VMEM((halo + tm, tc), F32)],
        compiler_params=_cp(("parallel", "arbitrary")),
    )(*args)


def conv_bwd(name, k, halo, pre_fn, pre, w, post_fns, extras, params, dout, c_total, tc, tm_pref=512):
    t = pre[0][0].shape[0]
    tm = _tile(t, tm_pref, halo)
    npre, nex, npar = len(pre), len(extras), len(params)
    ngr = tc // LANES
    nout = npre + nex
    assert nout == 1 or c_total == tc
    nblk = t // tm

    def body(*refs):
        it = iter(refs)
        prev = [next(it) for _ in range(npre)]
        cur = [next(it) for _ in range(npre)]
        nxt = [next(it) for _ in range(npre)]
        w_ref = next(it)
        ex_c = [next(it) for _ in range(nex)]
        ex_n = [next(it) for _ in range(nex)]
        par = [next(it) for _ in range(npar)]
        do_c, do_n = next(it), next(it)
        din_ref, dw_ref = next(it), next(it)
        dpar = [next(it) for _ in range(npar)]
        buf, dbuf = next(it), next(it)
        i = pl.program_id(1)

        @pl.when(i == 0)
        def _():
            dw_ref[...] = jnp.zeros_like(dw_ref)
            for r in dpar:
                r[...] = jnp.zeros_like(r)

        buf[0:halo, :] = jnp.where(i > 0, pre_fn(*[r[...] for r in prev]), 0.0)
        cur_vals = [r[...] for r in cur]
        buf[halo:halo + tm, :] = pre_fn(*cur_vals)
        buf[halo + tm:, :] = pre_fn(*[r[...] for r in nxt])
        ext = tm + halo
        conv = w_ref[0:1, :] * buf[pl.ds(halo - (k - 1), ext), :]
        for j in range(1, k):
            conv = conv + w_ref[j:j + 1, :] * buf[pl.ds(halo - (k - 1) + j, ext), :]
        don = jnp.where(i < nblk - 1, do_n[...], 0.0)
        for gi in range(ngr):
            sl = slice(gi * LANES, (gi + 1) * LANES)
            pv = [p_[:, sl] for p_ in par]
            _, vj = jax.vjp(post_fns[gi], conv[:tm, sl], *[e[:, sl] for e in ex_c], *pv)
            gr = vj(do_c[:, sl])
            dbuf[0:tm, sl] = gr[0]
            for e in range(nex):
                din_ref[:, (npre + e) * tc + gi * LANES:(npre + e) * tc + (gi + 1) * LANES] = gr[1 + e].astype(din_ref.dtype)
            for q_ in range(npar):
                dpar[q_][:, sl] += gr[1 + nex + q_]
            _, vjn = jax.vjp(post_fns[gi], conv[tm:, sl], *[e[:, sl] for e in ex_n], *pv)
            dbuf[tm:, sl] = vjn(don[:, sl])[0]
        dcur = dbuf[0:tm, :]
        dpre = w_ref[0:1, :] * dbuf[pl.ds(k - 1, tm), :]
        dw_ref[0:1, :] += jnp.sum(dcur * buf[pl.ds(halo - (k - 1), tm), :], axis=0, keepdims=True)
        for j in range(1, k):
            dpre = dpre + w_ref[j:j + 1, :] * dbuf[pl.ds(k - 1 - j, tm), :]
            dw_ref[j:j + 1, :] += jnp.sum(dcur * buf[pl.ds(halo - (k - 1) + j, tm), :], axis=0, keepdims=True)
        _, vjp_pre = jax.vjp(pre_fn, *cur_vals)
        for e, gval in enumerate(vjp_pre(dpre)):
            din_ref[:, e * tc:(e + 1) * tc] = gval.astype(din_ref.dtype)

    hb = tm // halo
    last_h = t // halo - 1

    def spec(kind, col):
        o = col // tc
        if kind == "prev":
            return pl.BlockSpec((halo, tc), lambda j, i: (jnp.maximum(i * hb - 1, 0), o + j))
        if kind == "next":
            return pl.BlockSpec((halo, tc), lambda j, i: (jnp.minimum((i + 1) * hb, last_h), o + j))
        return pl.BlockSpec((tm, tc), lambda j, i: (i, o + j))

    in_specs = ([spec("prev", col) for _, col in pre] + [spec("cur", col) for _, col in pre]
                + [spec("next", col) for _, col in pre] + [pl.BlockSpec((k, tc), lambda j, i: (0, j))]
                + [spec("cur", col) for _, col in extras] + [spec("next", col) for _, col in extras]
                + [pl.BlockSpec((1, tc), lambda j, i: (0, j)) for _ in params]
                + [spec("cur", 0), spec("next", 0)])
    args = [a for a, _ in pre] * 3 + [w] + [a for a, _ in extras] * 2 + list(params) + [dout, dout]
    out = pl.pallas_call(
        body, name=name, grid=(c_total // tc, nblk), in_specs=in_specs,
        out_specs=[pl.BlockSpec((tm, nout * tc), lambda j, i: (i, j)), pl.BlockSpec((k, tc), lambda j, i: (0, j))]
        + [pl.BlockSpec((1, tc), lambda j, i: (0, j)) for _ in params],
        out_shape=[jax.ShapeDtypeStruct((t, nout * c_total), BF16), jax.ShapeDtypeStruct((k, c_total), F32)]
        + [jax.ShapeDtypeStruct((1, c_total), F32) for _ in params],
        scratch_shapes=[pltpu.VMEM((2 * halo + tm, tc), F32), pltpu.VMEM((halo + tm, tc), F32)],
        compiler_params=_cp(("parallel", "arbitrary")),
    )(*args)
    return out[0], out[1], out[2:]


def _head_pick(ref_val, row):
    rr = lax.broadcasted_iota(jnp.int32, ref_val.shape, 0)
    v = jnp.sum(jnp.where(rr == row, ref_val, 0.0), axis=0, keepdims=True)
    ll = lax.broadcasted_iota(jnp.int32, v.shape, 1)
    return jnp.sum(jnp.where(ll == 0, v, 0.0), axis=1, keepdims=True)


def _lane_col(blk, lane_idx):
    ll = lax.broadcasted_iota(jnp.int32, blk.shape, 1)
    return jnp.sum(jnp.where(ll == lane_idx, blk, 0.0), axis=1, keepdims=True)


def dn_core_fwd(qkv, p, z_col, ba_col, alog_b, dtb_b, ng, nheads, hb_heads):
    t = qkv.shape[0]
    tm = _tile(t, 512, CHUNK)
    nc = tm // CHUNK
    hb = hb_heads
    hp = alog_b.shape[0]

    def body(qkv_ref, z_ref, ba_ref, al_ref, dt_ref, ng_ref, y_ref, ss_ref, s_scr):
        i, hblk = pl.program_id(0), pl.program_id(1)
        for hl in range(hb):
            @pl.when(i == 0)
            def _():
                s_scr[hblk * hb + hl] = jnp.zeros((LANES, LANES), F32)
        alv, dtv, ngv = al_ref[...], dt_ref[...], ng_ref[...]

        def chunk(ci, carry):
            r0 = pl.multiple_of(ci * CHUNK, CHUNK)
            ba = ba_ref[pl.ds(r0, CHUNK), :]
            for hl in range(hb):
                h = hblk * hb + hl
                q = qkv_ref[pl.ds(r0, CHUNK), hl * 384:hl * 384 + 128]
                k = qkv_ref[pl.ds(r0, CHUNK), hl * 384 + 128:hl * 384 + 256]
                v = qkv_ref[pl.ds(r0, CHUNK), hl * 384 + 256:hl * 384 + 384]
                z = z_ref[pl.ds(r0, CHUNK), hl * 128:(hl + 1) * 128]
                s = s_scr[h]
                ss_ref[ci, hl * 128:(hl + 1) * 128, :] = s
                y, sn = _dn_chunk(q, k, v, _lane_col(ba, h), _lane_col(ba, nheads + h), z, s,
                                  _head_pick(alv, h), _head_pick(dtv, h), ngv)
                y_ref[pl.ds(r0, CHUNK), hl * 128:(hl + 1) * 128] = y
                s_scr[h] = sn
            return carry

        lax.fori_loop(0, nc, chunk, 0)

    zc, bc = z_col // (hb * 128), ba_col // 128
    return pl.pallas_call(
        body, name="dn_core_fwd", grid=(t // tm, nheads // hb),
        in_specs=[pl.BlockSpec((tm, hb * 384), lambda i, h: (i, h)),
                  pl.BlockSpec((tm, hb * 128), lambda i, h: (i, zc + h)),
                  pl.BlockSpec((tm, 128), lambda i, h: (i, bc)),
                  pl.BlockSpec((hp, 128), lambda i, h: (0, 0)), pl.BlockSpec((hp, 128), lambda i, h: (0, 0)),
                  pl.BlockSpec((1, 128), lambda i, h: (0, 0))],
        out_specs=[pl.BlockSpec((tm, hb * 128), lambda i, h: (i, h)),
                   pl.BlockSpec((nc, hb * 128, 128), lambda i, h: (i, h, 0))],
        out_shape=[jax.ShapeDtypeStruct((t, nheads * 128), F32),
                   jax.ShapeDtypeStruct((t // CHUNK, nheads * 128, 128), F32)],
        scratch_shapes=[pltpu.VMEM((nheads, LANES, LANES), F32)],
        compiler_params=_cp(("arbitrary", "arbitrary")),
    )(qkv, p, p, alog_b, dtb_b, ng)


def dn_core_bwd(qkv, p, z_col, ba_col, alog_b, dtb_b, ng, ss, dy, nheads, hb_heads):
    t = qkv.shape[0]
    tm = _tile(t, 512, CHUNK)
    nc = tm // CHUNK
    nblk = t // tm
    hb = hb_heads
    hp = alog_b.shape[0]

    def body(qkv_ref, z_ref, ba_ref, al_ref, dt_ref, ng_ref, ss_ref, dy_ref,
             dqkv_ref, dz_ref, dba_ref, dng_ref, dal_ref, ddt_ref, ds_scr):
        i, hblk = pl.program_id(0), pl.program_id(1)

        @pl.when((i == 0) & (hblk == 0))
        def _():
            dng_ref[...] = jnp.zeros_like(dng_ref)
            dal_ref[...] = jnp.zeros_like(dal_ref)
            ddt_ref[...] = jnp.zeros_like(ddt_ref)

        @pl.when(hblk == 0)
        def _():
            dba_ref[...] = jnp.zeros_like(dba_ref)

        for hl in range(hb):
            @pl.when(i == 0)
            def _():
                ds_scr[hblk * hb + hl] = jnp.zeros((LANES, LANES), F32)
        alv, dtv, ngv = al_ref[...], dt_ref[...], ng_ref[...]
        lane = lax.broadcasted_iota(jnp.int32, (CHUNK, LANES), 1)
        rowp = lax.broadcasted_iota(jnp.int32, (hp, LANES), 0)

        def chunk(cj, carry):
            ci = nc - 1 - cj
            r0 = pl.multiple_of(ci * CHUNK, CHUNK)
            ba = ba_ref[pl.ds(r0, CHUNK), :]
            for hl in range(hb):
                h = hblk * hb + hl
                q = qkv_ref[pl.ds(r0, CHUNK), hl * 384:hl * 384 + 128]
                k = qkv_ref[pl.ds(r0, CHUNK), hl * 384 + 128:hl * 384 + 256]
                v = qkv_ref[pl.ds(r0, CHUNK), hl * 384 + 256:hl * 384 + 384]
                z = z_ref[pl.ds(r0, CHUNK), hl * 128:(hl + 1) * 128]
                s = ss_ref[ci, hl * 128:(hl + 1) * 128, :]
                _, vj = jax.vjp(_dn_chunk, q, k, v, _lane_col(ba, h), _lane_col(ba, nheads + h), z, s,
                                _head_pick(alv, h), _head_pick(dtv, h), ngv)
                dyv = dy_ref[pl.ds(r0, CHUNK), hl * 128:(hl + 1) * 128]
                dq, dk, dv, dbc, dac, dz, dsp, dal, ddt, dng = vj((dyv, ds_scr[h]))
                dqkv_ref[pl.ds(r0, CHUNK), hl * 384:hl * 384 + 128] = dq
                dqkv_ref[pl.ds(r0, CHUNK), hl * 384 + 128:hl * 384 + 256] = dk
                dqkv_ref[pl.ds(r0, CHUNK), hl * 384 + 256:hl * 384 + 384] = dv
                dz_ref[pl.ds(r0, CHUNK), hl * 128:(hl + 1) * 128] = dz
                dba_ref[pl.ds(r0, CHUNK), :] += (jnp.where(lane == h, dbc, 0.0) + jnp.where(lane == nheads + h, dac, 0.0))
                ds_scr[h] = dsp
                dng_ref[...] += dng
                dal_ref[...] += jnp.where(rowp == h, dal, 0.0)
                ddt_ref[...] += jnp.where(rowp == h, ddt, 0.0)
            return carry

        lax.fori_loop(0, nc, chunk, 0)

    zc, bc = z_col // (hb * 128), ba_col // 128
    rv = lambda i: nblk - 1 - i
    const = lambda i, h: (0, 0)
    return pl.pallas_call(
        body, name="dn_core_bwd", grid=(nblk, nheads // hb),
        in_specs=[pl.BlockSpec((tm, hb * 384), lambda i, h: (rv(i), h)),
                  pl.BlockSpec((tm, hb * 128), lambda i, h: (rv(i), zc + h)),
                  pl.BlockSpec((tm, 128), lambda i, h: (rv(i), bc)),
                  pl.BlockSpec((hp, 128), const), pl.BlockSpec((hp, 128), const), pl.BlockSpec((1, 128), const),
                  pl.BlockSpec((nc, hb * 128, 128), lambda i, h: (rv(i), h, 0)),
                  pl.BlockSpec((tm, hb * 128), lambda i, h: (rv(i), h))],
        out_specs=[pl.BlockSpec((tm, hb * 384), lambda i, h: (rv(i), h)),
                   pl.BlockSpec((tm, hb * 128), lambda i, h: (rv(i), h)),
                   pl.BlockSpec((tm, 128), lambda i, h: (rv(i), 0)),
                   pl.BlockSpec((1, 128), const), pl.BlockSpec((hp, 128), const), pl.BlockSpec((hp, 128), const)],
        out_shape=[jax.ShapeDtypeStruct((t, nheads * 384), F32), jax.ShapeDtypeStruct((t, nheads * 128), F32),
                   jax.ShapeDtypeStruct((t, 128), F32), jax.ShapeDtypeStruct((1, 128), F32),
                   jax.ShapeDtypeStruct((hp, 128), F32), jax.ShapeDtypeStruct((hp, 128), F32)],
        scratch_shapes=[pltpu.VMEM((nheads, LANES, LANES), F32)],
        compiler_params=_cp(("arbitrary", "arbitrary")),
    )(qkv, p, p, alog_b, dtb_b, ng, ss, dy)


def sg_fwd(p, col, sg, lg, lb, w, bias_b):
    t = p.shape[0]
    ng_ = sg // LANES
    tm = _tile(t, 256, LANES)
    cb = col // sg

    def body(u_ref, v_ref, g_ref, lg_ref, lb_ref, w_ref, b_ref, y_ref):
        for n in range(tm // LANES):
            rs = slice(n * LANES, (n + 1) * LANES)
            for gi in range(ng_):
                sl = slice(gi * LANES, (gi + 1) * LANES)
                y_ref[rs, sl] = _sg_block(u_ref[rs, sl], v_ref[rs, sl], g_ref[rs, sl], lg_ref[:, sl], lb_ref[:, sl],
                                          w_ref[gi], b_ref[gi])

    vec = pl.BlockSpec((1, sg), lambda i: (0, 0))
    full = pl.BlockSpec((ng_, LANES, LANES), lambda i: (0, 0, 0))
    return pl.pallas_call(
        body, name="sg_fwd", grid=(t // tm,),
        in_specs=[pl.BlockSpec((tm, sg), lambda i: (i, cb)), pl.BlockSpec((tm, sg), lambda i: (i, cb + 1)),
                  pl.BlockSpec((tm, sg), lambda i: (i, cb + 2)), vec, vec, full, full],
        out_specs=pl.BlockSpec((tm, sg), lambda i: (i, 0)),
        out_shape=jax.ShapeDtypeStruct((t, sg), F32),
        compiler_params=_cp(("parallel",)),
    )(p, p, p, lg, lb, w, bias_b)


def sg_bwd(p, col, sg, lg, lb, w, bias_b, dy):
    t = p.shape[0]
    ng_ = sg // LANES
    tm = _tile(t, 256, LANES)
    cb = col // sg

    def body(u_ref, v_ref, g_ref, lg_ref, lb_ref, w_ref, b_ref, dy_ref, d_ref, dlg_ref, dlb_ref, dw_ref, db_ref):
        @pl.when(pl.program_id(0) == 0)
        def _():
            for r in (dlg_ref, dlb_ref, dw_ref, db_ref):
                r[...] = jnp.zeros_like(r)

        for n in range(tm // LANES):
            rs = slice(n * LANES, (n + 1) * LANES)
            for gi in range(ng_):
                sl = slice(gi * LANES, (gi + 1) * LANES)
                _, vj = jax.vjp(_sg_block, u_ref[rs, sl], v_ref[rs, sl], g_ref[rs, sl], lg_ref[:, sl], lb_ref[:, sl],
                                w_ref[gi], b_ref[gi])
                du, dv, dg, dlg, dlb, dw, db = vj(dy_ref[rs, sl])
                d_ref[rs, gi * LANES:(gi + 1) * LANES] = du.astype(BF16)
                d_ref[rs, sg + gi * LANES:sg + (gi + 1) * LANES] = dv.astype(BF16)
                d_ref[rs, 2 * sg + gi * LANES:2 * sg + (gi + 1) * LANES] = dg.astype(BF16)
                dlg_ref[:, sl] += dlg
                dlb_ref[:, sl] += dlb
                dw_ref[gi] += dw
                db_ref[gi] += jnp.broadcast_to(jnp.sum(db, axis=1, keepdims=True), (LANES, LANES))

    vec = pl.BlockSpec((1, sg), lambda i: (0, 0))
    full = pl.BlockSpec((ng_, LANES, LANES), lambda i: (0, 0, 0))
    return pl.pallas_call(
        body, name="sg_bwd", grid=(t // tm,),
        in_specs=[pl.BlockSpec((tm, sg), lambda i: (i, cb)), pl.BlockSpec((tm, sg), lambda i: (i, cb + 1)),
                  pl.BlockSpec((tm, sg), lambda i: (i, cb + 2)), vec, vec, full, full,
                  pl.BlockSpec((tm, sg), lambda i: (i, 0))],
        out_specs=[pl.BlockSpec((tm, 3 * sg), lambda i: (i, 0)), vec, vec, full, full],
        out_shape=[jax.ShapeDtypeStruct((t, 3 * sg), BF16), jax.ShapeDtypeStruct((1, sg), F32),
                   jax.ShapeDtypeStruct((1, sg), F32), jax.ShapeDtypeStruct((ng_, LANES, LANES), F32),
                   jax.ShapeDtypeStruct((ng_, LANES, LANES), F32)],
        compiler_params=_cp(("arbitrary",)),
    )(p, p, p, lg, lb, w, bias_b, dy)


def out_proj(x, y_dn, y_sg, y_cv, wo, gate):
    t, d = x.shape
    dn, sg, cv = y_dn.shape[1], y_sg.shape[1], y_cv.shape[1]
    dmix = dn + sg + cv
    tm = _tile(t, 256, LANES)

    def body(x_ref, a_ref, b_ref, c_ref, w_ref, g_ref, xn_ref, y_ref, yt_ref):
        a, b, c = a_ref[...], b_ref[...], c_ref[...]
        y = (jnp.dot(a.astype(BF16), w_ref[0:dn, :], preferred_element_type=F32)
             + jnp.dot(b.astype(BF16), w_ref[dn:dn + sg, :], preferred_element_type=F32)
             + jnp.dot(c.astype(BF16), w_ref[dn + sg:, :], preferred_element_type=F32))
        y_ref[...] = y
        xn_ref[...] = x_ref[...] + g_ref[...] * y
        yt_ref[0:dn, :] = a.T.astype(BF16)
        yt_ref[dn:dn + sg, :] = b.T.astype(BF16)
        yt_ref[dn + sg:, :] = c.T.astype(BF16)

    row = lambda w_: pl.BlockSpec((tm, w_), lambda i: (i, 0))
    return pl.pallas_call(
        body, name="out_proj", grid=(t // tm,),
        in_specs=[row(d), row(dn), row(sg), row(cv), pl.BlockSpec((dmix, d), lambda i: (0, 0)),
                  pl.BlockSpec((1, d), lambda i: (0, 0))],
        out_specs=[row(d), row(d), pl.BlockSpec((dmix, tm), lambda i: (0, i))],
        out_shape=[jax.ShapeDtypeStruct((t, d), F32), jax.ShapeDtypeStruct((t, d), F32),
                   jax.ShapeDtypeStruct((dmix, t), BF16)],
        compiler_params=_cp(("parallel",)),
    )(x, y_dn, y_sg, y_cv, wo, gate)


def out_proj_bwd(dxn, y, gate, wo, dn, sg, cv):
    t, d = dxn.shape
    dmix = dn + sg + cv
    tm = _tile(t, 256, LANES)

    def body(dx_ref, y_ref, g_ref, w_ref, da_ref, db_ref, dc_ref, dyb_ref, dg_ref):
        @pl.when(pl.program_id(0) == 0)
        def _():
            dg_ref[...] = jnp.zeros_like(dg_ref)
        dx = dx_ref[...]
        dg_ref[...] += jnp.sum(dx * y_ref[...], axis=0, keepdims=True)
        dyb = (dx * g_ref[...]).astype(BF16)
        dyb_ref[...] = dyb
        dcat = lax.dot_general(dyb, w_ref[...], (((1,), (1,)), ((), ())), preferred_element_type=F32)
        da_ref[...] = dcat[:, 0:dn]
        db_ref[...] = dcat[:, dn:dn + sg]
        dc_ref[...] = dcat[:, dn + sg:]

    row = lambda w_: pl.BlockSpec((tm, w_), lambda i: (i, 0))
    vec = pl.BlockSpec((1, d), lambda i: (0, 0))
    return pl.pallas_call(
        body, name="out_proj_bwd", grid=(t // tm,),
        in_specs=[row(d), row(d), vec, pl.BlockSpec((dmix, d), lambda i: (0, 0))],
        out_specs=[row(dn), row(sg), row(cv), row(d), vec],
        out_shape=[jax.ShapeDtypeStruct((t, dn), F32), jax.ShapeDtypeStruct((t, sg), F32),
                   jax.ShapeDtypeStruct((t, cv), F32), jax.ShapeDtypeStruct((t, d), BF16),
                   jax.ShapeDtypeStruct((1, d), F32)],
        compiler_params=_cp(("arbitrary",)),
    )(dxn, y, gate, wo)


def matmul_acc(name, at, b):
    m, t = at.shape
    n = b.shape[1]
    tn, tk = _tile(n, 512, LANES), _tile(t, 512, LANES)

    def body(a_ref, b_ref, o_ref):
        @pl.when(pl.program_id(1) == 0)
        def _():
            o_ref[...] = jnp.zeros_like(o_ref)
        o_ref[...] += jnp.dot(a_ref[...], b_ref[...], preferred_element_type=F32)

    return pl.pallas_call(
        body, name=name, grid=(n // tn, t // tk),
        in_specs=[pl.BlockSpec((m, tk), lambda j, k: (0, k)), pl.BlockSpec((tk, tn), lambda j, k: (k, j))],
        out_specs=pl.BlockSpec((m, tn), lambda j, k: (0, j)),
        out_shape=jax.ShapeDtypeStruct((m, n), F32),
        compiler_params=_cp(("parallel", "arbitrary")),
    )(at, b)


def in_proj_bwd(dp, wp, x, dxn, shift, scale, ng):
    t, d = x.shape
    npc = wp.shape[1]
    tm, tk = _tile(t, 512, LANES), _tile(npc, 384, LANES)
    nk = npc // tk

    def body(dp_ref, w_ref, x_ref, dxn_ref, sh_ref, sc_ref, g_ref, dx_ref, dg_ref, dsc_ref, dsh_ref, acc):
        i, kk = pl.program_id(0), pl.program_id(1)

        @pl.when((i == 0) & (kk == 0))
        def _():
            for r in (dg_ref, dsc_ref, dsh_ref):
                r[...] = jnp.zeros_like(r)

        @pl.when(kk == 0)
        def _():
            acc[...] = jnp.zeros_like(acc)
        acc[...] += lax.dot_general(dp_ref[...], w_ref[...], (((1,), (1,)), ((), ())), preferred_element_type=F32)

        @pl.when(kk == nk - 1)
        def _():
            _, vj = jax.vjp(_modnorm, x_ref[...], g_ref[...], sc_ref[...], sh_ref[...])
            dx, dg, dsc, dsh = vj(acc[...])
            dx_ref[...] = dxn_ref[...] + dx
            dg_ref[...] += dg
            dsc_ref[...] += dsc
            dsh_ref[...] += dsh

    vec = pl.BlockSpec((1, d), lambda i, k: (0, 0))
    row = pl.BlockSpec((tm, d), lambda i, k: (i, 0))
    return pl.pallas_call(
        body, name="in_proj_bwd", grid=(t // tm, nk),
        in_specs=[pl.BlockSpec((tm, tk), lambda i, k: (i, k)), pl.BlockSpec((d, tk), lambda i, k: (0, k)),
                  row, row, vec, vec, vec],
        out_specs=[row, vec, vec, vec],
        out_shape=[jax.ShapeDtypeStruct((t, d), F32)] + [jax.ShapeDtypeStruct((1, d), F32)] * 3,
        scratch_shapes=[pltpu.VMEM((tm, d), F32)],
        compiler_params=_cp(("arbitrary", "arbitrary")),
    )(dp, wp, x, dxn, shift, scale, ng)


def loss_head(x, tgt, fg):
    t, d = x.shape
    tm = _tile(t, 512, 8)

    def body(x_ref, t_ref, g_ref, l_ref, dx_ref, dg_ref):
        @pl.when(pl.program_id(0) == 0)
        def _():
            l_ref[...] = jnp.zeros_like(l_ref)
            dg_ref[...] = jnp.zeros_like(dg_ref)
        y, vj = jax.vjp(_rmsnorm, x_ref[...], g_ref[...])
        err = y - t_ref[...]
        part = 0.5 * jnp.sum(jnp.sum(err * err, axis=1, keepdims=True), axis=0, keepdims=True) / d
        l_ref[...] += jnp.broadcast_to(part, l_ref.shape)
        dx, dg = vj(err / d)
        dx_ref[...] = dx
        dg_ref[...] += dg

    row = pl.BlockSpec((tm, d), lambda i: (i, 0))
    vec = pl.BlockSpec((1, d), lambda i: (0, 0))
    return pl.pallas_call(
        body, name="loss_head", grid=(t // tm,), in_specs=[row, row, vec],
        out_specs=[pl.BlockSpec((1, LANES), lambda i: (0, 0)), row, vec],
        out_shape=[jax.ShapeDtypeStruct((1, LANES), F32), jax.ShapeDtypeStruct((t, d), F32),
                   jax.ShapeDtypeStruct((1, d), F32)],
        compiler_params=_cp(("arbitrary",)),
    )(x, tgt, fg)


def adamw(name, w, g, m, v):
    r, c = w.shape
    tr = _tile(r, 256, 8) if r % 8 == 0 else r

    def body(w_ref, g_ref, m_ref, v_ref, d_ref, mo_ref, vo_ref):
        d_ref[...], mo_ref[...], vo_ref[...] = _adamw_math(w_ref[...], g_ref[...], m_ref[...], v_ref[...])

    blk = pl.BlockSpec((tr, c), lambda i: (i, 0))
    return pl.pallas_call(
        body, name=name, grid=(r // tr,), in_specs=[blk] * 4, out_specs=[blk] * 3,
        out_shape=[jax.ShapeDtypeStruct((r, c), F32)] * 3, compiler_params=_cp(("parallel",)),
    )(w, g, m, v)


def ada_fwd(c_all, w_ada, b_loc):
    nl, d, cols = w_ada.shape
    nb = c_all.shape[0]
    tn = _tile(cols, 512, LANES)

    def body(c_ref, w_ref, b_ref, o_ref):
        ca = _silu(c_ref[...]).astype(BF16)
        o_ref[0] = jnp.dot(ca, w_ref[0].astype(BF16), preferred_element_type=F32) + b_ref[0]

    return pl.pallas_call(
        body, name="ada_fwd", grid=(nl, cols // tn),
        in_specs=[pl.BlockSpec((nb, d), lambda l, j: (0, 0)), pl.BlockSpec((1, d, tn), lambda l, j: (l, 0, j)),
                  pl.BlockSpec((1, 1, tn), lambda l, j: (l, 0, j))],
        out_specs=pl.BlockSpec((1, nb, tn), lambda l, j: (l, 0, j)),
        out_shape=jax.ShapeDtypeStruct((nl, nb, cols), F32),
        compiler_params=_cp(("parallel", "parallel")),
    )(c_all, w_ada, b_loc)


def ada_bwd(c_all_t, dmod_loc, w, m, v):
    nl, d, cols = w.shape
    nb = c_all_t.shape[1]
    tr = _tile(d, 256, 8)

    def body(c_ref, dm_ref, w_ref, m_ref, v_ref, g_ref, d_ref, mo_ref, vo_ref):
        ca = _silu(c_ref[...])
        dm = dm_ref[0]
        g = _lane_col(ca, 0) * dm[0:1, :]
        for b in range(1, nb):
            g = g + _lane_col(ca, b) * dm[b:b + 1, :]
        g_ref[0] = g
        d_ref[0], mo_ref[0], vo_ref[0] = _adamw_math(w_ref[0], g, m_ref[0], v_ref[0])

    blk = pl.BlockSpec((1, tr, cols), lambda l, i: (l, i, 0))
    return pl.pallas_call(
        body, name="ada_bwd", grid=(nl, d // tr),
        in_specs=[pl.BlockSpec((tr, nb), lambda l, i: (i, 0)), pl.BlockSpec((1, nb, cols), lambda l, i: (l, 0, 0)),
                  blk, blk, blk],
        out_specs=[blk] * 4, out_shape=[jax.ShapeDtypeStruct((nl, d, cols), F32)] * 4,
        compiler_params=_cp(("parallel", "parallel")),
    )(c_all_t, dmod_loc, w, m, v)


def sum8(g):
    _, r, c = g.shape
    tr = _tile(r, 256, 8)

    def body(g_ref, o_ref):
        acc = g_ref[0]
        for k in range(1, NDEV):
            acc = acc + g_ref[k]
        o_ref[...] = acc

    return pl.pallas_call(
        body, name="sum8", grid=(r // tr,), in_specs=[pl.BlockSpec((NDEV, tr, c), lambda i: (0, i, 0))],
        out_specs=pl.BlockSpec((tr, c), lambda i: (i, 0)), out_shape=jax.ShapeDtypeStruct((r, c), F32),
        compiler_params=_cp(("parallel",)),
    )(g)


def pair_sum_bf16(ids, g, r1):
    _, ns, r, c = g.shape
    tr = _tile(r, 256, 8)

    def body(ids_ref, g_ref, r_ref, o_ref):
        o_ref[...] = (g_ref[0] + r_ref[...]).astype(BF16)

    return pl.pallas_call(
        body, name="pair_sum_bf16",
        grid_spec=pltpu.PrefetchScalarGridSpec(
            num_scalar_prefetch=1, grid=(ns, r // tr),
            in_specs=[pl.BlockSpec((1, 1, tr, c), lambda s, i, ids: (ids[0], s, i, 0)),
                      pl.BlockSpec((1, tr, c), lambda s, i, ids: (s, i, 0))],
            out_specs=pl.BlockSpec((1, tr, c), lambda s, i, ids: (s, i, 0))),
        out_shape=jax.ShapeDtypeStruct((ns, r, c), BF16), compiler_params=_cp(("parallel", "parallel")),
    )(ids, g, r1)


def chip_sum(ids, g, r1, r2):
    _, _, r, c = g.shape
    tr = _tile(r, 256, 8)

    def body(ids_ref, g_ref, r1_ref, r2_ref, o_ref):
        acc = g_ref[0, 0] + r1_ref[0]
        for k in range(NCHIP - 1):
            acc = acc + r2_ref[k].astype(F32)
        o_ref[...] = acc

    return pl.pallas_call(
        body, name="chip_sum",
        grid_spec=pltpu.PrefetchScalarGridSpec(
            num_scalar_prefetch=1, grid=(r // tr,),
            in_specs=[pl.BlockSpec((1, 1, tr, c), lambda i, ids: (ids[0], ids[1], i, 0)),
                      pl.BlockSpec((1, tr, c), lambda i, ids: (ids[1], i, 0)),
                      pl.BlockSpec((NCHIP - 1, tr, c), lambda i, ids: (0, i, 0))],
            out_specs=pl.BlockSpec((tr, c), lambda i, ids: (i, 0))),
        out_shape=jax.ShapeDtypeStruct((r, c), F32), compiler_params=_cp(("parallel",)),
    )(ids, g, r1, r2)


def _me():
    return lax.axis_index("x"), lax.axis_index("y"), lax.axis_index("c")


_FLIPS = ((1, 0), (0, 1), (1, 1))


def all_gather8(v):
    m_per, n = v.shape

    def body(x_ref, out_ref, send_sems, recv_sems, local_sem):
        x, y, c = _me()
        me, sibling = (x, y, c), (x, y, 1 - c)
        chips = [(x ^ fx, y ^ fy) for fx, fy in _FLIPS]

        def rows(px, py, pc):
            return out_ref.at[pl.ds((4 * px + 2 * py + pc) * m_per, m_per), :]

        def copy(k, block, to, src=None):
            return pltpu.make_async_remote_copy(
                src_ref=rows(*block) if src is None else src, dst_ref=rows(*block),
                send_sem=send_sems.at[k], recv_sem=recv_sems.at[k], device_id=to, device_id_type=MESH)

        mine = pltpu.make_async_copy(x_ref, rows(*me), local_sem)
        mine.start()
        first = [copy(0, me, sibling, src=x_ref)]
        first += [copy(1 + j, me, (*chip, c), src=x_ref) for j, chip in enumerate(chips)]
        for cp in first:
            cp.start()
        passed = [copy(4 + j, (*chip, c), sibling) for j, chip in enumerate(chips)]
        for j, chip in enumerate(chips):
            copy(1 + j, (*chip, c), me).wait_recv()
            passed[j].start()
        copy(0, sibling, me).wait_recv()
        for j, chip in enumerate(chips):
            copy(4 + j, (*chip, 1 - c), me).wait_recv()
        for cp in first + passed:
            cp.wait_send()
        mine.wait()

    return pl.pallas_call(
        body, name="all_gather8", out_shape=jax.ShapeDtypeStruct((NDEV * m_per, n), v.dtype),
        in_specs=[pl.BlockSpec(memory_space=pltpu.VMEM)], out_specs=pl.BlockSpec(memory_space=pltpu.VMEM),
        scratch_shapes=[pltpu.SemaphoreType.DMA((7,)), pltpu.SemaphoreType.DMA((7,)), pltpu.SemaphoreType.DMA],
        compiler_params=pltpu.CompilerParams(vmem_limit_bytes=VMEM_LIMIT),
    )(v)


def gather_weights(ws):
    na = len(ws)

    def body(*refs):
        srcs, outs = refs[:na], refs[na:2 * na]
        send_sems, recv_sems, local_sems = refs[2 * na:]
        x, y, c = _me()
        chip = 2 * x + y
        sibling = (x, y, 1 - c)
        peers = [(x ^ fx, y ^ fy) for fx, fy in _FLIPS]
        local = [pltpu.make_async_copy(srcs[a], outs[a].at[chip], local_sems.at[a]) for a in range(na)]
        for cp in local:
            cp.start()

        def ici(a, j, half_of, to):
            return pltpu.make_async_remote_copy(
                src_ref=srcs[a].at[c], dst_ref=outs[a].at[half_of, c], send_sem=send_sems.at[a * 6 + j],
                recv_sem=recv_sems.at[a * 6 + j], device_id=to, device_id_type=MESH)

        def d2d(a, j, slot, half):
            return pltpu.make_async_remote_copy(
                src_ref=outs[a].at[slot, half], dst_ref=outs[a].at[slot, half], send_sem=send_sems.at[a * 6 + 3 + j],
                recv_sem=recv_sems.at[a * 6 + 3 + j], device_id=sibling, device_id_type=MESH)

        sends = []
        for a in range(na):
            for j, (px, py) in enumerate(peers):
                cp = ici(a, j, chip, (px, py, c))
                cp.start()
                sends.append(cp)
        for a in range(na):
            for j, (px, py) in enumerate(peers):
                ici(a, j, 2 * px + py, (px, py, c)).wait_recv()
                cp = d2d(a, j, 2 * px + py, c)
                cp.start()
                sends.append(cp)
        for a in range(na):
            for j, (px, py) in enumerate(peers):
                d2d(a, j, 2 * px + py, 1 - c).wait_recv()
        for cp in sends:
            cp.wait_send()
        for cp in local:
            cp.wait()

    return pl.pallas_call(
        body, name="gather_weights",
        out_shape=[jax.ShapeDtypeStruct((NCHIP,) + w.shape, w.dtype) for w in ws],
        in_specs=[ANY] * na, out_specs=[ANY] * na,
        scratch_shapes=[pltpu.SemaphoreType.DMA((6 * na,)), pltpu.SemaphoreType.DMA((6 * na,)),
                        pltpu.SemaphoreType.DMA((na,))],
    )(*ws)


def swap_halves(gs):
    na = len(gs)

    def body(*refs):
        srcs, outs = refs[:na], refs[na:2 * na]
        send_sems, recv_sems = refs[2 * na:]
        x, y, c = _me()
        cps = [pltpu.make_async_remote_copy(
            src_ref=srcs[a].at[1 - c], dst_ref=outs[a], send_sem=send_sems.at[a], recv_sem=recv_sems.at[a],
            device_id=(x, y, 1 - c), device_id_type=MESH) for a in range(na)]
        for cp in cps:
            cp.start()
        for cp in cps:
            cp.wait()

    return pl.pallas_call(
        body, name="swap_halves", out_shape=[jax.ShapeDtypeStruct(g.shape[1:], g.dtype) for g in gs],
        in_specs=[ANY] * na, out_specs=[ANY] * na,
        scratch_shapes=[pltpu.SemaphoreType.DMA((na,)), pltpu.SemaphoreType.DMA((na,))],
    )(*gs)


def chip_exchange(ps):
    na = len(ps)

    def body(*refs):
        srcs, outs = refs[:na], refs[na:2 * na]
        send_sems, recv_sems = refs[2 * na:]
        x, y, c = _me()
        cps = []
        for a in range(na):
            for j, (fx, fy) in enumerate(_FLIPS):
                px, py = x ^ fx, y ^ fy
                cps.append(pltpu.make_async_remote_copy(
                    src_ref=srcs[a].at[2 * px + py], dst_ref=outs[a].at[j], send_sem=send_sems.at[a * 3 + j],
                    recv_sem=recv_sems.at[a * 3 + j], device_id=(px, py, c), device_id_type=MESH))
        for cp in cps:
            cp.start()
        for cp in cps:
            cp.wait()

    return pl.pallas_call(
        body, name="chip_exchange",
        out_shape=[jax.ShapeDtypeStruct((NCHIP - 1,) + p_.shape[1:], p_.dtype) for p_ in ps],
        in_specs=[ANY] * na, out_specs=[ANY] * na,
        scratch_shapes=[pltpu.SemaphoreType.DMA((3 * na,)), pltpu.SemaphoreType.DMA((3 * na,))],
    )(*ps)


def join_halves(hs):
    na = len(hs)

    def body(*refs):
        srcs, outs = refs[:na], refs[na:2 * na]
        send_sems, recv_sems, local_sems = refs[2 * na:]
        x, y, c = _me()
        loc = [pltpu.make_async_copy(srcs[a], outs[a].at[c], local_sems.at[a]) for a in range(na)]
        cps = [pltpu.make_async_remote_copy(
            src_ref=srcs[a], dst_ref=outs[a].at[c], send_sem=send_sems.at[a], recv_sem=recv_sems.at[a],
            device_id=(x, y, 1 - c), device_id_type=MESH) for a in range(na)]
        for cp in loc + cps:
            cp.start()
        for cp in cps:
            cp.wait()
        for cp in loc:
            cp.wait()

    return pl.pallas_call(
        body, name="join_halves", out_shape=[jax.ShapeDtypeStruct((2,) + h.shape, h.dtype) for h in hs],
        in_specs=[ANY] * na, out_specs=[ANY] * na,
        scratch_shapes=[pltpu.SemaphoreType.DMA((na,)), pltpu.SemaphoreType.DMA((na,)), pltpu.SemaphoreType.DMA((na,))],
    )(*hs)


class _Cfg:
    def __init__(self, x, a_log, sg_w, cv_ln_g, cv_w, conv_qkv):
        self.t, self.d = x.shape[1], x.shape[2]
        self.nl, self.h = a_log.shape
        self.dn = self.h * LANES
        self.g = sg_w.shape[1]
        self.sg = self.g * LANES
        self.cv = cv_ln_g.shape[1]
        self.kc = cv_w.shape[1]
        self.k4 = conv_qkv.shape[1]
        self.o_z = 3 * self.dn
        self.o_sg = 4 * self.dn
        self.o_cv = self.o_sg + 3 * self.sg
        self.o_ba = self.o_cv + 3 * self.cv
        self.npc = self.o_ba + LANES
        self.d_in = self.o_ba + 2 * self.h
        self.dmix = self.dn + self.sg + self.cv
        self.hb = 2 if self.h % 2 == 0 else 1


def _perm_cols(cfg, w):
    dn, h = cfg.dn, cfg.h
    lead = w.shape[:-1]
    qkv = w[..., :3 * dn].reshape(lead + (3, h, LANES))
    qkv = jnp.moveaxis(qkv, -3, -2).reshape(lead + (3 * dn,))
    ba = w[..., 4 * dn:4 * dn + 2 * h]
    pad = jnp.zeros(lead + (LANES - 2 * h,), w.dtype)
    return jnp.concatenate([qkv, w[..., 3 * dn:4 * dn], w[..., 4 * dn + 2 * h:], ba, pad], axis=-1)


def _unperm_cols(cfg, w):
    dn, h = cfg.dn, cfg.h
    lead = w.shape[:-1]
    qkv = w[..., :3 * dn].reshape(lead + (h, 3, LANES))
    qkv = jnp.moveaxis(qkv, -3, -2).reshape(lead + (3 * dn,))
    return jnp.concatenate([qkv, w[..., 3 * dn:4 * dn], w[..., cfg.o_ba:cfg.o_ba + 2 * h], w[..., 4 * dn:cfg.o_ba]], axis=-1)


def _layer_fwd(cfg, x, mod, lw):
    shift, scale, gate = mod
    p, ht = in_proj(x, shift, scale, lw["norm_g"], lw["wp"])
    qk_post = [_qk_post, _qk_post, _v_post]
    qkv = conv_fwd("dn_pre_fwd", cfg.k4, HALO4, lambda a: a, [(p, 0)], lw["conv_qkv"], qk_post, [], [],
                   3 * cfg.dn, 3 * LANES)
    y_dn, ss = dn_core_fwd(qkv, p, cfg.o_z, cfg.o_ba, lw["alog_b"], lw["dtb_b"], lw["dn_norm_g"], cfg.h, cfg.hb)
    y_sg = sg_fwd(p, cfg.o_sg, cfg.sg, lw["sg_ln_g"], lw["sg_ln_b"], lw["sg_w"], lw["sg_bias_b"])
    cv_post = [_cv_post] * (cfg.cv // LANES)
    y_cv = conv_fwd("cv_fwd", cfg.kc, HALO31, _glu, [(p, cfg.o_cv), (p, cfg.o_cv + cfg.cv)], lw["cv_w"], cv_post,
                    [(p, cfg.o_cv + 2 * cfg.cv)], [lw["cv_b"], lw["cv_ln_g"], lw["cv_ln_b"]], cfg.cv, cfg.cv)
    xn, y, yt = out_proj(x, y_dn, y_sg, y_cv, lw["wo"], gate)
    return xn, dict(x=x, p=p, ht=ht, qkv=qkv, ss=ss, y=y, yt=yt)


def _layer_bwd(cfg, dxn, mod, lw, sv):
    shift, scale, gate = mod
    p = sv["p"]
    d_dn, d_sg, d_cv, dyb, dgate = out_proj_bwd(dxn, sv["y"], gate, lw["wo"], cfg.dn, cfg.sg, cfg.cv)
    g_wo = matmul_acc("w_out_grad", sv["yt"], dyb)
    cv_post = [_cv_post] * (cfg.cv // LANES)
    dcv, g_cvw, (g_cvb, g_cvlg, g_cvlb) = conv_bwd(
        "cv_bwd", cfg.kc, HALO31, _glu, [(p, cfg.o_cv), (p, cfg.o_cv + cfg.cv)], lw["cv_w"], cv_post,
        [(p, cfg.o_cv + 2 * cfg.cv)], [lw["cv_b"], lw["cv_ln_g"], lw["cv_ln_b"]], d_cv, cfg.cv, cfg.cv, tm_pref=256)
    dsg, g_sglg, g_sglb, g_sgw, g_sgb = sg_bwd(p, cfg.o_sg, cfg.sg, lw["sg_ln_g"], lw["sg_ln_b"], lw["sg_w"],
                                               lw["sg_bias_b"], d_sg)
    dqkv, dz, dba, g_dng, g_al, g_dt = dn_core_bwd(sv["qkv"], p, cfg.o_z, cfg.o_ba, lw["alog_b"], lw["dtb_b"],
                                                   lw["dn_norm_g"], sv["ss"], d_dn, cfg.h, cfg.hb)
    qk_post = [_qk_post, _qk_post, _v_post]
    dqkv_pre, g_cq, _ = conv_bwd("dn_pre_bwd", cfg.k4, HALO4, lambda a: a, [(p, 0)], lw["conv_qkv"], qk_post, [], [],
                                 dqkv, 3 * cfg.dn, 3 * LANES)
    dp = jnp.concatenate([dqkv_pre, dz.astype(BF16), dsg, dcv, dba.astype(BF16)], axis=1)
    g_wp = matmul_acc("w_in_grad", sv["ht"], dp)
    dx, g_ng, dscale, dshift = in_proj_bwd(dp, lw["wp"], sv["x"], dxn, shift, scale, lw["norm_g"])
    grads = dict(norm_g=g_ng, conv_qkv=g_cq, a_log=g_al[:cfg.h, 0], dt_bias=g_dt[:cfg.h, 0], dn_norm_g=g_dng,
                 sg_ln_g=g_sglg, sg_ln_b=g_sglb, sg_w=g_sgw, sg_b=g_sgb[:, :, 0], cv_w=g_cvw, cv_b=g_cvb,
                 cv_ln_g=g_cvlg, cv_ln_b=g_cvlb, wp=g_wp, wo=g_wo)
    return dx, grads, (dshift, dscale, dgate)


def _local_step(cfg, xs, tgt, mods, lws, fg):
    nl = len(lws)
    saved = []
    for l in range(nl):
        xs, sv = _layer_fwd(cfg, xs, mods[l], lws[l])
        saved.append(sv)
    loss_b, dx, g_fg = loss_head(xs, tgt, fg)
    lg = [None] * nl
    dmods = [None] * nl
    for l in reversed(range(nl)):
        dx, lg[l], dmods[l] = _layer_bwd(cfg, dx, mods[l], lws[l], saved[l])
    return loss_b, dx, g_fg, lg, dmods


SMALL = ("norm_g", "conv_qkv", "a_log", "dt_bias", "dn_norm_g", "sg_ln_g", "sg_ln_b", "sg_w", "sg_b", "cv_w",
         "cv_b", "cv_ln_g", "cv_ln_b", "final_g", "b_ada")
PACK_N = 1024


def _pack(arrs):
    flat = jnp.concatenate([a.reshape(-1).astype(F32) for a in arrs])
    rows = -(-flat.shape[0] // PACK_N)
    rows = -(-rows // 8) * 8
    return jnp.pad(flat, (0, rows * PACK_N - flat.shape[0])).reshape(rows, PACK_N)


def _unpack(buf, shapes):
    flat = buf.reshape(-1)
    out, o = [], 0
    for s in shapes:
        n = 1
        for d_ in s:
            n *= d_
        out.append(flat[o:o + n].reshape(s))
        o += n
    return out


def kernel(x, c, norm_g, w_ada, b_ada, w_in, conv_qkv, a_log, dt_bias, dn_norm_g, sg_ln_g, sg_ln_b, sg_w, sg_b, cv_w, cv_b, cv_ln_g, cv_ln_b, w_out, final_g, loss_target, m_norm_g, m_w_ada, m_b_ada, m_w_in, m_conv_qkv, m_a_log, m_dt_bias, m_dn_norm_g, m_sg_ln_g, m_sg_ln_b, m_sg_w, m_sg_b, m_cv_w, m_cv_b, m_cv_ln_g, m_cv_ln_b, m_w_out, m_final_g, v_norm_g, v_w_ada, v_b_ada, v_w_in, v_conv_qkv, v_a_log, v_dt_bias, v_dn_norm_g, v_sg_ln_g, v_sg_ln_b, v_sg_w, v_sg_b, v_cv_w, v_cv_b, v_cv_ln_g, v_cv_ln_b, v_w_out, v_final_g):
    cfg = _Cfg(x, a_log, sg_w, cv_ln_g, cv_w, conv_qkv)
    nl, d, t, h = cfg.nl, cfg.d, cfg.t, cfg.h
    lh = nl // 2
    ax, ay, ac = _me()
    chip = 2 * ax + ay
    dev = 2 * chip + ac
    ids = jnp.stack([ac, chip]).astype(jnp.int32)
    wts = dict(norm_g=norm_g, w_ada=w_ada, b_ada=b_ada, w_in=w_in, conv_qkv=conv_qkv, a_log=a_log, dt_bias=dt_bias,
               dn_norm_g=dn_norm_g, sg_ln_g=sg_ln_g, sg_ln_b=sg_ln_b, sg_w=sg_w, sg_b=sg_b, cv_w=cv_w, cv_b=cv_b,
               cv_ln_g=cv_ln_g, cv_ln_b=cv_ln_b, w_out=w_out, final_g=final_g)
    mom = dict(norm_g=m_norm_g, w_ada=m_w_ada, b_ada=m_b_ada, w_in=m_w_in, conv_qkv=m_conv_qkv, a_log=m_a_log,
               dt_bias=m_dt_bias, dn_norm_g=m_dn_norm_g, sg_ln_g=m_sg_ln_g, sg_ln_b=m_sg_ln_b, sg_w=m_sg_w,
               sg_b=m_sg_b, cv_w=m_cv_w, cv_b=m_cv_b, cv_ln_g=m_cv_ln_g, cv_ln_b=m_cv_ln_b, w_out=m_w_out,
               final_g=m_final_g)
    vel = dict(norm_g=v_norm_g, w_ada=v_w_ada, b_ada=v_b_ada, w_in=v_w_in, conv_qkv=v_conv_qkv, a_log=v_a_log,
               dt_bias=v_dt_bias, dn_norm_g=v_dn_norm_g, sg_ln_g=v_sg_ln_g, sg_ln_b=v_sg_ln_b, sg_w=v_sg_w,
               sg_b=v_sg_b, cv_w=v_cv_w, cv_b=v_cv_b, cv_ln_g=v_cv_ln_g, cv_ln_b=v_cv_ln_b, w_out=v_w_out,
               final_g=v_final_g)
    ada_cols = w_ada.shape[2]
    in_cols = w_in.shape[2]
    out_rows = w_out.shape[1]
    cq_cols = conv_qkv.shape[2]
    cvw_cols = cv_w.shape[2]

    c_all = all_gather8(jnp.pad(c, ((0, 7), (0, 0)))).reshape(NDEV, 8, d)[:, 0, :]
    b_loc = lax.dynamic_slice_in_dim(b_ada, chip * ada_cols, ada_cols, axis=1)[:, None, :]
    mod_part = ada_fwd(c_all, w_ada, b_loc)
    mod_all = all_gather8(mod_part.reshape(nl * NDEV, ada_cols)).reshape(NDEV, nl, NDEV, ada_cols)
    mod_me = lax.dynamic_index_in_dim(mod_all[0::2], dev, axis=2, keepdims=False)
    mod_me = jnp.moveaxis(mod_me, 0, 1).reshape(nl, 3, 1, d)

    win_all, wout_all = gather_weights([w_in.astype(BF16).reshape(2, lh, d, in_cols),
                                        w_out.astype(BF16).reshape(2, lh, out_rows, d)])
    win_full = jnp.moveaxis(win_all.reshape(NCHIP, nl, d, in_cols), 0, 2).reshape(nl, d, NCHIP * in_cols)
    wp_all = _perm_cols(cfg, win_full)
    wo_all = jnp.moveaxis(wout_all.reshape(NCHIP, nl, out_rows, d), 0, 1).reshape(nl, NCHIP * out_rows, d)

    cq_all = all_gather8(conv_qkv.reshape(nl * cfg.k4, cq_cols)).reshape(NDEV, nl, cfg.k4, cq_cols)[0::2]
    cq_full = jnp.moveaxis(cq_all, 0, 2).reshape(nl, cfg.k4, NCHIP * cq_cols)
    cq_perm = _perm_cols_qkv(cfg, cq_full)
    kcp = -(-cfg.kc // 8) * 8
    cvw_all = all_gather8(jnp.pad(cv_w, ((0, 0), (0, kcp - cfg.kc), (0, 0))).reshape(nl * kcp, cvw_cols))
    cvw_all = cvw_all.reshape(NDEV, nl, kcp, cvw_cols)[0::2]
    cvw_full = jnp.moveaxis(cvw_all, 0, 2).reshape(nl, kcp, NCHIP * cvw_cols)[:, :cfg.kc]

    hp = -(-h // 8) * 8
    lws = []
    for l in range(nl):
        lws.append(dict(
            norm_g=norm_g[l][None], wp=wp_all[l], wo=wo_all[l], conv_qkv=cq_perm[l],
            alog_b=jnp.pad(jnp.broadcast_to(a_log[l][:, None], (h, LANES)), ((0, hp - h), (0, 0))),
            dtb_b=jnp.pad(jnp.broadcast_to(dt_bias[l][:, None], (h, LANES)), ((0, hp - h), (0, 0))),
            dn_norm_g=dn_norm_g[l][None], sg_ln_g=sg_ln_g[l][None], sg_ln_b=sg_ln_b[l][None], sg_w=sg_w[l],
            sg_bias_b=jnp.broadcast_to(sg_b[l][:, :, None], (cfg.g, LANES, LANES)),
            cv_w=cvw_full[l], cv_b=cv_b[l][None], cv_ln_g=cv_ln_g[l][None], cv_ln_b=cv_ln_b[l][None]))

    mods = [(mod_me[l, 0], mod_me[l, 1], mod_me[l, 2]) for l in range(nl)]
    loss_b, dx, g_fg, lg, dmods = _local_step(cfg, x[0], loss_target[0], mods, lws, final_g[None])
    grad_x = dx[None]

    dmod = jnp.stack([jnp.concatenate(dm, axis=1)[0] for dm in dmods])
    stack = lambda k: jnp.stack([g_[k] for g_ in lg])
    small_local = [stack(k).reshape(wts_shape) for k, wts_shape in
                   (("norm_g", (nl, d)), ("conv_qkv", (nl, cfg.k4, 3 * cfg.dn)), ("a_log", (nl, h)),
                    ("dt_bias", (nl, h)), ("dn_norm_g", (nl, LANES)), ("sg_ln_g", (nl, cfg.sg)),
                    ("sg_ln_b", (nl, cfg.sg)), ("sg_w", (nl, cfg.g, LANES, LANES)), ("sg_b", (nl, cfg.g, LANES)),
                    ("cv_w", (nl, cfg.kc, cfg.cv)), ("cv_b", (nl, cfg.cv)), ("cv_ln_g", (nl, cfg.cv)),
                    ("cv_ln_b", (nl, cfg.cv)))]
    small_local[1] = _unperm_cols_qkv(cfg, small_local[1])
    small_local += [g_fg[0], dmod, loss_b[0, 0:1]]
    shapes = [a.shape for a in small_local]
    packed = _pack(small_local)
    rows = packed.shape[0]
    gathered = all_gather8(packed).reshape(NDEV, rows, PACK_N)
    summed = _unpack(sum8(gathered), shapes)
    sgrads = dict(zip(SMALL, summed[:15]))
    loss = summed[15][0]
    sgrads["conv_qkv"] = lax.dynamic_slice_in_dim(sgrads["conv_qkv"], chip * cq_cols, cq_cols, axis=2)
    sgrads["cv_w"] = lax.dynamic_slice_in_dim(sgrads["cv_w"], chip * cvw_cols, cvw_cols, axis=2)

    off = sum(math.prod(s) for s in shapes[:14])
    dmod_all = gathered.reshape(NDEV, rows * PACK_N)[:, off:off + nl * 3 * d].reshape(NDEV, nl, 3 * d)
    dmod_loc = jnp.moveaxis(lax.dynamic_slice_in_dim(dmod_all, chip * ada_cols, ada_cols, axis=2), 0, 1)
    g_wada, d_wada, nm_wada, nv_wada = ada_bwd(c_all.T, dmod_loc, w_ada, m_w_ada, v_w_ada)

    g_wp = _unperm_cols(cfg, jnp.stack([g_["wp"] for g_ in lg]))
    g_in = jnp.moveaxis(g_wp.reshape(2, lh, d, NCHIP, in_cols), 3, 1).reshape(2, NCHIP, lh * d, in_cols)
    g_wo = jnp.stack([g_["wo"] for g_ in lg]).reshape(2, lh, NCHIP, out_rows, d)
    g_out = jnp.moveaxis(g_wo, 2, 1).reshape(2, NCHIP, lh * out_rows, d)
    r1_in, r1_out = swap_halves([g_in, g_out])
    p_in, p_out = pair_sum_bf16(ids, g_in, r1_in), pair_sum_bf16(ids, g_out, r1_out)
    r2_in, r2_out = chip_exchange([p_in, p_out])
    h_in, h_out = chip_sum(ids, g_in, r1_in, r2_in), chip_sum(ids, g_out, r1_out, r2_out)
    j_in, j_out = join_halves([h_in, h_out])
    grad_w_in = j_in.reshape(nl, d, in_cols)
    grad_w_out = j_out.reshape(nl, out_rows, d)

    d_in_, nm_in, nv_in = adamw("adamw_w_in", w_in.reshape(nl * d, in_cols), grad_w_in.reshape(nl * d, in_cols),
                                m_w_in.reshape(nl * d, in_cols), v_w_in.reshape(nl * d, in_cols))
    d_out_, nm_out, nv_out = adamw("adamw_w_out", w_out.reshape(nl * out_rows, d), grad_w_out.reshape(nl * out_rows, d),
                                   m_w_out.reshape(nl * out_rows, d), v_w_out.reshape(nl * out_rows, d))
    sshapes = [wts[k].shape for k in SMALL]
    pk = lambda dct: _pack([dct[k] for k in SMALL])
    d_s, m_s, v_s = adamw("adamw_small", pk(wts), pk(sgrads), pk(mom), pk(vel))
    d_small = dict(zip(SMALL, _unpack(d_s, sshapes)))
    m_small = dict(zip(SMALL, _unpack(m_s, sshapes)))
    v_small = dict(zip(SMALL, _unpack(v_s, sshapes)))

    grads = dict(sgrads, w_ada=g_wada, w_in=grad_w_in, w_out=grad_w_out)
    deltas = dict(d_small, w_ada=d_wada, w_in=d_in_.reshape(w_in.shape), w_out=d_out_.reshape(w_out.shape))
    new_m = dict(m_small, w_ada=nm_wada, w_in=nm_in.reshape(w_in.shape), w_out=nm_out.reshape(w_out.shape))
    new_v = dict(v_small, w_ada=nv_wada, w_in=nv_in.reshape(w_in.shape), w_out=nv_out.reshape(w_out.shape))
    order = ("norm_g", "w_ada", "b_ada", "w_in", "conv_qkv", "a_log", "dt_bias", "dn_norm_g", "sg_ln_g", "sg_ln_b",
             "sg_w", "sg_b", "cv_w", "cv_b", "cv_ln_g", "cv_ln_b", "w_out", "final_g")
    return (loss, grad_x, *[grads[k] for k in order], *[deltas[k] for k in order], *[new_m[k] for k in order],
            *[new_v[k] for k in order])


def _perm_cols_qkv(cfg, w):
    lead = w.shape[:-1]
    return jnp.moveaxis(w.reshape(lead + (3, cfg.h, LANES)), -3, -2).reshape(lead + (3 * cfg.dn,))


def _unperm_cols_qkv(cfg, w):
    lead = w.shape[:-1]
    return jnp.moveaxis(w.reshape(lead + (cfg.h, 3, LANES)), -3, -2).reshape(lead + (3 * cfg.dn,))
```

```python
import functools
import math

import jax
import jax.numpy as jnp
from jax import lax
from jax.experimental import pallas as pl
from jax.experimental.pallas import tpu as pltpu

F32 = jnp.float32
BF16 = jnp.bfloat16
EPS = 1e-6
LN_EPS = 1e-5
LANES = 128
CHUNK = 64
HALO4 = 8
HALO31 = 32
NCHIP = 4
NDEV = 8
VMEM_LIMIT = 56 * 2 ** 20
ADAM_LR, ADAM_B1, ADAM_B2, ADAM_EPS, ADAM_WD, ADAM_STEP = 0.001, 0.9, 0.999, 1e-08, 0.01, 10
MESH = pl.DeviceIdType.MESH
ANY = pl.BlockSpec(memory_space=pl.ANY)


def _cp(sem=None, vmem=VMEM_LIMIT):
    return pltpu.CompilerParams(dimension_semantics=sem, vmem_limit_bytes=vmem)


def _tile(n, pref, mult):
    t = min(n, pref) // mult * mult
    while t > 0 and n % t:
        t -= mult
    return t if t > 0 else n


def _split(a):
    hi = a.astype(BF16)
    return hi, (a - hi.astype(F32)).astype(BF16)


def _raw_dot(a, b, ca, cb, hi):
    dn = (((ca,), (cb,)), ((), ()))
    if hi:
        ah, al = _split(a.astype(F32))
        bh, bl = _split(b.astype(F32))
        d3 = lambda x, y: lax.dot_general(x, y, dn, preferred_element_type=F32)
        return d3(ah, bh) + (d3(al, bh) + d3(ah, bl))
    return lax.dot_general(a.astype(BF16), b.astype(BF16), dn, preferred_element_type=F32)


@functools.partial(jax.custom_vjp, nondiff_argnums=(2, 3, 4))
def bdot(a, b, ca, cb, hi):
    return _raw_dot(a, b, ca, cb, hi)


def _bdot_fwd(a, b, ca, cb, hi):
    return _raw_dot(a, b, ca, cb, hi), (a, b)


def _bdot_bwd(ca, cb, hi, res, ct):
    a, b = res
    fa, fb = 1 - ca, 1 - cb
    da = _raw_dot(ct, b, 1, fb, hi) if ca == 1 else _raw_dot(b, ct, fb, 1, hi)
    db = _raw_dot(a, ct, fa, 0, hi) if cb == 0 else _raw_dot(ct, a, 0, fa, hi)
    return da.astype(a.dtype), db.astype(b.dtype)


bdot.defvjp(_bdot_fwd, _bdot_bwd)


def _sigmoid(x):
    return 1.0 / (1.0 + jnp.exp(-x))


def _silu(x):
    return x * _sigmoid(x)


def _gelu(x):
    return 0.5 * x * (1.0 + lax.erf(x * (2.0 ** -0.5)))


def _softplus(x):
    return jnp.maximum(x, 0.0) + jnp.log(1.0 + jnp.exp(-jnp.abs(x)))


def _modnorm(x, g, scale, shift):
    y = x * lax.rsqrt(jnp.mean(x * x, axis=-1, keepdims=True) + EPS)
    return (y * g) * (1.0 + scale) + shift


def _rmsnorm(x, g):
    return x * lax.rsqrt(jnp.mean(x * x, axis=-1, keepdims=True) + EPS) * g


def _layernorm(x, g, b):
    mu = jnp.mean(x, axis=-1, keepdims=True)
    xc = x - mu
    var = jnp.mean(xc * xc, axis=-1, keepdims=True)
    return xc * lax.rsqrt(var + LN_EPS) * g + b


def _l2norm(t):
    return t * lax.rsqrt(jnp.sum(t * t, axis=-1, keepdims=True) + EPS)


def _adamw_math(w, g, m, v):
    mn = ADAM_B1 * m + (1.0 - ADAM_B1) * g
    vn = ADAM_B2 * v + (1.0 - ADAM_B2) * (g * g)
    mh = mn / (1.0 - ADAM_B1 ** ADAM_STEP)
    vh = vn / (1.0 - ADAM_B2 ** ADAM_STEP)
    delta = -ADAM_LR * (mh / (jnp.sqrt(vh) + ADAM_EPS) + ADAM_WD * w)
    return delta, mn, vn


def _each(f, *lists):
    return [f(*xs) for xs in zip(*lists)]


def _wy(q, k, v, bcol, acol, alog, dtb):
    c = CHUNK
    r = lax.broadcasted_iota(jnp.int32, (c, c), 0)
    cc = lax.broadcasted_iota(jnp.int32, (c, c), 1)
    rr = lax.broadcasted_iota(jnp.int32, (c, 1), 0)
    tri_incl, tri_strict, eye = r >= cc, r > cc, r == cc
    beta = _each(_sigmoid, bcol)
    g = _each(lambda al, a_, dt: -jnp.exp(al) * _softplus(a_ + dt), alog, acol, dtb)
    gb = [jnp.broadcast_to(g_, (c, c)) for g_ in g]
    g_row = [jnp.sum(jnp.where(eye, b_, 0.0), axis=0, keepdims=True) for b_ in gb]
    gc_col = [jnp.sum(jnp.where(tri_incl, jnp.broadcast_to(gr, (c, c)), 0.0), axis=1, keepdims=True) for gr in g_row]
    gc_row = [jnp.sum(jnp.where(r <= cc, b_, 0.0), axis=0, keepdims=True) for b_ in gb]
    decay = _each(lambda gcc, gcr: jnp.where(tri_incl, jnp.exp(jnp.where(tri_incl, gcc - gcr, 0.0)), 0.0),
                  gc_col, gc_row)
    qs = [q_ * (q_.shape[-1] ** -0.5) for q_ in q]
    kb = _each(lambda k_, b_: k_ * b_, k, beta)
    a = _each(lambda kb_, k_, d_: jnp.where(tri_strict, bdot(kb_, k_, 1, 1, False) * d_, 0.0), kb, k, decay)
    dv = v[0].shape[-1]
    x = _each(lambda v_, b_, kb_, gcc: jnp.concatenate([v_ * b_, kb_ * jnp.exp(gcc)], axis=1), v, beta, kb, gc_col)
    x = _each(lambda a_, x_: x_ - bdot(a_, x_, 1, 0, True), a, x)
    p = a
    for _ in range(5):
        p = _each(lambda p_: bdot(p_, p_, 1, 0, True), p)
        x = _each(lambda p_, x_: x_ + bdot(p_, x_, 1, 0, True), p, x)
    xv = [x_[:, :dv] for x_ in x]
    xk = [x_[:, dv:] for x_ in x]
    qk = _each(lambda q_, k_, d_: bdot(q_, k_, 1, 1, False) * d_, qs, k, decay)
    g_last = [jnp.sum(jnp.where(rr == c - 1, gcc, 0.0), axis=0, keepdims=True) for gcc in gc_col]
    qg = _each(lambda q_, gcc: q_ * jnp.exp(gcc), qs, gc_col)
    kd = _each(lambda k_, gl, gcc: k_ * jnp.exp(gl - gcc), k, g_last, gc_col)
    return xv, xk, qg, kd, qk, [jnp.exp(gl) for gl in g_last]


def _seq(u, w, qg, kd, qk, e, z, s, ng):
    v_new = _each(lambda u_, w_, s_: u_ - bdot(w_, s_, 1, 0, False), u, w, s)
    o1 = _each(lambda q_, s_: bdot(q_, s_, 1, 0, False), qg, s)
    o2 = _each(lambda qk_, vn: bdot(qk_, vn, 1, 0, False), qk, v_new)
    ds = _each(lambda kd_, vn: bdot(kd_, vn, 0, 0, False), kd, v_new)
    s_next = _each(lambda s_, e_, d_: s_ * e_ + d_, s, e, ds)
    y = _each(lambda a_, b_, z_: _rmsnorm(a_ + b_, ng) * _silu(z_), o1, o2, z)
    return y, s_next


def _sg_block(u, v, gt, lg, lb, w, bias):
    n = w.shape[0]
    pr = lax.broadcasted_iota(jnp.int32, (n, n), 0) // CHUNK
    pc = lax.broadcasted_iota(jnp.int32, (n, n), 1) // CHUNK
    wm = jnp.where(pr >= pc, w, 0.0)
    vl = _layernorm(_gelu(v), lg, lb)
    mixed = bdot(wm, vl, 1, 0, False) + bias
    return _gelu(u) * mixed * _silu(gt)


def _glu(a, b):
    return a * _sigmoid(b)


def _cv_post(conv, gate, cb, lg, lb):
    return _silu(_layernorm(conv + cb, lg, lb)) * _silu(gate)


def _qk_post(conv):
    return _l2norm(_silu(conv))


def _v_post(conv):
    return _silu(conv)


def in_proj(x, shift, scale, ng, wp):
    t, d = x.shape
    npc = wp.shape[1]
    tm, tn = _tile(t, 1024, LANES), _tile(npc, 384, LANES)

    def body(x_ref, sh_ref, sc_ref, g_ref, w_ref, p_ref, ht_ref, h_scr):
        @pl.when(pl.program_id(1) == 0)
        def _():
            h = _modnorm(x_ref[...], g_ref[...], sc_ref[...], sh_ref[...])
            h_scr[...] = h.astype(BF16)
            ht_ref[...] = h.T.astype(BF16)
        p_ref[...] = jnp.dot(h_scr[...], w_ref[...], preferred_element_type=F32)

    vec = pl.BlockSpec((1, d), lambda i, j: (0, 0))
    return pl.pallas_call(
        body, name="in_proj", grid=(t // tm, npc // tn),
        in_specs=[pl.BlockSpec((tm, d), lambda i, j: (i, 0)), vec, vec, vec,
                  pl.BlockSpec((d, tn), lambda i, j: (0, j))],
        out_specs=[pl.BlockSpec((tm, tn), lambda i, j: (i, j)), pl.BlockSpec((d, tm), lambda i, j: (0, i))],
        out_shape=[jax.ShapeDtypeStruct((t, npc), F32), jax.ShapeDtypeStruct((d, t), BF16)],
        scratch_shapes=[pltpu.VMEM((tm, d), BF16)],
        compiler_params=_cp(("parallel", "arbitrary")),
    )(x, shift, scale, ng, wp)


def conv_fwd(name, k, halo, pre_fn, pre, w, post_fns, extras, params, c_total, tc, tm_pref=512):
    t = pre[0][0].shape[0]
    tm = _tile(t, tm_pref, halo)
    npre, nex, npar = len(pre), len(extras), len(params)
    ngr = tc // LANES

    def body(*refs):
        prev = refs[:npre]
        cur = refs[npre:2 * npre]
        w_ref = refs[2 * npre]
        ex = refs[2 * npre + 1:2 * npre + 1 + nex]
        par = refs[2 * npre + 1 + nex:2 * npre + 1 + nex + npar]
        out_ref, buf = refs[-2], refs[-1]
        i = pl.program_id(1)
        pv = pre_fn(*[r[...] for r in prev])
        buf[0:halo, :] = jnp.where(i > 0, pv, 0.0)
        buf[halo:, :] = pre_fn(*[r[...] for r in cur])
        acc = w_ref[0:1, :] * buf[pl.ds(halo - (k - 1), tm), :]
        for j in range(1, k):
            acc = acc + w_ref[j:j + 1, :] * buf[pl.ds(halo - (k - 1) + j, tm), :]
        for gi in range(ngr):
            sl = slice(gi * LANES, (gi + 1) * LANES)
            out_ref[:, sl] = post_fns[gi](acc[:, sl], *[e[:, sl] for e in ex], *[p_[:, sl] for p_ in par])

    hb = tm // halo
    in_specs = ([pl.BlockSpec((halo, tc), functools.partial(lambda j, i, o: (jnp.maximum(i * hb - 1, 0), o + j), o=col // tc))
                 for _, col in pre]
                + [pl.BlockSpec((tm, tc), functools.partial(lambda j, i, o: (i, o + j), o=col // tc)) for _, col in pre]
                + [pl.BlockSpec((k, tc), lambda j, i: (0, j))]
                + [pl.BlockSpec((tm, tc), functools.partial(lambda j, i, o: (i, o + j), o=col // tc)) for _, col in extras]
                + [pl.BlockSpec((1, tc), lambda j, i: (0, j)) for _ in params])
    args = [a for a, _ in pre] * 2 + [w] + [a for a, _ in extras] + list(params)
    return pl.pallas_call(
        body, name=name, grid=(c_total // tc, t // tm), in_specs=in_specs,
        out_specs=pl.BlockSpec((tm, tc), lambda j, i: (i, j)),
        out_shape=jax.ShapeDtypeStruct((t, c_total), F32),
        scratch_shapes=[pltpu.VMEM((halo + tm, tc), F32)],
        compiler_params=_cp(("parallel", "arbitrary")),
    )(*args)


def conv_bwd(name, k, halo, pre_fn, pre, w, post_fns, extras, params, dout, c_total, tc, tm_pref=512):
    t = pre[0][0].shape[0]
    tm = _tile(t, tm_pref, halo)
    npre, nex, npar = len(pre), len(extras), len(params)
    ngr = tc // LANES
    nout = npre + nex
    assert nout == 1 or c_total == tc
    nblk = t // tm

    def body(*refs):
        it = iter(refs)
        prev = [next(it) for _ in range(npre)]
        cur = [next(it) for _ in range(npre)]
        nxt = [next(it) for _ in range(npre)]
        w_ref = next(it)
        ex_c = [next(it) for _ in range(nex)]
        ex_n = [next(it) for _ in range(nex)]
        par = [next(it) for _ in range(npar)]
        do_c, do_n = next(it), next(it)
        din_ref, dw_ref = next(it), next(it)
        dpar = [next(it) for _ in range(npar)]
        buf, dbuf = next(it), next(it)
        i = pl.program_id(1)

        @pl.when(i == 0)
        def _():
            dw_ref[...] = jnp.zeros_like(dw_ref)
            for r in dpar:
                r[...] = jnp.zeros_like(r)

        buf[0:halo, :] = jnp.where(i > 0, pre_fn(*[r[...] for r in prev]), 0.0)
        cur_vals = [r[...] for r in cur]
        buf[halo:halo + tm, :] = pre_fn(*cur_vals)
        buf[halo + tm:, :] = pre_fn(*[r[...] for r in nxt])
        ext = tm + halo
        conv = w_ref[0:1, :] * buf[pl.ds(halo - (k - 1), ext), :]
        for j in range(1, k):
            conv = conv + w_ref[j:j + 1, :] * buf[pl.ds(halo - (k - 1) + j, ext), :]
        don = jnp.where(i < nblk - 1, do_n[...], 0.0)
        for gi in range(ngr):
            sl = slice(gi * LANES, (gi + 1) * LANES)
            pv = [p_[:, sl] for p_ in par]
            _, vj = jax.vjp(post_fns[gi], conv[:tm, sl], *[e[:, sl] for e in ex_c], *pv)
            gr = vj(do_c[:, sl])
            dbuf[0:tm, sl] = gr[0]
            for e in range(nex):
                din_ref[:, (npre + e) * tc + gi * LANES:(npre + e) * tc + (gi + 1) * LANES] = gr[1 + e].astype(din_ref.dtype)
            for q_ in range(npar):
                dpar[q_][:, sl] += gr[1 + nex + q_]
            _, vjn = jax.vjp(post_fns[gi], conv[tm:, sl], *[e[:, sl] for e in ex_n], *pv)
            dbuf[tm:, sl] = vjn(don[:, sl])[0]
        dcur = dbuf[0:tm, :]
        dpre = w_ref[0:1, :] * dbuf[pl.ds(k - 1, tm), :]
        dw_ref[0:1, :] += jnp.sum(dcur * buf[pl.ds(halo - (k - 1), tm), :], axis=0, keepdims=True)
        for j in range(1, k):
            dpre = dpre + w_ref[j:j + 1, :] * dbuf[pl.ds(k - 1 - j, tm), :]
            dw_ref[j:j + 1, :] += jnp.sum(dcur * buf[pl.ds(halo - (k - 1) + j, tm), :], axis=0, keepdims=True)
        _, vjp_pre = jax.vjp(pre_fn, *cur_vals)
        for e, gval in enumerate(vjp_pre(dpre)):
            din_ref[:, e * tc:(e + 1) * tc] = gval.astype(din_ref.dtype)

    hb = tm // halo
    last_h = t // halo - 1

    def spec(kind, col):
        o = col // tc
        if kind == "prev":
            return pl.BlockSpec((halo, tc), lambda j, i: (jnp.maximum(i * hb - 1, 0), o + j))
        if kind == "next":
            return pl.BlockSpec((halo, tc), lambda j, i: (jnp.minimum((i + 1) * hb, last_h), o + j))
        return pl.BlockSpec((tm, tc), lambda j, i: (i, o + j))

    in_specs = ([spec("prev", col) for _, col in pre] + [spec("cur", col) for _, col in pre]
                + [spec("next", col) for _, col in pre] + [pl.BlockSpec((k, tc), lambda j, i: (0, j))]
                + [spec("cur", col) for _, col in extras] + [spec("next", col) for _, col in extras]
                + [pl.BlockSpec((1, tc), lambda j, i: (0, j)) for _ in params]
                + [spec("cur", 0), spec("next", 0)])
    args = [a for a, _ in pre] * 3 + [w] + [a for a, _ in extras] * 2 + list(params) + [dout, dout]
    out = pl.pallas_call(
        body, name=name, grid=(c_total // tc, nblk), in_specs=in_specs,
        out_specs=[pl.BlockSpec((tm, nout * tc), lambda j, i: (i, j)), pl.BlockSpec((k, tc), lambda j, i: (0, j))]
        + [pl.BlockSpec((1, tc), lambda j, i: (0, j)) for _ in params],
        out_shape=[jax.ShapeDtypeStruct((t, nout * c_total), BF16), jax.ShapeDtypeStruct((k, c_total), F32)]
        + [jax.ShapeDtypeStruct((1, c_total), F32) for _ in params],
        scratch_shapes=[pltpu.VMEM((2 * halo + tm, tc), F32), pltpu.VMEM((halo + tm, tc), F32)],
        compiler_params=_cp(("parallel", "arbitrary")),
    )(*args)
    return out[0], out[1], out[2:]


def _head_pick(ref_val, row):
    rr = lax.broadcasted_iota(jnp.int32, ref_val.shape, 0)
    v = jnp.sum(jnp.where(rr == row, ref_val, 0.0), axis=0, keepdims=True)
    ll = lax.broadcasted_iota(jnp.int32, v.shape, 1)
    return jnp.sum(jnp.where(ll == 0, v, 0.0), axis=1, keepdims=True)


def _lane_col(blk, lane_idx):
    ll = lax.broadcasted_iota(jnp.int32, blk.shape, 1)
    return jnp.sum(jnp.where(ll == lane_idx, blk, 0.0), axis=1, keepdims=True)


WY_HEADS = 2
WY_UNROLL = 4


def dn_wy_fwd(qkv, p, ba_col, alog_b, dtb_b, nheads):
    t = qkv.shape[0]
    tm = _tile(t, 512, CHUNK * WY_UNROLL)
    nc = tm // CHUNK
    hb = WY_HEADS
    hp = alog_b.shape[0]
    w_ = hb * LANES

    def body(qkv_ref, ba_ref, al_ref, dt_ref, u_ref, w_ref, qg_ref, kd_ref, qk_ref, e_ref):
        hblk = pl.program_id(1)
        alv, dtv = al_ref[...], dt_ref[...]

        def trip(cj, carry):
            units = [(cj * WY_UNROLL + cu, hl) for cu in range(WY_UNROLL) for hl in range(hb)]
            args = [[] for _ in range(7)]
            for ci, hl in units:
                rows = pl.ds(pl.multiple_of(ci * CHUNK, CHUNK), CHUNK)
                ba = ba_ref[rows, :]
                h = hblk * hb + hl
                for lst, val in zip(args, (qkv_ref[rows, hl * 384:hl * 384 + 128],
                                           qkv_ref[rows, hl * 384 + 128:hl * 384 + 256],
                                           qkv_ref[rows, hl * 384 + 256:hl * 384 + 384],
                                           _lane_col(ba, h), _lane_col(ba, nheads + h),
                                           _head_pick(alv, h), _head_pick(dtv, h))):
                    lst.append(val)
            outs = _wy(*args)
            for n, (ci, hl) in enumerate(units):
                rows = pl.ds(pl.multiple_of(ci * CHUNK, CHUNK), CHUNK)
                u, w, qg, kd, qk, e = [o[n] for o in outs]
                sl = slice(hl * LANES, (hl + 1) * LANES)
                u_ref[rows, sl] = u
                w_ref[rows, sl] = w.astype(BF16)
                qg_ref[rows, sl] = qg.astype(BF16)
                kd_ref[rows, sl] = kd.astype(BF16)
                qk_ref[rows, hl * LANES:hl * LANES + CHUNK] = qk.astype(BF16)
                qk_ref[rows, hl * LANES + CHUNK:(hl + 1) * LANES] = jnp.zeros((CHUNK, LANES - CHUNK), BF16)
                e_ref[ci, :, sl] = jnp.broadcast_to(e, (1, LANES))
            return carry

        lax.fori_loop(0, nc // WY_UNROLL, trip, 0)

    bc = ba_col // LANES
    blk = pl.BlockSpec((tm, w_), lambda i, h: (i, h))
    tab = pl.BlockSpec((hp, LANES), lambda i, h: (0, 0))
    wide = lambda dt: jax.ShapeDtypeStruct((t, nheads * LANES), dt)
    return pl.pallas_call(
        body, name="dn_wy_fwd", grid=(t // tm, nheads // hb),
        in_specs=[pl.BlockSpec((tm, hb * 384), lambda i, h: (i, h)), pl.BlockSpec((tm, LANES), lambda i, h: (i, bc)),
                  tab, tab],
        out_specs=[blk] * 5 + [pl.BlockSpec((nc, 1, w_), lambda i, h: (i, 0, h))],
        out_shape=[wide(F32), wide(BF16), wide(BF16), wide(BF16), wide(BF16),
                   jax.ShapeDtypeStruct((t // CHUNK, 1, nheads * LANES), F32)],
        compiler_params=_cp(("parallel", "parallel")),
    )(qkv, p, alog_b, dtb_b)


def dn_seq_fwd(u, w, qg, kd, qk, e, p, z_col, ng, nheads, hb):
    t = u.shape[0]
    tm = _tile(t, 512, CHUNK)
    nc = tm // CHUNK
    w_ = hb * LANES

    def body(u_ref, w_ref, qg_ref, kd_ref, qk_ref, e_ref, z_ref, ng_ref, y_ref, ss_ref, s_scr):
        i, hblk = pl.program_id(0), pl.program_id(1)
        for hl in range(hb):
            @pl.when(i == 0)
            def _():
                s_scr[hblk * hb + hl] = jnp.zeros((LANES, LANES), F32)
        ngv = ng_ref[...]

        def chunk(ci, carry):
            rows = pl.ds(pl.multiple_of(ci * CHUNK, CHUNK), CHUNK)
            ev = e_ref[ci]
            sls = [slice(hl * LANES, (hl + 1) * LANES) for hl in range(hb)]
            s = [s_scr[hblk * hb + hl] for hl in range(hb)]
            for hl in range(hb):
                ss_ref[ci, sls[hl], :] = s[hl]
            y, sn = _seq([u_ref[rows, sl] for sl in sls], [w_ref[rows, sl].astype(F32) for sl in sls],
                         [qg_ref[rows, sl].astype(F32) for sl in sls], [kd_ref[rows, sl].astype(F32) for sl in sls],
                         [qk_ref[rows, sl][:, :CHUNK].astype(F32) for sl in sls], [ev[:, sl] for sl in sls],
                         [z_ref[rows, sl] for sl in sls], s, ngv)
            for hl in range(hb):
                y_ref[rows, sls[hl]] = y[hl]
                s_scr[hblk * hb + hl] = sn[hl]
            return carry

        lax.fori_loop(0, nc, chunk, 0)

    zc = z_col // w_
    blk = pl.BlockSpec((tm, w_), lambda i, h: (i, h))
    return pl.pallas_call(
        body, name="dn_seq_fwd", grid=(t // tm, nheads // hb),
        in_specs=[blk] * 5 + [pl.BlockSpec((nc, 1, w_), lambda i, h: (i, 0, h)),
                              pl.BlockSpec((tm, w_), lambda i, h: (i, zc + h)),
                              pl.BlockSpec((1, LANES), lambda i, h: (0, 0))],
        out_specs=[blk, pl.BlockSpec((nc, w_, LANES), lambda i, h: (i, h, 0))],
        out_shape=[jax.ShapeDtypeStruct((t, nheads * LANES), F32),
                   jax.ShapeDtypeStruct((t // CHUNK, nheads * LANES, LANES), F32)],
        scratch_shapes=[pltpu.VMEM((nheads, LANES, LANES), F32)],
        compiler_params=_cp(("arbitrary", "arbitrary")),
    )(u, w, qg, kd, qk, e, p, ng)


def dn_seq_bwd(u, w, qg, kd, qk, e, p, z_col, ng, ss, dy, nheads, hb):
    t = u.shape[0]
    tm = _tile(t, 512, CHUNK)
    nc = tm // CHUNK
    nblk = t // tm
    w_ = hb * LANES

    def body(u_ref, w_ref, qg_ref, kd_ref, qk_ref, e_ref, z_ref, ng_ref, ss_ref, dy_ref,
             du_ref, dw_ref, dqg_ref, dkd_ref, dqk_ref, de_ref, dz_ref, dng_ref, ds_scr):
        i, hblk = pl.program_id(0), pl.program_id(1)

        @pl.when((i == 0) & (hblk == 0))
        def _():
            dng_ref[...] = jnp.zeros_like(dng_ref)

        for hl in range(hb):
            @pl.when(i == 0)
            def _():
                ds_scr[hblk * hb + hl] = jnp.zeros((LANES, LANES), F32)
        ngv = ng_ref[...]

        def chunk(cj, carry):
            ci = nc - 1 - cj
            rows = pl.ds(pl.multiple_of(ci * CHUNK, CHUNK), CHUNK)
            ev = e_ref[ci]
            sls = [slice(hl * LANES, (hl + 1) * LANES) for hl in range(hb)]
            _, vj = jax.vjp(_seq, [u_ref[rows, sl] for sl in sls], [w_ref[rows, sl].astype(F32) for sl in sls],
                            [qg_ref[rows, sl].astype(F32) for sl in sls], [kd_ref[rows, sl].astype(F32) for sl in sls],
                            [qk_ref[rows, sl][:, :CHUNK].astype(F32) for sl in sls], [ev[:, sl] for sl in sls],
                            [z_ref[rows, sl] for sl in sls], [ss_ref[ci, sl, :] for sl in sls], ngv)
            du, dw, dqg, dkd, dqk, de, dz, dsp, dng = vj(([dy_ref[rows, sl] for sl in sls],
                                                          [ds_scr[hblk * hb + hl] for hl in range(hb)]))
            for hl, sl in enumerate(sls):
                du_ref[rows, sl] = du[hl]
                dw_ref[rows, sl] = dw[hl]
                dqg_ref[rows, sl] = dqg[hl]
                dkd_ref[rows, sl] = dkd[hl]
                dqk_ref[rows, hl * LANES:hl * LANES + CHUNK] = dqk[hl]
                dqk_ref[rows, hl * LANES + CHUNK:(hl + 1) * LANES] = jnp.zeros((CHUNK, LANES - CHUNK), F32)
                de_ref[ci, :, sl] = de[hl]
                dz_ref[rows, sl] = dz[hl]
                ds_scr[hblk * hb + hl] = dsp[hl]
            dng_ref[...] += dng
            return carry

        lax.fori_loop(0, nc, chunk, 0)

    zc = z_col // w_
    rv = lambda i: nblk - 1 - i
    blk = pl.BlockSpec((tm, w_), lambda i, h: (rv(i), h))
    eblk = pl.BlockSpec((nc, 1, w_), lambda i, h: (rv(i), 0, h))
    one = pl.BlockSpec((1, LANES), lambda i, h: (0, 0))
    wide = jax.ShapeDtypeStruct((t, nheads * LANES), F32)
    return pl.pallas_call(
        body, name="dn_seq_bwd", grid=(nblk, nheads // hb),
        in_specs=[blk] * 5 + [eblk, pl.BlockSpec((tm, w_), lambda i, h: (rv(i), zc + h)), one,
                              pl.BlockSpec((nc, w_, LANES), lambda i, h: (rv(i), h, 0)), blk],
        out_specs=[blk] * 5 + [eblk, blk, one],
        out_shape=[wide] * 5 + [jax.ShapeDtypeStruct((t // CHUNK, 1, nheads * LANES), F32), wide,
                                jax.ShapeDtypeStruct((1, LANES), F32)],
        scratch_shapes=[pltpu.VMEM((nheads, LANES, LANES), F32)],
        compiler_params=_cp(("arbitrary", "arbitrary")),
    )(u, w, qg, kd, qk, e, p, ng, ss, dy)


def dn_wy_bwd(qkv, p, ba_col, alog_b, dtb_b, du, dw, dqg, dkd, dqk, de, nheads):
    t = qkv.shape[0]
    tm = _tile(t, 512, CHUNK * WY_UNROLL)
    nc = tm // CHUNK
    hb = WY_HEADS
    hp = alog_b.shape[0]
    w_ = hb * LANES

    def body(qkv_ref, ba_ref, al_ref, dt_ref, du_ref, dw_ref, dqg_ref, dkd_ref, dqk_ref, de_ref,
             dqkv_ref, dba_ref, dal_ref, ddt_ref):
        i, hblk = pl.program_id(0), pl.program_id(1)

        @pl.when((i == 0) & (hblk == 0))
        def _():
            dal_ref[...] = jnp.zeros_like(dal_ref)
            ddt_ref[...] = jnp.zeros_like(ddt_ref)

        @pl.when(hblk == 0)
        def _():
            dba_ref[...] = jnp.zeros_like(dba_ref)

        alv, dtv = al_ref[...], dt_ref[...]
        lane = lax.broadcasted_iota(jnp.int32, (CHUNK, LANES), 1)
        rowp = lax.broadcasted_iota(jnp.int32, (hp, LANES), 0)

        def trip(cj, carry):
            units = [(cj * WY_UNROLL + cu, hl) for cu in range(WY_UNROLL) for hl in range(hb)]
            args = [[] for _ in range(7)]
            cts = [[] for _ in range(6)]
            for ci, hl in units:
                rows = pl.ds(pl.multiple_of(ci * CHUNK, CHUNK), CHUNK)
                ba = ba_ref[rows, :]
                h = hblk * hb + hl
                sl = slice(hl * LANES, (hl + 1) * LANES)
                for lst, val in zip(args, (qkv_ref[rows, hl * 384:hl * 384 + 128],
                                           qkv_ref[rows, hl * 384 + 128:hl * 384 + 256],
                                           qkv_ref[rows, hl * 384 + 256:hl * 384 + 384],
                                           _lane_col(ba, h), _lane_col(ba, nheads + h),
                                           _head_pick(alv, h), _head_pick(dtv, h))):
                    lst.append(val)
                de11 = jnp.sum(de_ref[ci][:, sl], axis=1, keepdims=True)
                for lst, val in zip(cts, (du_ref[rows, sl], dw_ref[rows, sl], dqg_ref[rows, sl], dkd_ref[rows, sl],
                                          dqk_ref[rows, sl][:, :CHUNK], de11)):
                    lst.append(val)
            _, vj = jax.vjp(_wy, *args)
            grads = vj(tuple(cts))
            for n, (ci, hl) in enumerate(units):
                rows = pl.ds(pl.multiple_of(ci * CHUNK, CHUNK), CHUNK)
                h = hblk * hb + hl
                dq, dk, dv, dbc, dac, dal, ddt = [g_[n] for g_ in grads]
                dqkv_ref[rows, hl * 384:hl * 384 + 128] = dq
                dqkv_ref[rows, hl * 384 + 128:hl * 384 + 256] = dk
                dqkv_ref[rows, hl * 384 + 256:hl * 384 + 384] = dv
                dba_ref[rows, :] += jnp.where(lane == h, dbc, 0.0) + jnp.where(lane == nheads + h, dac, 0.0)
                dal_ref[...] += jnp.where(rowp == h, dal, 0.0)
                ddt_ref[...] += jnp.where(rowp == h, ddt, 0.0)
            return carry

        lax.fori_loop(0, nc // WY_UNROLL, trip, 0)

    bc = ba_col // LANES
    blk = pl.BlockSpec((tm, w_), lambda i, h: (i, h))
    tab = pl.BlockSpec((hp, LANES), lambda i, h: (0, 0))
    return pl.pallas_call(
        body, name="dn_wy_bwd", grid=(t // tm, nheads // hb),
        in_specs=[pl.BlockSpec((tm, hb * 384), lambda i, h: (i, h)), pl.BlockSpec((tm, LANES), lambda i, h: (i, bc)),
                  tab, tab] + [blk] * 5 + [pl.BlockSpec((nc, 1, w_), lambda i, h: (i, 0, h))],
        out_specs=[pl.BlockSpec((tm, hb * 384), lambda i, h: (i, h)), pl.BlockSpec((tm, LANES), lambda i, h: (i, 0)),
                   tab, tab],
        out_shape=[jax.ShapeDtypeStruct((t, nheads * 384), F32), jax.ShapeDtypeStruct((t, LANES), F32),
                   jax.ShapeDtypeStruct((hp, LANES), F32), jax.ShapeDtypeStruct((hp, LANES), F32)],
        compiler_params=_cp(("arbitrary", "arbitrary")),
    )(qkv, p, alog_b, dtb_b, du, dw, dqg, dkd, dqk, de)


def sg_fwd(p, col, sg, lg, lb, w, bias_b):
    t = p.shape[0]
    ng_ = sg // LANES
    tm = _tile(t, 256, LANES)
    cb = col // sg

    def body(u_ref, v_ref, g_ref, lg_ref, lb_ref, w_ref, b_ref, y_ref):
        for n in range(tm // LANES):
            rs = slice(n * LANES, (n + 1) * LANES)
            for gi in range(ng_):
                sl = slice(gi * LANES, (gi + 1) * LANES)
                y_ref[rs, sl] = _sg_block(u_ref[rs, sl], v_ref[rs, sl], g_ref[rs, sl], lg_ref[:, sl], lb_ref[:, sl],
                                          w_ref[gi], b_ref[gi])

    vec = pl.BlockSpec((1, sg), lambda i: (0, 0))
    full = pl.BlockSpec((ng_, LANES, LANES), lambda i: (0, 0, 0))
    return pl.pallas_call(
        body, name="sg_fwd", grid=(t // tm,),
        in_specs=[pl.BlockSpec((tm, sg), lambda i: (i, cb)), pl.BlockSpec((tm, sg), lambda i: (i, cb + 1)),
                  pl.BlockSpec((tm, sg), lambda i: (i, cb + 2)), vec, vec, full, full],
        out_specs=pl.BlockSpec((tm, sg), lambda i: (i, 0)),
        out_shape=jax.ShapeDtypeStruct((t, sg), F32),
        compiler_params=_cp(("parallel",)),
    )(p, p, p, lg, lb, w, bias_b)


def sg_bwd(p, col, sg, lg, lb, w, bias_b, dy):
    t = p.shape[0]
    ng_ = sg // LANES
    tm = _tile(t, 256, LANES)
    cb = col // sg

    def body(u_ref, v_ref, g_ref, lg_ref, lb_ref, w_ref, b_ref, dy_ref, d_ref, dlg_ref, dlb_ref, dw_ref, db_ref):
        @pl.when(pl.program_id(0) == 0)
        def _():
            for r in (dlg_ref, dlb_ref, dw_ref, db_ref):
                r[...] = jnp.zeros_like(r)

        for n in range(tm // LANES):
            rs = slice(n * LANES, (n + 1) * LANES)
            for gi in range(ng_):
                sl = slice(gi * LANES, (gi + 1) * LANES)
                _, vj = jax.vjp(_sg_block, u_ref[rs, sl], v_ref[rs, sl], g_ref[rs, sl], lg_ref[:, sl], lb_ref[:, sl],
                                w_ref[gi], b_ref[gi])
                du, dv, dg, dlg, dlb, dw, db = vj(dy_ref[rs, sl])
                d_ref[rs, gi * LANES:(gi + 1) * LANES] = du.astype(BF16)
                d_ref[rs, sg + gi * LANES:sg + (gi + 1) * LANES] = dv.astype(BF16)
                d_ref[rs, 2 * sg + gi * LANES:2 * sg + (gi + 1) * LANES] = dg.astype(BF16)
                dlg_ref[:, sl] += dlg
                dlb_ref[:, sl] += dlb
                dw_ref[gi] += dw
                db_ref[gi] += jnp.broadcast_to(jnp.sum(db, axis=1, keepdims=True), (LANES, LANES))

    vec = pl.BlockSpec((1, sg), lambda i: (0, 0))
    full = pl.BlockSpec((ng_, LANES, LANES), lambda i: (0, 0, 0))
    return pl.pallas_call(
        body, name="sg_bwd", grid=(t // tm,),
        in_specs=[pl.BlockSpec((tm, sg), lambda i: (i, cb)), pl.BlockSpec((tm, sg), lambda i: (i, cb + 1)),
                  pl.BlockSpec((tm, sg), lambda i: (i, cb + 2)), vec, vec, full, full,
                  pl.BlockSpec((tm, sg), lambda i: (i, 0))],
        out_specs=[pl.BlockSpec((tm, 3 * sg), lambda i: (i, 0)), vec, vec, full, full],
        out_shape=[jax.ShapeDtypeStruct((t, 3 * sg), BF16), jax.ShapeDtypeStruct((1, sg), F32),
                   jax.ShapeDtypeStruct((1, sg), F32), jax.ShapeDtypeStruct((ng_, LANES, LANES), F32),
                   jax.ShapeDtypeStruct((ng_, LANES, LANES), F32)],
        compiler_params=_cp(("arbitrary",)),
    )(p, p, p, lg, lb, w, bias_b, dy)


def out_proj(x, y_dn, y_sg, y_cv, wo, gate):
    t, d = x.shape
    dn, sg, cv = y_dn.shape[1], y_sg.shape[1], y_cv.shape[1]
    dmix = dn + sg + cv
    tm = _tile(t, 256, LANES)

    def body(x_ref, a_ref, b_ref, c_ref, w_ref, g_ref, xn_ref, y_ref, yt_ref):
        a, b, c = a_ref[...], b_ref[...], c_ref[...]
        y = (jnp.dot(a.astype(BF16), w_ref[0:dn, :], preferred_element_type=F32)
             + jnp.dot(b.astype(BF16), w_ref[dn:dn + sg, :], preferred_element_type=F32)
             + jnp.dot(c.astype(BF16), w_ref[dn + sg:, :], preferred_element_type=F32))
        y_ref[...] = y
        xn_ref[...] = x_ref[...] + g_ref[...] * y
        yt_ref[0:dn, :] = a.T.astype(BF16)
        yt_ref[dn:dn + sg, :] = b.T.astype(BF16)
        yt_ref[dn + sg:, :] = c.T.astype(BF16)

    row = lambda w_: pl.BlockSpec((tm, w_), lambda i: (i, 0))
    return pl.pallas_call(
        body, name="out_proj", grid=(t // tm,),
        in_specs=[row(d), row(dn), row(sg), row(cv), pl.BlockSpec((dmix, d), lambda i: (0, 0)),
                  pl.BlockSpec((1, d), lambda i: (0, 0))],
        out_specs=[row(d), row(d), pl.BlockSpec((dmix, tm), lambda i: (0, i))],
        out_shape=[jax.ShapeDtypeStruct((t, d), F32), jax.ShapeDtypeStruct((t, d), F32),
                   jax.ShapeDtypeStruct((dmix, t), BF16)],
        compiler_params=_cp(("parallel",)),
    )(x, y_dn, y_sg, y_cv, wo, gate)


def out_proj_bwd(dxn, y, gate, wo, dn, sg, cv):
    t, d = dxn.shape
    dmix = dn + sg + cv
    tm = _tile(t, 256, LANES)

    def body(dx_ref, y_ref, g_ref, w_ref, da_ref, db_ref, dc_ref, dyb_ref, dg_ref):
        @pl.when(pl.program_id(0) == 0)
        def _():
            dg_ref[...] = jnp.zeros_like(dg_ref)
        dx = dx_ref[...]
        dg_ref[...] += jnp.sum(dx * y_ref[...], axis=0, keepdims=True)
        dyb = (dx * g_ref[...]).astype(BF16)
        dyb_ref[...] = dyb
        dcat = lax.dot_general(dyb, w_ref[...], (((1,), (1,)), ((), ())), preferred_element_type=F32)
        da_ref[...] = dcat[:, 0:dn]
        db_ref[...] = dcat[:, dn:dn + sg]
        dc_ref[...] = dcat[:, dn + sg:]

    row = lambda w_: pl.BlockSpec((tm, w_), lambda i: (i, 0))
    vec = pl.BlockSpec((1, d), lambda i: (0, 0))
    return pl.pallas_call(
        body, name="out_proj_bwd", grid=(t // tm,),
        in_specs=[row(d), row(d), vec, pl.BlockSpec((dmix, d), lambda i: (0, 0))],
        out_specs=[row(dn), row(sg), row(cv), row(d), vec],
        out_shape=[jax.ShapeDtypeStruct((t, dn), F32), jax.ShapeDtypeStruct((t, sg), F32),
                   jax.ShapeDtypeStruct((t, cv), F32), jax.ShapeDtypeStruct((t, d), BF16),
                   jax.ShapeDtypeStruct((1, d), F32)],
        compiler_params=_cp(("arbitrary",)),
    )(dxn, y, gate, wo)


def matmul_acc(name, at, b):
    m, t = at.shape
    n = b.shape[1]
    tn, tk = _tile(n, 512, LANES), _tile(t, 512, LANES)

    def body(a_ref, b_ref, o_ref):
        @pl.when(pl.program_id(1) == 0)
        def _():
            o_ref[...] = jnp.zeros_like(o_ref)
        o_ref[...] += jnp.dot(a_ref[...], b_ref[...], preferred_element_type=F32)

    return pl.pallas_call(
        body, name=name, grid=(n // tn, t // tk),
        in_specs=[pl.BlockSpec((m, tk), lambda j, k: (0, k)), pl.BlockSpec((tk, tn), lambda j, k: (k, j))],
        out_specs=pl.BlockSpec((m, tn), lambda j, k: (0, j)),
        out_shape=jax.ShapeDtypeStruct((m, n), F32),
        compiler_params=_cp(("parallel", "arbitrary")),
    )(at, b)


def in_proj_bwd(dp, wp, x, dxn, shift, scale, ng):
    t, d = x.shape
    npc = wp.shape[1]
    tm, tk = _tile(t, 512, LANES), _tile(npc, 384, LANES)
    nk = npc // tk

    def body(dp_ref, w_ref, x_ref, dxn_ref, sh_ref, sc_ref, g_ref, dx_ref, dg_ref, dsc_ref, dsh_ref, acc):
        i, kk = pl.program_id(0), pl.program_id(1)

        @pl.when((i == 0) & (kk == 0))
        def _():
            for r in (dg_ref, dsc_ref, dsh_ref):
                r[...] = jnp.zeros_like(r)

        @pl.when(kk == 0)
        def _():
            acc[...] = jnp.zeros_like(acc)
        acc[...] += lax.dot_general(dp_ref[...], w_ref[...], (((1,), (1,)), ((), ())), preferred_element_type=F32)

        @pl.when(kk == nk - 1)
        def _():
            _, vj = jax.vjp(_modnorm, x_ref[...], g_ref[...], sc_ref[...], sh_ref[...])
            dx, dg, dsc, dsh = vj(acc[...])
            dx_ref[...] = dxn_ref[...] + dx
            dg_ref[...] += dg
            dsc_ref[...] += dsc
            dsh_ref[...] += dsh

    vec = pl.BlockSpec((1, d), lambda i, k: (0, 0))
    row = pl.BlockSpec((tm, d), lambda i, k: (i, 0))
    return pl.pallas_call(
        body, name="in_proj_bwd", grid=(t // tm, nk),
        in_specs=[pl.BlockSpec((tm, tk), lambda i, k: (i, k)), pl.BlockSpec((d, tk), lambda i, k: (0, k)),
                  row, row, vec, vec, vec],
        out_specs=[row, vec, vec, vec],
        out_shape=[jax.ShapeDtypeStruct((t, d), F32)] + [jax.ShapeDtypeStruct((1, d), F32)] * 3,
        scratch_shapes=[pltpu.VMEM((tm, d), F32)],
        compiler_params=_cp(("arbitrary", "arbitrary")),
    )(dp, wp, x, dxn, shift, scale, ng)


def loss_head(x, tgt, fg):
    t, d = x.shape
    tm = _tile(t, 512, 8)

    def body(x_ref, t_ref, g_ref, l_ref, dx_ref, dg_ref):
        @pl.when(pl.program_id(0) == 0)
        def _():
            l_ref[...] = jnp.zeros_like(l_ref)
            dg_ref[...] = jnp.zeros_like(dg_ref)
        y, vj = jax.vjp(_rmsnorm, x_ref[...], g_ref[...])
        err = y - t_ref[...]
        part = 0.5 * jnp.sum(jnp.sum(err * err, axis=1, keepdims=True), axis=0, keepdims=True) / d
        l_ref[...] += jnp.broadcast_to(part, l_ref.shape)
        dx, dg = vj(err / d)
        dx_ref[...] = dx
        dg_ref[...] += dg

    row = pl.BlockSpec((tm, d), lambda i: (i, 0))
    vec = pl.BlockSpec((1, d), lambda i: (0, 0))
    return pl.pallas_call(
        body, name="loss_head", grid=(t // tm,), in_specs=[row, row, vec],
        out_specs=[pl.BlockSpec((1, LANES), lambda i: (0, 0)), row, vec],
        out_shape=[jax.ShapeDtypeStruct((1, LANES), F32), jax.ShapeDtypeStruct((t, d), F32),
                   jax.ShapeDtypeStruct((1, d), F32)],
        compiler_params=_cp(("arbitrary",)),
    )(x, tgt, fg)


def adamw(name, w, g, m, v):
    r, c = w.shape
    tr = _tile(r, 256, 8) if r % 8 == 0 else r

    def body(w_ref, g_ref, m_ref, v_ref, d_ref, mo_ref, vo_ref):
        d_ref[...], mo_ref[...], vo_ref[...] = _adamw_math(w_ref[...], g_ref[...], m_ref[...], v_ref[...])

    blk = pl.BlockSpec((tr, c), lambda i: (i, 0))
    return pl.pallas_call(
        body, name=name, grid=(r // tr,), in_specs=[blk] * 4, out_specs=[blk] * 3,
        out_shape=[jax.ShapeDtypeStruct((r, c), F32)] * 3, compiler_params=_cp(("parallel",)),
    )(w, g, m, v)


def ada_fwd(c_all, w_ada, b_loc):
    nl, d, cols = w_ada.shape
    nb = c_all.shape[0]
    tn = _tile(cols, 512, LANES)

    def body(c_ref, w_ref, b_ref, o_ref):
        ca = _silu(c_ref[...]).astype(BF16)
        o_ref[0] = jnp.dot(ca, w_ref[0].astype(BF16), preferred_element_type=F32) + b_ref[0]

    return pl.pallas_call(
        body, name="ada_fwd", grid=(nl, cols // tn),
        in_specs=[pl.BlockSpec((nb, d), lambda l, j: (0, 0)), pl.BlockSpec((1, d, tn), lambda l, j: (l, 0, j)),
                  pl.BlockSpec((1, 1, tn), lambda l, j: (l, 0, j))],
        out_specs=pl.BlockSpec((1, nb, tn), lambda l, j: (l, 0, j)),
        out_shape=jax.ShapeDtypeStruct((nl, nb, cols), F32),
        compiler_params=_cp(("parallel", "parallel")),
    )(c_all, w_ada, b_loc)


def ada_bwd(c_all_t, dmod_loc, w, m, v):
    nl, d, cols = w.shape
    nb = c_all_t.shape[1]
    tr = _tile(d, 256, 8)

    def body(c_ref, dm_ref, w_ref, m_ref, v_ref, g_ref, d_ref, mo_ref, vo_ref):
        ca = _silu(c_ref[...])
        dm = dm_ref[0]
        g = _lane_col(ca, 0) * dm[0:1, :]
        for b in range(1, nb):
            g = g + _lane_col(ca, b) * dm[b:b + 1, :]
        g_ref[0] = g
        d_ref[0], mo_ref[0], vo_ref[0] = _adamw_math(w_ref[0], g, m_ref[0], v_ref[0])

    blk = pl.BlockSpec((1, tr, cols), lambda l, i: (l, i, 0))
    return pl.pallas_call(
        body, name="ada_bwd", grid=(nl, d // tr),
        in_specs=[pl.BlockSpec((tr, nb), lambda l, i: (i, 0)), pl.BlockSpec((1, nb, cols), lambda l, i: (l, 0, 0)),
                  blk, blk, blk],
        out_specs=[blk] * 4, out_shape=[jax.ShapeDtypeStruct((nl, d, cols), F32)] * 4,
        compiler_params=_cp(("parallel", "parallel")),
    )(c_all_t, dmod_loc, w, m, v)


def sum8(g):
    _, r, c = g.shape
    tr = _tile(r, 256, 8)

    def body(g_ref, o_ref):
        acc = g_ref[0]
        for k in range(1, NDEV):
            acc = acc + g_ref[k]
        o_ref[...] = acc

    return pl.pallas_call(
        body, name="sum8", grid=(r // tr,), in_specs=[pl.BlockSpec((NDEV, tr, c), lambda i: (0, i, 0))],
        out_specs=pl.BlockSpec((tr, c), lambda i: (i, 0)), out_shape=jax.ShapeDtypeStruct((r, c), F32),
        compiler_params=_cp(("parallel",)),
    )(g)


def pair_sum(own, r1):
    ns, r, c = own.shape
    tr = _tile(r, 256, 8)

    def body(g_ref, r_ref, o_ref, ob_ref):
        s = g_ref[...] + r_ref[...]
        o_ref[...] = s
        ob_ref[...] = s.astype(BF16)

    blk = pl.BlockSpec((1, tr, c), lambda s, i: (s, i, 0))
    return pl.pallas_call(
        body, name="pair_sum", grid=(ns, r // tr), in_specs=[blk, blk], out_specs=[blk, blk],
        out_shape=[jax.ShapeDtypeStruct((ns, r, c), F32), jax.ShapeDtypeStruct((ns, r, c), BF16)],
        compiler_params=_cp(("parallel", "parallel")),
    )(own, r1)


def chip_sum(mine, r2):
    r, c = mine.shape
    tr = _tile(r, 256, 8)

    def body(m_ref, r2_ref, o_ref):
        acc = m_ref[...]
        for k in range(NCHIP - 1):
            acc = acc + r2_ref[k].astype(F32)
        o_ref[...] = acc

    return pl.pallas_call(
        body, name="chip_sum", grid=(r // tr,),
        in_specs=[pl.BlockSpec((tr, c), lambda i: (i, 0)), pl.BlockSpec((NCHIP - 1, tr, c), lambda i: (0, i, 0))],
        out_specs=pl.BlockSpec((tr, c), lambda i: (i, 0)),
        out_shape=jax.ShapeDtypeStruct((r, c), F32), compiler_params=_cp(("parallel",)),
    )(mine, r2)


def _me():
    return lax.axis_index("x"), lax.axis_index("y"), lax.axis_index("c")


_FLIPS = ((1, 0), (0, 1), (1, 1))


def all_gather8(v):
    m_per, n = v.shape

    def body(x_ref, out_ref, send_sems, recv_sems, local_sem):
        x, y, c = _me()
        me, sibling = (x, y, c), (x, y, 1 - c)
        chips = [(x ^ fx, y ^ fy) for fx, fy in _FLIPS]

        def rows(px, py, pc):
            return out_ref.at[pl.ds((4 * px + 2 * py + pc) * m_per, m_per), :]

        def copy(k, block, to, src=None):
            return pltpu.make_async_remote_copy(
                src_ref=rows(*block) if src is None else src, dst_ref=rows(*block),
                send_sem=send_sems.at[k], recv_sem=recv_sems.at[k], device_id=to, device_id_type=MESH)

        mine = pltpu.make_async_copy(x_ref, rows(*me), local_sem)
        mine.start()
        first = [copy(0, me, sibling, src=x_ref)]
        first += [copy(1 + j, me, (*chip, c), src=x_ref) for j, chip in enumerate(chips)]
        for cp in first:
            cp.start()
        passed = [copy(4 + j, (*chip, c), sibling) for j, chip in enumerate(chips)]
        for j, chip in enumerate(chips):
            copy(1 + j, (*chip, c), me).wait_recv()
            passed[j].start()
        copy(0, sibling, me).wait_recv()
        for j, chip in enumerate(chips):
            copy(4 + j, (*chip, 1 - c), me).wait_recv()
        for cp in first + passed:
            cp.wait_send()
        mine.wait()

    return pl.pallas_call(
        body, name="all_gather8", out_shape=jax.ShapeDtypeStruct((NDEV * m_per, n), v.dtype),
        in_specs=[pl.BlockSpec(memory_space=pltpu.VMEM)], out_specs=pl.BlockSpec(memory_space=pltpu.VMEM),
        scratch_shapes=[pltpu.SemaphoreType.DMA((7,)), pltpu.SemaphoreType.DMA((7,)), pltpu.SemaphoreType.DMA],
        compiler_params=pltpu.CompilerParams(vmem_limit_bytes=VMEM_LIMIT),
    )(v)


def gather_weights(ws):
    na = len(ws)

    def body(*refs):
        srcs, outs = refs[:na], refs[na:2 * na]
        send_sems, recv_sems, local_sems = refs[2 * na:]
        x, y, c = _me()
        chip = 2 * x + y
        sibling = (x, y, 1 - c)
        peers = [(x ^ fx, y ^ fy) for fx, fy in _FLIPS]
        local = [pltpu.make_async_copy(srcs[a], outs[a].at[chip], local_sems.at[a]) for a in range(na)]
        for cp in local:
            cp.start()

        def ici(a, j, half_of, to):
            return pltpu.make_async_remote_copy(
                src_ref=srcs[a].at[c], dst_ref=outs[a].at[half_of, c], send_sem=send_sems.at[a * 6 + j],
                recv_sem=recv_sems.at[a * 6 + j], device_id=to, device_id_type=MESH)

        def d2d(a, j, slot, half):
            return pltpu.make_async_remote_copy(
                src_ref=outs[a].at[slot, half], dst_ref=outs[a].at[slot, half], send_sem=send_sems.at[a * 6 + 3 + j],
                recv_sem=recv_sems.at[a * 6 + 3 + j], device_id=sibling, device_id_type=MESH)

        sends = []
        for a in range(na):
            for j, (px, py) in enumerate(peers):
                cp = ici(a, j, chip, (px, py, c))
                cp.start()
                sends.append(cp)
        for a in range(na):
            for j, (px, py) in enumerate(peers):
                ici(a, j, 2 * px + py, (px, py, c)).wait_recv()
                cp = d2d(a, j, 2 * px + py, c)
                cp.start()
                sends.append(cp)
        for a in range(na):
            for j, (px, py) in enumerate(peers):
                d2d(a, j, 2 * px + py, 1 - c).wait_recv()
        for cp in sends:
            cp.wait_send()
        for cp in local:
            cp.wait()

    return pl.pallas_call(
        body, name="gather_weights",
        out_shape=[jax.ShapeDtypeStruct((NCHIP,) + w.shape, w.dtype) for w in ws],
        in_specs=[ANY] * na, out_specs=[ANY] * na,
        scratch_shapes=[pltpu.SemaphoreType.DMA((6 * na,)), pltpu.SemaphoreType.DMA((6 * na,)),
                        pltpu.SemaphoreType.DMA((na,))],
    )(*ws)


def swap_halves(gs):
    na = len(gs)

    def body(*refs):
        srcs, owns, outs = refs[:na], refs[na:3 * na:2], refs[na + 1:3 * na:2]
        send_sems, recv_sems, local_sems = refs[3 * na:]
        x, y, c = _me()
        loc = [pltpu.make_async_copy(srcs[a].at[c], owns[a], local_sems.at[a]) for a in range(na)]
        cps = [pltpu.make_async_remote_copy(
            src_ref=srcs[a].at[1 - c], dst_ref=outs[a], send_sem=send_sems.at[a], recv_sem=recv_sems.at[a],
            device_id=(x, y, 1 - c), device_id_type=MESH) for a in range(na)]
        for cp in cps + loc:
            cp.start()
        for cp in cps + loc:
            cp.wait()

    half = [jax.ShapeDtypeStruct(g.shape[1:], g.dtype) for g in gs]
    return pl.pallas_call(
        body, name="swap_halves", out_shape=[s for h_ in half for s in (h_, h_)],
        in_specs=[ANY] * na, out_specs=[ANY] * (2 * na),
        scratch_shapes=[pltpu.SemaphoreType.DMA((na,)), pltpu.SemaphoreType.DMA((na,)), pltpu.SemaphoreType.DMA((na,))],
    )(*gs)


def chip_exchange(ps, fs):
    na = len(ps)

    def body(*refs):
        srcs, fsrc = refs[:na], refs[na:2 * na]
        outs, mine = refs[2 * na:4 * na:2], refs[2 * na + 1:4 * na:2]
        send_sems, recv_sems, local_sems = refs[4 * na:]
        x, y, c = _me()
        cps = []
        for a in range(na):
            for j, (fx, fy) in enumerate(_FLIPS):
                px, py = x ^ fx, y ^ fy
                cps.append(pltpu.make_async_remote_copy(
                    src_ref=srcs[a].at[2 * px + py], dst_ref=outs[a].at[j], send_sem=send_sems.at[a * 3 + j],
                    recv_sem=recv_sems.at[a * 3 + j], device_id=(px, py, c), device_id_type=MESH))
        cps += [pltpu.make_async_copy(fsrc[a].at[2 * x + y], mine[a], local_sems.at[a]) for a in range(na)]
        for cp in cps:
            cp.start()
        for cp in cps:
            cp.wait()

    shapes = []
    for p_, f_ in zip(ps, fs):
        shapes += [jax.ShapeDtypeStruct((NCHIP - 1,) + p_.shape[1:], p_.dtype), jax.ShapeDtypeStruct(f_.shape[1:], f_.dtype)]
    return pl.pallas_call(
        body, name="chip_exchange", out_shape=shapes, in_specs=[ANY] * (2 * na), out_specs=[ANY] * (2 * na),
        scratch_shapes=[pltpu.SemaphoreType.DMA((3 * na,)), pltpu.SemaphoreType.DMA((3 * na,)),
                        pltpu.SemaphoreType.DMA((na,))],
    )(*ps, *fs)


def join_halves(hs):
    na = len(hs)

    def body(*refs):
        srcs, outs = refs[:na], refs[na:2 * na]
        send_sems, recv_sems, local_sems = refs[2 * na:]
        x, y, c = _me()
        loc = [pltpu.make_async_copy(srcs[a], outs[a].at[c], local_sems.at[a]) for a in range(na)]
        cps = [pltpu.make_async_remote_copy(
            src_ref=srcs[a], dst_ref=outs[a].at[c], send_sem=send_sems.at[a], recv_sem=recv_sems.at[a],
            device_id=(x, y, 1 - c), device_id_type=MESH) for a in range(na)]
        for cp in loc + cps:
            cp.start()
        for cp in cps:
            cp.wait()
        for cp in loc:
            cp.wait()

    return pl.pallas_call(
        body, name="join_halves", out_shape=[jax.ShapeDtypeStruct((2,) + h.shape, h.dtype) for h in hs],
        in_specs=[ANY] * na, out_specs=[ANY] * na,
        scratch_shapes=[pltpu.SemaphoreType.DMA((na,)), pltpu.SemaphoreType.DMA((na,)), pltpu.SemaphoreType.DMA((na,))],
    )(*hs)


class _Cfg:
    def __init__(self, x, a_log, sg_w, cv_ln_g, cv_w, conv_qkv):
        self.t, self.d = x.shape[1], x.shape[2]
        self.nl, self.h = a_log.shape
        self.dn = self.h * LANES
        self.g = sg_w.shape[1]
        self.sg = self.g * LANES
        self.cv = cv_ln_g.shape[1]
        self.kc = cv_w.shape[1]
        self.k4 = conv_qkv.shape[1]
        self.o_z = 3 * self.dn
        self.o_sg = 4 * self.dn
        self.o_cv = self.o_sg + 3 * self.sg
        self.o_ba = self.o_cv + 3 * self.cv
        self.npc = self.o_ba + LANES
        self.d_in = self.o_ba + 2 * self.h
        self.dmix = self.dn + self.sg + self.cv
        self.hb_fwd = _tile(self.h, 8, 1)
        self.hb_bwd = _tile(self.h, 4, 1)


def _perm_cols(cfg, w):
    dn, h = cfg.dn, cfg.h
    lead = w.shape[:-1]
    qkv = w[..., :3 * dn].reshape(lead + (3, h, LANES))
    qkv = jnp.moveaxis(qkv, -3, -2).reshape(lead + (3 * dn,))
    ba = w[..., 4 * dn:4 * dn + 2 * h]
    pad = jnp.zeros(lead + (LANES - 2 * h,), w.dtype)
    return jnp.concatenate([qkv, w[..., 3 * dn:4 * dn], w[..., 4 * dn + 2 * h:], ba, pad], axis=-1)


def _unperm_cols(cfg, w):
    dn, h = cfg.dn, cfg.h
    lead = w.shape[:-1]
    qkv = w[..., :3 * dn].reshape(lead + (h, 3, LANES))
    qkv = jnp.moveaxis(qkv, -3, -2).reshape(lead + (3 * dn,))
    return jnp.concatenate([qkv, w[..., 3 * dn:4 * dn], w[..., cfg.o_ba:cfg.o_ba + 2 * h], w[..., 4 * dn:cfg.o_ba]], axis=-1)


def _layer_fwd(cfg, x, mod, lw):
    shift, scale, gate = mod
    p, ht = in_proj(x, shift, scale, lw["norm_g"], lw["wp"])
    qk_post = [_qk_post, _qk_post, _v_post]
    qkv = conv_fwd("dn_pre_fwd", cfg.k4, HALO4, lambda a: a, [(p, 0)], lw["conv_qkv"], qk_post, [], [],
                   3 * cfg.dn, 3 * LANES)
    wy = dn_wy_fwd(qkv, p, cfg.o_ba, lw["alog_b"], lw["dtb_b"], cfg.h)
    y_dn, ss = dn_seq_fwd(*wy, p, cfg.o_z, lw["dn_norm_g"], cfg.h, cfg.hb_fwd)
    y_sg = sg_fwd(p, cfg.o_sg, cfg.sg, lw["sg_ln_g"], lw["sg_ln_b"], lw["sg_w"], lw["sg_bias_b"])
    cv_post = [_cv_post] * (cfg.cv // LANES)
    y_cv = conv_fwd("cv_fwd", cfg.kc, HALO31, _glu, [(p, cfg.o_cv), (p, cfg.o_cv + cfg.cv)], lw["cv_w"], cv_post,
                    [(p, cfg.o_cv + 2 * cfg.cv)], [lw["cv_b"], lw["cv_ln_g"], lw["cv_ln_b"]], cfg.cv, cfg.cv)
    xn, y, yt = out_proj(x, y_dn, y_sg, y_cv, lw["wo"], gate)
    return xn, dict(x=x, p=p, ht=ht, qkv=qkv, wy=wy, ss=ss, y=y, yt=yt)


def _layer_bwd(cfg, dxn, mod, lw, sv):
    shift, scale, gate = mod
    p = sv["p"]
    d_dn, d_sg, d_cv, dyb, dgate = out_proj_bwd(dxn, sv["y"], gate, lw["wo"], cfg.dn, cfg.sg, cfg.cv)
    g_wo = matmul_acc("w_out_grad", sv["yt"], dyb)
    cv_post = [_cv_post] * (cfg.cv // LANES)
    dcv, g_cvw, (g_cvb, g_cvlg, g_cvlb) = conv_bwd(
        "cv_bwd", cfg.kc, HALO31, _glu, [(p, cfg.o_cv), (p, cfg.o_cv + cfg.cv)], lw["cv_w"], cv_post,
        [(p, cfg.o_cv + 2 * cfg.cv)], [lw["cv_b"], lw["cv_ln_g"], lw["cv_ln_b"]], d_cv, cfg.cv, cfg.cv, tm_pref=256)
    dsg, g_sglg, g_sglb, g_sgw, g_sgb = sg_bwd(p, cfg.o_sg, cfg.sg, lw["sg_ln_g"], lw["sg_ln_b"], lw["sg_w"],
                                               lw["sg_bias_b"], d_sg)
    *dwy, dz, g_dng = dn_seq_bwd(*sv["wy"], p, cfg.o_z, lw["dn_norm_g"], sv["ss"], d_dn, cfg.h, cfg.hb_bwd)
    dqkv, dba, g_al, g_dt = dn_wy_bwd(sv["qkv"], p, cfg.o_ba, lw["alog_b"], lw["dtb_b"], *dwy, cfg.h)
    qk_post = [_qk_post, _qk_post, _v_post]
    dqkv_pre, g_cq, _ = conv_bwd("dn_pre_bwd", cfg.k4, HALO4, lambda a: a, [(p, 0)], lw["conv_qkv"], qk_post, [], [],
                                 dqkv, 3 * cfg.dn, 3 * LANES)
    dp = jnp.concatenate([dqkv_pre, dz.astype(BF16), dsg, dcv, dba.astype(BF16)], axis=1)
    g_wp = matmul_acc("w_in_grad", sv["ht"], dp)
    dx, g_ng, dscale, dshift = in_proj_bwd(dp, lw["wp"], sv["x"], dxn, shift, scale, lw["norm_g"])
    grads = dict(norm_g=g_ng, conv_qkv=g_cq, a_log=g_al[:cfg.h, 0], dt_bias=g_dt[:cfg.h, 0], dn_norm_g=g_dng,
                 sg_ln_g=g_sglg, sg_ln_b=g_sglb, sg_w=g_sgw, sg_b=g_sgb[:, :, 0], cv_w=g_cvw, cv_b=g_cvb,
                 cv_ln_g=g_cvlg, cv_ln_b=g_cvlb, wp=g_wp, wo=g_wo)
    return dx, grads, (dshift, dscale, dgate)


def _local_step(cfg, xs, tgt, mods, lws, fg):
    nl = len(lws)
    saved = []
    for l in range(nl):
        xs, sv = _layer_fwd(cfg, xs, mods[l], lws[l])
        saved.append(sv)
    loss_b, dx, g_fg = loss_head(xs, tgt, fg)
    lg = [None] * nl
    dmods = [None] * nl
    for l in reversed(range(nl)):
        dx, lg[l], dmods[l] = _layer_bwd(cfg, dx, mods[l], lws[l], saved[l])
    return loss_b, dx, g_fg, lg, dmods


SMALL = ("norm_g", "conv_qkv", "a_log", "dt_bias", "dn_norm_g", "sg_ln_g", "sg_ln_b", "sg_w", "sg_b", "cv_w",
         "cv_b", "cv_ln_g", "cv_ln_b", "final_g", "b_ada")
PACK_N = 1024


def _pack(arrs):
    flat = jnp.concatenate([a.reshape(-1).astype(F32) for a in arrs])
    rows = -(-flat.shape[0] // PACK_N)
    rows = -(-rows // 8) * 8
    return jnp.pad(flat, (0, rows * PACK_N - flat.shape[0])).reshape(rows, PACK_N)


def _unpack(buf, shapes):
    flat = buf.reshape(-1)
    out, o = [], 0
    for s in shapes:
        n = 1
        for d_ in s:
            n *= d_
        out.append(flat[o:o + n].reshape(s))
        o += n
    return out


def kernel(x, c, norm_g, w_ada, b_ada, w_in, conv_qkv, a_log, dt_bias, dn_norm_g, sg_ln_g, sg_ln_b, sg_w, sg_b, cv_w, cv_b, cv_ln_g, cv_ln_b, w_out, final_g, loss_target, m_norm_g, m_w_ada, m_b_ada, m_w_in, m_conv_qkv, m_a_log, m_dt_bias, m_dn_norm_g, m_sg_ln_g, m_sg_ln_b, m_sg_w, m_sg_b, m_cv_w, m_cv_b, m_cv_ln_g, m_cv_ln_b, m_w_out, m_final_g, v_norm_g, v_w_ada, v_b_ada, v_w_in, v_conv_qkv, v_a_log, v_dt_bias, v_dn_norm_g, v_sg_ln_g, v_sg_ln_b, v_sg_w, v_sg_b, v_cv_w, v_cv_b, v_cv_ln_g, v_cv_ln_b, v_w_out, v_final_g):
    cfg = _Cfg(x, a_log, sg_w, cv_ln_g, cv_w, conv_qkv)
    nl, d, t, h = cfg.nl, cfg.d, cfg.t, cfg.h
    lh = nl // 2
    ax, ay, ac = _me()
    chip = 2 * ax + ay
    dev = 2 * chip + ac
    wts = dict(norm_g=norm_g, w_ada=w_ada, b_ada=b_ada, w_in=w_in, conv_qkv=conv_qkv, a_log=a_log, dt_bias=dt_bias,
               dn_norm_g=dn_norm_g, sg_ln_g=sg_ln_g, sg_ln_b=sg_ln_b, sg_w=sg_w, sg_b=sg_b, cv_w=cv_w, cv_b=cv_b,
               cv_ln_g=cv_ln_g, cv_ln_b=cv_ln_b, w_out=w_out, final_g=final_g)
    mom = dict(norm_g=m_norm_g, w_ada=m_w_ada, b_ada=m_b_ada, w_in=m_w_in, conv_qkv=m_conv_qkv, a_log=m_a_log,
               dt_bias=m_dt_bias, dn_norm_g=m_dn_norm_g, sg_ln_g=m_sg_ln_g, sg_ln_b=m_sg_ln_b, sg_w=m_sg_w,
               sg_b=m_sg_b, cv_w=m_cv_w, cv_b=m_cv_b, cv_ln_g=m_cv_ln_g, cv_ln_b=m_cv_ln_b, w_out=m_w_out,
               final_g=m_final_g)
    vel = dict(norm_g=v_norm_g, w_ada=v_w_ada, b_ada=v_b_ada, w_in=v_w_in, conv_qkv=v_conv_qkv, a_log=v_a_log,
               dt_bias=v_dt_bias, dn_norm_g=v_dn_norm_g, sg_ln_g=v_sg_ln_g, sg_ln_b=v_sg_ln_b, sg_w=v_sg_w,
               sg_b=v_sg_b, cv_w=v_cv_w, cv_b=v_cv_b, cv_ln_g=v_cv_ln_g, cv_ln_b=v_cv_ln_b, w_out=v_w_out,
               final_g=v_final_g)
    ada_cols = w_ada.shape[2]
    in_cols = w_in.shape[2]
    out_rows = w_out.shape[1]
    cq_cols = conv_qkv.shape[2]
    cvw_cols = cv_w.shape[2]

    c_all = all_gather8(jnp.pad(c, ((0, 7), (0, 0)))).reshape(NDEV, 8, d)[:, 0, :]
    b_loc = lax.dynamic_slice_in_dim(b_ada, chip * ada_cols, ada_cols, axis=1)[:, None, :]
    mod_part = ada_fwd(c_all, w_ada, b_loc)
    mod_all = all_gather8(mod_part.reshape(nl * NDEV, ada_cols)).reshape(NDEV, nl, NDEV, ada_cols)
    mod_me = lax.dynamic_index_in_dim(mod_all[0::2], dev, axis=2, keepdims=False)
    mod_me = jnp.moveaxis(mod_me, 0, 1).reshape(nl, 3, 1, d)

    win_all, wout_all = gather_weights([w_in.astype(BF16).reshape(2, lh, d, in_cols),
                                        w_out.astype(BF16).reshape(2, lh, out_rows, d)])
    win_full = jnp.moveaxis(win_all.reshape(NCHIP, nl, d, in_cols), 0, 2).reshape(nl, d, NCHIP * in_cols)
    wp_all = _perm_cols(cfg, win_full)
    wo_all = jnp.moveaxis(wout_all.reshape(NCHIP, nl, out_rows, d), 0, 1).reshape(nl, NCHIP * out_rows, d)

    cq_all = all_gather8(conv_qkv.reshape(nl * cfg.k4, cq_cols)).reshape(NDEV, nl, cfg.k4, cq_cols)[0::2]
    cq_full = jnp.moveaxis(cq_all, 0, 2).reshape(nl, cfg.k4, NCHIP * cq_cols)
    cq_perm = _perm_cols_qkv(cfg, cq_full)
    kcp = -(-cfg.kc // 8) * 8
    cvw_all = all_gather8(jnp.pad(cv_w, ((0, 0), (0, kcp - cfg.kc), (0, 0))).reshape(nl * kcp, cvw_cols))
    cvw_all = cvw_all.reshape(NDEV, nl, kcp, cvw_cols)[0::2]
    cvw_full = jnp.moveaxis(cvw_all, 0, 2).reshape(nl, kcp, NCHIP * cvw_cols)[:, :cfg.kc]

    hp = -(-h // 8) * 8
    lws = []
    for l in range(nl):
        lws.append(dict(
            norm_g=norm_g[l][None], wp=wp_all[l], wo=wo_all[l], conv_qkv=cq_perm[l],
            alog_b=jnp.pad(jnp.broadcast_to(a_log[l][:, None], (h, LANES)), ((0, hp - h), (0, 0))),
            dtb_b=jnp.pad(jnp.broadcast_to(dt_bias[l][:, None], (h, LANES)), ((0, hp - h), (0, 0))),
            dn_norm_g=dn_norm_g[l][None], sg_ln_g=sg_ln_g[l][None], sg_ln_b=sg_ln_b[l][None], sg_w=sg_w[l],
            sg_bias_b=jnp.broadcast_to(sg_b[l][:, :, None], (cfg.g, LANES, LANES)),
            cv_w=cvw_full[l], cv_b=cv_b[l][None], cv_ln_g=cv_ln_g[l][None], cv_ln_b=cv_ln_b[l][None]))

    mods = [(mod_me[l, 0], mod_me[l, 1], mod_me[l, 2]) for l in range(nl)]
    loss_b, dx, g_fg, lg, dmods = _local_step(cfg, x[0], loss_target[0], mods, lws, final_g[None])
    grad_x = dx[None]

    dmod = jnp.stack([jnp.concatenate(dm, axis=1)[0] for dm in dmods])
    stack = lambda k: jnp.stack([g_[k] for g_ in lg])
    small_local = [stack(k).reshape(wts_shape) for k, wts_shape in
                   (("norm_g", (nl, d)), ("conv_qkv", (nl, cfg.k4, 3 * cfg.dn)), ("a_log", (nl, h)),
                    ("dt_bias", (nl, h)), ("dn_norm_g", (nl, LANES)), ("sg_ln_g", (nl, cfg.sg)),
                    ("sg_ln_b", (nl, cfg.sg)), ("sg_w", (nl, cfg.g, LANES, LANES)), ("sg_b", (nl, cfg.g, LANES)),
                    ("cv_w", (nl, cfg.kc, cfg.cv)), ("cv_b", (nl, cfg.cv)), ("cv_ln_g", (nl, cfg.cv)),
                    ("cv_ln_b", (nl, cfg.cv)))]
    small_local[1] = _unperm_cols_qkv(cfg, small_local[1])
    small_local += [g_fg[0], dmod, loss_b[0, 0:1]]
    shapes = [a.shape for a in small_local]
    packed = _pack(small_local)
    rows = packed.shape[0]
    gathered = all_gather8(packed).reshape(NDEV, rows, PACK_N)
    summed = _unpack(sum8(gathered), shapes)
    sgrads = dict(zip(SMALL, summed[:15]))
    loss = summed[15][0]
    sgrads["conv_qkv"] = lax.dynamic_slice_in_dim(sgrads["conv_qkv"], chip * cq_cols, cq_cols, axis=2)
    sgrads["cv_w"] = lax.dynamic_slice_in_dim(sgrads["cv_w"], chip * cvw_cols, cvw_cols, axis=2)

    off = sum(math.prod(s) for s in shapes[:14])
    dmod_all = gathered.reshape(NDEV, rows * PACK_N)[:, off:off + nl * 3 * d].reshape(NDEV, nl, 3 * d)
    dmod_loc = jnp.moveaxis(lax.dynamic_slice_in_dim(dmod_all, chip * ada_cols, ada_cols, axis=2), 0, 1)
    g_wada, d_wada, nm_wada, nv_wada = ada_bwd(c_all.T, dmod_loc, w_ada, m_w_ada, v_w_ada)

    g_wp = _unperm_cols(cfg, jnp.stack([g_["wp"] for g_ in lg]))
    g_in = jnp.moveaxis(g_wp.reshape(2, lh, d, NCHIP, in_cols), 3, 1).reshape(2, NCHIP, lh * d, in_cols)
    g_wo = jnp.stack([g_["wo"] for g_ in lg]).reshape(2, lh, NCHIP, out_rows, d)
    g_out = jnp.moveaxis(g_wo, 2, 1).reshape(2, NCHIP, lh * out_rows, d)
    own_in, r1_in, own_out, r1_out = swap_halves([g_in, g_out])
    (f_in, p_in), (f_out, p_out) = pair_sum(own_in, r1_in), pair_sum(own_out, r1_out)
    r2_in, mine_in, r2_out, mine_out = chip_exchange([p_in, p_out], [f_in, f_out])
    h_in, h_out = chip_sum(mine_in, r2_in), chip_sum(mine_out, r2_out)
    j_in, j_out = join_halves([h_in, h_out])
    grad_w_in = j_in.reshape(nl, d, in_cols)
    grad_w_out = j_out.reshape(nl, out_rows, d)

    d_in_, nm_in, nv_in = adamw("adamw_w_in", w_in.reshape(nl * d, in_cols), grad_w_in.reshape(nl * d, in_cols),
                                m_w_in.reshape(nl * d, in_cols), v_w_in.reshape(nl * d, in_cols))
    d_out_, nm_out, nv_out = adamw("adamw_w_out", w_out.reshape(nl * out_rows, d), grad_w_out.reshape(nl * out_rows, d),
                                   m_w_out.reshape(nl * out_rows, d), v_w_out.reshape(nl * out_rows, d))
    sshapes = [wts[k].shape for k in SMALL]
    pk = lambda dct: _pack([dct[k] for k in SMALL])
    d_s, m_s, v_s = adamw("adamw_small", pk(wts), pk(sgrads), pk(mom), pk(vel))
    d_small = dict(zip(SMALL, _unpack(d_s, sshapes)))
    m_small = dict(zip(SMALL, _unpack(m_s, sshapes)))
    v_small = dict(zip(SMALL, _unpack(v_s, sshapes)))

    grads = dict(sgrads, w_ada=g_wada, w_in=grad_w_in, w_out=grad_w_out)
    deltas = dict(d_small, w_ada=d_wada, w_in=d_in_.reshape(w_in.shape), w_out=d_out_.reshape(w_out.shape))
    new_m = dict(m_small, w_ada=nm_wada, w_in=nm_in.reshape(w_in.shape), w_out=nm_out.reshape(w_out.shape))
    new_v = dict(v_small, w_ada=nv_wada, w_in=nv_in.reshape(w_in.shape), w_out=nv_out.reshape(w_out.shape))
    order = ("norm_g", "w_ada", "b_ada", "w_in", "conv_qkv", "a_log", "dt_bias", "dn_norm_g", "sg_ln_g", "sg_ln_b",
             "sg_w", "sg_b", "cv_w", "cv_b", "cv_ln_g", "cv_ln_b", "w_out", "final_g")
    return (loss, grad_x, *[grads[k] for k in order], *[deltas[k] for k in order], *[new_m[k] for k in order],
            *[new_v[k] for k in order])


def _perm_cols_qkv(cfg, w):
    lead = w.shape[:-1]
    return jnp.moveaxis(w.reshape(lead + (3, cfg.h, LANES)), -3, -2).reshape(lead + (3 * cfg.dn,))


def _unperm_cols_qkv(cfg, w):
    lead = w.shape[:-1]
    return jnp.moveaxis(w.reshape(lead + (cfg.h, 3, LANES)), -3, -2).reshape(lead + (3 * cfg.dn,))
```

```python
import functools
import math

import jax
import jax.numpy as jnp
from jax import lax
from jax.experimental import pallas as pl
from jax.experimental.pallas import tpu as pltpu

F32 = jnp.float32
BF16 = jnp.bfloat16
EPS = 1e-6
LN_EPS = 1e-5
LANES = 128
CHUNK = 64
HALO4 = 8
HALO31 = 32
NCHIP = 4
NDEV = 8
VMEM_LIMIT = 56 * 2 ** 20
ADAM_LR, ADAM_B1, ADAM_B2, ADAM_EPS, ADAM_WD, ADAM_STEP = 0.001, 0.9, 0.999, 1e-08, 0.01, 10
MESH = pl.DeviceIdType.MESH
ANY = pl.BlockSpec(memory_space=pl.ANY)


def _cp(sem=None, vmem=VMEM_LIMIT):
    return pltpu.CompilerParams(dimension_semantics=sem, vmem_limit_bytes=vmem)


def _tile(n, pref, mult):
    t = min(n, pref) // mult * mult
    while t > 0 and n % t:
        t -= mult
    return t if t > 0 else n


def _split(a):
    hi = a.astype(BF16)
    return hi, (a - hi.astype(F32)).astype(BF16)


def _raw_dot(a, b, ca, cb, hi):
    dn = (((ca,), (cb,)), ((), ()))
    if hi:
        ah, al = _split(a.astype(F32))
        bh, bl = _split(b.astype(F32))
        d3 = lambda x, y: lax.dot_general(x, y, dn, preferred_element_type=F32)
        return d3(ah, bh) + (d3(al, bh) + d3(ah, bl))
    return lax.dot_general(a.astype(BF16), b.astype(BF16), dn, preferred_element_type=F32)


@functools.partial(jax.custom_vjp, nondiff_argnums=(2, 3, 4))
def bdot(a, b, ca, cb, hi):
    return _raw_dot(a, b, ca, cb, hi)


def _bdot_fwd(a, b, ca, cb, hi):
    return _raw_dot(a, b, ca, cb, hi), (a, b)


def _bdot_bwd(ca, cb, hi, res, ct):
    a, b = res
    fa, fb = 1 - ca, 1 - cb
    da = _raw_dot(ct, b, 1, fb, hi) if ca == 1 else _raw_dot(b, ct, fb, 1, hi)
    db = _raw_dot(a, ct, fa, 0, hi) if cb == 0 else _raw_dot(ct, a, 0, fa, hi)
    return da.astype(a.dtype), db.astype(b.dtype)


bdot.defvjp(_bdot_fwd, _bdot_bwd)


def _sigmoid(x):
    return 1.0 / (1.0 + jnp.exp(-x))


def _silu(x):
    return x * _sigmoid(x)


def _gelu(x):
    return 0.5 * x * (1.0 + lax.erf(x * (2.0 ** -0.5)))


def _softplus(x):
    return jnp.maximum(x, 0.0) + jnp.log(1.0 + jnp.exp(-jnp.abs(x)))


def _modnorm(x, g, scale, shift):
    y = x * lax.rsqrt(jnp.mean(x * x, axis=-1, keepdims=True) + EPS)
    return (y * g) * (1.0 + scale) + shift


def _rmsnorm(x, g):
    return x * lax.rsqrt(jnp.mean(x * x, axis=-1, keepdims=True) + EPS) * g


def _layernorm(x, g, b):
    mu = jnp.mean(x, axis=-1, keepdims=True)
    xc = x - mu
    var = jnp.mean(xc * xc, axis=-1, keepdims=True)
    return xc * lax.rsqrt(var + LN_EPS) * g + b


def _l2norm(t):
    return t * lax.rsqrt(jnp.sum(t * t, axis=-1, keepdims=True) + EPS)


def _adamw_math(w, g, m, v):
    mn = ADAM_B1 * m + (1.0 - ADAM_B1) * g
    vn = ADAM_B2 * v + (1.0 - ADAM_B2) * (g * g)
    mh = mn / (1.0 - ADAM_B1 ** ADAM_STEP)
    vh = vn / (1.0 - ADAM_B2 ** ADAM_STEP)
    delta = -ADAM_LR * (mh / (jnp.sqrt(vh) + ADAM_EPS) + ADAM_WD * w)
    return delta, mn, vn


def _each(f, *lists):
    return [f(*xs) for xs in zip(*lists)]


def _wy(q, k, v, bcol, acol, alog, dtb):
    c = CHUNK
    r = lax.broadcasted_iota(jnp.int32, (c, c), 0)
    cc = lax.broadcasted_iota(jnp.int32, (c, c), 1)
    rr = lax.broadcasted_iota(jnp.int32, (c, 1), 0)
    tri_incl, tri_strict, eye = r >= cc, r > cc, r == cc
    beta = _each(_sigmoid, bcol)
    g = _each(lambda al, a_, dt: -jnp.exp(al) * _softplus(a_ + dt), alog, acol, dtb)
    gb = [jnp.broadcast_to(g_, (c, c)) for g_ in g]
    g_row = [jnp.sum(jnp.where(eye, b_, 0.0), axis=0, keepdims=True) for b_ in gb]
    gc_col = [jnp.sum(jnp.where(tri_incl, jnp.broadcast_to(gr, (c, c)), 0.0), axis=1, keepdims=True) for gr in g_row]
    gc_row = [jnp.sum(jnp.where(r <= cc, b_, 0.0), axis=0, keepdims=True) for b_ in gb]
    decay = _each(lambda gcc, gcr: jnp.where(tri_incl, jnp.exp(jnp.where(tri_incl, gcc - gcr, 0.0)), 0.0),
                  gc_col, gc_row)
    qs = [q_ * (q_.shape[-1] ** -0.5) for q_ in q]
    kb = _each(lambda k_, b_: k_ * b_, k, beta)
    a = _each(lambda kb_, k_, d_: jnp.where(tri_strict, bdot(kb_, k_, 1, 1, False) * d_, 0.0), kb, k, decay)
    dv = v[0].shape[-1]
    x = _each(lambda v_, b_, kb_, gcc: jnp.concatenate([v_ * b_, kb_ * jnp.exp(gcc)], axis=1), v, beta, kb, gc_col)
    x = _each(lambda a_, x_: x_ - bdot(a_, x_, 1, 0, True), a, x)
    p = a
    for _ in range(5):
        p = _each(lambda p_: bdot(p_, p_, 1, 0, True), p)
        x = _each(lambda p_, x_: x_ + bdot(p_, x_, 1, 0, True), p, x)
    xv = [x_[:, :dv] for x_ in x]
    xk = [x_[:, dv:] for x_ in x]
    qk = _each(lambda q_, k_, d_: bdot(q_, k_, 1, 1, False) * d_, qs, k, decay)
    g_last = [jnp.sum(jnp.where(rr == c - 1, gcc, 0.0), axis=0, keepdims=True) for gcc in gc_col]
    qg = _each(lambda q_, gcc: q_ * jnp.exp(gcc), qs, gc_col)
    kd = _each(lambda k_, gl, gcc: k_ * jnp.exp(gl - gcc), k, g_last, gc_col)
    return xv, xk, qg, kd, qk, [jnp.exp(gl) for gl in g_last]


def _seq(u, w, qg, kd, qk, e, z, s, ng):
    v_new = _each(lambda u_, w_, s_: u_ - bdot(w_, s_, 1, 0, False), u, w, s)
    o1 = _each(lambda q_, s_: bdot(q_, s_, 1, 0, False), qg, s)
    o2 = _each(lambda qk_, vn: bdot(qk_, vn, 1, 0, False), qk, v_new)
    ds = _each(lambda kd_, vn: bdot(kd_, vn, 0, 0, False), kd, v_new)
    s_next = _each(lambda s_, e_, d_: s_ * e_ + d_, s, e, ds)
    y = _each(lambda a_, b_, z_: _rmsnorm(a_ + b_, ng) * _silu(z_), o1, o2, z)
    return y, s_next


def _sg_block(u, v, gt, lg, lb, w, bias):
    n = w.shape[0]
    pr = lax.broadcasted_iota(jnp.int32, (n, n), 0) // CHUNK
    pc = lax.broadcasted_iota(jnp.int32, (n, n), 1) // CHUNK
    wm = jnp.where(pr >= pc, w, 0.0)
    vl = _layernorm(_gelu(v), lg, lb)
    mixed = bdot(wm, vl, 1, 0, False) + bias
    return _gelu(u) * mixed * _silu(gt)


def _glu(a, b):
    return a * _sigmoid(b)


def _cv_post(conv, gate, cb, lg, lb):
    return _silu(_layernorm(conv + cb, lg, lb)) * _silu(gate)


def _qk_post(conv):
    return _l2norm(_silu(conv))


def _v_post(conv):
    return _silu(conv)


def in_proj(x, shift, scale, ng, wp):
    t, d = x.shape
    npc = wp.shape[1]
    tm, tn = _tile(t, 1024, LANES), _tile(npc, 384, LANES)

    def body(x_ref, sh_ref, sc_ref, g_ref, w_ref, p_ref, ht_ref, h_scr):
        @pl.when(pl.program_id(1) == 0)
        def _():
            h = _modnorm(x_ref[...], g_ref[...], sc_ref[...], sh_ref[...])
            h_scr[...] = h.astype(BF16)
            ht_ref[...] = h.T.astype(BF16)
        p_ref[...] = jnp.dot(h_scr[...], w_ref[...], preferred_element_type=F32)

    vec = pl.BlockSpec((1, d), lambda i, j: (0, 0))
    return pl.pallas_call(
        body, name="in_proj", grid=(t // tm, npc // tn),
        in_specs=[pl.BlockSpec((tm, d), lambda i, j: (i, 0)), vec, vec, vec,
                  pl.BlockSpec((d, tn), lambda i, j: (0, j))],
        out_specs=[pl.BlockSpec((tm, tn), lambda i, j: (i, j)), pl.BlockSpec((d, tm), lambda i, j: (0, i))],
        out_shape=[jax.ShapeDtypeStruct((t, npc), F32), jax.ShapeDtypeStruct((d, t), BF16)],
        scratch_shapes=[pltpu.VMEM((tm, d), BF16)],
        compiler_params=_cp(("parallel", "arbitrary")),
    )(x, shift, scale, ng, wp)


def conv_fwd(name, k, halo, pre_fn, pre, w, post_fns, extras, params, c_total, tc, tm_pref=512):
    t = pre[0][0].shape[0]
    tm = _tile(t, tm_pref, halo)
    npre, nex, npar = len(pre), len(extras), len(params)
    ngr = tc // LANES

    def body(*refs):
        prev = refs[:npre]
        cur = refs[npre:2 * npre]
        w_ref = refs[2 * npre]
        ex = refs[2 * npre + 1:2 * npre + 1 + nex]
        par = refs[2 * npre + 1 + nex:2 * npre + 1 + nex + npar]
        out_ref, buf = refs[-2], refs[-1]
        i = pl.program_id(1)
        pv = pre_fn(*[r[...] for r in prev])
        buf[0:halo, :] = jnp.where(i > 0, pv, 0.0)
        buf[halo:, :] = pre_fn(*[r[...] for r in cur])
        acc = w_ref[0:1, :] * buf[pl.ds(halo - (k - 1), tm), :]
        for j in range(1, k):
            acc = acc + w_ref[j:j + 1, :] * buf[pl.ds(halo - (k - 1) + j, tm), :]
        for gi in range(ngr):
            sl = slice(gi * LANES, (gi + 1) * LANES)
            out_ref[:, sl] = post_fns[gi](acc[:, sl], *[e[:, sl] for e in ex], *[p_[:, sl] for p_ in par])

    hb = tm // halo
    in_specs = ([pl.BlockSpec((halo, tc), functools.partial(lambda j, i, o: (jnp.maximum(i * hb - 1, 0), o + j), o=col // tc))
                 for _, col in pre]
                + [pl.BlockSpec((tm, tc), functools.partial(lambda j, i, o: (i, o + j), o=col // tc)) for _, col in pre]
                + [pl.BlockSpec((k, tc), lambda j, i: (0, j))]
                + [pl.BlockSpec((tm, tc), functools.partial(lambda j, i, o: (i, o + j), o=col // tc)) for _, col in extras]
                + [pl.BlockSpec((1, tc), lambda j, i: (0, j)) for _ in params])
    args = [a for a, _ in pre] * 2 + [w] + [a for a, _ in extras] + list(params)
    return pl.pallas_call(
        body, name=name, grid=(c_total // tc, t // tm), in_specs=in_specs,
        out_specs=pl.BlockSpec((tm, tc), lambda j, i: (i, j)),
        out_shape=jax.ShapeDtypeStruct((t, c_total), F32),
        scratch_shapes=[pltpu.VMEM((halo + tm, tc), F32)],
        compiler_params=_cp(("parallel", "arbitrary")),
    )(*args)


def conv_bwd(name, k, halo, pre_fn, pre, w, post_fns, extras, params, dout, c_total, tc, tm_pref=512):
    t = pre[0][0].shape[0]
    tm = _tile(t, tm_pref, halo)
    npre, nex, npar = len(pre), len(extras), len(params)
    ngr = tc // LANES
    nout = npre + nex
    assert nout == 1 or c_total == tc
    nblk = t // tm

    def body(*refs):
        it = iter(refs)
        prev = [next(it) for _ in range(npre)]
        cur = [next(it) for _ in range(npre)]
        nxt = [next(it) for _ in range(npre)]
        w_ref = next(it)
        ex_c = [next(it) for _ in range(nex)]
        ex_n = [next(it) for _ in range(nex)]
        par = [next(it) for _ in range(npar)]
        do_c, do_n = next(it), next(it)
        din_ref, dw_ref = next(it), next(it)
        dpar = [next(it) for _ in range(npar)]
        buf, dbuf = next(it), next(it)
        i = pl.program_id(1)

        @pl.when(i == 0)
        def _():
            dw_ref[...] = jnp.zeros_like(dw_ref)
            for r in dpar:
                r[...] = jnp.zeros_like(r)

        buf[0:halo, :] = jnp.where(i > 0, pre_fn(*[r[...] for r in prev]), 0.0)
        cur_vals = [r[...] for r in cur]
        buf[halo:halo + tm, :] = pre_fn(*cur_vals)
        buf[halo + tm:, :] = pre_fn(*[r[...] for r in nxt])
        ext = tm + halo
        conv = w_ref[0:1, :] * buf[pl.ds(halo - (k - 1), ext), :]
        for j in range(1, k):
            conv = conv + w_ref[j:j + 1, :] * buf[pl.ds(halo - (k - 1) + j, ext), :]
        don = jnp.where(i < nblk - 1, do_n[...], 0.0)
        for gi in range(ngr):
            sl = slice(gi * LANES, (gi + 1) * LANES)
            pv = [p_[:, sl] for p_ in par]
            _, vj = jax.vjp(post_fns[gi], conv[:tm, sl], *[e[:, sl] for e in ex_c], *pv)
            gr = vj(do_c[:, sl])
            dbuf[0:tm, sl] = gr[0]
            for e in range(nex):
                din_ref[:, (npre + e) * tc + gi * LANES:(npre + e) * tc + (gi + 1) * LANES] = gr[1 + e].astype(din_ref.dtype)
            for q_ in range(npar):
                dpar[q_][:, sl] += gr[1 + nex + q_]
            _, vjn = jax.vjp(post_fns[gi], conv[tm:, sl], *[e[:, sl] for e in ex_n], *pv)
            dbuf[tm:, sl] = vjn(don[:, sl])[0]
        dcur = dbuf[0:tm, :]
        dpre = w_ref[0:1, :] * dbuf[pl.ds(k - 1, tm), :]
        dw_ref[0:1, :] += jnp.sum(dcur * buf[pl.ds(halo - (k - 1), tm), :], axis=0, keepdims=True)
        for j in range(1, k):
            dpre = dpre + w_ref[j:j + 1, :] * dbuf[pl.ds(k - 1 - j, tm), :]
            dw_ref[j:j + 1, :] += jnp.sum(dcur * buf[pl.ds(halo - (k - 1) + j, tm), :], axis=0, keepdims=True)
        _, vjp_pre = jax.vjp(pre_fn, *cur_vals)
        for e, gval in enumerate(vjp_pre(dpre)):
            din_ref[:, e * tc:(e + 1) * tc] = gval.astype(din_ref.dtype)

    hb = tm // halo
    last_h = t // halo - 1

    def spec(kind, col):
        o = col // tc
        if kind == "prev":
            return pl.BlockSpec((halo, tc), lambda j, i: (jnp.maximum(i * hb - 1, 0), o + j))
        if kind == "next":
            return pl.BlockSpec((halo, tc), lambda j, i: (jnp.minimum((i + 1) * hb, last_h), o + j))
        return pl.BlockSpec((tm, tc), lambda j, i: (i, o + j))

    in_specs = ([spec("prev", col) for _, col in pre] + [spec("cur", col) for _, col in pre]
                + [spec("next", col) for _, col in pre] + [pl.BlockSpec((k, tc), lambda j, i: (0, j))]
                + [spec("cur", col) for _, col in extras] + [spec("next", col) for _, col in extras]
                + [pl.BlockSpec((1, tc), lambda j, i: (0, j)) for _ in params]
                + [spec("cur", 0), spec("next", 0)])
    args = [a for a, _ in pre] * 3 + [w] + [a for a, _ in extras] * 2 + list(params) + [dout, dout]
    out = pl.pallas_call(
        body, name=name, grid=(c_total // tc, nblk), in_specs=in_specs,
        out_specs=[pl.BlockSpec((tm, nout * tc), lambda j, i: (i, j)), pl.BlockSpec((k, tc), lambda j, i: (0, j))]
        + [pl.BlockSpec((1, tc), lambda j, i: (0, j)) for _ in params],
        out_shape=[jax.ShapeDtypeStruct((t, nout * c_total), BF16), jax.ShapeDtypeStruct((k, c_total), F32)]
        + [jax.ShapeDtypeStruct((1, c_total), F32) for _ in params],
        scratch_shapes=[pltpu.VMEM((2 * halo + tm, tc), F32), pltpu.VMEM((halo + tm, tc), F32)],
        compiler_params=_cp(("parallel", "arbitrary")),
    )(*args)
    return out[0], out[1], out[2:]


def _head_pick(ref_val, row):
    rr = lax.broadcasted_iota(jnp.int32, ref_val.shape, 0)
    v = jnp.sum(jnp.where(rr == row, ref_val, 0.0), axis=0, keepdims=True)
    ll = lax.broadcasted_iota(jnp.int32, v.shape, 1)
    return jnp.sum(jnp.where(ll == 0, v, 0.0), axis=1, keepdims=True)


def _lane_col(blk, lane_idx):
    ll = lax.broadcasted_iota(jnp.int32, blk.shape, 1)
    return jnp.sum(jnp.where(ll == lane_idx, blk, 0.0), axis=1, keepdims=True)


WY_HEADS = 2
WY_UNROLL = 4


def dn_wy_fwd(qkv, p, ba_col, alog_b, dtb_b, nheads):
    t = qkv.shape[0]
    tm = _tile(t, 512, CHUNK * WY_UNROLL)
    nc = tm // CHUNK
    hb = WY_HEADS
    hp = alog_b.shape[0]
    w_ = hb * LANES

    def body(qkv_ref, ba_ref, al_ref, dt_ref, u_ref, w_ref, qg_ref, kd_ref, qk_ref, e_ref):
        hblk = pl.program_id(1)
        alv, dtv = al_ref[...], dt_ref[...]

        def trip(cj, carry):
            units = [(cj * WY_UNROLL + cu, hl) for cu in range(WY_UNROLL) for hl in range(hb)]
            args = [[] for _ in range(7)]
            for ci, hl in units:
                rows = pl.ds(pl.multiple_of(ci * CHUNK, CHUNK), CHUNK)
                ba = ba_ref[rows, :]
                h = hblk * hb + hl
                for lst, val in zip(args, (qkv_ref[rows, hl * 384:hl * 384 + 128],
                                           qkv_ref[rows, hl * 384 + 128:hl * 384 + 256],
                                           qkv_ref[rows, hl * 384 + 256:hl * 384 + 384],
                                           _lane_col(ba, h), _lane_col(ba, nheads + h),
                                           _head_pick(alv, h), _head_pick(dtv, h))):
                    lst.append(val)
            outs = _wy(*args)
            for n, (ci, hl) in enumerate(units):
                rows = pl.ds(pl.multiple_of(ci * CHUNK, CHUNK), CHUNK)
                u, w, qg, kd, qk, e = [o[n] for o in outs]
                sl = slice(hl * LANES, (hl + 1) * LANES)
                u_ref[rows, sl] = u
                w_ref[rows, sl] = w.astype(BF16)
                qg_ref[rows, sl] = qg.astype(BF16)
                kd_ref[rows, sl] = kd.astype(BF16)
                qk_ref[rows, hl * LANES:hl * LANES + CHUNK] = qk.astype(BF16)
                qk_ref[rows, hl * LANES + CHUNK:(hl + 1) * LANES] = jnp.zeros((CHUNK, LANES - CHUNK), BF16)
                e_ref[ci, :, sl] = jnp.broadcast_to(e, (1, LANES))
            return carry

        lax.fori_loop(0, nc // WY_UNROLL, trip, 0)

    bc = ba_col // LANES
    blk = pl.BlockSpec((tm, w_), lambda i, h: (i, h))
    tab = pl.BlockSpec((hp, LANES), lambda i, h: (0, 0))
    wide = lambda dt: jax.ShapeDtypeStruct((t, nheads * LANES), dt)
    return pl.pallas_call(
        body, name="dn_wy_fwd", grid=(t // tm, nheads // hb),
        in_specs=[pl.BlockSpec((tm, hb * 384), lambda i, h: (i, h)), pl.BlockSpec((tm, LANES), lambda i, h: (i, bc)),
                  tab, tab],
        out_specs=[blk] * 5 + [pl.BlockSpec((nc, 1, w_), lambda i, h: (i, 0, h))],
        out_shape=[wide(F32), wide(BF16), wide(BF16), wide(BF16), wide(BF16),
                   jax.ShapeDtypeStruct((t // CHUNK, 1, nheads * LANES), F32)],
        compiler_params=_cp(("parallel", "parallel")),
    )(qkv, p, alog_b, dtb_b)


def dn_seq_fwd(u, w, qg, kd, qk, e, p, z_col, ng, nheads, hb):
    t = u.shape[0]
    tm = _tile(t, 512, CHUNK)
    nc = tm // CHUNK
    w_ = hb * LANES

    def body(u_ref, w_ref, qg_ref, kd_ref, qk_ref, e_ref, z_ref, ng_ref, y_ref, ss_ref, s_scr):
        i, hblk = pl.program_id(0), pl.program_id(1)
        for hl in range(hb):
            @pl.when(i == 0)
            def _():
                s_scr[hblk * hb + hl] = jnp.zeros((LANES, LANES), F32)
        ngv = ng_ref[...]

        def chunk(ci, carry):
            rows = pl.ds(pl.multiple_of(ci * CHUNK, CHUNK), CHUNK)
            ev = e_ref[ci]
            sls = [slice(hl * LANES, (hl + 1) * LANES) for hl in range(hb)]
            s = [s_scr[hblk * hb + hl] for hl in range(hb)]
            for hl in range(hb):
                ss_ref[ci, sls[hl], :] = s[hl]
            y, sn = _seq([u_ref[rows, sl] for sl in sls], [w_ref[rows, sl].astype(F32) for sl in sls],
                         [qg_ref[rows, sl].astype(F32) for sl in sls], [kd_ref[rows, sl].astype(F32) for sl in sls],
                         [qk_ref[rows, sl][:, :CHUNK].astype(F32) for sl in sls], [ev[:, sl] for sl in sls],
                         [z_ref[rows, sl] for sl in sls], s, ngv)
            for hl in range(hb):
                y_ref[rows, sls[hl]] = y[hl]
                s_scr[hblk * hb + hl] = sn[hl]
            return carry

        lax.fori_loop(0, nc, chunk, 0)

    zc = z_col // w_
    blk = pl.BlockSpec((tm, w_), lambda i, h: (i, h))
    return pl.pallas_call(
        body, name="dn_seq_fwd", grid=(t // tm, nheads // hb),
        in_specs=[blk] * 5 + [pl.BlockSpec((nc, 1, w_), lambda i, h: (i, 0, h)),
                              pl.BlockSpec((tm, w_), lambda i, h: (i, zc + h)),
                              pl.BlockSpec((1, LANES), lambda i, h: (0, 0))],
        out_specs=[blk, pl.BlockSpec((nc, w_, LANES), lambda i, h: (i, h, 0))],
        out_shape=[jax.ShapeDtypeStruct((t, nheads * LANES), F32),
                   jax.ShapeDtypeStruct((t // CHUNK, nheads * LANES, LANES), F32)],
        scratch_shapes=[pltpu.VMEM((nheads, LANES, LANES), F32)],
        compiler_params=_cp(("arbitrary", "arbitrary")),
    )(u, w, qg, kd, qk, e, p, ng)


def dn_seq_bwd(u, w, qg, kd, qk, e, p, z_col, ng, ss, dy, nheads, hb):
    t = u.shape[0]
    tm = _tile(t, 512, CHUNK)
    nc = tm // CHUNK
    nblk = t // tm
    w_ = hb * LANES

    def body(u_ref, w_ref, qg_ref, kd_ref, qk_ref, e_ref, z_ref, ng_ref, ss_ref, dy_ref,
             du_ref, dw_ref, dqg_ref, dkd_ref, dqk_ref, de_ref, dz_ref, dng_ref, ds_scr):
        i, hblk = pl.program_id(0), pl.program_id(1)

        @pl.when((i == 0) & (hblk == 0))
        def _():
            dng_ref[...] = jnp.zeros_like(dng_ref)

        for hl in range(hb):
            @pl.when(i == 0)
            def _():
                ds_scr[hblk * hb + hl] = jnp.zeros((LANES, LANES), F32)
        ngv = ng_ref[...]

        def chunk(cj, carry):
            ci = nc - 1 - cj
            rows = pl.ds(pl.multiple_of(ci * CHUNK, CHUNK), CHUNK)
            ev = e_ref[ci]
            sls = [slice(hl * LANES, (hl + 1) * LANES) for hl in range(hb)]
            _, vj = jax.vjp(_seq, [u_ref[rows, sl] for sl in sls], [w_ref[rows, sl].astype(F32) for sl in sls],
                            [qg_ref[rows, sl].astype(F32) for sl in sls], [kd_ref[rows, sl].astype(F32) for sl in sls],
                            [qk_ref[rows, sl][:, :CHUNK].astype(F32) for sl in sls], [ev[:, sl] for sl in sls],
                            [z_ref[rows, sl] for sl in sls], [ss_ref[ci, sl, :] for sl in sls], ngv)
            du, dw, dqg, dkd, dqk, de, dz, dsp, dng = vj(([dy_ref[rows, sl] for sl in sls],
                                                          [ds_scr[hblk * hb + hl] for hl in range(hb)]))
            for hl, sl in enumerate(sls):
                du_ref[rows, sl] = du[hl]
                dw_ref[rows, sl] = dw[hl]
                dqg_ref[rows, sl] = dqg[hl]
                dkd_ref[rows, sl] = dkd[hl]
                dqk_ref[rows, hl * LANES:hl * LANES + CHUNK] = dqk[hl]
                dqk_ref[rows, hl * LANES + CHUNK:(hl + 1) * LANES] = jnp.zeros((CHUNK, LANES - CHUNK), F32)
                de_ref[ci, :, sl] = de[hl]
                dz_ref[rows, sl] = dz[hl]
                ds_scr[hblk * hb + hl] = dsp[hl]
            dng_ref[...] += dng
            return carry

        lax.fori_loop(0, nc, chunk, 0)

    zc = z_col // w_
    rv = lambda i: nblk - 1 - i
    blk = pl.BlockSpec((tm, w_), lambda i, h: (rv(i), h))
    eblk = pl.BlockSpec((nc, 1, w_), lambda i, h: (rv(i), 0, h))
    one = pl.BlockSpec((1, LANES), lambda i, h: (0, 0))
    wide = jax.ShapeDtypeStruct((t, nheads * LANES), F32)
    return pl.pallas_call(
        body, name="dn_seq_bwd", grid=(nblk, nheads // hb),
        in_specs=[blk] * 5 + [eblk, pl.BlockSpec((tm, w_), lambda i, h: (rv(i), zc + h)), one,
                              pl.BlockSpec((nc, w_, LANES), lambda i, h: (rv(i), h, 0)), blk],
        out_specs=[blk] * 5 + [eblk, blk, one],
        out_shape=[wide] * 5 + [jax.ShapeDtypeStruct((t // CHUNK, 1, nheads * LANES), F32), wide,
                                jax.ShapeDtypeStruct((1, LANES), F32)],
        scratch_shapes=[pltpu.VMEM((nheads, LANES, LANES), F32)],
        compiler_params=_cp(("arbitrary", "arbitrary")),
    )(u, w, qg, kd, qk, e, p, ng, ss, dy)


def dn_wy_bwd(qkv, p, ba_col, alog_b, dtb_b, du, dw, dqg, dkd, dqk, de, nheads):
    t = qkv.shape[0]
    tm = _tile(t, 512, CHUNK * WY_UNROLL)
    nc = tm // CHUNK
    hb = WY_HEADS
    hp = alog_b.shape[0]
    w_ = hb * LANES

    def body(qkv_ref, ba_ref, al_ref, dt_ref, du_ref, dw_ref, dqg_ref, dkd_ref, dqk_ref, de_ref,
             dqkv_ref, dba_ref, dal_ref, ddt_ref):
        i, hblk = pl.program_id(0), pl.program_id(1)

        @pl.when((i == 0) & (hblk == 0))
        def _():
            dal_ref[...] = jnp.zeros_like(dal_ref)
            ddt_ref[...] = jnp.zeros_like(ddt_ref)

        @pl.when(hblk == 0)
        def _():
            dba_ref[...] = jnp.zeros_like(dba_ref)

        alv, dtv = al_ref[...], dt_ref[...]
        lane = lax.broadcasted_iota(jnp.int32, (CHUNK, LANES), 1)
        rowp = lax.broadcasted_iota(jnp.int32, (hp, LANES), 0)

        def trip(cj, carry):
            units = [(cj * WY_UNROLL + cu, hl) for cu in range(WY_UNROLL) for hl in range(hb)]
            args = [[] for _ in range(7)]
            cts = [[] for _ in range(6)]
            for ci, hl in units:
                rows = pl.ds(pl.multiple_of(ci * CHUNK, CHUNK), CHUNK)
                ba = ba_ref[rows, :]
                h = hblk * hb + hl
                sl = slice(hl * LANES, (hl + 1) * LANES)
                for lst, val in zip(args, (qkv_ref[rows, hl * 384:hl * 384 + 128],
                                           qkv_ref[rows, hl * 384 + 128:hl * 384 + 256],
                                           qkv_ref[rows, hl * 384 + 256:hl * 384 + 384],
                                           _lane_col(ba, h), _lane_col(ba, nheads + h),
                                           _head_pick(alv, h), _head_pick(dtv, h))):
                    lst.append(val)
                de11 = jnp.sum(de_ref[ci][:, sl], axis=1, keepdims=True)
                for lst, val in zip(cts, (du_ref[rows, sl], dw_ref[rows, sl], dqg_ref[rows, sl], dkd_ref[rows, sl],
                                          dqk_ref[rows, sl][:, :CHUNK], de11)):
                    lst.append(val)
            _, vj = jax.vjp(_wy, *args)
            grads = vj(tuple(cts))
            for n, (ci, hl) in enumerate(units):
                rows = pl.ds(pl.multiple_of(ci * CHUNK, CHUNK), CHUNK)
                h = hblk * hb + hl
                dq, dk, dv, dbc, dac, dal, ddt = [g_[n] for g_ in grads]
                dqkv_ref[rows, hl * 384:hl * 384 + 128] = dq
                dqkv_ref[rows, hl * 384 + 128:hl * 384 + 256] = dk
                dqkv_ref[rows, hl * 384 + 256:hl * 384 + 384] = dv
                dba_ref[rows, :] += jnp.where(lane == h, dbc, 0.0) + jnp.where(lane == nheads + h, dac, 0.0)
                dal_ref[...] += jnp.where(rowp == h, dal, 0.0)
                ddt_ref[...] += jnp.where(rowp == h, ddt, 0.0)
            return carry

        lax.fori_loop(0, nc // WY_UNROLL, trip, 0)

    bc = ba_col // LANES
    blk = pl.BlockSpec((tm, w_), lambda i, h: (i, h))
    tab = pl.BlockSpec((hp, LANES), lambda i, h: (0, 0))
    return pl.pallas_call(
        body, name="dn_wy_bwd", grid=(t // tm, nheads // hb),
        in_specs=[pl.BlockSpec((tm, hb * 384), lambda i, h: (i, h)), pl.BlockSpec((tm, LANES), lambda i, h: (i, bc)),
                  tab, tab] + [blk] * 5 + [pl.BlockSpec((nc, 1, w_), lambda i, h: (i, 0, h))],
        out_specs=[pl.BlockSpec((tm, hb * 384), lambda i, h: (i, h)), pl.BlockSpec((tm, LANES), lambda i, h: (i, 0)),
                   tab, tab],
        out_shape=[jax.ShapeDtypeStruct((t, nheads * 384), F32), jax.ShapeDtypeStruct((t, LANES), F32),
                   jax.ShapeDtypeStruct((hp, LANES), F32), jax.ShapeDtypeStruct((hp, LANES), F32)],
        compiler_params=_cp(("arbitrary", "arbitrary")),
    )(qkv, p, alog_b, dtb_b, du, dw, dqg, dkd, dqk, de)


def sg_fwd(p, col, sg, lg, lb, w, bias_b):
    t = p.shape[0]
    ng_ = sg // LANES
    tm = _tile(t, 256, LANES)
    cb = col // sg

    def body(u_ref, v_ref, g_ref, lg_ref, lb_ref, w_ref, b_ref, y_ref):
        for n in range(tm // LANES):
            rs = slice(n * LANES, (n + 1) * LANES)
            for gi in range(ng_):
                sl = slice(gi * LANES, (gi + 1) * LANES)
                y_ref[rs, sl] = _sg_block(u_ref[rs, sl], v_ref[rs, sl], g_ref[rs, sl], lg_ref[:, sl], lb_ref[:, sl],
                                          w_ref[gi], b_ref[gi])

    vec = pl.BlockSpec((1, sg), lambda i: (0, 0))
    full = pl.BlockSpec((ng_, LANES, LANES), lambda i: (0, 0, 0))
    return pl.pallas_call(
        body, name="sg_fwd", grid=(t // tm,),
        in_specs=[pl.BlockSpec((tm, sg), lambda i: (i, cb)), pl.BlockSpec((tm, sg), lambda i: (i, cb + 1)),
                  pl.BlockSpec((tm, sg), lambda i: (i, cb + 2)), vec, vec, full, full],
        out_specs=pl.BlockSpec((tm, sg), lambda i: (i, 0)),
        out_shape=jax.ShapeDtypeStruct((t, sg), F32),
        compiler_params=_cp(("parallel",)),
    )(p, p, p, lg, lb, w, bias_b)


def sg_bwd(p, col, sg, lg, lb, w, bias_b, dy):
    t = p.shape[0]
    ng_ = sg // LANES
    tm = _tile(t, 256, LANES)
    cb = col // sg

    def body(u_ref, v_ref, g_ref, lg_ref, lb_ref, w_ref, b_ref, dy_ref, d_ref, dlg_ref, dlb_ref, dw_ref, db_ref):
        @pl.when(pl.program_id(0) == 0)
        def _():
            for r in (dlg_ref, dlb_ref, dw_ref, db_ref):
                r[...] = jnp.zeros_like(r)

        for n in range(tm // LANES):
            rs = slice(n * LANES, (n + 1) * LANES)
            for gi in range(ng_):
                sl = slice(gi * LANES, (gi + 1) * LANES)
                _, vj = jax.vjp(_sg_block, u_ref[rs, sl], v_ref[rs, sl], g_ref[rs, sl], lg_ref[:, sl], lb_ref[:, sl],
                                w_ref[gi], b_ref[gi])
                du, dv, dg, dlg, dlb, dw, db = vj(dy_ref[rs, sl])
                d_ref[rs, gi * LANES:(gi + 1) * LANES] = du.astype(BF16)
                d_ref[rs, sg + gi * LANES:sg + (gi + 1) * LANES] = dv.astype(BF16)
                d_ref[rs, 2 * sg + gi * LANES:2 * sg + (gi + 1) * LANES] = dg.astype(BF16)
                dlg_ref[:, sl] += dlg
                dlb_ref[:, sl] += dlb
                dw_ref[gi] += dw
                db_ref[gi] += jnp.broadcast_to(jnp.sum(db, axis=1, keepdims=True), (LANES, LANES))

    vec = pl.BlockSpec((1, sg), lambda i: (0, 0))
    full = pl.BlockSpec((ng_, LANES, LANES), lambda i: (0, 0, 0))
    return pl.pallas_call(
        body, name="sg_bwd", grid=(t // tm,),
        in_specs=[pl.BlockSpec((tm, sg), lambda i: (i, cb)), pl.BlockSpec((tm, sg), lambda i: (i, cb + 1)),
                  pl.BlockSpec((tm, sg), lambda i: (i, cb + 2)), vec, vec, full, full,
                  pl.BlockSpec((tm, sg), lambda i: (i, 0))],
        out_specs=[pl.BlockSpec((tm, 3 * sg), lambda i: (i, 0)), vec, vec, full, full],
        out_shape=[jax.ShapeDtypeStruct((t, 3 * sg), BF16), jax.ShapeDtypeStruct((1, sg), F32),
                   jax.ShapeDtypeStruct((1, sg), F32), jax.ShapeDtypeStruct((ng_, LANES, LANES), F32),
                   jax.ShapeDtypeStruct((ng_, LANES, LANES), F32)],
        compiler_params=_cp(("arbitrary",)),
    )(p, p, p, lg, lb, w, bias_b, dy)


def out_proj(x, y_dn, y_sg, y_cv, wo, gate):
    t, d = x.shape
    dn, sg, cv = y_dn.shape[1], y_sg.shape[1], y_cv.shape[1]
    dmix = dn + sg + cv
    tm = _tile(t, 256, LANES)

    def body(x_ref, a_ref, b_ref, c_ref, w_ref, g_ref, xn_ref, y_ref, yt_ref):
        a, b, c = a_ref[...], b_ref[...], c_ref[...]
        y = (jnp.dot(a.astype(BF16), w_ref[0:dn, :], preferred_element_type=F32)
             + jnp.dot(b.astype(BF16), w_ref[dn:dn + sg, :], preferred_element_type=F32)
             + jnp.dot(c.astype(BF16), w_ref[dn + sg:, :], preferred_element_type=F32))
        y_ref[...] = y
        xn_ref[...] = x_ref[...] + g_ref[...] * y
        yt_ref[0:dn, :] = a.T.astype(BF16)
        yt_ref[dn:dn + sg, :] = b.T.astype(BF16)
        yt_ref[dn + sg:, :] = c.T.astype(BF16)

    row = lambda w_: pl.BlockSpec((tm, w_), lambda i: (i, 0))
    return pl.pallas_call(
        body, name="out_proj", grid=(t // tm,),
        in_specs=[row(d), row(dn), row(sg), row(cv), pl.BlockSpec((dmix, d), lambda i: (0, 0)),
                  pl.BlockSpec((1, d), lambda i: (0, 0))],
        out_specs=[row(d), row(d), pl.BlockSpec((dmix, tm), lambda i: (0, i))],
        out_shape=[jax.ShapeDtypeStruct((t, d), F32), jax.ShapeDtypeStruct((t, d), F32),
                   jax.ShapeDtypeStruct((dmix, t), BF16)],
        compiler_params=_cp(("parallel",)),
    )(x, y_dn, y_sg, y_cv, wo, gate)


def out_proj_bwd(dxn, y, gate, wo, dn, sg, cv):
    t, d = dxn.shape
    dmix = dn + sg + cv
    tm = _tile(t, 256, LANES)

    def body(dx_ref, y_ref, g_ref, w_ref, da_ref, db_ref, dc_ref, dyb_ref, dg_ref):
        @pl.when(pl.program_id(0) == 0)
        def _():
            dg_ref[...] = jnp.zeros_like(dg_ref)
        dx = dx_ref[...]
        dg_ref[...] += jnp.sum(dx * y_ref[...], axis=0, keepdims=True)
        dyb = (dx * g_ref[...]).astype(BF16)
        dyb_ref[...] = dyb
        dcat = lax.dot_general(dyb, w_ref[...], (((1,), (1,)), ((), ())), preferred_element_type=F32)
        da_ref[...] = dcat[:, 0:dn]
        db_ref[...] = dcat[:, dn:dn + sg]
        dc_ref[...] = dcat[:, dn + sg:]

    row = lambda w_: pl.BlockSpec((tm, w_), lambda i: (i, 0))
    vec = pl.BlockSpec((1, d), lambda i: (0, 0))
    return pl.pallas_call(
        body, name="out_proj_bwd", grid=(t // tm,),
        in_specs=[row(d), row(d), vec, pl.BlockSpec((dmix, d), lambda i: (0, 0))],
        out_specs=[row(dn), row(sg), row(cv), row(d), vec],
        out_shape=[jax.ShapeDtypeStruct((t, dn), F32), jax.ShapeDtypeStruct((t, sg), F32),
                   jax.ShapeDtypeStruct((t, cv), F32), jax.ShapeDtypeStruct((t, d), BF16),
                   jax.ShapeDtypeStruct((1, d), F32)],
        compiler_params=_cp(("arbitrary",)),
    )(dxn, y, gate, wo)


def matmul_acc(name, at, b):
    m, t = at.shape
    n = b.shape[1]
    tn, tk = _tile(n, 512, LANES), _tile(t, 512, LANES)

    def body(a_ref, b_ref, o_ref):
        @pl.when(pl.program_id(1) == 0)
        def _():
            o_ref[...] = jnp.zeros_like(o_ref)
        o_ref[...] += jnp.dot(a_ref[...], b_ref[...], preferred_element_type=F32)

    return pl.pallas_call(
        body, name=name, grid=(n // tn, t // tk),
        in_specs=[pl.BlockSpec((m, tk), lambda j, k: (0, k)), pl.BlockSpec((tk, tn), lambda j, k: (k, j))],
        out_specs=pl.BlockSpec((m, tn), lambda j, k: (0, j)),
        out_shape=jax.ShapeDtypeStruct((m, n), F32),
        compiler_params=_cp(("parallel", "arbitrary")),
    )(at, b)


def in_proj_bwd(dp, wp, x, dxn, shift, scale, ng):
    t, d = x.shape
    npc = wp.shape[1]
    tm, tk = _tile(t, 512, LANES), _tile(npc, 384, LANES)
    nk = npc // tk

    def body(dp_ref, w_ref, x_ref, dxn_ref, sh_ref, sc_ref, g_ref, dx_ref, dg_ref, dsc_ref, dsh_ref, acc):
        i, kk = pl.program_id(0), pl.program_id(1)

        @pl.when((i == 0) & (kk == 0))
        def _():
            for r in (dg_ref, dsc_ref, dsh_ref):
                r[...] = jnp.zeros_like(r)

        @pl.when(kk == 0)
        def _():
            acc[...] = jnp.zeros_like(acc)
        acc[...] += lax.dot_general(dp_ref[...], w_ref[...], (((1,), (1,)), ((), ())), preferred_element_type=F32)

        @pl.when(kk == nk - 1)
        def _():
            _, vj = jax.vjp(_modnorm, x_ref[...], g_ref[...], sc_ref[...], sh_ref[...])
            dx, dg, dsc, dsh = vj(acc[...])
            dx_ref[...] = dxn_ref[...] + dx
            dg_ref[...] += dg
            dsc_ref[...] += dsc
            dsh_ref[...] += dsh

    vec = pl.BlockSpec((1, d), lambda i, k: (0, 0))
    row = pl.BlockSpec((tm, d), lambda i, k: (i, 0))
    return pl.pallas_call(
        body, name="in_proj_bwd", grid=(t // tm, nk),
        in_specs=[pl.BlockSpec((tm, tk), lambda i, k: (i, k)), pl.BlockSpec((d, tk), lambda i, k: (0, k)),
                  row, row, vec, vec, vec],
        out_specs=[row, vec, vec, vec],
        out_shape=[jax.ShapeDtypeStruct((t, d), F32)] + [jax.ShapeDtypeStruct((1, d), F32)] * 3,
        scratch_shapes=[pltpu.VMEM((tm, d), F32)],
        compiler_params=_cp(("arbitrary", "arbitrary")),
    )(dp, wp, x, dxn, shift, scale, ng)


def loss_head(x, tgt, fg):
    t, d = x.shape
    tm = _tile(t, 512, 8)

    def body(x_ref, t_ref, g_ref, l_ref, dx_ref, dg_ref):
        @pl.when(pl.program_id(0) == 0)
        def _():
            l_ref[...] = jnp.zeros_like(l_ref)
            dg_ref[...] = jnp.zeros_like(dg_ref)
        y, vj = jax.vjp(_rmsnorm, x_ref[...], g_ref[...])
        err = y - t_ref[...]
        part = 0.5 * jnp.sum(jnp.sum(err * err, axis=1, keepdims=True), axis=0, keepdims=True) / d
        l_ref[...] += jnp.broadcast_to(part, l_ref.shape)
        dx, dg = vj(err / d)
        dx_ref[...] = dx
        dg_ref[...] += dg

    row = pl.BlockSpec((tm, d), lambda i: (i, 0))
    vec = pl.BlockSpec((1, d), lambda i: (0, 0))
    return pl.pallas_call(
        body, name="loss_head", grid=(t // tm,), in_specs=[row, row, vec],
        out_specs=[pl.BlockSpec((1, LANES), lambda i: (0, 0)), row, vec],
        out_shape=[jax.ShapeDtypeStruct((1, LANES), F32), jax.ShapeDtypeStruct((t, d), F32),
                   jax.ShapeDtypeStruct((1, d), F32)],
        compiler_params=_cp(("arbitrary",)),
    )(x, tgt, fg)


def adamw(name, w, g, m, v):
    r, c = w.shape
    tr = _tile(r, 256, 8) if r % 8 == 0 else r

    def body(w_ref, g_ref, m_ref, v_ref, d_ref, mo_ref, vo_ref):
        d_ref[...], mo_ref[...], vo_ref[...] = _adamw_math(w_ref[...], g_ref[...], m_ref[...], v_ref[...])

    blk = pl.BlockSpec((tr, c), lambda i: (i, 0))
    return pl.pallas_call(
        body, name=name, grid=(r // tr,), in_specs=[blk] * 4, out_specs=[blk] * 3,
        out_shape=[jax.ShapeDtypeStruct((r, c), F32)] * 3, compiler_params=_cp(("parallel",)),
    )(w, g, m, v)


def ada_fwd(c_all, w_ada, b_loc):
    nl, d, cols = w_ada.shape
    nb = c_all.shape[0]
    tn = _tile(cols, 512, LANES)

    def body(c_ref, w_ref, b_ref, o_ref):
        ca = _silu(c_ref[...]).astype(BF16)
        o_ref[0] = jnp.dot(ca, w_ref[0].astype(BF16), preferred_element_type=F32) + b_ref[0]

    return pl.pallas_call(
        body, name="ada_fwd", grid=(nl, cols // tn),
        in_specs=[pl.BlockSpec((nb, d), lambda l, j: (0, 0)), pl.BlockSpec((1, d, tn), lambda l, j: (l, 0, j)),
                  pl.BlockSpec((1, 1, tn), lambda l, j: (l, 0, j))],
        out_specs=pl.BlockSpec((1, nb, tn), lambda l, j: (l, 0, j)),
        out_shape=jax.ShapeDtypeStruct((nl, nb, cols), F32),
        compiler_params=_cp(("parallel", "parallel")),
    )(c_all, w_ada, b_loc)


def ada_bwd(c_all_t, dmod_loc, w, m, v):
    nl, d, cols = w.shape
    nb = c_all_t.shape[1]
    tr = _tile(d, 256, 8)

    def body(c_ref, dm_ref, w_ref, m_ref, v_ref, g_ref, d_ref, mo_ref, vo_ref):
        ca = _silu(c_ref[...])
        dm = dm_ref[0]
        g = _lane_col(ca, 0) * dm[0:1, :]
        for b in range(1, nb):
            g = g + _lane_col(ca, b) * dm[b:b + 1, :]
        g_ref[0] = g
        d_ref[0], mo_ref[0], vo_ref[0] = _adamw_math(w_ref[0], g, m_ref[0], v_ref[0])

    blk = pl.BlockSpec((1, tr, cols), lambda l, i: (l, i, 0))
    return pl.pallas_call(
        body, name="ada_bwd", grid=(nl, d // tr),
        in_specs=[pl.BlockSpec((tr, nb), lambda l, i: (i, 0)), pl.BlockSpec((1, nb, cols), lambda l, i: (l, 0, 0)),
                  blk, blk, blk],
        out_specs=[blk] * 4, out_shape=[jax.ShapeDtypeStruct((nl, d, cols), F32)] * 4,
        compiler_params=_cp(("parallel", "parallel")),
    )(c_all_t, dmod_loc, w, m, v)


def sum8(g):
    _, r, c = g.shape
    tr = _tile(r, 256, 8)

    def body(g_ref, o_ref):
        acc = g_ref[0]
        for k in range(1, NDEV):
            acc = acc + g_ref[k]
        o_ref[...] = acc

    return pl.pallas_call(
        body, name="sum8", grid=(r // tr,), in_specs=[pl.BlockSpec((NDEV, tr, c), lambda i: (0, i, 0))],
        out_specs=pl.BlockSpec((tr, c), lambda i: (i, 0)), out_shape=jax.ShapeDtypeStruct((r, c), F32),
        compiler_params=_cp(("parallel",)),
    )(g)


def pair_sum(own, r1):
    ns, r, c = own.shape
    tr = _tile(r, 256, 8)

    def body(g_ref, r_ref, ob_ref):
        ob_ref[...] = (g_ref[...] + r_ref[...]).astype(BF16)

    blk = pl.BlockSpec((1, tr, c), lambda s, i: (s, i, 0))
    return pl.pallas_call(
        body, name="pair_sum", grid=(ns, r // tr), in_specs=[blk, blk], out_specs=blk,
        out_shape=jax.ShapeDtypeStruct((ns, r, c), BF16), compiler_params=_cp(("parallel", "parallel")),
    )(own, r1)


def chip_sum(own, r1, r2):
    r, c = own.shape
    tr = _tile(r, 256, 8)

    def body(g_ref, r1_ref, r2_ref, o_ref):
        acc = g_ref[...] + r1_ref[...]
        for k in range(NCHIP - 1):
            acc = acc + r2_ref[k].astype(F32)
        o_ref[...] = acc

    blk = pl.BlockSpec((tr, c), lambda i: (i, 0))
    return pl.pallas_call(
        body, name="chip_sum", grid=(r // tr,),
        in_specs=[blk, blk, pl.BlockSpec((NCHIP - 1, tr, c), lambda i: (0, i, 0))], out_specs=blk,
        out_shape=jax.ShapeDtypeStruct((r, c), F32), compiler_params=_cp(("parallel",)),
    )(own, r1, r2)


def adamw_halves(name, cflag, w, own, recv, m, v):
    _, r, c = w.shape
    tr = _tile(r, 256, 8)

    def body(cf_ref, w_ref, a_ref, b_ref, m_ref, v_ref, g_ref, d_ref, mo_ref, vo_ref):
        is_own = cf_ref[0:1, 0:1] == pl.program_id(0).astype(F32)
        g = jnp.where(is_own, a_ref[...], b_ref[...])
        g_ref[0] = g
        d_ref[0], mo_ref[0], vo_ref[0] = _adamw_math(w_ref[0], g, m_ref[0], v_ref[0])

    blk = pl.BlockSpec((1, tr, c), lambda h, i: (h, i, 0))
    hlf = pl.BlockSpec((tr, c), lambda h, i: (i, 0))
    return pl.pallas_call(
        body, name=name, grid=(2, r // tr),
        in_specs=[pl.BlockSpec((1, LANES), lambda h, i: (0, 0)), blk, hlf, hlf, blk, blk], out_specs=[blk] * 4,
        out_shape=[jax.ShapeDtypeStruct(w.shape, F32)] * 4, compiler_params=_cp(("parallel", "parallel")),
    )(cflag, w, own, recv, m, v)


def _me():
    return lax.axis_index("x"), lax.axis_index("y"), lax.axis_index("c")


_FLIPS = ((1, 0), (0, 1), (1, 1))


def all_gather8(v):
    m_per, n = v.shape

    def body(x_ref, out_ref, send_sems, recv_sems, local_sem):
        x, y, c = _me()
        me, sibling = (x, y, c), (x, y, 1 - c)
        chips = [(x ^ fx, y ^ fy) for fx, fy in _FLIPS]

        def rows(px, py, pc):
            return out_ref.at[pl.ds((4 * px + 2 * py + pc) * m_per, m_per), :]

        def copy(k, block, to, src=None):
            return pltpu.make_async_remote_copy(
                src_ref=rows(*block) if src is None else src, dst_ref=rows(*block),
                send_sem=send_sems.at[k], recv_sem=recv_sems.at[k], device_id=to, device_id_type=MESH)

        mine = pltpu.make_async_copy(x_ref, rows(*me), local_sem)
        mine.start()
        first = [copy(0, me, sibling, src=x_ref)]
        first += [copy(1 + j, me, (*chip, c), src=x_ref) for j, chip in enumerate(chips)]
        for cp in first:
            cp.start()
        passed = [copy(4 + j, (*chip, c), sibling) for j, chip in enumerate(chips)]
        for j, chip in enumerate(chips):
            copy(1 + j, (*chip, c), me).wait_recv()
            passed[j].start()
        copy(0, sibling, me).wait_recv()
        for j, chip in enumerate(chips):
            copy(4 + j, (*chip, 1 - c), me).wait_recv()
        for cp in first + passed:
            cp.wait_send()
        mine.wait()

    return pl.pallas_call(
        body, name="all_gather8", out_shape=jax.ShapeDtypeStruct((NDEV * m_per, n), v.dtype),
        in_specs=[pl.BlockSpec(memory_space=pltpu.VMEM)], out_specs=pl.BlockSpec(memory_space=pltpu.VMEM),
        scratch_shapes=[pltpu.SemaphoreType.DMA((7,)), pltpu.SemaphoreType.DMA((7,)), pltpu.SemaphoreType.DMA],
        compiler_params=pltpu.CompilerParams(vmem_limit_bytes=VMEM_LIMIT),
    )(v)


def gather_weights(ws):
    na = len(ws)

    def body(*refs):
        srcs, outs = refs[:na], refs[na:2 * na]
        send_sems, recv_sems = refs[2 * na:]
        x, y, c = _me()
        chip = 2 * x + y
        sibling = (x, y, 1 - c)
        peers = [(x ^ fx, y ^ fy) for fx, fy in _FLIPS]

        def ici(a, j, half_of, to):
            return pltpu.make_async_remote_copy(
                src_ref=srcs[a].at[c], dst_ref=outs[a].at[half_of, c], send_sem=send_sems.at[a * 6 + j],
                recv_sem=recv_sems.at[a * 6 + j], device_id=to, device_id_type=MESH)

        def d2d(a, j, slot, half):
            return pltpu.make_async_remote_copy(
                src_ref=outs[a].at[slot, half], dst_ref=outs[a].at[slot, half], send_sem=send_sems.at[a * 6 + 3 + j],
                recv_sem=recv_sems.at[a * 6 + 3 + j], device_id=sibling, device_id_type=MESH)

        sends = []
        for a in range(na):
            for j, (px, py) in enumerate(peers):
                cp = ici(a, j, chip, (px, py, c))
                cp.start()
                sends.append(cp)
        for a in range(na):
            for j, (px, py) in enumerate(peers):
                ici(a, j, 2 * px + py, (px, py, c)).wait_recv()
                cp = d2d(a, j, 2 * px + py, c)
                cp.start()
                sends.append(cp)
        for a in range(na):
            for j, (px, py) in enumerate(peers):
                d2d(a, j, 2 * px + py, 1 - c).wait_recv()
        for cp in sends:
            cp.wait_send()

    return pl.pallas_call(
        body, name="gather_weights",
        out_shape=[jax.ShapeDtypeStruct((NCHIP,) + w.shape, w.dtype) for w in ws],
        in_specs=[ANY] * na, out_specs=[ANY] * na,
        scratch_shapes=[pltpu.SemaphoreType.DMA((6 * na,)), pltpu.SemaphoreType.DMA((6 * na,))],
    )(*ws)


def sibling_swap(name, gs):
    na = len(gs)

    def body(*refs):
        srcs, outs = refs[:na], refs[na:2 * na]
        send_sems, recv_sems = refs[2 * na:]
        x, y, c = _me()
        cps = [pltpu.make_async_remote_copy(
            src_ref=srcs[a], dst_ref=outs[a], send_sem=send_sems.at[a], recv_sem=recv_sems.at[a],
            device_id=(x, y, 1 - c), device_id_type=MESH) for a in range(na)]
        for cp in cps:
            cp.start()
        for cp in cps:
            cp.wait()

    return pl.pallas_call(
        body, name=name, out_shape=[jax.ShapeDtypeStruct(g.shape, g.dtype) for g in gs],
        in_specs=[ANY] * na, out_specs=[ANY] * na,
        scratch_shapes=[pltpu.SemaphoreType.DMA((na,)), pltpu.SemaphoreType.DMA((na,))],
    )(*gs)


def chip_exchange(ps):
    na = len(ps)

    def body(*refs):
        srcs, outs = refs[:na], refs[na:2 * na]
        send_sems, recv_sems = refs[2 * na:]
        x, y, c = _me()
        cps = []
        for a in range(na):
            for j, (fx, fy) in enumerate(_FLIPS):
                px, py = x ^ fx, y ^ fy
                cps.append(pltpu.make_async_remote_copy(
                    src_ref=srcs[a].at[2 * px + py], dst_ref=outs[a].at[j], send_sem=send_sems.at[a * 3 + j],
                    recv_sem=recv_sems.at[a * 3 + j], device_id=(px, py, c), device_id_type=MESH))
        for cp in cps:
            cp.start()
        for cp in cps:
            cp.wait()

    return pl.pallas_call(
        body, name="chip_exchange",
        out_shape=[jax.ShapeDtypeStruct((NCHIP - 1,) + p_.shape[1:], p_.dtype) for p_ in ps],
        in_specs=[ANY] * na, out_specs=[ANY] * na,
        scratch_shapes=[pltpu.SemaphoreType.DMA((3 * na,)), pltpu.SemaphoreType.DMA((3 * na,))],
    )(*ps)


class _Cfg:
    def __init__(self, x, a_log, sg_w, cv_ln_g, cv_w, conv_qkv):
        self.t, self.d = x.shape[1], x.shape[2]
        self.nl, self.h = a_log.shape
        self.dn = self.h * LANES
        self.g = sg_w.shape[1]
        self.sg = self.g * LANES
        self.cv = cv_ln_g.shape[1]
        self.kc = cv_w.shape[1]
        self.k4 = conv_qkv.shape[1]
        self.o_z = 3 * self.dn
        self.o_sg = 4 * self.dn
        self.o_cv = self.o_sg + 3 * self.sg
        self.o_ba = self.o_cv + 3 * self.cv
        self.npc = self.o_ba + LANES
        self.d_in = self.o_ba + 2 * self.h
        self.dmix = self.dn + self.sg + self.cv
        self.hb_fwd = _tile(self.h, 8, 1)
        self.hb_bwd = _tile(self.h, 4, 1)


def _perm_cols(cfg, w):
    dn, h = cfg.dn, cfg.h
    lead = w.shape[:-1]
    qkv = w[..., :3 * dn].reshape(lead + (3, h, LANES))
    qkv = jnp.moveaxis(qkv, -3, -2).reshape(lead + (3 * dn,))
    ba = w[..., 4 * dn:4 * dn + 2 * h]
    pad = jnp.zeros(lead + (LANES - 2 * h,), w.dtype)
    return jnp.concatenate([qkv, w[..., 3 * dn:4 * dn], w[..., 4 * dn + 2 * h:], ba, pad], axis=-1)


def _unperm_cols(cfg, w):
    dn, h = cfg.dn, cfg.h
    lead = w.shape[:-1]
    qkv = w[..., :3 * dn].reshape(lead + (h, 3, LANES))
    qkv = jnp.moveaxis(qkv, -3, -2).reshape(lead + (3 * dn,))
    return jnp.concatenate([qkv, w[..., 3 * dn:4 * dn], w[..., cfg.o_ba:cfg.o_ba + 2 * h], w[..., 4 * dn:cfg.o_ba]], axis=-1)


def _layer_fwd(cfg, x, mod, lw):
    shift, scale, gate = mod
    p, ht = in_proj(x, shift, scale, lw["norm_g"], lw["wp"])
    qk_post = [_qk_post, _qk_post, _v_post]
    qkv = conv_fwd("dn_pre_fwd", cfg.k4, HALO4, lambda a: a, [(p, 0)], lw["conv_qkv"], qk_post, [], [],
                   3 * cfg.dn, 3 * LANES)
    wy = dn_wy_fwd(qkv, p, cfg.o_ba, lw["alog_b"], lw["dtb_b"], cfg.h)
    y_dn, ss = dn_seq_fwd(*wy, p, cfg.o_z, lw["dn_norm_g"], cfg.h, cfg.hb_fwd)
    y_sg = sg_fwd(p, cfg.o_sg, cfg.sg, lw["sg_ln_g"], lw["sg_ln_b"], lw["sg_w"], lw["sg_bias_b"])
    cv_post = [_cv_post] * (cfg.cv // LANES)
    y_cv = conv_fwd("cv_fwd", cfg.kc, HALO31, _glu, [(p, cfg.o_cv), (p, cfg.o_cv + cfg.cv)], lw["cv_w"], cv_post,
                    [(p, cfg.o_cv + 2 * cfg.cv)], [lw["cv_b"], lw["cv_ln_g"], lw["cv_ln_b"]], cfg.cv, cfg.cv)
    xn, y, yt = out_proj(x, y_dn, y_sg, y_cv, lw["wo"], gate)
    return xn, dict(x=x, p=p, ht=ht, qkv=qkv, wy=wy, ss=ss, y=y, yt=yt)


def _layer_bwd(cfg, dxn, mod, lw, sv):
    shift, scale, gate = mod
    p = sv["p"]
    d_dn, d_sg, d_cv, dyb, dgate = out_proj_bwd(dxn, sv["y"], gate, lw["wo"], cfg.dn, cfg.sg, cfg.cv)
    g_wo = matmul_acc("w_out_grad", sv["yt"], dyb)
    cv_post = [_cv_post] * (cfg.cv // LANES)
    dcv, g_cvw, (g_cvb, g_cvlg, g_cvlb) = conv_bwd(
        "cv_bwd", cfg.kc, HALO31, _glu, [(p, cfg.o_cv), (p, cfg.o_cv + cfg.cv)], lw["cv_w"], cv_post,
        [(p, cfg.o_cv + 2 * cfg.cv)], [lw["cv_b"], lw["cv_ln_g"], lw["cv_ln_b"]], d_cv, cfg.cv, cfg.cv, tm_pref=256)
    dsg, g_sglg, g_sglb, g_sgw, g_sgb = sg_bwd(p, cfg.o_sg, cfg.sg, lw["sg_ln_g"], lw["sg_ln_b"], lw["sg_w"],
                                               lw["sg_bias_b"], d_sg)
    *dwy, dz, g_dng = dn_seq_bwd(*sv["wy"], p, cfg.o_z, lw["dn_norm_g"], sv["ss"], d_dn, cfg.h, cfg.hb_bwd)
    dqkv, dba, g_al, g_dt = dn_wy_bwd(sv["qkv"], p, cfg.o_ba, lw["alog_b"], lw["dtb_b"], *dwy, cfg.h)
    qk_post = [_qk_post, _qk_post, _v_post]
    dqkv_pre, g_cq, _ = conv_bwd("dn_pre_bwd", cfg.k4, HALO4, lambda a: a, [(p, 0)], lw["conv_qkv"], qk_post, [], [],
                                 dqkv, 3 * cfg.dn, 3 * LANES)
    dp = jnp.concatenate([dqkv_pre, dz.astype(BF16), dsg, dcv, dba.astype(BF16)], axis=1)
    g_wp = matmul_acc("w_in_grad", sv["ht"], dp)
    dx, g_ng, dscale, dshift = in_proj_bwd(dp, lw["wp"], sv["x"], dxn, shift, scale, lw["norm_g"])
    grads = dict(norm_g=g_ng, conv_qkv=g_cq, a_log=g_al[:cfg.h, 0], dt_bias=g_dt[:cfg.h, 0], dn_norm_g=g_dng,
                 sg_ln_g=g_sglg, sg_ln_b=g_sglb, sg_w=g_sgw, sg_b=g_sgb[:, :, 0], cv_w=g_cvw, cv_b=g_cvb,
                 cv_ln_g=g_cvlg, cv_ln_b=g_cvlb, wp=g_wp, wo=g_wo)
    return dx, grads, (dshift, dscale, dgate)


def _local_step(cfg, xs, tgt, mods, lws, fg):
    nl = len(lws)
    saved = []
    for l in range(nl):
        xs, sv = _layer_fwd(cfg, xs, mods[l], lws[l])
        saved.append(sv)
    loss_b, dx, g_fg = loss_head(xs, tgt, fg)
    lg = [None] * nl
    dmods = [None] * nl
    for l in reversed(range(nl)):
        dx, lg[l], dmods[l] = _layer_bwd(cfg, dx, mods[l], lws[l], saved[l])
    return loss_b, dx, g_fg, lg, dmods


SMALL = ("norm_g", "conv_qkv", "a_log", "dt_bias", "dn_norm_g", "sg_ln_g", "sg_ln_b", "sg_w", "sg_b", "cv_w",
         "cv_b", "cv_ln_g", "cv_ln_b", "final_g", "b_ada")
PACK_N = 1024


def _pack(arrs):
    flat = jnp.concatenate([a.reshape(-1).astype(F32) for a in arrs])
    rows = -(-flat.shape[0] // PACK_N)
    rows = -(-rows // 8) * 8
    return jnp.pad(flat, (0, rows * PACK_N - flat.shape[0])).reshape(rows, PACK_N)


def _unpack(buf, shapes):
    flat = buf.reshape(-1)
    out, o = [], 0
    for s in shapes:
        n = 1
        for d_ in s:
            n *= d_
        out.append(flat[o:o + n].reshape(s))
        o += n
    return out


def kernel(x, c, norm_g, w_ada, b_ada, w_in, conv_qkv, a_log, dt_bias, dn_norm_g, sg_ln_g, sg_ln_b, sg_w, sg_b, cv_w, cv_b, cv_ln_g, cv_ln_b, w_out, final_g, loss_target, m_norm_g, m_w_ada, m_b_ada, m_w_in, m_conv_qkv, m_a_log, m_dt_bias, m_dn_norm_g, m_sg_ln_g, m_sg_ln_b, m_sg_w, m_sg_b, m_cv_w, m_cv_b, m_cv_ln_g, m_cv_ln_b, m_w_out, m_final_g, v_norm_g, v_w_ada, v_b_ada, v_w_in, v_conv_qkv, v_a_log, v_dt_bias, v_dn_norm_g, v_sg_ln_g, v_sg_ln_b, v_sg_w, v_sg_b, v_cv_w, v_cv_b, v_cv_ln_g, v_cv_ln_b, v_w_out, v_final_g):
    cfg = _Cfg(x, a_log, sg_w, cv_ln_g, cv_w, conv_qkv)
    nl, d, t, h = cfg.nl, cfg.d, cfg.t, cfg.h
    lh = nl // 2
    ax, ay, ac = _me()
    chip = 2 * ax + ay
    dev = 2 * chip + ac
    wts = dict(norm_g=norm_g, w_ada=w_ada, b_ada=b_ada, w_in=w_in, conv_qkv=conv_qkv, a_log=a_log, dt_bias=dt_bias,
               dn_norm_g=dn_norm_g, sg_ln_g=sg_ln_g, sg_ln_b=sg_ln_b, sg_w=sg_w, sg_b=sg_b, cv_w=cv_w, cv_b=cv_b,
               cv_ln_g=cv_ln_g, cv_ln_b=cv_ln_b, w_out=w_out, final_g=final_g)
    mom = dict(norm_g=m_norm_g, w_ada=m_w_ada, b_ada=m_b_ada, w_in=m_w_in, conv_qkv=m_conv_qkv, a_log=m_a_log,
               dt_bias=m_dt_bias, dn_norm_g=m_dn_norm_g, sg_ln_g=m_sg_ln_g, sg_ln_b=m_sg_ln_b, sg_w=m_sg_w,
               sg_b=m_sg_b, cv_w=m_cv_w, cv_b=m_cv_b, cv_ln_g=m_cv_ln_g, cv_ln_b=m_cv_ln_b, w_out=m_w_out,
               final_g=m_final_g)
    vel = dict(norm_g=v_norm_g, w_ada=v_w_ada, b_ada=v_b_ada, w_in=v_w_in, conv_qkv=v_conv_qkv, a_log=v_a_log,
               dt_bias=v_dt_bias, dn_norm_g=v_dn_norm_g, sg_ln_g=v_sg_ln_g, sg_ln_b=v_sg_ln_b, sg_w=v_sg_w,
               sg_b=v_sg_b, cv_w=v_cv_w, cv_b=v_cv_b, cv_ln_g=v_cv_ln_g, cv_ln_b=v_cv_ln_b, w_out=v_w_out,
               final_g=v_final_g)
    ada_cols = w_ada.shape[2]
    in_cols = w_in.shape[2]
    out_rows = w_out.shape[1]
    cq_cols = conv_qkv.shape[2]
    cvw_cols = cv_w.shape[2]

    c_all = all_gather8(jnp.pad(c, ((0, 7), (0, 0)))).reshape(NDEV, 8, d)[:, 0, :]
    b_loc = lax.dynamic_slice_in_dim(b_ada, chip * ada_cols, ada_cols, axis=1)[:, None, :]
    mod_part = ada_fwd(c_all, w_ada, b_loc)
    mod_all = all_gather8(mod_part.reshape(nl * NDEV, ada_cols)).reshape(NDEV, nl, NDEV, ada_cols)
    mod_me = lax.dynamic_index_in_dim(mod_all[0::2], dev, axis=2, keepdims=False)
    mod_me = jnp.moveaxis(mod_me, 0, 1).reshape(nl, 3, 1, d)

    win_b = w_in.astype(BF16).reshape(2, lh, d, in_cols)
    wout_b = w_out.astype(BF16).reshape(2, lh, out_rows, d)
    win_all, wout_all = gather_weights([win_b, wout_b])
    win_all = lax.dynamic_update_index_in_dim(win_all, win_b, chip, axis=0)
    wout_all = lax.dynamic_update_index_in_dim(wout_all, wout_b, chip, axis=0)
    win_full = jnp.moveaxis(win_all.reshape(NCHIP, nl, d, in_cols), 0, 2).reshape(nl, d, NCHIP * in_cols)
    wp_all = _perm_cols(cfg, win_full)
    wo_all = jnp.moveaxis(wout_all.reshape(NCHIP, nl, out_rows, d), 0, 1).reshape(nl, NCHIP * out_rows, d)

    cq_all = all_gather8(conv_qkv.reshape(nl * cfg.k4, cq_cols)).reshape(NDEV, nl, cfg.k4, cq_cols)[0::2]
    cq_full = jnp.moveaxis(cq_all, 0, 2).reshape(nl, cfg.k4, NCHIP * cq_cols)
    cq_perm = _perm_cols_qkv(cfg, cq_full)
    kcp = -(-cfg.kc // 8) * 8
    cvw_all = all_gather8(jnp.pad(cv_w, ((0, 0), (0, kcp - cfg.kc), (0, 0))).reshape(nl * kcp, cvw_cols))
    cvw_all = cvw_all.reshape(NDEV, nl, kcp, cvw_cols)[0::2]
    cvw_full = jnp.moveaxis(cvw_all, 0, 2).reshape(nl, kcp, NCHIP * cvw_cols)[:, :cfg.kc]

    hp = -(-h // 8) * 8
    lws = []
    for l in range(nl):
        lws.append(dict(
            norm_g=norm_g[l][None], wp=wp_all[l], wo=wo_all[l], conv_qkv=cq_perm[l],
            alog_b=jnp.pad(jnp.broadcast_to(a_log[l][:, None], (h, LANES)), ((0, hp - h), (0, 0))),
            dtb_b=jnp.pad(jnp.broadcast_to(dt_bias[l][:, None], (h, LANES)), ((0, hp - h), (0, 0))),
            dn_norm_g=dn_norm_g[l][None], sg_ln_g=sg_ln_g[l][None], sg_ln_b=sg_ln_b[l][None], sg_w=sg_w[l],
            sg_bias_b=jnp.broadcast_to(sg_b[l][:, :, None], (cfg.g, LANES, LANES)),
            cv_w=cvw_full[l], cv_b=cv_b[l][None], cv_ln_g=cv_ln_g[l][None], cv_ln_b=cv_ln_b[l][None]))

    mods = [(mod_me[l, 0], mod_me[l, 1], mod_me[l, 2]) for l in range(nl)]
    loss_b, dx, g_fg, lg, dmods = _local_step(cfg, x[0], loss_target[0], mods, lws, final_g[None])
    grad_x = dx[None]

    dmod = jnp.stack([jnp.concatenate(dm, axis=1)[0] for dm in dmods])
    stack = lambda k: jnp.stack([g_[k] for g_ in lg])
    small_local = [stack(k).reshape(wts_shape) for k, wts_shape in
                   (("norm_g", (nl, d)), ("conv_qkv", (nl, cfg.k4, 3 * cfg.dn)), ("a_log", (nl, h)),
                    ("dt_bias", (nl, h)), ("dn_norm_g", (nl, LANES)), ("sg_ln_g", (nl, cfg.sg)),
                    ("sg_ln_b", (nl, cfg.sg)), ("sg_w", (nl, cfg.g, LANES, LANES)), ("sg_b", (nl, cfg.g, LANES)),
                    ("cv_w", (nl, cfg.kc, cfg.cv)), ("cv_b", (nl, cfg.cv)), ("cv_ln_g", (nl, cfg.cv)),
                    ("cv_ln_b", (nl, cfg.cv)))]
    small_local[1] = _unperm_cols_qkv(cfg, small_local[1])
    small_local += [g_fg[0], dmod, loss_b[0, 0:1]]
    shapes = [a.shape for a in small_local]
    packed = _pack(small_local)
    rows = packed.shape[0]
    gathered = all_gather8(packed).reshape(NDEV, rows, PACK_N)
    summed = _unpack(sum8(gathered), shapes)
    sgrads = dict(zip(SMALL, summed[:15]))
    loss = summed[15][0]
    sgrads["conv_qkv"] = lax.dynamic_slice_in_dim(sgrads["conv_qkv"], chip * cq_cols, cq_cols, axis=2)
    sgrads["cv_w"] = lax.dynamic_slice_in_dim(sgrads["cv_w"], chip * cvw_cols, cvw_cols, axis=2)

    off = sum(math.prod(s) for s in shapes[:14])
    dmod_all = gathered.reshape(NDEV, rows * PACK_N)[:, off:off + nl * 3 * d].reshape(NDEV, nl, 3 * d)
    dmod_loc = jnp.moveaxis(lax.dynamic_slice_in_dim(dmod_all, chip * ada_cols, ada_cols, axis=2), 0, 1)
    g_wada, d_wada, nm_wada, nv_wada = ada_bwd(c_all.T, dmod_loc, w_ada, m_w_ada, v_w_ada)

    g_wp = _unperm_cols(cfg, jnp.stack([g_["wp"] for g_ in lg]))
    g_wp = g_wp.reshape(2, lh, d, NCHIP, in_cols)
    g_wo = jnp.stack([g_["wo"] for g_ in lg]).reshape(2, lh, NCHIP, out_rows, d)
    half = lambda g_, which: lax.dynamic_index_in_dim(g_, which, axis=0, keepdims=False)
    by_chip_in = lambda g_: jnp.moveaxis(g_, 2, 0).reshape(NCHIP, lh * d, in_cols)
    by_chip_out = lambda g_: jnp.moveaxis(g_, 1, 0).reshape(NCHIP, lh * out_rows, d)
    keep_in, send_in = by_chip_in(half(g_wp, ac)), by_chip_in(half(g_wp, 1 - ac))
    keep_out, send_out = by_chip_out(half(g_wo, ac)), by_chip_out(half(g_wo, 1 - ac))
    r1_in, r1_out = sibling_swap("swap_halves", [send_in, send_out])
    r2_in, r2_out = chip_exchange([pair_sum(keep_in, r1_in), pair_sum(keep_out, r1_out)])
    mine = lambda g_: lax.dynamic_index_in_dim(g_, chip, axis=0, keepdims=False)
    h_in = chip_sum(mine(keep_in), mine(r1_in), r2_in)
    h_out = chip_sum(mine(keep_out), mine(r1_out), r2_out)
    o_in, o_out = sibling_swap("join_halves", [h_in, h_out])

    cflag = jnp.full((1, LANES), ac, F32)
    v3 = lambda a, r_, c_: a.reshape(2, lh * r_, c_)
    grad_w_in, d_in_, nm_in, nv_in = adamw_halves("adamw_w_in", cflag, v3(w_in, d, in_cols), h_in, o_in,
                                                  v3(m_w_in, d, in_cols), v3(v_w_in, d, in_cols))
    grad_w_out, d_out_, nm_out, nv_out = adamw_halves("adamw_w_out", cflag, v3(w_out, out_rows, d), h_out, o_out,
                                                      v3(m_w_out, out_rows, d), v3(v_w_out, out_rows, d))
    grad_w_in = grad_w_in.reshape(w_in.shape)
    grad_w_out = grad_w_out.reshape(w_out.shape)
    sshapes = [wts[k].shape for k in SMALL]
    pk = lambda dct: _pack([dct[k] for k in SMALL])
    d_s, m_s, v_s = adamw("adamw_small", pk(wts), pk(sgrads), pk(mom), pk(vel))
    d_small = dict(zip(SMALL, _unpack(d_s, sshapes)))
    m_small = dict(zip(SMALL, _unpack(m_s, sshapes)))
    v_small = dict(zip(SMALL, _unpack(v_s, sshapes)))

    grads = dict(sgrads, w_ada=g_wada, w_in=grad_w_in, w_out=grad_w_out)
    deltas = dict(d_small, w_ada=d_wada, w_in=d_in_.reshape(w_in.shape), w_out=d_out_.reshape(w_out.shape))
    new_m = dict(m_small, w_ada=nm_wada, w_in=nm_in.reshape(w_in.shape), w_out=nm_out.reshape(w_out.shape))
    new_v = dict(v_small, w_ada=nv_wada, w_in=nv_in.reshape(w_in.shape), w_out=nv_out.reshape(w_out.shape))
    order = ("norm_g", "w_ada", "b_ada", "w_in", "conv_qkv", "a_log", "dt_bias", "dn_norm_g", "sg_ln_g", "sg_ln_b",
             "sg_w", "sg_b", "cv_w", "cv_b", "cv_ln_g", "cv_ln_b", "w_out", "final_g")
    return (loss, grad_x, *[grads[k] for k in order], *[deltas[k] for k in order], *[new_m[k] for k in order],
            *[new_v[k] for k in order])


def _perm_cols_qkv(cfg, w):
    lead = w.shape[:-1]
    return jnp.moveaxis(w.reshape(lead + (3, cfg.h, LANES)), -3, -2).reshape(lead + (3 * cfg.dn,))


def _unperm_cols_qkv(cfg, w):
    lead = w.shape[:-1]
    return jnp.moveaxis(w.reshape(lead + (cfg.h, 3, LANES)), -3, -2).reshape(lead + (3 * cfg.dn,))
```

```python
import functools
import math

import jax
import jax.numpy as jnp
from jax import lax
from jax.experimental import pallas as pl
from jax.experimental.pallas import tpu as pltpu

F32 = jnp.float32
BF16 = jnp.bfloat16
EPS = 1e-6
LN_EPS = 1e-5
LANES = 128
CHUNK = 64
SUBLANES = 8
HALO4 = 8
HALO31 = 32
NCHIP = 4
NDEV = 8
VMEM_LIMIT = 56 * 2 ** 20
ADAM_LR, ADAM_B1, ADAM_B2, ADAM_EPS, ADAM_WD, ADAM_STEP = 0.001, 0.9, 0.999, 1e-08, 0.01, 10
MESH = pl.DeviceIdType.MESH
ANY = pl.BlockSpec(memory_space=pl.ANY)


def _cp(sem=None, vmem=VMEM_LIMIT):
    return pltpu.CompilerParams(dimension_semantics=sem, vmem_limit_bytes=vmem)


def _tile(n, pref, mult):
    t = min(n, pref) // mult * mult
    while t > 0 and n % t:
        t -= mult
    return t if t > 0 else n


def _split(a):
    hi = a.astype(BF16)
    return hi, (a - hi.astype(F32)).astype(BF16)


def _raw_dot(a, b, ca, cb, hi):
    dn = (((ca,), (cb,)), ((), ()))
    if hi:
        ah, al = _split(a.astype(F32))
        bh, bl = _split(b.astype(F32))
        d3 = lambda x, y: lax.dot_general(x, y, dn, preferred_element_type=F32)
        return d3(ah, bh) + (d3(al, bh) + d3(ah, bl))
    return lax.dot_general(a.astype(BF16), b.astype(BF16), dn, preferred_element_type=F32)


@functools.partial(jax.custom_vjp, nondiff_argnums=(2, 3, 4))
def bdot(a, b, ca, cb, hi):
    return _raw_dot(a, b, ca, cb, hi)


def _bdot_fwd(a, b, ca, cb, hi):
    return _raw_dot(a, b, ca, cb, hi), (a, b)


def _bdot_bwd(ca, cb, hi, res, ct):
    a, b = res
    fa, fb = 1 - ca, 1 - cb
    da = _raw_dot(ct, b, 1, fb, hi) if ca == 1 else _raw_dot(b, ct, fb, 1, hi)
    db = _raw_dot(a, ct, fa, 0, hi) if cb == 0 else _raw_dot(ct, a, 0, fa, hi)
    return da.astype(a.dtype), db.astype(b.dtype)


bdot.defvjp(_bdot_fwd, _bdot_bwd)


def _sigmoid(x):
    return 1.0 / (1.0 + jnp.exp(-x))


def _silu(x):
    return x * _sigmoid(x)


def _gelu(x):
    return 0.5 * x * (1.0 + lax.erf(x * (2.0 ** -0.5)))


def _softplus(x):
    return jnp.maximum(x, 0.0) + jnp.log(1.0 + jnp.exp(-jnp.abs(x)))


def _modnorm(x, g, scale, shift):
    y = x * lax.rsqrt(jnp.mean(x * x, axis=-1, keepdims=True) + EPS)
    return (y * g) * (1.0 + scale) + shift


def _rmsnorm(x, g):
    return x * lax.rsqrt(jnp.mean(x * x, axis=-1, keepdims=True) + EPS) * g


def _layernorm(x, g, b):
    mu = jnp.mean(x, axis=-1, keepdims=True)
    xc = x - mu
    var = jnp.mean(xc * xc, axis=-1, keepdims=True)
    return xc * lax.rsqrt(var + LN_EPS) * g + b


def _l2norm(t):
    return t * lax.rsqrt(jnp.sum(t * t, axis=-1, keepdims=True) + EPS)


def _adamw_math(w, g, m, v):
    mn = ADAM_B1 * m + (1.0 - ADAM_B1) * g
    vn = ADAM_B2 * v + (1.0 - ADAM_B2) * (g * g)
    mh = mn / (1.0 - ADAM_B1 ** ADAM_STEP)
    vh = vn / (1.0 - ADAM_B2 ** ADAM_STEP)
    delta = -ADAM_LR * (mh / (jnp.sqrt(vh) + ADAM_EPS) + ADAM_WD * w)
    return delta, mn, vn


def _each(f, *lists):
    return [f(*xs) for xs in zip(*lists)]


def _wy(q, k, v, bcol, acol, alog, dtb):
    c = CHUNK
    r = lax.broadcasted_iota(jnp.int32, (c, c), 0)
    cc = lax.broadcasted_iota(jnp.int32, (c, c), 1)
    rr = lax.broadcasted_iota(jnp.int32, (c, 1), 0)
    tri_incl, tri_strict, eye = r >= cc, r > cc, r == cc
    beta = _each(_sigmoid, bcol)
    g = _each(lambda al, a_, dt: -jnp.exp(al) * _softplus(a_ + dt), alog, acol, dtb)
    gb = [jnp.broadcast_to(g_, (c, c)) for g_ in g]
    g_row = [jnp.sum(jnp.where(eye, b_, 0.0), axis=0, keepdims=True) for b_ in gb]
    gc_col = [jnp.sum(jnp.where(tri_incl, jnp.broadcast_to(gr, (c, c)), 0.0), axis=1, keepdims=True) for gr in g_row]
    gc_row = [jnp.sum(jnp.where(r <= cc, b_, 0.0), axis=0, keepdims=True) for b_ in gb]
    decay = _each(lambda gcc, gcr: jnp.where(tri_incl, jnp.exp(jnp.where(tri_incl, gcc - gcr, 0.0)), 0.0),
                  gc_col, gc_row)
    qs = [q_ * (q_.shape[-1] ** -0.5) for q_ in q]
    kb = _each(lambda k_, b_: k_ * b_, k, beta)
    a = _each(lambda kb_, k_, d_: jnp.where(tri_strict, bdot(kb_, k_, 1, 1, False) * d_, 0.0), kb, k, decay)
    dv = v[0].shape[-1]
    x = _each(lambda v_, b_, kb_, gcc: jnp.concatenate([v_ * b_, kb_ * jnp.exp(gcc)], axis=1), v, beta, kb, gc_col)
    x = _each(lambda a_, x_: x_ - bdot(a_, x_, 1, 0, True), a, x)
    p = a
    for _ in range(5):
        p = _each(lambda p_: bdot(p_, p_, 1, 0, True), p)
        x = _each(lambda p_, x_: x_ + bdot(p_, x_, 1, 0, True), p, x)
    xv = [x_[:, :dv] for x_ in x]
    xk = [x_[:, dv:] for x_ in x]
    qk = _each(lambda q_, k_, d_: bdot(q_, k_, 1, 1, False) * d_, qs, k, decay)
    g_last = [jnp.sum(jnp.where(rr == c - 1, gcc, 0.0), axis=0, keepdims=True) for gcc in gc_col]
    qg = _each(lambda q_, gcc: q_ * jnp.exp(gcc), qs, gc_col)
    kd = _each(lambda k_, gl, gcc: k_ * jnp.exp(gl - gcc), k, g_last, gc_col)
    return xv, xk, qg, kd, qk, [jnp.exp(gl) for gl in g_last]


def _seq(u, w, qg, kd, qk, e, z, s, ng):
    v_new = _each(lambda u_, w_, s_: u_ - bdot(w_, s_, 1, 0, False), u, w, s)
    o1 = _each(lambda q_, s_: bdot(q_, s_, 1, 0, False), qg, s)
    o2 = _each(lambda qk_, vn: bdot(qk_, vn, 1, 0, False), qk, v_new)
    ds = _each(lambda kd_, vn: bdot(kd_, vn, 0, 0, False), kd, v_new)
    s_next = _each(lambda s_, e_, d_: s_ * e_ + d_, s, e, ds)
    y = _each(lambda a_, b_, z_: _rmsnorm(a_ + b_, ng) * _silu(z_), o1, o2, z)
    return y, s_next


def _sg_block(u, v, gt, lg, lb, w, bias):
    n = w.shape[0]
    pr = lax.broadcasted_iota(jnp.int32, (n, n), 0) // CHUNK
    pc = lax.broadcasted_iota(jnp.int32, (n, n), 1) // CHUNK
    wm = jnp.where(pr >= pc, w, 0.0)
    vl = _layernorm(_gelu(v), lg, lb)
    mixed = bdot(wm, vl, 1, 0, False) + bias
    return _gelu(u) * mixed * _silu(gt)


def _glu(a, b):
    return a * _sigmoid(b)


def _cv_post(conv, gate, cb, lg, lb):
    return _silu(_layernorm(conv + cb, lg, lb)) * _silu(gate)


def _qk_post(conv):
    return _l2norm(_silu(conv))


def _v_post(conv):
    return _silu(conv)


def in_proj(x, shift, scale, ng, wp):
    t, d = x.shape
    npc = wp.shape[1]
    tm, tn = _tile(t, 512, LANES), _tile(npc, 2432, LANES)

    def body(x_ref, sh_ref, sc_ref, g_ref, w_ref, p_ref, ht_ref, h_scr):
        @pl.when(pl.program_id(1) == 0)
        def _():
            h = _modnorm(x_ref[...], g_ref[...], sc_ref[...], sh_ref[...])
            h_scr[...] = h.astype(BF16)
            ht_ref[...] = h.T.astype(BF16)
        p_ref[...] = jnp.dot(h_scr[...], w_ref[...], preferred_element_type=F32)

    vec = pl.BlockSpec((1, d), lambda i, j: (0, 0))
    return pl.pallas_call(
        body, name="in_proj", grid=(t // tm, npc // tn),
        in_specs=[pl.BlockSpec((tm, d), lambda i, j: (i, 0)), vec, vec, vec,
                  pl.BlockSpec((d, tn), lambda i, j: (0, j))],
        out_specs=[pl.BlockSpec((tm, tn), lambda i, j: (i, j)), pl.BlockSpec((d, tm), lambda i, j: (0, i))],
        out_shape=[jax.ShapeDtypeStruct((t, npc), F32), jax.ShapeDtypeStruct((d, t), BF16)],
        scratch_shapes=[pltpu.VMEM((tm, d), BF16)],
        compiler_params=_cp(("parallel", "arbitrary")),
    )(x, shift, scale, ng, wp)


def _roll_bank(x, bank_ref, offsets):
    rows = x.shape[0]
    residues = sorted({o % SUBLANES for o in offsets})
    for slot, b in enumerate(residues):
        bank_ref[slot] = x if b == 0 else pltpu.roll(x, rows - b, 0)
    return {o: (residues.index(o % SUBLANES), o - o % SUBLANES) for o in offsets}


def _n_residues(offsets):
    return len({o % SUBLANES for o in offsets})


def conv_fwd(name, k, halo, pre_fn, pre, w, post_fns, extras, params, c_total, tc, tm_pref=512):
    t = pre[0][0].shape[0]
    tm = _tile(t, tm_pref, halo)
    npre, nex, npar = len(pre), len(extras), len(params)
    ngr = tc // LANES
    taps = [halo - (k - 1) + j for j in range(k)]

    def body(*refs):
        prev = refs[:npre]
        cur = refs[npre:2 * npre]
        w_ref = refs[2 * npre]
        ex = refs[2 * npre + 1:2 * npre + 1 + nex]
        par = refs[2 * npre + 1 + nex:2 * npre + 1 + nex + npar]
        out_ref, buf, bank = refs[-3], refs[-2], refs[-1]
        i = pl.program_id(1)
        pv = pre_fn(*[r[...] for r in prev])
        buf[0:halo, :] = jnp.where(i > 0, pv, 0.0)
        buf[halo:, :] = pre_fn(*[r[...] for r in cur])
        where = _roll_bank(buf[...], bank, taps)
        acc = None
        for j, o in enumerate(taps):
            slot, st = where[o]
            term = w_ref[j:j + 1, :] * bank[slot, st:st + tm, :]
            acc = term if acc is None else acc + term
        for gi in range(ngr):
            sl = slice(gi * LANES, (gi + 1) * LANES)
            out_ref[:, sl] = post_fns[gi](acc[:, sl], *[e[:, sl] for e in ex], *[p_[:, sl] for p_ in par])

    hb = tm // halo
    in_specs = ([pl.BlockSpec((halo, tc), functools.partial(lambda j, i, o: (jnp.maximum(i * hb - 1, 0), o + j), o=col // tc))
                 for _, col in pre]
                + [pl.BlockSpec((tm, tc), functools.partial(lambda j, i, o: (i, o + j), o=col // tc)) for _, col in pre]
                + [pl.BlockSpec((k, tc), lambda j, i: (0, j))]
                + [pl.BlockSpec((tm, tc), functools.partial(lambda j, i, o: (i, o + j), o=col // tc)) for _, col in extras]
                + [pl.BlockSpec((1, tc), lambda j, i: (0, j)) for _ in params])
    args = [a for a, _ in pre] * 2 + [w] + [a for a, _ in extras] + list(params)
    return pl.pallas_call(
        body, name=name, grid=(c_total // tc, t // tm), in_specs=in_specs,
        out_specs=pl.BlockSpec((tm, tc), lambda j, i: (i, j)),
        out_shape=jax.ShapeDtypeStruct((t, c_total), F32),
        scratch_shapes=[pltpu.VMEM((halo + tm, tc), F32), pltpu.VMEM((_n_residues(taps), halo + tm, tc), F32)],
        compiler_params=_cp(("parallel", "arbitrary")),
    )(*args)


def conv_bwd(name, k, halo, pre_fn, pre, w, post_fns, extras, params, dout, c_total, tc, tm_pref=512):
    t = pre[0][0].shape[0]
    tm = _tile(t, tm_pref, halo)
    npre, nex, npar = len(pre), len(extras), len(params)
    ngr = tc // LANES
    nout = npre + nex
    assert nout == 1 or c_total == tc
    nblk = t // tm
    ext = tm + halo
    taps = [halo - (k - 1) + j for j in range(k)]
    back = [k - 1 - j for j in range(k)]

    def body(*refs):
        it = iter(refs)
        prev = [next(it) for _ in range(npre)]
        cur = [next(it) for _ in range(npre)]
        nxt = [next(it) for _ in range(npre)]
        w_ref = next(it)
        ex_c = [next(it) for _ in range(nex)]
        ex_n = [next(it) for _ in range(nex)]
        par = [next(it) for _ in range(npar)]
        do_c, do_n = next(it), next(it)
        din_ref, dw_ref = next(it), next(it)
        dpar = [next(it) for _ in range(npar)]
        buf, dbuf, bank, dbank = next(it), next(it), next(it), next(it)
        i = pl.program_id(1)

        @pl.when(i == 0)
        def _():
            dw_ref[...] = jnp.zeros_like(dw_ref)
            for r in dpar:
                r[...] = jnp.zeros_like(r)

        buf[0:halo, :] = jnp.where(i > 0, pre_fn(*[r[...] for r in prev]), 0.0)
        cur_vals = [r[...] for r in cur]
        buf[halo:halo + tm, :] = pre_fn(*cur_vals)
        buf[halo + tm:, :] = pre_fn(*[r[...] for r in nxt])
        where = _roll_bank(buf[...], bank, taps)
        conv = None
        for j, o in enumerate(taps):
            slot, st = where[o]
            term = w_ref[j:j + 1, :] * bank[slot, st:st + ext, :]
            conv = term if conv is None else conv + term
        don = jnp.where(i < nblk - 1, do_n[...], 0.0)
        for gi in range(ngr):
            sl = slice(gi * LANES, (gi + 1) * LANES)
            pv = [p_[:, sl] for p_ in par]
            _, vj = jax.vjp(post_fns[gi], conv[:tm, sl], *[e[:, sl] for e in ex_c], *pv)
            gr = vj(do_c[:, sl])
            dbuf[0:tm, sl] = gr[0]
            for e in range(nex):
                din_ref[:, (npre + e) * tc + gi * LANES:(npre + e) * tc + (gi + 1) * LANES] = gr[1 + e].astype(din_ref.dtype)
            for q_ in range(npar):
                dpar[q_][:, sl] += gr[1 + nex + q_]
            _, vjn = jax.vjp(post_fns[gi], conv[tm:, sl], *[e[:, sl] for e in ex_n], *pv)
            dbuf[tm:, sl] = vjn(don[:, sl])[0]
        dcur = dbuf[0:tm, :]
        dwhere = _roll_bank(dbuf[...], dbank, back)
        dpre = None
        for j in range(k):
            slot, st = dwhere[back[j]]
            term = w_ref[j:j + 1, :] * dbank[slot, st:st + tm, :]
            dpre = term if dpre is None else dpre + term
            slot, st = where[taps[j]]
            dw_ref[j:j + 1, :] += jnp.sum(dcur * bank[slot, st:st + tm, :], axis=0, keepdims=True)
        _, vjp_pre = jax.vjp(pre_fn, *cur_vals)
        for e, gval in enumerate(vjp_pre(dpre)):
            din_ref[:, e * tc:(e + 1) * tc] = gval.astype(din_ref.dtype)

    hb = tm // halo
    last_h = t // halo - 1

    def spec(kind, col):
        o = col // tc
        if kind == "prev":
            return pl.BlockSpec((halo, tc), lambda j, i: (jnp.maximum(i * hb - 1, 0), o + j))
        if kind == "next":
            return pl.BlockSpec((halo, tc), lambda j, i: (jnp.minimum((i + 1) * hb, last_h), o + j))
        return pl.BlockSpec((tm, tc), lambda j, i: (i, o + j))

    in_specs = ([spec("prev", col) for _, col in pre] + [spec("cur", col) for _, col in pre]
                + [spec("next", col) for _, col in pre] + [pl.BlockSpec((k, tc), lambda j, i: (0, j))]
                + [spec("cur", col) for _, col in extras] + [spec("next", col) for _, col in extras]
                + [pl.BlockSpec((1, tc), lambda j, i: (0, j)) for _ in params]
                + [spec("cur", 0), spec("next", 0)])
    args = [a for a, _ in pre] * 3 + [w] + [a for a, _ in extras] * 2 + list(params) + [dout, dout]
    out = pl.pallas_call(
        body, name=name, grid=(c_total // tc, nblk), in_specs=in_specs,
        out_specs=[pl.BlockSpec((tm, nout * tc), lambda j, i: (i, j)), pl.BlockSpec((k, tc), lambda j, i: (0, j))]
        + [pl.BlockSpec((1, tc), lambda j, i: (0, j)) for _ in params],
        out_shape=[jax.ShapeDtypeStruct((t, nout * c_total), BF16), jax.ShapeDtypeStruct((k, c_total), F32)]
        + [jax.ShapeDtypeStruct((1, c_total), F32) for _ in params],
        scratch_shapes=[pltpu.VMEM((2 * halo + tm, tc), F32), pltpu.VMEM((ext, tc), F32),
                        pltpu.VMEM((_n_residues(taps), 2 * halo + tm, tc), F32),
                        pltpu.VMEM((_n_residues(back), ext, tc), F32)],
        compiler_params=_cp(("parallel", "arbitrary")),
    )(*args)
    return out[0], out[1], out[2:]


def _head_pick(ref_val, row):
    rr = lax.broadcasted_iota(jnp.int32, ref_val.shape, 0)
    v = jnp.sum(jnp.where(rr == row, ref_val, 0.0), axis=0, keepdims=True)
    ll = lax.broadcasted_iota(jnp.int32, v.shape, 1)
    return jnp.sum(jnp.where(ll == 0, v, 0.0), axis=1, keepdims=True)


def _lane_col(blk, lane_idx):
    ll = lax.broadcasted_iota(jnp.int32, blk.shape, 1)
    return jnp.sum(jnp.where(ll == lane_idx, blk, 0.0), axis=1, keepdims=True)


WY_HEADS = 2
WY_UNROLL = 4


def dn_wy_fwd(qkv, p, ba_col, alog_b, dtb_b, nheads):
    t = qkv.shape[0]
    tm = _tile(t, 512, CHUNK * WY_UNROLL)
    nc = tm // CHUNK
    hb = WY_HEADS
    hp = alog_b.shape[0]
    w_ = hb * LANES

    def body(qkv_ref, ba_ref, al_ref, dt_ref, u_ref, w_ref, qg_ref, kd_ref, qk_ref, e_ref):
        hblk = pl.program_id(1)
        alv, dtv = al_ref[...], dt_ref[...]

        def trip(cj, carry):
            units = [(cj * WY_UNROLL + cu, hl) for cu in range(WY_UNROLL) for hl in range(hb)]
            args = [[] for _ in range(7)]
            for ci, hl in units:
                rows = pl.ds(pl.multiple_of(ci * CHUNK, CHUNK), CHUNK)
                ba = ba_ref[rows, :]
                h = hblk * hb + hl
                for lst, val in zip(args, (qkv_ref[rows, hl * 384:hl * 384 + 128],
                                           qkv_ref[rows, hl * 384 + 128:hl * 384 + 256],
                                           qkv_ref[rows, hl * 384 + 256:hl * 384 + 384],
                                           _lane_col(ba, h), _lane_col(ba, nheads + h),
                                           _head_pick(alv, h), _head_pick(dtv, h))):
                    lst.append(val)
            outs = _wy(*args)
            for n, (ci, hl) in enumerate(units):
                rows = pl.ds(pl.multiple_of(ci * CHUNK, CHUNK), CHUNK)
                u, w, qg, kd, qk, e = [o[n] for o in outs]
                sl = slice(hl * LANES, (hl + 1) * LANES)
                u_ref[rows, sl] = u
                w_ref[rows, sl] = w.astype(BF16)
                qg_ref[rows, sl] = qg.astype(BF16)
                kd_ref[rows, sl] = kd.astype(BF16)
                qk_ref[rows, hl * LANES:hl * LANES + CHUNK] = qk.astype(BF16)
                qk_ref[rows, hl * LANES + CHUNK:(hl + 1) * LANES] = jnp.zeros((CHUNK, LANES - CHUNK), BF16)
                e_ref[ci, :, sl] = jnp.broadcast_to(e, (1, LANES))
            return carry

        lax.fori_loop(0, nc // WY_UNROLL, trip, 0)

    bc = ba_col // LANES
    blk = pl.BlockSpec((tm, w_), lambda i, h: (i, h))
    tab = pl.BlockSpec((hp, LANES), lambda i, h: (0, 0))
    wide = lambda dt: jax.ShapeDtypeStruct((t, nheads * LANES), dt)
    return pl.pallas_call(
        body, name="dn_wy_fwd", grid=(t // tm, nheads // hb),
        in_specs=[pl.BlockSpec((tm, hb * 384), lambda i, h: (i, h)), pl.BlockSpec((tm, LANES), lambda i, h: (i, bc)),
                  tab, tab],
        out_specs=[blk] * 5 + [pl.BlockSpec((nc, 1, w_), lambda i, h: (i, 0, h))],
        out_shape=[wide(F32), wide(BF16), wide(BF16), wide(BF16), wide(BF16),
                   jax.ShapeDtypeStruct((t // CHUNK, 1, nheads * LANES), F32)],
        compiler_params=_cp(("parallel", "parallel")),
    )(qkv, p, alog_b, dtb_b)


def dn_seq_fwd(u, w, qg, kd, qk, e, p, z_col, ng, nheads, hb):
    t = u.shape[0]
    tm = _tile(t, 512, CHUNK)
    nc = tm // CHUNK
    w_ = hb * LANES

    def body(u_ref, w_ref, qg_ref, kd_ref, qk_ref, e_ref, z_ref, ng_ref, y_ref, ss_ref, s_scr):
        i, hblk = pl.program_id(0), pl.program_id(1)
        for hl in range(hb):
            @pl.when(i == 0)
            def _():
                s_scr[hblk * hb + hl] = jnp.zeros((LANES, LANES), F32)
        ngv = ng_ref[...]

        def chunk(ci, carry):
            rows = pl.ds(pl.multiple_of(ci * CHUNK, CHUNK), CHUNK)
            ev = e_ref[ci]
            sls = [slice(hl * LANES, (hl + 1) * LANES) for hl in range(hb)]
            s = [s_scr[hblk * hb + hl] for hl in range(hb)]
            for hl in range(hb):
                ss_ref[ci, sls[hl], :] = s[hl]
            y, sn = _seq([u_ref[rows, sl] for sl in sls], [w_ref[rows, sl].astype(F32) for sl in sls],
                         [qg_ref[rows, sl].astype(F32) for sl in sls], [kd_ref[rows, sl].astype(F32) for sl in sls],
                         [qk_ref[rows, sl][:, :CHUNK].astype(F32) for sl in sls], [ev[:, sl] for sl in sls],
                         [z_ref[rows, sl] for sl in sls], s, ngv)
            for hl in range(hb):
                y_ref[rows, sls[hl]] = y[hl]
                s_scr[hblk * hb + hl] = sn[hl]
            return carry

        lax.fori_loop(0, nc, chunk, 0)

    zc = z_col // w_
    blk = pl.BlockSpec((tm, w_), lambda i, h: (i, h))
    return pl.pallas_call(
        body, name="dn_seq_fwd", grid=(t // tm, nheads // hb),
        in_specs=[blk] * 5 + [pl.BlockSpec((nc, 1, w_), lambda i, h: (i, 0, h)),
                              pl.BlockSpec((tm, w_), lambda i, h: (i, zc + h)),
                              pl.BlockSpec((1, LANES), lambda i, h: (0, 0))],
        out_specs=[blk, pl.BlockSpec((nc, w_, LANES), lambda i, h: (i, h, 0))],
        out_shape=[jax.ShapeDtypeStruct((t, nheads * LANES), F32),
                   jax.ShapeDtypeStruct((t // CHUNK, nheads * LANES, LANES), F32)],
        scratch_shapes=[pltpu.VMEM((nheads, LANES, LANES), F32)],
        compiler_params=_cp(("arbitrary", "arbitrary")),
    )(u, w, qg, kd, qk, e, p, ng)


def dn_seq_bwd(u, w, qg, kd, qk, e, p, z_col, ng, ss, dy, nheads, hb):
    t = u.shape[0]
    tm = _tile(t, 512, CHUNK)
    nc = tm // CHUNK
    nblk = t // tm
    w_ = hb * LANES

    def body(u_ref, w_ref, qg_ref, kd_ref, qk_ref, e_ref, z_ref, ng_ref, ss_ref, dy_ref,
             du_ref, dw_ref, dqg_ref, dkd_ref, dqk_ref, de_ref, dz_ref, dng_ref, ds_scr):
        i, hblk = pl.program_id(0), pl.program_id(1)

        @pl.when((i == 0) & (hblk == 0))
        def _():
            dng_ref[...] = jnp.zeros_like(dng_ref)

        for hl in range(hb):
            @pl.when(i == 0)
            def _():
                ds_scr[hblk * hb + hl] = jnp.zeros((LANES, LANES), F32)
        ngv = ng_ref[...]

        def chunk(cj, carry):
            ci = nc - 1 - cj
            rows = pl.ds(pl.multiple_of(ci * CHUNK, CHUNK), CHUNK)
            ev = e_ref[ci]
            sls = [slice(hl * LANES, (hl + 1) * LANES) for hl in range(hb)]
            _, vj = jax.vjp(_seq, [u_ref[rows, sl] for sl in sls], [w_ref[rows, sl].astype(F32) for sl in sls],
                            [qg_ref[rows, sl].astype(F32) for sl in sls], [kd_ref[rows, sl].astype(F32) for sl in sls],
                            [qk_ref[rows, sl][:, :CHUNK].astype(F32) for sl in sls], [ev[:, sl] for sl in sls],
                            [z_ref[rows, sl] for sl in sls], [ss_ref[ci, sl, :] for sl in sls], ngv)
            du, dw, dqg, dkd, dqk, de, dz, dsp, dng = vj(([dy_ref[rows, sl] for sl in sls],
                                                          [ds_scr[hblk * hb + hl] for hl in range(hb)]))
            for hl, sl in enumerate(sls):
                du_ref[rows, sl] = du[hl]
                dw_ref[rows, sl] = dw[hl]
                dqg_ref[rows, sl] = dqg[hl]
                dkd_ref[rows, sl] = dkd[hl]
                dqk_ref[rows, hl * LANES:hl * LANES + CHUNK] = dqk[hl]
                dqk_ref[rows, hl * LANES + CHUNK:(hl + 1) * LANES] = jnp.zeros((CHUNK, LANES - CHUNK), F32)
                de_ref[ci, :, sl] = de[hl]
                dz_ref[rows, sl] = dz[hl]
                ds_scr[hblk * hb + hl] = dsp[hl]
            dng_ref[...] += dng
            return carry

        lax.fori_loop(0, nc, chunk, 0)

    zc = z_col // w_
    rv = lambda i: nblk - 1 - i
    blk = pl.BlockSpec((tm, w_), lambda i, h: (rv(i), h))
    eblk = pl.BlockSpec((nc, 1, w_), lambda i, h: (rv(i), 0, h))
    one = pl.BlockSpec((1, LANES), lambda i, h: (0, 0))
    wide = jax.ShapeDtypeStruct((t, nheads * LANES), F32)
    return pl.pallas_call(
        body, name="dn_seq_bwd", grid=(nblk, nheads // hb),
        in_specs=[blk] * 5 + [eblk, pl.BlockSpec((tm, w_), lambda i, h: (rv(i), zc + h)), one,
                              pl.BlockSpec((nc, w_, LANES), lambda i, h: (rv(i), h, 0)), blk],
        out_specs=[blk] * 5 + [eblk, blk, one],
        out_shape=[wide] * 5 + [jax.ShapeDtypeStruct((t // CHUNK, 1, nheads * LANES), F32), wide,
                                jax.ShapeDtypeStruct((1, LANES), F32)],
        scratch_shapes=[pltpu.VMEM((nheads, LANES, LANES), F32)],
        compiler_params=_cp(("arbitrary", "arbitrary")),
    )(u, w, qg, kd, qk, e, p, ng, ss, dy)


def dn_wy_bwd(qkv, p, ba_col, alog_b, dtb_b, du, dw, dqg, dkd, dqk, de, nheads):
    t = qkv.shape[0]
    tm = _tile(t, 512, CHUNK * WY_UNROLL)
    nc = tm // CHUNK
    hb = WY_HEADS
    hp = alog_b.shape[0]
    w_ = hb * LANES

    def body(qkv_ref, ba_ref, al_ref, dt_ref, du_ref, dw_ref, dqg_ref, dkd_ref, dqk_ref, de_ref,
             dqkv_ref, dba_ref, dal_ref, ddt_ref):
        i, hblk = pl.program_id(0), pl.program_id(1)

        @pl.when((i == 0) & (hblk == 0))
        def _():
            dal_ref[...] = jnp.zeros_like(dal_ref)
            ddt_ref[...] = jnp.zeros_like(ddt_ref)

        @pl.when(hblk == 0)
        def _():
            dba_ref[...] = jnp.zeros_like(dba_ref)

        alv, dtv = al_ref[...], dt_ref[...]
        lane = lax.broadcasted_iota(jnp.int32, (CHUNK, LANES), 1)
        rowp = lax.broadcasted_iota(jnp.int32, (hp, LANES), 0)

        def trip(cj, carry):
            units = [(cj * WY_UNROLL + cu, hl) for cu in range(WY_UNROLL) for hl in range(hb)]
            args = [[] for _ in range(7)]
            cts = [[] for _ in range(6)]
            for ci, hl in units:
                rows = pl.ds(pl.multiple_of(ci * CHUNK, CHUNK), CHUNK)
                ba = ba_ref[rows, :]
                h = hblk * hb + hl
                sl = slice(hl * LANES, (hl + 1) * LANES)
                for lst, val in zip(args, (qkv_ref[rows, hl * 384:hl * 384 + 128],
                                           qkv_ref[rows, hl * 384 + 128:hl * 384 + 256],
                                           qkv_ref[rows, hl * 384 + 256:hl * 384 + 384],
                                           _lane_col(ba, h), _lane_col(ba, nheads + h),
                                           _head_pick(alv, h), _head_pick(dtv, h))):
                    lst.append(val)
                de11 = jnp.sum(de_ref[ci][:, sl], axis=1, keepdims=True)
                for lst, val in zip(cts, (du_ref[rows, sl], dw_ref[rows, sl], dqg_ref[rows, sl], dkd_ref[rows, sl],
                                          dqk_ref[rows, sl][:, :CHUNK], de11)):
                    lst.append(val)
            _, vj = jax.vjp(_wy, *args)
            grads = vj(tuple(cts))
            for n, (ci, hl) in enumerate(units):
                rows = pl.ds(pl.multiple_of(ci * CHUNK, CHUNK), CHUNK)
                h = hblk * hb + hl
                dq, dk, dv, dbc, dac, dal, ddt = [g_[n] for g_ in grads]
                dqkv_ref[rows, hl * 384:hl * 384 + 128] = dq
                dqkv_ref[rows, hl * 384 + 128:hl * 384 + 256] = dk
                dqkv_ref[rows, hl * 384 + 256:hl * 384 + 384] = dv
                dba_ref[rows, :] += jnp.where(lane == h, dbc, 0.0) + jnp.where(lane == nheads + h, dac, 0.0)
                dal_ref[...] += jnp.where(rowp == h, dal, 0.0)
                ddt_ref[...] += jnp.where(rowp == h, ddt, 0.0)
            return carry

        lax.fori_loop(0, nc // WY_UNROLL, trip, 0)

    bc = ba_col // LANES
    blk = pl.BlockSpec((tm, w_), lambda i, h: (i, h))
    tab = pl.BlockSpec((hp, LANES), lambda i, h: (0, 0))
    return pl.pallas_call(
        body, name="dn_wy_bwd", grid=(t // tm, nheads // hb),
        in_specs=[pl.BlockSpec((tm, hb * 384), lambda i, h: (i, h)), pl.BlockSpec((tm, LANES), lambda i, h: (i, bc)),
                  tab, tab] + [blk] * 5 + [pl.BlockSpec((nc, 1, w_), lambda i, h: (i, 0, h))],
        out_specs=[pl.BlockSpec((tm, hb * 384), lambda i, h: (i, h)), pl.BlockSpec((tm, LANES), lambda i, h: (i, 0)),
                   tab, tab],
        out_shape=[jax.ShapeDtypeStruct((t, nheads * 384), F32), jax.ShapeDtypeStruct((t, LANES), F32),
                   jax.ShapeDtypeStruct((hp, LANES), F32), jax.ShapeDtypeStruct((hp, LANES), F32)],
        compiler_params=_cp(("arbitrary", "arbitrary")),
    )(qkv, p, alog_b, dtb_b, du, dw, dqg, dkd, dqk, de)


def sg_fwd(p, col, sg, lg, lb, w, bias_b):
    t = p.shape[0]
    ng_ = sg // LANES
    tm = _tile(t, 256, LANES)
    cb = col // sg

    def body(u_ref, v_ref, g_ref, lg_ref, lb_ref, w_ref, b_ref, y_ref):
        for n in range(tm // LANES):
            rs = slice(n * LANES, (n + 1) * LANES)
            for gi in range(ng_):
                sl = slice(gi * LANES, (gi + 1) * LANES)
                y_ref[rs, sl] = _sg_block(u_ref[rs, sl], v_ref[rs, sl], g_ref[rs, sl], lg_ref[:, sl], lb_ref[:, sl],
                                          w_ref[gi], b_ref[gi])

    vec = pl.BlockSpec((1, sg), lambda i: (0, 0))
    full = pl.BlockSpec((ng_, LANES, LANES), lambda i: (0, 0, 0))
    return pl.pallas_call(
        body, name="sg_fwd", grid=(t // tm,),
        in_specs=[pl.BlockSpec((tm, sg), lambda i: (i, cb)), pl.BlockSpec((tm, sg), lambda i: (i, cb + 1)),
                  pl.BlockSpec((tm, sg), lambda i: (i, cb + 2)), vec, vec, full, full],
        out_specs=pl.BlockSpec((tm, sg), lambda i: (i, 0)),
        out_shape=jax.ShapeDtypeStruct((t, sg), F32),
        compiler_params=_cp(("parallel",)),
    )(p, p, p, lg, lb, w, bias_b)


def sg_bwd(p, col, sg, lg, lb, w, bias_b, dy):
    t = p.shape[0]
    ng_ = sg // LANES
    tm = _tile(t, 256, LANES)
    cb = col // sg

    def body(u_ref, v_ref, g_ref, lg_ref, lb_ref, w_ref, b_ref, dy_ref, d_ref, dlg_ref, dlb_ref, dw_ref, db_ref):
        @pl.when(pl.program_id(0) == 0)
        def _():
            for r in (dlg_ref, dlb_ref, dw_ref, db_ref):
                r[...] = jnp.zeros_like(r)

        for n in range(tm // LANES):
            rs = slice(n * LANES, (n + 1) * LANES)
            for gi in range(ng_):
                sl = slice(gi * LANES, (gi + 1) * LANES)
                _, vj = jax.vjp(_sg_block, u_ref[rs, sl], v_ref[rs, sl], g_ref[rs, sl], lg_ref[:, sl], lb_ref[:, sl],
                                w_ref[gi], b_ref[gi])
                du, dv, dg, dlg, dlb, dw, db = vj(dy_ref[rs, sl])
                d_ref[rs, gi * LANES:(gi + 1) * LANES] = du.astype(BF16)
                d_ref[rs, sg + gi * LANES:sg + (gi + 1) * LANES] = dv.astype(BF16)
                d_ref[rs, 2 * sg + gi * LANES:2 * sg + (gi + 1) * LANES] = dg.astype(BF16)
                dlg_ref[:, sl] += dlg
                dlb_ref[:, sl] += dlb
                dw_ref[gi] += dw
                db_ref[gi] += jnp.broadcast_to(jnp.sum(db, axis=1, keepdims=True), (LANES, LANES))

    vec = pl.BlockSpec((1, sg), lambda i: (0, 0))
    full = pl.BlockSpec((ng_, LANES, LANES), lambda i: (0, 0, 0))
    return pl.pallas_call(
        body, name="sg_bwd", grid=(t // tm,),
        in_specs=[pl.BlockSpec((tm, sg), lambda i: (i, cb)), pl.BlockSpec((tm, sg), lambda i: (i, cb + 1)),
                  pl.BlockSpec((tm, sg), lambda i: (i, cb + 2)), vec, vec, full, full,
                  pl.BlockSpec((tm, sg), lambda i: (i, 0))],
        out_specs=[pl.BlockSpec((tm, 3 * sg), lambda i: (i, 0)), vec, vec, full, full],
        out_shape=[jax.ShapeDtypeStruct((t, 3 * sg), BF16), jax.ShapeDtypeStruct((1, sg), F32),
                   jax.ShapeDtypeStruct((1, sg), F32), jax.ShapeDtypeStruct((ng_, LANES, LANES), F32),
                   jax.ShapeDtypeStruct((ng_, LANES, LANES), F32)],
        compiler_params=_cp(("arbitrary",)),
    )(p, p, p, lg, lb, w, bias_b, dy)


def out_proj(x, y_dn, y_sg, y_cv, wo, gate):
    t, d = x.shape
    dn, sg, cv = y_dn.shape[1], y_sg.shape[1], y_cv.shape[1]
    dmix = dn + sg + cv
    tm = _tile(t, 256, LANES)

    def body(x_ref, a_ref, b_ref, c_ref, w_ref, g_ref, xn_ref, y_ref, yt_ref):
        a, b, c = a_ref[...], b_ref[...], c_ref[...]
        y = (jnp.dot(a.astype(BF16), w_ref[0:dn, :], preferred_element_type=F32)
             + jnp.dot(b.astype(BF16), w_ref[dn:dn + sg, :], preferred_element_type=F32)
             + jnp.dot(c.astype(BF16), w_ref[dn + sg:, :], preferred_element_type=F32))
        y_ref[...] = y
        xn_ref[...] = x_ref[...] + g_ref[...] * y
        yt_ref[0:dn, :] = a.T.astype(BF16)
        yt_ref[dn:dn + sg, :] = b.T.astype(BF16)
        yt_ref[dn + sg:, :] = c.T.astype(BF16)

    row = lambda w_: pl.BlockSpec((tm, w_), lambda i: (i, 0))
    return pl.pallas_call(
        body, name="out_proj", grid=(t // tm,),
        in_specs=[row(d), row(dn), row(sg), row(cv), pl.BlockSpec((dmix, d), lambda i: (0, 0)),
                  pl.BlockSpec((1, d), lambda i: (0, 0))],
        out_specs=[row(d), row(d), pl.BlockSpec((dmix, tm), lambda i: (0, i))],
        out_shape=[jax.ShapeDtypeStruct((t, d), F32), jax.ShapeDtypeStruct((t, d), F32),
                   jax.ShapeDtypeStruct((dmix, t), BF16)],
        compiler_params=_cp(("parallel",)),
    )(x, y_dn, y_sg, y_cv, wo, gate)


def out_proj_bwd(dxn, y, gate, wo, dn, sg, cv):
    t, d = dxn.shape
    dmix = dn + sg + cv
    tm = _tile(t, 256, LANES)

    def body(dx_ref, y_ref, g_ref, w_ref, da_ref, db_ref, dc_ref, dyb_ref, dg_ref):
        @pl.when(pl.program_id(0) == 0)
        def _():
            dg_ref[...] = jnp.zeros_like(dg_ref)
        dx = dx_ref[...]
        dg_ref[...] += jnp.sum(dx * y_ref[...], axis=0, keepdims=True)
        dyb = (dx * g_ref[...]).astype(BF16)
        dyb_ref[...] = dyb
        dcat = lax.dot_general(dyb, w_ref[...], (((1,), (1,)), ((), ())), preferred_element_type=F32)
        da_ref[...] = dcat[:, 0:dn]
        db_ref[...] = dcat[:, dn:dn + sg]
        dc_ref[...] = dcat[:, dn + sg:]

    row = lambda w_: pl.BlockSpec((tm, w_), lambda i: (i, 0))
    vec = pl.BlockSpec((1, d), lambda i: (0, 0))
    return pl.pallas_call(
        body, name="out_proj_bwd", grid=(t // tm,),
        in_specs=[row(d), row(d), vec, pl.BlockSpec((dmix, d), lambda i: (0, 0))],
        out_specs=[row(dn), row(sg), row(cv), row(d), vec],
        out_shape=[jax.ShapeDtypeStruct((t, dn), F32), jax.ShapeDtypeStruct((t, sg), F32),
                   jax.ShapeDtypeStruct((t, cv), F32), jax.ShapeDtypeStruct((t, d), BF16),
                   jax.ShapeDtypeStruct((1, d), F32)],
        compiler_params=_cp(("arbitrary",)),
    )(dxn, y, gate, wo)


def matmul_acc(name, at, b):
    m, t = at.shape
    n = b.shape[1]
    tm, tn, tk = _tile(m, 1024, LANES), _tile(n, 2432, LANES), _tile(t, 1024, LANES)

    def body(a_ref, b_ref, o_ref):
        @pl.when(pl.program_id(2) == 0)
        def _():
            o_ref[...] = jnp.zeros_like(o_ref)
        o_ref[...] += jnp.dot(a_ref[...], b_ref[...], preferred_element_type=F32)

    return pl.pallas_call(
        body, name=name, grid=(m // tm, n // tn, t // tk),
        in_specs=[pl.BlockSpec((tm, tk), lambda i, j, k: (i, k)), pl.BlockSpec((tk, tn), lambda i, j, k: (k, j))],
        out_specs=pl.BlockSpec((tm, tn), lambda i, j, k: (i, j)),
        out_shape=jax.ShapeDtypeStruct((m, n), F32),
        compiler_params=_cp(("parallel", "parallel", "arbitrary")),
    )(at, b)


def in_proj_bwd(dp, wp, x, dxn, shift, scale, ng):
    t, d = x.shape
    npc = wp.shape[1]
    tm, tk = _tile(t, 512, LANES), _tile(npc, 2432, LANES)

    def mm_body(dp_ref, w_ref, dh_ref):
        @pl.when(pl.program_id(1) == 0)
        def _():
            dh_ref[...] = jnp.zeros_like(dh_ref)
        dh_ref[...] += lax.dot_general(dp_ref[...], w_ref[...], (((1,), (1,)), ((), ())), preferred_element_type=F32)

    dh = pl.pallas_call(
        mm_body, name="in_proj_bwd", grid=(t // tm, npc // tk),
        in_specs=[pl.BlockSpec((tm, tk), lambda i, k: (i, k)), pl.BlockSpec((d, tk), lambda i, k: (0, k))],
        out_specs=pl.BlockSpec((tm, d), lambda i, k: (i, 0)), out_shape=jax.ShapeDtypeStruct((t, d), F32),
        compiler_params=_cp(("parallel", "arbitrary")),
    )(dp, wp)

    tr = _tile(t, 256, 8)

    def norm_body(dh_ref, x_ref, dxn_ref, sh_ref, sc_ref, g_ref, dx_ref, dg_ref, dsc_ref, dsh_ref):
        @pl.when(pl.program_id(0) == 0)
        def _():
            for r in (dg_ref, dsc_ref, dsh_ref):
                r[...] = jnp.zeros_like(r)
        _, vj = jax.vjp(_modnorm, x_ref[...], g_ref[...], sc_ref[...], sh_ref[...])
        dx, dg, dsc, dsh = vj(dh_ref[...])
        dx_ref[...] = dxn_ref[...] + dx
        dg_ref[...] += dg
        dsc_ref[...] += dsc
        dsh_ref[...] += dsh

    vec = pl.BlockSpec((1, d), lambda i: (0, 0))
    row = pl.BlockSpec((tr, d), lambda i: (i, 0))
    return pl.pallas_call(
        norm_body, name="modnorm_bwd", grid=(t // tr,), in_specs=[row, row, row, vec, vec, vec],
        out_specs=[row, vec, vec, vec],
        out_shape=[jax.ShapeDtypeStruct((t, d), F32)] + [jax.ShapeDtypeStruct((1, d), F32)] * 3,
        compiler_params=_cp(("arbitrary",)),
    )(dh, x, dxn, shift, scale, ng)


def loss_head(x, tgt, fg):
    t, d = x.shape
    tm = _tile(t, 512, 8)

    def body(x_ref, t_ref, g_ref, l_ref, dx_ref, dg_ref):
        @pl.when(pl.program_id(0) == 0)
        def _():
            l_ref[...] = jnp.zeros_like(l_ref)
            dg_ref[...] = jnp.zeros_like(dg_ref)
        y, vj = jax.vjp(_rmsnorm, x_ref[...], g_ref[...])
        err = y - t_ref[...]
        part = 0.5 * jnp.sum(jnp.sum(err * err, axis=1, keepdims=True), axis=0, keepdims=True) / d
        l_ref[...] += jnp.broadcast_to(part, l_ref.shape)
        dx, dg = vj(err / d)
        dx_ref[...] = dx
        dg_ref[...] += dg

    row = pl.BlockSpec((tm, d), lambda i: (i, 0))
    vec = pl.BlockSpec((1, d), lambda i: (0, 0))
    return pl.pallas_call(
        body, name="loss_head", grid=(t // tm,), in_specs=[row, row, vec],
        out_specs=[pl.BlockSpec((1, LANES), lambda i: (0, 0)), row, vec],
        out_shape=[jax.ShapeDtypeStruct((1, LANES), F32), jax.ShapeDtypeStruct((t, d), F32),
                   jax.ShapeDtypeStruct((1, d), F32)],
        compiler_params=_cp(("arbitrary",)),
    )(x, tgt, fg)


def adamw(name, w, g, m, v):
    r, c = w.shape
    tr = _tile(r, 256, 8) if r % 8 == 0 else r

    def body(w_ref, g_ref, m_ref, v_ref, d_ref, mo_ref, vo_ref):
        d_ref[...], mo_ref[...], vo_ref[...] = _adamw_math(w_ref[...], g_ref[...], m_ref[...], v_ref[...])

    blk = pl.BlockSpec((tr, c), lambda i: (i, 0))
    return pl.pallas_call(
        body, name=name, grid=(r // tr,), in_specs=[blk] * 4, out_specs=[blk] * 3,
        out_shape=[jax.ShapeDtypeStruct((r, c), F32)] * 3, compiler_params=_cp(("parallel",)),
    )(w, g, m, v)


def ada_fwd(c_all, w_ada, b_loc):
    nl, d, cols = w_ada.shape
    nb = c_all.shape[0]
    tn = _tile(cols, 512, LANES)

    def body(c_ref, w_ref, b_ref, o_ref):
        ca = _silu(c_ref[...]).astype(BF16)
        o_ref[0] = jnp.dot(ca, w_ref[0].astype(BF16), preferred_element_type=F32) + b_ref[0]

    return pl.pallas_call(
        body, name="ada_fwd", grid=(nl, cols // tn),
        in_specs=[pl.BlockSpec((nb, d), lambda l, j: (0, 0)), pl.BlockSpec((1, d, tn), lambda l, j: (l, 0, j)),
                  pl.BlockSpec((1, 1, tn), lambda l, j: (l, 0, j))],
        out_specs=pl.BlockSpec((1, nb, tn), lambda l, j: (l, 0, j)),
        out_shape=jax.ShapeDtypeStruct((nl, nb, cols), F32),
        compiler_params=_cp(("parallel", "parallel")),
    )(c_all, w_ada, b_loc)


def ada_bwd(c_all_t, dmod_loc, w, m, v):
    nl, d, cols = w.shape
    nb = c_all_t.shape[1]
    tr = _tile(d, 256, 8)

    def body(c_ref, dm_ref, w_ref, m_ref, v_ref, g_ref, d_ref, mo_ref, vo_ref):
        ca = _silu(c_ref[...])
        dm = dm_ref[0]
        g = _lane_col(ca, 0) * dm[0:1, :]
        for b in range(1, nb):
            g = g + _lane_col(ca, b) * dm[b:b + 1, :]
        g_ref[0] = g
        d_ref[0], mo_ref[0], vo_ref[0] = _adamw_math(w_ref[0], g, m_ref[0], v_ref[0])

    blk = pl.BlockSpec((1, tr, cols), lambda l, i: (l, i, 0))
    return pl.pallas_call(
        body, name="ada_bwd", grid=(nl, d // tr),
        in_specs=[pl.BlockSpec((tr, nb), lambda l, i: (i, 0)), pl.BlockSpec((1, nb, cols), lambda l, i: (l, 0, 0)),
                  blk, blk, blk],
        out_specs=[blk] * 4, out_shape=[jax.ShapeDtypeStruct((nl, d, cols), F32)] * 4,
        compiler_params=_cp(("parallel", "parallel")),
    )(c_all_t, dmod_loc, w, m, v)


def sum8(g):
    _, r, c = g.shape
    tr = _tile(r, 256, 8)

    def body(g_ref, o_ref):
        acc = g_ref[0]
        for k in range(1, NDEV):
            acc = acc + g_ref[k]
        o_ref[...] = acc

    return pl.pallas_call(
        body, name="sum8", grid=(r // tr,), in_specs=[pl.BlockSpec((NDEV, tr, c), lambda i: (0, i, 0))],
        out_specs=pl.BlockSpec((tr, c), lambda i: (i, 0)), out_shape=jax.ShapeDtypeStruct((r, c), F32),
        compiler_params=_cp(("parallel",)),
    )(g)


def pair_sum(own, r1):
    ns, r, c = own.shape
    tr = _tile(r, 256, 8)

    def body(g_ref, r_ref, ob_ref):
        ob_ref[...] = (g_ref[...] + r_ref[...]).astype(BF16)

    blk = pl.BlockSpec((1, tr, c), lambda s, i: (s, i, 0))
    return pl.pallas_call(
        body, name="pair_sum", grid=(ns, r // tr), in_specs=[blk, blk], out_specs=blk,
        out_shape=jax.ShapeDtypeStruct((ns, r, c), BF16), compiler_params=_cp(("parallel", "parallel")),
    )(own, r1)


def chip_sum(own, r1, r2):
    r, c = own.shape
    tr = _tile(r, 256, 8)

    def body(g_ref, r1_ref, r2_ref, o_ref):
        acc = g_ref[...] + r1_ref[...]
        for k in range(NCHIP - 1):
            acc = acc + r2_ref[k].astype(F32)
        o_ref[...] = acc

    blk = pl.BlockSpec((tr, c), lambda i: (i, 0))
    return pl.pallas_call(
        body, name="chip_sum", grid=(r // tr,),
        in_specs=[blk, blk, pl.BlockSpec((NCHIP - 1, tr, c), lambda i: (0, i, 0))], out_specs=blk,
        out_shape=jax.ShapeDtypeStruct((r, c), F32), compiler_params=_cp(("parallel",)),
    )(own, r1, r2)


def adamw_halves(name, cflag, w, own, recv, m, v):
    _, r, c = w.shape
    tr = _tile(r, 256, 8)

    def body(cf_ref, w_ref, a_ref, b_ref, m_ref, v_ref, g_ref, d_ref, mo_ref, vo_ref):
        is_own = cf_ref[0:1, 0:1] == pl.program_id(0).astype(F32)
        g = jnp.where(is_own, a_ref[...], b_ref[...])
        g_ref[0] = g
        d_ref[0], mo_ref[0], vo_ref[0] = _adamw_math(w_ref[0], g, m_ref[0], v_ref[0])

    blk = pl.BlockSpec((1, tr, c), lambda h, i: (h, i, 0))
    hlf = pl.BlockSpec((tr, c), lambda h, i: (i, 0))
    return pl.pallas_call(
        body, name=name, grid=(2, r // tr),
        in_specs=[pl.BlockSpec((1, LANES), lambda h, i: (0, 0)), blk, hlf, hlf, blk, blk], out_specs=[blk] * 4,
        out_shape=[jax.ShapeDtypeStruct(w.shape, F32)] * 4, compiler_params=_cp(("parallel", "parallel")),
    )(cflag, w, own, recv, m, v)


def _me():
    return lax.axis_index("x"), lax.axis_index("y"), lax.axis_index("c")


_FLIPS = ((1, 0), (0, 1), (1, 1))


def all_gather8(v):
    m_per, n = v.shape

    def body(x_ref, out_ref, send_sems, recv_sems, local_sem):
        x, y, c = _me()
        me, sibling = (x, y, c), (x, y, 1 - c)
        chips = [(x ^ fx, y ^ fy) for fx, fy in _FLIPS]

        def rows(px, py, pc):
            return out_ref.at[pl.ds((4 * px + 2 * py + pc) * m_per, m_per), :]

        def copy(k, block, to, src=None):
            return pltpu.make_async_remote_copy(
                src_ref=rows(*block) if src is None else src, dst_ref=rows(*block),
                send_sem=send_sems.at[k], recv_sem=recv_sems.at[k], device_id=to, device_id_type=MESH)

        mine = pltpu.make_async_copy(x_ref, rows(*me), local_sem)
        mine.start()
        first = [copy(0, me, sibling, src=x_ref)]
        first += [copy(1 + j, me, (*chip, c), src=x_ref) for j, chip in enumerate(chips)]
        for cp in first:
            cp.start()
        passed = [copy(4 + j, (*chip, c), sibling) for j, chip in enumerate(chips)]
        for j, chip in enumerate(chips):
            copy(1 + j, (*chip, c), me).wait_recv()
            passed[j].start()
        copy(0, sibling, me).wait_recv()
        for j, chip in enumerate(chips):
            copy(4 + j, (*chip, 1 - c), me).wait_recv()
        for cp in first + passed:
            cp.wait_send()
        mine.wait()

    return pl.pallas_call(
        body, name="all_gather8", out_shape=jax.ShapeDtypeStruct((NDEV * m_per, n), v.dtype),
        in_specs=[pl.BlockSpec(memory_space=pltpu.VMEM)], out_specs=pl.BlockSpec(memory_space=pltpu.VMEM),
        scratch_shapes=[pltpu.SemaphoreType.DMA((7,)), pltpu.SemaphoreType.DMA((7,)), pltpu.SemaphoreType.DMA],
        compiler_params=pltpu.CompilerParams(vmem_limit_bytes=VMEM_LIMIT),
    )(v)


def gather_weights(ws):
    na = len(ws)

    def body(*refs):
        srcs, outs = refs[:na], refs[na:2 * na]
        send_sems, recv_sems = refs[2 * na:]
        x, y, c = _me()
        chip = 2 * x + y
        sibling = (x, y, 1 - c)
        peers = [(x ^ fx, y ^ fy) for fx, fy in _FLIPS]

        def ici(a, j, half_of, to):
            return pltpu.make_async_remote_copy(
                src_ref=srcs[a].at[c], dst_ref=outs[a].at[half_of, c], send_sem=send_sems.at[a * 6 + j],
                recv_sem=recv_sems.at[a * 6 + j], device_id=to, device_id_type=MESH)

        def d2d(a, j, slot, half):
            return pltpu.make_async_remote_copy(
                src_ref=outs[a].at[slot, half], dst_ref=outs[a].at[slot, half], send_sem=send_sems.at[a * 6 + 3 + j],
                recv_sem=recv_sems.at[a * 6 + 3 + j], device_id=sibling, device_id_type=MESH)

        sends = []
        for a in range(na):
            for j, (px, py) in enumerate(peers):
                cp = ici(a, j, chip, (px, py, c))
                cp.start()
                sends.append(cp)
        for a in range(na):
            for j, (px, py) in enumerate(peers):
                ici(a, j, 2 * px + py, (px, py, c)).wait_recv()
                cp = d2d(a, j, 2 * px + py, c)
                cp.start()
                sends.append(cp)
        for a in range(na):
            for j, (px, py) in enumerate(peers):
                d2d(a, j, 2 * px + py, 1 - c).wait_recv()
        for cp in sends:
            cp.wait_send()

    return pl.pallas_call(
        body, name="gather_weights",
        out_shape=[jax.ShapeDtypeStruct((NCHIP,) + w.shape, w.dtype) for w in ws],
        in_specs=[ANY] * na, out_specs=[ANY] * na,
        scratch_shapes=[pltpu.SemaphoreType.DMA((6 * na,)), pltpu.SemaphoreType.DMA((6 * na,))],
    )(*ws)


def sibling_swap(name, gs):
    na = len(gs)

    def body(*refs):
        srcs, outs = refs[:na], refs[na:2 * na]
        send_sems, recv_sems = refs[2 * na:]
        x, y, c = _me()
        cps = [pltpu.make_async_remote_copy(
            src_ref=srcs[a], dst_ref=outs[a], send_sem=send_sems.at[a], recv_sem=recv_sems.at[a],
            device_id=(x, y, 1 - c), device_id_type=MESH) for a in range(na)]
        for cp in cps:
            cp.start()
        for cp in cps:
            cp.wait()

    return pl.pallas_call(
        body, name=name, out_shape=[jax.ShapeDtypeStruct(g.shape, g.dtype) for g in gs],
        in_specs=[ANY] * na, out_specs=[ANY] * na,
        scratch_shapes=[pltpu.SemaphoreType.DMA((na,)), pltpu.SemaphoreType.DMA((na,))],
    )(*gs)


def chip_exchange(ps):
    na = len(ps)

    def body(*refs):
        srcs, outs = refs[:na], refs[na:2 * na]
        send_sems, recv_sems = refs[2 * na:]
        x, y, c = _me()
        cps = []
        for a in range(na):
            for j, (fx, fy) in enumerate(_FLIPS):
                px, py = x ^ fx, y ^ fy
                cps.append(pltpu.make_async_remote_copy(
                    src_ref=srcs[a].at[2 * px + py], dst_ref=outs[a].at[j], send_sem=send_sems.at[a * 3 + j],
                    recv_sem=recv_sems.at[a * 3 + j], device_id=(px, py, c), device_id_type=MESH))
        for cp in cps:
            cp.start()
        for cp in cps:
            cp.wait()

    return pl.pallas_call(
        body, name="chip_exchange",
        out_shape=[jax.ShapeDtypeStruct((NCHIP - 1,) + p_.shape[1:], p_.dtype) for p_ in ps],
        in_specs=[ANY] * na, out_specs=[ANY] * na,
        scratch_shapes=[pltpu.SemaphoreType.DMA((3 * na,)), pltpu.SemaphoreType.DMA((3 * na,))],
    )(*ps)


class _Cfg:
    def __init__(self, x, a_log, sg_w, cv_ln_g, cv_w, conv_qkv):
        self.t, self.d = x.shape[1], x.shape[2]
        self.nl, self.h = a_log.shape
        self.dn = self.h * LANES
        self.g = sg_w.shape[1]
        self.sg = self.g * LANES
        self.cv = cv_ln_g.shape[1]
        self.kc = cv_w.shape[1]
        self.k4 = conv_qkv.shape[1]
        self.o_z = 3 * self.dn
        self.o_sg = 4 * self.dn
        self.o_cv = self.o_sg + 3 * self.sg
        self.o_ba = self.o_cv + 3 * self.cv
        self.npc = self.o_ba + LANES
        self.d_in = self.o_ba + 2 * self.h
        self.dmix = self.dn + self.sg + self.cv
        self.hb_fwd = _tile(self.h, 8, 1)
        self.hb_bwd = _tile(self.h, 4, 1)


def _perm_cols(cfg, w):
    dn, h = cfg.dn, cfg.h
    lead = w.shape[:-1]
    qkv = w[..., :3 * dn].reshape(lead + (3, h, LANES))
    qkv = jnp.moveaxis(qkv, -3, -2).reshape(lead + (3 * dn,))
    ba = w[..., 4 * dn:4 * dn + 2 * h]
    pad = jnp.zeros(lead + (LANES - 2 * h,), w.dtype)
    return jnp.concatenate([qkv, w[..., 3 * dn:4 * dn], w[..., 4 * dn + 2 * h:], ba, pad], axis=-1)


def _unperm_cols(cfg, w):
    dn, h = cfg.dn, cfg.h
    lead = w.shape[:-1]
    qkv = w[..., :3 * dn].reshape(lead + (h, 3, LANES))
    qkv = jnp.moveaxis(qkv, -3, -2).reshape(lead + (3 * dn,))
    return jnp.concatenate([qkv, w[..., 3 * dn:4 * dn], w[..., cfg.o_ba:cfg.o_ba + 2 * h], w[..., 4 * dn:cfg.o_ba]], axis=-1)


def _layer_fwd(cfg, x, mod, lw):
    shift, scale, gate = mod
    p, ht = in_proj(x, shift, scale, lw["norm_g"], lw["wp"])
    qk_post = [_qk_post, _qk_post, _v_post]
    qkv = conv_fwd("dn_pre_fwd", cfg.k4, HALO4, lambda a: a, [(p, 0)], lw["conv_qkv"], qk_post, [], [],
                   3 * cfg.dn, 3 * LANES)
    wy = dn_wy_fwd(qkv, p, cfg.o_ba, lw["alog_b"], lw["dtb_b"], cfg.h)
    y_dn, ss = dn_seq_fwd(*wy, p, cfg.o_z, lw["dn_norm_g"], cfg.h, cfg.hb_fwd)
    y_sg = sg_fwd(p, cfg.o_sg, cfg.sg, lw["sg_ln_g"], lw["sg_ln_b"], lw["sg_w"], lw["sg_bias_b"])
    cv_post = [_cv_post] * (cfg.cv // LANES)
    y_cv = conv_fwd("cv_fwd", cfg.kc, HALO31, _glu, [(p, cfg.o_cv), (p, cfg.o_cv + cfg.cv)], lw["cv_w"], cv_post,
                    [(p, cfg.o_cv + 2 * cfg.cv)], [lw["cv_b"], lw["cv_ln_g"], lw["cv_ln_b"]], cfg.cv, cfg.cv)
    xn, y, yt = out_proj(x, y_dn, y_sg, y_cv, lw["wo"], gate)
    return xn, dict(x=x, p=p, ht=ht, qkv=qkv, wy=wy, ss=ss, y=y, yt=yt)


def _layer_bwd(cfg, dxn, mod, lw, sv):
    shift, scale, gate = mod
    p = sv["p"]
    d_dn, d_sg, d_cv, dyb, dgate = out_proj_bwd(dxn, sv["y"], gate, lw["wo"], cfg.dn, cfg.sg, cfg.cv)
    g_wo = matmul_acc("w_out_grad", sv["yt"], dyb)
    cv_post = [_cv_post] * (cfg.cv // LANES)
    dcv, g_cvw, (g_cvb, g_cvlg, g_cvlb) = conv_bwd(
        "cv_bwd", cfg.kc, HALO31, _glu, [(p, cfg.o_cv), (p, cfg.o_cv + cfg.cv)], lw["cv_w"], cv_post,
        [(p, cfg.o_cv + 2 * cfg.cv)], [lw["cv_b"], lw["cv_ln_g"], lw["cv_ln_b"]], d_cv, cfg.cv, cfg.cv, tm_pref=256)
    dsg, g_sglg, g_sglb, g_sgw, g_sgb = sg_bwd(p, cfg.o_sg, cfg.sg, lw["sg_ln_g"], lw["sg_ln_b"], lw["sg_w"],
                                               lw["sg_bias_b"], d_sg)
    *dwy, dz, g_dng = dn_seq_bwd(*sv["wy"], p, cfg.o_z, lw["dn_norm_g"], sv["ss"], d_dn, cfg.h, cfg.hb_bwd)
    dqkv, dba, g_al, g_dt = dn_wy_bwd(sv["qkv"], p, cfg.o_ba, lw["alog_b"], lw["dtb_b"], *dwy, cfg.h)
    qk_post = [_qk_post, _qk_post, _v_post]
    dqkv_pre, g_cq, _ = conv_bwd("dn_pre_bwd", cfg.k4, HALO4, lambda a: a, [(p, 0)], lw["conv_qkv"], qk_post, [], [],
                                 dqkv, 3 * cfg.dn, 3 * LANES)
    dp = jnp.concatenate([dqkv_pre, dz.astype(BF16), dsg, dcv, dba.astype(BF16)], axis=1)
    g_wp = matmul_acc("w_in_grad", sv["ht"], dp)
    dx, g_ng, dscale, dshift = in_proj_bwd(dp, lw["wp"], sv["x"], dxn, shift, scale, lw["norm_g"])
    grads = dict(norm_g=g_ng, conv_qkv=g_cq, a_log=g_al[:cfg.h, 0], dt_bias=g_dt[:cfg.h, 0], dn_norm_g=g_dng,
                 sg_ln_g=g_sglg, sg_ln_b=g_sglb, sg_w=g_sgw, sg_b=g_sgb[:, :, 0], cv_w=g_cvw, cv_b=g_cvb,
                 cv_ln_g=g_cvlg, cv_ln_b=g_cvlb, wp=g_wp, wo=g_wo)
    return dx, grads, (dshift, dscale, dgate)


def _local_step(cfg, xs, tgt, mods, lws, fg):
    nl = len(lws)
    saved = []
    for l in range(nl):
        xs, sv = _layer_fwd(cfg, xs, mods[l], lws[l])
        saved.append(sv)
    loss_b, dx, g_fg = loss_head(xs, tgt, fg)
    lg = [None] * nl
    dmods = [None] * nl
    for l in reversed(range(nl)):
        dx, lg[l], dmods[l] = _layer_bwd(cfg, dx, mods[l], lws[l], saved[l])
    return loss_b, dx, g_fg, lg, dmods


SMALL = ("norm_g", "conv_qkv", "a_log", "dt_bias", "dn_norm_g", "sg_ln_g", "sg_ln_b", "sg_w", "sg_b", "cv_w",
         "cv_b", "cv_ln_g", "cv_ln_b", "final_g", "b_ada")
PACK_N = 1024


def _pack(arrs):
    flat = jnp.concatenate([a.reshape(-1).astype(F32) for a in arrs])
    rows = -(-flat.shape[0] // PACK_N)
    rows = -(-rows // 8) * 8
    return jnp.pad(flat, (0, rows * PACK_N - flat.shape[0])).reshape(rows, PACK_N)


def _unpack(buf, shapes):
    flat = buf.reshape(-1)
    out, o = [], 0
    for s in shapes:
        n = 1
        for d_ in s:
            n *= d_
        out.append(flat[o:o + n].reshape(s))
        o += n
    return out


def kernel(x, c, norm_g, w_ada, b_ada, w_in, conv_qkv, a_log, dt_bias, dn_norm_g, sg_ln_g, sg_ln_b, sg_w, sg_b, cv_w, cv_b, cv_ln_g, cv_ln_b, w_out, final_g, loss_target, m_norm_g, m_w_ada, m_b_ada, m_w_in, m_conv_qkv, m_a_log, m_dt_bias, m_dn_norm_g, m_sg_ln_g, m_sg_ln_b, m_sg_w, m_sg_b, m_cv_w, m_cv_b, m_cv_ln_g, m_cv_ln_b, m_w_out, m_final_g, v_norm_g, v_w_ada, v_b_ada, v_w_in, v_conv_qkv, v_a_log, v_dt_bias, v_dn_norm_g, v_sg_ln_g, v_sg_ln_b, v_sg_w, v_sg_b, v_cv_w, v_cv_b, v_cv_ln_g, v_cv_ln_b, v_w_out, v_final_g):
    cfg = _Cfg(x, a_log, sg_w, cv_ln_g, cv_w, conv_qkv)
    nl, d, t, h = cfg.nl, cfg.d, cfg.t, cfg.h
    lh = nl // 2
    ax, ay, ac = _me()
    chip = 2 * ax + ay
    dev = 2 * chip + ac
    wts = dict(norm_g=norm_g, w_ada=w_ada, b_ada=b_ada, w_in=w_in, conv_qkv=conv_qkv, a_log=a_log, dt_bias=dt_bias,
               dn_norm_g=dn_norm_g, sg_ln_g=sg_ln_g, sg_ln_b=sg_ln_b, sg_w=sg_w, sg_b=sg_b, cv_w=cv_w, cv_b=cv_b,
               cv_ln_g=cv_ln_g, cv_ln_b=cv_ln_b, w_out=w_out, final_g=final_g)
    mom = dict(norm_g=m_norm_g, w_ada=m_w_ada, b_ada=m_b_ada, w_in=m_w_in, conv_qkv=m_conv_qkv, a_log=m_a_log,
               dt_bias=m_dt_bias, dn_norm_g=m_dn_norm_g, sg_ln_g=m_sg_ln_g, sg_ln_b=m_sg_ln_b, sg_w=m_sg_w,
               sg_b=m_sg_b, cv_w=m_cv_w, cv_b=m_cv_b, cv_ln_g=m_cv_ln_g, cv_ln_b=m_cv_ln_b, w_out=m_w_out,
               final_g=m_final_g)
    vel = dict(norm_g=v_norm_g, w_ada=v_w_ada, b_ada=v_b_ada, w_in=v_w_in, conv_qkv=v_conv_qkv, a_log=v_a_log,
               dt_bias=v_dt_bias, dn_norm_g=v_dn_norm_g, sg_ln_g=v_sg_ln_g, sg_ln_b=v_sg_ln_b, sg_w=v_sg_w,
               sg_b=v_sg_b, cv_w=v_cv_w, cv_b=v_cv_b, cv_ln_g=v_cv_ln_g, cv_ln_b=v_cv_ln_b, w_out=v_w_out,
               final_g=v_final_g)
    ada_cols = w_ada.shape[2]
    in_cols = w_in.shape[2]
    out_rows = w_out.shape[1]
    cq_cols = conv_qkv.shape[2]
    cvw_cols = cv_w.shape[2]

    c_all = all_gather8(jnp.pad(c, ((0, 7), (0, 0)))).reshape(NDEV, 8, d)[:, 0, :]
    b_loc = lax.dynamic_slice_in_dim(b_ada, chip * ada_cols, ada_cols, axis=1)[:, None, :]
    mod_part = ada_fwd(c_all, w_ada, b_loc)
    mod_all = all_gather8(mod_part.reshape(nl * NDEV, ada_cols)).reshape(NDEV, nl, NDEV, ada_cols)
    mod_me = lax.dynamic_index_in_dim(mod_all[0::2], dev, axis=2, keepdims=False)
    mod_me = jnp.moveaxis(mod_me, 0, 1).reshape(nl, 3, 1, d)

    win_b = w_in.astype(BF16).reshape(2, lh, d, in_cols)
    wout_b = w_out.astype(BF16).reshape(2, lh, out_rows, d)
    win_all, wout_all = gather_weights([win_b, wout_b])
    win_all = lax.dynamic_update_index_in_dim(win_all, win_b, chip, axis=0)
    wout_all = lax.dynamic_update_index_in_dim(wout_all, wout_b, chip, axis=0)
    win_full = jnp.moveaxis(win_all.reshape(NCHIP, nl, d, in_cols), 0, 2).reshape(nl, d, NCHIP * in_cols)
    wp_all = _perm_cols(cfg, win_full)
    wo_all = jnp.moveaxis(wout_all.reshape(NCHIP, nl, out_rows, d), 0, 1).reshape(nl, NCHIP * out_rows, d)

    cq_all = all_gather8(conv_qkv.reshape(nl * cfg.k4, cq_cols)).reshape(NDEV, nl, cfg.k4, cq_cols)[0::2]
    cq_full = jnp.moveaxis(cq_all, 0, 2).reshape(nl, cfg.k4, NCHIP * cq_cols)
    cq_perm = _perm_cols_qkv(cfg, cq_full)
    kcp = -(-cfg.kc // 8) * 8
    cvw_all = all_gather8(jnp.pad(cv_w, ((0, 0), (0, kcp - cfg.kc), (0, 0))).reshape(nl * kcp, cvw_cols))
    cvw_all = cvw_all.reshape(NDEV, nl, kcp, cvw_cols)[0::2]
    cvw_full = jnp.moveaxis(cvw_all, 0, 2).reshape(nl, kcp, NCHIP * cvw_cols)[:, :cfg.kc]

    hp = -(-h // 8) * 8
    lws = []
    for l in range(nl):
        lws.append(dict(
            norm_g=norm_g[l][None], wp=wp_all[l], wo=wo_all[l], conv_qkv=cq_perm[l],
            alog_b=jnp.pad(jnp.broadcast_to(a_log[l][:, None], (h, LANES)), ((0, hp - h), (0, 0))),
            dtb_b=jnp.pad(jnp.broadcast_to(dt_bias[l][:, None], (h, LANES)), ((0, hp - h), (0, 0))),
            dn_norm_g=dn_norm_g[l][None], sg_ln_g=sg_ln_g[l][None], sg_ln_b=sg_ln_b[l][None], sg_w=sg_w[l],
            sg_bias_b=jnp.broadcast_to(sg_b[l][:, :, None], (cfg.g, LANES, LANES)),
            cv_w=cvw_full[l], cv_b=cv_b[l][None], cv_ln_g=cv_ln_g[l][None], cv_ln_b=cv_ln_b[l][None]))

    mods = [(mod_me[l, 0], mod_me[l, 1], mod_me[l, 2]) for l in range(nl)]
    loss_b, dx, g_fg, lg, dmods = _local_step(cfg, x[0], loss_target[0], mods, lws, final_g[None])
    grad_x = dx[None]

    dmod = jnp.stack([jnp.concatenate(dm, axis=1)[0] for dm in dmods])
    stack = lambda k: jnp.stack([g_[k] for g_ in lg])
    small_local = [stack(k).reshape(wts_shape) for k, wts_shape in
                   (("norm_g", (nl, d)), ("conv_qkv", (nl, cfg.k4, 3 * cfg.dn)), ("a_log", (nl, h)),
                    ("dt_bias", (nl, h)), ("dn_norm_g", (nl, LANES)), ("sg_ln_g", (nl, cfg.sg)),
                    ("sg_ln_b", (nl, cfg.sg)), ("sg_w", (nl, cfg.g, LANES, LANES)), ("sg_b", (nl, cfg.g, LANES)),
                    ("cv_w", (nl, cfg.kc, cfg.cv)), ("cv_b", (nl, cfg.cv)), ("cv_ln_g", (nl, cfg.cv)),
                    ("cv_ln_b", (nl, cfg.cv)))]
    small_local[1] = _unperm_cols_qkv(cfg, small_local[1])
    small_local += [g_fg[0], dmod, loss_b[0, 0:1]]
    shapes = [a.shape for a in small_local]
    packed = _pack(small_local)
    rows = packed.shape[0]
    gathered = all_gather8(packed).reshape(NDEV, rows, PACK_N)
    summed = _unpack(sum8(gathered), shapes)
    sgrads = dict(zip(SMALL, summed[:15]))
    loss = summed[15][0]
    sgrads["conv_qkv"] = lax.dynamic_slice_in_dim(sgrads["conv_qkv"], chip * cq_cols, cq_cols, axis=2)
    sgrads["cv_w"] = lax.dynamic_slice_in_dim(sgrads["cv_w"], chip * cvw_cols, cvw_cols, axis=2)

    off = sum(math.prod(s) for s in shapes[:14])
    dmod_all = gathered.reshape(NDEV, rows * PACK_N)[:, off:off + nl * 3 * d].reshape(NDEV, nl, 3 * d)
    dmod_loc = jnp.moveaxis(lax.dynamic_slice_in_dim(dmod_all, chip * ada_cols, ada_cols, axis=2), 0, 1)
    g_wada, d_wada, nm_wada, nv_wada = ada_bwd(c_all.T, dmod_loc, w_ada, m_w_ada, v_w_ada)

    g_wp = _unperm_cols(cfg, jnp.stack([g_["wp"] for g_ in lg]))
    g_wp = g_wp.reshape(2, lh, d, NCHIP, in_cols)
    g_wo = jnp.stack([g_["wo"] for g_ in lg]).reshape(2, lh, NCHIP, out_rows, d)
    half = lambda g_, which: lax.dynamic_index_in_dim(g_, which, axis=0, keepdims=False)
    by_chip_in = lambda g_: jnp.moveaxis(g_, 2, 0).reshape(NCHIP, lh * d, in_cols)
    by_chip_out = lambda g_: jnp.moveaxis(g_, 1, 0).reshape(NCHIP, lh * out_rows, d)
    keep_in, send_in = by_chip_in(half(g_wp, ac)), by_chip_in(half(g_wp, 1 - ac))
    keep_out, send_out = by_chip_out(half(g_wo, ac)), by_chip_out(half(g_wo, 1 - ac))
    r1_in, r1_out = sibling_swap("swap_halves", [send_in, send_out])
    r2_in, r2_out = chip_exchange([pair_sum(keep_in, r1_in), pair_sum(keep_out, r1_out)])
    mine = lambda g_: lax.dynamic_index_in_dim(g_, chip, axis=0, keepdims=False)
    h_in = chip_sum(mine(keep_in), mine(r1_in), r2_in)
    h_out = chip_sum(mine(keep_out), mine(r1_out), r2_out)
    o_in, o_out = sibling_swap("join_halves", [h_in, h_out])

    cflag = jnp.full((1, LANES), ac, F32)
    v3 = lambda a, r_, c_: a.reshape(2, lh * r_, c_)
    grad_w_in, d_in_, nm_in, nv_in = adamw_halves("adamw_w_in", cflag, v3(w_in, d, in_cols), h_in, o_in,
                                                  v3(m_w_in, d, in_cols), v3(v_w_in, d, in_cols))
    grad_w_out, d_out_, nm_out, nv_out = adamw_halves("adamw_w_out", cflag, v3(w_out, out_rows, d), h_out, o_out,
                                                      v3(m_w_out, out_rows, d), v3(v_w_out, out_rows, d))
    grad_w_in = grad_w_in.reshape(w_in.shape)
    grad_w_out = grad_w_out.reshape(w_out.shape)
    sshapes = [wts[k].shape for k in SMALL]
    pk = lambda dct: _pack([dct[k] for k in SMALL])
    d_s, m_s, v_s = adamw("adamw_small", pk(wts), pk(sgrads), pk(mom), pk(vel))
    d_small = dict(zip(SMALL, _unpack(d_s, sshapes)))
    m_small = dict(zip(SMALL, _unpack(m_s, sshapes)))
    v_small = dict(zip(SMALL, _unpack(v_s, sshapes)))

    grads = dict(sgrads, w_ada=g_wada, w_in=grad_w_in, w_out=grad_w_out)
    deltas = dict(d_small, w_ada=d_wada, w_in=d_in_.reshape(w_in.shape), w_out=d_out_.reshape(w_out.shape))
    new_m = dict(m_small, w_ada=nm_wada, w_in=nm_in.reshape(w_in.shape), w_out=nm_out.reshape(w_out.shape))
    new_v = dict(v_small, w_ada=nv_wada, w_in=nv_in.reshape(w_in.shape), w_out=nv_out.reshape(w_out.shape))
    order = ("norm_g", "w_ada", "b_ada", "w_in", "conv_qkv", "a_log", "dt_bias", "dn_norm_g", "sg_ln_g", "sg_ln_b",
             "sg_w", "sg_b", "cv_w", "cv_b", "cv_ln_g", "cv_ln_b", "w_out", "final_g")
    return (loss, grad_x, *[grads[k] for k in order], *[deltas[k] for k in order], *[new_m[k] for k in order],
            *[new_v[k] for k in order])


def _perm_cols_qkv(cfg, w):
    lead = w.shape[:-1]
    return jnp.moveaxis(w.reshape(lead + (3, cfg.h, LANES)), -3, -2).reshape(lead + (3 * cfg.dn,))


def _unperm_cols_qkv(cfg, w):
    lead = w.shape[:-1]
    return jnp.moveaxis(w.reshape(lead + (cfg.h, 3, LANES)), -3, -2).reshape(lead + (3 * cfg.dn,))
```

```python
import functools
import math

import jax
import jax.numpy as jnp
from jax import lax
from jax.experimental import pallas as pl
from jax.experimental.pallas import tpu as pltpu

F32 = jnp.float32
BF16 = jnp.bfloat16
EPS = 1e-6
LN_EPS = 1e-5
LANES = 128
CHUNK = 64
SUBLANES = 8
HALO4 = 8
HALO31 = 32
NCHIP = 4
NDEV = 8
VMEM_LIMIT = 56 * 2 ** 20
ADAM_LR, ADAM_B1, ADAM_B2, ADAM_EPS, ADAM_WD, ADAM_STEP = 0.001, 0.9, 0.999, 1e-08, 0.01, 10
MESH = pl.DeviceIdType.MESH
ANY = pl.BlockSpec(memory_space=pl.ANY)


def _cp(sem=None, vmem=VMEM_LIMIT):
    return pltpu.CompilerParams(dimension_semantics=sem, vmem_limit_bytes=vmem)


def _tile(n, pref, mult):
    t = min(n, pref) // mult * mult
    while t > 0 and n % t:
        t -= mult
    return t if t > 0 else n


def _split(a):
    hi = a.astype(BF16)
    return hi, (a - hi.astype(F32)).astype(BF16)


def _raw_dot(a, b, ca, cb, hi):
    dn = (((ca,), (cb,)), ((), ()))
    if hi:
        ah, al = _split(a.astype(F32))
        bh, bl = _split(b.astype(F32))
        d3 = lambda x, y: lax.dot_general(x, y, dn, preferred_element_type=F32)
        return d3(ah, bh) + (d3(al, bh) + d3(ah, bl))
    return lax.dot_general(a.astype(BF16), b.astype(BF16), dn, preferred_element_type=F32)


@functools.partial(jax.custom_vjp, nondiff_argnums=(2, 3, 4))
def bdot(a, b, ca, cb, hi):
    return _raw_dot(a, b, ca, cb, hi)


def _bdot_fwd(a, b, ca, cb, hi):
    return _raw_dot(a, b, ca, cb, hi), (a, b)


def _bdot_bwd(ca, cb, hi, res, ct):
    a, b = res
    fa, fb = 1 - ca, 1 - cb
    da = _raw_dot(ct, b, 1, fb, hi) if ca == 1 else _raw_dot(b, ct, fb, 1, hi)
    db = _raw_dot(a, ct, fa, 0, hi) if cb == 0 else _raw_dot(ct, a, 0, fa, hi)
    return da.astype(a.dtype), db.astype(b.dtype)


bdot.defvjp(_bdot_fwd, _bdot_bwd)


def _sigmoid(x):
    return 1.0 / (1.0 + jnp.exp(-x))


def _silu(x):
    return x * _sigmoid(x)


def _gelu(x):
    return 0.5 * x * (1.0 + lax.erf(x * (2.0 ** -0.5)))


def _softplus(x):
    return jnp.maximum(x, 0.0) + jnp.log(1.0 + jnp.exp(-jnp.abs(x)))


def _modnorm(x, g, scale, shift):
    y = x * lax.rsqrt(jnp.mean(x * x, axis=-1, keepdims=True) + EPS)
    return (y * g) * (1.0 + scale) + shift


def _rmsnorm(x, g):
    return x * lax.rsqrt(jnp.mean(x * x, axis=-1, keepdims=True) + EPS) * g


def _layernorm(x, g, b):
    mu = jnp.mean(x, axis=-1, keepdims=True)
    xc = x - mu
    var = jnp.mean(xc * xc, axis=-1, keepdims=True)
    return xc * lax.rsqrt(var + LN_EPS) * g + b


def _l2norm(t):
    return t * lax.rsqrt(jnp.sum(t * t, axis=-1, keepdims=True) + EPS)


def _adamw_math(w, g, m, v):
    mn = ADAM_B1 * m + (1.0 - ADAM_B1) * g
    vn = ADAM_B2 * v + (1.0 - ADAM_B2) * (g * g)
    mh = mn / (1.0 - ADAM_B1 ** ADAM_STEP)
    vh = vn / (1.0 - ADAM_B2 ** ADAM_STEP)
    delta = -ADAM_LR * (mh / (jnp.sqrt(vh) + ADAM_EPS) + ADAM_WD * w)
    return delta, mn, vn


def _each(f, *lists):
    return [f(*xs) for xs in zip(*lists)]


def _wy(q, k, v, bcol, acol, alog, dtb):
    c = CHUNK
    r = lax.broadcasted_iota(jnp.int32, (c, c), 0)
    cc = lax.broadcasted_iota(jnp.int32, (c, c), 1)
    rr = lax.broadcasted_iota(jnp.int32, (c, 1), 0)
    tri_incl, tri_strict, eye = r >= cc, r > cc, r == cc
    beta = _each(_sigmoid, bcol)
    g = _each(lambda al, a_, dt: -jnp.exp(al) * _softplus(a_ + dt), alog, acol, dtb)
    gb = [jnp.broadcast_to(g_, (c, c)) for g_ in g]
    g_row = [jnp.sum(jnp.where(eye, b_, 0.0), axis=0, keepdims=True) for b_ in gb]
    gc_col = [jnp.sum(jnp.where(tri_incl, jnp.broadcast_to(gr, (c, c)), 0.0), axis=1, keepdims=True) for gr in g_row]
    gc_row = [jnp.sum(jnp.where(r <= cc, b_, 0.0), axis=0, keepdims=True) for b_ in gb]
    decay = _each(lambda gcc, gcr: jnp.where(tri_incl, jnp.exp(jnp.where(tri_incl, gcc - gcr, 0.0)), 0.0),
                  gc_col, gc_row)
    qs = [q_ * (q_.shape[-1] ** -0.5) for q_ in q]
    kb = _each(lambda k_, b_: k_ * b_, k, beta)
    a = _each(lambda kb_, k_, d_: jnp.where(tri_strict, bdot(kb_, k_, 1, 1, False) * d_, 0.0), kb, k, decay)
    dv = v[0].shape[-1]
    x = _each(lambda v_, b_, kb_, gcc: jnp.concatenate([v_ * b_, kb_ * jnp.exp(gcc)], axis=1), v, beta, kb, gc_col)
    inv = [jnp.where(eye, 1.0, 0.0) - a_ for a_ in a]
    p = a
    for _ in range(5):
        p = _each(lambda p_: bdot(p_, p_, 1, 0, True), p)
        inv = _each(lambda t_, p_: t_ + bdot(t_, p_, 1, 0, True), inv, p)
    x = _each(lambda t_, x_: bdot(t_, x_, 1, 0, True), inv, x)
    xv = [x_[:, :dv] for x_ in x]
    xk = [x_[:, dv:] for x_ in x]
    qk = _each(lambda q_, k_, d_: bdot(q_, k_, 1, 1, False) * d_, qs, k, decay)
    g_last = [jnp.sum(jnp.where(rr == c - 1, gcc, 0.0), axis=0, keepdims=True) for gcc in gc_col]
    qg = _each(lambda q_, gcc: q_ * jnp.exp(gcc), qs, gc_col)
    kd = _each(lambda k_, gl, gcc: k_ * jnp.exp(gl - gcc), k, g_last, gc_col)
    return xv, xk, qg, kd, qk, [jnp.exp(gl) for gl in g_last]


def _seq(u, w, qg, kd, qk, e, z, s, ng):
    v_new = _each(lambda u_, w_, s_: u_ - bdot(w_, s_, 1, 0, False), u, w, s)
    o1 = _each(lambda q_, s_: bdot(q_, s_, 1, 0, False), qg, s)
    o2 = _each(lambda qk_, vn: bdot(qk_, vn, 1, 0, False), qk, v_new)
    ds = _each(lambda kd_, vn: bdot(kd_, vn, 0, 0, False), kd, v_new)
    s_next = _each(lambda s_, e_, d_: s_ * e_ + d_, s, e, ds)
    y = _each(lambda a_, b_, z_: _rmsnorm(a_ + b_, ng) * _silu(z_), o1, o2, z)
    return y, s_next


def _sg_block(u, v, gt, lg, lb, w, bias):
    n = w.shape[0]
    pr = lax.broadcasted_iota(jnp.int32, (n, n), 0) // CHUNK
    pc = lax.broadcasted_iota(jnp.int32, (n, n), 1) // CHUNK
    wm = jnp.where(pr >= pc, w, 0.0)
    vl = _layernorm(_gelu(v), lg, lb)
    mixed = bdot(wm, vl, 1, 0, False) + bias
    return _gelu(u) * mixed * _silu(gt)


def _glu(a, b):
    return a * _sigmoid(b)


def _cv_post(conv, gate, cb, lg, lb):
    return _silu(_layernorm(conv + cb, lg, lb)) * _silu(gate)


def _qk_post(conv):
    return _l2norm(_silu(conv))


def _v_post(conv):
    return _silu(conv)


def in_proj(x, shift, scale, ng, wp):
    t, d = x.shape
    npc = wp.shape[1]
    tm, tn = _tile(t, 512, LANES), _tile(npc, 2432, LANES)

    def body(x_ref, sh_ref, sc_ref, g_ref, w_ref, p_ref, ht_ref, h_scr):
        @pl.when(pl.program_id(1) == 0)
        def _():
            h = _modnorm(x_ref[...], g_ref[...], sc_ref[...], sh_ref[...])
            h_scr[...] = h.astype(BF16)
            ht_ref[...] = h.T.astype(BF16)
        p_ref[...] = jnp.dot(h_scr[...], w_ref[...], preferred_element_type=F32)

    vec = pl.BlockSpec((1, d), lambda i, j: (0, 0))
    return pl.pallas_call(
        body, name="in_proj", grid=(t // tm, npc // tn),
        in_specs=[pl.BlockSpec((tm, d), lambda i, j: (i, 0)), vec, vec, vec,
                  pl.BlockSpec((d, tn), lambda i, j: (0, j))],
        out_specs=[pl.BlockSpec((tm, tn), lambda i, j: (i, j)), pl.BlockSpec((d, tm), lambda i, j: (0, i))],
        out_shape=[jax.ShapeDtypeStruct((t, npc), F32), jax.ShapeDtypeStruct((d, t), BF16)],
        scratch_shapes=[pltpu.VMEM((tm, d), BF16)],
        compiler_params=_cp(("parallel", "arbitrary")),
    )(x, shift, scale, ng, wp)


def _roll_bank(x, bank_ref, offsets):
    rows = x.shape[0]
    residues = sorted({o % SUBLANES for o in offsets})
    for slot, b in enumerate(residues):
        bank_ref[slot] = x if b == 0 else pltpu.roll(x, rows - b, 0)
    return {o: (residues.index(o % SUBLANES), o - o % SUBLANES) for o in offsets}


def _n_residues(offsets):
    return len({o % SUBLANES for o in offsets})


def conv_fwd(name, k, halo, pre_fn, pre, w, post_fns, extras, params, c_total, tc, tm_pref=512):
    t = pre[0][0].shape[0]
    tm = _tile(t, tm_pref, halo)
    npre, nex, npar = len(pre), len(extras), len(params)
    ngr = tc // LANES
    taps = [halo - (k - 1) + j for j in range(k)]

    def body(*refs):
        prev = refs[:npre]
        cur = refs[npre:2 * npre]
        w_ref = refs[2 * npre]
        ex = refs[2 * npre + 1:2 * npre + 1 + nex]
        par = refs[2 * npre + 1 + nex:2 * npre + 1 + nex + npar]
        out_ref, buf, bank = refs[-3], refs[-2], refs[-1]
        i = pl.program_id(1)
        pv = pre_fn(*[r[...] for r in prev])
        buf[0:halo, :] = jnp.where(i > 0, pv, 0.0)
        buf[halo:, :] = pre_fn(*[r[...] for r in cur])
        where = _roll_bank(buf[...], bank, taps)
        acc = None
        for j, o in enumerate(taps):
            slot, st = where[o]
            term = w_ref[j:j + 1, :] * bank[slot, st:st + tm, :]
            acc = term if acc is None else acc + term
        for gi in range(ngr):
            sl = slice(gi * LANES, (gi + 1) * LANES)
            out_ref[:, sl] = post_fns[gi](acc[:, sl], *[e[:, sl] for e in ex], *[p_[:, sl] for p_ in par])

    hb = tm // halo
    in_specs = ([pl.BlockSpec((halo, tc), functools.partial(lambda j, i, o: (jnp.maximum(i * hb - 1, 0), o + j), o=col // tc))
                 for _, col in pre]
                + [pl.BlockSpec((tm, tc), functools.partial(lambda j, i, o: (i, o + j), o=col // tc)) for _, col in pre]
                + [pl.BlockSpec((k, tc), lambda j, i: (0, j))]
                + [pl.BlockSpec((tm, tc), functools.partial(lambda j, i, o: (i, o + j), o=col // tc)) for _, col in extras]
                + [pl.BlockSpec((1, tc), lambda j, i: (0, j)) for _ in params])
    args = [a for a, _ in pre] * 2 + [w] + [a for a, _ in extras] + list(params)
    return pl.pallas_call(
        body, name=name, grid=(c_total // tc, t // tm), in_specs=in_specs,
        out_specs=pl.BlockSpec((tm, tc), lambda j, i: (i, j)),
        out_shape=jax.ShapeDtypeStruct((t, c_total), F32),
        scratch_shapes=[pltpu.VMEM((halo + tm, tc), F32), pltpu.VMEM((_n_residues(taps), halo + tm, tc), F32)],
        compiler_params=_cp(("parallel", "arbitrary")),
    )(*args)


def conv_bwd(name, k, halo, pre_fn, pre, w, post_fns, extras, params, dout, c_total, tc, tm_pref=512):
    t = pre[0][0].shape[0]
    tm = _tile(t, tm_pref, halo)
    npre, nex, npar = len(pre), len(extras), len(params)
    ngr = tc // LANES
    nout = npre + nex
    assert nout == 1 or c_total == tc
    nblk = t // tm
    ext = tm + halo
    taps = [halo - (k - 1) + j for j in range(k)]
    back = [k - 1 - j for j in range(k)]

    def body(*refs):
        it = iter(refs)
        prev = [next(it) for _ in range(npre)]
        cur = [next(it) for _ in range(npre)]
        nxt = [next(it) for _ in range(npre)]
        w_ref = next(it)
        ex_c = [next(it) for _ in range(nex)]
        ex_n = [next(it) for _ in range(nex)]
        par = [next(it) for _ in range(npar)]
        do_c, do_n = next(it), next(it)
        din_ref, dw_ref = next(it), next(it)
        dpar = [next(it) for _ in range(npar)]
        buf, dbuf, bank, dbank = next(it), next(it), next(it), next(it)
        i = pl.program_id(1)

        @pl.when(i == 0)
        def _():
            dw_ref[...] = jnp.zeros_like(dw_ref)
            for r in dpar:
                r[...] = jnp.zeros_like(r)

        buf[0:halo, :] = jnp.where(i > 0, pre_fn(*[r[...] for r in prev]), 0.0)
        cur_vals = [r[...] for r in cur]
        buf[halo:halo + tm, :] = pre_fn(*cur_vals)
        buf[halo + tm:, :] = pre_fn(*[r[...] for r in nxt])
        where = _roll_bank(buf[...], bank, taps)
        conv = None
        for j, o in enumerate(taps):
            slot, st = where[o]
            term = w_ref[j:j + 1, :] * bank[slot, st:st + ext, :]
            conv = term if conv is None else conv + term
        don = jnp.where(i < nblk - 1, do_n[...], 0.0)
        for gi in range(ngr):
            sl = slice(gi * LANES, (gi + 1) * LANES)
            pv = [p_[:, sl] for p_ in par]
            _, vj = jax.vjp(post_fns[gi], conv[:tm, sl], *[e[:, sl] for e in ex_c], *pv)
            gr = vj(do_c[:, sl])
            dbuf[0:tm, sl] = gr[0]
            for e in range(nex):
                din_ref[:, (npre + e) * tc + gi * LANES:(npre + e) * tc + (gi + 1) * LANES] = gr[1 + e].astype(din_ref.dtype)
            for q_ in range(npar):
                dpar[q_][:, sl] += gr[1 + nex + q_]
            _, vjn = jax.vjp(post_fns[gi], conv[tm:, sl], *[e[:, sl] for e in ex_n], *pv)
            dbuf[tm:, sl] = vjn(don[:, sl])[0]
        dcur = dbuf[0:tm, :]
        dwhere = _roll_bank(dbuf[...], dbank, back)
        dpre = None
        for j in range(k):
            slot, st = dwhere[back[j]]
            term = w_ref[j:j + 1, :] * dbank[slot, st:st + tm, :]
            dpre = term if dpre is None else dpre + term
            slot, st = where[taps[j]]
            dw_ref[j:j + 1, :] += jnp.sum(dcur * bank[slot, st:st + tm, :], axis=0, keepdims=True)
        _, vjp_pre = jax.vjp(pre_fn, *cur_vals)
        for e, gval in enumerate(vjp_pre(dpre)):
            din_ref[:, e * tc:(e + 1) * tc] = gval.astype(din_ref.dtype)

    hb = tm // halo
    last_h = t // halo - 1

    def spec(kind, col):
        o = col // tc
        if kind == "prev":
            return pl.BlockSpec((halo, tc), lambda j, i: (jnp.maximum(i * hb - 1, 0), o + j))
        if kind == "next":
            return pl.BlockSpec((halo, tc), lambda j, i: (jnp.minimum((i + 1) * hb, last_h), o + j))
        return pl.BlockSpec((tm, tc), lambda j, i: (i, o + j))

    in_specs = ([spec("prev", col) for _, col in pre] + [spec("cur", col) for _, col in pre]
                + [spec("next", col) for _, col in pre] + [pl.BlockSpec((k, tc), lambda j, i: (0, j))]
                + [spec("cur", col) for _, col in extras] + [spec("next", col) for _, col in extras]
                + [pl.BlockSpec((1, tc), lambda j, i: (0, j)) for _ in params]
                + [spec("cur", 0), spec("next", 0)])
    args = [a for a, _ in pre] * 3 + [w] + [a for a, _ in extras] * 2 + list(params) + [dout, dout]
    out = pl.pallas_call(
        body, name=name, grid=(c_total // tc, nblk), in_specs=in_specs,
        out_specs=[pl.BlockSpec((tm, nout * tc), lambda j, i: (i, j)), pl.BlockSpec((k, tc), lambda j, i: (0, j))]
        + [pl.BlockSpec((1, tc), lambda j, i: (0, j)) for _ in params],
        out_shape=[jax.ShapeDtypeStruct((t, nout * c_total), BF16), jax.ShapeDtypeStruct((k, c_total), F32)]
        + [jax.ShapeDtypeStruct((1, c_total), F32) for _ in params],
        scratch_shapes=[pltpu.VMEM((2 * halo + tm, tc), F32), pltpu.VMEM((ext, tc), F32),
                        pltpu.VMEM((_n_residues(taps), 2 * halo + tm, tc), F32),
                        pltpu.VMEM((_n_residues(back), ext, tc), F32)],
        compiler_params=_cp(("parallel", "arbitrary")),
    )(*args)
    return out[0], out[1], out[2:]


def _head_pick(ref_val, row):
    rr = lax.broadcasted_iota(jnp.int32, ref_val.shape, 0)
    v = jnp.sum(jnp.where(rr == row, ref_val, 0.0), axis=0, keepdims=True)
    ll = lax.broadcasted_iota(jnp.int32, v.shape, 1)
    return jnp.sum(jnp.where(ll == 0, v, 0.0), axis=1, keepdims=True)


def _lane_col(blk, lane_idx):
    ll = lax.broadcasted_iota(jnp.int32, blk.shape, 1)
    return jnp.sum(jnp.where(ll == lane_idx, blk, 0.0), axis=1, keepdims=True)


WY_HEADS = 2
WY_UNROLL = 4


def dn_wy_fwd(qkv, p, ba_col, alog_b, dtb_b, nheads):
    t = qkv.shape[0]
    tm = _tile(t, 512, CHUNK * WY_UNROLL)
    nc = tm // CHUNK
    hb = WY_HEADS
    hp = alog_b.shape[0]
    w_ = hb * LANES

    def body(qkv_ref, ba_ref, al_ref, dt_ref, u_ref, w_ref, qg_ref, kd_ref, qk_ref, e_ref):
        hblk = pl.program_id(1)
        alv, dtv = al_ref[...], dt_ref[...]

        def trip(cj, carry):
            units = [(cj * WY_UNROLL + cu, hl) for cu in range(WY_UNROLL) for hl in range(hb)]
            args = [[] for _ in range(7)]
            for ci, hl in units:
                rows = pl.ds(pl.multiple_of(ci * CHUNK, CHUNK), CHUNK)
                ba = ba_ref[rows, :]
                h = hblk * hb + hl
                for lst, val in zip(args, (qkv_ref[rows, hl * 384:hl * 384 + 128],
                                           qkv_ref[rows, hl * 384 + 128:hl * 384 + 256],
                                           qkv_ref[rows, hl * 384 + 256:hl * 384 + 384],
                                           _lane_col(ba, h), _lane_col(ba, nheads + h),
                                           _head_pick(alv, h), _head_pick(dtv, h))):
                    lst.append(val)
            outs = _wy(*args)
            for n, (ci, hl) in enumerate(units):
                rows = pl.ds(pl.multiple_of(ci * CHUNK, CHUNK), CHUNK)
                u, w, qg, kd, qk, e = [o[n] for o in outs]
                sl = slice(hl * LANES, (hl + 1) * LANES)
                u_ref[rows, sl] = u
                w_ref[rows, sl] = w.astype(BF16)
                qg_ref[rows, sl] = qg.astype(BF16)
                kd_ref[rows, sl] = kd.astype(BF16)
                qk_ref[rows, hl * LANES:hl * LANES + CHUNK] = qk.astype(BF16)
                qk_ref[rows, hl * LANES + CHUNK:(hl + 1) * LANES] = jnp.zeros((CHUNK, LANES - CHUNK), BF16)
                e_ref[ci, :, sl] = jnp.broadcast_to(e, (1, LANES))
            return carry

        lax.fori_loop(0, nc // WY_UNROLL, trip, 0)

    bc = ba_col // LANES
    blk = pl.BlockSpec((tm, w_), lambda i, h: (i, h))
    tab = pl.BlockSpec((hp, LANES), lambda i, h: (0, 0))
    wide = lambda dt: jax.ShapeDtypeStruct((t, nheads * LANES), dt)
    return pl.pallas_call(
        body, name="dn_wy_fwd", grid=(t // tm, nheads // hb),
        in_specs=[pl.BlockSpec((tm, hb * 384), lambda i, h: (i, h)), pl.BlockSpec((tm, LANES), lambda i, h: (i, bc)),
                  tab, tab],
        out_specs=[blk] * 5 + [pl.BlockSpec((nc, 1, w_), lambda i, h: (i, 0, h))],
        out_shape=[wide(F32), wide(BF16), wide(BF16), wide(BF16), wide(BF16),
                   jax.ShapeDtypeStruct((t // CHUNK, 1, nheads * LANES), F32)],
        compiler_params=_cp(("parallel", "parallel")),
    )(qkv, p, alog_b, dtb_b)


def dn_seq_fwd(u, w, qg, kd, qk, e, p, z_col, ng, nheads, hb):
    t = u.shape[0]
    tm = _tile(t, 512, CHUNK)
    nc = tm // CHUNK
    w_ = hb * LANES

    def body(u_ref, w_ref, qg_ref, kd_ref, qk_ref, e_ref, z_ref, ng_ref, y_ref, ss_ref, s_scr):
        i, hblk = pl.program_id(0), pl.program_id(1)
        for hl in range(hb):
            @pl.when(i == 0)
            def _():
                s_scr[hblk * hb + hl] = jnp.zeros((LANES, LANES), F32)
        ngv = ng_ref[...]

        def chunk(ci, carry):
            rows = pl.ds(pl.multiple_of(ci * CHUNK, CHUNK), CHUNK)
            ev = e_ref[ci]
            sls = [slice(hl * LANES, (hl + 1) * LANES) for hl in range(hb)]
            s = [s_scr[hblk * hb + hl] for hl in range(hb)]
            for hl in range(hb):
                ss_ref[ci, sls[hl], :] = s[hl]
            y, sn = _seq([u_ref[rows, sl] for sl in sls], [w_ref[rows, sl].astype(F32) for sl in sls],
                         [qg_ref[rows, sl].astype(F32) for sl in sls], [kd_ref[rows, sl].astype(F32) for sl in sls],
                         [qk_ref[rows, sl][:, :CHUNK].astype(F32) for sl in sls], [ev[:, sl] for sl in sls],
                         [z_ref[rows, sl] for sl in sls], s, ngv)
            for hl in range(hb):
                y_ref[rows, sls[hl]] = y[hl]
                s_scr[hblk * hb + hl] = sn[hl]
            return carry

        lax.fori_loop(0, nc, chunk, 0)

    zc = z_col // w_
    blk = pl.BlockSpec((tm, w_), lambda i, h: (i, h))
    return pl.pallas_call(
        body, name="dn_seq_fwd", grid=(t // tm, nheads // hb),
        in_specs=[blk] * 5 + [pl.BlockSpec((nc, 1, w_), lambda i, h: (i, 0, h)),
                              pl.BlockSpec((tm, w_), lambda i, h: (i, zc + h)),
                              pl.BlockSpec((1, LANES), lambda i, h: (0, 0))],
        out_specs=[blk, pl.BlockSpec((nc, w_, LANES), lambda i, h: (i, h, 0))],
        out_shape=[jax.ShapeDtypeStruct((t, nheads * LANES), F32),
                   jax.ShapeDtypeStruct((t // CHUNK, nheads * LANES, LANES), F32)],
        scratch_shapes=[pltpu.VMEM((nheads, LANES, LANES), F32)],
        compiler_params=_cp(("arbitrary", "arbitrary")),
    )(u, w, qg, kd, qk, e, p, ng)


def dn_seq_bwd(u, w, qg, kd, qk, e, p, z_col, ng, ss, dy, nheads, hb):
    t = u.shape[0]
    tm = _tile(t, 512, CHUNK)
    nc = tm // CHUNK
    nblk = t // tm
    w_ = hb * LANES

    def body(u_ref, w_ref, qg_ref, kd_ref, qk_ref, e_ref, z_ref, ng_ref, ss_ref, dy_ref,
             du_ref, dw_ref, dqg_ref, dkd_ref, dqk_ref, de_ref, dz_ref, dng_ref, ds_scr):
        i, hblk = pl.program_id(0), pl.program_id(1)

        @pl.when((i == 0) & (hblk == 0))
        def _():
            dng_ref[...] = jnp.zeros_like(dng_ref)

        for hl in range(hb):
            @pl.when(i == 0)
            def _():
                ds_scr[hblk * hb + hl] = jnp.zeros((LANES, LANES), F32)
        ngv = ng_ref[...]

        def chunk(cj, carry):
            ci = nc - 1 - cj
            rows = pl.ds(pl.multiple_of(ci * CHUNK, CHUNK), CHUNK)
            ev = e_ref[ci]
            sls = [slice(hl * LANES, (hl + 1) * LANES) for hl in range(hb)]
            _, vj = jax.vjp(_seq, [u_ref[rows, sl] for sl in sls], [w_ref[rows, sl].astype(F32) for sl in sls],
                            [qg_ref[rows, sl].astype(F32) for sl in sls], [kd_ref[rows, sl].astype(F32) for sl in sls],
                            [qk_ref[rows, sl][:, :CHUNK].astype(F32) for sl in sls], [ev[:, sl] for sl in sls],
                            [z_ref[rows, sl] for sl in sls], [ss_ref[ci, sl, :] for sl in sls], ngv)
            du, dw, dqg, dkd, dqk, de, dz, dsp, dng = vj(([dy_ref[rows, sl] for sl in sls],
                                                          [ds_scr[hblk * hb + hl] for hl in range(hb)]))
            for hl, sl in enumerate(sls):
                du_ref[rows, sl] = du[hl]
                dw_ref[rows, sl] = dw[hl]
                dqg_ref[rows, sl] = dqg[hl]
                dkd_ref[rows, sl] = dkd[hl]
                dqk_ref[rows, hl * LANES:hl * LANES + CHUNK] = dqk[hl]
                dqk_ref[rows, hl * LANES + CHUNK:(hl + 1) * LANES] = jnp.zeros((CHUNK, LANES - CHUNK), F32)
                de_ref[ci, :, sl] = de[hl]
                dz_ref[rows, sl] = dz[hl]
                ds_scr[hblk * hb + hl] = dsp[hl]
            dng_ref[...] += dng
            return carry

        lax.fori_loop(0, nc, chunk, 0)

    zc = z_col // w_
    rv = lambda i: nblk - 1 - i
    blk = pl.BlockSpec((tm, w_), lambda i, h: (rv(i), h))
    eblk = pl.BlockSpec((nc, 1, w_), lambda i, h: (rv(i), 0, h))
    one = pl.BlockSpec((1, LANES), lambda i, h: (0, 0))
    wide = jax.ShapeDtypeStruct((t, nheads * LANES), F32)
    return pl.pallas_call(
        body, name="dn_seq_bwd", grid=(nblk, nheads // hb),
        in_specs=[blk] * 5 + [eblk, pl.BlockSpec((tm, w_), lambda i, h: (rv(i), zc + h)), one,
                              pl.BlockSpec((nc, w_, LANES), lambda i, h: (rv(i), h, 0)), blk],
        out_specs=[blk] * 5 + [eblk, blk, one],
        out_shape=[wide] * 5 + [jax.ShapeDtypeStruct((t // CHUNK, 1, nheads * LANES), F32), wide,
                                jax.ShapeDtypeStruct((1, LANES), F32)],
        scratch_shapes=[pltpu.VMEM((nheads, LANES, LANES), F32)],
        compiler_params=_cp(("arbitrary", "arbitrary")),
    )(u, w, qg, kd, qk, e, p, ng, ss, dy)


def dn_wy_bwd(qkv, p, ba_col, alog_b, dtb_b, du, dw, dqg, dkd, dqk, de, nheads):
    t = qkv.shape[0]
    tm = _tile(t, 512, CHUNK * WY_UNROLL)
    nc = tm // CHUNK
    hb = WY_HEADS
    hp = alog_b.shape[0]
    w_ = hb * LANES

    def body(qkv_ref, ba_ref, al_ref, dt_ref, du_ref, dw_ref, dqg_ref, dkd_ref, dqk_ref, de_ref,
             dqkv_ref, dba_ref, dal_ref, ddt_ref):
        i, hblk = pl.program_id(0), pl.program_id(1)

        @pl.when((i == 0) & (hblk == 0))
        def _():
            dal_ref[...] = jnp.zeros_like(dal_ref)
            ddt_ref[...] = jnp.zeros_like(ddt_ref)

        @pl.when(hblk == 0)
        def _():
            dba_ref[...] = jnp.zeros_like(dba_ref)

        alv, dtv = al_ref[...], dt_ref[...]
        lane = lax.broadcasted_iota(jnp.int32, (CHUNK, LANES), 1)
        rowp = lax.broadcasted_iota(jnp.int32, (hp, LANES), 0)

        def trip(cj, carry):
            units = [(cj * WY_UNROLL + cu, hl) for cu in range(WY_UNROLL) for hl in range(hb)]
            args = [[] for _ in range(7)]
            cts = [[] for _ in range(6)]
            for ci, hl in units:
                rows = pl.ds(pl.multiple_of(ci * CHUNK, CHUNK), CHUNK)
                ba = ba_ref[rows, :]
                h = hblk * hb + hl
                sl = slice(hl * LANES, (hl + 1) * LANES)
                for lst, val in zip(args, (qkv_ref[rows, hl * 384:hl * 384 + 128],
                                           qkv_ref[rows, hl * 384 + 128:hl * 384 + 256],
                                           qkv_ref[rows, hl * 384 + 256:hl * 384 + 384],
                                           _lane_col(ba, h), _lane_col(ba, nheads + h),
                                           _head_pick(alv, h), _head_pick(dtv, h))):
                    lst.append(val)
                de11 = jnp.sum(de_ref[ci][:, sl], axis=1, keepdims=True)
                for lst, val in zip(cts, (du_ref[rows, sl], dw_ref[rows, sl], dqg_ref[rows, sl], dkd_ref[rows, sl],
                                          dqk_ref[rows, sl][:, :CHUNK], de11)):
                    lst.append(val)
            _, vj = jax.vjp(_wy, *args)
            grads = vj(tuple(cts))
            for n, (ci, hl) in enumerate(units):
                rows = pl.ds(pl.multiple_of(ci * CHUNK, CHUNK), CHUNK)
                h = hblk * hb + hl
                dq, dk, dv, dbc, dac, dal, ddt = [g_[n] for g_ in grads]
                dqkv_ref[rows, hl * 384:hl * 384 + 128] = dq
                dqkv_ref[rows, hl * 384 + 128:hl * 384 + 256] = dk
                dqkv_ref[rows, hl * 384 + 256:hl * 384 + 384] = dv
                dba_ref[rows, :] += jnp.where(lane == h, dbc, 0.0) + jnp.where(lane == nheads + h, dac, 0.0)
                dal_ref[...] += jnp.where(rowp == h, dal, 0.0)
                ddt_ref[...] += jnp.where(rowp == h, ddt, 0.0)
            return carry

        lax.fori_loop(0, nc // WY_UNROLL, trip, 0)

    bc = ba_col // LANES
    blk = pl.BlockSpec((tm, w_), lambda i, h: (i, h))
    tab = pl.BlockSpec((hp, LANES), lambda i, h: (0, 0))
    return pl.pallas_call(
        body, name="dn_wy_bwd", grid=(t // tm, nheads // hb),
        in_specs=[pl.BlockSpec((tm, hb * 384), lambda i, h: (i, h)), pl.BlockSpec((tm, LANES), lambda i, h: (i, bc)),
                  tab, tab] + [blk] * 5 + [pl.BlockSpec((nc, 1, w_), lambda i, h: (i, 0, h))],
        out_specs=[pl.BlockSpec((tm, hb * 384), lambda i, h: (i, h)), pl.BlockSpec((tm, LANES), lambda i, h: (i, 0)),
                   tab, tab],
        out_shape=[jax.ShapeDtypeStruct((t, nheads * 384), F32), jax.ShapeDtypeStruct((t, LANES), F32),
                   jax.ShapeDtypeStruct((hp, LANES), F32), jax.ShapeDtypeStruct((hp, LANES), F32)],
        compiler_params=_cp(("arbitrary", "arbitrary")),
    )(qkv, p, alog_b, dtb_b, du, dw, dqg, dkd, dqk, de)


def sg_fwd(p, col, sg, lg, lb, w, bias_b):
    t = p.shape[0]
    ng_ = sg // LANES
    tm = _tile(t, 256, LANES)
    cb = col // sg

    def body(u_ref, v_ref, g_ref, lg_ref, lb_ref, w_ref, b_ref, y_ref):
        for n in range(tm // LANES):
            rs = slice(n * LANES, (n + 1) * LANES)
            for gi in range(ng_):
                sl = slice(gi * LANES, (gi + 1) * LANES)
                y_ref[rs, sl] = _sg_block(u_ref[rs, sl], v_ref[rs, sl], g_ref[rs, sl], lg_ref[:, sl], lb_ref[:, sl],
                                          w_ref[gi], b_ref[gi])

    vec = pl.BlockSpec((1, sg), lambda i: (0, 0))
    full = pl.BlockSpec((ng_, LANES, LANES), lambda i: (0, 0, 0))
    return pl.pallas_call(
        body, name="sg_fwd", grid=(t // tm,),
        in_specs=[pl.BlockSpec((tm, sg), lambda i: (i, cb)), pl.BlockSpec((tm, sg), lambda i: (i, cb + 1)),
                  pl.BlockSpec((tm, sg), lambda i: (i, cb + 2)), vec, vec, full, full],
        out_specs=pl.BlockSpec((tm, sg), lambda i: (i, 0)),
        out_shape=jax.ShapeDtypeStruct((t, sg), F32),
        compiler_params=_cp(("parallel",)),
    )(p, p, p, lg, lb, w, bias_b)


def sg_bwd(p, col, sg, lg, lb, w, bias_b, dy):
    t = p.shape[0]
    ng_ = sg // LANES
    tm = _tile(t, 256, LANES)
    cb = col // sg

    def body(u_ref, v_ref, g_ref, lg_ref, lb_ref, w_ref, b_ref, dy_ref, d_ref, dlg_ref, dlb_ref, dw_ref, db_ref):
        @pl.when(pl.program_id(0) == 0)
        def _():
            for r in (dlg_ref, dlb_ref, dw_ref, db_ref):
                r[...] = jnp.zeros_like(r)

        for n in range(tm // LANES):
            rs = slice(n * LANES, (n + 1) * LANES)
            for gi in range(ng_):
                sl = slice(gi * LANES, (gi + 1) * LANES)
                _, vj = jax.vjp(_sg_block, u_ref[rs, sl], v_ref[rs, sl], g_ref[rs, sl], lg_ref[:, sl], lb_ref[:, sl],
                                w_ref[gi], b_ref[gi])
                du, dv, dg, dlg, dlb, dw, db = vj(dy_ref[rs, sl])
                d_ref[rs, gi * LANES:(gi + 1) * LANES] = du.astype(BF16)
                d_ref[rs, sg + gi * LANES:sg + (gi + 1) * LANES] = dv.astype(BF16)
                d_ref[rs, 2 * sg + gi * LANES:2 * sg + (gi + 1) * LANES] = dg.astype(BF16)
                dlg_ref[:, sl] += dlg
                dlb_ref[:, sl] += dlb
                dw_ref[gi] += dw
                db_ref[gi] += jnp.broadcast_to(jnp.sum(db, axis=1, keepdims=True), (LANES, LANES))

    vec = pl.BlockSpec((1, sg), lambda i: (0, 0))
    full = pl.BlockSpec((ng_, LANES, LANES), lambda i: (0, 0, 0))
    return pl.pallas_call(
        body, name="sg_bwd", grid=(t // tm,),
        in_specs=[pl.BlockSpec((tm, sg), lambda i: (i, cb)), pl.BlockSpec((tm, sg), lambda i: (i, cb + 1)),
                  pl.BlockSpec((tm, sg), lambda i: (i, cb + 2)), vec, vec, full, full,
                  pl.BlockSpec((tm, sg), lambda i: (i, 0))],
        out_specs=[pl.BlockSpec((tm, 3 * sg), lambda i: (i, 0)), vec, vec, full, full],
        out_shape=[jax.ShapeDtypeStruct((t, 3 * sg), BF16), jax.ShapeDtypeStruct((1, sg), F32),
                   jax.ShapeDtypeStruct((1, sg), F32), jax.ShapeDtypeStruct((ng_, LANES, LANES), F32),
                   jax.ShapeDtypeStruct((ng_, LANES, LANES), F32)],
        compiler_params=_cp(("arbitrary",)),
    )(p, p, p, lg, lb, w, bias_b, dy)


def out_proj(x, y_dn, y_sg, y_cv, wo, gate):
    t, d = x.shape
    dn, sg, cv = y_dn.shape[1], y_sg.shape[1], y_cv.shape[1]
    dmix = dn + sg + cv
    tm = _tile(t, 256, LANES)

    def body(x_ref, a_ref, b_ref, c_ref, w_ref, g_ref, xn_ref, y_ref, yt_ref):
        a, b, c = a_ref[...], b_ref[...], c_ref[...]
        y = (jnp.dot(a.astype(BF16), w_ref[0:dn, :], preferred_element_type=F32)
             + jnp.dot(b.astype(BF16), w_ref[dn:dn + sg, :], preferred_element_type=F32)
             + jnp.dot(c.astype(BF16), w_ref[dn + sg:, :], preferred_element_type=F32))
        y_ref[...] = y
        xn_ref[...] = x_ref[...] + g_ref[...] * y
        yt_ref[0:dn, :] = a.T.astype(BF16)
        yt_ref[dn:dn + sg, :] = b.T.astype(BF16)
        yt_ref[dn + sg:, :] = c.T.astype(BF16)

    row = lambda w_: pl.BlockSpec((tm, w_), lambda i: (i, 0))
    return pl.pallas_call(
        body, name="out_proj", grid=(t // tm,),
        in_specs=[row(d), row(dn), row(sg), row(cv), pl.BlockSpec((dmix, d), lambda i: (0, 0)),
                  pl.BlockSpec((1, d), lambda i: (0, 0))],
        out_specs=[row(d), row(d), pl.BlockSpec((dmix, tm), lambda i: (0, i))],
        out_shape=[jax.ShapeDtypeStruct((t, d), F32), jax.ShapeDtypeStruct((t, d), F32),
                   jax.ShapeDtypeStruct((dmix, t), BF16)],
        compiler_params=_cp(("parallel",)),
    )(x, y_dn, y_sg, y_cv, wo, gate)


def out_proj_bwd(dxn, y, gate, wo, dn, sg, cv):
    t, d = dxn.shape
    dmix = dn + sg + cv
    tm = _tile(t, 256, LANES)

    def body(dx_ref, y_ref, g_ref, w_ref, da_ref, db_ref, dc_ref, dyb_ref, dg_ref):
        @pl.when(pl.program_id(0) == 0)
        def _():
            dg_ref[...] = jnp.zeros_like(dg_ref)
        dx = dx_ref[...]
        dg_ref[...] += jnp.sum(dx * y_ref[...], axis=0, keepdims=True)
        dyb = (dx * g_ref[...]).astype(BF16)
        dyb_ref[...] = dyb
        dcat = lax.dot_general(dyb, w_ref[...], (((1,), (1,)), ((), ())), preferred_element_type=F32)
        da_ref[...] = dcat[:, 0:dn]
        db_ref[...] = dcat[:, dn:dn + sg]
        dc_ref[...] = dcat[:, dn + sg:]

    row = lambda w_: pl.BlockSpec((tm, w_), lambda i: (i, 0))
    vec = pl.BlockSpec((1, d), lambda i: (0, 0))
    return pl.pallas_call(
        body, name="out_proj_bwd", grid=(t // tm,),
        in_specs=[row(d), row(d), vec, pl.BlockSpec((dmix, d), lambda i: (0, 0))],
        out_specs=[row(dn), row(sg), row(cv), row(d), vec],
        out_shape=[jax.ShapeDtypeStruct((t, dn), F32), jax.ShapeDtypeStruct((t, sg), F32),
                   jax.ShapeDtypeStruct((t, cv), F32), jax.ShapeDtypeStruct((t, d), BF16),
                   jax.ShapeDtypeStruct((1, d), F32)],
        compiler_params=_cp(("arbitrary",)),
    )(dxn, y, gate, wo)


def matmul_acc(name, at, b):
    m, t = at.shape
    n = b.shape[1]
    tm, tn, tk = _tile(m, 1024, LANES), _tile(n, 2432, LANES), _tile(t, 1024, LANES)

    def body(a_ref, b_ref, o_ref):
        @pl.when(pl.program_id(2) == 0)
        def _():
            o_ref[...] = jnp.zeros_like(o_ref)
        o_ref[...] += jnp.dot(a_ref[...], b_ref[...], preferred_element_type=F32)

    return pl.pallas_call(
        body, name=name, grid=(m // tm, n // tn, t // tk),
        in_specs=[pl.BlockSpec((tm, tk), lambda i, j, k: (i, k)), pl.BlockSpec((tk, tn), lambda i, j, k: (k, j))],
        out_specs=pl.BlockSpec((tm, tn), lambda i, j, k: (i, j)),
        out_shape=jax.ShapeDtypeStruct((m, n), F32),
        compiler_params=_cp(("parallel", "parallel", "arbitrary")),
    )(at, b)


def in_proj_bwd(dp, wp, x, dxn, shift, scale, ng):
    t, d = x.shape
    npc = wp.shape[1]
    tm, tk = _tile(t, 512, LANES), _tile(npc, 2432, LANES)

    def mm_body(dp_ref, w_ref, dh_ref):
        @pl.when(pl.program_id(1) == 0)
        def _():
            dh_ref[...] = jnp.zeros_like(dh_ref)
        dh_ref[...] += lax.dot_general(dp_ref[...], w_ref[...], (((1,), (1,)), ((), ())), preferred_element_type=F32)

    dh = pl.pallas_call(
        mm_body, name="in_proj_bwd", grid=(t // tm, npc // tk),
        in_specs=[pl.BlockSpec((tm, tk), lambda i, k: (i, k)), pl.BlockSpec((d, tk), lambda i, k: (0, k))],
        out_specs=pl.BlockSpec((tm, d), lambda i, k: (i, 0)), out_shape=jax.ShapeDtypeStruct((t, d), F32),
        compiler_params=_cp(("parallel", "arbitrary")),
    )(dp, wp)

    tr = _tile(t, 256, 8)

    def norm_body(dh_ref, x_ref, dxn_ref, sh_ref, sc_ref, g_ref, dx_ref, dg_ref, dsc_ref, dsh_ref):
        @pl.when(pl.program_id(0) == 0)
        def _():
            for r in (dg_ref, dsc_ref, dsh_ref):
                r[...] = jnp.zeros_like(r)
        _, vj = jax.vjp(_modnorm, x_ref[...], g_ref[...], sc_ref[...], sh_ref[...])
        dx, dg, dsc, dsh = vj(dh_ref[...])
        dx_ref[...] = dxn_ref[...] + dx
        dg_ref[...] += dg
        dsc_ref[...] += dsc
        dsh_ref[...] += dsh

    vec = pl.BlockSpec((1, d), lambda i: (0, 0))
    row = pl.BlockSpec((tr, d), lambda i: (i, 0))
    return pl.pallas_call(
        norm_body, name="modnorm_bwd", grid=(t // tr,), in_specs=[row, row, row, vec, vec, vec],
        out_specs=[row, vec, vec, vec],
        out_shape=[jax.ShapeDtypeStruct((t, d), F32)] + [jax.ShapeDtypeStruct((1, d), F32)] * 3,
        compiler_params=_cp(("arbitrary",)),
    )(dh, x, dxn, shift, scale, ng)


def loss_head(x, tgt, fg):
    t, d = x.shape
    tm = _tile(t, 512, 8)

    def body(x_ref, t_ref, g_ref, l_ref, dx_ref, dg_ref):
        @pl.when(pl.program_id(0) == 0)
        def _():
            l_ref[...] = jnp.zeros_like(l_ref)
            dg_ref[...] = jnp.zeros_like(dg_ref)
        y, vj = jax.vjp(_rmsnorm, x_ref[...], g_ref[...])
        err = y - t_ref[...]
        part = 0.5 * jnp.sum(jnp.sum(err * err, axis=1, keepdims=True), axis=0, keepdims=True) / d
        l_ref[...] += jnp.broadcast_to(part, l_ref.shape)
        dx, dg = vj(err / d)
        dx_ref[...] = dx
        dg_ref[...] += dg

    row = pl.BlockSpec((tm, d), lambda i: (i, 0))
    vec = pl.BlockSpec((1, d), lambda i: (0, 0))
    return pl.pallas_call(
        body, name="loss_head", grid=(t // tm,), in_specs=[row, row, vec],
        out_specs=[pl.BlockSpec((1, LANES), lambda i: (0, 0)), row, vec],
        out_shape=[jax.ShapeDtypeStruct((1, LANES), F32), jax.ShapeDtypeStruct((t, d), F32),
                   jax.ShapeDtypeStruct((1, d), F32)],
        compiler_params=_cp(("arbitrary",)),
    )(x, tgt, fg)


def adamw(name, w, g, m, v):
    r, c = w.shape
    tr = _tile(r, 256, 8) if r % 8 == 0 else r

    def body(w_ref, g_ref, m_ref, v_ref, d_ref, mo_ref, vo_ref):
        d_ref[...], mo_ref[...], vo_ref[...] = _adamw_math(w_ref[...], g_ref[...], m_ref[...], v_ref[...])

    blk = pl.BlockSpec((tr, c), lambda i: (i, 0))
    return pl.pallas_call(
        body, name=name, grid=(r // tr,), in_specs=[blk] * 4, out_specs=[blk] * 3,
        out_shape=[jax.ShapeDtypeStruct((r, c), F32)] * 3, compiler_params=_cp(("parallel",)),
    )(w, g, m, v)


def ada_fwd(c_all, w_ada, b_loc):
    nl, d, cols = w_ada.shape
    nb = c_all.shape[0]
    tn = _tile(cols, 512, LANES)

    def body(c_ref, w_ref, b_ref, o_ref):
        ca = _silu(c_ref[...]).astype(BF16)
        o_ref[0] = jnp.dot(ca, w_ref[0].astype(BF16), preferred_element_type=F32) + b_ref[0]

    return pl.pallas_call(
        body, name="ada_fwd", grid=(nl, cols // tn),
        in_specs=[pl.BlockSpec((nb, d), lambda l, j: (0, 0)), pl.BlockSpec((1, d, tn), lambda l, j: (l, 0, j)),
                  pl.BlockSpec((1, 1, tn), lambda l, j: (l, 0, j))],
        out_specs=pl.BlockSpec((1, nb, tn), lambda l, j: (l, 0, j)),
        out_shape=jax.ShapeDtypeStruct((nl, nb, cols), F32),
        compiler_params=_cp(("parallel", "parallel")),
    )(c_all, w_ada, b_loc)


def ada_bwd(c_all_t, dmod_loc, w, m, v):
    nl, d, cols = w.shape
    nb = c_all_t.shape[1]
    tr = _tile(d, 256, 8)

    def body(c_ref, dm_ref, w_ref, m_ref, v_ref, g_ref, d_ref, mo_ref, vo_ref):
        ca = _silu(c_ref[...])
        dm = dm_ref[0]
        g = _lane_col(ca, 0) * dm[0:1, :]
        for b in range(1, nb):
            g = g + _lane_col(ca, b) * dm[b:b + 1, :]
        g_ref[0] = g
        d_ref[0], mo_ref[0], vo_ref[0] = _adamw_math(w_ref[0], g, m_ref[0], v_ref[0])

    blk = pl.BlockSpec((1, tr, cols), lambda l, i: (l, i, 0))
    return pl.pallas_call(
        body, name="ada_bwd", grid=(nl, d // tr),
        in_specs=[pl.BlockSpec((tr, nb), lambda l, i: (i, 0)), pl.BlockSpec((1, nb, cols), lambda l, i: (l, 0, 0)),
                  blk, blk, blk],
        out_specs=[blk] * 4, out_shape=[jax.ShapeDtypeStruct((nl, d, cols), F32)] * 4,
        compiler_params=_cp(("parallel", "parallel")),
    )(c_all_t, dmod_loc, w, m, v)


def sum8(g):
    _, r, c = g.shape
    tr = _tile(r, 256, 8)

    def body(g_ref, o_ref):
        acc = g_ref[0]
        for k in range(1, NDEV):
            acc = acc + g_ref[k]
        o_ref[...] = acc

    return pl.pallas_call(
        body, name="sum8", grid=(r // tr,), in_specs=[pl.BlockSpec((NDEV, tr, c), lambda i: (0, i, 0))],
        out_specs=pl.BlockSpec((tr, c), lambda i: (i, 0)), out_shape=jax.ShapeDtypeStruct((r, c), F32),
        compiler_params=_cp(("parallel",)),
    )(g)


def pair_sum(cflag, g, r1):
    _, ns, r, c = g.shape
    tr = _tile(r, 256, 8)

    def body(cf_ref, g0_ref, g1_ref, r_ref, ob_ref):
        keep = jnp.where(cf_ref[0:1, 0:1] == 0.0, g0_ref[0], g1_ref[0])
        ob_ref[...] = (keep + r_ref[...]).astype(BF16)

    blk = pl.BlockSpec((1, tr, c), lambda s, i: (s, i, 0))
    return pl.pallas_call(
        body, name="pair_sum", grid=(ns, r // tr),
        in_specs=[pl.BlockSpec((1, LANES), lambda s, i: (0, 0)), pl.BlockSpec((1, 1, tr, c), lambda s, i: (0, s, i, 0)),
                  pl.BlockSpec((1, 1, tr, c), lambda s, i: (1, s, i, 0)), blk], out_specs=blk,
        out_shape=jax.ShapeDtypeStruct((ns, r, c), BF16), compiler_params=_cp(("parallel", "parallel")),
    )(cflag, g, g, r1)


def chip_sum(own, r1, r2):
    r, c = own.shape
    tr = _tile(r, 256, 8)

    def body(g_ref, r1_ref, r2_ref, o_ref):
        acc = g_ref[...] + r1_ref[...]
        for k in range(NCHIP - 1):
            acc = acc + r2_ref[k].astype(F32)
        o_ref[...] = acc

    blk = pl.BlockSpec((tr, c), lambda i: (i, 0))
    return pl.pallas_call(
        body, name="chip_sum", grid=(r // tr,),
        in_specs=[blk, blk, pl.BlockSpec((NCHIP - 1, tr, c), lambda i: (0, i, 0))], out_specs=blk,
        out_shape=jax.ShapeDtypeStruct((r, c), F32), compiler_params=_cp(("parallel",)),
    )(own, r1, r2)


def adamw_halves(name, cflag, w, own, recv, m, v):
    _, r, c = w.shape
    tr = _tile(r, 256, 8)

    def body(cf_ref, w_ref, a_ref, b_ref, m_ref, v_ref, g_ref, d_ref, mo_ref, vo_ref):
        is_own = cf_ref[0:1, 0:1] == pl.program_id(0).astype(F32)
        g = jnp.where(is_own, a_ref[...], b_ref[...])
        g_ref[0] = g
        d_ref[0], mo_ref[0], vo_ref[0] = _adamw_math(w_ref[0], g, m_ref[0], v_ref[0])

    blk = pl.BlockSpec((1, tr, c), lambda h, i: (h, i, 0))
    hlf = pl.BlockSpec((tr, c), lambda h, i: (i, 0))
    return pl.pallas_call(
        body, name=name, grid=(2, r // tr),
        in_specs=[pl.BlockSpec((1, LANES), lambda h, i: (0, 0)), blk, hlf, hlf, blk, blk], out_specs=[blk] * 4,
        out_shape=[jax.ShapeDtypeStruct(w.shape, F32)] * 4, compiler_params=_cp(("parallel", "parallel")),
    )(cflag, w, own, recv, m, v)


def _me():
    return lax.axis_index("x"), lax.axis_index("y"), lax.axis_index("c")


_FLIPS = ((1, 0), (0, 1), (1, 1))


def all_gather8(v):
    m_per, n = v.shape

    def body(x_ref, out_ref, send_sems, recv_sems, local_sem):
        x, y, c = _me()
        me, sibling = (x, y, c), (x, y, 1 - c)
        chips = [(x ^ fx, y ^ fy) for fx, fy in _FLIPS]

        def rows(px, py, pc):
            return out_ref.at[pl.ds((4 * px + 2 * py + pc) * m_per, m_per), :]

        def copy(k, block, to, src=None):
            return pltpu.make_async_remote_copy(
                src_ref=rows(*block) if src is None else src, dst_ref=rows(*block),
                send_sem=send_sems.at[k], recv_sem=recv_sems.at[k], device_id=to, device_id_type=MESH)

        mine = pltpu.make_async_copy(x_ref, rows(*me), local_sem)
        mine.start()
        first = [copy(0, me, sibling, src=x_ref)]
        first += [copy(1 + j, me, (*chip, c), src=x_ref) for j, chip in enumerate(chips)]
        for cp in first:
            cp.start()
        passed = [copy(4 + j, (*chip, c), sibling) for j, chip in enumerate(chips)]
        for j, chip in enumerate(chips):
            copy(1 + j, (*chip, c), me).wait_recv()
            passed[j].start()
        copy(0, sibling, me).wait_recv()
        for j, chip in enumerate(chips):
            copy(4 + j, (*chip, 1 - c), me).wait_recv()
        for cp in first + passed:
            cp.wait_send()
        mine.wait()

    return pl.pallas_call(
        body, name="all_gather8", out_shape=jax.ShapeDtypeStruct((NDEV * m_per, n), v.dtype),
        in_specs=[pl.BlockSpec(memory_space=pltpu.VMEM)], out_specs=pl.BlockSpec(memory_space=pltpu.VMEM),
        scratch_shapes=[pltpu.SemaphoreType.DMA((7,)), pltpu.SemaphoreType.DMA((7,)), pltpu.SemaphoreType.DMA],
        compiler_params=pltpu.CompilerParams(vmem_limit_bytes=VMEM_LIMIT),
    )(v)


def gather_weights(ws):
    na = len(ws)

    def body(*refs):
        srcs, outs = refs[:na], refs[na:2 * na]
        send_sems, recv_sems = refs[2 * na:]
        x, y, c = _me()
        chip = 2 * x + y
        sibling = (x, y, 1 - c)
        peers = [(x ^ fx, y ^ fy) for fx, fy in _FLIPS]

        def ici(a, j, half_of, to):
            return pltpu.make_async_remote_copy(
                src_ref=srcs[a].at[c], dst_ref=outs[a].at[half_of, c], send_sem=send_sems.at[a * 6 + j],
                recv_sem=recv_sems.at[a * 6 + j], device_id=to, device_id_type=MESH)

        def d2d(a, j, slot, half):
            return pltpu.make_async_remote_copy(
                src_ref=outs[a].at[slot, half], dst_ref=outs[a].at[slot, half], send_sem=send_sems.at[a * 6 + 3 + j],
                recv_sem=recv_sems.at[a * 6 + 3 + j], device_id=sibling, device_id_type=MESH)

        sends = []
        for a in range(na):
            for j, (px, py) in enumerate(peers):
                cp = ici(a, j, chip, (px, py, c))
                cp.start()
                sends.append(cp)
        for a in range(na):
            for j, (px, py) in enumerate(peers):
                ici(a, j, 2 * px + py, (px, py, c)).wait_recv()
                cp = d2d(a, j, 2 * px + py, c)
                cp.start()
                sends.append(cp)
        for a in range(na):
            for j, (px, py) in enumerate(peers):
                d2d(a, j, 2 * px + py, 1 - c).wait_recv()
        for cp in sends:
            cp.wait_send()

    return pl.pallas_call(
        body, name="gather_weights",
        out_shape=[jax.ShapeDtypeStruct((NCHIP,) + w.shape, w.dtype) for w in ws],
        in_specs=[ANY] * na, out_specs=[ANY] * na,
        scratch_shapes=[pltpu.SemaphoreType.DMA((6 * na,)), pltpu.SemaphoreType.DMA((6 * na,))],
    )(*ws)


def sibling_swap(name, gs, other_half=False):
    na = len(gs)

    def body(*refs):
        srcs, outs = refs[:na], refs[na:2 * na]
        send_sems, recv_sems = refs[2 * na:]
        x, y, c = _me()
        cps = [pltpu.make_async_remote_copy(
            src_ref=srcs[a].at[1 - c] if other_half else srcs[a], dst_ref=outs[a], send_sem=send_sems.at[a],
            recv_sem=recv_sems.at[a], device_id=(x, y, 1 - c), device_id_type=MESH) for a in range(na)]
        for cp in cps:
            cp.start()
        for cp in cps:
            cp.wait()

    return pl.pallas_call(
        body, name=name, out_shape=[jax.ShapeDtypeStruct(g.shape[1:] if other_half else g.shape, g.dtype) for g in gs],
        in_specs=[ANY] * na, out_specs=[ANY] * na,
        scratch_shapes=[pltpu.SemaphoreType.DMA((na,)), pltpu.SemaphoreType.DMA((na,))],
    )(*gs)


def chip_exchange(ps):
    na = len(ps)

    def body(*refs):
        srcs, outs = refs[:na], refs[na:2 * na]
        send_sems, recv_sems = refs[2 * na:]
        x, y, c = _me()
        cps = []
        for a in range(na):
            for j, (fx, fy) in enumerate(_FLIPS):
                px, py = x ^ fx, y ^ fy
                cps.append(pltpu.make_async_remote_copy(
                    src_ref=srcs[a].at[2 * px + py], dst_ref=outs[a].at[j], send_sem=send_sems.at[a * 3 + j],
                    recv_sem=recv_sems.at[a * 3 + j], device_id=(px, py, c), device_id_type=MESH))
        for cp in cps:
            cp.start()
        for cp in cps:
            cp.wait()

    return pl.pallas_call(
        body, name="chip_exchange",
        out_shape=[jax.ShapeDtypeStruct((NCHIP - 1,) + p_.shape[1:], p_.dtype) for p_ in ps],
        in_specs=[ANY] * na, out_specs=[ANY] * na,
        scratch_shapes=[pltpu.SemaphoreType.DMA((3 * na,)), pltpu.SemaphoreType.DMA((3 * na,))],
    )(*ps)


class _Cfg:
    def __init__(self, x, a_log, sg_w, cv_ln_g, cv_w, conv_qkv):
        self.t, self.d = x.shape[1], x.shape[2]
        self.nl, self.h = a_log.shape
        self.dn = self.h * LANES
        self.g = sg_w.shape[1]
        self.sg = self.g * LANES
        self.cv = cv_ln_g.shape[1]
        self.kc = cv_w.shape[1]
        self.k4 = conv_qkv.shape[1]
        self.o_z = 3 * self.dn
        self.o_sg = 4 * self.dn
        self.o_cv = self.o_sg + 3 * self.sg
        self.o_ba = self.o_cv + 3 * self.cv
        self.npc = self.o_ba + LANES
        self.d_in = self.o_ba + 2 * self.h
        self.dmix = self.dn + self.sg + self.cv
        self.hb_fwd = _tile(self.h, 8, 1)
        self.hb_bwd = _tile(self.h, 4, 1)


def _runs(cfg):
    dn, h = cfg.dn, cfg.h
    runs = [(part * dn + hd * LANES, hd * 3 * LANES + part * LANES, LANES) for part in range(3) for hd in range(h)]
    return runs + [(3 * dn, 3 * dn, dn), (4 * dn, cfg.o_ba, 2 * h), (4 * dn + 2 * h, 4 * dn, cfg.o_ba - 4 * dn)]


def _assemble_perm(cfg, shards):
    cols = shards[0].shape[-1]
    pieces = []
    for nat, _, wdt in sorted(_runs(cfg), key=lambda r_: r_[1]):
        a = nat
        while a < nat + wdt:
            s = a // cols
            b = min(nat + wdt, (s + 1) * cols)
            pieces.append(shards[s][..., a - s * cols:b - s * cols])
            a = b
    pieces.append(jnp.zeros(shards[0].shape[:-1] + (cfg.npc - cfg.o_ba - 2 * cfg.h,), shards[0].dtype))
    return jnp.concatenate(pieces, axis=-1)


def _natural_shard(cfg, g, s, cols):
    lo, hi = s * cols, (s + 1) * cols
    pieces = []
    for nat, perm, wdt in sorted(_runs(cfg)):
        a, b = max(nat, lo), min(nat + wdt, hi)
        if a < b:
            pieces.append(g[..., perm + a - nat:perm + b - nat])
    return jnp.concatenate(pieces, axis=-1)


def _layer_fwd(cfg, x, mod, lw):
    shift, scale, gate = mod
    p, ht = in_proj(x, shift, scale, lw["norm_g"], lw["wp"])
    qk_post = [_qk_post, _qk_post, _v_post]
    qkv = conv_fwd("dn_pre_fwd", cfg.k4, HALO4, lambda a: a, [(p, 0)], lw["conv_qkv"], qk_post, [], [],
                   3 * cfg.dn, 3 * LANES)
    wy = dn_wy_fwd(qkv, p, cfg.o_ba, lw["alog_b"], lw["dtb_b"], cfg.h)
    y_dn, ss = dn_seq_fwd(*wy, p, cfg.o_z, lw["dn_norm_g"], cfg.h, cfg.hb_fwd)
    y_sg = sg_fwd(p, cfg.o_sg, cfg.sg, lw["sg_ln_g"], lw["sg_ln_b"], lw["sg_w"], lw["sg_bias_b"])
    cv_post = [_cv_post] * (cfg.cv // LANES)
    y_cv = conv_fwd("cv_fwd", cfg.kc, HALO31, _glu, [(p, cfg.o_cv), (p, cfg.o_cv + cfg.cv)], lw["cv_w"], cv_post,
                    [(p, cfg.o_cv + 2 * cfg.cv)], [lw["cv_b"], lw["cv_ln_g"], lw["cv_ln_b"]], cfg.cv, cfg.cv)
    xn, y, yt = out_proj(x, y_dn, y_sg, y_cv, lw["wo"], gate)
    return xn, dict(x=x, p=p, ht=ht, qkv=qkv, wy=wy, ss=ss, y=y, yt=yt)


def _layer_bwd(cfg, dxn, mod, lw, sv):
    shift, scale, gate = mod
    p = sv["p"]
    d_dn, d_sg, d_cv, dyb, dgate = out_proj_bwd(dxn, sv["y"], gate, lw["wo"], cfg.dn, cfg.sg, cfg.cv)
    g_wo = matmul_acc("w_out_grad", sv["yt"], dyb)
    cv_post = [_cv_post] * (cfg.cv // LANES)
    dcv, g_cvw, (g_cvb, g_cvlg, g_cvlb) = conv_bwd(
        "cv_bwd", cfg.kc, HALO31, _glu, [(p, cfg.o_cv), (p, cfg.o_cv + cfg.cv)], lw["cv_w"], cv_post,
        [(p, cfg.o_cv + 2 * cfg.cv)], [lw["cv_b"], lw["cv_ln_g"], lw["cv_ln_b"]], d_cv, cfg.cv, cfg.cv, tm_pref=256)
    dsg, g_sglg, g_sglb, g_sgw, g_sgb = sg_bwd(p, cfg.o_sg, cfg.sg, lw["sg_ln_g"], lw["sg_ln_b"], lw["sg_w"],
                                               lw["sg_bias_b"], d_sg)
    *dwy, dz, g_dng = dn_seq_bwd(*sv["wy"], p, cfg.o_z, lw["dn_norm_g"], sv["ss"], d_dn, cfg.h, cfg.hb_bwd)
    dqkv, dba, g_al, g_dt = dn_wy_bwd(sv["qkv"], p, cfg.o_ba, lw["alog_b"], lw["dtb_b"], *dwy, cfg.h)
    qk_post = [_qk_post, _qk_post, _v_post]
    dqkv_pre, g_cq, _ = conv_bwd("dn_pre_bwd", cfg.k4, HALO4, lambda a: a, [(p, 0)], lw["conv_qkv"], qk_post, [], [],
                                 dqkv, 3 * cfg.dn, 3 * LANES)
    dp = jnp.concatenate([dqkv_pre, dz.astype(BF16), dsg, dcv, dba.astype(BF16)], axis=1)
    g_wp = matmul_acc("w_in_grad", sv["ht"], dp)
    dx, g_ng, dscale, dshift = in_proj_bwd(dp, lw["wp"], sv["x"], dxn, shift, scale, lw["norm_g"])
    grads = dict(norm_g=g_ng, conv_qkv=g_cq, a_log=g_al[:cfg.h, 0], dt_bias=g_dt[:cfg.h, 0], dn_norm_g=g_dng,
                 sg_ln_g=g_sglg, sg_ln_b=g_sglb, sg_w=g_sgw, sg_b=g_sgb[:, :, 0], cv_w=g_cvw, cv_b=g_cvb,
                 cv_ln_g=g_cvlg, cv_ln_b=g_cvlb, wp=g_wp, wo=g_wo)
    return dx, grads, (dshift, dscale, dgate)


def _local_step(cfg, xs, tgt, mods, lws, fg):
    nl = len(lws)
    saved = []
    for l in range(nl):
        xs, sv = _layer_fwd(cfg, xs, mods[l], lws[l])
        saved.append(sv)
    loss_b, dx, g_fg = loss_head(xs, tgt, fg)
    lg = [None] * nl
    dmods = [None] * nl
    for l in reversed(range(nl)):
        dx, lg[l], dmods[l] = _layer_bwd(cfg, dx, mods[l], lws[l], saved[l])
    return loss_b, dx, g_fg, lg, dmods


SMALL = ("norm_g", "conv_qkv", "a_log", "dt_bias", "dn_norm_g", "sg_ln_g", "sg_ln_b", "sg_w", "sg_b", "cv_w",
         "cv_b", "cv_ln_g", "cv_ln_b", "final_g", "b_ada")
PACK_N = 1024


def _pack(arrs):
    flat = jnp.concatenate([a.reshape(-1).astype(F32) for a in arrs])
    rows = -(-flat.shape[0] // PACK_N)
    rows = -(-rows // 8) * 8
    return jnp.pad(flat, (0, rows * PACK_N - flat.shape[0])).reshape(rows, PACK_N)


def _unpack(buf, shapes):
    flat = buf.reshape(-1)
    out, o = [], 0
    for s in shapes:
        n = 1
        for d_ in s:
            n *= d_
        out.append(flat[o:o + n].reshape(s))
        o += n
    return out


def kernel(x, c, norm_g, w_ada, b_ada, w_in, conv_qkv, a_log, dt_bias, dn_norm_g, sg_ln_g, sg_ln_b, sg_w, sg_b, cv_w, cv_b, cv_ln_g, cv_ln_b, w_out, final_g, loss_target, m_norm_g, m_w_ada, m_b_ada, m_w_in, m_conv_qkv, m_a_log, m_dt_bias, m_dn_norm_g, m_sg_ln_g, m_sg_ln_b, m_sg_w, m_sg_b, m_cv_w, m_cv_b, m_cv_ln_g, m_cv_ln_b, m_w_out, m_final_g, v_norm_g, v_w_ada, v_b_ada, v_w_in, v_conv_qkv, v_a_log, v_dt_bias, v_dn_norm_g, v_sg_ln_g, v_sg_ln_b, v_sg_w, v_sg_b, v_cv_w, v_cv_b, v_cv_ln_g, v_cv_ln_b, v_w_out, v_final_g):
    cfg = _Cfg(x, a_log, sg_w, cv_ln_g, cv_w, conv_qkv)
    nl, d, t, h = cfg.nl, cfg.d, cfg.t, cfg.h
    lh = nl // 2
    ax, ay, ac = _me()
    chip = 2 * ax + ay
    dev = 2 * chip + ac
    wts = dict(norm_g=norm_g, w_ada=w_ada, b_ada=b_ada, w_in=w_in, conv_qkv=conv_qkv, a_log=a_log, dt_bias=dt_bias,
               dn_norm_g=dn_norm_g, sg_ln_g=sg_ln_g, sg_ln_b=sg_ln_b, sg_w=sg_w, sg_b=sg_b, cv_w=cv_w, cv_b=cv_b,
               cv_ln_g=cv_ln_g, cv_ln_b=cv_ln_b, w_out=w_out, final_g=final_g)
    mom = dict(norm_g=m_norm_g, w_ada=m_w_ada, b_ada=m_b_ada, w_in=m_w_in, conv_qkv=m_conv_qkv, a_log=m_a_log,
               dt_bias=m_dt_bias, dn_norm_g=m_dn_norm_g, sg_ln_g=m_sg_ln_g, sg_ln_b=m_sg_ln_b, sg_w=m_sg_w,
               sg_b=m_sg_b, cv_w=m_cv_w, cv_b=m_cv_b, cv_ln_g=m_cv_ln_g, cv_ln_b=m_cv_ln_b, w_out=m_w_out,
               final_g=m_final_g)
    vel = dict(norm_g=v_norm_g, w_ada=v_w_ada, b_ada=v_b_ada, w_in=v_w_in, conv_qkv=v_conv_qkv, a_log=v_a_log,
               dt_bias=v_dt_bias, dn_norm_g=v_dn_norm_g, sg_ln_g=v_sg_ln_g, sg_ln_b=v_sg_ln_b, sg_w=v_sg_w,
               sg_b=v_sg_b, cv_w=v_cv_w, cv_b=v_cv_b, cv_ln_g=v_cv_ln_g, cv_ln_b=v_cv_ln_b, w_out=v_w_out,
               final_g=v_final_g)
    ada_cols = w_ada.shape[2]
    in_cols = w_in.shape[2]
    out_rows = w_out.shape[1]
    cq_cols = conv_qkv.shape[2]
    cvw_cols = cv_w.shape[2]

    c_all = all_gather8(jnp.pad(c, ((0, 7), (0, 0)))).reshape(NDEV, 8, d)[:, 0, :]
    b_loc = lax.dynamic_slice_in_dim(b_ada, chip * ada_cols, ada_cols, axis=1)[:, None, :]
    mod_part = ada_fwd(c_all, w_ada, b_loc)
    mod_all = all_gather8(mod_part.reshape(nl * NDEV, ada_cols)).reshape(NDEV, nl, NDEV, ada_cols)
    mod_me = lax.dynamic_index_in_dim(mod_all[0::2], dev, axis=2, keepdims=False)
    mod_me = jnp.moveaxis(mod_me, 0, 1).reshape(nl, 3, 1, d)

    win_b = w_in.astype(BF16).reshape(2, lh, d, in_cols)
    wout_b = w_out.astype(BF16).reshape(2, lh, out_rows, d)
    win_all, wout_all = gather_weights([win_b, wout_b])
    win_all = lax.dynamic_update_index_in_dim(win_all, win_b, chip, axis=0)
    wout_all = lax.dynamic_update_index_in_dim(wout_all, wout_b, chip, axis=0)
    win_all = win_all.reshape(NCHIP, nl, d, in_cols)
    wp_all = [_assemble_perm(cfg, [win_all[s, l] for s in range(NCHIP)]) for l in range(nl)]
    wo_all = jnp.moveaxis(wout_all.reshape(NCHIP, nl, out_rows, d), 0, 1).reshape(nl, NCHIP * out_rows, d)

    cq_all = all_gather8(conv_qkv.reshape(nl * cfg.k4, cq_cols)).reshape(NDEV, nl, cfg.k4, cq_cols)[0::2]
    cq_full = jnp.moveaxis(cq_all, 0, 2).reshape(nl, cfg.k4, NCHIP * cq_cols)
    cq_perm = _perm_cols_qkv(cfg, cq_full)
    kcp = -(-cfg.kc // 8) * 8
    cvw_all = all_gather8(jnp.pad(cv_w, ((0, 0), (0, kcp - cfg.kc), (0, 0))).reshape(nl * kcp, cvw_cols))
    cvw_all = cvw_all.reshape(NDEV, nl, kcp, cvw_cols)[0::2]
    cvw_full = jnp.moveaxis(cvw_all, 0, 2).reshape(nl, kcp, NCHIP * cvw_cols)[:, :cfg.kc]

    hp = -(-h // 8) * 8
    lws = []
    for l in range(nl):
        lws.append(dict(
            norm_g=norm_g[l][None], wp=wp_all[l], wo=wo_all[l], conv_qkv=cq_perm[l],
            alog_b=jnp.pad(jnp.broadcast_to(a_log[l][:, None], (h, LANES)), ((0, hp - h), (0, 0))),
            dtb_b=jnp.pad(jnp.broadcast_to(dt_bias[l][:, None], (h, LANES)), ((0, hp - h), (0, 0))),
            dn_norm_g=dn_norm_g[l][None], sg_ln_g=sg_ln_g[l][None], sg_ln_b=sg_ln_b[l][None], sg_w=sg_w[l],
            sg_bias_b=jnp.broadcast_to(sg_b[l][:, :, None], (cfg.g, LANES, LANES)),
            cv_w=cvw_full[l], cv_b=cv_b[l][None], cv_ln_g=cv_ln_g[l][None], cv_ln_b=cv_ln_b[l][None]))

    mods = [(mod_me[l, 0], mod_me[l, 1], mod_me[l, 2]) for l in range(nl)]
    loss_b, dx, g_fg, lg, dmods = _local_step(cfg, x[0], loss_target[0], mods, lws, final_g[None])
    grad_x = dx[None]

    dmod = jnp.stack([jnp.concatenate(dm, axis=1)[0] for dm in dmods])
    stack = lambda k: jnp.stack([g_[k] for g_ in lg])
    small_local = [stack(k).reshape(wts_shape) for k, wts_shape in
                   (("norm_g", (nl, d)), ("conv_qkv", (nl, cfg.k4, 3 * cfg.dn)), ("a_log", (nl, h)),
                    ("dt_bias", (nl, h)), ("dn_norm_g", (nl, LANES)), ("sg_ln_g", (nl, cfg.sg)),
                    ("sg_ln_b", (nl, cfg.sg)), ("sg_w", (nl, cfg.g, LANES, LANES)), ("sg_b", (nl, cfg.g, LANES)),
                    ("cv_w", (nl, cfg.kc, cfg.cv)), ("cv_b", (nl, cfg.cv)), ("cv_ln_g", (nl, cfg.cv)),
                    ("cv_ln_b", (nl, cfg.cv)))]
    small_local[1] = _unperm_cols_qkv(cfg, small_local[1])
    small_local += [g_fg[0], dmod, loss_b[0, 0:1]]
    shapes = [a.shape for a in small_local]
    packed = _pack(small_local)
    rows = packed.shape[0]
    gathered = all_gather8(packed).reshape(NDEV, rows, PACK_N)
    summed = _unpack(sum8(gathered), shapes)
    sgrads = dict(zip(SMALL, summed[:15]))
    loss = summed[15][0]
    sgrads["conv_qkv"] = lax.dynamic_slice_in_dim(sgrads["conv_qkv"], chip * cq_cols, cq_cols, axis=2)
    sgrads["cv_w"] = lax.dynamic_slice_in_dim(sgrads["cv_w"], chip * cvw_cols, cvw_cols, axis=2)

    off = sum(math.prod(s) for s in shapes[:14])
    dmod_all = gathered.reshape(NDEV, rows * PACK_N)[:, off:off + nl * 3 * d].reshape(NDEV, nl, 3 * d)
    dmod_loc = jnp.moveaxis(lax.dynamic_slice_in_dim(dmod_all, chip * ada_cols, ada_cols, axis=2), 0, 1)
    g_wada, d_wada, nm_wada, nv_wada = ada_bwd(c_all.T, dmod_loc, w_ada, m_w_ada, v_w_ada)

    g_in = jnp.stack([jnp.stack([jnp.concatenate(
        [_natural_shard(cfg, lg[hh * lh + j]["wp"], s, in_cols) for j in range(lh)], axis=0)
        for s in range(NCHIP)]) for hh in range(2)])
    g_wo = jnp.stack([g_["wo"] for g_ in lg]).reshape(2, lh, NCHIP, out_rows, d)
    g_out = jnp.moveaxis(g_wo, 2, 1).reshape(2, NCHIP, lh * out_rows, d)
    cflag = jnp.full((1, LANES), ac, F32)
    r1_in, r1_out = sibling_swap("swap_halves", [g_in, g_out], other_half=True)
    r2_in, r2_out = chip_exchange([pair_sum(cflag, g_in, r1_in), pair_sum(cflag, g_out, r1_out)])
    mine = lambda g_: lax.dynamic_index_in_dim(g_, chip, axis=0, keepdims=False)
    keep = lambda g_: lax.dynamic_index_in_dim(g_, ac, axis=0, keepdims=False)
    h_in = chip_sum(mine(keep(g_in)), mine(r1_in), r2_in)
    h_out = chip_sum(mine(keep(g_out)), mine(r1_out), r2_out)
    o_in, o_out = sibling_swap("join_halves", [h_in, h_out])

    v3 = lambda a, r_, c_: a.reshape(2, lh * r_, c_)
    grad_w_in, d_in_, nm_in, nv_in = adamw_halves("adamw_w_in", cflag, v3(w_in, d, in_cols), h_in, o_in,
                                                  v3(m_w_in, d, in_cols), v3(v_w_in, d, in_cols))
    grad_w_out, d_out_, nm_out, nv_out = adamw_halves("adamw_w_out", cflag, v3(w_out, out_rows, d), h_out, o_out,
                                                      v3(m_w_out, out_rows, d), v3(v_w_out, out_rows, d))
    grad_w_in = grad_w_in.reshape(w_in.shape)
    grad_w_out = grad_w_out.reshape(w_out.shape)
    sshapes = [wts[k].shape for k in SMALL]
    pk = lambda dct: _pack([dct[k] for k in SMALL])
    d_s, m_s, v_s = adamw("adamw_small", pk(wts), pk(sgrads), pk(mom), pk(vel))
    d_small = dict(zip(SMALL, _unpack(d_s, sshapes)))
    m_small = dict(zip(SMALL, _unpack(m_s, sshapes)))
    v_small = dict(zip(SMALL, _unpack(v_s, sshapes)))

    grads = dict(sgrads, w_ada=g_wada, w_in=grad_w_in, w_out=grad_w_out)
    deltas = dict(d_small, w_ada=d_wada, w_in=d_in_.reshape(w_in.shape), w_out=d_out_.reshape(w_out.shape))
    new_m = dict(m_small, w_ada=nm_wada, w_in=nm_in.reshape(w_in.shape), w_out=nm_out.reshape(w_out.shape))
    new_v = dict(v_small, w_ada=nv_wada, w_in=nv_in.reshape(w_in.shape), w_out=nv_out.reshape(w_out.shape))
    order = ("norm_g", "w_ada", "b_ada", "w_in", "conv_qkv", "a_log", "dt_bias", "dn_norm_g", "sg_ln_g", "sg_ln_b",
             "sg_w", "sg_b", "cv_w", "cv_b", "cv_ln_g", "cv_ln_b", "w_out", "final_g")
    return (loss, grad_x, *[grads[k] for k in order], *[deltas[k] for k in order], *[new_m[k] for k in order],
            *[new_v[k] for k in order])


def _perm_cols_qkv(cfg, w):
    lead = w.shape[:-1]
    return jnp.moveaxis(w.reshape(lead + (3, cfg.h, LANES)), -3, -2).reshape(lead + (3 * cfg.dn,))


def _unperm_cols_qkv(cfg, w):
    lead = w.shape[:-1]
    return jnp.moveaxis(w.reshape(lead + (cfg.h, 3, LANES)), -3, -2).reshape(lead + (3 * cfg.dn,))
```

```python
import functools
import math

import jax
import jax.numpy as jnp
from jax import lax
from jax.experimental import pallas as pl
from jax.experimental.pallas import tpu as pltpu

F32 = jnp.float32
BF16 = jnp.bfloat16
EPS = 1e-6
LN_EPS = 1e-5
LANES = 128
CHUNK = 64
SUBLANES = 8
HALO4 = 8
HALO31 = 32
NCHIP = 4
NDEV = 8
VMEM_LIMIT = 56 * 2 ** 20
ADAM_LR, ADAM_B1, ADAM_B2, ADAM_EPS, ADAM_WD, ADAM_STEP = 0.001, 0.9, 0.999, 1e-08, 0.01, 10
MESH = pl.DeviceIdType.MESH
ANY = pl.BlockSpec(memory_space=pl.ANY)


def _cp(sem=None, vmem=VMEM_LIMIT):
    return pltpu.CompilerParams(dimension_semantics=sem, vmem_limit_bytes=vmem)


def _tile(n, pref, mult):
    t = min(n, pref) // mult * mult
    while t > 0 and n % t:
        t -= mult
    return t if t > 0 else n


def _split(a):
    hi = a.astype(BF16)
    return hi, (a - hi.astype(F32)).astype(BF16)


def _raw_dot(a, b, ca, cb, hi):
    dn = (((ca,), (cb,)), ((), ()))
    if hi:
        ah, al = _split(a.astype(F32))
        bh, bl = _split(b.astype(F32))
        d3 = lambda x, y: lax.dot_general(x, y, dn, preferred_element_type=F32)
        return d3(ah, bh) + (d3(al, bh) + d3(ah, bl))
    return lax.dot_general(a.astype(BF16), b.astype(BF16), dn, preferred_element_type=F32)


@functools.partial(jax.custom_vjp, nondiff_argnums=(2, 3, 4))
def bdot(a, b, ca, cb, hi):
    return _raw_dot(a, b, ca, cb, hi)


def _bdot_fwd(a, b, ca, cb, hi):
    return _raw_dot(a, b, ca, cb, hi), (a, b)


def _bdot_bwd(ca, cb, hi, res, ct):
    a, b = res
    fa, fb = 1 - ca, 1 - cb
    da = _raw_dot(ct, b, 1, fb, hi) if ca == 1 else _raw_dot(b, ct, fb, 1, hi)
    db = _raw_dot(a, ct, fa, 0, hi) if cb == 0 else _raw_dot(ct, a, 0, fa, hi)
    return da.astype(a.dtype), db.astype(b.dtype)


bdot.defvjp(_bdot_fwd, _bdot_bwd)


def _sigmoid(x):
    return 1.0 / (1.0 + jnp.exp(-x))


def _silu(x):
    return x * _sigmoid(x)


def _gelu(x):
    return 0.5 * x * (1.0 + lax.erf(x * (2.0 ** -0.5)))


def _softplus(x):
    return jnp.maximum(x, 0.0) + jnp.log(1.0 + jnp.exp(-jnp.abs(x)))


def _modnorm(x, g, scale, shift):
    y = x * lax.rsqrt(jnp.mean(x * x, axis=-1, keepdims=True) + EPS)
    return (y * g) * (1.0 + scale) + shift


def _rmsnorm(x, g):
    return x * lax.rsqrt(jnp.mean(x * x, axis=-1, keepdims=True) + EPS) * g


def _layernorm(x, g, b):
    mu = jnp.mean(x, axis=-1, keepdims=True)
    xc = x - mu
    var = jnp.mean(xc * xc, axis=-1, keepdims=True)
    return xc * lax.rsqrt(var + LN_EPS) * g + b


def _l2norm(t):
    return t * lax.rsqrt(jnp.sum(t * t, axis=-1, keepdims=True) + EPS)


def _adamw_math(w, g, m, v):
    mn = ADAM_B1 * m + (1.0 - ADAM_B1) * g
    vn = ADAM_B2 * v + (1.0 - ADAM_B2) * (g * g)
    mh = mn / (1.0 - ADAM_B1 ** ADAM_STEP)
    vh = vn / (1.0 - ADAM_B2 ** ADAM_STEP)
    delta = -ADAM_LR * (mh / (jnp.sqrt(vh) + ADAM_EPS) + ADAM_WD * w)
    return delta, mn, vn


def _each(f, *lists):
    return [f(*xs) for xs in zip(*lists)]


def _wy(q, k, v, bcol, acol, alog, dtb, tinv=None):
    c = CHUNK
    r = lax.broadcasted_iota(jnp.int32, (c, c), 0)
    cc = lax.broadcasted_iota(jnp.int32, (c, c), 1)
    rr = lax.broadcasted_iota(jnp.int32, (c, 1), 0)
    tri_incl, tri_strict, eye = r >= cc, r > cc, r == cc
    beta = _each(_sigmoid, bcol)
    g = _each(lambda al, a_, dt: -jnp.exp(al) * _softplus(a_ + dt), alog, acol, dtb)
    gb = [jnp.broadcast_to(g_, (c, c)) for g_ in g]
    g_row = [jnp.sum(jnp.where(eye, b_, 0.0), axis=0, keepdims=True) for b_ in gb]
    gc_col = [jnp.sum(jnp.where(tri_incl, jnp.broadcast_to(gr, (c, c)), 0.0), axis=1, keepdims=True) for gr in g_row]
    gc_row = [jnp.sum(jnp.where(r <= cc, b_, 0.0), axis=0, keepdims=True) for b_ in gb]
    decay = _each(lambda gcc, gcr: jnp.where(tri_incl, jnp.exp(jnp.where(tri_incl, gcc - gcr, 0.0)), 0.0),
                  gc_col, gc_row)
    qs = [q_ * (q_.shape[-1] ** -0.5) for q_ in q]
    kb = _each(lambda k_, b_: k_ * b_, k, beta)
    a = _each(lambda kb_, k_, d_: jnp.where(tri_strict, bdot(kb_, k_, 1, 1, False) * d_, 0.0), kb, k, decay)
    dv = v[0].shape[-1]
    x = _each(lambda v_, b_, kb_, gcc: jnp.concatenate([v_ * b_, kb_ * jnp.exp(gcc)], axis=1), v, beta, kb, gc_col)
    if tinv is None:
        inv = [jnp.where(eye, 1.0, 0.0) - a_ for a_ in a]
        p = a
        for _ in range(5):
            p = _each(lambda p_: bdot(p_, p_, 1, 0, True), p)
            inv = _each(lambda t_, p_: t_ + bdot(t_, p_, 1, 0, True), inv, p)
        x = _each(lambda t_, x_: bdot(t_, x_, 1, 0, True), inv, x)
    else:
        x = _each(_solve_given_inverse, a, x, tinv)
    xv = [x_[:, :dv] for x_ in x]
    xk = [x_[:, dv:] for x_ in x]
    qk = _each(lambda q_, k_, d_: bdot(q_, k_, 1, 1, False) * d_, qs, k, decay)
    g_last = [jnp.sum(jnp.where(rr == c - 1, gcc, 0.0), axis=0, keepdims=True) for gcc in gc_col]
    qg = _each(lambda q_, gcc: q_ * jnp.exp(gcc), qs, gc_col)
    kd = _each(lambda k_, gl, gcc: k_ * jnp.exp(gl - gcc), k, g_last, gc_col)
    outs = (xv, xk, qg, kd, qk, [jnp.exp(gl) for gl in g_last])
    return outs + (inv,) if tinv is None else outs


@jax.custom_vjp
def _solve_given_inverse(a, rhs, tinv):
    return _raw_dot(tinv, rhs, 1, 0, True)


def _solve_given_inverse_fwd(a, rhs, tinv):
    x = _raw_dot(tinv, rhs, 1, 0, True)
    return x, (x, tinv)


def _solve_given_inverse_bwd(res, dx):
    x, tinv = res
    drhs = _raw_dot(tinv, dx, 0, 0, True)
    return -_raw_dot(drhs, x, 1, 1, True), drhs, jnp.zeros_like(tinv)


_solve_given_inverse.defvjp(_solve_given_inverse_fwd, _solve_given_inverse_bwd)


def _seq(u, w, qg, kd, qk, e, z, s, ng):
    v_new = _each(lambda u_, w_, s_: u_ - bdot(w_, s_, 1, 0, False), u, w, s)
    o1 = _each(lambda q_, s_: bdot(q_, s_, 1, 0, False), qg, s)
    o2 = _each(lambda qk_, vn: bdot(qk_, vn, 1, 0, False), qk, v_new)
    ds = _each(lambda kd_, vn: bdot(kd_, vn, 0, 0, False), kd, v_new)
    s_next = _each(lambda s_, e_, d_: s_ * e_ + d_, s, e, ds)
    y = _each(lambda a_, b_, z_: _rmsnorm(a_ + b_, ng) * _silu(z_), o1, o2, z)
    return y, s_next


def _sg_block(u, v, gt, lg, lb, w, bias):
    n = w.shape[0]
    pr = lax.broadcasted_iota(jnp.int32, (n, n), 0) // CHUNK
    pc = lax.broadcasted_iota(jnp.int32, (n, n), 1) // CHUNK
    wm = jnp.where(pr >= pc, w, 0.0)
    vl = _layernorm(_gelu(v), lg, lb)
    mixed = bdot(wm, vl, 1, 0, False) + bias
    return _gelu(u) * mixed * _silu(gt)


def _glu(a, b):
    return a * _sigmoid(b)


def _cv_post(conv, gate, cb, lg, lb):
    return _silu(_layernorm(conv + cb, lg, lb)) * _silu(gate)


def _qk_post(conv):
    return _l2norm(_silu(conv))


def _v_post(conv):
    return _silu(conv)


def in_proj(x, shift, scale, ng, wp):
    t, d = x.shape
    npc = wp.shape[1]
    tm, tn = _tile(t, 512, LANES), _tile(npc, 2432, LANES)

    def body(x_ref, sh_ref, sc_ref, g_ref, w_ref, p_ref, ht_ref, h_scr):
        @pl.when(pl.program_id(1) == 0)
        def _():
            h = _modnorm(x_ref[...], g_ref[...], sc_ref[...], sh_ref[...])
            h_scr[...] = h.astype(BF16)
            ht_ref[...] = h.T.astype(BF16)
        p_ref[...] = jnp.dot(h_scr[...], w_ref[...], preferred_element_type=F32)

    vec = pl.BlockSpec((1, d), lambda i, j: (0, 0))
    return pl.pallas_call(
        body, name="in_proj", grid=(t // tm, npc // tn),
        in_specs=[pl.BlockSpec((tm, d), lambda i, j: (i, 0)), vec, vec, vec,
                  pl.BlockSpec((d, tn), lambda i, j: (0, j))],
        out_specs=[pl.BlockSpec((tm, tn), lambda i, j: (i, j)), pl.BlockSpec((d, tm), lambda i, j: (0, i))],
        out_shape=[jax.ShapeDtypeStruct((t, npc), F32), jax.ShapeDtypeStruct((d, t), BF16)],
        scratch_shapes=[pltpu.VMEM((tm, d), BF16)],
        compiler_params=_cp(("parallel", "arbitrary")),
    )(x, shift, scale, ng, wp)


def _roll_bank(x, bank_ref, offsets):
    rows = x.shape[0]
    residues = sorted({o % SUBLANES for o in offsets})
    for slot, b in enumerate(residues):
        bank_ref[slot] = x if b == 0 else pltpu.roll(x, rows - b, 0)
    return {o: (residues.index(o % SUBLANES), o - o % SUBLANES) for o in offsets}


def _n_residues(offsets):
    return len({o % SUBLANES for o in offsets})


def conv_fwd(name, k, halo, pre_fn, pre, w, post_fns, extras, params, c_total, tc, tm_pref=512):
    t = pre[0][0].shape[0]
    tm = _tile(t, tm_pref, halo)
    npre, nex, npar = len(pre), len(extras), len(params)
    ngr = tc // LANES
    taps = [halo - (k - 1) + j for j in range(k)]

    def body(*refs):
        prev = refs[:npre]
        cur = refs[npre:2 * npre]
        w_ref = refs[2 * npre]
        ex = refs[2 * npre + 1:2 * npre + 1 + nex]
        par = refs[2 * npre + 1 + nex:2 * npre + 1 + nex + npar]
        out_ref, buf, bank = refs[-3], refs[-2], refs[-1]
        i = pl.program_id(1)
        pv = pre_fn(*[r[...] for r in prev])
        buf[0:halo, :] = jnp.where(i > 0, pv, 0.0)
        buf[halo:, :] = pre_fn(*[r[...] for r in cur])
        where = _roll_bank(buf[...], bank, taps)
        acc = None
        for j, o in enumerate(taps):
            slot, st = where[o]
            term = w_ref[j:j + 1, :] * bank[slot, st:st + tm, :]
            acc = term if acc is None else acc + term
        for gi in range(ngr):
            sl = slice(gi * LANES, (gi + 1) * LANES)
            out_ref[:, sl] = post_fns[gi](acc[:, sl], *[e[:, sl] for e in ex], *[p_[:, sl] for p_ in par])

    hb = tm // halo
    in_specs = ([pl.BlockSpec((halo, tc), functools.partial(lambda j, i, o: (jnp.maximum(i * hb - 1, 0), o + j), o=col // tc))
                 for _, col in pre]
                + [pl.BlockSpec((tm, tc), functools.partial(lambda j, i, o: (i, o + j), o=col // tc)) for _, col in pre]
                + [pl.BlockSpec((k, tc), lambda j, i: (0, j))]
                + [pl.BlockSpec((tm, tc), functools.partial(lambda j, i, o: (i, o + j), o=col // tc)) for _, col in extras]
                + [pl.BlockSpec((1, tc), lambda j, i: (0, j)) for _ in params])
    args = [a for a, _ in pre] * 2 + [w] + [a for a, _ in extras] + list(params)
    return pl.pallas_call(
        body, name=name, grid=(c_total // tc, t // tm), in_specs=in_specs,
        out_specs=pl.BlockSpec((tm, tc), lambda j, i: (i, j)),
        out_shape=jax.ShapeDtypeStruct((t, c_total), F32),
        scratch_shapes=[pltpu.VMEM((halo + tm, tc), F32), pltpu.VMEM((_n_residues(taps), halo + tm, tc), F32)],
        compiler_params=_cp(("parallel", "arbitrary")),
    )(*args)


def conv_bwd(name, k, halo, pre_fn, pre, w, post_fns, extras, params, dout, c_total, tc, tm_pref=512):
    t = pre[0][0].shape[0]
    tm = _tile(t, tm_pref, halo)
    npre, nex, npar = len(pre), len(extras), len(params)
    ngr = tc // LANES
    nout = npre + nex
    assert nout == 1 or c_total == tc
    nblk = t // tm
    ext = tm + halo
    taps = [halo - (k - 1) + j for j in range(k)]
    back = [k - 1 - j for j in range(k)]

    def body(*refs):
        it = iter(refs)
        prev = [next(it) for _ in range(npre)]
        cur = [next(it) for _ in range(npre)]
        nxt = [next(it) for _ in range(npre)]
        w_ref = next(it)
        ex_c = [next(it) for _ in range(nex)]
        ex_n = [next(it) for _ in range(nex)]
        par = [next(it) for _ in range(npar)]
        do_c, do_n = next(it), next(it)
        din_ref, dw_ref = next(it), next(it)
        dpar = [next(it) for _ in range(npar)]
        buf, dbuf, bank, dbank = next(it), next(it), next(it), next(it)
        i = pl.program_id(1)

        @pl.when(i == 0)
        def _():
            dw_ref[...] = jnp.zeros_like(dw_ref)
            for r in dpar:
                r[...] = jnp.zeros_like(r)

        buf[0:halo, :] = jnp.where(i > 0, pre_fn(*[r[...] for r in prev]), 0.0)
        cur_vals = [r[...] for r in cur]
        buf[halo:halo + tm, :] = pre_fn(*cur_vals)
        buf[halo + tm:, :] = pre_fn(*[r[...] for r in nxt])
        where = _roll_bank(buf[...], bank, taps)
        conv = None
        for j, o in enumerate(taps):
            slot, st = where[o]
            term = w_ref[j:j + 1, :] * bank[slot, st:st + ext, :]
            conv = term if conv is None else conv + term
        don = jnp.where(i < nblk - 1, do_n[...], 0.0)
        for gi in range(ngr):
            sl = slice(gi * LANES, (gi + 1) * LANES)
            pv = [p_[:, sl] for p_ in par]
            _, vj = jax.vjp(post_fns[gi], conv[:tm, sl], *[e[:, sl] for e in ex_c], *pv)
            gr = vj(do_c[:, sl])
            dbuf[0:tm, sl] = gr[0]
            for e in range(nex):
                din_ref[:, (npre + e) * tc + gi * LANES:(npre + e) * tc + (gi + 1) * LANES] = gr[1 + e].astype(din_ref.dtype)
            for q_ in range(npar):
                dpar[q_][:, sl] += gr[1 + nex + q_]
            _, vjn = jax.vjp(post_fns[gi], conv[tm:, sl], *[e[:, sl] for e in ex_n], *pv)
            dbuf[tm:, sl] = vjn(don[:, sl])[0]
        dcur = dbuf[0:tm, :]
        dwhere = _roll_bank(dbuf[...], dbank, back)
        dpre = None
        for j in range(k):
            slot, st = dwhere[back[j]]
            term = w_ref[j:j + 1, :] * dbank[slot, st:st + tm, :]
            dpre = term if dpre is None else dpre + term
            slot, st = where[taps[j]]
            dw_ref[j:j + 1, :] += jnp.sum(dcur * bank[slot, st:st + tm, :], axis=0, keepdims=True)
        _, vjp_pre = jax.vjp(pre_fn, *cur_vals)
        for e, gval in enumerate(vjp_pre(dpre)):
            din_ref[:, e * tc:(e + 1) * tc] = gval.astype(din_ref.dtype)

    hb = tm // halo
    last_h = t // halo - 1

    def spec(kind, col):
        o = col // tc
        if kind == "prev":
            return pl.BlockSpec((halo, tc), lambda j, i: (jnp.maximum(i * hb - 1, 0), o + j))
        if kind == "next":
            return pl.BlockSpec((halo, tc), lambda j, i: (jnp.minimum((i + 1) * hb, last_h), o + j))
        return pl.BlockSpec((tm, tc), lambda j, i: (i, o + j))

    in_specs = ([spec("prev", col) for _, col in pre] + [spec("cur", col) for _, col in pre]
                + [spec("next", col) for _, col in pre] + [pl.BlockSpec((k, tc), lambda j, i: (0, j))]
                + [spec("cur", col) for _, col in extras] + [spec("next", col) for _, col in extras]
                + [pl.BlockSpec((1, tc), lambda j, i: (0, j)) for _ in params]
                + [spec("cur", 0), spec("next", 0)])
    args = [a for a, _ in pre] * 3 + [w] + [a for a, _ in extras] * 2 + list(params) + [dout, dout]
    out = pl.pallas_call(
        body, name=name, grid=(c_total // tc, nblk), in_specs=in_specs,
        out_specs=[pl.BlockSpec((tm, nout * tc), lambda j, i: (i, j)), pl.BlockSpec((k, tc), lambda j, i: (0, j))]
        + [pl.BlockSpec((1, tc), lambda j, i: (0, j)) for _ in params],
        out_shape=[jax.ShapeDtypeStruct((t, nout * c_total), BF16), jax.ShapeDtypeStruct((k, c_total), F32)]
        + [jax.ShapeDtypeStruct((1, c_total), F32) for _ in params],
        scratch_shapes=[pltpu.VMEM((2 * halo + tm, tc), F32), pltpu.VMEM((ext, tc), F32),
                        pltpu.VMEM((_n_residues(taps), 2 * halo + tm, tc), F32),
                        pltpu.VMEM((_n_residues(back), ext, tc), F32)],
        compiler_params=_cp(("parallel", "arbitrary")),
    )(*args)
    return out[0], out[1], out[2:]


def _head_pick(ref_val, row):
    rr = lax.broadcasted_iota(jnp.int32, ref_val.shape, 0)
    v = jnp.sum(jnp.where(rr == row, ref_val, 0.0), axis=0, keepdims=True)
    ll = lax.broadcasted_iota(jnp.int32, v.shape, 1)
    return jnp.sum(jnp.where(ll == 0, v, 0.0), axis=1, keepdims=True)


def _lane_col(blk, lane_idx):
    ll = lax.broadcasted_iota(jnp.int32, blk.shape, 1)
    return jnp.sum(jnp.where(ll == lane_idx, blk, 0.0), axis=1, keepdims=True)


WY_HEADS = 2
WY_UNROLL = 4


def dn_wy_fwd(qkv, p, ba_col, alog_b, dtb_b, nheads):
    t = qkv.shape[0]
    tm = _tile(t, 512, CHUNK * WY_UNROLL)
    nc = tm // CHUNK
    hb = WY_HEADS
    hp = alog_b.shape[0]
    w_ = hb * LANES

    def body(qkv_ref, ba_ref, al_ref, dt_ref, u_ref, w_ref, qg_ref, kd_ref, qk_ref, e_ref, ti_ref):
        hblk = pl.program_id(1)
        alv, dtv = al_ref[...], dt_ref[...]

        def trip(cj, carry):
            units = [(cj * WY_UNROLL + cu, hl) for cu in range(WY_UNROLL) for hl in range(hb)]
            args = [[] for _ in range(7)]
            for ci, hl in units:
                rows = pl.ds(pl.multiple_of(ci * CHUNK, CHUNK), CHUNK)
                ba = ba_ref[rows, :]
                h = hblk * hb + hl
                for lst, val in zip(args, (qkv_ref[rows, hl * 384:hl * 384 + 128],
                                           qkv_ref[rows, hl * 384 + 128:hl * 384 + 256],
                                           qkv_ref[rows, hl * 384 + 256:hl * 384 + 384],
                                           _lane_col(ba, h), _lane_col(ba, nheads + h),
                                           _head_pick(alv, h), _head_pick(dtv, h))):
                    lst.append(val)
            outs = _wy(*args)
            for n, (ci, hl) in enumerate(units):
                rows = pl.ds(pl.multiple_of(ci * CHUNK, CHUNK), CHUNK)
                u, w, qg, kd, qk, e, ti = [o[n] for o in outs]
                sl = slice(hl * LANES, (hl + 1) * LANES)
                u_ref[rows, sl] = u
                w_ref[rows, sl] = w.astype(BF16)
                qg_ref[rows, sl] = qg.astype(BF16)
                kd_ref[rows, sl] = kd.astype(BF16)
                qk_ref[rows, hl * LANES:hl * LANES + CHUNK] = qk.astype(BF16)
                qk_ref[rows, hl * LANES + CHUNK:(hl + 1) * LANES] = jnp.zeros((CHUNK, LANES - CHUNK), BF16)
                e_ref[ci, :, sl] = jnp.broadcast_to(e, (1, LANES))
                ti_ref[rows, hl * LANES:hl * LANES + CHUNK] = ti
                ti_ref[rows, hl * LANES + CHUNK:(hl + 1) * LANES] = jnp.zeros((CHUNK, LANES - CHUNK), F32)
            return carry

        lax.fori_loop(0, nc // WY_UNROLL, trip, 0)

    bc = ba_col // LANES
    blk = pl.BlockSpec((tm, w_), lambda i, h: (i, h))
    tab = pl.BlockSpec((hp, LANES), lambda i, h: (0, 0))
    wide = lambda dt: jax.ShapeDtypeStruct((t, nheads * LANES), dt)
    return pl.pallas_call(
        body, name="dn_wy_fwd", grid=(t // tm, nheads // hb),
        in_specs=[pl.BlockSpec((tm, hb * 384), lambda i, h: (i, h)), pl.BlockSpec((tm, LANES), lambda i, h: (i, bc)),
                  tab, tab],
        out_specs=[blk] * 5 + [pl.BlockSpec((nc, 1, w_), lambda i, h: (i, 0, h)), blk],
        out_shape=[wide(F32), wide(BF16), wide(BF16), wide(BF16), wide(BF16),
                   jax.ShapeDtypeStruct((t // CHUNK, 1, nheads * LANES), F32), wide(F32)],
        compiler_params=_cp(("parallel", "parallel")),
    )(qkv, p, alog_b, dtb_b)


def dn_seq_fwd(u, w, qg, kd, qk, e, p, z_col, ng, nheads, hb):
    t = u.shape[0]
    tm = _tile(t, 512, CHUNK)
    nc = tm // CHUNK
    w_ = hb * LANES

    def body(u_ref, w_ref, qg_ref, kd_ref, qk_ref, e_ref, z_ref, ng_ref, y_ref, ss_ref, s_scr):
        i, hblk = pl.program_id(0), pl.program_id(1)
        for hl in range(hb):
            @pl.when(i == 0)
            def _():
                s_scr[hblk * hb + hl] = jnp.zeros((LANES, LANES), F32)
        ngv = ng_ref[...]

        def chunk(ci, carry):
            rows = pl.ds(pl.multiple_of(ci * CHUNK, CHUNK), CHUNK)
            ev = e_ref[ci]
            sls = [slice(hl * LANES, (hl + 1) * LANES) for hl in range(hb)]
            s = [s_scr[hblk * hb + hl] for hl in range(hb)]
            for hl in range(hb):
                ss_ref[ci, sls[hl], :] = s[hl]
            y, sn = _seq([u_ref[rows, sl] for sl in sls], [w_ref[rows, sl].astype(F32) for sl in sls],
                         [qg_ref[rows, sl].astype(F32) for sl in sls], [kd_ref[rows, sl].astype(F32) for sl in sls],
                         [qk_ref[rows, sl][:, :CHUNK].astype(F32) for sl in sls], [ev[:, sl] for sl in sls],
                         [z_ref[rows, sl] for sl in sls], s, ngv)
            for hl in range(hb):
                y_ref[rows, sls[hl]] = y[hl]
                s_scr[hblk * hb + hl] = sn[hl]
            return carry

        lax.fori_loop(0, nc, chunk, 0)

    zc = z_col // w_
    blk = pl.BlockSpec((tm, w_), lambda i, h: (i, h))
    return pl.pallas_call(
        body, name="dn_seq_fwd", grid=(t // tm, nheads // hb),
        in_specs=[blk] * 5 + [pl.BlockSpec((nc, 1, w_), lambda i, h: (i, 0, h)),
                              pl.BlockSpec((tm, w_), lambda i, h: (i, zc + h)),
                              pl.BlockSpec((1, LANES), lambda i, h: (0, 0))],
        out_specs=[blk, pl.BlockSpec((nc, w_, LANES), lambda i, h: (i, h, 0))],
        out_shape=[jax.ShapeDtypeStruct((t, nheads * LANES), F32),
                   jax.ShapeDtypeStruct((t // CHUNK, nheads * LANES, LANES), F32)],
        scratch_shapes=[pltpu.VMEM((nheads, LANES, LANES), F32)],
        compiler_params=_cp(("arbitrary", "arbitrary")),
    )(u, w, qg, kd, qk, e, p, ng)


def dn_seq_bwd(u, w, qg, kd, qk, e, p, z_col, ng, ss, dy, nheads, hb):
    t = u.shape[0]
    tm = _tile(t, 2048 // hb, CHUNK)
    nc = tm // CHUNK
    nblk = t // tm
    w_ = hb * LANES

    def body(u_ref, w_ref, qg_ref, kd_ref, qk_ref, e_ref, z_ref, ng_ref, ss_ref, dy_ref,
             du_ref, dw_ref, dqg_ref, dkd_ref, dqk_ref, de_ref, dz_ref, dng_ref, ds_scr):
        i, hblk = pl.program_id(0), pl.program_id(1)

        @pl.when((i == 0) & (hblk == 0))
        def _():
            dng_ref[...] = jnp.zeros_like(dng_ref)

        for hl in range(hb):
            @pl.when(i == 0)
            def _():
                ds_scr[hblk * hb + hl] = jnp.zeros((LANES, LANES), F32)
        ngv = ng_ref[...]

        def chunk(cj, carry):
            ci = nc - 1 - cj
            rows = pl.ds(pl.multiple_of(ci * CHUNK, CHUNK), CHUNK)
            ev = e_ref[ci]
            sls = [slice(hl * LANES, (hl + 1) * LANES) for hl in range(hb)]
            _, vj = jax.vjp(_seq, [u_ref[rows, sl] for sl in sls], [w_ref[rows, sl].astype(F32) for sl in sls],
                            [qg_ref[rows, sl].astype(F32) for sl in sls], [kd_ref[rows, sl].astype(F32) for sl in sls],
                            [qk_ref[rows, sl][:, :CHUNK].astype(F32) for sl in sls], [ev[:, sl] for sl in sls],
                            [z_ref[rows, sl] for sl in sls], [ss_ref[ci, sl, :] for sl in sls], ngv)
            du, dw, dqg, dkd, dqk, de, dz, dsp, dng = vj(([dy_ref[rows, sl] for sl in sls],
                                                          [ds_scr[hblk * hb + hl] for hl in range(hb)]))
            for hl, sl in enumerate(sls):
                du_ref[rows, sl] = du[hl]
                dw_ref[rows, sl] = dw[hl]
                dqg_ref[rows, sl] = dqg[hl]
                dkd_ref[rows, sl] = dkd[hl]
                dqk_ref[rows, hl * LANES:hl * LANES + CHUNK] = dqk[hl]
                dqk_ref[rows, hl * LANES + CHUNK:(hl + 1) * LANES] = jnp.zeros((CHUNK, LANES - CHUNK), F32)
                de_ref[ci, :, sl] = de[hl]
                dz_ref[rows, sl] = dz[hl]
                ds_scr[hblk * hb + hl] = dsp[hl]
            dng_ref[...] += dng
            return carry

        lax.fori_loop(0, nc, chunk, 0)

    zc = z_col // w_
    rv = lambda i: nblk - 1 - i
    blk = pl.BlockSpec((tm, w_), lambda i, h: (rv(i), h))
    eblk = pl.BlockSpec((nc, 1, w_), lambda i, h: (rv(i), 0, h))
    one = pl.BlockSpec((1, LANES), lambda i, h: (0, 0))
    wide = jax.ShapeDtypeStruct((t, nheads * LANES), F32)
    return pl.pallas_call(
        body, name="dn_seq_bwd", grid=(nblk, nheads // hb),
        in_specs=[blk] * 5 + [eblk, pl.BlockSpec((tm, w_), lambda i, h: (rv(i), zc + h)), one,
                              pl.BlockSpec((nc, w_, LANES), lambda i, h: (rv(i), h, 0)), blk],
        out_specs=[blk] * 5 + [eblk, blk, one],
        out_shape=[wide] * 5 + [jax.ShapeDtypeStruct((t // CHUNK, 1, nheads * LANES), F32), wide,
                                jax.ShapeDtypeStruct((1, LANES), F32)],
        scratch_shapes=[pltpu.VMEM((nheads, LANES, LANES), F32)],
        compiler_params=_cp(("arbitrary", "arbitrary")),
    )(u, w, qg, kd, qk, e, p, ng, ss, dy)


def dn_wy_bwd(qkv, p, ba_col, alog_b, dtb_b, ti, du, dw, dqg, dkd, dqk, de, nheads):
    t = qkv.shape[0]
    tm = _tile(t, 512, CHUNK * WY_UNROLL)
    nc = tm // CHUNK
    hb = WY_HEADS
    hp = alog_b.shape[0]
    w_ = hb * LANES

    def body(qkv_ref, ba_ref, al_ref, dt_ref, du_ref, dw_ref, dqg_ref, dkd_ref, dqk_ref, de_ref, ti_ref,
             dqkv_ref, dba_ref, dal_ref, ddt_ref):
        i, hblk = pl.program_id(0), pl.program_id(1)

        @pl.when((i == 0) & (hblk == 0))
        def _():
            dal_ref[...] = jnp.zeros_like(dal_ref)
            ddt_ref[...] = jnp.zeros_like(ddt_ref)

        @pl.when(hblk == 0)
        def _():
            dba_ref[...] = jnp.zeros_like(dba_ref)

        alv, dtv = al_ref[...], dt_ref[...]
        lane = lax.broadcasted_iota(jnp.int32, (CHUNK, LANES), 1)
        rowp = lax.broadcasted_iota(jnp.int32, (hp, LANES), 0)

        def trip(cj, carry):
            units = [(cj * WY_UNROLL + cu, hl) for cu in range(WY_UNROLL) for hl in range(hb)]
            args = [[] for _ in range(8)]
            cts = [[] for _ in range(6)]
            for ci, hl in units:
                rows = pl.ds(pl.multiple_of(ci * CHUNK, CHUNK), CHUNK)
                ba = ba_ref[rows, :]
                h = hblk * hb + hl
                sl = slice(hl * LANES, (hl + 1) * LANES)
                for lst, val in zip(args, (qkv_ref[rows, hl * 384:hl * 384 + 128],
                                           qkv_ref[rows, hl * 384 + 128:hl * 384 + 256],
                                           qkv_ref[rows, hl * 384 + 256:hl * 384 + 384],
                                           _lane_col(ba, h), _lane_col(ba, nheads + h),
                                           _head_pick(alv, h), _head_pick(dtv, h), ti_ref[rows, sl][:, :CHUNK])):
                    lst.append(val)
                de11 = jnp.sum(de_ref[ci][:, sl], axis=1, keepdims=True)
                for lst, val in zip(cts, (du_ref[rows, sl], dw_ref[rows, sl], dqg_ref[rows, sl], dkd_ref[rows, sl],
                                          dqk_ref[rows, sl][:, :CHUNK], de11)):
                    lst.append(val)
            _, vj = jax.vjp(_wy, *args)
            grads = vj(tuple(cts))
            for n, (ci, hl) in enumerate(units):
                rows = pl.ds(pl.multiple_of(ci * CHUNK, CHUNK), CHUNK)
                h = hblk * hb + hl
                dq, dk, dv, dbc, dac, dal, ddt = [g_[n] for g_ in grads[:7]]
                dqkv_ref[rows, hl * 384:hl * 384 + 128] = dq
                dqkv_ref[rows, hl * 384 + 128:hl * 384 + 256] = dk
                dqkv_ref[rows, hl * 384 + 256:hl * 384 + 384] = dv
                dba_ref[rows, :] += jnp.where(lane == h, dbc, 0.0) + jnp.where(lane == nheads + h, dac, 0.0)
                dal_ref[...] += jnp.where(rowp == h, dal, 0.0)
                ddt_ref[...] += jnp.where(rowp == h, ddt, 0.0)
            return carry

        lax.fori_loop(0, nc // WY_UNROLL, trip, 0)

    bc = ba_col // LANES
    blk = pl.BlockSpec((tm, w_), lambda i, h: (i, h))
    tab = pl.BlockSpec((hp, LANES), lambda i, h: (0, 0))
    return pl.pallas_call(
        body, name="dn_wy_bwd", grid=(t // tm, nheads // hb),
        in_specs=[pl.BlockSpec((tm, hb * 384), lambda i, h: (i, h)), pl.BlockSpec((tm, LANES), lambda i, h: (i, bc)),
                  tab, tab] + [blk] * 5 + [pl.BlockSpec((nc, 1, w_), lambda i, h: (i, 0, h)), blk],
        out_specs=[pl.BlockSpec((tm, hb * 384), lambda i, h: (i, h)), pl.BlockSpec((tm, LANES), lambda i, h: (i, 0)),
                   tab, tab],
        out_shape=[jax.ShapeDtypeStruct((t, nheads * 384), F32), jax.ShapeDtypeStruct((t, LANES), F32),
                   jax.ShapeDtypeStruct((hp, LANES), F32), jax.ShapeDtypeStruct((hp, LANES), F32)],
        compiler_params=_cp(("arbitrary", "arbitrary")),
    )(qkv, p, alog_b, dtb_b, du, dw, dqg, dkd, dqk, de, ti)


def sg_fwd(p, col, sg, lg, lb, w, bias_b):
    t = p.shape[0]
    ng_ = sg // LANES
    tm = _tile(t, 256, LANES)
    cb = col // sg

    def body(u_ref, v_ref, g_ref, lg_ref, lb_ref, w_ref, b_ref, y_ref):
        for n in range(tm // LANES):
            rs = slice(n * LANES, (n + 1) * LANES)
            for gi in range(ng_):
                sl = slice(gi * LANES, (gi + 1) * LANES)
                y_ref[rs, sl] = _sg_block(u_ref[rs, sl], v_ref[rs, sl], g_ref[rs, sl], lg_ref[:, sl], lb_ref[:, sl],
                                          w_ref[gi], b_ref[gi])

    vec = pl.BlockSpec((1, sg), lambda i: (0, 0))
    full = pl.BlockSpec((ng_, LANES, LANES), lambda i: (0, 0, 0))
    return pl.pallas_call(
        body, name="sg_fwd", grid=(t // tm,),
        in_specs=[pl.BlockSpec((tm, sg), lambda i: (i, cb)), pl.BlockSpec((tm, sg), lambda i: (i, cb + 1)),
                  pl.BlockSpec((tm, sg), lambda i: (i, cb + 2)), vec, vec, full, full],
        out_specs=pl.BlockSpec((tm, sg), lambda i: (i, 0)),
        out_shape=jax.ShapeDtypeStruct((t, sg), F32),
        compiler_params=_cp(("parallel",)),
    )(p, p, p, lg, lb, w, bias_b)


def sg_bwd(p, col, sg, lg, lb, w, bias_b, dy):
    t = p.shape[0]
    ng_ = sg // LANES
    tm = _tile(t, 256, LANES)
    cb = col // sg

    def body(u_ref, v_ref, g_ref, lg_ref, lb_ref, w_ref, b_ref, dy_ref, d_ref, dlg_ref, dlb_ref, dw_ref, db_ref):
        @pl.when(pl.program_id(0) == 0)
        def _():
            for r in (dlg_ref, dlb_ref, dw_ref, db_ref):
                r[...] = jnp.zeros_like(r)

        for n in range(tm // LANES):
            rs = slice(n * LANES, (n + 1) * LANES)
            for gi in range(ng_):
                sl = slice(gi * LANES, (gi + 1) * LANES)
                _, vj = jax.vjp(_sg_block, u_ref[rs, sl], v_ref[rs, sl], g_ref[rs, sl], lg_ref[:, sl], lb_ref[:, sl],
                                w_ref[gi], b_ref[gi])
                du, dv, dg, dlg, dlb, dw, db = vj(dy_ref[rs, sl])
                d_ref[rs, gi * LANES:(gi + 1) * LANES] = du.astype(BF16)
                d_ref[rs, sg + gi * LANES:sg + (gi + 1) * LANES] = dv.astype(BF16)
                d_ref[rs, 2 * sg + gi * LANES:2 * sg + (gi + 1) * LANES] = dg.astype(BF16)
                dlg_ref[:, sl] += dlg
                dlb_ref[:, sl] += dlb
                dw_ref[gi] += dw
                db_ref[gi] += jnp.broadcast_to(jnp.sum(db, axis=1, keepdims=True), (LANES, LANES))

    vec = pl.BlockSpec((1, sg), lambda i: (0, 0))
    full = pl.BlockSpec((ng_, LANES, LANES), lambda i: (0, 0, 0))
    return pl.pallas_call(
        body, name="sg_bwd", grid=(t // tm,),
        in_specs=[pl.BlockSpec((tm, sg), lambda i: (i, cb)), pl.BlockSpec((tm, sg), lambda i: (i, cb + 1)),
                  pl.BlockSpec((tm, sg), lambda i: (i, cb + 2)), vec, vec, full, full,
                  pl.BlockSpec((tm, sg), lambda i: (i, 0))],
        out_specs=[pl.BlockSpec((tm, 3 * sg), lambda i: (i, 0)), vec, vec, full, full],
        out_shape=[jax.ShapeDtypeStruct((t, 3 * sg), BF16), jax.ShapeDtypeStruct((1, sg), F32),
                   jax.ShapeDtypeStruct((1, sg), F32), jax.ShapeDtypeStruct((ng_, LANES, LANES), F32),
                   jax.ShapeDtypeStruct((ng_, LANES, LANES), F32)],
        compiler_params=_cp(("arbitrary",)),
    )(p, p, p, lg, lb, w, bias_b, dy)


def out_proj(x, y_dn, y_sg, y_cv, wo, gate):
    t, d = x.shape
    dn, sg, cv = y_dn.shape[1], y_sg.shape[1], y_cv.shape[1]
    dmix = dn + sg + cv
    tm = _tile(t, 256, LANES)

    def body(x_ref, a_ref, b_ref, c_ref, w_ref, g_ref, xn_ref, y_ref, yt_ref):
        a, b, c = a_ref[...], b_ref[...], c_ref[...]
        y = (jnp.dot(a.astype(BF16), w_ref[0:dn, :], preferred_element_type=F32)
             + jnp.dot(b.astype(BF16), w_ref[dn:dn + sg, :], preferred_element_type=F32)
             + jnp.dot(c.astype(BF16), w_ref[dn + sg:, :], preferred_element_type=F32))
        y_ref[...] = y
        xn_ref[...] = x_ref[...] + g_ref[...] * y
        yt_ref[0:dn, :] = a.T.astype(BF16)
        yt_ref[dn:dn + sg, :] = b.T.astype(BF16)
        yt_ref[dn + sg:, :] = c.T.astype(BF16)

    row = lambda w_: pl.BlockSpec((tm, w_), lambda i: (i, 0))
    return pl.pallas_call(
        body, name="out_proj", grid=(t // tm,),
        in_specs=[row(d), row(dn), row(sg), row(cv), pl.BlockSpec((dmix, d), lambda i: (0, 0)),
                  pl.BlockSpec((1, d), lambda i: (0, 0))],
        out_specs=[row(d), row(d), pl.BlockSpec((dmix, tm), lambda i: (0, i))],
        out_shape=[jax.ShapeDtypeStruct((t, d), F32), jax.ShapeDtypeStruct((t, d), F32),
                   jax.ShapeDtypeStruct((dmix, t), BF16)],
        compiler_params=_cp(("parallel",)),
    )(x, y_dn, y_sg, y_cv, wo, gate)


def out_proj_bwd(dxn, y, gate, wo, dn, sg, cv):
    t, d = dxn.shape
    dmix = dn + sg + cv
    tm = _tile(t, 256, LANES)

    def body(dx_ref, y_ref, g_ref, w_ref, da_ref, db_ref, dc_ref, dyb_ref, dg_ref):
        @pl.when(pl.program_id(0) == 0)
        def _():
            dg_ref[...] = jnp.zeros_like(dg_ref)
        dx = dx_ref[...]
        dg_ref[...] += jnp.sum(dx * y_ref[...], axis=0, keepdims=True)
        dyb = (dx * g_ref[...]).astype(BF16)
        dyb_ref[...] = dyb
        dcat = lax.dot_general(dyb, w_ref[...], (((1,), (1,)), ((), ())), preferred_element_type=F32)
        da_ref[...] = dcat[:, 0:dn]
        db_ref[...] = dcat[:, dn:dn + sg]
        dc_ref[...] = dcat[:, dn + sg:]

    row = lambda w_: pl.BlockSpec((tm, w_), lambda i: (i, 0))
    vec = pl.BlockSpec((1, d), lambda i: (0, 0))
    return pl.pallas_call(
        body, name="out_proj_bwd", grid=(t // tm,),
        in_specs=[row(d), row(d), vec, pl.BlockSpec((dmix, d), lambda i: (0, 0))],
        out_specs=[row(dn), row(sg), row(cv), row(d), vec],
        out_shape=[jax.ShapeDtypeStruct((t, dn), F32), jax.ShapeDtypeStruct((t, sg), F32),
                   jax.ShapeDtypeStruct((t, cv), F32), jax.ShapeDtypeStruct((t, d), BF16),
                   jax.ShapeDtypeStruct((1, d), F32)],
        compiler_params=_cp(("arbitrary",)),
    )(dxn, y, gate, wo)


def matmul_acc(name, at, b):
    m, t = at.shape
    n = b.shape[1]
    tm, tn, tk = _tile(m, 1024, LANES), _tile(n, 2432, LANES), _tile(t, 1024, LANES)

    def body(a_ref, b_ref, o_ref):
        @pl.when(pl.program_id(2) == 0)
        def _():
            o_ref[...] = jnp.zeros_like(o_ref)
        o_ref[...] += jnp.dot(a_ref[...], b_ref[...], preferred_element_type=F32)

    return pl.pallas_call(
        body, name=name, grid=(m // tm, n // tn, t // tk),
        in_specs=[pl.BlockSpec((tm, tk), lambda i, j, k: (i, k)), pl.BlockSpec((tk, tn), lambda i, j, k: (k, j))],
        out_specs=pl.BlockSpec((tm, tn), lambda i, j, k: (i, j)),
        out_shape=jax.ShapeDtypeStruct((m, n), F32),
        compiler_params=_cp(("parallel", "parallel", "arbitrary")),
    )(at, b)


def in_proj_bwd(dp, wp, x, dxn, shift, scale, ng):
    t, d = x.shape
    npc = wp.shape[1]
    tm, tk = _tile(t, 512, LANES), _tile(npc, 2432, LANES)

    def mm_body(dp_ref, w_ref, dh_ref):
        @pl.when(pl.program_id(1) == 0)
        def _():
            dh_ref[...] = jnp.zeros_like(dh_ref)
        dh_ref[...] += lax.dot_general(dp_ref[...], w_ref[...], (((1,), (1,)), ((), ())), preferred_element_type=F32)

    dh = pl.pallas_call(
        mm_body, name="in_proj_bwd", grid=(t // tm, npc // tk),
        in_specs=[pl.BlockSpec((tm, tk), lambda i, k: (i, k)), pl.BlockSpec((d, tk), lambda i, k: (0, k))],
        out_specs=pl.BlockSpec((tm, d), lambda i, k: (i, 0)), out_shape=jax.ShapeDtypeStruct((t, d), F32),
        compiler_params=_cp(("parallel", "arbitrary")),
    )(dp, wp)

    tr = _tile(t, 256, 8)

    def norm_body(dh_ref, x_ref, dxn_ref, sh_ref, sc_ref, g_ref, dx_ref, dg_ref, dsc_ref, dsh_ref):
        @pl.when(pl.program_id(0) == 0)
        def _():
            for r in (dg_ref, dsc_ref, dsh_ref):
                r[...] = jnp.zeros_like(r)
        _, vj = jax.vjp(_modnorm, x_ref[...], g_ref[...], sc_ref[...], sh_ref[...])
        dx, dg, dsc, dsh = vj(dh_ref[...])
        dx_ref[...] = dxn_ref[...] + dx
        dg_ref[...] += dg
        dsc_ref[...] += dsc
        dsh_ref[...] += dsh

    vec = pl.BlockSpec((1, d), lambda i: (0, 0))
    row = pl.BlockSpec((tr, d), lambda i: (i, 0))
    return pl.pallas_call(
        norm_body, name="modnorm_bwd", grid=(t // tr,), in_specs=[row, row, row, vec, vec, vec],
        out_specs=[row, vec, vec, vec],
        out_shape=[jax.ShapeDtypeStruct((t, d), F32)] + [jax.ShapeDtypeStruct((1, d), F32)] * 3,
        compiler_params=_cp(("arbitrary",)),
    )(dh, x, dxn, shift, scale, ng)


def loss_head(x, tgt, fg):
    t, d = x.shape
    tm = _tile(t, 512, 8)

    def body(x_ref, t_ref, g_ref, l_ref, dx_ref, dg_ref):
        @pl.when(pl.program_id(0) == 0)
        def _():
            l_ref[...] = jnp.zeros_like(l_ref)
            dg_ref[...] = jnp.zeros_like(dg_ref)
        y, vj = jax.vjp(_rmsnorm, x_ref[...], g_ref[...])
        err = y - t_ref[...]
        part = 0.5 * jnp.sum(jnp.sum(err * err, axis=1, keepdims=True), axis=0, keepdims=True) / d
        l_ref[...] += jnp.broadcast_to(part, l_ref.shape)
        dx, dg = vj(err / d)
        dx_ref[...] = dx
        dg_ref[...] += dg

    row = pl.BlockSpec((tm, d), lambda i: (i, 0))
    vec = pl.BlockSpec((1, d), lambda i: (0, 0))
    return pl.pallas_call(
        body, name="loss_head", grid=(t // tm,), in_specs=[row, row, vec],
        out_specs=[pl.BlockSpec((1, LANES), lambda i: (0, 0)), row, vec],
        out_shape=[jax.ShapeDtypeStruct((1, LANES), F32), jax.ShapeDtypeStruct((t, d), F32),
                   jax.ShapeDtypeStruct((1, d), F32)],
        compiler_params=_cp(("arbitrary",)),
    )(x, tgt, fg)


def adamw(name, w, g, m, v):
    r, c = w.shape
    tr = _tile(r, 256, 8) if r % 8 == 0 else r

    def body(w_ref, g_ref, m_ref, v_ref, d_ref, mo_ref, vo_ref):
        d_ref[...], mo_ref[...], vo_ref[...] = _adamw_math(w_ref[...], g_ref[...], m_ref[...], v_ref[...])

    blk = pl.BlockSpec((tr, c), lambda i: (i, 0))
    return pl.pallas_call(
        body, name=name, grid=(r // tr,), in_specs=[blk] * 4, out_specs=[blk] * 3,
        out_shape=[jax.ShapeDtypeStruct((r, c), F32)] * 3, compiler_params=_cp(("parallel",)),
    )(w, g, m, v)


def ada_fwd(c_all, w_ada, b_loc):
    nl, d, cols = w_ada.shape
    nb = c_all.shape[0]
    tn = _tile(cols, 512, LANES)

    def body(c_ref, w_ref, b_ref, o_ref):
        ca = _silu(c_ref[...]).astype(BF16)
        o_ref[0] = jnp.dot(ca, w_ref[0].astype(BF16), preferred_element_type=F32) + b_ref[0]

    return pl.pallas_call(
        body, name="ada_fwd", grid=(nl, cols // tn),
        in_specs=[pl.BlockSpec((nb, d), lambda l, j: (0, 0)), pl.BlockSpec((1, d, tn), lambda l, j: (l, 0, j)),
                  pl.BlockSpec((1, 1, tn), lambda l, j: (l, 0, j))],
        out_specs=pl.BlockSpec((1, nb, tn), lambda l, j: (l, 0, j)),
        out_shape=jax.ShapeDtypeStruct((nl, nb, cols), F32),
        compiler_params=_cp(("parallel", "parallel")),
    )(c_all, w_ada, b_loc)


def ada_bwd(c_all_t, dmod_loc, w, m, v):
    nl, d, cols = w.shape
    nb = c_all_t.shape[1]
    tr = _tile(d, 256, 8)

    def body(c_ref, dm_ref, w_ref, m_ref, v_ref, g_ref, d_ref, mo_ref, vo_ref):
        ca = _silu(c_ref[...])
        dm = dm_ref[0]
        g = _lane_col(ca, 0) * dm[0:1, :]
        for b in range(1, nb):
            g = g + _lane_col(ca, b) * dm[b:b + 1, :]
        g_ref[0] = g
        d_ref[0], mo_ref[0], vo_ref[0] = _adamw_math(w_ref[0], g, m_ref[0], v_ref[0])

    blk = pl.BlockSpec((1, tr, cols), lambda l, i: (l, i, 0))
    return pl.pallas_call(
        body, name="ada_bwd", grid=(nl, d // tr),
        in_specs=[pl.BlockSpec((tr, nb), lambda l, i: (i, 0)), pl.BlockSpec((1, nb, cols), lambda l, i: (l, 0, 0)),
                  blk, blk, blk],
        out_specs=[blk] * 4, out_shape=[jax.ShapeDtypeStruct((nl, d, cols), F32)] * 4,
        compiler_params=_cp(("parallel", "parallel")),
    )(c_all_t, dmod_loc, w, m, v)


def sum8(g):
    _, r, c = g.shape
    tr = _tile(r, 256, 8)

    def body(g_ref, o_ref):
        acc = g_ref[0]
        for k in range(1, NDEV):
            acc = acc + g_ref[k]
        o_ref[...] = acc

    return pl.pallas_call(
        body, name="sum8", grid=(r // tr,), in_specs=[pl.BlockSpec((NDEV, tr, c), lambda i: (0, i, 0))],
        out_specs=pl.BlockSpec((tr, c), lambda i: (i, 0)), out_shape=jax.ShapeDtypeStruct((r, c), F32),
        compiler_params=_cp(("parallel",)),
    )(g)


def pair_sum(cflag, g, r1):
    _, ns, r, c = g.shape
    tr = _tile(r, 256, 8)

    def body(cf_ref, g0_ref, g1_ref, r_ref, ob_ref):
        keep = jnp.where(cf_ref[0:1, 0:1] == 0.0, g0_ref[0], g1_ref[0])
        ob_ref[...] = (keep + r_ref[...]).astype(BF16)

    blk = pl.BlockSpec((1, tr, c), lambda s, i: (s, i, 0))
    return pl.pallas_call(
        body, name="pair_sum", grid=(ns, r // tr),
        in_specs=[pl.BlockSpec((1, LANES), lambda s, i: (0, 0)), pl.BlockSpec((1, 1, tr, c), lambda s, i: (0, s, i, 0)),
                  pl.BlockSpec((1, 1, tr, c), lambda s, i: (1, s, i, 0)), blk], out_specs=blk,
        out_shape=jax.ShapeDtypeStruct((ns, r, c), BF16), compiler_params=_cp(("parallel", "parallel")),
    )(cflag, g, g, r1)


def chip_sum(own, r1, r2):
    r, c = own.shape
    tr = _tile(r, 256, 8)

    def body(g_ref, r1_ref, r2_ref, o_ref):
        acc = g_ref[...] + r1_ref[...]
        for k in range(NCHIP - 1):
            acc = acc + r2_ref[k].astype(F32)
        o_ref[...] = acc

    blk = pl.BlockSpec((tr, c), lambda i: (i, 0))
    return pl.pallas_call(
        body, name="chip_sum", grid=(r // tr,),
        in_specs=[blk, blk, pl.BlockSpec((NCHIP - 1, tr, c), lambda i: (0, i, 0))], out_specs=blk,
        out_shape=jax.ShapeDtypeStruct((r, c), F32), compiler_params=_cp(("parallel",)),
    )(own, r1, r2)


def adamw_halves(name, cflag, w, own, recv, m, v):
    _, r, c = w.shape
    tr = _tile(r, 256, 8)

    def body(cf_ref, w_ref, a_ref, b_ref, m_ref, v_ref, g_ref, d_ref, mo_ref, vo_ref):
        is_own = cf_ref[0:1, 0:1] == pl.program_id(0).astype(F32)
        g = jnp.where(is_own, a_ref[...], b_ref[...])
        g_ref[0] = g
        d_ref[0], mo_ref[0], vo_ref[0] = _adamw_math(w_ref[0], g, m_ref[0], v_ref[0])

    blk = pl.BlockSpec((1, tr, c), lambda h, i: (h, i, 0))
    hlf = pl.BlockSpec((tr, c), lambda h, i: (i, 0))
    return pl.pallas_call(
        body, name=name, grid=(2, r // tr),
        in_specs=[pl.BlockSpec((1, LANES), lambda h, i: (0, 0)), blk, hlf, hlf, blk, blk], out_specs=[blk] * 4,
        out_shape=[jax.ShapeDtypeStruct(w.shape, F32)] * 4, compiler_params=_cp(("parallel", "parallel")),
    )(cflag, w, own, recv, m, v)


def _me():
    return lax.axis_index("x"), lax.axis_index("y"), lax.axis_index("c")


_FLIPS = ((1, 0), (0, 1), (1, 1))


def all_gather8(v):
    m_per, n = v.shape

    def body(x_ref, out_ref, send_sems, recv_sems, local_sem):
        x, y, c = _me()
        me, sibling = (x, y, c), (x, y, 1 - c)
        chips = [(x ^ fx, y ^ fy) for fx, fy in _FLIPS]

        def rows(px, py, pc):
            return out_ref.at[pl.ds((4 * px + 2 * py + pc) * m_per, m_per), :]

        def copy(k, block, to, src=None):
            return pltpu.make_async_remote_copy(
                src_ref=rows(*block) if src is None else src, dst_ref=rows(*block),
                send_sem=send_sems.at[k], recv_sem=recv_sems.at[k], device_id=to, device_id_type=MESH)

        mine = pltpu.make_async_copy(x_ref, rows(*me), local_sem)
        mine.start()
        first = [copy(0, me, sibling, src=x_ref)]
        first += [copy(1 + j, me, (*chip, c), src=x_ref) for j, chip in enumerate(chips)]
        for cp in first:
            cp.start()
        passed = [copy(4 + j, (*chip, c), sibling) for j, chip in enumerate(chips)]
        for j, chip in enumerate(chips):
            copy(1 + j, (*chip, c), me).wait_recv()
            passed[j].start()
        copy(0, sibling, me).wait_recv()
        for j, chip in enumerate(chips):
            copy(4 + j, (*chip, 1 - c), me).wait_recv()
        for cp in first + passed:
            cp.wait_send()
        mine.wait()

    return pl.pallas_call(
        body, name="all_gather8", out_shape=jax.ShapeDtypeStruct((NDEV * m_per, n), v.dtype),
        in_specs=[pl.BlockSpec(memory_space=pltpu.VMEM)], out_specs=pl.BlockSpec(memory_space=pltpu.VMEM),
        scratch_shapes=[pltpu.SemaphoreType.DMA((7,)), pltpu.SemaphoreType.DMA((7,)), pltpu.SemaphoreType.DMA],
        compiler_params=pltpu.CompilerParams(vmem_limit_bytes=VMEM_LIMIT),
    )(v)


def gather_weights(ws):
    na = len(ws)

    def body(*refs):
        srcs, outs = refs[:na], refs[na:2 * na]
        send_sems, recv_sems = refs[2 * na:]
        x, y, c = _me()
        chip = 2 * x + y
        sibling = (x, y, 1 - c)
        peers = [(x ^ fx, y ^ fy) for fx, fy in _FLIPS]

        def ici(a, j, half_of, to):
            return pltpu.make_async_remote_copy(
                src_ref=srcs[a].at[c], dst_ref=outs[a].at[half_of, c], send_sem=send_sems.at[a * 6 + j],
                recv_sem=recv_sems.at[a * 6 + j], device_id=to, device_id_type=MESH)

        def d2d(a, j, slot, half):
            return pltpu.make_async_remote_copy(
                src_ref=outs[a].at[slot, half], dst_ref=outs[a].at[slot, half], send_sem=send_sems.at[a * 6 + 3 + j],
                recv_sem=recv_sems.at[a * 6 + 3 + j], device_id=sibling, device_id_type=MESH)

        sends = []
        for a in range(na):
            for j, (px, py) in enumerate(peers):
                cp = ici(a, j, chip, (px, py, c))
                cp.start()
                sends.append(cp)
        for a in range(na):
            for j, (px, py) in enumerate(peers):
                ici(a, j, 2 * px + py, (px, py, c)).wait_recv()
                cp = d2d(a, j, 2 * px + py, c)
                cp.start()
                sends.append(cp)
        for a in range(na):
            for j, (px, py) in enumerate(peers):
                d2d(a, j, 2 * px + py, 1 - c).wait_recv()
        for cp in sends:
            cp.wait_send()

    return pl.pallas_call(
        body, name="gather_weights",
        out_shape=[jax.ShapeDtypeStruct((NCHIP,) + w.shape, w.dtype) for w in ws],
        in_specs=[ANY] * na, out_specs=[ANY] * na,
        scratch_shapes=[pltpu.SemaphoreType.DMA((6 * na,)), pltpu.SemaphoreType.DMA((6 * na,))],
    )(*ws)


def sibling_swap(name, gs, other_half=False):
    na = len(gs)

    def body(*refs):
        srcs, outs = refs[:na], refs[na:2 * na]
        send_sems, recv_sems = refs[2 * na:]
        x, y, c = _me()
        cps = [pltpu.make_async_remote_copy(
            src_ref=srcs[a].at[1 - c] if other_half else srcs[a], dst_ref=outs[a], send_sem=send_sems.at[a],
            recv_sem=recv_sems.at[a], device_id=(x, y, 1 - c), device_id_type=MESH) for a in range(na)]
        for cp in cps:
            cp.start()
        for cp in cps:
            cp.wait()

    return pl.pallas_call(
        body, name=name, out_shape=[jax.ShapeDtypeStruct(g.shape[1:] if other_half else g.shape, g.dtype) for g in gs],
        in_specs=[ANY] * na, out_specs=[ANY] * na,
        scratch_shapes=[pltpu.SemaphoreType.DMA((na,)), pltpu.SemaphoreType.DMA((na,))],
    )(*gs)


def chip_exchange(ps):
    na = len(ps)

    def body(*refs):
        srcs, outs = refs[:na], refs[na:2 * na]
        send_sems, recv_sems = refs[2 * na:]
        x, y, c = _me()
        cps = []
        for a in range(na):
            for j, (fx, fy) in enumerate(_FLIPS):
                px, py = x ^ fx, y ^ fy
                cps.append(pltpu.make_async_remote_copy(
                    src_ref=srcs[a].at[2 * px + py], dst_ref=outs[a].at[j], send_sem=send_sems.at[a * 3 + j],
                    recv_sem=recv_sems.at[a * 3 + j], device_id=(px, py, c), device_id_type=MESH))
        for cp in cps:
            cp.start()
        for cp in cps:
            cp.wait()

    return pl.pallas_call(
        body, name="chip_exchange",
        out_shape=[jax.ShapeDtypeStruct((NCHIP - 1,) + p_.shape[1:], p_.dtype) for p_ in ps],
        in_specs=[ANY] * na, out_specs=[ANY] * na,
        scratch_shapes=[pltpu.SemaphoreType.DMA((3 * na,)), pltpu.SemaphoreType.DMA((3 * na,))],
    )(*ps)


class _Cfg:
    def __init__(self, x, a_log, sg_w, cv_ln_g, cv_w, conv_qkv):
        self.t, self.d = x.shape[1], x.shape[2]
        self.nl, self.h = a_log.shape
        self.dn = self.h * LANES
        self.g = sg_w.shape[1]
        self.sg = self.g * LANES
        self.cv = cv_ln_g.shape[1]
        self.kc = cv_w.shape[1]
        self.k4 = conv_qkv.shape[1]
        self.o_z = 3 * self.dn
        self.o_sg = 4 * self.dn
        self.o_cv = self.o_sg + 3 * self.sg
        self.o_ba = self.o_cv + 3 * self.cv
        self.npc = self.o_ba + LANES
        self.d_in = self.o_ba + 2 * self.h
        self.dmix = self.dn + self.sg + self.cv
        self.hb_fwd = _tile(self.h, 8, 1)
        self.hb_bwd = _tile(self.h, 8, 1)


def _runs(cfg):
    dn, h = cfg.dn, cfg.h
    runs = [(part * dn + hd * LANES, hd * 3 * LANES + part * LANES, LANES) for part in range(3) for hd in range(h)]
    return runs + [(3 * dn, 3 * dn, dn), (4 * dn, cfg.o_ba, 2 * h), (4 * dn + 2 * h, 4 * dn, cfg.o_ba - 4 * dn)]


def _assemble_perm(cfg, shards):
    cols = shards[0].shape[-1]
    pieces = []
    for nat, _, wdt in sorted(_runs(cfg), key=lambda r_: r_[1]):
        a = nat
        while a < nat + wdt:
            s = a // cols
            b = min(nat + wdt, (s + 1) * cols)
            pieces.append(shards[s][..., a - s * cols:b - s * cols])
            a = b
    pieces.append(jnp.zeros(shards[0].shape[:-1] + (cfg.npc - cfg.o_ba - 2 * cfg.h,), shards[0].dtype))
    return jnp.concatenate(pieces, axis=-1)


def _natural_shard(cfg, g, s, cols):
    lo, hi = s * cols, (s + 1) * cols
    pieces = []
    for nat, perm, wdt in sorted(_runs(cfg)):
        a, b = max(nat, lo), min(nat + wdt, hi)
        if a < b:
            pieces.append(g[..., perm + a - nat:perm + b - nat])
    return jnp.concatenate(pieces, axis=-1)


def _layer_fwd(cfg, x, mod, lw):
    shift, scale, gate = mod
    p, ht = in_proj(x, shift, scale, lw["norm_g"], lw["wp"])
    qk_post = [_qk_post, _qk_post, _v_post]
    qkv = conv_fwd("dn_pre_fwd", cfg.k4, HALO4, lambda a: a, [(p, 0)], lw["conv_qkv"], qk_post, [], [],
                   3 * cfg.dn, 3 * LANES)
    wy = dn_wy_fwd(qkv, p, cfg.o_ba, lw["alog_b"], lw["dtb_b"], cfg.h)
    y_dn, ss = dn_seq_fwd(*wy[:6], p, cfg.o_z, lw["dn_norm_g"], cfg.h, cfg.hb_fwd)
    y_sg = sg_fwd(p, cfg.o_sg, cfg.sg, lw["sg_ln_g"], lw["sg_ln_b"], lw["sg_w"], lw["sg_bias_b"])
    cv_post = [_cv_post] * (cfg.cv // LANES)
    y_cv = conv_fwd("cv_fwd", cfg.kc, HALO31, _glu, [(p, cfg.o_cv), (p, cfg.o_cv + cfg.cv)], lw["cv_w"], cv_post,
                    [(p, cfg.o_cv + 2 * cfg.cv)], [lw["cv_b"], lw["cv_ln_g"], lw["cv_ln_b"]], cfg.cv, cfg.cv)
    xn, y, yt = out_proj(x, y_dn, y_sg, y_cv, lw["wo"], gate)
    return xn, dict(x=x, p=p, ht=ht, qkv=qkv, wy=wy, ss=ss, y=y, yt=yt)


def _layer_bwd(cfg, dxn, mod, lw, sv):
    shift, scale, gate = mod
    p = sv["p"]
    d_dn, d_sg, d_cv, dyb, dgate = out_proj_bwd(dxn, sv["y"], gate, lw["wo"], cfg.dn, cfg.sg, cfg.cv)
    g_wo = matmul_acc("w_out_grad", sv["yt"], dyb)
    cv_post = [_cv_post] * (cfg.cv // LANES)
    dcv, g_cvw, (g_cvb, g_cvlg, g_cvlb) = conv_bwd(
        "cv_bwd", cfg.kc, HALO31, _glu, [(p, cfg.o_cv), (p, cfg.o_cv + cfg.cv)], lw["cv_w"], cv_post,
        [(p, cfg.o_cv + 2 * cfg.cv)], [lw["cv_b"], lw["cv_ln_g"], lw["cv_ln_b"]], d_cv, cfg.cv, cfg.cv, tm_pref=256)
    dsg, g_sglg, g_sglb, g_sgw, g_sgb = sg_bwd(p, cfg.o_sg, cfg.sg, lw["sg_ln_g"], lw["sg_ln_b"], lw["sg_w"],
                                               lw["sg_bias_b"], d_sg)
    *dwy, dz, g_dng = dn_seq_bwd(*sv["wy"][:6], p, cfg.o_z, lw["dn_norm_g"], sv["ss"], d_dn, cfg.h, cfg.hb_bwd)
    dqkv, dba, g_al, g_dt = dn_wy_bwd(sv["qkv"], p, cfg.o_ba, lw["alog_b"], lw["dtb_b"], sv["wy"][6], *dwy, cfg.h)
    qk_post = [_qk_post, _qk_post, _v_post]
    dqkv_pre, g_cq, _ = conv_bwd("dn_pre_bwd", cfg.k4, HALO4, lambda a: a, [(p, 0)], lw["conv_qkv"], qk_post, [], [],
                                 dqkv, 3 * cfg.dn, 3 * LANES)
    dp = jnp.concatenate([dqkv_pre, dz.astype(BF16), dsg, dcv, dba.astype(BF16)], axis=1)
    g_wp = matmul_acc("w_in_grad", sv["ht"], dp)
    dx, g_ng, dscale, dshift = in_proj_bwd(dp, lw["wp"], sv["x"], dxn, shift, scale, lw["norm_g"])
    grads = dict(norm_g=g_ng, conv_qkv=g_cq, a_log=g_al[:cfg.h, 0], dt_bias=g_dt[:cfg.h, 0], dn_norm_g=g_dng,
                 sg_ln_g=g_sglg, sg_ln_b=g_sglb, sg_w=g_sgw, sg_b=g_sgb[:, :, 0], cv_w=g_cvw, cv_b=g_cvb,
                 cv_ln_g=g_cvlg, cv_ln_b=g_cvlb, wp=g_wp, wo=g_wo)
    return dx, grads, (dshift, dscale, dgate)


def _local_step(cfg, xs, tgt, mods, lws, fg):
    nl = len(lws)
    saved = []
    for l in range(nl):
        xs, sv = _layer_fwd(cfg, xs, mods[l], lws[l])
        saved.append(sv)
    loss_b, dx, g_fg = loss_head(xs, tgt, fg)
    lg = [None] * nl
    dmods = [None] * nl
    for l in reversed(range(nl)):
        dx, lg[l], dmods[l] = _layer_bwd(cfg, dx, mods[l], lws[l], saved[l])
    return loss_b, dx, g_fg, lg, dmods


SMALL = ("norm_g", "conv_qkv", "a_log", "dt_bias", "dn_norm_g", "sg_ln_g", "sg_ln_b", "sg_w", "sg_b", "cv_w",
         "cv_b", "cv_ln_g", "cv_ln_b", "final_g", "b_ada")
PACK_N = 1024


def _pack(arrs):
    flat = jnp.concatenate([a.reshape(-1).astype(F32) for a in arrs])
    rows = -(-flat.shape[0] // PACK_N)
    rows = -(-rows // 8) * 8
    return jnp.pad(flat, (0, rows * PACK_N - flat.shape[0])).reshape(rows, PACK_N)


def _unpack(buf, shapes):
    flat = buf.reshape(-1)
    out, o = [], 0
    for s in shapes:
        n = 1
        for d_ in s:
            n *= d_
        out.append(flat[o:o + n].reshape(s))
        o += n
    return out


def kernel(x, c, norm_g, w_ada, b_ada, w_in, conv_qkv, a_log, dt_bias, dn_norm_g, sg_ln_g, sg_ln_b, sg_w, sg_b, cv_w, cv_b, cv_ln_g, cv_ln_b, w_out, final_g, loss_target, m_norm_g, m_w_ada, m_b_ada, m_w_in, m_conv_qkv, m_a_log, m_dt_bias, m_dn_norm_g, m_sg_ln_g, m_sg_ln_b, m_sg_w, m_sg_b, m_cv_w, m_cv_b, m_cv_ln_g, m_cv_ln_b, m_w_out, m_final_g, v_norm_g, v_w_ada, v_b_ada, v_w_in, v_conv_qkv, v_a_log, v_dt_bias, v_dn_norm_g, v_sg_ln_g, v_sg_ln_b, v_sg_w, v_sg_b, v_cv_w, v_cv_b, v_cv_ln_g, v_cv_ln_b, v_w_out, v_final_g):
    cfg = _Cfg(x, a_log, sg_w, cv_ln_g, cv_w, conv_qkv)
    nl, d, t, h = cfg.nl, cfg.d, cfg.t, cfg.h
    lh = nl // 2
    ax, ay, ac = _me()
    chip = 2 * ax + ay
    dev = 2 * chip + ac
    wts = dict(norm_g=norm_g, w_ada=w_ada, b_ada=b_ada, w_in=w_in, conv_qkv=conv_qkv, a_log=a_log, dt_bias=dt_bias,
               dn_norm_g=dn_norm_g, sg_ln_g=sg_ln_g, sg_ln_b=sg_ln_b, sg_w=sg_w, sg_b=sg_b, cv_w=cv_w, cv_b=cv_b,
               cv_ln_g=cv_ln_g, cv_ln_b=cv_ln_b, w_out=w_out, final_g=final_g)
    mom = dict(norm_g=m_norm_g, w_ada=m_w_ada, b_ada=m_b_ada, w_in=m_w_in, conv_qkv=m_conv_qkv, a_log=m_a_log,
               dt_bias=m_dt_bias, dn_norm_g=m_dn_norm_g, sg_ln_g=m_sg_ln_g, sg_ln_b=m_sg_ln_b, sg_w=m_sg_w,
               sg_b=m_sg_b, cv_w=m_cv_w, cv_b=m_cv_b, cv_ln_g=m_cv_ln_g, cv_ln_b=m_cv_ln_b, w_out=m_w_out,
               final_g=m_final_g)
    vel = dict(norm_g=v_norm_g, w_ada=v_w_ada, b_ada=v_b_ada, w_in=v_w_in, conv_qkv=v_conv_qkv, a_log=v_a_log,
               dt_bias=v_dt_bias, dn_norm_g=v_dn_norm_g, sg_ln_g=v_sg_ln_g, sg_ln_b=v_sg_ln_b, sg_w=v_sg_w,
               sg_b=v_sg_b, cv_w=v_cv_w, cv_b=v_cv_b, cv_ln_g=v_cv_ln_g, cv_ln_b=v_cv_ln_b, w_out=v_w_out,
               final_g=v_final_g)
    ada_cols = w_ada.shape[2]
    in_cols = w_in.shape[2]
    out_rows = w_out.shape[1]
    cq_cols = conv_qkv.shape[2]
    cvw_cols = cv_w.shape[2]

    c_all = all_gather8(jnp.pad(c, ((0, 7), (0, 0)))).reshape(NDEV, 8, d)[:, 0, :]
    b_loc = lax.dynamic_slice_in_dim(b_ada, chip * ada_cols, ada_cols, axis=1)[:, None, :]
    mod_part = ada_fwd(c_all, w_ada, b_loc)
    mod_all = all_gather8(mod_part.reshape(nl * NDEV, ada_cols)).reshape(NDEV, nl, NDEV, ada_cols)
    mod_me = lax.dynamic_index_in_dim(mod_all[0::2], dev, axis=2, keepdims=False)
    mod_me = jnp.moveaxis(mod_me, 0, 1).reshape(nl, 3, 1, d)

    win_b = w_in.astype(BF16).reshape(2, lh, d, in_cols)
    wout_b = w_out.astype(BF16).reshape(2, lh, out_rows, d)
    win_all, wout_all = gather_weights([win_b, wout_b])
    win_all = lax.dynamic_update_index_in_dim(win_all, win_b, chip, axis=0)
    wout_all = lax.dynamic_update_index_in_dim(wout_all, wout_b, chip, axis=0)
    win_all = win_all.reshape(NCHIP, nl, d, in_cols)
    wp_all = [_assemble_perm(cfg, [win_all[s, l] for s in range(NCHIP)]) for l in range(nl)]
    wo_all = jnp.moveaxis(wout_all.reshape(NCHIP, nl, out_rows, d), 0, 1).reshape(nl, NCHIP * out_rows, d)

    cq_all = all_gather8(conv_qkv.reshape(nl * cfg.k4, cq_cols)).reshape(NDEV, nl, cfg.k4, cq_cols)[0::2]
    cq_full = jnp.moveaxis(cq_all, 0, 2).reshape(nl, cfg.k4, NCHIP * cq_cols)
    cq_perm = _perm_cols_qkv(cfg, cq_full)
    kcp = -(-cfg.kc // 8) * 8
    cvw_all = all_gather8(jnp.pad(cv_w, ((0, 0), (0, kcp - cfg.kc), (0, 0))).reshape(nl * kcp, cvw_cols))
    cvw_all = cvw_all.reshape(NDEV, nl, kcp, cvw_cols)[0::2]
    cvw_full = jnp.moveaxis(cvw_all, 0, 2).reshape(nl, kcp, NCHIP * cvw_cols)[:, :cfg.kc]

    hp = -(-h // 8) * 8
    lws = []
    for l in range(nl):
        lws.append(dict(
            norm_g=norm_g[l][None], wp=wp_all[l], wo=wo_all[l], conv_qkv=cq_perm[l],
            alog_b=jnp.pad(jnp.broadcast_to(a_log[l][:, None], (h, LANES)), ((0, hp - h), (0, 0))),
            dtb_b=jnp.pad(jnp.broadcast_to(dt_bias[l][:, None], (h, LANES)), ((0, hp - h), (0, 0))),
            dn_norm_g=dn_norm_g[l][None], sg_ln_g=sg_ln_g[l][None], sg_ln_b=sg_ln_b[l][None], sg_w=sg_w[l],
            sg_bias_b=jnp.broadcast_to(sg_b[l][:, :, None], (cfg.g, LANES, LANES)),
            cv_w=cvw_full[l], cv_b=cv_b[l][None], cv_ln_g=cv_ln_g[l][None], cv_ln_b=cv_ln_b[l][None]))

    mods = [(mod_me[l, 0], mod_me[l, 1], mod_me[l, 2]) for l in range(nl)]
    loss_b, dx, g_fg, lg, dmods = _local_step(cfg, x[0], loss_target[0], mods, lws, final_g[None])
    grad_x = dx[None]

    dmod = jnp.stack([jnp.concatenate(dm, axis=1)[0] for dm in dmods])
    stack = lambda k: jnp.stack([g_[k] for g_ in lg])
    small_local = [stack(k).reshape(wts_shape) for k, wts_shape in
                   (("norm_g", (nl, d)), ("conv_qkv", (nl, cfg.k4, 3 * cfg.dn)), ("a_log", (nl, h)),
                    ("dt_bias", (nl, h)), ("dn_norm_g", (nl, LANES)), ("sg_ln_g", (nl, cfg.sg)),
                    ("sg_ln_b", (nl, cfg.sg)), ("sg_w", (nl, cfg.g, LANES, LANES)), ("sg_b", (nl, cfg.g, LANES)),
                    ("cv_w", (nl, cfg.kc, cfg.cv)), ("cv_b", (nl, cfg.cv)), ("cv_ln_g", (nl, cfg.cv)),
                    ("cv_ln_b", (nl, cfg.cv)))]
    small_local[1] = _unperm_cols_qkv(cfg, small_local[1])
    small_local += [g_fg[0], dmod, loss_b[0, 0:1]]
    shapes = [a.shape for a in small_local]
    packed = _pack(small_local)
    rows = packed.shape[0]
    gathered = all_gather8(packed).reshape(NDEV, rows, PACK_N)
    summed = _unpack(sum8(gathered), shapes)
    sgrads = dict(zip(SMALL, summed[:15]))
    loss = summed[15][0]
    sgrads["conv_qkv"] = lax.dynamic_slice_in_dim(sgrads["conv_qkv"], chip * cq_cols, cq_cols, axis=2)
    sgrads["cv_w"] = lax.dynamic_slice_in_dim(sgrads["cv_w"], chip * cvw_cols, cvw_cols, axis=2)

    off = sum(math.prod(s) for s in shapes[:14])
    dmod_all = gathered.reshape(NDEV, rows * PACK_N)[:, off:off + nl * 3 * d].reshape(NDEV, nl, 3 * d)
    dmod_loc = jnp.moveaxis(lax.dynamic_slice_in_dim(dmod_all, chip * ada_cols, ada_cols, axis=2), 0, 1)
    g_wada, d_wada, nm_wada, nv_wada = ada_bwd(c_all.T, dmod_loc, w_ada, m_w_ada, v_w_ada)

    g_in = jnp.stack([jnp.stack([jnp.concatenate(
        [_natural_shard(cfg, lg[hh * lh + j]["wp"], s, in_cols) for j in range(lh)], axis=0)
        for s in range(NCHIP)]) for hh in range(2)])
    g_wo = jnp.stack([g_["wo"] for g_ in lg]).reshape(2, lh, NCHIP, out_rows, d)
    g_out = jnp.moveaxis(g_wo, 2, 1).reshape(2, NCHIP, lh * out_rows, d)
    cflag = jnp.full((1, LANES), ac, F32)
    r1_in, r1_out = sibling_swap("swap_halves", [g_in, g_out], other_half=True)
    r2_in, r2_out = chip_exchange([pair_sum(cflag, g_in, r1_in), pair_sum(cflag, g_out, r1_out)])
    mine = lambda g_: lax.dynamic_index_in_dim(g_, chip, axis=0, keepdims=False)
    keep = lambda g_: lax.dynamic_index_in_dim(g_, ac, axis=0, keepdims=False)
    h_in = chip_sum(mine(keep(g_in)), mine(r1_in), r2_in)
    h_out = chip_sum(mine(keep(g_out)), mine(r1_out), r2_out)
    o_in, o_out = sibling_swap("join_halves", [h_in, h_out])

    v3 = lambda a, r_, c_: a.reshape(2, lh * r_, c_)
    grad_w_in, d_in_, nm_in, nv_in = adamw_halves("adamw_w_in", cflag, v3(w_in, d, in_cols), h_in, o_in,
                                                  v3(m_w_in, d, in_cols), v3(v_w_in, d, in_cols))
    grad_w_out, d_out_, nm_out, nv_out = adamw_halves("adamw_w_out", cflag, v3(w_out, out_rows, d), h_out, o_out,
                                                      v3(m_w_out, out_rows, d), v3(v_w_out, out_rows, d))
    grad_w_in = grad_w_in.reshape(w_in.shape)
    grad_w_out = grad_w_out.reshape(w_out.shape)
    sshapes = [wts[k].shape for k in SMALL]
    pk = lambda dct: _pack([dct[k] for k in SMALL])
    d_s, m_s, v_s = adamw("adamw_small", pk(wts), pk(sgrads), pk(mom), pk(vel))
    d_small = dict(zip(SMALL, _unpack(d_s, sshapes)))
    m_small = dict(zip(SMALL, _unpack(m_s, sshapes)))
    v_small = dict(zip(SMALL, _unpack(v_s, sshapes)))

    grads = dict(sgrads, w_ada=g_wada, w_in=grad_w_in, w_out=grad_w_out)
    deltas = dict(d_small, w_ada=d_wada, w_in=d_in_.reshape(w_in.shape), w_out=d_out_.reshape(w_out.shape))
    new_m = dict(m_small, w_ada=nm_wada, w_in=nm_in.reshape(w_in.shape), w_out=nm_out.reshape(w_out.shape))
    new_v = dict(v_small, w_ada=nv_wada, w_in=nv_in.reshape(w_in.shape), w_out=nv_out.reshape(w_out.shape))
    order = ("norm_g", "w_ada", "b_ada", "w_in", "conv_qkv", "a_log", "dt_bias", "dn_norm_g", "sg_ln_g", "sg_ln_b",
             "sg_w", "sg_b", "cv_w", "cv_b", "cv_ln_g", "cv_ln_b", "w_out", "final_g")
    return (loss, grad_x, *[grads[k] for k in order], *[deltas[k] for k in order], *[new_m[k] for k in order],
            *[new_v[k] for k in order])


def _perm_cols_qkv(cfg, w):
    lead = w.shape[:-1]
    return jnp.moveaxis(w.reshape(lead + (3, cfg.h, LANES)), -3, -2).reshape(lead + (3 * cfg.dn,))


def _unperm_cols_qkv(cfg, w):
    lead = w.shape[:-1]
    return jnp.moveaxis(w.reshape(lead + (cfg.h, 3, LANES)), -3, -2).reshape(lead + (3 * cfg.dn,))
```

```python
import functools
import math

import jax
import jax.numpy as jnp
from jax import lax
from jax.experimental import pallas as pl
from jax.experimental.pallas import tpu as pltpu

F32 = jnp.float32
BF16 = jnp.bfloat16
EPS = 1e-6
LN_EPS = 1e-5
LANES = 128
CHUNK = 64
SUBLANES = 8
HALO4 = 8
HALO31 = 32
NCHIP = 4
NDEV = 8
VMEM_LIMIT = 56 * 2 ** 20
ADAM_LR, ADAM_B1, ADAM_B2, ADAM_EPS, ADAM_WD, ADAM_STEP = 0.001, 0.9, 0.999, 1e-08, 0.01, 10
MESH = pl.DeviceIdType.MESH
ANY = pl.BlockSpec(memory_space=pl.ANY)


def _cp(sem=None, vmem=VMEM_LIMIT):
    return pltpu.CompilerParams(dimension_semantics=sem, vmem_limit_bytes=vmem)


def _tile(n, pref, mult):
    t = min(n, pref) // mult * mult
    while t > 0 and n % t:
        t -= mult
    return t if t > 0 else n


def _split(a):
    hi = a.astype(BF16)
    return hi, (a - hi.astype(F32)).astype(BF16)


def _raw_dot(a, b, ca, cb, hi):
    dn = (((ca,), (cb,)), ((), ()))
    if hi:
        ah, al = _split(a.astype(F32))
        bh, bl = _split(b.astype(F32))
        d3 = lambda x, y: lax.dot_general(x, y, dn, preferred_element_type=F32)
        return d3(ah, bh) + (d3(al, bh) + d3(ah, bl))
    return lax.dot_general(a.astype(BF16), b.astype(BF16), dn, preferred_element_type=F32)


@functools.partial(jax.custom_vjp, nondiff_argnums=(2, 3, 4))
def bdot(a, b, ca, cb, hi):
    return _raw_dot(a, b, ca, cb, hi)


def _bdot_fwd(a, b, ca, cb, hi):
    return _raw_dot(a, b, ca, cb, hi), (a, b)


def _bdot_bwd(ca, cb, hi, res, ct):
    a, b = res
    fa, fb = 1 - ca, 1 - cb
    da = _raw_dot(ct, b, 1, fb, hi) if ca == 1 else _raw_dot(b, ct, fb, 1, hi)
    db = _raw_dot(a, ct, fa, 0, hi) if cb == 0 else _raw_dot(ct, a, 0, fa, hi)
    return da.astype(a.dtype), db.astype(b.dtype)


bdot.defvjp(_bdot_fwd, _bdot_bwd)


def _sigmoid(x):
    return 1.0 / (1.0 + jnp.exp(-x))


def _silu(x):
    return x * _sigmoid(x)


def _gelu(x):
    return 0.5 * x * (1.0 + lax.erf(x * (2.0 ** -0.5)))


def _softplus(x):
    return jnp.maximum(x, 0.0) + jnp.log(1.0 + jnp.exp(-jnp.abs(x)))


def _modnorm(x, g, scale, shift):
    y = x * lax.rsqrt(jnp.mean(x * x, axis=-1, keepdims=True) + EPS)
    return (y * g) * (1.0 + scale) + shift


def _rmsnorm(x, g):
    return x * lax.rsqrt(jnp.mean(x * x, axis=-1, keepdims=True) + EPS) * g


def _layernorm(x, g, b):
    mu = jnp.mean(x, axis=-1, keepdims=True)
    xc = x - mu
    var = jnp.mean(xc * xc, axis=-1, keepdims=True)
    return xc * lax.rsqrt(var + LN_EPS) * g + b


def _l2norm(t):
    return t * lax.rsqrt(jnp.sum(t * t, axis=-1, keepdims=True) + EPS)


def _adamw_math(w, g, m, v):
    mn = ADAM_B1 * m + (1.0 - ADAM_B1) * g
    vn = ADAM_B2 * v + (1.0 - ADAM_B2) * (g * g)
    mh = mn / (1.0 - ADAM_B1 ** ADAM_STEP)
    vh = vn / (1.0 - ADAM_B2 ** ADAM_STEP)
    delta = -ADAM_LR * (mh / (jnp.sqrt(vh) + ADAM_EPS) + ADAM_WD * w)
    return delta, mn, vn


def _each(f, *lists):
    return [f(*xs) for xs in zip(*lists)]


def _wy(q, k, v, bcol, acol, alog, dtb, tinv=None):
    c = CHUNK
    r = lax.broadcasted_iota(jnp.int32, (c, c), 0)
    cc = lax.broadcasted_iota(jnp.int32, (c, c), 1)
    rr = lax.broadcasted_iota(jnp.int32, (c, 1), 0)
    tri_incl, tri_strict, eye = r >= cc, r > cc, r == cc
    beta = _each(_sigmoid, bcol)
    g = _each(lambda al, a_, dt: -jnp.exp(al) * _softplus(a_ + dt), alog, acol, dtb)
    gb = [jnp.broadcast_to(g_, (c, c)) for g_ in g]
    g_row = [jnp.sum(jnp.where(eye, b_, 0.0), axis=0, keepdims=True) for b_ in gb]
    gc_col = [jnp.sum(jnp.where(tri_incl, jnp.broadcast_to(gr, (c, c)), 0.0), axis=1, keepdims=True) for gr in g_row]
    gc_row = [jnp.sum(jnp.where(r <= cc, b_, 0.0), axis=0, keepdims=True) for b_ in gb]
    decay = _each(lambda gcc, gcr: jnp.where(tri_incl, jnp.exp(jnp.where(tri_incl, gcc - gcr, 0.0)), 0.0),
                  gc_col, gc_row)
    qs = [q_ * (q_.shape[-1] ** -0.5) for q_ in q]
    kb = _each(lambda k_, b_: k_ * b_, k, beta)
    a = _each(lambda kb_, k_, d_: jnp.where(tri_strict, bdot(kb_, k_, 1, 1, False) * d_, 0.0), kb, k, decay)
    dv = v[0].shape[-1]
    x = _each(lambda v_, b_, kb_, gcc: jnp.concatenate([v_ * b_, kb_ * jnp.exp(gcc)], axis=1), v, beta, kb, gc_col)
    if tinv is None:
        inv = [jnp.where(eye, 1.0, 0.0) - a_ for a_ in a]
        p = a
        for _ in range(5):
            p = _each(lambda p_: bdot(p_, p_, 1, 0, True), p)
            inv = _each(lambda t_, p_: t_ + bdot(t_, p_, 1, 0, True), inv, p)
        x = _each(lambda t_, x_: bdot(t_, x_, 1, 0, True), inv, x)
    else:
        x = _each(_solve_given_inverse, a, x, tinv)
    xv = [x_[:, :dv] for x_ in x]
    xk = [x_[:, dv:] for x_ in x]
    qk = _each(lambda q_, k_, d_: bdot(q_, k_, 1, 1, False) * d_, qs, k, decay)
    g_last = [jnp.sum(jnp.where(rr == c - 1, gcc, 0.0), axis=0, keepdims=True) for gcc in gc_col]
    qg = _each(lambda q_, gcc: q_ * jnp.exp(gcc), qs, gc_col)
    kd = _each(lambda k_, gl, gcc: k_ * jnp.exp(gl - gcc), k, g_last, gc_col)
    outs = (xv, xk, qg, kd, qk, [jnp.exp(gl) for gl in g_last])
    return outs + (inv,) if tinv is None else outs


@jax.custom_vjp
def _solve_given_inverse(a, rhs, tinv):
    return _raw_dot(tinv, rhs, 1, 0, True)


def _solve_given_inverse_fwd(a, rhs, tinv):
    x = _raw_dot(tinv, rhs, 1, 0, True)
    return x, (x, tinv)


def _solve_given_inverse_bwd(res, dx):
    x, tinv = res
    drhs = _raw_dot(tinv, dx, 0, 0, True)
    return -_raw_dot(drhs, x, 1, 1, True), drhs, jnp.zeros_like(tinv)


_solve_given_inverse.defvjp(_solve_given_inverse_fwd, _solve_given_inverse_bwd)


def _seq(u, w, qg, kd, qk, e, z, s, ng):
    v_new = _each(lambda u_, w_, s_: u_ - bdot(w_, s_, 1, 0, False), u, w, s)
    o1 = _each(lambda q_, s_: bdot(q_, s_, 1, 0, False), qg, s)
    o2 = _each(lambda qk_, vn: bdot(qk_, vn, 1, 0, False), qk, v_new)
    ds = _each(lambda kd_, vn: bdot(kd_, vn, 0, 0, False), kd, v_new)
    s_next = _each(lambda s_, e_, d_: s_ * e_ + d_, s, e, ds)
    y = _each(lambda a_, b_, z_: _rmsnorm(a_ + b_, ng) * _silu(z_), o1, o2, z)
    return y, s_next


def _sg_block(u, v, gt, lg, lb, w, bias):
    n = w.shape[0]
    pr = lax.broadcasted_iota(jnp.int32, (n, n), 0) // CHUNK
    pc = lax.broadcasted_iota(jnp.int32, (n, n), 1) // CHUNK
    wm = jnp.where(pr >= pc, w, 0.0)
    vl = _layernorm(_gelu(v), lg, lb)
    mixed = bdot(wm, vl, 1, 0, False) + bias
    return _gelu(u) * mixed * _silu(gt)


def _glu(a, b):
    return a * _sigmoid(b)


def _cv_post(conv, gate, cb, lg, lb):
    return _silu(_layernorm(conv + cb, lg, lb)) * _silu(gate)


def _qk_post(conv):
    return _l2norm(_silu(conv))


def _v_post(conv):
    return _silu(conv)


def in_proj(x, shift, scale, ng, wp):
    t, d = x.shape
    npc = wp.shape[1]
    tm, tn = _tile(t, 512, LANES), _tile(npc, 2432, LANES)

    def body(x_ref, sh_ref, sc_ref, g_ref, w_ref, p_ref, ht_ref, h_scr):
        @pl.when(pl.program_id(1) == 0)
        def _():
            h = _modnorm(x_ref[...], g_ref[...], sc_ref[...], sh_ref[...])
            h_scr[...] = h.astype(BF16)
            ht_ref[...] = h.T.astype(BF16)
        p_ref[...] = jnp.dot(h_scr[...], w_ref[...], preferred_element_type=F32)

    vec = pl.BlockSpec((1, d), lambda i, j: (0, 0))
    return pl.pallas_call(
        body, name="in_proj", grid=(t // tm, npc // tn),
        in_specs=[pl.BlockSpec((tm, d), lambda i, j: (i, 0)), vec, vec, vec,
                  pl.BlockSpec((d, tn), lambda i, j: (0, j))],
        out_specs=[pl.BlockSpec((tm, tn), lambda i, j: (i, j)), pl.BlockSpec((d, tm), lambda i, j: (0, i))],
        out_shape=[jax.ShapeDtypeStruct((t, npc), F32), jax.ShapeDtypeStruct((d, t), BF16)],
        scratch_shapes=[pltpu.VMEM((tm, d), BF16)],
        compiler_params=_cp(("parallel", "arbitrary")),
    )(x, shift, scale, ng, wp)


def _roll_bank(x, bank_ref, offsets):
    rows = x.shape[0]
    residues = sorted({o % SUBLANES for o in offsets})
    for slot, b in enumerate(residues):
        bank_ref[slot] = x if b == 0 else pltpu.roll(x, rows - b, 0)
    return {o: (residues.index(o % SUBLANES), o - o % SUBLANES) for o in offsets}


def _n_residues(offsets):
    return len({o % SUBLANES for o in offsets})


def conv_fwd(name, k, halo, pre_fn, pre, w, post_fns, extras, params, c_total, tc, tm_pref=512):
    t = pre[0][0].shape[0]
    tm = _tile(t, tm_pref, halo)
    npre, nex, npar = len(pre), len(extras), len(params)
    ngr = tc // LANES
    taps = [halo - (k - 1) + j for j in range(k)]

    def body(*refs):
        prev = refs[:npre]
        cur = refs[npre:2 * npre]
        w_ref = refs[2 * npre]
        ex = refs[2 * npre + 1:2 * npre + 1 + nex]
        par = refs[2 * npre + 1 + nex:2 * npre + 1 + nex + npar]
        out_ref, buf, bank = refs[-3], refs[-2], refs[-1]
        i = pl.program_id(1)
        pv = pre_fn(*[r[...] for r in prev])
        buf[0:halo, :] = jnp.where(i > 0, pv, 0.0)
        buf[halo:, :] = pre_fn(*[r[...] for r in cur])
        where = _roll_bank(buf[...], bank, taps)
        acc = None
        for j, o in enumerate(taps):
            slot, st = where[o]
            term = w_ref[j:j + 1, :] * bank[slot, st:st + tm, :]
            acc = term if acc is None else acc + term
        for gi in range(ngr):
            sl = slice(gi * LANES, (gi + 1) * LANES)
            out_ref[:, sl] = post_fns[gi](acc[:, sl], *[e[:, sl] for e in ex], *[p_[:, sl] for p_ in par])

    hb = tm // halo
    in_specs = ([pl.BlockSpec((halo, tc), functools.partial(lambda j, i, o: (jnp.maximum(i * hb - 1, 0), o + j), o=col // tc))
                 for _, col in pre]
                + [pl.BlockSpec((tm, tc), functools.partial(lambda j, i, o: (i, o + j), o=col // tc)) for _, col in pre]
                + [pl.BlockSpec((k, tc), lambda j, i: (0, j))]
                + [pl.BlockSpec((tm, tc), functools.partial(lambda j, i, o: (i, o + j), o=col // tc)) for _, col in extras]
                + [pl.BlockSpec((1, tc), lambda j, i: (0, j)) for _ in params])
    args = [a for a, _ in pre] * 2 + [w] + [a for a, _ in extras] + list(params)
    return pl.pallas_call(
        body, name=name, grid=(c_total // tc, t // tm), in_specs=in_specs,
        out_specs=pl.BlockSpec((tm, tc), lambda j, i: (i, j)),
        out_shape=jax.ShapeDtypeStruct((t, c_total), F32),
        scratch_shapes=[pltpu.VMEM((halo + tm, tc), F32), pltpu.VMEM((_n_residues(taps), halo + tm, tc), F32)],
        compiler_params=_cp(("parallel", "arbitrary")),
    )(*args)


def conv_bwd(name, k, halo, pre_fn, pre, w, post_fns, extras, params, dout, c_total, tc, tm_pref=512):
    t = pre[0][0].shape[0]
    tm = _tile(t, tm_pref, halo)
    npre, nex, npar = len(pre), len(extras), len(params)
    ngr = tc // LANES
    nout = npre + nex
    assert nout == 1 or c_total == tc
    nblk = t // tm
    ext = tm + halo
    taps = [halo - (k - 1) + j for j in range(k)]
    back = [k - 1 - j for j in range(k)]

    def body(*refs):
        it = iter(refs)
        prev = [next(it) for _ in range(npre)]
        cur = [next(it) for _ in range(npre)]
        nxt = [next(it) for _ in range(npre)]
        w_ref = next(it)
        ex_c = [next(it) for _ in range(nex)]
        ex_n = [next(it) for _ in range(nex)]
        par = [next(it) for _ in range(npar)]
        do_c, do_n = next(it), next(it)
        din_ref, dw_ref = next(it), next(it)
        dpar = [next(it) for _ in range(npar)]
        buf, dbuf, bank, dbank = next(it), next(it), next(it), next(it)
        i = pl.program_id(1)

        @pl.when(i == 0)
        def _():
            dw_ref[...] = jnp.zeros_like(dw_ref)
            for r in dpar:
                r[...] = jnp.zeros_like(r)

        buf[0:halo, :] = jnp.where(i > 0, pre_fn(*[r[...] for r in prev]), 0.0)
        cur_vals = [r[...] for r in cur]
        buf[halo:halo + tm, :] = pre_fn(*cur_vals)
        buf[halo + tm:, :] = pre_fn(*[r[...] for r in nxt])
        where = _roll_bank(buf[...], bank, taps)
        conv = None
        for j, o in enumerate(taps):
            slot, st = where[o]
            term = w_ref[j:j + 1, :] * bank[slot, st:st + ext, :]
            conv = term if conv is None else conv + term
        don = jnp.where(i < nblk - 1, do_n[...], 0.0)
        for gi in range(ngr):
            sl = slice(gi * LANES, (gi + 1) * LANES)
            pv = [p_[:, sl] for p_ in par]
            _, vj = jax.vjp(post_fns[gi], conv[:tm, sl], *[e[:, sl] for e in ex_c], *pv)
            gr = vj(do_c[:, sl])
            dbuf[0:tm, sl] = gr[0]
            for e in range(nex):
                din_ref[:, (npre + e) * tc + gi * LANES:(npre + e) * tc + (gi + 1) * LANES] = gr[1 + e].astype(din_ref.dtype)
            for q_ in range(npar):
                dpar[q_][:, sl] += gr[1 + nex + q_]
            _, vjn = jax.vjp(post_fns[gi], conv[tm:, sl], *[e[:, sl] for e in ex_n], *pv)
            dbuf[tm:, sl] = vjn(don[:, sl])[0]
        dcur = dbuf[0:tm, :]
        dwhere = _roll_bank(dbuf[...], dbank, back)
        dpre = None
        for j in range(k):
            slot, st = dwhere[back[j]]
            term = w_ref[j:j + 1, :] * dbank[slot, st:st + tm, :]
            dpre = term if dpre is None else dpre + term
            slot, st = where[taps[j]]
            dw_ref[j:j + 1, :] += jnp.sum(dcur * bank[slot, st:st + tm, :], axis=0, keepdims=True)
        _, vjp_pre = jax.vjp(pre_fn, *cur_vals)
        for e, gval in enumerate(vjp_pre(dpre)):
            din_ref[:, e * tc:(e + 1) * tc] = gval.astype(din_ref.dtype)

    hb = tm // halo
    last_h = t // halo - 1

    def spec(kind, col):
        o = col // tc
        if kind == "prev":
            return pl.BlockSpec((halo, tc), lambda j, i: (jnp.maximum(i * hb - 1, 0), o + j))
        if kind == "next":
            return pl.BlockSpec((halo, tc), lambda j, i: (jnp.minimum((i + 1) * hb, last_h), o + j))
        return pl.BlockSpec((tm, tc), lambda j, i: (i, o + j))

    in_specs = ([spec("prev", col) for _, col in pre] + [spec("cur", col) for _, col in pre]
                + [spec("next", col) for _, col in pre] + [pl.BlockSpec((k, tc), lambda j, i: (0, j))]
                + [spec("cur", col) for _, col in extras] + [spec("next", col) for _, col in extras]
                + [pl.BlockSpec((1, tc), lambda j, i: (0, j)) for _ in params]
                + [spec("cur", 0), spec("next", 0)])
    args = [a for a, _ in pre] * 3 + [w] + [a for a, _ in extras] * 2 + list(params) + [dout, dout]
    out = pl.pallas_call(
        body, name=name, grid=(c_total // tc, nblk), in_specs=in_specs,
        out_specs=[pl.BlockSpec((tm, nout * tc), lambda j, i: (i, j)), pl.BlockSpec((k, tc), lambda j, i: (0, j))]
        + [pl.BlockSpec((1, tc), lambda j, i: (0, j)) for _ in params],
        out_shape=[jax.ShapeDtypeStruct((t, nout * c_total), BF16), jax.ShapeDtypeStruct((k, c_total), F32)]
        + [jax.ShapeDtypeStruct((1, c_total), F32) for _ in params],
        scratch_shapes=[pltpu.VMEM((2 * halo + tm, tc), F32), pltpu.VMEM((ext, tc), F32),
                        pltpu.VMEM((_n_residues(taps), 2 * halo + tm, tc), F32),
                        pltpu.VMEM((_n_residues(back), ext, tc), F32)],
        compiler_params=_cp(("parallel", "arbitrary")),
    )(*args)
    return out[0], out[1], out[2:]


def _head_pick(ref_val, row):
    rr = lax.broadcasted_iota(jnp.int32, ref_val.shape, 0)
    v = jnp.sum(jnp.where(rr == row, ref_val, 0.0), axis=0, keepdims=True)
    ll = lax.broadcasted_iota(jnp.int32, v.shape, 1)
    return jnp.sum(jnp.where(ll == 0, v, 0.0), axis=1, keepdims=True)


def _lane_col(blk, lane_idx):
    ll = lax.broadcasted_iota(jnp.int32, blk.shape, 1)
    return jnp.sum(jnp.where(ll == lane_idx, blk, 0.0), axis=1, keepdims=True)


WY_HEADS = 2
WY_UNROLL = 4
WY_UNROLL_BWD = 4


def dn_wy_fwd(qkv, p, ba_col, alog_b, dtb_b, nheads):
    t = qkv.shape[0]
    tm = _tile(t, 512, CHUNK * WY_UNROLL)
    nc = tm // CHUNK
    hb = WY_HEADS
    hp = alog_b.shape[0]
    w_ = hb * LANES

    def body(qkv_ref, ba_ref, al_ref, dt_ref, u_ref, w_ref, qg_ref, kd_ref, qk_ref, e_ref, ti_ref):
        hblk = pl.program_id(1)
        alv, dtv = al_ref[...], dt_ref[...]

        def trip(cj, carry):
            units = [(cj * WY_UNROLL + cu, hl) for cu in range(WY_UNROLL) for hl in range(hb)]
            args = [[] for _ in range(7)]
            for ci, hl in units:
                rows = pl.ds(pl.multiple_of(ci * CHUNK, CHUNK), CHUNK)
                ba = ba_ref[rows, :]
                h = hblk * hb + hl
                for lst, val in zip(args, (qkv_ref[rows, hl * 384:hl * 384 + 128],
                                           qkv_ref[rows, hl * 384 + 128:hl * 384 + 256],
                                           qkv_ref[rows, hl * 384 + 256:hl * 384 + 384],
                                           _lane_col(ba, h), _lane_col(ba, nheads + h),
                                           _head_pick(alv, h), _head_pick(dtv, h))):
                    lst.append(val)
            outs = _wy(*args)
            for n, (ci, hl) in enumerate(units):
                rows = pl.ds(pl.multiple_of(ci * CHUNK, CHUNK), CHUNK)
                u, w, qg, kd, qk, e, ti = [o[n] for o in outs]
                sl = slice(hl * LANES, (hl + 1) * LANES)
                u_ref[rows, sl] = u
                w_ref[rows, sl] = w.astype(BF16)
                qg_ref[rows, sl] = qg.astype(BF16)
                kd_ref[rows, sl] = kd.astype(BF16)
                qk_ref[rows, hl * LANES:hl * LANES + CHUNK] = qk.astype(BF16)
                qk_ref[rows, hl * LANES + CHUNK:(hl + 1) * LANES] = jnp.zeros((CHUNK, LANES - CHUNK), BF16)
                e_ref[ci, :, sl] = jnp.broadcast_to(e, (1, LANES))
                ti_ref[rows, hl * LANES:hl * LANES + CHUNK] = ti
                ti_ref[rows, hl * LANES + CHUNK:(hl + 1) * LANES] = jnp.zeros((CHUNK, LANES - CHUNK), F32)
            return carry

        lax.fori_loop(0, nc // WY_UNROLL, trip, 0)

    bc = ba_col // LANES
    blk = pl.BlockSpec((tm, w_), lambda i, h: (i, h))
    tab = pl.BlockSpec((hp, LANES), lambda i, h: (0, 0))
    wide = lambda dt: jax.ShapeDtypeStruct((t, nheads * LANES), dt)
    return pl.pallas_call(
        body, name="dn_wy_fwd", grid=(t // tm, nheads // hb),
        in_specs=[pl.BlockSpec((tm, hb * 384), lambda i, h: (i, h)), pl.BlockSpec((tm, LANES), lambda i, h: (i, bc)),
                  tab, tab],
        out_specs=[blk] * 5 + [pl.BlockSpec((nc, 1, w_), lambda i, h: (i, 0, h)), blk],
        out_shape=[wide(F32), wide(BF16), wide(BF16), wide(BF16), wide(BF16),
                   jax.ShapeDtypeStruct((t // CHUNK, 1, nheads * LANES), F32), wide(F32)],
        compiler_params=_cp(("parallel", "parallel")),
    )(qkv, p, alog_b, dtb_b)


def dn_seq_fwd(u, w, qg, kd, qk, e, p, z_col, ng, nheads, hb):
    t = u.shape[0]
    tm = _tile(t, 512, CHUNK)
    nc = tm // CHUNK
    w_ = hb * LANES

    def body(u_ref, w_ref, qg_ref, kd_ref, qk_ref, e_ref, z_ref, ng_ref, y_ref, ss_ref, s_scr):
        i, hblk = pl.program_id(0), pl.program_id(1)
        for hl in range(hb):
            @pl.when(i == 0)
            def _():
                s_scr[hblk * hb + hl] = jnp.zeros((LANES, LANES), F32)
        ngv = ng_ref[...]

        def chunk(ci, carry):
            rows = pl.ds(pl.multiple_of(ci * CHUNK, CHUNK), CHUNK)
            ev = e_ref[ci]
            sls = [slice(hl * LANES, (hl + 1) * LANES) for hl in range(hb)]
            s = [s_scr[hblk * hb + hl] for hl in range(hb)]
            for hl in range(hb):
                ss_ref[ci, sls[hl], :] = s[hl]
            y, sn = _seq([u_ref[rows, sl] for sl in sls], [w_ref[rows, sl].astype(F32) for sl in sls],
                         [qg_ref[rows, sl].astype(F32) for sl in sls], [kd_ref[rows, sl].astype(F32) for sl in sls],
                         [qk_ref[rows, sl][:, :CHUNK].astype(F32) for sl in sls], [ev[:, sl] for sl in sls],
                         [z_ref[rows, sl] for sl in sls], s, ngv)
            for hl in range(hb):
                y_ref[rows, sls[hl]] = y[hl]
                s_scr[hblk * hb + hl] = sn[hl]
            return carry

        lax.fori_loop(0, nc, chunk, 0)

    zc = z_col // w_
    blk = pl.BlockSpec((tm, w_), lambda i, h: (i, h))
    return pl.pallas_call(
        body, name="dn_seq_fwd", grid=(t // tm, nheads // hb),
        in_specs=[blk] * 5 + [pl.BlockSpec((nc, 1, w_), lambda i, h: (i, 0, h)),
                              pl.BlockSpec((tm, w_), lambda i, h: (i, zc + h)),
                              pl.BlockSpec((1, LANES), lambda i, h: (0, 0))],
        out_specs=[blk, pl.BlockSpec((nc, w_, LANES), lambda i, h: (i, h, 0))],
        out_shape=[jax.ShapeDtypeStruct((t, nheads * LANES), F32),
                   jax.ShapeDtypeStruct((t // CHUNK, nheads * LANES, LANES), F32)],
        scratch_shapes=[pltpu.VMEM((nheads, LANES, LANES), F32)],
        compiler_params=_cp(("arbitrary", "arbitrary")),
    )(u, w, qg, kd, qk, e, p, ng)


def dn_seq_bwd(u, w, qg, kd, qk, e, p, z_col, ng, ss, dy, nheads, hb):
    t = u.shape[0]
    tm = _tile(t, 2048 // hb, CHUNK)
    nc = tm // CHUNK
    nblk = t // tm
    w_ = hb * LANES

    def body(u_ref, w_ref, qg_ref, kd_ref, qk_ref, e_ref, z_ref, ng_ref, ss_ref, dy_ref,
             du_ref, dw_ref, dqg_ref, dkd_ref, dqk_ref, de_ref, dz_ref, dng_ref, ds_scr):
        i, hblk = pl.program_id(0), pl.program_id(1)

        @pl.when((i == 0) & (hblk == 0))
        def _():
            dng_ref[...] = jnp.zeros_like(dng_ref)

        for hl in range(hb):
            @pl.when(i == 0)
            def _():
                ds_scr[hblk * hb + hl] = jnp.zeros((LANES, LANES), F32)
        ngv = ng_ref[...]

        def chunk(cj, carry):
            ci = nc - 1 - cj
            rows = pl.ds(pl.multiple_of(ci * CHUNK, CHUNK), CHUNK)
            ev = e_ref[ci]
            sls = [slice(hl * LANES, (hl + 1) * LANES) for hl in range(hb)]
            _, vj = jax.vjp(_seq, [u_ref[rows, sl] for sl in sls], [w_ref[rows, sl].astype(F32) for sl in sls],
                            [qg_ref[rows, sl].astype(F32) for sl in sls], [kd_ref[rows, sl].astype(F32) for sl in sls],
                            [qk_ref[rows, sl][:, :CHUNK].astype(F32) for sl in sls], [ev[:, sl] for sl in sls],
                            [z_ref[rows, sl] for sl in sls], [ss_ref[ci, sl, :] for sl in sls], ngv)
            du, dw, dqg, dkd, dqk, de, dz, dsp, dng = vj(([dy_ref[rows, sl] for sl in sls],
                                                          [ds_scr[hblk * hb + hl] for hl in range(hb)]))
            for hl, sl in enumerate(sls):
                du_ref[rows, sl] = du[hl]
                dw_ref[rows, sl] = dw[hl]
                dqg_ref[rows, sl] = dqg[hl]
                dkd_ref[rows, sl] = dkd[hl]
                dqk_ref[rows, hl * LANES:hl * LANES + CHUNK] = dqk[hl]
                dqk_ref[rows, hl * LANES + CHUNK:(hl + 1) * LANES] = jnp.zeros((CHUNK, LANES - CHUNK), F32)
                de_ref[ci, :, sl] = de[hl]
                dz_ref[rows, sl] = dz[hl]
                ds_scr[hblk * hb + hl] = dsp[hl]
            dng_ref[...] += dng
            return carry

        lax.fori_loop(0, nc, chunk, 0)

    zc = z_col // w_
    rv = lambda i: nblk - 1 - i
    blk = pl.BlockSpec((tm, w_), lambda i, h: (rv(i), h))
    eblk = pl.BlockSpec((nc, 1, w_), lambda i, h: (rv(i), 0, h))
    one = pl.BlockSpec((1, LANES), lambda i, h: (0, 0))
    wide = jax.ShapeDtypeStruct((t, nheads * LANES), F32)
    return pl.pallas_call(
        body, name="dn_seq_bwd", grid=(nblk, nheads // hb),
        in_specs=[blk] * 5 + [eblk, pl.BlockSpec((tm, w_), lambda i, h: (rv(i), zc + h)), one,
                              pl.BlockSpec((nc, w_, LANES), lambda i, h: (rv(i), h, 0)), blk],
        out_specs=[blk] * 5 + [eblk, blk, one],
        out_shape=[wide] * 5 + [jax.ShapeDtypeStruct((t // CHUNK, 1, nheads * LANES), F32), wide,
                                jax.ShapeDtypeStruct((1, LANES), F32)],
        scratch_shapes=[pltpu.VMEM((nheads, LANES, LANES), F32)],
        compiler_params=_cp(("arbitrary", "arbitrary")),
    )(u, w, qg, kd, qk, e, p, ng, ss, dy)


def dn_wy_bwd(qkv, p, ba_col, alog_b, dtb_b, ti, du, dw, dqg, dkd, dqk, de, nheads):
    t = qkv.shape[0]
    tm = _tile(t, 512, CHUNK * WY_UNROLL_BWD)
    nc = tm // CHUNK
    hb = WY_HEADS
    hp = alog_b.shape[0]
    w_ = hb * LANES

    def body(qkv_ref, ba_ref, al_ref, dt_ref, du_ref, dw_ref, dqg_ref, dkd_ref, dqk_ref, de_ref, ti_ref,
             dqkv_ref, dba_ref, dal_ref, ddt_ref):
        i, hblk = pl.program_id(0), pl.program_id(1)

        @pl.when((i == 0) & (hblk == 0))
        def _():
            dal_ref[...] = jnp.zeros_like(dal_ref)
            ddt_ref[...] = jnp.zeros_like(ddt_ref)

        @pl.when(hblk == 0)
        def _():
            dba_ref[...] = jnp.zeros_like(dba_ref)

        alv, dtv = al_ref[...], dt_ref[...]
        lane = lax.broadcasted_iota(jnp.int32, (CHUNK, LANES), 1)
        rowp = lax.broadcasted_iota(jnp.int32, (hp, LANES), 0)

        def trip(cj, carry):
            units = [(cj * WY_UNROLL_BWD + cu, hl) for cu in range(WY_UNROLL_BWD) for hl in range(hb)]
            args = [[] for _ in range(8)]
            cts = [[] for _ in range(6)]
            for ci, hl in units:
                rows = pl.ds(pl.multiple_of(ci * CHUNK, CHUNK), CHUNK)
                ba = ba_ref[rows, :]
                h = hblk * hb + hl
                sl = slice(hl * LANES, (hl + 1) * LANES)
                for lst, val in zip(args, (qkv_ref[rows, hl * 384:hl * 384 + 128],
                                           qkv_ref[rows, hl * 384 + 128:hl * 384 + 256],
                                           qkv_ref[rows, hl * 384 + 256:hl * 384 + 384],
                                           _lane_col(ba, h), _lane_col(ba, nheads + h),
                                           _head_pick(alv, h), _head_pick(dtv, h), ti_ref[rows, sl][:, :CHUNK])):
                    lst.append(val)
                de11 = jnp.sum(de_ref[ci][:, sl], axis=1, keepdims=True)
                for lst, val in zip(cts, (du_ref[rows, sl], dw_ref[rows, sl], dqg_ref[rows, sl], dkd_ref[rows, sl],
                                          dqk_ref[rows, sl][:, :CHUNK], de11)):
                    lst.append(val)
            _, vj = jax.vjp(_wy, *args)
            grads = vj(tuple(cts))
            for n, (ci, hl) in enumerate(units):
                rows = pl.ds(pl.multiple_of(ci * CHUNK, CHUNK), CHUNK)
                h = hblk * hb + hl
                dq, dk, dv, dbc, dac, dal, ddt = [g_[n] for g_ in grads[:7]]
                dqkv_ref[rows, hl * 384:hl * 384 + 128] = dq
                dqkv_ref[rows, hl * 384 + 128:hl * 384 + 256] = dk
                dqkv_ref[rows, hl * 384 + 256:hl * 384 + 384] = dv
                dba_ref[rows, :] += jnp.where(lane == h, dbc, 0.0) + jnp.where(lane == nheads + h, dac, 0.0)
                dal_ref[...] += jnp.where(rowp == h, dal, 0.0)
                ddt_ref[...] += jnp.where(rowp == h, ddt, 0.0)
            return carry

        lax.fori_loop(0, nc // WY_UNROLL_BWD, trip, 0)

    bc = ba_col // LANES
    blk = pl.BlockSpec((tm, w_), lambda i, h: (i, h))
    tab = pl.BlockSpec((hp, LANES), lambda i, h: (0, 0))
    return pl.pallas_call(
        body, name="dn_wy_bwd", grid=(t // tm, nheads // hb),
        in_specs=[pl.BlockSpec((tm, hb * 384), lambda i, h: (i, h)), pl.BlockSpec((tm, LANES), lambda i, h: (i, bc)),
                  tab, tab] + [blk] * 5 + [pl.BlockSpec((nc, 1, w_), lambda i, h: (i, 0, h)), blk],
        out_specs=[pl.BlockSpec((tm, hb * 384), lambda i, h: (i, h)), pl.BlockSpec((tm, LANES), lambda i, h: (i, 0)),
                   tab, tab],
        out_shape=[jax.ShapeDtypeStruct((t, nheads * 384), F32), jax.ShapeDtypeStruct((t, LANES), F32),
                   jax.ShapeDtypeStruct((hp, LANES), F32), jax.ShapeDtypeStruct((hp, LANES), F32)],
        compiler_params=_cp(("arbitrary", "arbitrary")),
    )(qkv, p, alog_b, dtb_b, du, dw, dqg, dkd, dqk, de, ti)


def sg_fwd(p, col, sg, lg, lb, w, bias_b):
    t = p.shape[0]
    ng_ = sg // LANES
    tm = _tile(t, 256, LANES)
    cb = col // sg

    def body(u_ref, v_ref, g_ref, lg_ref, lb_ref, w_ref, b_ref, y_ref):
        for n in range(tm // LANES):
            rs = slice(n * LANES, (n + 1) * LANES)
            for gi in range(ng_):
                sl = slice(gi * LANES, (gi + 1) * LANES)
                y_ref[rs, sl] = _sg_block(u_ref[rs, sl], v_ref[rs, sl], g_ref[rs, sl], lg_ref[:, sl], lb_ref[:, sl],
                                          w_ref[gi], b_ref[gi])

    vec = pl.BlockSpec((1, sg), lambda i: (0, 0))
    full = pl.BlockSpec((ng_, LANES, LANES), lambda i: (0, 0, 0))
    return pl.pallas_call(
        body, name="sg_fwd", grid=(t // tm,),
        in_specs=[pl.BlockSpec((tm, sg), lambda i: (i, cb)), pl.BlockSpec((tm, sg), lambda i: (i, cb + 1)),
                  pl.BlockSpec((tm, sg), lambda i: (i, cb + 2)), vec, vec, full, full],
        out_specs=pl.BlockSpec((tm, sg), lambda i: (i, 0)),
        out_shape=jax.ShapeDtypeStruct((t, sg), F32),
        compiler_params=_cp(("parallel",)),
    )(p, p, p, lg, lb, w, bias_b)


def sg_bwd(p, col, sg, lg, lb, w, bias_b, dy):
    t = p.shape[0]
    ng_ = sg // LANES
    tm = _tile(t, 256, LANES)
    cb = col // sg

    def body(u_ref, v_ref, g_ref, lg_ref, lb_ref, w_ref, b_ref, dy_ref, d_ref, dlg_ref, dlb_ref, dw_ref, db_ref):
        @pl.when(pl.program_id(0) == 0)
        def _():
            for r in (dlg_ref, dlb_ref, dw_ref, db_ref):
                r[...] = jnp.zeros_like(r)

        for n in range(tm // LANES):
            rs = slice(n * LANES, (n + 1) * LANES)
            for gi in range(ng_):
                sl = slice(gi * LANES, (gi + 1) * LANES)
                _, vj = jax.vjp(_sg_block, u_ref[rs, sl], v_ref[rs, sl], g_ref[rs, sl], lg_ref[:, sl], lb_ref[:, sl],
                                w_ref[gi], b_ref[gi])
                du, dv, dg, dlg, dlb, dw, db = vj(dy_ref[rs, sl])
                d_ref[rs, gi * LANES:(gi + 1) * LANES] = du.astype(BF16)
                d_ref[rs, sg + gi * LANES:sg + (gi + 1) * LANES] = dv.astype(BF16)
                d_ref[rs, 2 * sg + gi * LANES:2 * sg + (gi + 1) * LANES] = dg.astype(BF16)
                dlg_ref[:, sl] += dlg
                dlb_ref[:, sl] += dlb
                dw_ref[gi] += dw
                db_ref[gi] += jnp.broadcast_to(jnp.sum(db, axis=1, keepdims=True), (LANES, LANES))

    vec = pl.BlockSpec((1, sg), lambda i: (0, 0))
    full = pl.BlockSpec((ng_, LANES, LANES), lambda i: (0, 0, 0))
    return pl.pallas_call(
        body, name="sg_bwd", grid=(t // tm,),
        in_specs=[pl.BlockSpec((tm, sg), lambda i: (i, cb)), pl.BlockSpec((tm, sg), lambda i: (i, cb + 1)),
                  pl.BlockSpec((tm, sg), lambda i: (i, cb + 2)), vec, vec, full, full,
                  pl.BlockSpec((tm, sg), lambda i: (i, 0))],
        out_specs=[pl.BlockSpec((tm, 3 * sg), lambda i: (i, 0)), vec, vec, full, full],
        out_shape=[jax.ShapeDtypeStruct((t, 3 * sg), BF16), jax.ShapeDtypeStruct((1, sg), F32),
                   jax.ShapeDtypeStruct((1, sg), F32), jax.ShapeDtypeStruct((ng_, LANES, LANES), F32),
                   jax.ShapeDtypeStruct((ng_, LANES, LANES), F32)],
        compiler_params=_cp(("arbitrary",)),
    )(p, p, p, lg, lb, w, bias_b, dy)


def out_proj(x, y_dn, y_sg, y_cv, wo, gate):
    t, d = x.shape
    dn, sg, cv = y_dn.shape[1], y_sg.shape[1], y_cv.shape[1]
    dmix = dn + sg + cv
    tm = _tile(t, 256, LANES)

    def body(x_ref, a_ref, b_ref, c_ref, w_ref, g_ref, xn_ref, y_ref, yt_ref):
        a, b, c = a_ref[...], b_ref[...], c_ref[...]
        y = (jnp.dot(a.astype(BF16), w_ref[0:dn, :], preferred_element_type=F32)
             + jnp.dot(b.astype(BF16), w_ref[dn:dn + sg, :], preferred_element_type=F32)
             + jnp.dot(c.astype(BF16), w_ref[dn + sg:, :], preferred_element_type=F32))
        y_ref[...] = y
        xn_ref[...] = x_ref[...] + g_ref[...] * y
        yt_ref[0:dn, :] = a.T.astype(BF16)
        yt_ref[dn:dn + sg, :] = b.T.astype(BF16)
        yt_ref[dn + sg:, :] = c.T.astype(BF16)

    row = lambda w_: pl.BlockSpec((tm, w_), lambda i: (i, 0))
    return pl.pallas_call(
        body, name="out_proj", grid=(t // tm,),
        in_specs=[row(d), row(dn), row(sg), row(cv), pl.BlockSpec((dmix, d), lambda i: (0, 0)),
                  pl.BlockSpec((1, d), lambda i: (0, 0))],
        out_specs=[row(d), row(d), pl.BlockSpec((dmix, tm), lambda i: (0, i))],
        out_shape=[jax.ShapeDtypeStruct((t, d), F32), jax.ShapeDtypeStruct((t, d), F32),
                   jax.ShapeDtypeStruct((dmix, t), BF16)],
        compiler_params=_cp(("parallel",)),
    )(x, y_dn, y_sg, y_cv, wo, gate)


def out_proj_bwd(dxn, y, gate, wo, dn, sg, cv):
    t, d = dxn.shape
    dmix = dn + sg + cv
    tm = _tile(t, 256, LANES)

    def body(dx_ref, y_ref, g_ref, w_ref, da_ref, db_ref, dc_ref, dyb_ref, dg_ref):
        @pl.when(pl.program_id(0) == 0)
        def _():
            dg_ref[...] = jnp.zeros_like(dg_ref)
        dx = dx_ref[...]
        dg_ref[...] += jnp.sum(dx * y_ref[...], axis=0, keepdims=True)
        dyb = (dx * g_ref[...]).astype(BF16)
        dyb_ref[...] = dyb
        dcat = lax.dot_general(dyb, w_ref[...], (((1,), (1,)), ((), ())), preferred_element_type=F32)
        da_ref[...] = dcat[:, 0:dn]
        db_ref[...] = dcat[:, dn:dn + sg]
        dc_ref[...] = dcat[:, dn + sg:]

    row = lambda w_: pl.BlockSpec((tm, w_), lambda i: (i, 0))
    vec = pl.BlockSpec((1, d), lambda i: (0, 0))
    return pl.pallas_call(
        body, name="out_proj_bwd", grid=(t // tm,),
        in_specs=[row(d), row(d), vec, pl.BlockSpec((dmix, d), lambda i: (0, 0))],
        out_specs=[row(dn), row(sg), row(cv), row(d), vec],
        out_shape=[jax.ShapeDtypeStruct((t, dn), F32), jax.ShapeDtypeStruct((t, sg), F32),
                   jax.ShapeDtypeStruct((t, cv), F32), jax.ShapeDtypeStruct((t, d), BF16),
                   jax.ShapeDtypeStruct((1, d), F32)],
        compiler_params=_cp(("arbitrary",)),
    )(dxn, y, gate, wo)


def matmul_acc(name, at, b):
    m, t = at.shape
    n = b.shape[1]
    tm, tn, tk = _tile(m, 1024, LANES), _tile(n, 2432, LANES), _tile(t, 1024, LANES)

    def body(a_ref, b_ref, o_ref):
        @pl.when(pl.program_id(2) == 0)
        def _():
            o_ref[...] = jnp.zeros_like(o_ref)
        o_ref[...] += jnp.dot(a_ref[...], b_ref[...], preferred_element_type=F32)

    return pl.pallas_call(
        body, name=name, grid=(m // tm, n // tn, t // tk),
        in_specs=[pl.BlockSpec((tm, tk), lambda i, j, k: (i, k)), pl.BlockSpec((tk, tn), lambda i, j, k: (k, j))],
        out_specs=pl.BlockSpec((tm, tn), lambda i, j, k: (i, j)),
        out_shape=jax.ShapeDtypeStruct((m, n), F32),
        compiler_params=_cp(("parallel", "parallel", "arbitrary")),
    )(at, b)


def in_proj_bwd(dp, wp, x, dxn, shift, scale, ng):
    t, d = x.shape
    npc = wp.shape[1]
    tm, tk = _tile(t, 512, LANES), _tile(npc, 2432, LANES)

    def mm_body(dp_ref, w_ref, dh_ref):
        @pl.when(pl.program_id(1) == 0)
        def _():
            dh_ref[...] = jnp.zeros_like(dh_ref)
        dh_ref[...] += lax.dot_general(dp_ref[...], w_ref[...], (((1,), (1,)), ((), ())), preferred_element_type=F32)

    dh = pl.pallas_call(
        mm_body, name="in_proj_bwd", grid=(t // tm, npc // tk),
        in_specs=[pl.BlockSpec((tm, tk), lambda i, k: (i, k)), pl.BlockSpec((d, tk), lambda i, k: (0, k))],
        out_specs=pl.BlockSpec((tm, d), lambda i, k: (i, 0)), out_shape=jax.ShapeDtypeStruct((t, d), F32),
        compiler_params=_cp(("parallel", "arbitrary")),
    )(dp, wp)

    tr = _tile(t, 256, 8)

    def norm_body(dh_ref, x_ref, dxn_ref, sh_ref, sc_ref, g_ref, dx_ref, dg_ref, dsc_ref, dsh_ref):
        @pl.when(pl.program_id(0) == 0)
        def _():
            for r in (dg_ref, dsc_ref, dsh_ref):
                r[...] = jnp.zeros_like(r)
        _, vj = jax.vjp(_modnorm, x_ref[...], g_ref[...], sc_ref[...], sh_ref[...])
        dx, dg, dsc, dsh = vj(dh_ref[...])
        dx_ref[...] = dxn_ref[...] + dx
        dg_ref[...] += dg
        dsc_ref[...] += dsc
        dsh_ref[...] += dsh

    vec = pl.BlockSpec((1, d), lambda i: (0, 0))
    row = pl.BlockSpec((tr, d), lambda i: (i, 0))
    return pl.pallas_call(
        norm_body, name="modnorm_bwd", grid=(t // tr,), in_specs=[row, row, row, vec, vec, vec],
        out_specs=[row, vec, vec, vec],
        out_shape=[jax.ShapeDtypeStruct((t, d), F32)] + [jax.ShapeDtypeStruct((1, d), F32)] * 3,
        compiler_params=_cp(("arbitrary",)),
    )(dh, x, dxn, shift, scale, ng)


def loss_head(x, tgt, fg):
    t, d = x.shape
    tm = _tile(t, 512, 8)

    def body(x_ref, t_ref, g_ref, l_ref, dx_ref, dg_ref):
        @pl.when(pl.program_id(0) == 0)
        def _():
            l_ref[...] = jnp.zeros_like(l_ref)
            dg_ref[...] = jnp.zeros_like(dg_ref)
        y, vj = jax.vjp(_rmsnorm, x_ref[...], g_ref[...])
        err = y - t_ref[...]
        part = 0.5 * jnp.sum(jnp.sum(err * err, axis=1, keepdims=True), axis=0, keepdims=True) / d
        l_ref[...] += jnp.broadcast_to(part, l_ref.shape)
        dx, dg = vj(err / d)
        dx_ref[...] = dx
        dg_ref[...] += dg

    row = pl.BlockSpec((tm, d), lambda i: (i, 0))
    vec = pl.BlockSpec((1, d), lambda i: (0, 0))
    return pl.pallas_call(
        body, name="loss_head", grid=(t // tm,), in_specs=[row, row, vec],
        out_specs=[pl.BlockSpec((1, LANES), lambda i: (0, 0)), row, vec],
        out_shape=[jax.ShapeDtypeStruct((1, LANES), F32), jax.ShapeDtypeStruct((t, d), F32),
                   jax.ShapeDtypeStruct((1, d), F32)],
        compiler_params=_cp(("arbitrary",)),
    )(x, tgt, fg)


def adamw(name, w, g, m, v):
    r, c = w.shape
    tr = _tile(r, 256, 8) if r % 8 == 0 else r

    def body(w_ref, g_ref, m_ref, v_ref, d_ref, mo_ref, vo_ref):
        d_ref[...], mo_ref[...], vo_ref[...] = _adamw_math(w_ref[...], g_ref[...], m_ref[...], v_ref[...])

    blk = pl.BlockSpec((tr, c), lambda i: (i, 0))
    return pl.pallas_call(
        body, name=name, grid=(r // tr,), in_specs=[blk] * 4, out_specs=[blk] * 3,
        out_shape=[jax.ShapeDtypeStruct((r, c), F32)] * 3, compiler_params=_cp(("parallel",)),
    )(w, g, m, v)


def ada_fwd(c_all, w_ada, b_loc):
    nl, d, cols = w_ada.shape
    nb = c_all.shape[0]
    tn = _tile(cols, 512, LANES)

    def body(c_ref, w_ref, b_ref, o_ref):
        ca = _silu(c_ref[...]).astype(BF16)
        o_ref[0] = jnp.dot(ca, w_ref[0].astype(BF16), preferred_element_type=F32) + b_ref[0]

    return pl.pallas_call(
        body, name="ada_fwd", grid=(nl, cols // tn),
        in_specs=[pl.BlockSpec((nb, d), lambda l, j: (0, 0)), pl.BlockSpec((1, d, tn), lambda l, j: (l, 0, j)),
                  pl.BlockSpec((1, 1, tn), lambda l, j: (l, 0, j))],
        out_specs=pl.BlockSpec((1, nb, tn), lambda l, j: (l, 0, j)),
        out_shape=jax.ShapeDtypeStruct((nl, nb, cols), F32),
        compiler_params=_cp(("parallel", "parallel")),
    )(c_all, w_ada, b_loc)


def ada_bwd(c_all_t, dmod_loc, w, m, v):
    nl, d, cols = w.shape
    nb = c_all_t.shape[1]
    tr = _tile(d, 256, 8)

    def body(c_ref, dm_ref, w_ref, m_ref, v_ref, g_ref, d_ref, mo_ref, vo_ref):
        ca = _silu(c_ref[...])
        dm = dm_ref[0]
        g = _lane_col(ca, 0) * dm[0:1, :]
        for b in range(1, nb):
            g = g + _lane_col(ca, b) * dm[b:b + 1, :]
        g_ref[0] = g
        d_ref[0], mo_ref[0], vo_ref[0] = _adamw_math(w_ref[0], g, m_ref[0], v_ref[0])

    blk = pl.BlockSpec((1, tr, cols), lambda l, i: (l, i, 0))
    return pl.pallas_call(
        body, name="ada_bwd", grid=(nl, d // tr),
        in_specs=[pl.BlockSpec((tr, nb), lambda l, i: (i, 0)), pl.BlockSpec((1, nb, cols), lambda l, i: (l, 0, 0)),
                  blk, blk, blk],
        out_specs=[blk] * 4, out_shape=[jax.ShapeDtypeStruct((nl, d, cols), F32)] * 4,
        compiler_params=_cp(("parallel", "parallel")),
    )(c_all_t, dmod_loc, w, m, v)


def sum8(g):
    _, r, c = g.shape
    tr = _tile(r, 256, 8)

    def body(g_ref, o_ref):
        acc = g_ref[0]
        for k in range(1, NDEV):
            acc = acc + g_ref[k]
        o_ref[...] = acc

    return pl.pallas_call(
        body, name="sum8", grid=(r // tr,), in_specs=[pl.BlockSpec((NDEV, tr, c), lambda i: (0, i, 0))],
        out_specs=pl.BlockSpec((tr, c), lambda i: (i, 0)), out_shape=jax.ShapeDtypeStruct((r, c), F32),
        compiler_params=_cp(("parallel",)),
    )(g)


def pair_sum(cflag, g, r1):
    _, ns, r, c = g.shape
    tr = _tile(r, 256, 8)

    def body(cf_ref, g0_ref, g1_ref, r_ref, ob_ref):
        keep = jnp.where(cf_ref[0:1, 0:1] == 0.0, g0_ref[0], g1_ref[0])
        ob_ref[...] = (keep + r_ref[...]).astype(BF16)

    blk = pl.BlockSpec((1, tr, c), lambda s, i: (s, i, 0))
    return pl.pallas_call(
        body, name="pair_sum", grid=(ns, r // tr),
        in_specs=[pl.BlockSpec((1, LANES), lambda s, i: (0, 0)), pl.BlockSpec((1, 1, tr, c), lambda s, i: (0, s, i, 0)),
                  pl.BlockSpec((1, 1, tr, c), lambda s, i: (1, s, i, 0)), blk], out_specs=blk,
        out_shape=jax.ShapeDtypeStruct((ns, r, c), BF16), compiler_params=_cp(("parallel", "parallel")),
    )(cflag, g, g, r1)


def chip_sum(own, r1, r2):
    r, c = own.shape
    tr = _tile(r, 256, 8)

    def body(g_ref, r1_ref, r2_ref, o_ref):
        acc = g_ref[...] + r1_ref[...]
        for k in range(NCHIP - 1):
            acc = acc + r2_ref[k].astype(F32)
        o_ref[...] = acc

    blk = pl.BlockSpec((tr, c), lambda i: (i, 0))
    return pl.pallas_call(
        body, name="chip_sum", grid=(r // tr,),
        in_specs=[blk, blk, pl.BlockSpec((NCHIP - 1, tr, c), lambda i: (0, i, 0))], out_specs=blk,
        out_shape=jax.ShapeDtypeStruct((r, c), F32), compiler_params=_cp(("parallel",)),
    )(own, r1, r2)


def adamw_halves(name, cflag, w, own, recv, m, v):
    _, r, c = w.shape
    tr = _tile(r, 256, 8)

    def body(cf_ref, w_ref, a_ref, b_ref, m_ref, v_ref, g_ref, d_ref, mo_ref, vo_ref):
        is_own = cf_ref[0:1, 0:1] == pl.program_id(0).astype(F32)
        g = jnp.where(is_own, a_ref[...], b_ref[...])
        g_ref[0] = g
        d_ref[0], mo_ref[0], vo_ref[0] = _adamw_math(w_ref[0], g, m_ref[0], v_ref[0])

    blk = pl.BlockSpec((1, tr, c), lambda h, i: (h, i, 0))
    hlf = pl.BlockSpec((tr, c), lambda h, i: (i, 0))
    return pl.pallas_call(
        body, name=name, grid=(2, r // tr),
        in_specs=[pl.BlockSpec((1, LANES), lambda h, i: (0, 0)), blk, hlf, hlf, blk, blk], out_specs=[blk] * 4,
        out_shape=[jax.ShapeDtypeStruct(w.shape, F32)] * 4, compiler_params=_cp(("parallel", "parallel")),
    )(cflag, w, own, recv, m, v)


def _me():
    return lax.axis_index("x"), lax.axis_index("y"), lax.axis_index("c")


_FLIPS = ((1, 0), (0, 1), (1, 1))


def all_gather8(v):
    m_per, n = v.shape

    def body(x_ref, out_ref, send_sems, recv_sems, local_sem):
        x, y, c = _me()
        me, sibling = (x, y, c), (x, y, 1 - c)
        chips = [(x ^ fx, y ^ fy) for fx, fy in _FLIPS]

        def rows(px, py, pc):
            return out_ref.at[pl.ds((4 * px + 2 * py + pc) * m_per, m_per), :]

        def copy(k, block, to, src=None):
            return pltpu.make_async_remote_copy(
                src_ref=rows(*block) if src is None else src, dst_ref=rows(*block),
                send_sem=send_sems.at[k], recv_sem=recv_sems.at[k], device_id=to, device_id_type=MESH)

        mine = pltpu.make_async_copy(x_ref, rows(*me), local_sem)
        mine.start()
        first = [copy(0, me, sibling, src=x_ref)]
        first += [copy(1 + j, me, (*chip, c), src=x_ref) for j, chip in enumerate(chips)]
        for cp in first:
            cp.start()
        passed = [copy(4 + j, (*chip, c), sibling) for j, chip in enumerate(chips)]
        for j, chip in enumerate(chips):
            copy(1 + j, (*chip, c), me).wait_recv()
            passed[j].start()
        copy(0, sibling, me).wait_recv()
        for j, chip in enumerate(chips):
            copy(4 + j, (*chip, 1 - c), me).wait_recv()
        for cp in first + passed:
            cp.wait_send()
        mine.wait()

    return pl.pallas_call(
        body, name="all_gather8", out_shape=jax.ShapeDtypeStruct((NDEV * m_per, n), v.dtype),
        in_specs=[pl.BlockSpec(memory_space=pltpu.VMEM)], out_specs=pl.BlockSpec(memory_space=pltpu.VMEM),
        scratch_shapes=[pltpu.SemaphoreType.DMA((7,)), pltpu.SemaphoreType.DMA((7,)), pltpu.SemaphoreType.DMA],
        compiler_params=pltpu.CompilerParams(vmem_limit_bytes=VMEM_LIMIT),
    )(v)


def gather_weights(ws):
    na = len(ws)

    def body(*refs):
        srcs, outs = refs[:na], refs[na:2 * na]
        send_sems, recv_sems = refs[2 * na:]
        x, y, c = _me()
        chip = 2 * x + y
        sibling = (x, y, 1 - c)
        peers = [(x ^ fx, y ^ fy) for fx, fy in _FLIPS]

        def ici(a, j, half_of, to):
            return pltpu.make_async_remote_copy(
                src_ref=srcs[a].at[c], dst_ref=outs[a].at[half_of, c], send_sem=send_sems.at[a * 6 + j],
                recv_sem=recv_sems.at[a * 6 + j], device_id=to, device_id_type=MESH)

        def d2d(a, j, slot, half):
            return pltpu.make_async_remote_copy(
                src_ref=outs[a].at[slot, half], dst_ref=outs[a].at[slot, half], send_sem=send_sems.at[a * 6 + 3 + j],
                recv_sem=recv_sems.at[a * 6 + 3 + j], device_id=sibling, device_id_type=MESH)

        sends = []
        for a in range(na):
            for j, (px, py) in enumerate(peers):
                cp = ici(a, j, chip, (px, py, c))
                cp.start()
                sends.append(cp)
        for a in range(na):
            for j, (px, py) in enumerate(peers):
                ici(a, j, 2 * px + py, (px, py, c)).wait_recv()
                cp = d2d(a, j, 2 * px + py, c)
                cp.start()
                sends.append(cp)
        for a in range(na):
            for j, (px, py) in enumerate(peers):
                d2d(a, j, 2 * px + py, 1 - c).wait_recv()
        for cp in sends:
            cp.wait_send()

    return pl.pallas_call(
        body, name="gather_weights",
        out_shape=[jax.ShapeDtypeStruct((NCHIP,) + w.shape, w.dtype) for w in ws],
        in_specs=[ANY] * na, out_specs=[ANY] * na,
        scratch_shapes=[pltpu.SemaphoreType.DMA((6 * na,)), pltpu.SemaphoreType.DMA((6 * na,))],
    )(*ws)


def sibling_swap(name, gs, other_half=False):
    na = len(gs)

    def body(*refs):
        srcs, outs = refs[:na], refs[na:2 * na]
        send_sems, recv_sems = refs[2 * na:]
        x, y, c = _me()
        cps = [pltpu.make_async_remote_copy(
            src_ref=srcs[a].at[1 - c] if other_half else srcs[a], dst_ref=outs[a], send_sem=send_sems.at[a],
            recv_sem=recv_sems.at[a], device_id=(x, y, 1 - c), device_id_type=MESH) for a in range(na)]
        for cp in cps:
            cp.start()
        for cp in cps:
            cp.wait()

    return pl.pallas_call(
        body, name=name, out_shape=[jax.ShapeDtypeStruct(g.shape[1:] if other_half else g.shape, g.dtype) for g in gs],
        in_specs=[ANY] * na, out_specs=[ANY] * na,
        scratch_shapes=[pltpu.SemaphoreType.DMA((na,)), pltpu.SemaphoreType.DMA((na,))],
    )(*gs)


def chip_exchange(ps):
    na = len(ps)

    def body(*refs):
        srcs, outs = refs[:na], refs[na:2 * na]
        send_sems, recv_sems = refs[2 * na:]
        x, y, c = _me()
        cps = []
        for a in range(na):
            for j, (fx, fy) in enumerate(_FLIPS):
                px, py = x ^ fx, y ^ fy
                cps.append(pltpu.make_async_remote_copy(
                    src_ref=srcs[a].at[2 * px + py], dst_ref=outs[a].at[j], send_sem=send_sems.at[a * 3 + j],
                    recv_sem=recv_sems.at[a * 3 + j], device_id=(px, py, c), device_id_type=MESH))
        for cp in cps:
            cp.start()
        for cp in cps:
            cp.wait()

    return pl.pallas_call(
        body, name="chip_exchange",
        out_shape=[jax.ShapeDtypeStruct((NCHIP - 1,) + p_.shape[1:], p_.dtype) for p_ in ps],
        in_specs=[ANY] * na, out_specs=[ANY] * na,
        scratch_shapes=[pltpu.SemaphoreType.DMA((3 * na,)), pltpu.SemaphoreType.DMA((3 * na,))],
    )(*ps)


class _Cfg:
    def __init__(self, x, a_log, sg_w, cv_ln_g, cv_w, conv_qkv):
        self.t, self.d = x.shape[1], x.shape[2]
        self.nl, self.h = a_log.shape
        self.dn = self.h * LANES
        self.g = sg_w.shape[1]
        self.sg = self.g * LANES
        self.cv = cv_ln_g.shape[1]
        self.kc = cv_w.shape[1]
        self.k4 = conv_qkv.shape[1]
        self.o_z = 3 * self.dn
        self.o_sg = 4 * self.dn
        self.o_cv = self.o_sg + 3 * self.sg
        self.o_ba = self.o_cv + 3 * self.cv
        self.npc = self.o_ba + LANES
        self.d_in = self.o_ba + 2 * self.h
        self.dmix = self.dn + self.sg + self.cv
        self.hb_fwd = _tile(self.h, 8, 1)
        self.hb_bwd = _tile(self.h, 8, 1)


def _runs(cfg):
    dn, h = cfg.dn, cfg.h
    runs = [(part * dn + hd * LANES, hd * 3 * LANES + part * LANES, LANES) for part in range(3) for hd in range(h)]
    return runs + [(3 * dn, 3 * dn, dn), (4 * dn, cfg.o_ba, 2 * h), (4 * dn + 2 * h, 4 * dn, cfg.o_ba - 4 * dn)]


def _assemble_perm(cfg, shards):
    cols = shards[0].shape[-1]
    pieces = []
    for nat, _, wdt in sorted(_runs(cfg), key=lambda r_: r_[1]):
        a = nat
        while a < nat + wdt:
            s = a // cols
            b = min(nat + wdt, (s + 1) * cols)
            pieces.append(shards[s][..., a - s * cols:b - s * cols])
            a = b
    pieces.append(jnp.zeros(shards[0].shape[:-1] + (cfg.npc - cfg.o_ba - 2 * cfg.h,), shards[0].dtype))
    return jnp.concatenate(pieces, axis=-1)


def _natural_pieces(cfg, s, cols):
    lo, hi = s * cols, (s + 1) * cols
    pieces = []
    for nat, perm, wdt in sorted(_runs(cfg)):
        a, b = max(nat, lo), min(nat + wdt, hi)
        if a < b:
            pieces.append((perm + a - nat, perm + b - nat))
    return pieces


def _layer_fwd(cfg, x, mod, lw):
    shift, scale, gate = mod
    p, ht = in_proj(x, shift, scale, lw["norm_g"], lw["wp"])
    qk_post = [_qk_post, _qk_post, _v_post]
    qkv = conv_fwd("dn_pre_fwd", cfg.k4, HALO4, lambda a: a, [(p, 0)], lw["conv_qkv"], qk_post, [], [],
                   3 * cfg.dn, 3 * LANES)
    wy = dn_wy_fwd(qkv, p, cfg.o_ba, lw["alog_b"], lw["dtb_b"], cfg.h)
    y_dn, ss = dn_seq_fwd(*wy[:6], p, cfg.o_z, lw["dn_norm_g"], cfg.h, cfg.hb_fwd)
    y_sg = sg_fwd(p, cfg.o_sg, cfg.sg, lw["sg_ln_g"], lw["sg_ln_b"], lw["sg_w"], lw["sg_bias_b"])
    cv_post = [_cv_post] * (cfg.cv // LANES)
    y_cv = conv_fwd("cv_fwd", cfg.kc, HALO31, _glu, [(p, cfg.o_cv), (p, cfg.o_cv + cfg.cv)], lw["cv_w"], cv_post,
                    [(p, cfg.o_cv + 2 * cfg.cv)], [lw["cv_b"], lw["cv_ln_g"], lw["cv_ln_b"]], cfg.cv, cfg.cv)
    xn, y, yt = out_proj(x, y_dn, y_sg, y_cv, lw["wo"], gate)
    return xn, dict(x=x, p=p, ht=ht, qkv=qkv, wy=wy, ss=ss, y=y, yt=yt)


def _layer_bwd(cfg, dxn, mod, lw, sv):
    shift, scale, gate = mod
    p = sv["p"]
    d_dn, d_sg, d_cv, dyb, dgate = out_proj_bwd(dxn, sv["y"], gate, lw["wo"], cfg.dn, cfg.sg, cfg.cv)
    g_wo = matmul_acc("w_out_grad", sv["yt"], dyb)
    cv_post = [_cv_post] * (cfg.cv // LANES)
    dcv, g_cvw, (g_cvb, g_cvlg, g_cvlb) = conv_bwd(
        "cv_bwd", cfg.kc, HALO31, _glu, [(p, cfg.o_cv), (p, cfg.o_cv + cfg.cv)], lw["cv_w"], cv_post,
        [(p, cfg.o_cv + 2 * cfg.cv)], [lw["cv_b"], lw["cv_ln_g"], lw["cv_ln_b"]], d_cv, cfg.cv, cfg.cv, tm_pref=256)
    dsg, g_sglg, g_sglb, g_sgw, g_sgb = sg_bwd(p, cfg.o_sg, cfg.sg, lw["sg_ln_g"], lw["sg_ln_b"], lw["sg_w"],
                                               lw["sg_bias_b"], d_sg)
    *dwy, dz, g_dng = dn_seq_bwd(*sv["wy"][:6], p, cfg.o_z, lw["dn_norm_g"], sv["ss"], d_dn, cfg.h, cfg.hb_bwd)
    dqkv, dba, g_al, g_dt = dn_wy_bwd(sv["qkv"], p, cfg.o_ba, lw["alog_b"], lw["dtb_b"], sv["wy"][6], *dwy, cfg.h)
    qk_post = [_qk_post, _qk_post, _v_post]
    dqkv_pre, g_cq, _ = conv_bwd("dn_pre_bwd", cfg.k4, HALO4, lambda a: a, [(p, 0)], lw["conv_qkv"], qk_post, [], [],
                                 dqkv, 3 * cfg.dn, 3 * LANES)
    dp = jnp.concatenate([dqkv_pre, dz.astype(BF16), dsg, dcv, dba.astype(BF16)], axis=1)
    g_wp = matmul_acc("w_in_grad", sv["ht"], dp)
    dx, g_ng, dscale, dshift = in_proj_bwd(dp, lw["wp"], sv["x"], dxn, shift, scale, lw["norm_g"])
    grads = dict(norm_g=g_ng, conv_qkv=g_cq, a_log=g_al[:cfg.h, 0], dt_bias=g_dt[:cfg.h, 0], dn_norm_g=g_dng,
                 sg_ln_g=g_sglg, sg_ln_b=g_sglb, sg_w=g_sgw, sg_b=g_sgb[:, :, 0], cv_w=g_cvw, cv_b=g_cvb,
                 cv_ln_g=g_cvlg, cv_ln_b=g_cvlb, wp=g_wp, wo=g_wo)
    return dx, grads, (dshift, dscale, dgate)


def _local_step(cfg, xs, tgt, mods, lws, fg):
    nl = len(lws)
    saved = []
    for l in range(nl):
        xs, sv = _layer_fwd(cfg, xs, mods[l], lws[l])
        saved.append(sv)
    loss_b, dx, g_fg = loss_head(xs, tgt, fg)
    lg = [None] * nl
    dmods = [None] * nl
    for l in reversed(range(nl)):
        dx, lg[l], dmods[l] = _layer_bwd(cfg, dx, mods[l], lws[l], saved[l])
    return loss_b, dx, g_fg, lg, dmods


SMALL = ("norm_g", "conv_qkv", "a_log", "dt_bias", "dn_norm_g", "sg_ln_g", "sg_ln_b", "sg_w", "sg_b", "cv_w",
         "cv_b", "cv_ln_g", "cv_ln_b", "final_g", "b_ada")
PACK_N = 1024


def _pack(arrs):
    flat = jnp.concatenate([a.reshape(-1).astype(F32) for a in arrs])
    rows = -(-flat.shape[0] // PACK_N)
    rows = -(-rows // 8) * 8
    return jnp.pad(flat, (0, rows * PACK_N - flat.shape[0])).reshape(rows, PACK_N)


def _unpack(buf, shapes):
    flat = buf.reshape(-1)
    out, o = [], 0
    for s in shapes:
        n = 1
        for d_ in s:
            n *= d_
        out.append(flat[o:o + n].reshape(s))
        o += n
    return out


def kernel(x, c, norm_g, w_ada, b_ada, w_in, conv_qkv, a_log, dt_bias, dn_norm_g, sg_ln_g, sg_ln_b, sg_w, sg_b, cv_w, cv_b, cv_ln_g, cv_ln_b, w_out, final_g, loss_target, m_norm_g, m_w_ada, m_b_ada, m_w_in, m_conv_qkv, m_a_log, m_dt_bias, m_dn_norm_g, m_sg_ln_g, m_sg_ln_b, m_sg_w, m_sg_b, m_cv_w, m_cv_b, m_cv_ln_g, m_cv_ln_b, m_w_out, m_final_g, v_norm_g, v_w_ada, v_b_ada, v_w_in, v_conv_qkv, v_a_log, v_dt_bias, v_dn_norm_g, v_sg_ln_g, v_sg_ln_b, v_sg_w, v_sg_b, v_cv_w, v_cv_b, v_cv_ln_g, v_cv_ln_b, v_w_out, v_final_g):
    cfg = _Cfg(x, a_log, sg_w, cv_ln_g, cv_w, conv_qkv)
    nl, d, t, h = cfg.nl, cfg.d, cfg.t, cfg.h
    lh = nl // 2
    ax, ay, ac = _me()
    chip = 2 * ax + ay
    dev = 2 * chip + ac
    wts = dict(norm_g=norm_g, w_ada=w_ada, b_ada=b_ada, w_in=w_in, conv_qkv=conv_qkv, a_log=a_log, dt_bias=dt_bias,
               dn_norm_g=dn_norm_g, sg_ln_g=sg_ln_g, sg_ln_b=sg_ln_b, sg_w=sg_w, sg_b=sg_b, cv_w=cv_w, cv_b=cv_b,
               cv_ln_g=cv_ln_g, cv_ln_b=cv_ln_b, w_out=w_out, final_g=final_g)
    mom = dict(norm_g=m_norm_g, w_ada=m_w_ada, b_ada=m_b_ada, w_in=m_w_in, conv_qkv=m_conv_qkv, a_log=m_a_log,
               dt_bias=m_dt_bias, dn_norm_g=m_dn_norm_g, sg_ln_g=m_sg_ln_g, sg_ln_b=m_sg_ln_b, sg_w=m_sg_w,
               sg_b=m_sg_b, cv_w=m_cv_w, cv_b=m_cv_b, cv_ln_g=m_cv_ln_g, cv_ln_b=m_cv_ln_b, w_out=m_w_out,
               final_g=m_final_g)
    vel = dict(norm_g=v_norm_g, w_ada=v_w_ada, b_ada=v_b_ada, w_in=v_w_in, conv_qkv=v_conv_qkv, a_log=v_a_log,
               dt_bias=v_dt_bias, dn_norm_g=v_dn_norm_g, sg_ln_g=v_sg_ln_g, sg_ln_b=v_sg_ln_b, sg_w=v_sg_w,
               sg_b=v_sg_b, cv_w=v_cv_w, cv_b=v_cv_b, cv_ln_g=v_cv_ln_g, cv_ln_b=v_cv_ln_b, w_out=v_w_out,
               final_g=v_final_g)
    ada_cols = w_ada.shape[2]
    in_cols = w_in.shape[2]
    out_rows = w_out.shape[1]
    cq_cols = conv_qkv.shape[2]
    cvw_cols = cv_w.shape[2]

    c_all = all_gather8(jnp.pad(c, ((0, 7), (0, 0)))).reshape(NDEV, 8, d)[:, 0, :]
    b_loc = lax.dynamic_slice_in_dim(b_ada, chip * ada_cols, ada_cols, axis=1)[:, None, :]
    mod_part = ada_fwd(c_all, w_ada, b_loc)
    mod_all = all_gather8(mod_part.reshape(nl * NDEV, ada_cols)).reshape(NDEV, nl, NDEV, ada_cols)
    mod_me = lax.dynamic_index_in_dim(mod_all[0::2], dev, axis=2, keepdims=False)
    mod_me = jnp.moveaxis(mod_me, 0, 1).reshape(nl, 3, 1, d)

    win_b = w_in.astype(BF16).reshape(2, lh, d, in_cols)
    wout_b = w_out.astype(BF16).reshape(2, lh, out_rows, d)
    win_all, wout_all = gather_weights([win_b, wout_b])
    win_all = lax.dynamic_update_index_in_dim(win_all, win_b, chip, axis=0)
    wout_all = lax.dynamic_update_index_in_dim(wout_all, wout_b, chip, axis=0)
    win_all = win_all.reshape(NCHIP, nl, d, in_cols)
    wp_all = [_assemble_perm(cfg, [win_all[s, l] for s in range(NCHIP)]) for l in range(nl)]
    wo_all = jnp.moveaxis(wout_all.reshape(NCHIP, nl, out_rows, d), 0, 1).reshape(nl, NCHIP * out_rows, d)

    cq_all = all_gather8(conv_qkv.reshape(nl * cfg.k4, cq_cols)).reshape(NDEV, nl, cfg.k4, cq_cols)[0::2]
    cq_full = jnp.moveaxis(cq_all, 0, 2).reshape(nl, cfg.k4, NCHIP * cq_cols)
    cq_perm = _perm_cols_qkv(cfg, cq_full)
    kcp = -(-cfg.kc // 8) * 8
    cvw_all = all_gather8(jnp.pad(cv_w, ((0, 0), (0, kcp - cfg.kc), (0, 0))).reshape(nl * kcp, cvw_cols))
    cvw_all = cvw_all.reshape(NDEV, nl, kcp, cvw_cols)[0::2]
    cvw_full = jnp.moveaxis(cvw_all, 0, 2).reshape(nl, kcp, NCHIP * cvw_cols)[:, :cfg.kc]

    hp = -(-h // 8) * 8
    lws = []
    for l in range(nl):
        lws.append(dict(
            norm_g=norm_g[l][None], wp=wp_all[l], wo=wo_all[l], conv_qkv=cq_perm[l],
            alog_b=jnp.pad(jnp.broadcast_to(a_log[l][:, None], (h, LANES)), ((0, hp - h), (0, 0))),
            dtb_b=jnp.pad(jnp.broadcast_to(dt_bias[l][:, None], (h, LANES)), ((0, hp - h), (0, 0))),
            dn_norm_g=dn_norm_g[l][None], sg_ln_g=sg_ln_g[l][None], sg_ln_b=sg_ln_b[l][None], sg_w=sg_w[l],
            sg_bias_b=jnp.broadcast_to(sg_b[l][:, :, None], (cfg.g, LANES, LANES)),
            cv_w=cvw_full[l], cv_b=cv_b[l][None], cv_ln_g=cv_ln_g[l][None], cv_ln_b=cv_ln_b[l][None]))

    mods = [(mod_me[l, 0], mod_me[l, 1], mod_me[l, 2]) for l in range(nl)]
    loss_b, dx, g_fg, lg, dmods = _local_step(cfg, x[0], loss_target[0], mods, lws, final_g[None])
    grad_x = dx[None]

    dmod = jnp.stack([jnp.concatenate(dm, axis=1)[0] for dm in dmods])
    stack = lambda k: jnp.stack([g_[k] for g_ in lg])
    small_local = [stack(k).reshape(wts_shape) for k, wts_shape in
                   (("norm_g", (nl, d)), ("conv_qkv", (nl, cfg.k4, 3 * cfg.dn)), ("a_log", (nl, h)),
                    ("dt_bias", (nl, h)), ("dn_norm_g", (nl, LANES)), ("sg_ln_g", (nl, cfg.sg)),
                    ("sg_ln_b", (nl, cfg.sg)), ("sg_w", (nl, cfg.g, LANES, LANES)), ("sg_b", (nl, cfg.g, LANES)),
                    ("cv_w", (nl, cfg.kc, cfg.cv)), ("cv_b", (nl, cfg.cv)), ("cv_ln_g", (nl, cfg.cv)),
                    ("cv_ln_b", (nl, cfg.cv)))]
    small_local[1] = _unperm_cols_qkv(cfg, small_local[1])
    small_local += [g_fg[0], dmod, loss_b[0, 0:1]]
    shapes = [a.shape for a in small_local]
    packed = _pack(small_local)
    rows = packed.shape[0]
    gathered = all_gather8(packed).reshape(NDEV, rows, PACK_N)
    summed = _unpack(sum8(gathered), shapes)
    sgrads = dict(zip(SMALL, summed[:15]))
    loss = summed[15][0]
    sgrads["conv_qkv"] = lax.dynamic_slice_in_dim(sgrads["conv_qkv"], chip * cq_cols, cq_cols, axis=2)
    sgrads["cv_w"] = lax.dynamic_slice_in_dim(sgrads["cv_w"], chip * cvw_cols, cvw_cols, axis=2)

    off = sum(math.prod(s) for s in shapes[:14])
    dmod_all = gathered.reshape(NDEV, rows * PACK_N)[:, off:off + nl * 3 * d].reshape(NDEV, nl, 3 * d)
    dmod_loc = jnp.moveaxis(lax.dynamic_slice_in_dim(dmod_all, chip * ada_cols, ada_cols, axis=2), 0, 1)
    g_wada, d_wada, nm_wada, nv_wada = ada_bwd(c_all.T, dmod_loc, w_ada, m_w_ada, v_w_ada)

    g_in = jnp.zeros((2, NCHIP, lh * d, in_cols), F32)
    g_out = jnp.zeros((2, NCHIP, lh * out_rows, d), F32)
    for l in range(nl):
        hh, j = divmod(l, lh)
        for s in range(NCHIP):
            col = 0
            for a, b in _natural_pieces(cfg, s, in_cols):
                g_in = g_in.at[hh, s, j * d:(j + 1) * d, col:col + b - a].set(lg[l]["wp"][:, a:b])
                col += b - a
            g_out = g_out.at[hh, s, j * out_rows:(j + 1) * out_rows, :].set(lg[l]["wo"][s * out_rows:(s + 1) * out_rows, :])
    cflag = jnp.full((1, LANES), ac, F32)
    r1_in, r1_out = sibling_swap("swap_halves", [g_in, g_out], other_half=True)
    r2_in, r2_out = chip_exchange([pair_sum(cflag, g_in, r1_in), pair_sum(cflag, g_out, r1_out)])
    mine = lambda g_: lax.dynamic_index_in_dim(g_, chip, axis=0, keepdims=False)
    keep = lambda g_: lax.dynamic_index_in_dim(g_, ac, axis=0, keepdims=False)
    h_in = chip_sum(mine(keep(g_in)), mine(r1_in), r2_in)
    h_out = chip_sum(mine(keep(g_out)), mine(r1_out), r2_out)
    o_in, o_out = sibling_swap("join_halves", [h_in, h_out])

    v3 = lambda a, r_, c_: a.reshape(2, lh * r_, c_)
    grad_w_in, d_in_, nm_in, nv_in = adamw_halves("adamw_w_in", cflag, v3(w_in, d, in_cols), h_in, o_in,
                                                  v3(m_w_in, d, in_cols), v3(v_w_in, d, in_cols))
    grad_w_out, d_out_, nm_out, nv_out = adamw_halves("adamw_w_out", cflag, v3(w_out, out_rows, d), h_out, o_out,
                                                      v3(m_w_out, out_rows, d), v3(v_w_out, out_rows, d))
    grad_w_in = grad_w_in.reshape(w_in.shape)
    grad_w_out = grad_w_out.reshape(w_out.shape)
    sshapes = [wts[k].shape for k in SMALL]
    pk = lambda dct: _pack([dct[k] for k in SMALL])
    d_s, m_s, v_s = adamw("adamw_small", pk(wts), pk(sgrads), pk(mom), pk(vel))
    d_small = dict(zip(SMALL, _unpack(d_s, sshapes)))
    m_small = dict(zip(SMALL, _unpack(m_s, sshapes)))
    v_small = dict(zip(SMALL, _unpack(v_s, sshapes)))

    grads = dict(sgrads, w_ada=g_wada, w_in=grad_w_in, w_out=grad_w_out)
    deltas = dict(d_small, w_ada=d_wada, w_in=d_in_.reshape(w_in.shape), w_out=d_out_.reshape(w_out.shape))
    new_m = dict(m_small, w_ada=nm_wada, w_in=nm_in.reshape(w_in.shape), w_out=nm_out.reshape(w_out.shape))
    new_v = dict(v_small, w_ada=nv_wada, w_in=nv_in.reshape(w_in.shape), w_out=nv_out.reshape(w_out.shape))
    order = ("norm_g", "w_ada", "b_ada", "w_in", "conv_qkv", "a_log", "dt_bias", "dn_norm_g", "sg_ln_g", "sg_ln_b",
             "sg_w", "sg_b", "cv_w", "cv_b", "cv_ln_g", "cv_ln_b", "w_out", "final_g")
    return (loss, grad_x, *[grads[k] for k in order], *[deltas[k] for k in order], *[new_m[k] for k in order],
            *[new_v[k] for k in order])


def _perm_cols_qkv(cfg, w):
    lead = w.shape[:-1]
    return jnp.moveaxis(w.reshape(lead + (3, cfg.h, LANES)), -3, -2).reshape(lead + (3 * cfg.dn,))


def _unperm_cols_qkv(cfg, w):
    lead = w.shape[:-1]
    return jnp.moveaxis(w.reshape(lead + (cfg.h, 3, LANES)), -3, -2).reshape(lead + (3 * cfg.dn,))
```

```python
import functools
import math

import jax
import jax.numpy as jnp
from jax import lax
from jax.experimental import pallas as pl
from jax.experimental.pallas import tpu as pltpu

F32 = jnp.float32
BF16 = jnp.bfloat16
EPS = 1e-6
LN_EPS = 1e-5
LANES = 128
CHUNK = 64
SUBLANES = 8
HALO4 = 8
HALO31 = 32
NCHIP = 4
NDEV = 8
VMEM_LIMIT = 56 * 2 ** 20
ADAM_LR, ADAM_B1, ADAM_B2, ADAM_EPS, ADAM_WD, ADAM_STEP = 0.001, 0.9, 0.999, 1e-08, 0.01, 10
MESH = pl.DeviceIdType.MESH
ANY = pl.BlockSpec(memory_space=pl.ANY)


def _cp(sem=None, vmem=VMEM_LIMIT):
    return pltpu.CompilerParams(dimension_semantics=sem, vmem_limit_bytes=vmem)


def _tile(n, pref, mult):
    t = min(n, pref) // mult * mult
    while t > 0 and n % t:
        t -= mult
    return t if t > 0 else n


def _split(a):
    hi = a.astype(BF16)
    return hi, (a - hi.astype(F32)).astype(BF16)


def _raw_dot(a, b, ca, cb, hi):
    dn = (((ca,), (cb,)), ((), ()))
    if hi:
        ah, al = _split(a.astype(F32))
        bh, bl = _split(b.astype(F32))
        d3 = lambda x, y: lax.dot_general(x, y, dn, preferred_element_type=F32)
        return d3(ah, bh) + (d3(al, bh) + d3(ah, bl))
    return lax.dot_general(a.astype(BF16), b.astype(BF16), dn, preferred_element_type=F32)


@functools.partial(jax.custom_vjp, nondiff_argnums=(2, 3, 4))
def bdot(a, b, ca, cb, hi):
    return _raw_dot(a, b, ca, cb, hi)


def _bdot_fwd(a, b, ca, cb, hi):
    return _raw_dot(a, b, ca, cb, hi), (a, b)


def _bdot_bwd(ca, cb, hi, res, ct):
    a, b = res
    fa, fb = 1 - ca, 1 - cb
    da = _raw_dot(ct, b, 1, fb, hi) if ca == 1 else _raw_dot(b, ct, fb, 1, hi)
    db = _raw_dot(a, ct, fa, 0, hi) if cb == 0 else _raw_dot(ct, a, 0, fa, hi)
    return da.astype(a.dtype), db.astype(b.dtype)


bdot.defvjp(_bdot_fwd, _bdot_bwd)


def _sigmoid(x):
    return jax.nn.sigmoid(x)


def _silu(x):
    return x * _sigmoid(x)


def _gelu(x):
    return 0.5 * x * (1.0 + lax.erf(x * (2.0 ** -0.5)))


def _softplus(x):
    return jnp.maximum(x, 0.0) + jnp.log(1.0 + jnp.exp(-jnp.abs(x)))


def _modnorm(x, g, scale, shift):
    y = x * lax.rsqrt(jnp.mean(x * x, axis=-1, keepdims=True) + EPS)
    return (y * g) * (1.0 + scale) + shift


def _rmsnorm(x, g):
    return x * lax.rsqrt(jnp.mean(x * x, axis=-1, keepdims=True) + EPS) * g


def _layernorm(x, g, b):
    mu = jnp.mean(x, axis=-1, keepdims=True)
    xc = x - mu
    var = jnp.mean(xc * xc, axis=-1, keepdims=True)
    return xc * lax.rsqrt(var + LN_EPS) * g + b


def _l2norm(t):
    return t * lax.rsqrt(jnp.sum(t * t, axis=-1, keepdims=True) + EPS)


def _adamw_math(w, g, m, v):
    mn = ADAM_B1 * m + (1.0 - ADAM_B1) * g
    vn = ADAM_B2 * v + (1.0 - ADAM_B2) * (g * g)
    mh = mn / (1.0 - ADAM_B1 ** ADAM_STEP)
    vh = vn / (1.0 - ADAM_B2 ** ADAM_STEP)
    delta = -ADAM_LR * (mh / (jnp.sqrt(vh) + ADAM_EPS) + ADAM_WD * w)
    return delta, mn, vn


def _each(f, *lists):
    return [f(*xs) for xs in zip(*lists)]


def _wy(q, k, v, bcol, acol, alog, dtb, tinv=None):
    c = CHUNK
    r = lax.broadcasted_iota(jnp.int32, (c, c), 0)
    cc = lax.broadcasted_iota(jnp.int32, (c, c), 1)
    rr = lax.broadcasted_iota(jnp.int32, (c, 1), 0)
    tri_incl, tri_strict, eye = r >= cc, r > cc, r == cc
    beta = _each(_sigmoid, bcol)
    g = _each(lambda al, a_, dt: -jnp.exp(al) * _softplus(a_ + dt), alog, acol, dtb)
    gb = [jnp.broadcast_to(g_, (c, c)) for g_ in g]
    g_row = [jnp.sum(jnp.where(eye, b_, 0.0), axis=0, keepdims=True) for b_ in gb]
    gc_col = [jnp.sum(jnp.where(tri_incl, jnp.broadcast_to(gr, (c, c)), 0.0), axis=1, keepdims=True) for gr in g_row]
    gc_row = [jnp.sum(jnp.where(r <= cc, b_, 0.0), axis=0, keepdims=True) for b_ in gb]
    decay = _each(lambda gcc, gcr: jnp.where(tri_incl, jnp.exp(jnp.where(tri_incl, gcc - gcr, 0.0)), 0.0),
                  gc_col, gc_row)
    qs = [q_ * (q_.shape[-1] ** -0.5) for q_ in q]
    kb = _each(lambda k_, b_: k_ * b_, k, beta)
    a = _each(lambda kb_, k_, d_: jnp.where(tri_strict, bdot(kb_, k_, 1, 1, False) * d_, 0.0), kb, k, decay)
    dv = v[0].shape[-1]
    x = _each(lambda v_, b_, kb_, gcc: jnp.concatenate([v_ * b_, kb_ * jnp.exp(gcc)], axis=1), v, beta, kb, gc_col)
    if tinv is None:
        inv = [jnp.where(eye, 1.0, 0.0) - a_ for a_ in a]
        p = a
        for _ in range(5):
            p = _each(lambda p_: bdot(p_, p_, 1, 0, True), p)
            inv = _each(lambda t_, p_: t_ + bdot(t_, p_, 1, 0, True), inv, p)
        x = _each(lambda t_, x_: bdot(t_, x_, 1, 0, True), inv, x)
    else:
        x = _each(_solve_given_inverse, a, x, tinv)
    xv = [x_[:, :dv] for x_ in x]
    xk = [x_[:, dv:] for x_ in x]
    qk = _each(lambda q_, k_, d_: bdot(q_, k_, 1, 1, False) * d_, qs, k, decay)
    g_last = [jnp.sum(jnp.where(rr == c - 1, gcc, 0.0), axis=0, keepdims=True) for gcc in gc_col]
    qg = _each(lambda q_, gcc: q_ * jnp.exp(gcc), qs, gc_col)
    kd = _each(lambda k_, gl, gcc: k_ * jnp.exp(gl - gcc), k, g_last, gc_col)
    outs = (xv, xk, qg, kd, qk, [jnp.exp(gl) for gl in g_last])
    return outs + (inv,) if tinv is None else outs


@jax.custom_vjp
def _solve_given_inverse(a, rhs, tinv):
    return _raw_dot(tinv, rhs, 1, 0, True)


def _solve_given_inverse_fwd(a, rhs, tinv):
    x = _raw_dot(tinv, rhs, 1, 0, True)
    return x, (x, tinv)


def _solve_given_inverse_bwd(res, dx):
    x, tinv = res
    drhs = _raw_dot(tinv, dx, 0, 0, True)
    return -_raw_dot(drhs, x, 1, 1, True), drhs, jnp.zeros_like(tinv)


_solve_given_inverse.defvjp(_solve_given_inverse_fwd, _solve_given_inverse_bwd)


def _seq(u, w, qg, kd, qk, e, z, s, ng):
    v_new = _each(lambda u_, w_, s_: u_ - bdot(w_, s_, 1, 0, False), u, w, s)
    o1 = _each(lambda q_, s_: bdot(q_, s_, 1, 0, False), qg, s)
    o2 = _each(lambda qk_, vn: bdot(qk_, vn, 1, 0, False), qk, v_new)
    ds = _each(lambda kd_, vn: bdot(kd_, vn, 0, 0, False), kd, v_new)
    s_next = _each(lambda s_, e_, d_: s_ * e_ + d_, s, e, ds)
    y = _each(lambda a_, b_, z_: _rmsnorm(a_ + b_, ng) * _silu(z_), o1, o2, z)
    return y, s_next


def _sg_block(u, v, gt, lg, lb, w, bias):
    n = w.shape[0]
    pr = lax.broadcasted_iota(jnp.int32, (n, n), 0) // CHUNK
    pc = lax.broadcasted_iota(jnp.int32, (n, n), 1) // CHUNK
    wm = jnp.where(pr >= pc, w, 0.0)
    vl = _layernorm(_gelu(v), lg, lb)
    mixed = bdot(wm, vl, 1, 0, False) + bias
    return _gelu(u) * mixed * _silu(gt)


def _glu(a, b):
    return a * _sigmoid(b)


def _cv_post(conv, gate, cb, lg, lb):
    return _silu(_layernorm(conv + cb, lg, lb)) * _silu(gate)


def _qk_post(conv):
    return _l2norm(_silu(conv))


def _v_post(conv):
    return _silu(conv)


def in_proj(x, shift, scale, ng, wp):
    t, d = x.shape
    npc = wp.shape[1]
    tm, tn = _tile(t, 512, LANES), _tile(npc, 2432, LANES)

    def body(x_ref, sh_ref, sc_ref, g_ref, w_ref, p_ref, ht_ref, h_scr):
        @pl.when(pl.program_id(1) == 0)
        def _():
            h = _modnorm(x_ref[...], g_ref[...], sc_ref[...], sh_ref[...])
            h_scr[...] = h.astype(BF16)
            ht_ref[...] = h.T.astype(BF16)
        p_ref[...] = jnp.dot(h_scr[...], w_ref[...], preferred_element_type=F32)

    vec = pl.BlockSpec((1, d), lambda i, j: (0, 0))
    return pl.pallas_call(
        body, name="in_proj", grid=(t // tm, npc // tn),
        in_specs=[pl.BlockSpec((tm, d), lambda i, j: (i, 0)), vec, vec, vec,
                  pl.BlockSpec((d, tn), lambda i, j: (0, j))],
        out_specs=[pl.BlockSpec((tm, tn), lambda i, j: (i, j)), pl.BlockSpec((d, tm), lambda i, j: (0, i))],
        out_shape=[jax.ShapeDtypeStruct((t, npc), F32), jax.ShapeDtypeStruct((d, t), BF16)],
        scratch_shapes=[pltpu.VMEM((tm, d), BF16)],
        compiler_params=_cp(("parallel", "arbitrary")),
    )(x, shift, scale, ng, wp)


def _roll_bank(x, bank_ref, offsets):
    rows = x.shape[0]
    residues = sorted({o % SUBLANES for o in offsets})
    for slot, b in enumerate(residues):
        bank_ref[slot] = x if b == 0 else pltpu.roll(x, rows - b, 0)
    return {o: (residues.index(o % SUBLANES), o - o % SUBLANES) for o in offsets}


def _n_residues(offsets):
    return len({o % SUBLANES for o in offsets})


def conv_fwd(name, k, halo, pre_fn, pre, w, post_fns, extras, params, c_total, tc, tm_pref=512):
    t = pre[0][0].shape[0]
    tm = _tile(t, tm_pref, halo)
    npre, nex, npar = len(pre), len(extras), len(params)
    ngr = tc // LANES
    taps = [halo - (k - 1) + j for j in range(k)]

    def body(*refs):
        prev = refs[:npre]
        cur = refs[npre:2 * npre]
        w_ref = refs[2 * npre]
        ex = refs[2 * npre + 1:2 * npre + 1 + nex]
        par = refs[2 * npre + 1 + nex:2 * npre + 1 + nex + npar]
        out_ref, buf, bank = refs[-3], refs[-2], refs[-1]
        i = pl.program_id(1)
        pv = pre_fn(*[r[...] for r in prev])
        buf[0:halo, :] = jnp.where(i > 0, pv, 0.0)
        buf[halo:, :] = pre_fn(*[r[...] for r in cur])
        where = _roll_bank(buf[...], bank, taps)
        acc = None
        for j, o in enumerate(taps):
            slot, st = where[o]
            term = w_ref[j:j + 1, :] * bank[slot, st:st + tm, :]
            acc = term if acc is None else acc + term
        for gi in range(ngr):
            sl = slice(gi * LANES, (gi + 1) * LANES)
            out_ref[:, sl] = post_fns[gi](acc[:, sl], *[e[:, sl] for e in ex], *[p_[:, sl] for p_ in par])

    hb = tm // halo
    in_specs = ([pl.BlockSpec((halo, tc), functools.partial(lambda j, i, o: (jnp.maximum(i * hb - 1, 0), o + j), o=col // tc))
                 for _, col in pre]
                + [pl.BlockSpec((tm, tc), functools.partial(lambda j, i, o: (i, o + j), o=col // tc)) for _, col in pre]
                + [pl.BlockSpec((k, tc), lambda j, i: (0, j))]
                + [pl.BlockSpec((tm, tc), functools.partial(lambda j, i, o: (i, o + j), o=col // tc)) for _, col in extras]
                + [pl.BlockSpec((1, tc), lambda j, i: (0, j)) for _ in params])
    args = [a for a, _ in pre] * 2 + [w] + [a for a, _ in extras] + list(params)
    return pl.pallas_call(
        body, name=name, grid=(c_total // tc, t // tm), in_specs=in_specs,
        out_specs=pl.BlockSpec((tm, tc), lambda j, i: (i, j)),
        out_shape=jax.ShapeDtypeStruct((t, c_total), F32),
        scratch_shapes=[pltpu.VMEM((halo + tm, tc), F32), pltpu.VMEM((_n_residues(taps), halo + tm, tc), F32)],
        compiler_params=_cp(("parallel", "arbitrary")),
    )(*args)


def conv_bwd(name, k, halo, pre_fn, pre, w, post_fns, extras, params, dout, c_total, tc, tm_pref=512):
    t = pre[0][0].shape[0]
    tm = _tile(t, tm_pref, halo)
    npre, nex, npar = len(pre), len(extras), len(params)
    ngr = tc // LANES
    nout = npre + nex
    assert nout == 1 or c_total == tc
    nblk = t // tm
    ext = tm + halo
    taps = [halo - (k - 1) + j for j in range(k)]
    back = [k - 1 - j for j in range(k)]

    def body(*refs):
        it = iter(refs)
        prev = [next(it) for _ in range(npre)]
        cur = [next(it) for _ in range(npre)]
        nxt = [next(it) for _ in range(npre)]
        w_ref = next(it)
        ex_c = [next(it) for _ in range(nex)]
        ex_n = [next(it) for _ in range(nex)]
        par = [next(it) for _ in range(npar)]
        do_c, do_n = next(it), next(it)
        din_ref, dw_ref = next(it), next(it)
        dpar = [next(it) for _ in range(npar)]
        buf, dbuf, bank, dbank = next(it), next(it), next(it), next(it)
        i = pl.program_id(1)

        @pl.when(i == 0)
        def _():
            dw_ref[...] = jnp.zeros_like(dw_ref)
            for r in dpar:
                r[...] = jnp.zeros_like(r)

        buf[0:halo, :] = jnp.where(i > 0, pre_fn(*[r[...] for r in prev]), 0.0)
        cur_vals = [r[...] for r in cur]
        buf[halo:halo + tm, :] = pre_fn(*cur_vals)
        buf[halo + tm:, :] = pre_fn(*[r[...] for r in nxt])
        where = _roll_bank(buf[...], bank, taps)
        conv = None
        for j, o in enumerate(taps):
            slot, st = where[o]
            term = w_ref[j:j + 1, :] * bank[slot, st:st + ext, :]
            conv = term if conv is None else conv + term
        don = jnp.where(i < nblk - 1, do_n[...], 0.0)
        for gi in range(ngr):
            sl = slice(gi * LANES, (gi + 1) * LANES)
            pv = [p_[:, sl] for p_ in par]
            _, vj = jax.vjp(post_fns[gi], conv[:tm, sl], *[e[:, sl] for e in ex_c], *pv)
            gr = vj(do_c[:, sl])
            dbuf[0:tm, sl] = gr[0]
            for e in range(nex):
                din_ref[:, (npre + e) * tc + gi * LANES:(npre + e) * tc + (gi + 1) * LANES] = gr[1 + e].astype(din_ref.dtype)
            for q_ in range(npar):
                dpar[q_][:, sl] += gr[1 + nex + q_]
            _, vjn = jax.vjp(post_fns[gi], conv[tm:, sl], *[e[:, sl] for e in ex_n], *pv)
            dbuf[tm:, sl] = vjn(don[:, sl])[0]
        dcur = dbuf[0:tm, :]
        dwhere = _roll_bank(dbuf[...], dbank, back)
        dpre = None
        for j in range(k):
            slot, st = dwhere[back[j]]
            term = w_ref[j:j + 1, :] * dbank[slot, st:st + tm, :]
            dpre = term if dpre is None else dpre + term
            slot, st = where[taps[j]]
            dw_ref[j:j + 1, :] += jnp.sum(dcur * bank[slot, st:st + tm, :], axis=0, keepdims=True)
        _, vjp_pre = jax.vjp(pre_fn, *cur_vals)
        for e, gval in enumerate(vjp_pre(dpre)):
            din_ref[:, e * tc:(e + 1) * tc] = gval.astype(din_ref.dtype)

    hb = tm // halo
    last_h = t // halo - 1

    def spec(kind, col):
        o = col // tc
        if kind == "prev":
            return pl.BlockSpec((halo, tc), lambda j, i: (jnp.maximum(i * hb - 1, 0), o + j))
        if kind == "next":
            return pl.BlockSpec((halo, tc), lambda j, i: (jnp.minimum((i + 1) * hb, last_h), o + j))
        return pl.BlockSpec((tm, tc), lambda j, i: (i, o + j))

    in_specs = ([spec("prev", col) for _, col in pre] + [spec("cur", col) for _, col in pre]
                + [spec("next", col) for _, col in pre] + [pl.BlockSpec((k, tc), lambda j, i: (0, j))]
                + [spec("cur", col) for _, col in extras] + [spec("next", col) for _, col in extras]
                + [pl.BlockSpec((1, tc), lambda j, i: (0, j)) for _ in params]
                + [spec("cur", 0), spec("next", 0)])
    args = [a for a, _ in pre] * 3 + [w] + [a for a, _ in extras] * 2 + list(params) + [dout, dout]
    out = pl.pallas_call(
        body, name=name, grid=(c_total // tc, nblk), in_specs=in_specs,
        out_specs=[pl.BlockSpec((tm, nout * tc), lambda j, i: (i, j)), pl.BlockSpec((k, tc), lambda j, i: (0, j))]
        + [pl.BlockSpec((1, tc), lambda j, i: (0, j)) for _ in params],
        out_shape=[jax.ShapeDtypeStruct((t, nout * c_total), BF16), jax.ShapeDtypeStruct((k, c_total), F32)]
        + [jax.ShapeDtypeStruct((1, c_total), F32) for _ in params],
        scratch_shapes=[pltpu.VMEM((2 * halo + tm, tc), F32), pltpu.VMEM((ext, tc), F32),
                        pltpu.VMEM((_n_residues(taps), 2 * halo + tm, tc), F32),
                        pltpu.VMEM((_n_residues(back), ext, tc), F32)],
        compiler_params=_cp(("parallel", "arbitrary")),
    )(*args)
    return out[0], out[1], out[2:]


def _head_pick(ref_val, row):
    rr = lax.broadcasted_iota(jnp.int32, ref_val.shape, 0)
    v = jnp.sum(jnp.where(rr == row, ref_val, 0.0), axis=0, keepdims=True)
    ll = lax.broadcasted_iota(jnp.int32, v.shape, 1)
    return jnp.sum(jnp.where(ll == 0, v, 0.0), axis=1, keepdims=True)


def _lane_col(blk, lane_idx):
    ll = lax.broadcasted_iota(jnp.int32, blk.shape, 1)
    return jnp.sum(jnp.where(ll == lane_idx, blk, 0.0), axis=1, keepdims=True)


WY_HEADS = 2
WY_UNROLL = 4
WY_UNROLL_BWD = 4


def dn_wy_fwd(qkv, p, ba_col, alog_b, dtb_b, nheads):
    t = qkv.shape[0]
    tm = _tile(t, 512, CHUNK * WY_UNROLL)
    nc = tm // CHUNK
    hb = WY_HEADS
    hp = alog_b.shape[0]
    w_ = hb * LANES

    def body(qkv_ref, ba_ref, al_ref, dt_ref, u_ref, w_ref, qg_ref, kd_ref, qk_ref, e_ref, ti_ref):
        hblk = pl.program_id(1)
        alv, dtv = al_ref[...], dt_ref[...]

        def trip(cj, carry):
            units = [(cj * WY_UNROLL + cu, hl) for cu in range(WY_UNROLL) for hl in range(hb)]
            args = [[] for _ in range(7)]
            for ci, hl in units:
                rows = pl.ds(pl.multiple_of(ci * CHUNK, CHUNK), CHUNK)
                ba = ba_ref[rows, :]
                h = hblk * hb + hl
                for lst, val in zip(args, (qkv_ref[rows, hl * 384:hl * 384 + 128],
                                           qkv_ref[rows, hl * 384 + 128:hl * 384 + 256],
                                           qkv_ref[rows, hl * 384 + 256:hl * 384 + 384],
                                           _lane_col(ba, h), _lane_col(ba, nheads + h),
                                           _head_pick(alv, h), _head_pick(dtv, h))):
                    lst.append(val)
            outs = _wy(*args)
            for n, (ci, hl) in enumerate(units):
                rows = pl.ds(pl.multiple_of(ci * CHUNK, CHUNK), CHUNK)
                u, w, qg, kd, qk, e, ti = [o[n] for o in outs]
                sl = slice(hl * LANES, (hl + 1) * LANES)
                u_ref[rows, sl] = u
                w_ref[rows, sl] = w.astype(BF16)
                qg_ref[rows, sl] = qg.astype(BF16)
                kd_ref[rows, sl] = kd.astype(BF16)
                qk_ref[rows, hl * LANES:hl * LANES + CHUNK] = qk.astype(BF16)
                qk_ref[rows, hl * LANES + CHUNK:(hl + 1) * LANES] = jnp.zeros((CHUNK, LANES - CHUNK), BF16)
                e_ref[ci, :, sl] = jnp.broadcast_to(e, (1, LANES))
                ti_ref[rows, hl * LANES:hl * LANES + CHUNK] = ti
                ti_ref[rows, hl * LANES + CHUNK:(hl + 1) * LANES] = jnp.zeros((CHUNK, LANES - CHUNK), F32)
            return carry

        lax.fori_loop(0, nc // WY_UNROLL, trip, 0)

    bc = ba_col // LANES
    blk = pl.BlockSpec((tm, w_), lambda i, h: (i, h))
    tab = pl.BlockSpec((hp, LANES), lambda i, h: (0, 0))
    wide = lambda dt: jax.ShapeDtypeStruct((t, nheads * LANES), dt)
    return pl.pallas_call(
        body, name="dn_wy_fwd", grid=(t // tm, nheads // hb),
        in_specs=[pl.BlockSpec((tm, hb * 384), lambda i, h: (i, h)), pl.BlockSpec((tm, LANES), lambda i, h: (i, bc)),
                  tab, tab],
        out_specs=[blk] * 5 + [pl.BlockSpec((nc, 1, w_), lambda i, h: (i, 0, h)), blk],
        out_shape=[wide(F32), wide(BF16), wide(BF16), wide(BF16), wide(BF16),
                   jax.ShapeDtypeStruct((t // CHUNK, 1, nheads * LANES), F32), wide(F32)],
        compiler_params=_cp(("parallel", "parallel")),
    )(qkv, p, alog_b, dtb_b)


def dn_seq_fwd(u, w, qg, kd, qk, e, p, z_col, ng, nheads, hb):
    t = u.shape[0]
    tm = _tile(t, 512, CHUNK)
    nc = tm // CHUNK
    w_ = hb * LANES

    def body(u_ref, w_ref, qg_ref, kd_ref, qk_ref, e_ref, z_ref, ng_ref, y_ref, ss_ref, s_scr):
        i, hblk = pl.program_id(0), pl.program_id(1)
        for hl in range(hb):
            @pl.when(i == 0)
            def _():
                s_scr[hblk * hb + hl] = jnp.zeros((LANES, LANES), F32)
        ngv = ng_ref[...]

        def chunk(ci, carry):
            rows = pl.ds(pl.multiple_of(ci * CHUNK, CHUNK), CHUNK)
            ev = e_ref[ci]
            sls = [slice(hl * LANES, (hl + 1) * LANES) for hl in range(hb)]
            s = [s_scr[hblk * hb + hl] for hl in range(hb)]
            for hl in range(hb):
                ss_ref[ci, sls[hl], :] = s[hl]
            y, sn = _seq([u_ref[rows, sl] for sl in sls], [w_ref[rows, sl].astype(F32) for sl in sls],
                         [qg_ref[rows, sl].astype(F32) for sl in sls], [kd_ref[rows, sl].astype(F32) for sl in sls],
                         [qk_ref[rows, sl][:, :CHUNK].astype(F32) for sl in sls], [ev[:, sl] for sl in sls],
                         [z_ref[rows, sl] for sl in sls], s, ngv)
            for hl in range(hb):
                y_ref[rows, sls[hl]] = y[hl]
                s_scr[hblk * hb + hl] = sn[hl]
            return carry

        lax.fori_loop(0, nc, chunk, 0)

    zc = z_col // w_
    blk = pl.BlockSpec((tm, w_), lambda i, h: (i, h))
    return pl.pallas_call(
        body, name="dn_seq_fwd", grid=(t // tm, nheads // hb),
        in_specs=[blk] * 5 + [pl.BlockSpec((nc, 1, w_), lambda i, h: (i, 0, h)),
                              pl.BlockSpec((tm, w_), lambda i, h: (i, zc + h)),
                              pl.BlockSpec((1, LANES), lambda i, h: (0, 0))],
        out_specs=[blk, pl.BlockSpec((nc, w_, LANES), lambda i, h: (i, h, 0))],
        out_shape=[jax.ShapeDtypeStruct((t, nheads * LANES), F32),
                   jax.ShapeDtypeStruct((t // CHUNK, nheads * LANES, LANES), F32)],
        scratch_shapes=[pltpu.VMEM((nheads, LANES, LANES), F32)],
        compiler_params=_cp(("arbitrary", "arbitrary")),
    )(u, w, qg, kd, qk, e, p, ng)


def dn_seq_bwd(u, w, qg, kd, qk, e, p, z_col, ng, ss, dy, nheads, hb):
    t = u.shape[0]
    tm = _tile(t, 2048 // hb, CHUNK)
    nc = tm // CHUNK
    nblk = t // tm
    w_ = hb * LANES

    def body(u_ref, w_ref, qg_ref, kd_ref, qk_ref, e_ref, z_ref, ng_ref, ss_ref, dy_ref,
             du_ref, dw_ref, dqg_ref, dkd_ref, dqk_ref, de_ref, dz_ref, dng_ref, ds_scr):
        i, hblk = pl.program_id(0), pl.program_id(1)

        @pl.when((i == 0) & (hblk == 0))
        def _():
            dng_ref[...] = jnp.zeros_like(dng_ref)

        for hl in range(hb):
            @pl.when(i == 0)
            def _():
                ds_scr[hblk * hb + hl] = jnp.zeros((LANES, LANES), F32)
        ngv = ng_ref[...]

        def chunk(cj, carry):
            ci = nc - 1 - cj
            rows = pl.ds(pl.multiple_of(ci * CHUNK, CHUNK), CHUNK)
            ev = e_ref[ci]
            sls = [slice(hl * LANES, (hl + 1) * LANES) for hl in range(hb)]
            _, vj = jax.vjp(_seq, [u_ref[rows, sl] for sl in sls], [w_ref[rows, sl].astype(F32) for sl in sls],
                            [qg_ref[rows, sl].astype(F32) for sl in sls], [kd_ref[rows, sl].astype(F32) for sl in sls],
                            [qk_ref[rows, sl][:, :CHUNK].astype(F32) for sl in sls], [ev[:, sl] for sl in sls],
                            [z_ref[rows, sl] for sl in sls], [ss_ref[ci, sl, :] for sl in sls], ngv)
            du, dw, dqg, dkd, dqk, de, dz, dsp, dng = vj(([dy_ref[rows, sl] for sl in sls],
                                                          [ds_scr[hblk * hb + hl] for hl in range(hb)]))
            for hl, sl in enumerate(sls):
                du_ref[rows, sl] = du[hl]
                dw_ref[rows, sl] = dw[hl]
                dqg_ref[rows, sl] = dqg[hl]
                dkd_ref[rows, sl] = dkd[hl]
                dqk_ref[rows, hl * LANES:hl * LANES + CHUNK] = dqk[hl]
                dqk_ref[rows, hl * LANES + CHUNK:(hl + 1) * LANES] = jnp.zeros((CHUNK, LANES - CHUNK), F32)
                de_ref[ci, :, sl] = de[hl]
                dz_ref[rows, sl] = dz[hl]
                ds_scr[hblk * hb + hl] = dsp[hl]
            dng_ref[...] += dng
            return carry

        lax.fori_loop(0, nc, chunk, 0)

    zc = z_col // w_
    rv = lambda i: nblk - 1 - i
    blk = pl.BlockSpec((tm, w_), lambda i, h: (rv(i), h))
    eblk = pl.BlockSpec((nc, 1, w_), lambda i, h: (rv(i), 0, h))
    one = pl.BlockSpec((1, LANES), lambda i, h: (0, 0))
    wide = jax.ShapeDtypeStruct((t, nheads * LANES), F32)
    return pl.pallas_call(
        body, name="dn_seq_bwd", grid=(nblk, nheads // hb),
        in_specs=[blk] * 5 + [eblk, pl.BlockSpec((tm, w_), lambda i, h: (rv(i), zc + h)), one,
                              pl.BlockSpec((nc, w_, LANES), lambda i, h: (rv(i), h, 0)), blk],
        out_specs=[blk] * 5 + [eblk, blk, one],
        out_shape=[wide] * 5 + [jax.ShapeDtypeStruct((t // CHUNK, 1, nheads * LANES), F32), wide,
                                jax.ShapeDtypeStruct((1, LANES), F32)],
        scratch_shapes=[pltpu.VMEM((nheads, LANES, LANES), F32)],
        compiler_params=_cp(("arbitrary", "arbitrary")),
    )(u, w, qg, kd, qk, e, p, ng, ss, dy)


def dn_wy_bwd(qkv, p, ba_col, alog_b, dtb_b, ti, du, dw, dqg, dkd, dqk, de, nheads):
    t = qkv.shape[0]
    tm = _tile(t, 512, CHUNK * WY_UNROLL_BWD)
    nc = tm // CHUNK
    hb = WY_HEADS
    hp = alog_b.shape[0]
    w_ = hb * LANES

    def body(qkv_ref, ba_ref, al_ref, dt_ref, du_ref, dw_ref, dqg_ref, dkd_ref, dqk_ref, de_ref, ti_ref,
             dqkv_ref, dba_ref, dal_ref, ddt_ref):
        i, hblk = pl.program_id(0), pl.program_id(1)

        @pl.when((i == 0) & (hblk == 0))
        def _():
            dal_ref[...] = jnp.zeros_like(dal_ref)
            ddt_ref[...] = jnp.zeros_like(ddt_ref)

        @pl.when(hblk == 0)
        def _():
            dba_ref[...] = jnp.zeros_like(dba_ref)

        alv, dtv = al_ref[...], dt_ref[...]
        lane = lax.broadcasted_iota(jnp.int32, (CHUNK, LANES), 1)
        rowp = lax.broadcasted_iota(jnp.int32, (hp, LANES), 0)

        def trip(cj, carry):
            units = [(cj * WY_UNROLL_BWD + cu, hl) for cu in range(WY_UNROLL_BWD) for hl in range(hb)]
            args = [[] for _ in range(8)]
            cts = [[] for _ in range(6)]
            for ci, hl in units:
                rows = pl.ds(pl.multiple_of(ci * CHUNK, CHUNK), CHUNK)
                ba = ba_ref[rows, :]
                h = hblk * hb + hl
                sl = slice(hl * LANES, (hl + 1) * LANES)
                for lst, val in zip(args, (qkv_ref[rows, hl * 384:hl * 384 + 128],
                                           qkv_ref[rows, hl * 384 + 128:hl * 384 + 256],
                                           qkv_ref[rows, hl * 384 + 256:hl * 384 + 384],
                                           _lane_col(ba, h), _lane_col(ba, nheads + h),
                                           _head_pick(alv, h), _head_pick(dtv, h), ti_ref[rows, sl][:, :CHUNK])):
                    lst.append(val)
                de11 = jnp.sum(de_ref[ci][:, sl], axis=1, keepdims=True)
                for lst, val in zip(cts, (du_ref[rows, sl], dw_ref[rows, sl], dqg_ref[rows, sl], dkd_ref[rows, sl],
                                          dqk_ref[rows, sl][:, :CHUNK], de11)):
                    lst.append(val)
            _, vj = jax.vjp(_wy, *args)
            grads = vj(tuple(cts))
            for n, (ci, hl) in enumerate(units):
                rows = pl.ds(pl.multiple_of(ci * CHUNK, CHUNK), CHUNK)
                h = hblk * hb + hl
                dq, dk, dv, dbc, dac, dal, ddt = [g_[n] for g_ in grads[:7]]
                dqkv_ref[rows, hl * 384:hl * 384 + 128] = dq
                dqkv_ref[rows, hl * 384 + 128:hl * 384 + 256] = dk
                dqkv_ref[rows, hl * 384 + 256:hl * 384 + 384] = dv
                dba_ref[rows, :] += jnp.where(lane == h, dbc, 0.0) + jnp.where(lane == nheads + h, dac, 0.0)
                dal_ref[...] += jnp.where(rowp == h, dal, 0.0)
                ddt_ref[...] += jnp.where(rowp == h, ddt, 0.0)
            return carry

        lax.fori_loop(0, nc // WY_UNROLL_BWD, trip, 0)

    bc = ba_col // LANES
    blk = pl.BlockSpec((tm, w_), lambda i, h: (i, h))
    tab = pl.BlockSpec((hp, LANES), lambda i, h: (0, 0))
    return pl.pallas_call(
        body, name="dn_wy_bwd", grid=(t // tm, nheads // hb),
        in_specs=[pl.BlockSpec((tm, hb * 384), lambda i, h: (i, h)), pl.BlockSpec((tm, LANES), lambda i, h: (i, bc)),
                  tab, tab] + [blk] * 5 + [pl.BlockSpec((nc, 1, w_), lambda i, h: (i, 0, h)), blk],
        out_specs=[pl.BlockSpec((tm, hb * 384), lambda i, h: (i, h)), pl.BlockSpec((tm, LANES), lambda i, h: (i, 0)),
                   tab, tab],
        out_shape=[jax.ShapeDtypeStruct((t, nheads * 384), F32), jax.ShapeDtypeStruct((t, LANES), F32),
                   jax.ShapeDtypeStruct((hp, LANES), F32), jax.ShapeDtypeStruct((hp, LANES), F32)],
        compiler_params=_cp(("arbitrary", "arbitrary")),
    )(qkv, p, alog_b, dtb_b, du, dw, dqg, dkd, dqk, de, ti)


def sg_fwd(p, col, sg, lg, lb, w, bias_b):
    t = p.shape[0]
    ng_ = sg // LANES
    tm = _tile(t, 256, LANES)
    cb = col // sg

    def body(u_ref, v_ref, g_ref, lg_ref, lb_ref, w_ref, b_ref, y_ref):
        for n in range(tm // LANES):
            rs = slice(n * LANES, (n + 1) * LANES)
            for gi in range(ng_):
                sl = slice(gi * LANES, (gi + 1) * LANES)
                y_ref[rs, sl] = _sg_block(u_ref[rs, sl], v_ref[rs, sl], g_ref[rs, sl], lg_ref[:, sl], lb_ref[:, sl],
                                          w_ref[gi], b_ref[gi])

    vec = pl.BlockSpec((1, sg), lambda i: (0, 0))
    full = pl.BlockSpec((ng_, LANES, LANES), lambda i: (0, 0, 0))
    return pl.pallas_call(
        body, name="sg_fwd", grid=(t // tm,),
        in_specs=[pl.BlockSpec((tm, sg), lambda i: (i, cb)), pl.BlockSpec((tm, sg), lambda i: (i, cb + 1)),
                  pl.BlockSpec((tm, sg), lambda i: (i, cb + 2)), vec, vec, full, full],
        out_specs=pl.BlockSpec((tm, sg), lambda i: (i, 0)),
        out_shape=jax.ShapeDtypeStruct((t, sg), F32),
        compiler_params=_cp(("parallel",)),
    )(p, p, p, lg, lb, w, bias_b)


def sg_bwd(p, col, sg, lg, lb, w, bias_b, dy):
    t = p.shape[0]
    ng_ = sg // LANES
    tm = _tile(t, 256, LANES)
    cb = col // sg

    def body(u_ref, v_ref, g_ref, lg_ref, lb_ref, w_ref, b_ref, dy_ref, d_ref, dlg_ref, dlb_ref, dw_ref, db_ref):
        @pl.when(pl.program_id(0) == 0)
        def _():
            for r in (dlg_ref, dlb_ref, dw_ref, db_ref):
                r[...] = jnp.zeros_like(r)

        for n in range(tm // LANES):
            rs = slice(n * LANES, (n + 1) * LANES)
            for gi in range(ng_):
                sl = slice(gi * LANES, (gi + 1) * LANES)
                _, vj = jax.vjp(_sg_block, u_ref[rs, sl], v_ref[rs, sl], g_ref[rs, sl], lg_ref[:, sl], lb_ref[:, sl],
                                w_ref[gi], b_ref[gi])
                du, dv, dg, dlg, dlb, dw, db = vj(dy_ref[rs, sl])
                d_ref[rs, gi * LANES:(gi + 1) * LANES] = du.astype(BF16)
                d_ref[rs, sg + gi * LANES:sg + (gi + 1) * LANES] = dv.astype(BF16)
                d_ref[rs, 2 * sg + gi * LANES:2 * sg + (gi + 1) * LANES] = dg.astype(BF16)
                dlg_ref[:, sl] += dlg
                dlb_ref[:, sl] += dlb
                dw_ref[gi] += dw
                db_ref[gi] += jnp.broadcast_to(jnp.sum(db, axis=1, keepdims=True), (LANES, LANES))

    vec = pl.BlockSpec((1, sg), lambda i: (0, 0))
    full = pl.BlockSpec((ng_, LANES, LANES), lambda i: (0, 0, 0))
    return pl.pallas_call(
        body, name="sg_bwd", grid=(t // tm,),
        in_specs=[pl.BlockSpec((tm, sg), lambda i: (i, cb)), pl.BlockSpec((tm, sg), lambda i: (i, cb + 1)),
                  pl.BlockSpec((tm, sg), lambda i: (i, cb + 2)), vec, vec, full, full,
                  pl.BlockSpec((tm, sg), lambda i: (i, 0))],
        out_specs=[pl.BlockSpec((tm, 3 * sg), lambda i: (i, 0)), vec, vec, full, full],
        out_shape=[jax.ShapeDtypeStruct((t, 3 * sg), BF16), jax.ShapeDtypeStruct((1, sg), F32),
                   jax.ShapeDtypeStruct((1, sg), F32), jax.ShapeDtypeStruct((ng_, LANES, LANES), F32),
                   jax.ShapeDtypeStruct((ng_, LANES, LANES), F32)],
        compiler_params=_cp(("arbitrary",)),
    )(p, p, p, lg, lb, w, bias_b, dy)


def out_proj(x, y_dn, y_sg, y_cv, wo, gate):
    t, d = x.shape
    dn, sg, cv = y_dn.shape[1], y_sg.shape[1], y_cv.shape[1]
    dmix = dn + sg + cv
    tm = _tile(t, 256, LANES)

    def body(x_ref, a_ref, b_ref, c_ref, w_ref, g_ref, xn_ref, y_ref, yt_ref):
        a, b, c = a_ref[...], b_ref[...], c_ref[...]
        y = (jnp.dot(a.astype(BF16), w_ref[0:dn, :], preferred_element_type=F32)
             + jnp.dot(b.astype(BF16), w_ref[dn:dn + sg, :], preferred_element_type=F32)
             + jnp.dot(c.astype(BF16), w_ref[dn + sg:, :], preferred_element_type=F32))
        y_ref[...] = y
        xn_ref[...] = x_ref[...] + g_ref[...] * y
        yt_ref[0:dn, :] = a.T.astype(BF16)
        yt_ref[dn:dn + sg, :] = b.T.astype(BF16)
        yt_ref[dn + sg:, :] = c.T.astype(BF16)

    row = lambda w_: pl.BlockSpec((tm, w_), lambda i: (i, 0))
    return pl.pallas_call(
        body, name="out_proj", grid=(t // tm,),
        in_specs=[row(d), row(dn), row(sg), row(cv), pl.BlockSpec((dmix, d), lambda i: (0, 0)),
                  pl.BlockSpec((1, d), lambda i: (0, 0))],
        out_specs=[row(d), row(d), pl.BlockSpec((dmix, tm), lambda i: (0, i))],
        out_shape=[jax.ShapeDtypeStruct((t, d), F32), jax.ShapeDtypeStruct((t, d), F32),
                   jax.ShapeDtypeStruct((dmix, t), BF16)],
        compiler_params=_cp(("parallel",)),
    )(x, y_dn, y_sg, y_cv, wo, gate)


def out_proj_bwd(dxn, y, gate, wo, dn, sg, cv):
    t, d = dxn.shape
    dmix = dn + sg + cv
    tm = _tile(t, 256, LANES)

    def body(dx_ref, y_ref, g_ref, w_ref, da_ref, db_ref, dc_ref, dyb_ref, dg_ref):
        @pl.when(pl.program_id(0) == 0)
        def _():
            dg_ref[...] = jnp.zeros_like(dg_ref)
        dx = dx_ref[...]
        dg_ref[...] += jnp.sum(dx * y_ref[...], axis=0, keepdims=True)
        dyb = (dx * g_ref[...]).astype(BF16)
        dyb_ref[...] = dyb
        dcat = lax.dot_general(dyb, w_ref[...], (((1,), (1,)), ((), ())), preferred_element_type=F32)
        da_ref[...] = dcat[:, 0:dn]
        db_ref[...] = dcat[:, dn:dn + sg]
        dc_ref[...] = dcat[:, dn + sg:]

    row = lambda w_: pl.BlockSpec((tm, w_), lambda i: (i, 0))
    vec = pl.BlockSpec((1, d), lambda i: (0, 0))
    return pl.pallas_call(
        body, name="out_proj_bwd", grid=(t // tm,),
        in_specs=[row(d), row(d), vec, pl.BlockSpec((dmix, d), lambda i: (0, 0))],
        out_specs=[row(dn), row(sg), row(cv), row(d), vec],
        out_shape=[jax.ShapeDtypeStruct((t, dn), F32), jax.ShapeDtypeStruct((t, sg), F32),
                   jax.ShapeDtypeStruct((t, cv), F32), jax.ShapeDtypeStruct((t, d), BF16),
                   jax.ShapeDtypeStruct((1, d), F32)],
        compiler_params=_cp(("arbitrary",)),
    )(dxn, y, gate, wo)


def matmul_acc(name, at, b):
    m, t = at.shape
    n = b.shape[1]
    tm, tn, tk = _tile(m, 1024, LANES), _tile(n, 2432, LANES), _tile(t, 1024, LANES)

    def body(a_ref, b_ref, o_ref):
        @pl.when(pl.program_id(2) == 0)
        def _():
            o_ref[...] = jnp.zeros_like(o_ref)
        o_ref[...] += jnp.dot(a_ref[...], b_ref[...], preferred_element_type=F32)

    return pl.pallas_call(
        body, name=name, grid=(m // tm, n // tn, t // tk),
        in_specs=[pl.BlockSpec((tm, tk), lambda i, j, k: (i, k)), pl.BlockSpec((tk, tn), lambda i, j, k: (k, j))],
        out_specs=pl.BlockSpec((tm, tn), lambda i, j, k: (i, j)),
        out_shape=jax.ShapeDtypeStruct((m, n), F32),
        compiler_params=_cp(("parallel", "parallel", "arbitrary")),
    )(at, b)


def in_proj_bwd(dp, wp, x, dxn, shift, scale, ng):
    t, d = x.shape
    npc = wp.shape[1]
    tm, tk = _tile(t, 512, LANES), _tile(npc, 2432, LANES)

    def mm_body(dp_ref, w_ref, dh_ref):
        @pl.when(pl.program_id(1) == 0)
        def _():
            dh_ref[...] = jnp.zeros_like(dh_ref)
        dh_ref[...] += lax.dot_general(dp_ref[...], w_ref[...], (((1,), (1,)), ((), ())), preferred_element_type=F32)

    dh = pl.pallas_call(
        mm_body, name="in_proj_bwd", grid=(t // tm, npc // tk),
        in_specs=[pl.BlockSpec((tm, tk), lambda i, k: (i, k)), pl.BlockSpec((d, tk), lambda i, k: (0, k))],
        out_specs=pl.BlockSpec((tm, d), lambda i, k: (i, 0)), out_shape=jax.ShapeDtypeStruct((t, d), F32),
        compiler_params=_cp(("parallel", "arbitrary")),
    )(dp, wp)

    tr = _tile(t, 256, 8)

    def norm_body(dh_ref, x_ref, dxn_ref, sh_ref, sc_ref, g_ref, dx_ref, dg_ref, dsc_ref, dsh_ref):
        @pl.when(pl.program_id(0) == 0)
        def _():
            for r in (dg_ref, dsc_ref, dsh_ref):
                r[...] = jnp.zeros_like(r)
        _, vj = jax.vjp(_modnorm, x_ref[...], g_ref[...], sc_ref[...], sh_ref[...])
        dx, dg, dsc, dsh = vj(dh_ref[...])
        dx_ref[...] = dxn_ref[...] + dx
        dg_ref[...] += dg
        dsc_ref[...] += dsc
        dsh_ref[...] += dsh

    vec = pl.BlockSpec((1, d), lambda i: (0, 0))
    row = pl.BlockSpec((tr, d), lambda i: (i, 0))
    return pl.pallas_call(
        norm_body, name="modnorm_bwd", grid=(t // tr,), in_specs=[row, row, row, vec, vec, vec],
        out_specs=[row, vec, vec, vec],
        out_shape=[jax.ShapeDtypeStruct((t, d), F32)] + [jax.ShapeDtypeStruct((1, d), F32)] * 3,
        compiler_params=_cp(("arbitrary",)),
    )(dh, x, dxn, shift, scale, ng)


def loss_head(x, tgt, fg):
    t, d = x.shape
    tm = _tile(t, 512, 8)

    def body(x_ref, t_ref, g_ref, l_ref, dx_ref, dg_ref):
        @pl.when(pl.program_id(0) == 0)
        def _():
            l_ref[...] = jnp.zeros_like(l_ref)
            dg_ref[...] = jnp.zeros_like(dg_ref)
        y, vj = jax.vjp(_rmsnorm, x_ref[...], g_ref[...])
        err = y - t_ref[...]
        part = 0.5 * jnp.sum(jnp.sum(err * err, axis=1, keepdims=True), axis=0, keepdims=True) / d
        l_ref[...] += jnp.broadcast_to(part, l_ref.shape)
        dx, dg = vj(err / d)
        dx_ref[...] = dx
        dg_ref[...] += dg

    row = pl.BlockSpec((tm, d), lambda i: (i, 0))
    vec = pl.BlockSpec((1, d), lambda i: (0, 0))
    return pl.pallas_call(
        body, name="loss_head", grid=(t // tm,), in_specs=[row, row, vec],
        out_specs=[pl.BlockSpec((1, LANES), lambda i: (0, 0)), row, vec],
        out_shape=[jax.ShapeDtypeStruct((1, LANES), F32), jax.ShapeDtypeStruct((t, d), F32),
                   jax.ShapeDtypeStruct((1, d), F32)],
        compiler_params=_cp(("arbitrary",)),
    )(x, tgt, fg)


def adamw(name, w, g, m, v):
    r, c = w.shape
    tr = _tile(r, 256, 8) if r % 8 == 0 else r

    def body(w_ref, g_ref, m_ref, v_ref, d_ref, mo_ref, vo_ref):
        d_ref[...], mo_ref[...], vo_ref[...] = _adamw_math(w_ref[...], g_ref[...], m_ref[...], v_ref[...])

    blk = pl.BlockSpec((tr, c), lambda i: (i, 0))
    return pl.pallas_call(
        body, name=name, grid=(r // tr,), in_specs=[blk] * 4, out_specs=[blk] * 3,
        out_shape=[jax.ShapeDtypeStruct((r, c), F32)] * 3, compiler_params=_cp(("parallel",)),
    )(w, g, m, v)


def ada_fwd(c_all, w_ada, b_loc):
    nl, d, cols = w_ada.shape
    nb = c_all.shape[0]
    tn = _tile(cols, 512, LANES)

    def body(c_ref, w_ref, b_ref, o_ref):
        ca = _silu(c_ref[...]).astype(BF16)
        o_ref[0] = jnp.dot(ca, w_ref[0].astype(BF16), preferred_element_type=F32) + b_ref[0]

    return pl.pallas_call(
        body, name="ada_fwd", grid=(nl, cols // tn),
        in_specs=[pl.BlockSpec((nb, d), lambda l, j: (0, 0)), pl.BlockSpec((1, d, tn), lambda l, j: (l, 0, j)),
                  pl.BlockSpec((1, 1, tn), lambda l, j: (l, 0, j))],
        out_specs=pl.BlockSpec((1, nb, tn), lambda l, j: (l, 0, j)),
        out_shape=jax.ShapeDtypeStruct((nl, nb, cols), F32),
        compiler_params=_cp(("parallel", "parallel")),
    )(c_all, w_ada, b_loc)


def ada_bwd(c_all_t, dmod_loc, w, m, v):
    nl, d, cols = w.shape
    nb = c_all_t.shape[1]
    tr = _tile(d, 256, 8)

    def body(c_ref, dm_ref, w_ref, m_ref, v_ref, g_ref, d_ref, mo_ref, vo_ref):
        ca = _silu(c_ref[...])
        dm = dm_ref[0]
        g = _lane_col(ca, 0) * dm[0:1, :]
        for b in range(1, nb):
            g = g + _lane_col(ca, b) * dm[b:b + 1, :]
        g_ref[0] = g
        d_ref[0], mo_ref[0], vo_ref[0] = _adamw_math(w_ref[0], g, m_ref[0], v_ref[0])

    blk = pl.BlockSpec((1, tr, cols), lambda l, i: (l, i, 0))
    return pl.pallas_call(
        body, name="ada_bwd", grid=(nl, d // tr),
        in_specs=[pl.BlockSpec((tr, nb), lambda l, i: (i, 0)), pl.BlockSpec((1, nb, cols), lambda l, i: (l, 0, 0)),
                  blk, blk, blk],
        out_specs=[blk] * 4, out_shape=[jax.ShapeDtypeStruct((nl, d, cols), F32)] * 4,
        compiler_params=_cp(("parallel", "parallel")),
    )(c_all_t, dmod_loc, w, m, v)


def sum8(g):
    _, r, c = g.shape
    tr = _tile(r, 256, 8)

    def body(g_ref, o_ref):
        acc = g_ref[0]
        for k in range(1, NDEV):
            acc = acc + g_ref[k]
        o_ref[...] = acc

    return pl.pallas_call(
        body, name="sum8", grid=(r // tr,), in_specs=[pl.BlockSpec((NDEV, tr, c), lambda i: (0, i, 0))],
        out_specs=pl.BlockSpec((tr, c), lambda i: (i, 0)), out_shape=jax.ShapeDtypeStruct((r, c), F32),
        compiler_params=_cp(("parallel",)),
    )(g)


def pair_sum(cflag, g, r1):
    _, ns, r, c = g.shape
    tr = _tile(r, 256, 8)

    def body(cf_ref, g0_ref, g1_ref, r_ref, ob_ref):
        keep = jnp.where(cf_ref[0:1, 0:1] == 0.0, g0_ref[0], g1_ref[0])
        ob_ref[...] = (keep + r_ref[...]).astype(BF16)

    blk = pl.BlockSpec((1, tr, c), lambda s, i: (s, i, 0))
    return pl.pallas_call(
        body, name="pair_sum", grid=(ns, r // tr),
        in_specs=[pl.BlockSpec((1, LANES), lambda s, i: (0, 0)), pl.BlockSpec((1, 1, tr, c), lambda s, i: (0, s, i, 0)),
                  pl.BlockSpec((1, 1, tr, c), lambda s, i: (1, s, i, 0)), blk], out_specs=blk,
        out_shape=jax.ShapeDtypeStruct((ns, r, c), BF16), compiler_params=_cp(("parallel", "parallel")),
    )(cflag, g, g, r1)


def chip_sum(own, r1, r2):
    r, c = own.shape
    tr = _tile(r, 256, 8)

    def body(g_ref, r1_ref, r2_ref, o_ref):
        acc = g_ref[...] + r1_ref[...]
        for k in range(NCHIP - 1):
            acc = acc + r2_ref[k].astype(F32)
        o_ref[...] = acc

    blk = pl.BlockSpec((tr, c), lambda i: (i, 0))
    return pl.pallas_call(
        body, name="chip_sum", grid=(r // tr,),
        in_specs=[blk, blk, pl.BlockSpec((NCHIP - 1, tr, c), lambda i: (0, i, 0))], out_specs=blk,
        out_shape=jax.ShapeDtypeStruct((r, c), F32), compiler_params=_cp(("parallel",)),
    )(own, r1, r2)


def adamw_halves(name, cflag, w, own, recv, m, v):
    _, r, c = w.shape
    tr = _tile(r, 256, 8)

    def body(cf_ref, w_ref, a_ref, b_ref, m_ref, v_ref, g_ref, d_ref, mo_ref, vo_ref):
        is_own = cf_ref[0:1, 0:1] == pl.program_id(0).astype(F32)
        g = jnp.where(is_own, a_ref[...], b_ref[...])
        g_ref[0] = g
        d_ref[0], mo_ref[0], vo_ref[0] = _adamw_math(w_ref[0], g, m_ref[0], v_ref[0])

    blk = pl.BlockSpec((1, tr, c), lambda h, i: (h, i, 0))
    hlf = pl.BlockSpec((tr, c), lambda h, i: (i, 0))
    return pl.pallas_call(
        body, name=name, grid=(2, r // tr),
        in_specs=[pl.BlockSpec((1, LANES), lambda h, i: (0, 0)), blk, hlf, hlf, blk, blk], out_specs=[blk] * 4,
        out_shape=[jax.ShapeDtypeStruct(w.shape, F32)] * 4, compiler_params=_cp(("parallel", "parallel")),
    )(cflag, w, own, recv, m, v)


def _me():
    return lax.axis_index("x"), lax.axis_index("y"), lax.axis_index("c")


_FLIPS = ((1, 0), (0, 1), (1, 1))


def all_gather8(v):
    m_per, n = v.shape

    def body(x_ref, out_ref, send_sems, recv_sems, local_sem):
        x, y, c = _me()
        me, sibling = (x, y, c), (x, y, 1 - c)
        chips = [(x ^ fx, y ^ fy) for fx, fy in _FLIPS]

        def rows(px, py, pc):
            return out_ref.at[pl.ds((4 * px + 2 * py + pc) * m_per, m_per), :]

        def copy(k, block, to, src=None):
            return pltpu.make_async_remote_copy(
                src_ref=rows(*block) if src is None else src, dst_ref=rows(*block),
                send_sem=send_sems.at[k], recv_sem=recv_sems.at[k], device_id=to, device_id_type=MESH)

        mine = pltpu.make_async_copy(x_ref, rows(*me), local_sem)
        mine.start()
        first = [copy(0, me, sibling, src=x_ref)]
        first += [copy(1 + j, me, (*chip, c), src=x_ref) for j, chip in enumerate(chips)]
        for cp in first:
            cp.start()
        passed = [copy(4 + j, (*chip, c), sibling) for j, chip in enumerate(chips)]
        for j, chip in enumerate(chips):
            copy(1 + j, (*chip, c), me).wait_recv()
            passed[j].start()
        copy(0, sibling, me).wait_recv()
        for j, chip in enumerate(chips):
            copy(4 + j, (*chip, 1 - c), me).wait_recv()
        for cp in first + passed:
            cp.wait_send()
        mine.wait()

    return pl.pallas_call(
        body, name="all_gather8", out_shape=jax.ShapeDtypeStruct((NDEV * m_per, n), v.dtype),
        in_specs=[pl.BlockSpec(memory_space=pltpu.VMEM)], out_specs=pl.BlockSpec(memory_space=pltpu.VMEM),
        scratch_shapes=[pltpu.SemaphoreType.DMA((7,)), pltpu.SemaphoreType.DMA((7,)), pltpu.SemaphoreType.DMA],
        compiler_params=pltpu.CompilerParams(vmem_limit_bytes=VMEM_LIMIT),
    )(v)


def gather_weights(ws):
    na = len(ws)

    def body(*refs):
        srcs, outs = refs[:na], refs[na:2 * na]
        send_sems, recv_sems = refs[2 * na:]
        x, y, c = _me()
        chip = 2 * x + y
        sibling = (x, y, 1 - c)
        peers = [(x ^ fx, y ^ fy) for fx, fy in _FLIPS]

        def ici(a, j, half_of, to):
            return pltpu.make_async_remote_copy(
                src_ref=srcs[a].at[c], dst_ref=outs[a].at[half_of, c], send_sem=send_sems.at[a * 6 + j],
                recv_sem=recv_sems.at[a * 6 + j], device_id=to, device_id_type=MESH)

        def d2d(a, j, slot, half):
            return pltpu.make_async_remote_copy(
                src_ref=outs[a].at[slot, half], dst_ref=outs[a].at[slot, half], send_sem=send_sems.at[a * 6 + 3 + j],
                recv_sem=recv_sems.at[a * 6 + 3 + j], device_id=sibling, device_id_type=MESH)

        sends = []
        for a in range(na):
            for j, (px, py) in enumerate(peers):
                cp = ici(a, j, chip, (px, py, c))
                cp.start()
                sends.append(cp)
        for a in range(na):
            for j, (px, py) in enumerate(peers):
                ici(a, j, 2 * px + py, (px, py, c)).wait_recv()
                cp = d2d(a, j, 2 * px + py, c)
                cp.start()
                sends.append(cp)
        for a in range(na):
            for j, (px, py) in enumerate(peers):
                d2d(a, j, 2 * px + py, 1 - c).wait_recv()
        for cp in sends:
            cp.wait_send()

    return pl.pallas_call(
        body, name="gather_weights",
        out_shape=[jax.ShapeDtypeStruct((NCHIP,) + w.shape, w.dtype) for w in ws],
        in_specs=[ANY] * na, out_specs=[ANY] * na,
        scratch_shapes=[pltpu.SemaphoreType.DMA((6 * na,)), pltpu.SemaphoreType.DMA((6 * na,))],
    )(*ws)


def sibling_swap(name, gs, other_half=False):
    na = len(gs)

    def body(*refs):
        srcs, outs = refs[:na], refs[na:2 * na]
        send_sems, recv_sems = refs[2 * na:]
        x, y, c = _me()
        cps = [pltpu.make_async_remote_copy(
            src_ref=srcs[a].at[1 - c] if other_half else srcs[a], dst_ref=outs[a], send_sem=send_sems.at[a],
            recv_sem=recv_sems.at[a], device_id=(x, y, 1 - c), device_id_type=MESH) for a in range(na)]
        for cp in cps:
            cp.start()
        for cp in cps:
            cp.wait()

    return pl.pallas_call(
        body, name=name, out_shape=[jax.ShapeDtypeStruct(g.shape[1:] if other_half else g.shape, g.dtype) for g in gs],
        in_specs=[ANY] * na, out_specs=[ANY] * na,
        scratch_shapes=[pltpu.SemaphoreType.DMA((na,)), pltpu.SemaphoreType.DMA((na,))],
    )(*gs)


def chip_exchange(ps):
    na = len(ps)

    def body(*refs):
        srcs, outs = refs[:na], refs[na:2 * na]
        send_sems, recv_sems = refs[2 * na:]
        x, y, c = _me()
        cps = []
        for a in range(na):
            for j, (fx, fy) in enumerate(_FLIPS):
                px, py = x ^ fx, y ^ fy
                cps.append(pltpu.make_async_remote_copy(
                    src_ref=srcs[a].at[2 * px + py], dst_ref=outs[a].at[j], send_sem=send_sems.at[a * 3 + j],
                    recv_sem=recv_sems.at[a * 3 + j], device_id=(px, py, c), device_id_type=MESH))
        for cp in cps:
            cp.start()
        for cp in cps:
            cp.wait()

    return pl.pallas_call(
        body, name="chip_exchange",
        out_shape=[jax.ShapeDtypeStruct((NCHIP - 1,) + p_.shape[1:], p_.dtype) for p_ in ps],
        in_specs=[ANY] * na, out_specs=[ANY] * na,
        scratch_shapes=[pltpu.SemaphoreType.DMA((3 * na,)), pltpu.SemaphoreType.DMA((3 * na,))],
    )(*ps)


class _Cfg:
    def __init__(self, x, a_log, sg_w, cv_ln_g, cv_w, conv_qkv):
        self.t, self.d = x.shape[1], x.shape[2]
        self.nl, self.h = a_log.shape
        self.dn = self.h * LANES
        self.g = sg_w.shape[1]
        self.sg = self.g * LANES
        self.cv = cv_ln_g.shape[1]
        self.kc = cv_w.shape[1]
        self.k4 = conv_qkv.shape[1]
        self.o_z = 3 * self.dn
        self.o_sg = 4 * self.dn
        self.o_cv = self.o_sg + 3 * self.sg
        self.o_ba = self.o_cv + 3 * self.cv
        self.npc = self.o_ba + LANES
        self.d_in = self.o_ba + 2 * self.h
        self.dmix = self.dn + self.sg + self.cv
        self.hb_fwd = _tile(self.h, 8, 1)
        self.hb_bwd = _tile(self.h, 8, 1)


def _runs(cfg):
    dn, h = cfg.dn, cfg.h
    runs = [(part * dn + hd * LANES, hd * 3 * LANES + part * LANES, LANES) for part in range(3) for hd in range(h)]
    return runs + [(3 * dn, 3 * dn, dn), (4 * dn, cfg.o_ba, 2 * h), (4 * dn + 2 * h, 4 * dn, cfg.o_ba - 4 * dn)]


def _assemble_perm(cfg, shards):
    cols = shards[0].shape[-1]
    pieces = []
    for nat, _, wdt in sorted(_runs(cfg), key=lambda r_: r_[1]):
        a = nat
        while a < nat + wdt:
            s = a // cols
            b = min(nat + wdt, (s + 1) * cols)
            pieces.append(shards[s][..., a - s * cols:b - s * cols])
            a = b
    pieces.append(jnp.zeros(shards[0].shape[:-1] + (cfg.npc - cfg.o_ba - 2 * cfg.h,), shards[0].dtype))
    return jnp.concatenate(pieces, axis=-1)


def _natural_pieces(cfg, s, cols):
    lo, hi = s * cols, (s + 1) * cols
    pieces = []
    for nat, perm, wdt in sorted(_runs(cfg)):
        a, b = max(nat, lo), min(nat + wdt, hi)
        if a < b:
            pieces.append((perm + a - nat, perm + b - nat))
    return pieces


def _layer_fwd(cfg, x, mod, lw):
    shift, scale, gate = mod
    p, ht = in_proj(x, shift, scale, lw["norm_g"], lw["wp"])
    qk_post = [_qk_post, _qk_post, _v_post]
    qkv = conv_fwd("dn_pre_fwd", cfg.k4, HALO4, lambda a: a, [(p, 0)], lw["conv_qkv"], qk_post, [], [],
                   3 * cfg.dn, 3 * LANES)
    wy = dn_wy_fwd(qkv, p, cfg.o_ba, lw["alog_b"], lw["dtb_b"], cfg.h)
    y_dn, ss = dn_seq_fwd(*wy[:6], p, cfg.o_z, lw["dn_norm_g"], cfg.h, cfg.hb_fwd)
    y_sg = sg_fwd(p, cfg.o_sg, cfg.sg, lw["sg_ln_g"], lw["sg_ln_b"], lw["sg_w"], lw["sg_bias_b"])
    cv_post = [_cv_post] * (cfg.cv // LANES)
    y_cv = conv_fwd("cv_fwd", cfg.kc, HALO31, _glu, [(p, cfg.o_cv), (p, cfg.o_cv + cfg.cv)], lw["cv_w"], cv_post,
                    [(p, cfg.o_cv + 2 * cfg.cv)], [lw["cv_b"], lw["cv_ln_g"], lw["cv_ln_b"]], cfg.cv, cfg.cv)
    xn, y, yt = out_proj(x, y_dn, y_sg, y_cv, lw["wo"], gate)
    return xn, dict(x=x, p=p, ht=ht, qkv=qkv, wy=wy, ss=ss, y=y, yt=yt)


def _layer_bwd(cfg, dxn, mod, lw, sv):
    shift, scale, gate = mod
    p = sv["p"]
    d_dn, d_sg, d_cv, dyb, dgate = out_proj_bwd(dxn, sv["y"], gate, lw["wo"], cfg.dn, cfg.sg, cfg.cv)
    g_wo = matmul_acc("w_out_grad", sv["yt"], dyb)
    cv_post = [_cv_post] * (cfg.cv // LANES)
    dcv, g_cvw, (g_cvb, g_cvlg, g_cvlb) = conv_bwd(
        "cv_bwd", cfg.kc, HALO31, _glu, [(p, cfg.o_cv), (p, cfg.o_cv + cfg.cv)], lw["cv_w"], cv_post,
        [(p, cfg.o_cv + 2 * cfg.cv)], [lw["cv_b"], lw["cv_ln_g"], lw["cv_ln_b"]], d_cv, cfg.cv, cfg.cv, tm_pref=256)
    dsg, g_sglg, g_sglb, g_sgw, g_sgb = sg_bwd(p, cfg.o_sg, cfg.sg, lw["sg_ln_g"], lw["sg_ln_b"], lw["sg_w"],
                                               lw["sg_bias_b"], d_sg)
    *dwy, dz, g_dng = dn_seq_bwd(*sv["wy"][:6], p, cfg.o_z, lw["dn_norm_g"], sv["ss"], d_dn, cfg.h, cfg.hb_bwd)
    dqkv, dba, g_al, g_dt = dn_wy_bwd(sv["qkv"], p, cfg.o_ba, lw["alog_b"], lw["dtb_b"], sv["wy"][6], *dwy, cfg.h)
    qk_post = [_qk_post, _qk_post, _v_post]
    dqkv_pre, g_cq, _ = conv_bwd("dn_pre_bwd", cfg.k4, HALO4, lambda a: a, [(p, 0)], lw["conv_qkv"], qk_post, [], [],
                                 dqkv, 3 * cfg.dn, 3 * LANES)
    dp = jnp.concatenate([dqkv_pre, dz.astype(BF16), dsg, dcv, dba.astype(BF16)], axis=1)
    g_wp = matmul_acc("w_in_grad", sv["ht"], dp)
    dx, g_ng, dscale, dshift = in_proj_bwd(dp, lw["wp"], sv["x"], dxn, shift, scale, lw["norm_g"])
    grads = dict(norm_g=g_ng, conv_qkv=g_cq, a_log=g_al[:cfg.h, 0], dt_bias=g_dt[:cfg.h, 0], dn_norm_g=g_dng,
                 sg_ln_g=g_sglg, sg_ln_b=g_sglb, sg_w=g_sgw, sg_b=g_sgb[:, :, 0], cv_w=g_cvw, cv_b=g_cvb,
                 cv_ln_g=g_cvlg, cv_ln_b=g_cvlb, wp=g_wp, wo=g_wo)
    return dx, grads, (dshift, dscale, dgate)


def _local_step(cfg, xs, tgt, mods, lws, fg):
    nl = len(lws)
    saved = []
    for l in range(nl):
        xs, sv = _layer_fwd(cfg, xs, mods[l], lws[l])
        saved.append(sv)
    loss_b, dx, g_fg = loss_head(xs, tgt, fg)
    lg = [None] * nl
    dmods = [None] * nl
    for l in reversed(range(nl)):
        dx, lg[l], dmods[l] = _layer_bwd(cfg, dx, mods[l], lws[l], saved[l])
    return loss_b, dx, g_fg, lg, dmods


SMALL = ("norm_g", "conv_qkv", "a_log", "dt_bias", "dn_norm_g", "sg_ln_g", "sg_ln_b", "sg_w", "sg_b", "cv_w",
         "cv_b", "cv_ln_g", "cv_ln_b", "final_g", "b_ada")
PACK_N = 1024


def _pack(arrs):
    flat = jnp.concatenate([a.reshape(-1).astype(F32) for a in arrs])
    rows = -(-flat.shape[0] // PACK_N)
    rows = -(-rows // 8) * 8
    return jnp.pad(flat, (0, rows * PACK_N - flat.shape[0])).reshape(rows, PACK_N)


def _unpack(buf, shapes):
    flat = buf.reshape(-1)
    out, o = [], 0
    for s in shapes:
        n = 1
        for d_ in s:
            n *= d_
        out.append(flat[o:o + n].reshape(s))
        o += n
    return out


def kernel(x, c, norm_g, w_ada, b_ada, w_in, conv_qkv, a_log, dt_bias, dn_norm_g, sg_ln_g, sg_ln_b, sg_w, sg_b, cv_w, cv_b, cv_ln_g, cv_ln_b, w_out, final_g, loss_target, m_norm_g, m_w_ada, m_b_ada, m_w_in, m_conv_qkv, m_a_log, m_dt_bias, m_dn_norm_g, m_sg_ln_g, m_sg_ln_b, m_sg_w, m_sg_b, m_cv_w, m_cv_b, m_cv_ln_g, m_cv_ln_b, m_w_out, m_final_g, v_norm_g, v_w_ada, v_b_ada, v_w_in, v_conv_qkv, v_a_log, v_dt_bias, v_dn_norm_g, v_sg_ln_g, v_sg_ln_b, v_sg_w, v_sg_b, v_cv_w, v_cv_b, v_cv_ln_g, v_cv_ln_b, v_w_out, v_final_g):
    cfg = _Cfg(x, a_log, sg_w, cv_ln_g, cv_w, conv_qkv)
    nl, d, t, h = cfg.nl, cfg.d, cfg.t, cfg.h
    lh = nl // 2
    ax, ay, ac = _me()
    chip = 2 * ax + ay
    dev = 2 * chip + ac
    wts = dict(norm_g=norm_g, w_ada=w_ada, b_ada=b_ada, w_in=w_in, conv_qkv=conv_qkv, a_log=a_log, dt_bias=dt_bias,
               dn_norm_g=dn_norm_g, sg_ln_g=sg_ln_g, sg_ln_b=sg_ln_b, sg_w=sg_w, sg_b=sg_b, cv_w=cv_w, cv_b=cv_b,
               cv_ln_g=cv_ln_g, cv_ln_b=cv_ln_b, w_out=w_out, final_g=final_g)
    mom = dict(norm_g=m_norm_g, w_ada=m_w_ada, b_ada=m_b_ada, w_in=m_w_in, conv_qkv=m_conv_qkv, a_log=m_a_log,
               dt_bias=m_dt_bias, dn_norm_g=m_dn_norm_g, sg_ln_g=m_sg_ln_g, sg_ln_b=m_sg_ln_b, sg_w=m_sg_w,
               sg_b=m_sg_b, cv_w=m_cv_w, cv_b=m_cv_b, cv_ln_g=m_cv_ln_g, cv_ln_b=m_cv_ln_b, w_out=m_w_out,
               final_g=m_final_g)
    vel = dict(norm_g=v_norm_g, w_ada=v_w_ada, b_ada=v_b_ada, w_in=v_w_in, conv_qkv=v_conv_qkv, a_log=v_a_log,
               dt_bias=v_dt_bias, dn_norm_g=v_dn_norm_g, sg_ln_g=v_sg_ln_g, sg_ln_b=v_sg_ln_b, sg_w=v_sg_w,
               sg_b=v_sg_b, cv_w=v_cv_w, cv_b=v_cv_b, cv_ln_g=v_cv_ln_g, cv_ln_b=v_cv_ln_b, w_out=v_w_out,
               final_g=v_final_g)
    ada_cols = w_ada.shape[2]
    in_cols = w_in.shape[2]
    out_rows = w_out.shape[1]
    cq_cols = conv_qkv.shape[2]
    cvw_cols = cv_w.shape[2]

    c_all = all_gather8(jnp.pad(c, ((0, 7), (0, 0)))).reshape(NDEV, 8, d)[:, 0, :]
    b_loc = lax.dynamic_slice_in_dim(b_ada, chip * ada_cols, ada_cols, axis=1)[:, None, :]
    mod_part = ada_fwd(c_all, w_ada, b_loc)
    mod_all = all_gather8(mod_part.reshape(nl * NDEV, ada_cols)).reshape(NDEV, nl, NDEV, ada_cols)
    mod_me = lax.dynamic_index_in_dim(mod_all[0::2], dev, axis=2, keepdims=False)
    mod_me = jnp.moveaxis(mod_me, 0, 1).reshape(nl, 3, 1, d)

    win_b = w_in.astype(BF16).reshape(2, lh, d, in_cols)
    wout_b = w_out.astype(BF16).reshape(2, lh, out_rows, d)
    win_all, wout_all = gather_weights([win_b, wout_b])
    win_all = lax.dynamic_update_index_in_dim(win_all, win_b, chip, axis=0)
    wout_all = lax.dynamic_update_index_in_dim(wout_all, wout_b, chip, axis=0)
    win_all = win_all.reshape(NCHIP, nl, d, in_cols)
    wp_all = [_assemble_perm(cfg, [win_all[s, l] for s in range(NCHIP)]) for l in range(nl)]
    wo_all = jnp.moveaxis(wout_all.reshape(NCHIP, nl, out_rows, d), 0, 1).reshape(nl, NCHIP * out_rows, d)

    cq_all = all_gather8(conv_qkv.reshape(nl * cfg.k4, cq_cols)).reshape(NDEV, nl, cfg.k4, cq_cols)[0::2]
    cq_full = jnp.moveaxis(cq_all, 0, 2).reshape(nl, cfg.k4, NCHIP * cq_cols)
    cq_perm = _perm_cols_qkv(cfg, cq_full)
    kcp = -(-cfg.kc // 8) * 8
    cvw_all = all_gather8(jnp.pad(cv_w, ((0, 0), (0, kcp - cfg.kc), (0, 0))).reshape(nl * kcp, cvw_cols))
    cvw_all = cvw_all.reshape(NDEV, nl, kcp, cvw_cols)[0::2]
    cvw_full = jnp.moveaxis(cvw_all, 0, 2).reshape(nl, kcp, NCHIP * cvw_cols)[:, :cfg.kc]

    hp = -(-h // 8) * 8
    lws = []
    for l in range(nl):
        lws.append(dict(
            norm_g=norm_g[l][None], wp=wp_all[l], wo=wo_all[l], conv_qkv=cq_perm[l],
            alog_b=jnp.pad(jnp.broadcast_to(a_log[l][:, None], (h, LANES)), ((0, hp - h), (0, 0))),
            dtb_b=jnp.pad(jnp.broadcast_to(dt_bias[l][:, None], (h, LANES)), ((0, hp - h), (0, 0))),
            dn_norm_g=dn_norm_g[l][None], sg_ln_g=sg_ln_g[l][None], sg_ln_b=sg_ln_b[l][None], sg_w=sg_w[l],
            sg_bias_b=jnp.broadcast_to(sg_b[l][:, :, None], (cfg.g, LANES, LANES)),
            cv_w=cvw_full[l], cv_b=cv_b[l][None], cv_ln_g=cv_ln_g[l][None], cv_ln_b=cv_ln_b[l][None]))

    mods = [(mod_me[l, 0], mod_me[l, 1], mod_me[l, 2]) for l in range(nl)]
    loss_b, dx, g_fg, lg, dmods = _local_step(cfg, x[0], loss_target[0], mods, lws, final_g[None])
    grad_x = dx[None]

    dmod = jnp.stack([jnp.concatenate(dm, axis=1)[0] for dm in dmods])
    stack = lambda k: jnp.stack([g_[k] for g_ in lg])
    small_local = [stack(k).reshape(wts_shape) for k, wts_shape in
                   (("norm_g", (nl, d)), ("conv_qkv", (nl, cfg.k4, 3 * cfg.dn)), ("a_log", (nl, h)),
                    ("dt_bias", (nl, h)), ("dn_norm_g", (nl, LANES)), ("sg_ln_g", (nl, cfg.sg)),
                    ("sg_ln_b", (nl, cfg.sg)), ("sg_w", (nl, cfg.g, LANES, LANES)), ("sg_b", (nl, cfg.g, LANES)),
                    ("cv_w", (nl, cfg.kc, cfg.cv)), ("cv_b", (nl, cfg.cv)), ("cv_ln_g", (nl, cfg.cv)),
                    ("cv_ln_b", (nl, cfg.cv)))]
    small_local[1] = _unperm_cols_qkv(cfg, small_local[1])
    small_local += [g_fg[0], dmod, loss_b[0, 0:1]]
    shapes = [a.shape for a in small_local]
    packed = _pack(small_local)
    rows = packed.shape[0]
    gathered = all_gather8(packed).reshape(NDEV, rows, PACK_N)
    summed = _unpack(sum8(gathered), shapes)
    sgrads = dict(zip(SMALL, summed[:15]))
    loss = summed[15][0]
    sgrads["conv_qkv"] = lax.dynamic_slice_in_dim(sgrads["conv_qkv"], chip * cq_cols, cq_cols, axis=2)
    sgrads["cv_w"] = lax.dynamic_slice_in_dim(sgrads["cv_w"], chip * cvw_cols, cvw_cols, axis=2)

    off = sum(math.prod(s) for s in shapes[:14])
    dmod_all = gathered.reshape(NDEV, rows * PACK_N)[:, off:off + nl * 3 * d].reshape(NDEV, nl, 3 * d)
    dmod_loc = jnp.moveaxis(lax.dynamic_slice_in_dim(dmod_all, chip * ada_cols, ada_cols, axis=2), 0, 1)
    g_wada, d_wada, nm_wada, nv_wada = ada_bwd(c_all.T, dmod_loc, w_ada, m_w_ada, v_w_ada)

    shard = lambda g_, s: jnp.concatenate([g_[:, a:b] for a, b in _natural_pieces(cfg, s, in_cols)], axis=-1)
    g_in = jnp.stack([jnp.stack([jnp.concatenate([shard(lg[hh * lh + j]["wp"], s) for j in range(lh)], axis=0)
                                 for s in range(NCHIP)]) for hh in range(2)])
    g_wo = jnp.stack([g_["wo"] for g_ in lg]).reshape(2, lh, NCHIP, out_rows, d)
    g_out = jnp.moveaxis(g_wo, 2, 1).reshape(2, NCHIP, lh * out_rows, d)
    cflag = jnp.full((1, LANES), ac, F32)
    r1_in, r1_out = sibling_swap("swap_halves", [g_in, g_out], other_half=True)
    r2_in, r2_out = chip_exchange([pair_sum(cflag, g_in, r1_in), pair_sum(cflag, g_out, r1_out)])
    mine = lambda g_: lax.dynamic_index_in_dim(g_, chip, axis=0, keepdims=False)
    keep = lambda g_: lax.dynamic_index_in_dim(g_, ac, axis=0, keepdims=False)
    h_in = chip_sum(mine(keep(g_in)), mine(r1_in), r2_in)
    h_out = chip_sum(mine(keep(g_out)), mine(r1_out), r2_out)
    o_in, o_out = sibling_swap("join_halves", [h_in, h_out])

    v3 = lambda a, r_, c_: a.reshape(2, lh * r_, c_)
    grad_w_in, d_in_, nm_in, nv_in = adamw_halves("adamw_w_in", cflag, v3(w_in, d, in_cols), h_in, o_in,
                                                  v3(m_w_in, d, in_cols), v3(v_w_in, d, in_cols))
    grad_w_out, d_out_, nm_out, nv_out = adamw_halves("adamw_w_out", cflag, v3(w_out, out_rows, d), h_out, o_out,
                                                      v3(m_w_out, out_rows, d), v3(v_w_out, out_rows, d))
    grad_w_in = grad_w_in.reshape(w_in.shape)
    grad_w_out = grad_w_out.reshape(w_out.shape)
    sshapes = [wts[k].shape for k in SMALL]
    pk = lambda dct: _pack([dct[k] for k in SMALL])
    d_s, m_s, v_s = adamw("adamw_small", pk(wts), pk(sgrads), pk(mom), pk(vel))
    d_small = dict(zip(SMALL, _unpack(d_s, sshapes)))
    m_small = dict(zip(SMALL, _unpack(m_s, sshapes)))
    v_small = dict(zip(SMALL, _unpack(v_s, sshapes)))

    grads = dict(sgrads, w_ada=g_wada, w_in=grad_w_in, w_out=grad_w_out)
    deltas = dict(d_small, w_ada=d_wada, w_in=d_in_.reshape(w_in.shape), w_out=d_out_.reshape(w_out.shape))
    new_m = dict(m_small, w_ada=nm_wada, w_in=nm_in.reshape(w_in.shape), w_out=nm_out.reshape(w_out.shape))
    new_v = dict(v_small, w_ada=nv_wada, w_in=nv_in.reshape(w_in.shape), w_out=nv_out.reshape(w_out.shape))
    order = ("norm_g", "w_ada", "b_ada", "w_in", "conv_qkv", "a_log", "dt_bias", "dn_norm_g", "sg_ln_g", "sg_ln_b",
             "sg_w", "sg_b", "cv_w", "cv_b", "cv_ln_g", "cv_ln_b", "w_out", "final_g")
    return (loss, grad_x, *[grads[k] for k in order], *[deltas[k] for k in order], *[new_m[k] for k in order],
            *[new_v[k] for k in order])


def _perm_cols_qkv(cfg, w):
    lead = w.shape[:-1]
    return jnp.moveaxis(w.reshape(lead + (3, cfg.h, LANES)), -3, -2).reshape(lead + (3 * cfg.dn,))


def _unperm_cols_qkv(cfg, w):
    lead = w.shape[:-1]
    return jnp.moveaxis(w.reshape(lead + (cfg.h, 3, LANES)), -3, -2).reshape(lead + (3 * cfg.dn,))
```

```python
import functools
import math

import jax
import jax.numpy as jnp
from jax import lax
from jax.experimental import pallas as pl
from jax.experimental.pallas import tpu as pltpu

F32 = jnp.float32
BF16 = jnp.bfloat16
EPS = 1e-6
LN_EPS = 1e-5
LANES = 128
CHUNK = 64
SUBLANES = 8
HALO4 = 8
HALO31 = 32
NCHIP = 4
NDEV = 8
VMEM_LIMIT = 56 * 2 ** 20
ADAM_LR, ADAM_B1, ADAM_B2, ADAM_EPS, ADAM_WD, ADAM_STEP = 0.001, 0.9, 0.999, 1e-08, 0.01, 10
MESH = pl.DeviceIdType.MESH
ANY = pl.BlockSpec(memory_space=pl.ANY)


def _cp(sem=None, vmem=VMEM_LIMIT):
    return pltpu.CompilerParams(dimension_semantics=sem, vmem_limit_bytes=vmem)


def _tile(n, pref, mult):
    t = min(n, pref) // mult * mult
    while t > 0 and n % t:
        t -= mult
    return t if t > 0 else n


def _split(a):
    hi = a.astype(BF16)
    return hi, (a - hi.astype(F32)).astype(BF16)


def _raw_dot(a, b, ca, cb, hi):
    dn = (((ca,), (cb,)), ((), ()))
    if hi:
        ah, al = _split(a.astype(F32))
        bh, bl = _split(b.astype(F32))
        d3 = lambda x, y: lax.dot_general(x, y, dn, preferred_element_type=F32)
        return d3(ah, bh) + (d3(al, bh) + d3(ah, bl))
    return lax.dot_general(a.astype(BF16), b.astype(BF16), dn, preferred_element_type=F32)


@functools.partial(jax.custom_vjp, nondiff_argnums=(2, 3, 4))
def bdot(a, b, ca, cb, hi):
    return _raw_dot(a, b, ca, cb, hi)


def _bdot_fwd(a, b, ca, cb, hi):
    return _raw_dot(a, b, ca, cb, hi), (a, b)


def _bdot_bwd(ca, cb, hi, res, ct):
    a, b = res
    fa, fb = 1 - ca, 1 - cb
    da = _raw_dot(ct, b, 1, fb, hi) if ca == 1 else _raw_dot(b, ct, fb, 1, hi)
    db = _raw_dot(a, ct, fa, 0, hi) if cb == 0 else _raw_dot(ct, a, 0, fa, hi)
    return da.astype(a.dtype), db.astype(b.dtype)


bdot.defvjp(_bdot_fwd, _bdot_bwd)


def _sigmoid(x):
    return jax.nn.sigmoid(x)


def _silu(x):
    return x * _sigmoid(x)


def _gelu(x):
    return 0.5 * x * (1.0 + lax.erf(x * (2.0 ** -0.5)))


def _softplus(x):
    return jnp.maximum(x, 0.0) + jnp.log(1.0 + jnp.exp(-jnp.abs(x)))


def _modnorm(x, g, scale, shift):
    y = x * lax.rsqrt(jnp.mean(x * x, axis=-1, keepdims=True) + EPS)
    return (y * g) * (1.0 + scale) + shift


def _rmsnorm(x, g):
    return x * lax.rsqrt(jnp.mean(x * x, axis=-1, keepdims=True) + EPS) * g


def _layernorm(x, g, b):
    mu = jnp.mean(x, axis=-1, keepdims=True)
    xc = x - mu
    var = jnp.mean(xc * xc, axis=-1, keepdims=True)
    return xc * lax.rsqrt(var + LN_EPS) * g + b


def _l2norm(t):
    return t * lax.rsqrt(jnp.sum(t * t, axis=-1, keepdims=True) + EPS)


def _adamw_math(w, g, m, v):
    mn = ADAM_B1 * m + (1.0 - ADAM_B1) * g
    vn = ADAM_B2 * v + (1.0 - ADAM_B2) * (g * g)
    mh = mn / (1.0 - ADAM_B1 ** ADAM_STEP)
    vh = vn / (1.0 - ADAM_B2 ** ADAM_STEP)
    delta = -ADAM_LR * (mh / (jnp.sqrt(vh) + ADAM_EPS) + ADAM_WD * w)
    return delta, mn, vn


def _each(f, *lists):
    return [f(*xs) for xs in zip(*lists)]


def _wy(q, k, v, bcol, acol, alog, dtb, tinv=None):
    c = CHUNK
    r = lax.broadcasted_iota(jnp.int32, (c, c), 0)
    cc = lax.broadcasted_iota(jnp.int32, (c, c), 1)
    rr = lax.broadcasted_iota(jnp.int32, (c, 1), 0)
    tri_incl, tri_strict, eye = r >= cc, r > cc, r == cc
    beta = _each(_sigmoid, bcol)
    g = _each(lambda al, a_, dt: -jnp.exp(al) * _softplus(a_ + dt), alog, acol, dtb)
    gb = [jnp.broadcast_to(g_, (c, c)) for g_ in g]
    g_row = [jnp.sum(jnp.where(eye, b_, 0.0), axis=0, keepdims=True) for b_ in gb]
    gc_col = [jnp.sum(jnp.where(tri_incl, jnp.broadcast_to(gr, (c, c)), 0.0), axis=1, keepdims=True) for gr in g_row]
    gc_row = [jnp.sum(jnp.where(r <= cc, b_, 0.0), axis=0, keepdims=True) for b_ in gb]
    decay = _each(lambda gcc, gcr: jnp.where(tri_incl, jnp.exp(jnp.where(tri_incl, gcc - gcr, 0.0)), 0.0),
                  gc_col, gc_row)
    qs = [q_ * (q_.shape[-1] ** -0.5) for q_ in q]
    kb = _each(lambda k_, b_: k_ * b_, k, beta)
    a = _each(lambda kb_, k_, d_: jnp.where(tri_strict, bdot(kb_, k_, 1, 1, False) * d_, 0.0), kb, k, decay)
    dv = v[0].shape[-1]
    x = _each(lambda v_, b_, kb_, gcc: jnp.concatenate([v_ * b_, kb_ * jnp.exp(gcc)], axis=1), v, beta, kb, gc_col)
    if tinv is None:
        inv = [jnp.where(eye, 1.0, 0.0) - a_ for a_ in a]
        p = a
        for _ in range(5):
            p = _each(lambda p_: bdot(p_, p_, 1, 0, True), p)
            inv = _each(lambda t_, p_: t_ + bdot(t_, p_, 1, 0, True), inv, p)
        x = _each(lambda t_, x_: bdot(t_, x_, 1, 0, True), inv, x)
    else:
        x = _each(_solve_given_inverse, a, x, tinv)
    xv = [x_[:, :dv] for x_ in x]
    xk = [x_[:, dv:] for x_ in x]
    qk = _each(lambda q_, k_, d_: bdot(q_, k_, 1, 1, False) * d_, qs, k, decay)
    g_last = [jnp.sum(jnp.where(rr == c - 1, gcc, 0.0), axis=0, keepdims=True) for gcc in gc_col]
    qg = _each(lambda q_, gcc: q_ * jnp.exp(gcc), qs, gc_col)
    kd = _each(lambda k_, gl, gcc: k_ * jnp.exp(gl - gcc), k, g_last, gc_col)
    outs = (xv, xk, qg, kd, qk, [jnp.exp(gl) for gl in g_last])
    return outs + (inv,) if tinv is None else outs


@jax.custom_vjp
def _solve_given_inverse(a, rhs, tinv):
    return _raw_dot(tinv, rhs, 1, 0, True)


def _solve_given_inverse_fwd(a, rhs, tinv):
    x = _raw_dot(tinv, rhs, 1, 0, True)
    return x, (x, tinv)


def _solve_given_inverse_bwd(res, dx):
    x, tinv = res
    drhs = _raw_dot(tinv, dx, 0, 0, True)
    return -_raw_dot(drhs, x, 1, 1, True), drhs, jnp.zeros_like(tinv)


_solve_given_inverse.defvjp(_solve_given_inverse_fwd, _solve_given_inverse_bwd)


def _seq(u, w, qg, kd, qk, e, z, s, ng):
    v_new = _each(lambda u_, w_, s_: u_ - bdot(w_, s_, 1, 0, False), u, w, s)
    o1 = _each(lambda q_, s_: bdot(q_, s_, 1, 0, False), qg, s)
    o2 = _each(lambda qk_, vn: bdot(qk_, vn, 1, 0, False), qk, v_new)
    ds = _each(lambda kd_, vn: bdot(kd_, vn, 0, 0, False), kd, v_new)
    s_next = _each(lambda s_, e_, d_: s_ * e_ + d_, s, e, ds)
    y = _each(lambda a_, b_, z_: _rmsnorm(a_ + b_, ng) * _silu(z_), o1, o2, z)
    return y, s_next


def _sg_block(u, v, gt, lg, lb, w, bias):
    n = w.shape[0]
    pr = lax.broadcasted_iota(jnp.int32, (n, n), 0) // CHUNK
    pc = lax.broadcasted_iota(jnp.int32, (n, n), 1) // CHUNK
    wm = jnp.where(pr >= pc, w, 0.0)
    vl = _layernorm(_gelu(v), lg, lb)
    mixed = bdot(wm, vl, 1, 0, False) + bias
    return _gelu(u) * mixed * _silu(gt)


def _glu(a, b):
    return a * _sigmoid(b)


def _cv_post(conv, gate, cb, lg, lb):
    return _silu(_layernorm(conv + cb, lg, lb)) * _silu(gate)


def _qk_post(conv):
    return _l2norm(_silu(conv))


def _v_post(conv):
    return _silu(conv)


def in_proj(x, shift, scale, ng, wp):
    t, d = x.shape
    npc = wp.shape[1]
    tm, tn = _tile(t, 512, LANES), _tile(npc, 2432, LANES)

    def body(x_ref, sh_ref, sc_ref, g_ref, w_ref, p_ref, ht_ref, h_scr):
        @pl.when(pl.program_id(1) == 0)
        def _():
            h = _modnorm(x_ref[...], g_ref[...], sc_ref[...], sh_ref[...])
            h_scr[...] = h.astype(BF16)
            ht_ref[...] = h.T.astype(BF16)
        p_ref[...] = jnp.dot(h_scr[...], w_ref[...], preferred_element_type=F32)

    vec = pl.BlockSpec((1, d), lambda i, j: (0, 0))
    return pl.pallas_call(
        body, name="in_proj", grid=(t // tm, npc // tn),
        in_specs=[pl.BlockSpec((tm, d), lambda i, j: (i, 0)), vec, vec, vec,
                  pl.BlockSpec((d, tn), lambda i, j: (0, j))],
        out_specs=[pl.BlockSpec((tm, tn), lambda i, j: (i, j)), pl.BlockSpec((d, tm), lambda i, j: (0, i))],
        out_shape=[jax.ShapeDtypeStruct((t, npc), F32), jax.ShapeDtypeStruct((d, t), BF16)],
        scratch_shapes=[pltpu.VMEM((tm, d), BF16)],
        compiler_params=_cp(("parallel", "arbitrary")),
    )(x, shift, scale, ng, wp)


def _roll_bank(x, bank_ref, offsets):
    rows = x.shape[0]
    residues = sorted({o % SUBLANES for o in offsets})
    for slot, b in enumerate(residues):
        bank_ref[slot] = x if b == 0 else pltpu.roll(x, rows - b, 0)
    return {o: (residues.index(o % SUBLANES), o - o % SUBLANES) for o in offsets}


def _n_residues(offsets):
    return len({o % SUBLANES for o in offsets})


def conv_fwd(name, k, halo, pre_fn, pre, w, post_fns, extras, params, c_total, tc, tm_pref=512):
    t = pre[0][0].shape[0]
    tm = _tile(t, tm_pref, halo)
    npre, nex, npar = len(pre), len(extras), len(params)
    ngr = tc // LANES
    taps = [halo - (k - 1) + j for j in range(k)]

    def body(*refs):
        prev = refs[:npre]
        cur = refs[npre:2 * npre]
        w_ref = refs[2 * npre]
        ex = refs[2 * npre + 1:2 * npre + 1 + nex]
        par = refs[2 * npre + 1 + nex:2 * npre + 1 + nex + npar]
        out_ref, buf, bank = refs[-3], refs[-2], refs[-1]
        i = pl.program_id(1)
        pv = pre_fn(*[r[...] for r in prev])
        buf[0:halo, :] = jnp.where(i > 0, pv, 0.0)
        buf[halo:, :] = pre_fn(*[r[...] for r in cur])
        where = _roll_bank(buf[...], bank, taps)
        acc = None
        for j, o in enumerate(taps):
            slot, st = where[o]
            term = w_ref[j:j + 1, :] * bank[slot, st:st + tm, :]
            acc = term if acc is None else acc + term
        for gi in range(ngr):
            sl = slice(gi * LANES, (gi + 1) * LANES)
            out_ref[:, sl] = post_fns[gi](acc[:, sl], *[e[:, sl] for e in ex], *[p_[:, sl] for p_ in par])

    hb = tm // halo
    in_specs = ([pl.BlockSpec((halo, tc), functools.partial(lambda j, i, o: (jnp.maximum(i * hb - 1, 0), o + j), o=col // tc))
                 for _, col in pre]
                + [pl.BlockSpec((tm, tc), functools.partial(lambda j, i, o: (i, o + j), o=col // tc)) for _, col in pre]
                + [pl.BlockSpec((k, tc), lambda j, i: (0, j))]
                + [pl.BlockSpec((tm, tc), functools.partial(lambda j, i, o: (i, o + j), o=col // tc)) for _, col in extras]
                + [pl.BlockSpec((1, tc), lambda j, i: (0, j)) for _ in params])
    args = [a for a, _ in pre] * 2 + [w] + [a for a, _ in extras] + list(params)
    return pl.pallas_call(
        body, name=name, grid=(c_total // tc, t // tm), in_specs=in_specs,
        out_specs=pl.BlockSpec((tm, tc), lambda j, i: (i, j)),
        out_shape=jax.ShapeDtypeStruct((t, c_total), F32),
        scratch_shapes=[pltpu.VMEM((halo + tm, tc), F32), pltpu.VMEM((_n_residues(taps), halo + tm, tc), F32)],
        compiler_params=_cp(("parallel", "arbitrary")),
    )(*args)


def conv_bwd(name, k, halo, pre_fn, pre, w, post_fns, extras, params, dout, c_total, tc, tm_pref=512):
    t = pre[0][0].shape[0]
    tm = _tile(t, tm_pref, halo)
    npre, nex, npar = len(pre), len(extras), len(params)
    ngr = tc // LANES
    nout = npre + nex
    assert nout == 1 or c_total == tc
    nblk = t // tm
    ext = tm + halo
    taps = [halo - (k - 1) + j for j in range(k)]
    back = [k - 1 - j for j in range(k)]

    def body(*refs):
        it = iter(refs)
        prev = [next(it) for _ in range(npre)]
        cur = [next(it) for _ in range(npre)]
        nxt = [next(it) for _ in range(npre)]
        w_ref = next(it)
        ex_c = [next(it) for _ in range(nex)]
        ex_n = [next(it) for _ in range(nex)]
        par = [next(it) for _ in range(npar)]
        do_c, do_n = next(it), next(it)
        din_ref, dw_ref = next(it), next(it)
        dpar = [next(it) for _ in range(npar)]
        buf, dbuf, bank, dbank = next(it), next(it), next(it), next(it)
        i = pl.program_id(1)

        @pl.when(i == 0)
        def _():
            dw_ref[...] = jnp.zeros_like(dw_ref)
            for r in dpar:
                r[...] = jnp.zeros_like(r)

        buf[0:halo, :] = jnp.where(i > 0, pre_fn(*[r[...] for r in prev]), 0.0)
        cur_vals = [r[...] for r in cur]
        buf[halo:halo + tm, :] = pre_fn(*cur_vals)
        buf[halo + tm:, :] = pre_fn(*[r[...] for r in nxt])
        where = _roll_bank(buf[...], bank, taps)
        conv = None
        for j, o in enumerate(taps):
            slot, st = where[o]
            term = w_ref[j:j + 1, :] * bank[slot, st:st + ext, :]
            conv = term if conv is None else conv + term
        don = jnp.where(i < nblk - 1, do_n[...], 0.0)
        for gi in range(ngr):
            sl = slice(gi * LANES, (gi + 1) * LANES)
            pv = [p_[:, sl] for p_ in par]
            _, vj = jax.vjp(post_fns[gi], conv[:tm, sl], *[e[:, sl] for e in ex_c], *pv)
            gr = vj(do_c[:, sl])
            dbuf[0:tm, sl] = gr[0]
            for e in range(nex):
                din_ref[:, (npre + e) * tc + gi * LANES:(npre + e) * tc + (gi + 1) * LANES] = gr[1 + e].astype(din_ref.dtype)
            for q_ in range(npar):
                dpar[q_][:, sl] += gr[1 + nex + q_]
            _, vjn = jax.vjp(post_fns[gi], conv[tm:, sl], *[e[:, sl] for e in ex_n], *pv)
            dbuf[tm:, sl] = vjn(don[:, sl])[0]
        dcur = dbuf[0:tm, :]
        dwhere = _roll_bank(dbuf[...], dbank, back)
        dpre = None
        for j in range(k):
            slot, st = dwhere[back[j]]
            term = w_ref[j:j + 1, :] * dbank[slot, st:st + tm, :]
            dpre = term if dpre is None else dpre + term
            slot, st = where[taps[j]]
            dw_ref[j:j + 1, :] += jnp.sum(dcur * bank[slot, st:st + tm, :], axis=0, keepdims=True)
        _, vjp_pre = jax.vjp(pre_fn, *cur_vals)
        for e, gval in enumerate(vjp_pre(dpre)):
            din_ref[:, e * tc:(e + 1) * tc] = gval.astype(din_ref.dtype)

    hb = tm // halo
    last_h = t // halo - 1

    def spec(kind, col):
        o = col // tc
        if kind == "prev":
            return pl.BlockSpec((halo, tc), lambda j, i: (jnp.maximum(i * hb - 1, 0), o + j))
        if kind == "next":
            return pl.BlockSpec((halo, tc), lambda j, i: (jnp.minimum((i + 1) * hb, last_h), o + j))
        return pl.BlockSpec((tm, tc), lambda j, i: (i, o + j))

    in_specs = ([spec("prev", col) for _, col in pre] + [spec("cur", col) for _, col in pre]
                + [spec("next", col) for _, col in pre] + [pl.BlockSpec((k, tc), lambda j, i: (0, j))]
                + [spec("cur", col) for _, col in extras] + [spec("next", col) for _, col in extras]
                + [pl.BlockSpec((1, tc), lambda j, i: (0, j)) for _ in params]
                + [spec("cur", 0), spec("next", 0)])
    args = [a for a, _ in pre] * 3 + [w] + [a for a, _ in extras] * 2 + list(params) + [dout, dout]
    out = pl.pallas_call(
        body, name=name, grid=(c_total // tc, nblk), in_specs=in_specs,
        out_specs=[pl.BlockSpec((tm, nout * tc), lambda j, i: (i, j)), pl.BlockSpec((k, tc), lambda j, i: (0, j))]
        + [pl.BlockSpec((1, tc), lambda j, i: (0, j)) for _ in params],
        out_shape=[jax.ShapeDtypeStruct((t, nout * c_total), BF16), jax.ShapeDtypeStruct((k, c_total), F32)]
        + [jax.ShapeDtypeStruct((1, c_total), F32) for _ in params],
        scratch_shapes=[pltpu.VMEM((2 * halo + tm, tc), F32), pltpu.VMEM((ext, tc), F32),
                        pltpu.VMEM((_n_residues(taps), 2 * halo + tm, tc), F32),
                        pltpu.VMEM((_n_residues(back), ext, tc), F32)],
        compiler_params=_cp(("parallel", "arbitrary")),
    )(*args)
    return out[0], out[1], out[2:]


def _head_pick(ref_val, row):
    rr = lax.broadcasted_iota(jnp.int32, ref_val.shape, 0)
    v = jnp.sum(jnp.where(rr == row, ref_val, 0.0), axis=0, keepdims=True)
    ll = lax.broadcasted_iota(jnp.int32, v.shape, 1)
    return jnp.sum(jnp.where(ll == 0, v, 0.0), axis=1, keepdims=True)


def _lane_col(blk, lane_idx):
    ll = lax.broadcasted_iota(jnp.int32, blk.shape, 1)
    return jnp.sum(jnp.where(ll == lane_idx, blk, 0.0), axis=1, keepdims=True)


WY_HEADS = 2
WY_UNROLL = 4
WY_UNROLL_BWD = 4


def dn_wy_fwd(qkv, p, ba_col, alog_b, dtb_b, nheads):
    t = qkv.shape[0]
    tm = _tile(t, 512, CHUNK * WY_UNROLL)
    nc = tm // CHUNK
    hb = WY_HEADS
    hp = alog_b.shape[0]
    w_ = hb * LANES

    def body(qkv_ref, ba_ref, al_ref, dt_ref, u_ref, w_ref, qg_ref, kd_ref, qk_ref, e_ref, ti_ref):
        hblk = pl.program_id(1)
        alv, dtv = al_ref[...], dt_ref[...]

        def trip(cj, carry):
            units = [(cj * WY_UNROLL + cu, hl) for cu in range(WY_UNROLL) for hl in range(hb)]
            args = [[] for _ in range(7)]
            for ci, hl in units:
                rows = pl.ds(pl.multiple_of(ci * CHUNK, CHUNK), CHUNK)
                ba = ba_ref[rows, :]
                h = hblk * hb + hl
                for lst, val in zip(args, (qkv_ref[rows, hl * 384:hl * 384 + 128],
                                           qkv_ref[rows, hl * 384 + 128:hl * 384 + 256],
                                           qkv_ref[rows, hl * 384 + 256:hl * 384 + 384],
                                           _lane_col(ba, h), _lane_col(ba, nheads + h),
                                           _head_pick(alv, h), _head_pick(dtv, h))):
                    lst.append(val)
            outs = _wy(*args)
            for n, (ci, hl) in enumerate(units):
                rows = pl.ds(pl.multiple_of(ci * CHUNK, CHUNK), CHUNK)
                u, w, qg, kd, qk, e, ti = [o[n] for o in outs]
                sl = slice(hl * LANES, (hl + 1) * LANES)
                u_ref[rows, sl] = u
                w_ref[rows, sl] = w.astype(BF16)
                qg_ref[rows, sl] = qg.astype(BF16)
                kd_ref[rows, sl] = kd.astype(BF16)
                qk_ref[rows, hl * LANES:hl * LANES + CHUNK] = qk.astype(BF16)
                qk_ref[rows, hl * LANES + CHUNK:(hl + 1) * LANES] = jnp.zeros((CHUNK, LANES - CHUNK), BF16)
                e_ref[ci, :, sl] = jnp.broadcast_to(e, (1, LANES))
                ti_ref[rows, hl * LANES:hl * LANES + CHUNK] = ti
                ti_ref[rows, hl * LANES + CHUNK:(hl + 1) * LANES] = jnp.zeros((CHUNK, LANES - CHUNK), F32)
            return carry

        lax.fori_loop(0, nc // WY_UNROLL, trip, 0)

    bc = ba_col // LANES
    blk = pl.BlockSpec((tm, w_), lambda i, h: (i, h))
    tab = pl.BlockSpec((hp, LANES), lambda i, h: (0, 0))
    wide = lambda dt: jax.ShapeDtypeStruct((t, nheads * LANES), dt)
    return pl.pallas_call(
        body, name="dn_wy_fwd", grid=(t // tm, nheads // hb),
        in_specs=[pl.BlockSpec((tm, hb * 384), lambda i, h: (i, h)), pl.BlockSpec((tm, LANES), lambda i, h: (i, bc)),
                  tab, tab],
        out_specs=[blk] * 5 + [pl.BlockSpec((nc, 1, w_), lambda i, h: (i, 0, h)), blk],
        out_shape=[wide(F32), wide(BF16), wide(BF16), wide(BF16), wide(BF16),
                   jax.ShapeDtypeStruct((t // CHUNK, 1, nheads * LANES), F32), wide(F32)],
        compiler_params=_cp(("parallel", "parallel")),
    )(qkv, p, alog_b, dtb_b)


def dn_seq_fwd(u, w, qg, kd, qk, e, p, z_col, ng, nheads, hb):
    t = u.shape[0]
    tm = _tile(t, 512, CHUNK)
    nc = tm // CHUNK
    w_ = hb * LANES

    def body(u_ref, w_ref, qg_ref, kd_ref, qk_ref, e_ref, z_ref, ng_ref, y_ref, ss_ref, s_scr):
        i, hblk = pl.program_id(0), pl.program_id(1)
        for hl in range(hb):
            @pl.when(i == 0)
            def _():
                s_scr[hblk * hb + hl] = jnp.zeros((LANES, LANES), F32)
        ngv = ng_ref[...]

        def chunk(ci, carry):
            rows = pl.ds(pl.multiple_of(ci * CHUNK, CHUNK), CHUNK)
            ev = e_ref[ci]
            sls = [slice(hl * LANES, (hl + 1) * LANES) for hl in range(hb)]
            s = [s_scr[hblk * hb + hl] for hl in range(hb)]
            for hl in range(hb):
                ss_ref[ci, sls[hl], :] = s[hl]
            y, sn = _seq([u_ref[rows, sl] for sl in sls], [w_ref[rows, sl].astype(F32) for sl in sls],
                         [qg_ref[rows, sl].astype(F32) for sl in sls], [kd_ref[rows, sl].astype(F32) for sl in sls],
                         [qk_ref[rows, sl][:, :CHUNK].astype(F32) for sl in sls], [ev[:, sl] for sl in sls],
                         [z_ref[rows, sl] for sl in sls], s, ngv)
            for hl in range(hb):
                y_ref[rows, sls[hl]] = y[hl]
                s_scr[hblk * hb + hl] = sn[hl]
            return carry

        lax.fori_loop(0, nc, chunk, 0)

    zc = z_col // w_
    blk = pl.BlockSpec((tm, w_), lambda i, h: (i, h))
    return pl.pallas_call(
        body, name="dn_seq_fwd", grid=(t // tm, nheads // hb),
        in_specs=[blk] * 5 + [pl.BlockSpec((nc, 1, w_), lambda i, h: (i, 0, h)),
                              pl.BlockSpec((tm, w_), lambda i, h: (i, zc + h)),
                              pl.BlockSpec((1, LANES), lambda i, h: (0, 0))],
        out_specs=[blk, pl.BlockSpec((nc, w_, LANES), lambda i, h: (i, h, 0))],
        out_shape=[jax.ShapeDtypeStruct((t, nheads * LANES), F32),
                   jax.ShapeDtypeStruct((t // CHUNK, nheads * LANES, LANES), F32)],
        scratch_shapes=[pltpu.VMEM((nheads, LANES, LANES), F32)],
        compiler_params=_cp(("arbitrary", "arbitrary")),
    )(u, w, qg, kd, qk, e, p, ng)


def dn_seq_bwd(u, w, qg, kd, qk, e, p, z_col, ng, ss, dy, nheads, hb):
    t = u.shape[0]
    tm = _tile(t, 2048 // hb, CHUNK)
    nc = tm // CHUNK
    nblk = t // tm
    w_ = hb * LANES

    def body(u_ref, w_ref, qg_ref, kd_ref, qk_ref, e_ref, z_ref, ng_ref, ss_ref, dy_ref,
             du_ref, dw_ref, dqg_ref, dkd_ref, dqk_ref, de_ref, dz_ref, dng_ref, ds_scr):
        i, hblk = pl.program_id(0), pl.program_id(1)

        @pl.when((i == 0) & (hblk == 0))
        def _():
            dng_ref[...] = jnp.zeros_like(dng_ref)

        for hl in range(hb):
            @pl.when(i == 0)
            def _():
                ds_scr[hblk * hb + hl] = jnp.zeros((LANES, LANES), F32)
        ngv = ng_ref[...]

        def chunk(cj, carry):
            ci = nc - 1 - cj
            rows = pl.ds(pl.multiple_of(ci * CHUNK, CHUNK), CHUNK)
            ev = e_ref[ci]
            sls = [slice(hl * LANES, (hl + 1) * LANES) for hl in range(hb)]
            _, vj = jax.vjp(_seq, [u_ref[rows, sl] for sl in sls], [w_ref[rows, sl].astype(F32) for sl in sls],
                            [qg_ref[rows, sl].astype(F32) for sl in sls], [kd_ref[rows, sl].astype(F32) for sl in sls],
                            [qk_ref[rows, sl][:, :CHUNK].astype(F32) for sl in sls], [ev[:, sl] for sl in sls],
                            [z_ref[rows, sl] for sl in sls], [ss_ref[ci, sl, :] for sl in sls], ngv)
            du, dw, dqg, dkd, dqk, de, dz, dsp, dng = vj(([dy_ref[rows, sl] for sl in sls],
                                                          [ds_scr[hblk * hb + hl] for hl in range(hb)]))
            for hl, sl in enumerate(sls):
                du_ref[rows, sl] = du[hl]
                dw_ref[rows, sl] = dw[hl]
                dqg_ref[rows, sl] = dqg[hl]
                dkd_ref[rows, sl] = dkd[hl]
                dqk_ref[rows, hl * LANES:hl * LANES + CHUNK] = dqk[hl]
                dqk_ref[rows, hl * LANES + CHUNK:(hl + 1) * LANES] = jnp.zeros((CHUNK, LANES - CHUNK), F32)
                de_ref[ci, :, sl] = de[hl]
                dz_ref[rows, sl] = dz[hl]
                ds_scr[hblk * hb + hl] = dsp[hl]
            dng_ref[...] += dng
            return carry

        lax.fori_loop(0, nc, chunk, 0)

    zc = z_col // w_
    rv = lambda i: nblk - 1 - i
    blk = pl.BlockSpec((tm, w_), lambda i, h: (rv(i), h))
    eblk = pl.BlockSpec((nc, 1, w_), lambda i, h: (rv(i), 0, h))
    one = pl.BlockSpec((1, LANES), lambda i, h: (0, 0))
    wide = jax.ShapeDtypeStruct((t, nheads * LANES), F32)
    return pl.pallas_call(
        body, name="dn_seq_bwd", grid=(nblk, nheads // hb),
        in_specs=[blk] * 5 + [eblk, pl.BlockSpec((tm, w_), lambda i, h: (rv(i), zc + h)), one,
                              pl.BlockSpec((nc, w_, LANES), lambda i, h: (rv(i), h, 0)), blk],
        out_specs=[blk] * 5 + [eblk, blk, one],
        out_shape=[wide] * 5 + [jax.ShapeDtypeStruct((t // CHUNK, 1, nheads * LANES), F32), wide,
                                jax.ShapeDtypeStruct((1, LANES), F32)],
        scratch_shapes=[pltpu.VMEM((nheads, LANES, LANES), F32)],
        compiler_params=_cp(("arbitrary", "arbitrary")),
    )(u, w, qg, kd, qk, e, p, ng, ss, dy)


def dn_wy_bwd(qkv, p, ba_col, alog_b, dtb_b, ti, du, dw, dqg, dkd, dqk, de, nheads):
    t = qkv.shape[0]
    tm = _tile(t, 512, CHUNK * WY_UNROLL_BWD)
    nc = tm // CHUNK
    hb = WY_HEADS
    hp = alog_b.shape[0]
    w_ = hb * LANES

    def body(qkv_ref, ba_ref, al_ref, dt_ref, du_ref, dw_ref, dqg_ref, dkd_ref, dqk_ref, de_ref, ti_ref,
             dqkv_ref, dba_ref, dal_ref, ddt_ref):
        i, hblk = pl.program_id(0), pl.program_id(1)

        @pl.when((i == 0) & (hblk == 0))
        def _():
            dal_ref[...] = jnp.zeros_like(dal_ref)
            ddt_ref[...] = jnp.zeros_like(ddt_ref)

        @pl.when(hblk == 0)
        def _():
            dba_ref[...] = jnp.zeros_like(dba_ref)

        alv, dtv = al_ref[...], dt_ref[...]
        lane = lax.broadcasted_iota(jnp.int32, (CHUNK, LANES), 1)
        rowp = lax.broadcasted_iota(jnp.int32, (hp, LANES), 0)

        def trip(cj, carry):
            units = [(cj * WY_UNROLL_BWD + cu, hl) for cu in range(WY_UNROLL_BWD) for hl in range(hb)]
            args = [[] for _ in range(8)]
            cts = [[] for _ in range(6)]
            for ci, hl in units:
                rows = pl.ds(pl.multiple_of(ci * CHUNK, CHUNK), CHUNK)
                ba = ba_ref[rows, :]
                h = hblk * hb + hl
                sl = slice(hl * LANES, (hl + 1) * LANES)
                for lst, val in zip(args, (qkv_ref[rows, hl * 384:hl * 384 + 128],
                                           qkv_ref[rows, hl * 384 + 128:hl * 384 + 256],
                                           qkv_ref[rows, hl * 384 + 256:hl * 384 + 384],
                                           _lane_col(ba, h), _lane_col(ba, nheads + h),
                                           _head_pick(alv, h), _head_pick(dtv, h), ti_ref[rows, sl][:, :CHUNK])):
                    lst.append(val)
                de11 = jnp.sum(de_ref[ci][:, sl], axis=1, keepdims=True)
                for lst, val in zip(cts, (du_ref[rows, sl], dw_ref[rows, sl], dqg_ref[rows, sl], dkd_ref[rows, sl],
                                          dqk_ref[rows, sl][:, :CHUNK], de11)):
                    lst.append(val)
            _, vj = jax.vjp(_wy, *args)
            grads = vj(tuple(cts))
            for n, (ci, hl) in enumerate(units):
                rows = pl.ds(pl.multiple_of(ci * CHUNK, CHUNK), CHUNK)
                h = hblk * hb + hl
                dq, dk, dv, dbc, dac, dal, ddt = [g_[n] for g_ in grads[:7]]
                dqkv_ref[rows, hl * 384:hl * 384 + 128] = dq
                dqkv_ref[rows, hl * 384 + 128:hl * 384 + 256] = dk
                dqkv_ref[rows, hl * 384 + 256:hl * 384 + 384] = dv
                dba_ref[rows, :] += jnp.where(lane == h, dbc, 0.0) + jnp.where(lane == nheads + h, dac, 0.0)
                dal_ref[...] += jnp.where(rowp == h, dal, 0.0)
                ddt_ref[...] += jnp.where(rowp == h, ddt, 0.0)
            return carry

        lax.fori_loop(0, nc // WY_UNROLL_BWD, trip, 0)

    bc = ba_col // LANES
    blk = pl.BlockSpec((tm, w_), lambda i, h: (i, h))
    tab = pl.BlockSpec((hp, LANES), lambda i, h: (0, 0))
    return pl.pallas_call(
        body, name="dn_wy_bwd", grid=(t // tm, nheads // hb),
        in_specs=[pl.BlockSpec((tm, hb * 384), lambda i, h: (i, h)), pl.BlockSpec((tm, LANES), lambda i, h: (i, bc)),
                  tab, tab] + [blk] * 5 + [pl.BlockSpec((nc, 1, w_), lambda i, h: (i, 0, h)), blk],
        out_specs=[pl.BlockSpec((tm, hb * 384), lambda i, h: (i, h)), pl.BlockSpec((tm, LANES), lambda i, h: (i, 0)),
                   tab, tab],
        out_shape=[jax.ShapeDtypeStruct((t, nheads * 384), F32), jax.ShapeDtypeStruct((t, LANES), F32),
                   jax.ShapeDtypeStruct((hp, LANES), F32), jax.ShapeDtypeStruct((hp, LANES), F32)],
        compiler_params=_cp(("arbitrary", "arbitrary")),
    )(qkv, p, alog_b, dtb_b, du, dw, dqg, dkd, dqk, de, ti)


def sg_fwd(p, col, sg, lg, lb, w, bias_b):
    t = p.shape[0]
    ng_ = sg // LANES
    tm = _tile(t, 256, LANES)
    cb = col // sg

    def body(u_ref, v_ref, g_ref, lg_ref, lb_ref, w_ref, b_ref, y_ref):
        for n in range(tm // LANES):
            rs = slice(n * LANES, (n + 1) * LANES)
            for gi in range(ng_):
                sl = slice(gi * LANES, (gi + 1) * LANES)
                y_ref[rs, sl] = _sg_block(u_ref[rs, sl], v_ref[rs, sl], g_ref[rs, sl], lg_ref[:, sl], lb_ref[:, sl],
                                          w_ref[gi], b_ref[gi])

    vec = pl.BlockSpec((1, sg), lambda i: (0, 0))
    full = pl.BlockSpec((ng_, LANES, LANES), lambda i: (0, 0, 0))
    return pl.pallas_call(
        body, name="sg_fwd", grid=(t // tm,),
        in_specs=[pl.BlockSpec((tm, sg), lambda i: (i, cb)), pl.BlockSpec((tm, sg), lambda i: (i, cb + 1)),
                  pl.BlockSpec((tm, sg), lambda i: (i, cb + 2)), vec, vec, full, full],
        out_specs=pl.BlockSpec((tm, sg), lambda i: (i, 0)),
        out_shape=jax.ShapeDtypeStruct((t, sg), F32),
        compiler_params=_cp(("parallel",)),
    )(p, p, p, lg, lb, w, bias_b)


def sg_bwd(p, col, sg, lg, lb, w, bias_b, dy):
    t = p.shape[0]
    ng_ = sg // LANES
    tm = _tile(t, 256, LANES)
    cb = col // sg

    def body(u_ref, v_ref, g_ref, lg_ref, lb_ref, w_ref, b_ref, dy_ref, d_ref, dlg_ref, dlb_ref, dw_ref, db_ref):
        @pl.when(pl.program_id(0) == 0)
        def _():
            for r in (dlg_ref, dlb_ref, dw_ref, db_ref):
                r[...] = jnp.zeros_like(r)

        for n in range(tm // LANES):
            rs = slice(n * LANES, (n + 1) * LANES)
            for gi in range(ng_):
                sl = slice(gi * LANES, (gi + 1) * LANES)
                _, vj = jax.vjp(_sg_block, u_ref[rs, sl], v_ref[rs, sl], g_ref[rs, sl], lg_ref[:, sl], lb_ref[:, sl],
                                w_ref[gi], b_ref[gi])
                du, dv, dg, dlg, dlb, dw, db = vj(dy_ref[rs, sl])
                d_ref[rs, gi * LANES:(gi + 1) * LANES] = du.astype(BF16)
                d_ref[rs, sg + gi * LANES:sg + (gi + 1) * LANES] = dv.astype(BF16)
                d_ref[rs, 2 * sg + gi * LANES:2 * sg + (gi + 1) * LANES] = dg.astype(BF16)
                dlg_ref[:, sl] += dlg
                dlb_ref[:, sl] += dlb
                dw_ref[gi] += dw
                db_ref[gi] += jnp.broadcast_to(jnp.sum(db, axis=1, keepdims=True), (LANES, LANES))

    vec = pl.BlockSpec((1, sg), lambda i: (0, 0))
    full = pl.BlockSpec((ng_, LANES, LANES), lambda i: (0, 0, 0))
    return pl.pallas_call(
        body, name="sg_bwd", grid=(t // tm,),
        in_specs=[pl.BlockSpec((tm, sg), lambda i: (i, cb)), pl.BlockSpec((tm, sg), lambda i: (i, cb + 1)),
                  pl.BlockSpec((tm, sg), lambda i: (i, cb + 2)), vec, vec, full, full,
                  pl.BlockSpec((tm, sg), lambda i: (i, 0))],
        out_specs=[pl.BlockSpec((tm, 3 * sg), lambda i: (i, 0)), vec, vec, full, full],
        out_shape=[jax.ShapeDtypeStruct((t, 3 * sg), BF16), jax.ShapeDtypeStruct((1, sg), F32),
                   jax.ShapeDtypeStruct((1, sg), F32), jax.ShapeDtypeStruct((ng_, LANES, LANES), F32),
                   jax.ShapeDtypeStruct((ng_, LANES, LANES), F32)],
        compiler_params=_cp(("arbitrary",)),
    )(p, p, p, lg, lb, w, bias_b, dy)


def out_proj(x, y_dn, y_sg, y_cv, wo, gate):
    t, d = x.shape
    dn, sg, cv = y_dn.shape[1], y_sg.shape[1], y_cv.shape[1]
    dmix = dn + sg + cv
    tm = _tile(t, 256, LANES)

    def body(x_ref, a_ref, b_ref, c_ref, w_ref, g_ref, xn_ref, y_ref, yt_ref):
        a, b, c = a_ref[...], b_ref[...], c_ref[...]
        y = (jnp.dot(a.astype(BF16), w_ref[0:dn, :], preferred_element_type=F32)
             + jnp.dot(b.astype(BF16), w_ref[dn:dn + sg, :], preferred_element_type=F32)
             + jnp.dot(c.astype(BF16), w_ref[dn + sg:, :], preferred_element_type=F32))
        y_ref[...] = y
        xn_ref[...] = x_ref[...] + g_ref[...] * y
        yt_ref[0:dn, :] = a.T.astype(BF16)
        yt_ref[dn:dn + sg, :] = b.T.astype(BF16)
        yt_ref[dn + sg:, :] = c.T.astype(BF16)

    row = lambda w_: pl.BlockSpec((tm, w_), lambda i: (i, 0))
    return pl.pallas_call(
        body, name="out_proj", grid=(t // tm,),
        in_specs=[row(d), row(dn), row(sg), row(cv), pl.BlockSpec((dmix, d), lambda i: (0, 0)),
                  pl.BlockSpec((1, d), lambda i: (0, 0))],
        out_specs=[row(d), row(d), pl.BlockSpec((dmix, tm), lambda i: (0, i))],
        out_shape=[jax.ShapeDtypeStruct((t, d), F32), jax.ShapeDtypeStruct((t, d), F32),
                   jax.ShapeDtypeStruct((dmix, t), BF16)],
        compiler_params=_cp(("parallel",)),
    )(x, y_dn, y_sg, y_cv, wo, gate)


def out_proj_bwd(dxn, y, gate, wo, dn, sg, cv):
    t, d = dxn.shape
    dmix = dn + sg + cv
    tm = _tile(t, 256, LANES)

    def body(dx_ref, y_ref, g_ref, w_ref, da_ref, db_ref, dc_ref, dyb_ref, dg_ref):
        @pl.when(pl.program_id(0) == 0)
        def _():
            dg_ref[...] = jnp.zeros_like(dg_ref)
        dx = dx_ref[...]
        dg_ref[...] += jnp.sum(dx * y_ref[...], axis=0, keepdims=True)
        dyb = (dx * g_ref[...]).astype(BF16)
        dyb_ref[...] = dyb
        dcat = lax.dot_general(dyb, w_ref[...], (((1,), (1,)), ((), ())), preferred_element_type=F32)
        da_ref[...] = dcat[:, 0:dn]
        db_ref[...] = dcat[:, dn:dn + sg]
        dc_ref[...] = dcat[:, dn + sg:]

    row = lambda w_: pl.BlockSpec((tm, w_), lambda i: (i, 0))
    vec = pl.BlockSpec((1, d), lambda i: (0, 0))
    return pl.pallas_call(
        body, name="out_proj_bwd", grid=(t // tm,),
        in_specs=[row(d), row(d), vec, pl.BlockSpec((dmix, d), lambda i: (0, 0))],
        out_specs=[row(dn), row(sg), row(cv), row(d), vec],
        out_shape=[jax.ShapeDtypeStruct((t, dn), F32), jax.ShapeDtypeStruct((t, sg), F32),
                   jax.ShapeDtypeStruct((t, cv), F32), jax.ShapeDtypeStruct((t, d), BF16),
                   jax.ShapeDtypeStruct((1, d), F32)],
        compiler_params=_cp(("arbitrary",)),
    )(dxn, y, gate, wo)


def matmul_acc(name, at, b):
    m, t = at.shape
    n = b.shape[1]
    tm, tn, tk = _tile(m, 1024, LANES), _tile(n, 2432, LANES), _tile(t, 1024, LANES)

    def body(a_ref, b_ref, o_ref):
        @pl.when(pl.program_id(2) == 0)
        def _():
            o_ref[...] = jnp.zeros_like(o_ref)
        o_ref[...] += jnp.dot(a_ref[...], b_ref[...], preferred_element_type=F32)

    return pl.pallas_call(
        body, name=name, grid=(m // tm, n // tn, t // tk),
        in_specs=[pl.BlockSpec((tm, tk), lambda i, j, k: (i, k)), pl.BlockSpec((tk, tn), lambda i, j, k: (k, j))],
        out_specs=pl.BlockSpec((tm, tn), lambda i, j, k: (i, j)),
        out_shape=jax.ShapeDtypeStruct((m, n), F32),
        compiler_params=_cp(("parallel", "parallel", "arbitrary")),
    )(at, b)


def in_proj_bwd(dp, wp, x, dxn, shift, scale, ng):
    t, d = x.shape
    npc = wp.shape[1]
    tm, tk = _tile(t, 512, LANES), _tile(npc, 2432, LANES)

    def mm_body(dp_ref, w_ref, dh_ref):
        @pl.when(pl.program_id(1) == 0)
        def _():
            dh_ref[...] = jnp.zeros_like(dh_ref)
        dh_ref[...] += lax.dot_general(dp_ref[...], w_ref[...], (((1,), (1,)), ((), ())), preferred_element_type=F32)

    dh = pl.pallas_call(
        mm_body, name="in_proj_bwd", grid=(t // tm, npc // tk),
        in_specs=[pl.BlockSpec((tm, tk), lambda i, k: (i, k)), pl.BlockSpec((d, tk), lambda i, k: (0, k))],
        out_specs=pl.BlockSpec((tm, d), lambda i, k: (i, 0)), out_shape=jax.ShapeDtypeStruct((t, d), F32),
        compiler_params=_cp(("parallel", "arbitrary")),
    )(dp, wp)

    tr = _tile(t, 256, 8)

    def norm_body(dh_ref, x_ref, dxn_ref, sh_ref, sc_ref, g_ref, dx_ref, dg_ref, dsc_ref, dsh_ref):
        @pl.when(pl.program_id(0) == 0)
        def _():
            for r in (dg_ref, dsc_ref, dsh_ref):
                r[...] = jnp.zeros_like(r)
        _, vj = jax.vjp(_modnorm, x_ref[...], g_ref[...], sc_ref[...], sh_ref[...])
        dx, dg, dsc, dsh = vj(dh_ref[...])
        dx_ref[...] = dxn_ref[...] + dx
        dg_ref[...] += dg
        dsc_ref[...] += dsc
        dsh_ref[...] += dsh

    vec = pl.BlockSpec((1, d), lambda i: (0, 0))
    row = pl.BlockSpec((tr, d), lambda i: (i, 0))
    return pl.pallas_call(
        norm_body, name="modnorm_bwd", grid=(t // tr,), in_specs=[row, row, row, vec, vec, vec],
        out_specs=[row, vec, vec, vec],
        out_shape=[jax.ShapeDtypeStruct((t, d), F32)] + [jax.ShapeDtypeStruct((1, d), F32)] * 3,
        compiler_params=_cp(("arbitrary",)),
    )(dh, x, dxn, shift, scale, ng)


def loss_head(x, tgt, fg):
    t, d = x.shape
    tm = _tile(t, 512, 8)

    def body(x_ref, t_ref, g_ref, l_ref, dx_ref, dg_ref):
        @pl.when(pl.program_id(0) == 0)
        def _():
            l_ref[...] = jnp.zeros_like(l_ref)
            dg_ref[...] = jnp.zeros_like(dg_ref)
        y, vj = jax.vjp(_rmsnorm, x_ref[...], g_ref[...])
        err = y - t_ref[...]
        part = 0.5 * jnp.sum(jnp.sum(err * err, axis=1, keepdims=True), axis=0, keepdims=True) / d
        l_ref[...] += jnp.broadcast_to(part, l_ref.shape)
        dx, dg = vj(err / d)
        dx_ref[...] = dx
        dg_ref[...] += dg

    row = pl.BlockSpec((tm, d), lambda i: (i, 0))
    vec = pl.BlockSpec((1, d), lambda i: (0, 0))
    return pl.pallas_call(
        body, name="loss_head", grid=(t // tm,), in_specs=[row, row, vec],
        out_specs=[pl.BlockSpec((1, LANES), lambda i: (0, 0)), row, vec],
        out_shape=[jax.ShapeDtypeStruct((1, LANES), F32), jax.ShapeDtypeStruct((t, d), F32),
                   jax.ShapeDtypeStruct((1, d), F32)],
        compiler_params=_cp(("arbitrary",)),
    )(x, tgt, fg)


def adamw(name, w, g, m, v):
    r, c = w.shape
    tr = _tile(r, 256, 8) if r % 8 == 0 else r

    def body(w_ref, g_ref, m_ref, v_ref, d_ref, mo_ref, vo_ref):
        d_ref[...], mo_ref[...], vo_ref[...] = _adamw_math(w_ref[...], g_ref[...], m_ref[...], v_ref[...])

    blk = pl.BlockSpec((tr, c), lambda i: (i, 0))
    return pl.pallas_call(
        body, name=name, grid=(r // tr,), in_specs=[blk] * 4, out_specs=[blk] * 3,
        out_shape=[jax.ShapeDtypeStruct((r, c), F32)] * 3, compiler_params=_cp(("parallel",)),
    )(w, g, m, v)


def ada_fwd(c_all, w_ada, b_loc):
    nl, d, cols = w_ada.shape
    nb = c_all.shape[0]
    tn = _tile(cols, 512, LANES)

    def body(c_ref, w_ref, b_ref, o_ref):
        ca = _silu(c_ref[...]).astype(BF16)
        o_ref[0] = jnp.dot(ca, w_ref[0].astype(BF16), preferred_element_type=F32) + b_ref[0]

    return pl.pallas_call(
        body, name="ada_fwd", grid=(nl, cols // tn),
        in_specs=[pl.BlockSpec((nb, d), lambda l, j: (0, 0)), pl.BlockSpec((1, d, tn), lambda l, j: (l, 0, j)),
                  pl.BlockSpec((1, 1, tn), lambda l, j: (l, 0, j))],
        out_specs=pl.BlockSpec((1, nb, tn), lambda l, j: (l, 0, j)),
        out_shape=jax.ShapeDtypeStruct((nl, nb, cols), F32),
        compiler_params=_cp(("parallel", "parallel")),
    )(c_all, w_ada, b_loc)


def ada_bwd(c_all_t, dmod_loc, w, m, v):
    nl, d, cols = w.shape
    nb = c_all_t.shape[1]
    tr = _tile(d, 256, 8)

    def body(c_ref, dm_ref, w_ref, m_ref, v_ref, g_ref, d_ref, mo_ref, vo_ref):
        ca = _silu(c_ref[...])
        dm = dm_ref[0]
        g = _lane_col(ca, 0) * dm[0:1, :]
        for b in range(1, nb):
            g = g + _lane_col(ca, b) * dm[b:b + 1, :]
        g_ref[0] = g
        d_ref[0], mo_ref[0], vo_ref[0] = _adamw_math(w_ref[0], g, m_ref[0], v_ref[0])

    blk = pl.BlockSpec((1, tr, cols), lambda l, i: (l, i, 0))
    return pl.pallas_call(
        body, name="ada_bwd", grid=(nl, d // tr),
        in_specs=[pl.BlockSpec((tr, nb), lambda l, i: (i, 0)), pl.BlockSpec((1, nb, cols), lambda l, i: (l, 0, 0)),
                  blk, blk, blk],
        out_specs=[blk] * 4, out_shape=[jax.ShapeDtypeStruct((nl, d, cols), F32)] * 4,
        compiler_params=_cp(("parallel", "parallel")),
    )(c_all_t, dmod_loc, w, m, v)


def sum8(g):
    _, r, c = g.shape
    tr = _tile(r, 256, 8)

    def body(g_ref, o_ref):
        acc = g_ref[0]
        for k in range(1, NDEV):
            acc = acc + g_ref[k]
        o_ref[...] = acc

    return pl.pallas_call(
        body, name="sum8", grid=(r // tr,), in_specs=[pl.BlockSpec((NDEV, tr, c), lambda i: (0, i, 0))],
        out_specs=pl.BlockSpec((tr, c), lambda i: (i, 0)), out_shape=jax.ShapeDtypeStruct((r, c), F32),
        compiler_params=_cp(("parallel",)),
    )(g)


def pair_sum(cflag, g, r1):
    _, ns, r, c = g.shape
    tr = _tile(r, 256, 8)

    def body(cf_ref, g0_ref, g1_ref, r_ref, ob_ref):
        keep = jnp.where(cf_ref[0:1, 0:1] == 0.0, g0_ref[0], g1_ref[0])
        ob_ref[...] = (keep + r_ref[...]).astype(BF16)

    blk = pl.BlockSpec((1, tr, c), lambda s, i: (s, i, 0))
    return pl.pallas_call(
        body, name="pair_sum", grid=(ns, r // tr),
        in_specs=[pl.BlockSpec((1, LANES), lambda s, i: (0, 0)), pl.BlockSpec((1, 1, tr, c), lambda s, i: (0, s, i, 0)),
                  pl.BlockSpec((1, 1, tr, c), lambda s, i: (1, s, i, 0)), blk], out_specs=blk,
        out_shape=jax.ShapeDtypeStruct((ns, r, c), BF16), compiler_params=_cp(("parallel", "parallel")),
    )(cflag, g, g, r1)


def chip_sum(own, r1, r2):
    r, c = own.shape
    tr = _tile(r, 256, 8)

    def body(g_ref, r1_ref, r2_ref, o_ref):
        acc = g_ref[...] + r1_ref[...]
        for k in range(NCHIP - 1):
            acc = acc + r2_ref[k].astype(F32)
        o_ref[...] = acc

    blk = pl.BlockSpec((tr, c), lambda i: (i, 0))
    return pl.pallas_call(
        body, name="chip_sum", grid=(r // tr,),
        in_specs=[blk, blk, pl.BlockSpec((NCHIP - 1, tr, c), lambda i: (0, i, 0))], out_specs=blk,
        out_shape=jax.ShapeDtypeStruct((r, c), F32), compiler_params=_cp(("parallel",)),
    )(own, r1, r2)


def adamw_halves(name, cflag, w, own, recv, m, v):
    _, r, c = w.shape
    tr = _tile(r, 256, 8)

    def body(cf_ref, w_ref, a_ref, b_ref, m_ref, v_ref, g_ref, d_ref, mo_ref, vo_ref):
        is_own = cf_ref[0:1, 0:1] == pl.program_id(0).astype(F32)
        g = jnp.where(is_own, a_ref[...], b_ref[...])
        g_ref[0] = g
        d_ref[0], mo_ref[0], vo_ref[0] = _adamw_math(w_ref[0], g, m_ref[0], v_ref[0])

    blk = pl.BlockSpec((1, tr, c), lambda h, i: (h, i, 0))
    hlf = pl.BlockSpec((tr, c), lambda h, i: (i, 0))
    return pl.pallas_call(
        body, name=name, grid=(2, r // tr),
        in_specs=[pl.BlockSpec((1, LANES), lambda h, i: (0, 0)), blk, hlf, hlf, blk, blk], out_specs=[blk] * 4,
        out_shape=[jax.ShapeDtypeStruct(w.shape, F32)] * 4, compiler_params=_cp(("parallel", "parallel")),
    )(cflag, w, own, recv, m, v)


def _me():
    return lax.axis_index("x"), lax.axis_index("y"), lax.axis_index("c")


_FLIPS = ((1, 0), (0, 1), (1, 1))


def all_gather8(v):
    m_per, n = v.shape

    def body(x_ref, out_ref, send_sems, recv_sems, local_sem):
        x, y, c = _me()
        me, sibling = (x, y, c), (x, y, 1 - c)
        chips = [(x ^ fx, y ^ fy) for fx, fy in _FLIPS]

        def rows(px, py, pc):
            return out_ref.at[pl.ds((4 * px + 2 * py + pc) * m_per, m_per), :]

        def copy(k, block, to, src=None):
            return pltpu.make_async_remote_copy(
                src_ref=rows(*block) if src is None else src, dst_ref=rows(*block),
                send_sem=send_sems.at[k], recv_sem=recv_sems.at[k], device_id=to, device_id_type=MESH)

        mine = pltpu.make_async_copy(x_ref, rows(*me), local_sem)
        mine.start()
        first = [copy(0, me, sibling, src=x_ref)]
        first += [copy(1 + j, me, (*chip, c), src=x_ref) for j, chip in enumerate(chips)]
        for cp in first:
            cp.start()
        passed = [copy(4 + j, (*chip, c), sibling) for j, chip in enumerate(chips)]
        for j, chip in enumerate(chips):
            copy(1 + j, (*chip, c), me).wait_recv()
            passed[j].start()
        copy(0, sibling, me).wait_recv()
        for j, chip in enumerate(chips):
            copy(4 + j, (*chip, 1 - c), me).wait_recv()
        for cp in first + passed:
            cp.wait_send()
        mine.wait()

    return pl.pallas_call(
        body, name="all_gather8", out_shape=jax.ShapeDtypeStruct((NDEV * m_per, n), v.dtype),
        in_specs=[pl.BlockSpec(memory_space=pltpu.VMEM)], out_specs=pl.BlockSpec(memory_space=pltpu.VMEM),
        scratch_shapes=[pltpu.SemaphoreType.DMA((7,)), pltpu.SemaphoreType.DMA((7,)), pltpu.SemaphoreType.DMA],
        compiler_params=pltpu.CompilerParams(vmem_limit_bytes=VMEM_LIMIT),
    )(v)


def gather_weights(ws):
    na = len(ws)
    ncp = 9

    def body(*refs):
        srcs, outs = refs[:na], refs[na:2 * na]
        send_sems, recv_sems = refs[2 * na:]
        x, y, c = _me()
        chip, cx, cy, cd = 2 * x + y, 2 * (1 - x) + y, 2 * x + 1 - y, 2 * (1 - x) + 1 - y
        nx, ny, sibling = (1 - x, y, c), (x, 1 - y, c), (x, y, 1 - c)

        def copy(a, k, src, dst, to):
            return pltpu.make_async_remote_copy(
                src_ref=src, dst_ref=dst, send_sem=send_sems.at[a * ncp + k], recv_sem=recv_sems.at[a * ncp + k],
                device_id=to, device_id_type=MESH)

        def land(a, k, ch, q):
            return copy(a, k, srcs[a].at[c, q], outs[a].at[ch, c, q], sibling)

        sends = []
        for a in range(na):
            for k, q, to in ((0, 0, nx), (2, 1, ny), (1, 1, nx), (3, 0, ny)):
                sends.append(copy(a, k, srcs[a].at[c, q], outs[a].at[chip, c, q], to))
                sends[-1].start()
        for a in range(na):
            land(a, 0, cx, 0).wait_recv()
            sends.append(copy(a, 4, outs[a].at[cx, c, 0], outs[a].at[cx, c, 0], ny))
            sends[-1].start()
            land(a, 2, cy, 1).wait_recv()
            sends.append(copy(a, 5, outs[a].at[cy, c, 1], outs[a].at[cy, c, 1], nx))
            sends[-1].start()
        for a in range(na):
            for k, ch, q in ((1, cx, 1), (3, cy, 0), (4, cd, 0), (5, cd, 1)):
                land(a, k, ch, q).wait_recv()
            for j, ch in enumerate((cx, cy, cd)):
                sends.append(copy(a, 6 + j, outs[a].at[ch, c], outs[a].at[ch, c], sibling))
                sends[-1].start()
        for a in range(na):
            for j, ch in enumerate((cx, cy, cd)):
                copy(a, 6 + j, outs[a].at[ch, c], outs[a].at[ch, 1 - c], sibling).wait_recv()
        for cp in sends:
            cp.wait_send()

    return pl.pallas_call(
        body, name="gather_weights",
        out_shape=[jax.ShapeDtypeStruct((NCHIP,) + w.shape, w.dtype) for w in ws],
        in_specs=[ANY] * na, out_specs=[ANY] * na,
        scratch_shapes=[pltpu.SemaphoreType.DMA((ncp * na,)), pltpu.SemaphoreType.DMA((ncp * na,))],
    )(*ws)


def sibling_swap(name, gs, other_half=False):
    na = len(gs)

    def body(*refs):
        srcs, outs = refs[:na], refs[na:2 * na]
        send_sems, recv_sems = refs[2 * na:]
        x, y, c = _me()
        cps = [pltpu.make_async_remote_copy(
            src_ref=srcs[a].at[1 - c] if other_half else srcs[a], dst_ref=outs[a], send_sem=send_sems.at[a],
            recv_sem=recv_sems.at[a], device_id=(x, y, 1 - c), device_id_type=MESH) for a in range(na)]
        for cp in cps:
            cp.start()
        for cp in cps:
            cp.wait()

    return pl.pallas_call(
        body, name=name, out_shape=[jax.ShapeDtypeStruct(g.shape[1:] if other_half else g.shape, g.dtype) for g in gs],
        in_specs=[ANY] * na, out_specs=[ANY] * na,
        scratch_shapes=[pltpu.SemaphoreType.DMA((na,)), pltpu.SemaphoreType.DMA((na,))],
    )(*gs)


def chip_exchange(ps):
    na = len(ps)

    def body(*refs):
        srcs, outs = refs[:na], refs[na:2 * na]
        send_sems, recv_sems = refs[2 * na:]
        x, y, c = _me()
        cps = []
        for a in range(na):
            for j, (fx, fy) in enumerate(_FLIPS):
                px, py = x ^ fx, y ^ fy
                cps.append(pltpu.make_async_remote_copy(
                    src_ref=srcs[a].at[2 * px + py], dst_ref=outs[a].at[j], send_sem=send_sems.at[a * 3 + j],
                    recv_sem=recv_sems.at[a * 3 + j], device_id=(px, py, c), device_id_type=MESH))
        for cp in cps:
            cp.start()
        for cp in cps:
            cp.wait()

    return pl.pallas_call(
        body, name="chip_exchange",
        out_shape=[jax.ShapeDtypeStruct((NCHIP - 1,) + p_.shape[1:], p_.dtype) for p_ in ps],
        in_specs=[ANY] * na, out_specs=[ANY] * na,
        scratch_shapes=[pltpu.SemaphoreType.DMA((3 * na,)), pltpu.SemaphoreType.DMA((3 * na,))],
    )(*ps)


class _Cfg:
    def __init__(self, x, a_log, sg_w, cv_ln_g, cv_w, conv_qkv):
        self.t, self.d = x.shape[1], x.shape[2]
        self.nl, self.h = a_log.shape
        self.dn = self.h * LANES
        self.g = sg_w.shape[1]
        self.sg = self.g * LANES
        self.cv = cv_ln_g.shape[1]
        self.kc = cv_w.shape[1]
        self.k4 = conv_qkv.shape[1]
        self.o_z = 3 * self.dn
        self.o_sg = 4 * self.dn
        self.o_cv = self.o_sg + 3 * self.sg
        self.o_ba = self.o_cv + 3 * self.cv
        self.npc = self.o_ba + LANES
        self.d_in = self.o_ba + 2 * self.h
        self.dmix = self.dn + self.sg + self.cv
        self.hb_fwd = _tile(self.h, 8, 1)
        self.hb_bwd = _tile(self.h, 8, 1)


def _runs(cfg):
    dn, h = cfg.dn, cfg.h
    runs = [(part * dn + hd * LANES, hd * 3 * LANES + part * LANES, LANES) for part in range(3) for hd in range(h)]
    return runs + [(3 * dn, 3 * dn, dn), (4 * dn, cfg.o_ba, 2 * h), (4 * dn + 2 * h, 4 * dn, cfg.o_ba - 4 * dn)]


def _assemble_perm(cfg, shards):
    cols = shards[0].shape[-1]
    pieces = []
    for nat, _, wdt in sorted(_runs(cfg), key=lambda r_: r_[1]):
        a = nat
        while a < nat + wdt:
            s = a // cols
            b = min(nat + wdt, (s + 1) * cols)
            pieces.append(shards[s][..., a - s * cols:b - s * cols])
            a = b
    pieces.append(jnp.zeros(shards[0].shape[:-1] + (cfg.npc - cfg.o_ba - 2 * cfg.h,), shards[0].dtype))
    return jnp.concatenate(pieces, axis=-1)


def _natural_pieces(cfg, s, cols):
    lo, hi = s * cols, (s + 1) * cols
    pieces = []
    for nat, perm, wdt in sorted(_runs(cfg)):
        a, b = max(nat, lo), min(nat + wdt, hi)
        if a < b:
            pieces.append((perm + a - nat, perm + b - nat))
    return pieces


def _layer_fwd(cfg, x, mod, lw):
    shift, scale, gate = mod
    p, ht = in_proj(x, shift, scale, lw["norm_g"], lw["wp"])
    qk_post = [_qk_post, _qk_post, _v_post]
    qkv = conv_fwd("dn_pre_fwd", cfg.k4, HALO4, lambda a: a, [(p, 0)], lw["conv_qkv"], qk_post, [], [],
                   3 * cfg.dn, 3 * LANES)
    wy = dn_wy_fwd(qkv, p, cfg.o_ba, lw["alog_b"], lw["dtb_b"], cfg.h)
    y_dn, ss = dn_seq_fwd(*wy[:6], p, cfg.o_z, lw["dn_norm_g"], cfg.h, cfg.hb_fwd)
    y_sg = sg_fwd(p, cfg.o_sg, cfg.sg, lw["sg_ln_g"], lw["sg_ln_b"], lw["sg_w"], lw["sg_bias_b"])
    cv_post = [_cv_post] * (cfg.cv // LANES)
    y_cv = conv_fwd("cv_fwd", cfg.kc, HALO31, _glu, [(p, cfg.o_cv), (p, cfg.o_cv + cfg.cv)], lw["cv_w"], cv_post,
                    [(p, cfg.o_cv + 2 * cfg.cv)], [lw["cv_b"], lw["cv_ln_g"], lw["cv_ln_b"]], cfg.cv, cfg.cv)
    xn, y, yt = out_proj(x, y_dn, y_sg, y_cv, lw["wo"], gate)
    return xn, dict(x=x, p=p, ht=ht, qkv=qkv, wy=wy, ss=ss, y=y, yt=yt)


def _layer_bwd(cfg, dxn, mod, lw, sv):
    shift, scale, gate = mod
    p = sv["p"]
    d_dn, d_sg, d_cv, dyb, dgate = out_proj_bwd(dxn, sv["y"], gate, lw["wo"], cfg.dn, cfg.sg, cfg.cv)
    g_wo = matmul_acc("w_out_grad", sv["yt"], dyb)
    cv_post = [_cv_post] * (cfg.cv // LANES)
    dcv, g_cvw, (g_cvb, g_cvlg, g_cvlb) = conv_bwd(
        "cv_bwd", cfg.kc, HALO31, _glu, [(p, cfg.o_cv), (p, cfg.o_cv + cfg.cv)], lw["cv_w"], cv_post,
        [(p, cfg.o_cv + 2 * cfg.cv)], [lw["cv_b"], lw["cv_ln_g"], lw["cv_ln_b"]], d_cv, cfg.cv, cfg.cv, tm_pref=256)
    dsg, g_sglg, g_sglb, g_sgw, g_sgb = sg_bwd(p, cfg.o_sg, cfg.sg, lw["sg_ln_g"], lw["sg_ln_b"], lw["sg_w"],
                                               lw["sg_bias_b"], d_sg)
    *dwy, dz, g_dng = dn_seq_bwd(*sv["wy"][:6], p, cfg.o_z, lw["dn_norm_g"], sv["ss"], d_dn, cfg.h, cfg.hb_bwd)
    dqkv, dba, g_al, g_dt = dn_wy_bwd(sv["qkv"], p, cfg.o_ba, lw["alog_b"], lw["dtb_b"], sv["wy"][6], *dwy, cfg.h)
    qk_post = [_qk_post, _qk_post, _v_post]
    dqkv_pre, g_cq, _ = conv_bwd("dn_pre_bwd", cfg.k4, HALO4, lambda a: a, [(p, 0)], lw["conv_qkv"], qk_post, [], [],
                                 dqkv, 3 * cfg.dn, 3 * LANES)
    dp = jnp.concatenate([dqkv_pre, dz.astype(BF16), dsg, dcv, dba.astype(BF16)], axis=1)
    g_wp = matmul_acc("w_in_grad", sv["ht"], dp)
    dx, g_ng, dscale, dshift = in_proj_bwd(dp, lw["wp"], sv["x"], dxn, shift, scale, lw["norm_g"])
    grads = dict(norm_g=g_ng, conv_qkv=g_cq, a_log=g_al[:cfg.h, 0], dt_bias=g_dt[:cfg.h, 0], dn_norm_g=g_dng,
                 sg_ln_g=g_sglg, sg_ln_b=g_sglb, sg_w=g_sgw, sg_b=g_sgb[:, :, 0], cv_w=g_cvw, cv_b=g_cvb,
                 cv_ln_g=g_cvlg, cv_ln_b=g_cvlb, wp=g_wp, wo=g_wo)
    return dx, grads, (dshift, dscale, dgate)


def _local_step(cfg, xs, tgt, mods, lws, fg):
    nl = len(lws)
    saved = []
    for l in range(nl):
        xs, sv = _layer_fwd(cfg, xs, mods[l], lws[l])
        saved.append(sv)
    loss_b, dx, g_fg = loss_head(xs, tgt, fg)
    lg = [None] * nl
    dmods = [None] * nl
    for l in reversed(range(nl)):
        dx, lg[l], dmods[l] = _layer_bwd(cfg, dx, mods[l], lws[l], saved[l])
    return loss_b, dx, g_fg, lg, dmods


SMALL = ("norm_g", "conv_qkv", "a_log", "dt_bias", "dn_norm_g", "sg_ln_g", "sg_ln_b", "sg_w", "sg_b", "cv_w",
         "cv_b", "cv_ln_g", "cv_ln_b", "final_g", "b_ada")
PACK_N = 1024


def _pack(arrs):
    flat = jnp.concatenate([a.reshape(-1).astype(F32) for a in arrs])
    rows = -(-flat.shape[0] // PACK_N)
    rows = -(-rows // 8) * 8
    return jnp.pad(flat, (0, rows * PACK_N - flat.shape[0])).reshape(rows, PACK_N)


def _unpack(buf, shapes):
    flat = buf.reshape(-1)
    out, o = [], 0
    for s in shapes:
        n = 1
        for d_ in s:
            n *= d_
        out.append(flat[o:o + n].reshape(s))
        o += n
    return out


def kernel(x, c, norm_g, w_ada, b_ada, w_in, conv_qkv, a_log, dt_bias, dn_norm_g, sg_ln_g, sg_ln_b, sg_w, sg_b, cv_w, cv_b, cv_ln_g, cv_ln_b, w_out, final_g, loss_target, m_norm_g, m_w_ada, m_b_ada, m_w_in, m_conv_qkv, m_a_log, m_dt_bias, m_dn_norm_g, m_sg_ln_g, m_sg_ln_b, m_sg_w, m_sg_b, m_cv_w, m_cv_b, m_cv_ln_g, m_cv_ln_b, m_w_out, m_final_g, v_norm_g, v_w_ada, v_b_ada, v_w_in, v_conv_qkv, v_a_log, v_dt_bias, v_dn_norm_g, v_sg_ln_g, v_sg_ln_b, v_sg_w, v_sg_b, v_cv_w, v_cv_b, v_cv_ln_g, v_cv_ln_b, v_w_out, v_final_g):
    cfg = _Cfg(x, a_log, sg_w, cv_ln_g, cv_w, conv_qkv)
    nl, d, t, h = cfg.nl, cfg.d, cfg.t, cfg.h
    lh = nl // 2
    ax, ay, ac = _me()
    chip = 2 * ax + ay
    dev = 2 * chip + ac
    wts = dict(norm_g=norm_g, w_ada=w_ada, b_ada=b_ada, w_in=w_in, conv_qkv=conv_qkv, a_log=a_log, dt_bias=dt_bias,
               dn_norm_g=dn_norm_g, sg_ln_g=sg_ln_g, sg_ln_b=sg_ln_b, sg_w=sg_w, sg_b=sg_b, cv_w=cv_w, cv_b=cv_b,
               cv_ln_g=cv_ln_g, cv_ln_b=cv_ln_b, w_out=w_out, final_g=final_g)
    mom = dict(norm_g=m_norm_g, w_ada=m_w_ada, b_ada=m_b_ada, w_in=m_w_in, conv_qkv=m_conv_qkv, a_log=m_a_log,
               dt_bias=m_dt_bias, dn_norm_g=m_dn_norm_g, sg_ln_g=m_sg_ln_g, sg_ln_b=m_sg_ln_b, sg_w=m_sg_w,
               sg_b=m_sg_b, cv_w=m_cv_w, cv_b=m_cv_b, cv_ln_g=m_cv_ln_g, cv_ln_b=m_cv_ln_b, w_out=m_w_out,
               final_g=m_final_g)
    vel = dict(norm_g=v_norm_g, w_ada=v_w_ada, b_ada=v_b_ada, w_in=v_w_in, conv_qkv=v_conv_qkv, a_log=v_a_log,
               dt_bias=v_dt_bias, dn_norm_g=v_dn_norm_g, sg_ln_g=v_sg_ln_g, sg_ln_b=v_sg_ln_b, sg_w=v_sg_w,
               sg_b=v_sg_b, cv_w=v_cv_w, cv_b=v_cv_b, cv_ln_g=v_cv_ln_g, cv_ln_b=v_cv_ln_b, w_out=v_w_out,
               final_g=v_final_g)
    ada_cols = w_ada.shape[2]
    in_cols = w_in.shape[2]
    out_rows = w_out.shape[1]
    cq_cols = conv_qkv.shape[2]
    cvw_cols = cv_w.shape[2]

    c_all = all_gather8(jnp.pad(c, ((0, 7), (0, 0)))).reshape(NDEV, 8, d)[:, 0, :]
    b_loc = lax.dynamic_slice_in_dim(b_ada, chip * ada_cols, ada_cols, axis=1)[:, None, :]
    mod_part = ada_fwd(c_all, w_ada, b_loc)
    mod_all = all_gather8(mod_part.reshape(nl * NDEV, ada_cols)).reshape(NDEV, nl, NDEV, ada_cols)
    mod_me = lax.dynamic_index_in_dim(mod_all[0::2], dev, axis=2, keepdims=False)
    mod_me = jnp.moveaxis(mod_me, 0, 1).reshape(nl, 3, 1, d)

    win_b = w_in.astype(BF16).reshape(2, 2, lh * d // 2, in_cols)
    wout_b = w_out.astype(BF16).reshape(2, 2, lh * out_rows // 2, d)
    win_all, wout_all = gather_weights([win_b, wout_b])
    win_all = lax.dynamic_update_index_in_dim(win_all, win_b, chip, axis=0)
    wout_all = lax.dynamic_update_index_in_dim(wout_all, wout_b, chip, axis=0)
    win_all = win_all.reshape(NCHIP, nl, d, in_cols)
    wp_all = [_assemble_perm(cfg, [win_all[s, l] for s in range(NCHIP)]) for l in range(nl)]
    wo_all = jnp.moveaxis(wout_all.reshape(NCHIP, nl, out_rows, d), 0, 1).reshape(nl, NCHIP * out_rows, d)

    cq_all = all_gather8(conv_qkv.reshape(nl * cfg.k4, cq_cols)).reshape(NDEV, nl, cfg.k4, cq_cols)[0::2]
    cq_full = jnp.moveaxis(cq_all, 0, 2).reshape(nl, cfg.k4, NCHIP * cq_cols)
    cq_perm = _perm_cols_qkv(cfg, cq_full)
    kcp = -(-cfg.kc // 8) * 8
    cvw_all = all_gather8(jnp.pad(cv_w, ((0, 0), (0, kcp - cfg.kc), (0, 0))).reshape(nl * kcp, cvw_cols))
    cvw_all = cvw_all.reshape(NDEV, nl, kcp, cvw_cols)[0::2]
    cvw_full = jnp.moveaxis(cvw_all, 0, 2).reshape(nl, kcp, NCHIP * cvw_cols)[:, :cfg.kc]

    hp = -(-h // 8) * 8
    lws = []
    for l in range(nl):
        lws.append(dict(
            norm_g=norm_g[l][None], wp=wp_all[l], wo=wo_all[l], conv_qkv=cq_perm[l],
            alog_b=jnp.pad(jnp.broadcast_to(a_log[l][:, None], (h, LANES)), ((0, hp - h), (0, 0))),
            dtb_b=jnp.pad(jnp.broadcast_to(dt_bias[l][:, None], (h, LANES)), ((0, hp - h), (0, 0))),
            dn_norm_g=dn_norm_g[l][None], sg_ln_g=sg_ln_g[l][None], sg_ln_b=sg_ln_b[l][None], sg_w=sg_w[l],
            sg_bias_b=jnp.broadcast_to(sg_b[l][:, :, None], (cfg.g, LANES, LANES)),
            cv_w=cvw_full[l], cv_b=cv_b[l][None], cv_ln_g=cv_ln_g[l][None], cv_ln_b=cv_ln_b[l][None]))

    mods = [(mod_me[l, 0], mod_me[l, 1], mod_me[l, 2]) for l in range(nl)]
    loss_b, dx, g_fg, lg, dmods = _local_step(cfg, x[0], loss_target[0], mods, lws, final_g[None])
    grad_x = dx[None]

    dmod = jnp.stack([jnp.concatenate(dm, axis=1)[0] for dm in dmods])
    stack = lambda k: jnp.stack([g_[k] for g_ in lg])
    small_local = [stack(k).reshape(wts_shape) for k, wts_shape in
                   (("norm_g", (nl, d)), ("conv_qkv", (nl, cfg.k4, 3 * cfg.dn)), ("a_log", (nl, h)),
                    ("dt_bias", (nl, h)), ("dn_norm_g", (nl, LANES)), ("sg_ln_g", (nl, cfg.sg)),
                    ("sg_ln_b", (nl, cfg.sg)), ("sg_w", (nl, cfg.g, LANES, LANES)), ("sg_b", (nl, cfg.g, LANES)),
                    ("cv_w", (nl, cfg.kc, cfg.cv)), ("cv_b", (nl, cfg.cv)), ("cv_ln_g", (nl, cfg.cv)),
                    ("cv_ln_b", (nl, cfg.cv)))]
    small_local[1] = _unperm_cols_qkv(cfg, small_local[1])
    small_local += [g_fg[0], dmod, loss_b[0, 0:1]]
    shapes = [a.shape for a in small_local]
    packed = _pack(small_local)
    rows = packed.shape[0]
    gathered = all_gather8(packed).reshape(NDEV, rows, PACK_N)
    summed = _unpack(sum8(gathered), shapes)
    sgrads = dict(zip(SMALL, summed[:15]))
    loss = summed[15][0]
    sgrads["conv_qkv"] = lax.dynamic_slice_in_dim(sgrads["conv_qkv"], chip * cq_cols, cq_cols, axis=2)
    sgrads["cv_w"] = lax.dynamic_slice_in_dim(sgrads["cv_w"], chip * cvw_cols, cvw_cols, axis=2)

    off = sum(math.prod(s) for s in shapes[:14])
    dmod_all = gathered.reshape(NDEV, rows * PACK_N)[:, off:off + nl * 3 * d].reshape(NDEV, nl, 3 * d)
    dmod_loc = jnp.moveaxis(lax.dynamic_slice_in_dim(dmod_all, chip * ada_cols, ada_cols, axis=2), 0, 1)
    g_wada, d_wada, nm_wada, nv_wada = ada_bwd(c_all.T, dmod_loc, w_ada, m_w_ada, v_w_ada)

    shard = lambda g_, s: jnp.concatenate([g_[:, a:b] for a, b in _natural_pieces(cfg, s, in_cols)], axis=-1)
    g_in = jnp.stack([jnp.stack([jnp.concatenate([shard(lg[hh * lh + j]["wp"], s) for j in range(lh)], axis=0)
                                 for s in range(NCHIP)]) for hh in range(2)])
    g_wo = jnp.stack([g_["wo"] for g_ in lg]).reshape(2, lh, NCHIP, out_rows, d)
    g_out = jnp.moveaxis(g_wo, 2, 1).reshape(2, NCHIP, lh * out_rows, d)
    cflag = jnp.full((1, LANES), ac, F32)
    r1_in, r1_out = sibling_swap("swap_halves", [g_in, g_out], other_half=True)
    r2_in, r2_out = chip_exchange([pair_sum(cflag, g_in, r1_in), pair_sum(cflag, g_out, r1_out)])
    mine = lambda g_: lax.dynamic_index_in_dim(g_, chip, axis=0, keepdims=False)
    keep = lambda g_: lax.dynamic_index_in_dim(g_, ac, axis=0, keepdims=False)
    h_in = chip_sum(mine(keep(g_in)), mine(r1_in), r2_in)
    h_out = chip_sum(mine(keep(g_out)), mine(r1_out), r2_out)
    o_in, o_out = sibling_swap("join_halves", [h_in, h_out])

    v3 = lambda a, r_, c_: a.reshape(2, lh * r_, c_)
    grad_w_in, d_in_, nm_in, nv_in = adamw_halves("adamw_w_in", cflag, v3(w_in, d, in_cols), h_in, o_in,
                                                  v3(m_w_in, d, in_cols), v3(v_w_in, d, in_cols))
    grad_w_out, d_out_, nm_out, nv_out = adamw_halves("adamw_w_out", cflag, v3(w_out, out_rows, d), h_out, o_out,
                                                      v3(m_w_out, out_rows, d), v3(v_w_out, out_rows, d))
    grad_w_in = grad_w_in.reshape(w_in.shape)
    grad_w_out = grad_w_out.reshape(w_out.shape)
    sshapes = [wts[k].shape for k in SMALL]
    pk = lambda dct: _pack([dct[k] for k in SMALL])
    d_s, m_s, v_s = adamw("adamw_small", pk(wts), pk(sgrads), pk(mom), pk(vel))
    d_small = dict(zip(SMALL, _unpack(d_s, sshapes)))
    m_small = dict(zip(SMALL, _unpack(m_s, sshapes)))
    v_small = dict(zip(SMALL, _unpack(v_s, sshapes)))

    grads = dict(sgrads, w_ada=g_wada, w_in=grad_w_in, w_out=grad_w_out)
    deltas = dict(d_small, w_ada=d_wada, w_in=d_in_.reshape(w_in.shape), w_out=d_out_.reshape(w_out.shape))
    new_m = dict(m_small, w_ada=nm_wada, w_in=nm_in.reshape(w_in.shape), w_out=nm_out.reshape(w_out.shape))
    new_v = dict(v_small, w_ada=nv_wada, w_in=nv_in.reshape(w_in.shape), w_out=nv_out.reshape(w_out.shape))
    order = ("norm_g", "w_ada", "b_ada", "w_in", "conv_qkv", "a_log", "dt_bias", "dn_norm_g", "sg_ln_g", "sg_ln_b",
             "sg_w", "sg_b", "cv_w", "cv_b", "cv_ln_g", "cv_ln_b", "w_out", "final_g")
    return (loss, grad_x, *[grads[k] for k in order], *[deltas[k] for k in order], *[new_m[k] for k in order],
            *[new_v[k] for k in order])


def _perm_cols_qkv(cfg, w):
    lead = w.shape[:-1]
    return jnp.moveaxis(w.reshape(lead + (3, cfg.h, LANES)), -3, -2).reshape(lead + (3 * cfg.dn,))


def _unperm_cols_qkv(cfg, w):
    lead = w.shape[:-1]
    return jnp.moveaxis(w.reshape(lead + (cfg.h, 3, LANES)), -3, -2).reshape(lead + (3 * cfg.dn,))
```

```python
import functools
import math

import jax
import jax.numpy as jnp
from jax import lax
from jax.experimental import pallas as pl
from jax.experimental.pallas import tpu as pltpu

F32 = jnp.float32
BF16 = jnp.bfloat16
EPS = 1e-6
LN_EPS = 1e-5
LANES = 128
CHUNK = 64
SUBLANES = 8
HALO4 = 8
HALO31 = 32
NCHIP = 4
NDEV = 8
VMEM_LIMIT = 56 * 2 ** 20
ADAM_LR, ADAM_B1, ADAM_B2, ADAM_EPS, ADAM_WD, ADAM_STEP = 0.001, 0.9, 0.999, 1e-08, 0.01, 10
MESH = pl.DeviceIdType.MESH
ANY = pl.BlockSpec(memory_space=pl.ANY)


def _cp(sem=None, vmem=VMEM_LIMIT):
    return pltpu.CompilerParams(dimension_semantics=sem, vmem_limit_bytes=vmem)


def _tile(n, pref, mult):
    t = min(n, pref) // mult * mult
    while t > 0 and n % t:
        t -= mult
    return t if t > 0 else n


def _split(a):
    hi = a.astype(BF16)
    return hi, (a - hi.astype(F32)).astype(BF16)


def _raw_dot(a, b, ca, cb, hi):
    dn = (((ca,), (cb,)), ((), ()))
    if hi:
        ah, al = _split(a.astype(F32))
        bh, bl = _split(b.astype(F32))
        d3 = lambda x, y: lax.dot_general(x, y, dn, preferred_element_type=F32)
        return d3(ah, bh) + (d3(al, bh) + d3(ah, bl))
    return lax.dot_general(a.astype(BF16), b.astype(BF16), dn, preferred_element_type=F32)


@functools.partial(jax.custom_vjp, nondiff_argnums=(2, 3, 4))
def bdot(a, b, ca, cb, hi):
    return _raw_dot(a, b, ca, cb, hi)


def _bdot_fwd(a, b, ca, cb, hi):
    return _raw_dot(a, b, ca, cb, hi), (a, b)


def _bdot_bwd(ca, cb, hi, res, ct):
    a, b = res
    fa, fb = 1 - ca, 1 - cb
    da = _raw_dot(ct, b, 1, fb, hi) if ca == 1 else _raw_dot(b, ct, fb, 1, hi)
    db = _raw_dot(a, ct, fa, 0, hi) if cb == 0 else _raw_dot(ct, a, 0, fa, hi)
    return da.astype(a.dtype), db.astype(b.dtype)


bdot.defvjp(_bdot_fwd, _bdot_bwd)


def _sigmoid(x):
    return jax.nn.sigmoid(x)


def _silu(x):
    return x * _sigmoid(x)


def _gelu(x):
    return 0.5 * x * (1.0 + lax.erf(x * (2.0 ** -0.5)))


def _softplus(x):
    return jnp.maximum(x, 0.0) + jnp.log(1.0 + jnp.exp(-jnp.abs(x)))


def _modnorm(x, g, scale, shift):
    y = x * lax.rsqrt(jnp.mean(x * x, axis=-1, keepdims=True) + EPS)
    return (y * g) * (1.0 + scale) + shift


def _rmsnorm(x, g):
    return x * lax.rsqrt(jnp.mean(x * x, axis=-1, keepdims=True) + EPS) * g


def _layernorm(x, g, b):
    mu = jnp.mean(x, axis=-1, keepdims=True)
    xc = x - mu
    var = jnp.mean(xc * xc, axis=-1, keepdims=True)
    return xc * lax.rsqrt(var + LN_EPS) * g + b


def _l2norm(t):
    return t * lax.rsqrt(jnp.sum(t * t, axis=-1, keepdims=True) + EPS)


def _adamw_math(w, g, m, v):
    mn = ADAM_B1 * m + (1.0 - ADAM_B1) * g
    vn = ADAM_B2 * v + (1.0 - ADAM_B2) * (g * g)
    mh = mn / (1.0 - ADAM_B1 ** ADAM_STEP)
    vh = vn / (1.0 - ADAM_B2 ** ADAM_STEP)
    delta = -ADAM_LR * (mh / (jnp.sqrt(vh) + ADAM_EPS) + ADAM_WD * w)
    return delta, mn, vn


def _each(f, *lists):
    return [f(*xs) for xs in zip(*lists)]


def _wy(q, k, v, bcol, acol, alog, dtb, tinv=None):
    c = CHUNK
    r = lax.broadcasted_iota(jnp.int32, (c, c), 0)
    cc = lax.broadcasted_iota(jnp.int32, (c, c), 1)
    rr = lax.broadcasted_iota(jnp.int32, (c, 1), 0)
    tri_incl, tri_strict, eye = r >= cc, r > cc, r == cc
    beta = _each(_sigmoid, bcol)
    g = _each(lambda al, a_, dt: -jnp.exp(al) * _softplus(a_ + dt), alog, acol, dtb)
    gb = [jnp.broadcast_to(g_, (c, c)) for g_ in g]
    g_row = [jnp.sum(jnp.where(eye, b_, 0.0), axis=0, keepdims=True) for b_ in gb]
    gc_col = [jnp.sum(jnp.where(tri_incl, jnp.broadcast_to(gr, (c, c)), 0.0), axis=1, keepdims=True) for gr in g_row]
    gc_row = [jnp.sum(jnp.where(r <= cc, b_, 0.0), axis=0, keepdims=True) for b_ in gb]
    decay = _each(lambda gcc, gcr: jnp.where(tri_incl, jnp.exp(jnp.where(tri_incl, gcc - gcr, 0.0)), 0.0),
                  gc_col, gc_row)
    qs = [q_ * (q_.shape[-1] ** -0.5) for q_ in q]
    kb = _each(lambda k_, b_: k_ * b_, k, beta)
    a = _each(lambda kb_, k_, d_: jnp.where(tri_strict, bdot(kb_, k_, 1, 1, False) * d_, 0.0), kb, k, decay)
    dv = v[0].shape[-1]
    x = _each(lambda v_, b_, kb_, gcc: jnp.concatenate([v_ * b_, kb_ * jnp.exp(gcc)], axis=1), v, beta, kb, gc_col)
    if tinv is None:
        inv = [jnp.where(eye, 1.0, 0.0) - a_ for a_ in a]
        p = a
        for _ in range(5):
            p = _each(lambda p_: bdot(p_, p_, 1, 0, True), p)
            inv = _each(lambda t_, p_: t_ + bdot(t_, p_, 1, 0, True), inv, p)
        x = _each(lambda t_, x_: bdot(t_, x_, 1, 0, True), inv, x)
    else:
        x = _each(_solve_given_inverse, a, x, tinv)
    xv = [x_[:, :dv] for x_ in x]
    xk = [x_[:, dv:] for x_ in x]
    qk = _each(lambda q_, k_, d_: bdot(q_, k_, 1, 1, False) * d_, qs, k, decay)
    g_last = [jnp.sum(jnp.where(rr == c - 1, gcc, 0.0), axis=0, keepdims=True) for gcc in gc_col]
    qg = _each(lambda q_, gcc: q_ * jnp.exp(gcc), qs, gc_col)
    kd = _each(lambda k_, gl, gcc: k_ * jnp.exp(gl - gcc), k, g_last, gc_col)
    outs = (xv, xk, qg, kd, qk, [jnp.exp(gl) for gl in g_last])
    return outs + (inv,) if tinv is None else outs


@jax.custom_vjp
def _solve_given_inverse(a, rhs, tinv):
    return _raw_dot(tinv, rhs, 1, 0, True)


def _solve_given_inverse_fwd(a, rhs, tinv):
    x = _raw_dot(tinv, rhs, 1, 0, True)
    return x, (x, tinv)


def _solve_given_inverse_bwd(res, dx):
    x, tinv = res
    drhs = _raw_dot(tinv, dx, 0, 0, True)
    return -_raw_dot(drhs, x, 1, 1, True), drhs, jnp.zeros_like(tinv)


_solve_given_inverse.defvjp(_solve_given_inverse_fwd, _solve_given_inverse_bwd)


def _seq(u, w, qg, kd, qk, e, z, s, ng):
    v_new = _each(lambda u_, w_, s_: u_ - bdot(w_, s_, 1, 0, False), u, w, s)
    o1 = _each(lambda q_, s_: bdot(q_, s_, 1, 0, False), qg, s)
    o2 = _each(lambda qk_, vn: bdot(qk_, vn, 1, 0, False), qk, v_new)
    ds = _each(lambda kd_, vn: bdot(kd_, vn, 0, 0, False), kd, v_new)
    s_next = _each(lambda s_, e_, d_: s_ * e_ + d_, s, e, ds)
    y = _each(lambda a_, b_, z_: _rmsnorm(a_ + b_, ng) * _silu(z_), o1, o2, z)
    return y, s_next


def _sg_block(u, v, gt, lg, lb, w, bias):
    n = w.shape[0]
    pr = lax.broadcasted_iota(jnp.int32, (n, n), 0) // CHUNK
    pc = lax.broadcasted_iota(jnp.int32, (n, n), 1) // CHUNK
    wm = jnp.where(pr >= pc, w, 0.0)
    vl = _layernorm(_gelu(v), lg, lb)
    mixed = bdot(wm, vl, 1, 0, False) + bias
    return _gelu(u) * mixed * _silu(gt)


def _glu(a, b):
    return a * _sigmoid(b)


def _cv_post(conv, gate, cb, lg, lb):
    return _silu(_layernorm(conv + cb, lg, lb)) * _silu(gate)


def _qk_post(conv):
    return _l2norm(_silu(conv))


def _v_post(conv):
    return _silu(conv)


def in_proj(x, shift, scale, ng, wp):
    t, d = x.shape
    npc = wp.shape[1]
    tm, tn = _tile(t, 512, LANES), _tile(npc, 2432, LANES)

    def body(x_ref, sh_ref, sc_ref, g_ref, w_ref, p_ref, ht_ref, h_scr):
        @pl.when(pl.program_id(1) == 0)
        def _():
            h = _modnorm(x_ref[...], g_ref[...], sc_ref[...], sh_ref[...])
            h_scr[...] = h.astype(BF16)
            ht_ref[...] = h.T.astype(BF16)
        p_ref[...] = jnp.dot(h_scr[...], w_ref[...], preferred_element_type=F32)

    vec = pl.BlockSpec((1, d), lambda i, j: (0, 0))
    return pl.pallas_call(
        body, name="in_proj", grid=(t // tm, npc // tn),
        in_specs=[pl.BlockSpec((tm, d), lambda i, j: (i, 0)), vec, vec, vec,
                  pl.BlockSpec((d, tn), lambda i, j: (0, j))],
        out_specs=[pl.BlockSpec((tm, tn), lambda i, j: (i, j)), pl.BlockSpec((d, tm), lambda i, j: (0, i))],
        out_shape=[jax.ShapeDtypeStruct((t, npc), F32), jax.ShapeDtypeStruct((d, t), BF16)],
        scratch_shapes=[pltpu.VMEM((tm, d), BF16)],
        compiler_params=_cp(("parallel", "arbitrary")),
    )(x, shift, scale, ng, wp)


def _roll_bank(x, bank_ref, offsets):
    rows = x.shape[0]
    residues = sorted({o % SUBLANES for o in offsets})
    for slot, b in enumerate(residues):
        bank_ref[slot] = x if b == 0 else pltpu.roll(x, rows - b, 0)
    return {o: (residues.index(o % SUBLANES), o - o % SUBLANES) for o in offsets}


def _n_residues(offsets):
    return len({o % SUBLANES for o in offsets})


def conv_fwd(name, k, halo, pre_fn, pre, w, post_fns, extras, params, c_total, tc, tm_pref=512):
    t = pre[0][0].shape[0]
    tm = _tile(t, tm_pref, halo)
    npre, nex, npar = len(pre), len(extras), len(params)
    ngr = tc // LANES
    taps = [halo - (k - 1) + j for j in range(k)]

    def body(*refs):
        prev = refs[:npre]
        cur = refs[npre:2 * npre]
        w_ref = refs[2 * npre]
        ex = refs[2 * npre + 1:2 * npre + 1 + nex]
        par = refs[2 * npre + 1 + nex:2 * npre + 1 + nex + npar]
        out_ref, buf, bank = refs[-3], refs[-2], refs[-1]
        i = pl.program_id(1)
        pv = pre_fn(*[r[...] for r in prev])
        buf[0:halo, :] = jnp.where(i > 0, pv, 0.0)
        buf[halo:, :] = pre_fn(*[r[...] for r in cur])
        where = _roll_bank(buf[...], bank, taps)
        acc = None
        for j, o in enumerate(taps):
            slot, st = where[o]
            term = w_ref[j:j + 1, :] * bank[slot, st:st + tm, :]
            acc = term if acc is None else acc + term
        for gi in range(ngr):
            sl = slice(gi * LANES, (gi + 1) * LANES)
            out_ref[:, sl] = post_fns[gi](acc[:, sl], *[e[:, sl] for e in ex], *[p_[:, sl] for p_ in par])

    hb = tm // halo
    in_specs = ([pl.BlockSpec((halo, tc), functools.partial(lambda j, i, o: (jnp.maximum(i * hb - 1, 0), o + j), o=col // tc))
                 for _, col in pre]
                + [pl.BlockSpec((tm, tc), functools.partial(lambda j, i, o: (i, o + j), o=col // tc)) for _, col in pre]
                + [pl.BlockSpec((k, tc), lambda j, i: (0, j))]
                + [pl.BlockSpec((tm, tc), functools.partial(lambda j, i, o: (i, o + j), o=col // tc)) for _, col in extras]
                + [pl.BlockSpec((1, tc), lambda j, i: (0, j)) for _ in params])
    args = [a for a, _ in pre] * 2 + [w] + [a for a, _ in extras] + list(params)
    return pl.pallas_call(
        body, name=name, grid=(c_total // tc, t // tm), in_specs=in_specs,
        out_specs=pl.BlockSpec((tm, tc), lambda j, i: (i, j)),
        out_shape=jax.ShapeDtypeStruct((t, c_total), F32),
        scratch_shapes=[pltpu.VMEM((halo + tm, tc), F32), pltpu.VMEM((_n_residues(taps), halo + tm, tc), F32)],
        compiler_params=_cp(("parallel", "arbitrary")),
    )(*args)


def conv_bwd(name, k, halo, pre_fn, pre, w, post_fns, extras, params, dout, c_total, tc, tm_pref=512):
    t = pre[0][0].shape[0]
    tm = _tile(t, tm_pref, halo)
    npre, nex, npar = len(pre), len(extras), len(params)
    ngr = tc // LANES
    nout = npre + nex
    assert nout == 1 or c_total == tc
    nblk = t // tm
    ext = tm + halo
    taps = [halo - (k - 1) + j for j in range(k)]
    back = [k - 1 - j for j in range(k)]

    def body(*refs):
        it = iter(refs)
        prev = [next(it) for _ in range(npre)]
        cur = [next(it) for _ in range(npre)]
        nxt = [next(it) for _ in range(npre)]
        w_ref = next(it)
        ex_c = [next(it) for _ in range(nex)]
        ex_n = [next(it) for _ in range(nex)]
        par = [next(it) for _ in range(npar)]
        do_c, do_n = next(it), next(it)
        din_ref, dw_ref = next(it), next(it)
        dpar = [next(it) for _ in range(npar)]
        buf, dbuf, bank, dbank = next(it), next(it), next(it), next(it)
        i = pl.program_id(1)

        @pl.when(i == 0)
        def _():
            dw_ref[...] = jnp.zeros_like(dw_ref)
            for r in dpar:
                r[...] = jnp.zeros_like(r)

        buf[0:halo, :] = jnp.where(i > 0, pre_fn(*[r[...] for r in prev]), 0.0)
        cur_vals = [r[...] for r in cur]
        buf[halo:halo + tm, :] = pre_fn(*cur_vals)
        buf[halo + tm:, :] = pre_fn(*[r[...] for r in nxt])
        where = _roll_bank(buf[...], bank, taps)
        conv = None
        for j, o in enumerate(taps):
            slot, st = where[o]
            term = w_ref[j:j + 1, :] * bank[slot, st:st + ext, :]
            conv = term if conv is None else conv + term
        don = jnp.where(i < nblk - 1, do_n[...], 0.0)
        for gi in range(ngr):
            sl = slice(gi * LANES, (gi + 1) * LANES)
            pv = [p_[:, sl] for p_ in par]
            _, vj = jax.vjp(post_fns[gi], conv[:tm, sl], *[e[:, sl] for e in ex_c], *pv)
            gr = vj(do_c[:, sl])
            dbuf[0:tm, sl] = gr[0]
            for e in range(nex):
                din_ref[:, (npre + e) * tc + gi * LANES:(npre + e) * tc + (gi + 1) * LANES] = gr[1 + e].astype(din_ref.dtype)
            for q_ in range(npar):
                dpar[q_][:, sl] += gr[1 + nex + q_]
            _, vjn = jax.vjp(post_fns[gi], conv[tm:, sl], *[e[:, sl] for e in ex_n], *pv)
            dbuf[tm:, sl] = vjn(don[:, sl])[0]
        dcur = dbuf[0:tm, :]
        dwhere = _roll_bank(dbuf[...], dbank, back)
        dpre = None
        for j in range(k):
            slot, st = dwhere[back[j]]
            term = w_ref[j:j + 1, :] * dbank[slot, st:st + tm, :]
            dpre = term if dpre is None else dpre + term
            slot, st = where[taps[j]]
            dw_ref[j:j + 1, :] += jnp.sum(dcur * bank[slot, st:st + tm, :], axis=0, keepdims=True)
        _, vjp_pre = jax.vjp(pre_fn, *cur_vals)
        for e, gval in enumerate(vjp_pre(dpre)):
            din_ref[:, e * tc:(e + 1) * tc] = gval.astype(din_ref.dtype)

    hb = tm // halo
    last_h = t // halo - 1

    def spec(kind, col):
        o = col // tc
        if kind == "prev":
            return pl.BlockSpec((halo, tc), lambda j, i: (jnp.maximum(i * hb - 1, 0), o + j))
        if kind == "next":
            return pl.BlockSpec((halo, tc), lambda j, i: (jnp.minimum((i + 1) * hb, last_h), o + j))
        return pl.BlockSpec((tm, tc), lambda j, i: (i, o + j))

    in_specs = ([spec("prev", col) for _, col in pre] + [spec("cur", col) for _, col in pre]
                + [spec("next", col) for _, col in pre] + [pl.BlockSpec((k, tc), lambda j, i: (0, j))]
                + [spec("cur", col) for _, col in extras] + [spec("next", col) for _, col in extras]
                + [pl.BlockSpec((1, tc), lambda j, i: (0, j)) for _ in params]
                + [spec("cur", 0), spec("next", 0)])
    args = [a for a, _ in pre] * 3 + [w] + [a for a, _ in extras] * 2 + list(params) + [dout, dout]
    out = pl.pallas_call(
        body, name=name, grid=(c_total // tc, nblk), in_specs=in_specs,
        out_specs=[pl.BlockSpec((tm, nout * tc), lambda j, i: (i, j)), pl.BlockSpec((k, tc), lambda j, i: (0, j))]
        + [pl.BlockSpec((1, tc), lambda j, i: (0, j)) for _ in params],
        out_shape=[jax.ShapeDtypeStruct((t, nout * c_total), BF16), jax.ShapeDtypeStruct((k, c_total), F32)]
        + [jax.ShapeDtypeStruct((1, c_total), F32) for _ in params],
        scratch_shapes=[pltpu.VMEM((2 * halo + tm, tc), F32), pltpu.VMEM((ext, tc), F32),
                        pltpu.VMEM((_n_residues(taps), 2 * halo + tm, tc), F32),
                        pltpu.VMEM((_n_residues(back), ext, tc), F32)],
        compiler_params=_cp(("parallel", "arbitrary")),
    )(*args)
    return out[0], out[1], out[2:]


def _head_pick(ref_val, row):
    rr = lax.broadcasted_iota(jnp.int32, ref_val.shape, 0)
    v = jnp.sum(jnp.where(rr == row, ref_val, 0.0), axis=0, keepdims=True)
    ll = lax.broadcasted_iota(jnp.int32, v.shape, 1)
    return jnp.sum(jnp.where(ll == 0, v, 0.0), axis=1, keepdims=True)


def _lane_col(blk, lane_idx):
    ll = lax.broadcasted_iota(jnp.int32, blk.shape, 1)
    return jnp.sum(jnp.where(ll == lane_idx, blk, 0.0), axis=1, keepdims=True)


WY_HEADS = 2
WY_UNROLL = 4
WY_UNROLL_BWD = 4


def dn_wy_fwd(qkv, p, ba_col, alog_b, dtb_b, nheads):
    t = qkv.shape[0]
    tm = _tile(t, 512, CHUNK * WY_UNROLL)
    nc = tm // CHUNK
    hb = WY_HEADS
    hp = alog_b.shape[0]
    w_ = hb * LANES

    def body(qkv_ref, ba_ref, al_ref, dt_ref, u_ref, w_ref, qg_ref, kd_ref, qk_ref, e_ref, ti_ref):
        hblk = pl.program_id(1)
        alv, dtv = al_ref[...], dt_ref[...]

        def trip(cj, carry):
            units = [(cj * WY_UNROLL + cu, hl) for cu in range(WY_UNROLL) for hl in range(hb)]
            args = [[] for _ in range(7)]
            for ci, hl in units:
                rows = pl.ds(pl.multiple_of(ci * CHUNK, CHUNK), CHUNK)
                ba = ba_ref[rows, :]
                h = hblk * hb + hl
                for lst, val in zip(args, (qkv_ref[rows, hl * 384:hl * 384 + 128],
                                           qkv_ref[rows, hl * 384 + 128:hl * 384 + 256],
                                           qkv_ref[rows, hl * 384 + 256:hl * 384 + 384],
                                           _lane_col(ba, h), _lane_col(ba, nheads + h),
                                           _head_pick(alv, h), _head_pick(dtv, h))):
                    lst.append(val)
            outs = _wy(*args)
            for n, (ci, hl) in enumerate(units):
                rows = pl.ds(pl.multiple_of(ci * CHUNK, CHUNK), CHUNK)
                u, w, qg, kd, qk, e, ti = [o[n] for o in outs]
                sl = slice(hl * LANES, (hl + 1) * LANES)
                u_ref[rows, sl] = u
                w_ref[rows, sl] = w.astype(BF16)
                qg_ref[rows, sl] = qg.astype(BF16)
                kd_ref[rows, sl] = kd.astype(BF16)
                qk_ref[rows, hl * LANES:hl * LANES + CHUNK] = qk.astype(BF16)
                qk_ref[rows, hl * LANES + CHUNK:(hl + 1) * LANES] = jnp.zeros((CHUNK, LANES - CHUNK), BF16)
                e_ref[ci, :, sl] = jnp.broadcast_to(e, (1, LANES))
                ti_ref[rows, hl * LANES:hl * LANES + CHUNK] = ti
                ti_ref[rows, hl * LANES + CHUNK:(hl + 1) * LANES] = jnp.zeros((CHUNK, LANES - CHUNK), F32)
            return carry

        lax.fori_loop(0, nc // WY_UNROLL, trip, 0)

    bc = ba_col // LANES
    blk = pl.BlockSpec((tm, w_), lambda i, h: (i, h))
    tab = pl.BlockSpec((hp, LANES), lambda i, h: (0, 0))
    wide = lambda dt: jax.ShapeDtypeStruct((t, nheads * LANES), dt)
    return pl.pallas_call(
        body, name="dn_wy_fwd", grid=(t // tm, nheads // hb),
        in_specs=[pl.BlockSpec((tm, hb * 384), lambda i, h: (i, h)), pl.BlockSpec((tm, LANES), lambda i, h: (i, bc)),
                  tab, tab],
        out_specs=[blk] * 5 + [pl.BlockSpec((nc, 1, w_), lambda i, h: (i, 0, h)), blk],
        out_shape=[wide(F32), wide(BF16), wide(BF16), wide(BF16), wide(BF16),
                   jax.ShapeDtypeStruct((t // CHUNK, 1, nheads * LANES), F32), wide(F32)],
        compiler_params=_cp(("parallel", "parallel")),
    )(qkv, p, alog_b, dtb_b)


def dn_seq_fwd(u, w, qg, kd, qk, e, p, z_col, ng, nheads, hb):
    t = u.shape[0]
    tm = _tile(t, 512, CHUNK)
    nc = tm // CHUNK
    w_ = hb * LANES

    def body(u_ref, w_ref, qg_ref, kd_ref, qk_ref, e_ref, z_ref, ng_ref, y_ref, ss_ref, s_scr):
        i, hblk = pl.program_id(0), pl.program_id(1)
        for hl in range(hb):
            @pl.when(i == 0)
            def _():
                s_scr[hblk * hb + hl] = jnp.zeros((LANES, LANES), F32)
        ngv = ng_ref[...]

        def chunk(ci, carry):
            rows = pl.ds(pl.multiple_of(ci * CHUNK, CHUNK), CHUNK)
            ev = e_ref[ci]
            sls = [slice(hl * LANES, (hl + 1) * LANES) for hl in range(hb)]
            s = [s_scr[hblk * hb + hl] for hl in range(hb)]
            for hl in range(hb):
                ss_ref[ci, sls[hl], :] = s[hl]
            y, sn = _seq([u_ref[rows, sl] for sl in sls], [w_ref[rows, sl].astype(F32) for sl in sls],
                         [qg_ref[rows, sl].astype(F32) for sl in sls], [kd_ref[rows, sl].astype(F32) for sl in sls],
                         [qk_ref[rows, sl][:, :CHUNK].astype(F32) for sl in sls], [ev[:, sl] for sl in sls],
                         [z_ref[rows, sl] for sl in sls], s, ngv)
            for hl in range(hb):
                y_ref[rows, sls[hl]] = y[hl]
                s_scr[hblk * hb + hl] = sn[hl]
            return carry

        lax.fori_loop(0, nc, chunk, 0)

    zc = z_col // w_
    blk = pl.BlockSpec((tm, w_), lambda i, h: (i, h))
    return pl.pallas_call(
        body, name="dn_seq_fwd", grid=(t // tm, nheads // hb),
        in_specs=[blk] * 5 + [pl.BlockSpec((nc, 1, w_), lambda i, h: (i, 0, h)),
                              pl.BlockSpec((tm, w_), lambda i, h: (i, zc + h)),
                              pl.BlockSpec((1, LANES), lambda i, h: (0, 0))],
        out_specs=[blk, pl.BlockSpec((nc, w_, LANES), lambda i, h: (i, h, 0))],
        out_shape=[jax.ShapeDtypeStruct((t, nheads * LANES), F32),
                   jax.ShapeDtypeStruct((t // CHUNK, nheads * LANES, LANES), F32)],
        scratch_shapes=[pltpu.VMEM((nheads, LANES, LANES), F32)],
        compiler_params=_cp(("arbitrary", "arbitrary")),
    )(u, w, qg, kd, qk, e, p, ng)


def dn_seq_bwd(u, w, qg, kd, qk, e, p, z_col, ng, ss, dy, nheads, hb):
    t = u.shape[0]
    tm = _tile(t, 2048 // hb, CHUNK)
    nc = tm // CHUNK
    nblk = t // tm
    w_ = hb * LANES

    def body(u_ref, w_ref, qg_ref, kd_ref, qk_ref, e_ref, z_ref, ng_ref, ss_ref, dy_ref,
             du_ref, dw_ref, dqg_ref, dkd_ref, dqk_ref, de_ref, dz_ref, dng_ref, ds_scr):
        i, hblk = pl.program_id(0), pl.program_id(1)

        @pl.when((i == 0) & (hblk == 0))
        def _():
            dng_ref[...] = jnp.zeros_like(dng_ref)

        for hl in range(hb):
            @pl.when(i == 0)
            def _():
                ds_scr[hblk * hb + hl] = jnp.zeros((LANES, LANES), F32)
        ngv = ng_ref[...]

        def chunk(cj, carry):
            ci = nc - 1 - cj
            rows = pl.ds(pl.multiple_of(ci * CHUNK, CHUNK), CHUNK)
            ev = e_ref[ci]
            sls = [slice(hl * LANES, (hl + 1) * LANES) for hl in range(hb)]
            _, vj = jax.vjp(_seq, [u_ref[rows, sl] for sl in sls], [w_ref[rows, sl].astype(F32) for sl in sls],
                            [qg_ref[rows, sl].astype(F32) for sl in sls], [kd_ref[rows, sl].astype(F32) for sl in sls],
                            [qk_ref[rows, sl][:, :CHUNK].astype(F32) for sl in sls], [ev[:, sl] for sl in sls],
                            [z_ref[rows, sl] for sl in sls], [ss_ref[ci, sl, :] for sl in sls], ngv)
            du, dw, dqg, dkd, dqk, de, dz, dsp, dng = vj(([dy_ref[rows, sl] for sl in sls],
                                                          [ds_scr[hblk * hb + hl] for hl in range(hb)]))
            for hl, sl in enumerate(sls):
                du_ref[rows, sl] = du[hl]
                dw_ref[rows, sl] = dw[hl]
                dqg_ref[rows, sl] = dqg[hl]
                dkd_ref[rows, sl] = dkd[hl]
                dqk_ref[rows, hl * LANES:hl * LANES + CHUNK] = dqk[hl]
                dqk_ref[rows, hl * LANES + CHUNK:(hl + 1) * LANES] = jnp.zeros((CHUNK, LANES - CHUNK), F32)
                de_ref[ci, :, sl] = de[hl]
                dz_ref[rows, sl] = dz[hl]
                ds_scr[hblk * hb + hl] = dsp[hl]
            dng_ref[...] += dng
            return carry

        lax.fori_loop(0, nc, chunk, 0)

    zc = z_col // w_
    rv = lambda i: nblk - 1 - i
    blk = pl.BlockSpec((tm, w_), lambda i, h: (rv(i), h))
    eblk = pl.BlockSpec((nc, 1, w_), lambda i, h: (rv(i), 0, h))
    one = pl.BlockSpec((1, LANES), lambda i, h: (0, 0))
    wide = jax.ShapeDtypeStruct((t, nheads * LANES), F32)
    return pl.pallas_call(
        body, name="dn_seq_bwd", grid=(nblk, nheads // hb),
        in_specs=[blk] * 5 + [eblk, pl.BlockSpec((tm, w_), lambda i, h: (rv(i), zc + h)), one,
                              pl.BlockSpec((nc, w_, LANES), lambda i, h: (rv(i), h, 0)), blk],
        out_specs=[blk] * 5 + [eblk, blk, one],
        out_shape=[wide] * 5 + [jax.ShapeDtypeStruct((t // CHUNK, 1, nheads * LANES), F32), wide,
                                jax.ShapeDtypeStruct((1, LANES), F32)],
        scratch_shapes=[pltpu.VMEM((nheads, LANES, LANES), F32)],
        compiler_params=_cp(("arbitrary", "arbitrary")),
    )(u, w, qg, kd, qk, e, p, ng, ss, dy)


def dn_wy_bwd(qkv, p, ba_col, alog_b, dtb_b, ti, du, dw, dqg, dkd, dqk, de, nheads):
    t = qkv.shape[0]
    tm = _tile(t, 512, CHUNK * WY_UNROLL_BWD)
    nc = tm // CHUNK
    hb = WY_HEADS
    hp = alog_b.shape[0]
    w_ = hb * LANES

    def body(qkv_ref, ba_ref, al_ref, dt_ref, du_ref, dw_ref, dqg_ref, dkd_ref, dqk_ref, de_ref, ti_ref,
             dqkv_ref, dba_ref, dal_ref, ddt_ref):
        i, hblk = pl.program_id(0), pl.program_id(1)

        @pl.when((i == 0) & (hblk == 0))
        def _():
            dal_ref[...] = jnp.zeros_like(dal_ref)
            ddt_ref[...] = jnp.zeros_like(ddt_ref)

        @pl.when(hblk == 0)
        def _():
            dba_ref[...] = jnp.zeros_like(dba_ref)

        alv, dtv = al_ref[...], dt_ref[...]
        lane = lax.broadcasted_iota(jnp.int32, (CHUNK, LANES), 1)
        rowp = lax.broadcasted_iota(jnp.int32, (hp, LANES), 0)

        def trip(cj, carry):
            units = [(cj * WY_UNROLL_BWD + cu, hl) for cu in range(WY_UNROLL_BWD) for hl in range(hb)]
            args = [[] for _ in range(8)]
            cts = [[] for _ in range(6)]
            for ci, hl in units:
                rows = pl.ds(pl.multiple_of(ci * CHUNK, CHUNK), CHUNK)
                ba = ba_ref[rows, :]
                h = hblk * hb + hl
                sl = slice(hl * LANES, (hl + 1) * LANES)
                for lst, val in zip(args, (qkv_ref[rows, hl * 384:hl * 384 + 128],
                                           qkv_ref[rows, hl * 384 + 128:hl * 384 + 256],
                                           qkv_ref[rows, hl * 384 + 256:hl * 384 + 384],
                                           _lane_col(ba, h), _lane_col(ba, nheads + h),
                                           _head_pick(alv, h), _head_pick(dtv, h), ti_ref[rows, sl][:, :CHUNK])):
                    lst.append(val)
                de11 = jnp.sum(de_ref[ci][:, sl], axis=1, keepdims=True)
                for lst, val in zip(cts, (du_ref[rows, sl], dw_ref[rows, sl], dqg_ref[rows, sl], dkd_ref[rows, sl],
                                          dqk_ref[rows, sl][:, :CHUNK], de11)):
                    lst.append(val)
            _, vj = jax.vjp(_wy, *args)
            grads = vj(tuple(cts))
            for n, (ci, hl) in enumerate(units):
                rows = pl.ds(pl.multiple_of(ci * CHUNK, CHUNK), CHUNK)
                h = hblk * hb + hl
                dq, dk, dv, dbc, dac, dal, ddt = [g_[n] for g_ in grads[:7]]
                dqkv_ref[rows, hl * 384:hl * 384 + 128] = dq
                dqkv_ref[rows, hl * 384 + 128:hl * 384 + 256] = dk
                dqkv_ref[rows, hl * 384 + 256:hl * 384 + 384] = dv
                dba_ref[rows, :] += jnp.where(lane == h, dbc, 0.0) + jnp.where(lane == nheads + h, dac, 0.0)
                dal_ref[...] += jnp.where(rowp == h, dal, 0.0)
                ddt_ref[...] += jnp.where(rowp == h, ddt, 0.0)
            return carry

        lax.fori_loop(0, nc // WY_UNROLL_BWD, trip, 0)

    bc = ba_col // LANES
    blk = pl.BlockSpec((tm, w_), lambda i, h: (i, h))
    tab = pl.BlockSpec((hp, LANES), lambda i, h: (0, 0))
    return pl.pallas_call(
        body, name="dn_wy_bwd", grid=(t // tm, nheads // hb),
        in_specs=[pl.BlockSpec((tm, hb * 384), lambda i, h: (i, h)), pl.BlockSpec((tm, LANES), lambda i, h: (i, bc)),
                  tab, tab] + [blk] * 5 + [pl.BlockSpec((nc, 1, w_), lambda i, h: (i, 0, h)), blk],
        out_specs=[pl.BlockSpec((tm, hb * 384), lambda i, h: (i, h)), pl.BlockSpec((tm, LANES), lambda i, h: (i, 0)),
                   tab, tab],
        out_shape=[jax.ShapeDtypeStruct((t, nheads * 384), F32), jax.ShapeDtypeStruct((t, LANES), F32),
                   jax.ShapeDtypeStruct((hp, LANES), F32), jax.ShapeDtypeStruct((hp, LANES), F32)],
        compiler_params=_cp(("arbitrary", "arbitrary")),
    )(qkv, p, alog_b, dtb_b, du, dw, dqg, dkd, dqk, de, ti)


def sg_fwd(p, col, sg, lg, lb, w, bias_b):
    t = p.shape[0]
    ng_ = sg // LANES
    tm = _tile(t, 256, LANES)
    cb = col // sg

    def body(u_ref, v_ref, g_ref, lg_ref, lb_ref, w_ref, b_ref, y_ref):
        for n in range(tm // LANES):
            rs = slice(n * LANES, (n + 1) * LANES)
            for gi in range(ng_):
                sl = slice(gi * LANES, (gi + 1) * LANES)
                y_ref[rs, sl] = _sg_block(u_ref[rs, sl], v_ref[rs, sl], g_ref[rs, sl], lg_ref[:, sl], lb_ref[:, sl],
                                          w_ref[gi], b_ref[gi])

    vec = pl.BlockSpec((1, sg), lambda i: (0, 0))
    full = pl.BlockSpec((ng_, LANES, LANES), lambda i: (0, 0, 0))
    return pl.pallas_call(
        body, name="sg_fwd", grid=(t // tm,),
        in_specs=[pl.BlockSpec((tm, sg), lambda i: (i, cb)), pl.BlockSpec((tm, sg), lambda i: (i, cb + 1)),
                  pl.BlockSpec((tm, sg), lambda i: (i, cb + 2)), vec, vec, full, full],
        out_specs=pl.BlockSpec((tm, sg), lambda i: (i, 0)),
        out_shape=jax.ShapeDtypeStruct((t, sg), F32),
        compiler_params=_cp(("parallel",)),
    )(p, p, p, lg, lb, w, bias_b)


def sg_bwd(p, col, sg, lg, lb, w, bias_b, dy):
    t = p.shape[0]
    ng_ = sg // LANES
    tm = _tile(t, 256, LANES)
    cb = col // sg

    def body(u_ref, v_ref, g_ref, lg_ref, lb_ref, w_ref, b_ref, dy_ref, d_ref, dlg_ref, dlb_ref, dw_ref, db_ref):
        @pl.when(pl.program_id(0) == 0)
        def _():
            for r in (dlg_ref, dlb_ref, dw_ref, db_ref):
                r[...] = jnp.zeros_like(r)

        for n in range(tm // LANES):
            rs = slice(n * LANES, (n + 1) * LANES)
            for gi in range(ng_):
                sl = slice(gi * LANES, (gi + 1) * LANES)
                _, vj = jax.vjp(_sg_block, u_ref[rs, sl], v_ref[rs, sl], g_ref[rs, sl], lg_ref[:, sl], lb_ref[:, sl],
                                w_ref[gi], b_ref[gi])
                du, dv, dg, dlg, dlb, dw, db = vj(dy_ref[rs, sl])
                d_ref[rs, gi * LANES:(gi + 1) * LANES] = du.astype(BF16)
                d_ref[rs, sg + gi * LANES:sg + (gi + 1) * LANES] = dv.astype(BF16)
                d_ref[rs, 2 * sg + gi * LANES:2 * sg + (gi + 1) * LANES] = dg.astype(BF16)
                dlg_ref[:, sl] += dlg
                dlb_ref[:, sl] += dlb
                dw_ref[gi] += dw
                db_ref[gi] += jnp.broadcast_to(jnp.sum(db, axis=1, keepdims=True), (LANES, LANES))

    vec = pl.BlockSpec((1, sg), lambda i: (0, 0))
    full = pl.BlockSpec((ng_, LANES, LANES), lambda i: (0, 0, 0))
    return pl.pallas_call(
        body, name="sg_bwd", grid=(t // tm,),
        in_specs=[pl.BlockSpec((tm, sg), lambda i: (i, cb)), pl.BlockSpec((tm, sg), lambda i: (i, cb + 1)),
                  pl.BlockSpec((tm, sg), lambda i: (i, cb + 2)), vec, vec, full, full,
                  pl.BlockSpec((tm, sg), lambda i: (i, 0))],
        out_specs=[pl.BlockSpec((tm, 3 * sg), lambda i: (i, 0)), vec, vec, full, full],
        out_shape=[jax.ShapeDtypeStruct((t, 3 * sg), BF16), jax.ShapeDtypeStruct((1, sg), F32),
                   jax.ShapeDtypeStruct((1, sg), F32), jax.ShapeDtypeStruct((ng_, LANES, LANES), F32),
                   jax.ShapeDtypeStruct((ng_, LANES, LANES), F32)],
        compiler_params=_cp(("arbitrary",)),
    )(p, p, p, lg, lb, w, bias_b, dy)


def out_proj(x, y_dn, y_sg, y_cv, wo, gate):
    t, d = x.shape
    dn, sg, cv = y_dn.shape[1], y_sg.shape[1], y_cv.shape[1]
    dmix = dn + sg + cv
    tm = _tile(t, 256, LANES)

    def body(x_ref, a_ref, b_ref, c_ref, w_ref, g_ref, xn_ref, y_ref, yt_ref):
        a, b, c = a_ref[...], b_ref[...], c_ref[...]
        y = (jnp.dot(a.astype(BF16), w_ref[0:dn, :], preferred_element_type=F32)
             + jnp.dot(b.astype(BF16), w_ref[dn:dn + sg, :], preferred_element_type=F32)
             + jnp.dot(c.astype(BF16), w_ref[dn + sg:, :], preferred_element_type=F32))
        y_ref[...] = y
        xn_ref[...] = x_ref[...] + g_ref[...] * y
        yt_ref[0:dn, :] = a.T.astype(BF16)
        yt_ref[dn:dn + sg, :] = b.T.astype(BF16)
        yt_ref[dn + sg:, :] = c.T.astype(BF16)

    row = lambda w_: pl.BlockSpec((tm, w_), lambda i: (i, 0))
    return pl.pallas_call(
        body, name="out_proj", grid=(t // tm,),
        in_specs=[row(d), row(dn), row(sg), row(cv), pl.BlockSpec((dmix, d), lambda i: (0, 0)),
                  pl.BlockSpec((1, d), lambda i: (0, 0))],
        out_specs=[row(d), row(d), pl.BlockSpec((dmix, tm), lambda i: (0, i))],
        out_shape=[jax.ShapeDtypeStruct((t, d), F32), jax.ShapeDtypeStruct((t, d), F32),
                   jax.ShapeDtypeStruct((dmix, t), BF16)],
        compiler_params=_cp(("parallel",)),
    )(x, y_dn, y_sg, y_cv, wo, gate)


def out_proj_bwd(dxn, y, gate, wo, dn, sg, cv):
    t, d = dxn.shape
    dmix = dn + sg + cv
    tm = _tile(t, 256, LANES)

    def body(dx_ref, y_ref, g_ref, w_ref, da_ref, db_ref, dc_ref, dyb_ref, dg_ref):
        @pl.when(pl.program_id(0) == 0)
        def _():
            dg_ref[...] = jnp.zeros_like(dg_ref)
        dx = dx_ref[...]
        dg_ref[...] += jnp.sum(dx * y_ref[...], axis=0, keepdims=True)
        dyb = (dx * g_ref[...]).astype(BF16)
        dyb_ref[...] = dyb
        dcat = lax.dot_general(dyb, w_ref[...], (((1,), (1,)), ((), ())), preferred_element_type=F32)
        da_ref[...] = dcat[:, 0:dn]
        db_ref[...] = dcat[:, dn:dn + sg]
        dc_ref[...] = dcat[:, dn + sg:]

    row = lambda w_: pl.BlockSpec((tm, w_), lambda i: (i, 0))
    vec = pl.BlockSpec((1, d), lambda i: (0, 0))
    return pl.pallas_call(
        body, name="out_proj_bwd", grid=(t // tm,),
        in_specs=[row(d), row(d), vec, pl.BlockSpec((dmix, d), lambda i: (0, 0))],
        out_specs=[row(dn), row(sg), row(cv), row(d), vec],
        out_shape=[jax.ShapeDtypeStruct((t, dn), F32), jax.ShapeDtypeStruct((t, sg), F32),
                   jax.ShapeDtypeStruct((t, cv), F32), jax.ShapeDtypeStruct((t, d), BF16),
                   jax.ShapeDtypeStruct((1, d), F32)],
        compiler_params=_cp(("arbitrary",)),
    )(dxn, y, gate, wo)


def matmul_acc(name, at, b):
    m, t = at.shape
    n = b.shape[1]
    tm, tn, tk = _tile(m, 1024, LANES), _tile(n, 2432, LANES), _tile(t, 1024, LANES)

    def body(a_ref, b_ref, o_ref):
        @pl.when(pl.program_id(2) == 0)
        def _():
            o_ref[...] = jnp.zeros_like(o_ref)
        o_ref[...] += jnp.dot(a_ref[...], b_ref[...], preferred_element_type=F32)

    return pl.pallas_call(
        body, name=name, grid=(m // tm, n // tn, t // tk),
        in_specs=[pl.BlockSpec((tm, tk), lambda i, j, k: (i, k)), pl.BlockSpec((tk, tn), lambda i, j, k: (k, j))],
        out_specs=pl.BlockSpec((tm, tn), lambda i, j, k: (i, j)),
        out_shape=jax.ShapeDtypeStruct((m, n), F32),
        compiler_params=_cp(("parallel", "parallel", "arbitrary")),
    )(at, b)


def in_proj_bwd(dp, wp, x, dxn, shift, scale, ng):
    t, d = x.shape
    npc = wp.shape[1]
    tm, tk = _tile(t, 512, LANES), _tile(npc, 2432, LANES)

    def mm_body(dp_ref, w_ref, dh_ref):
        @pl.when(pl.program_id(1) == 0)
        def _():
            dh_ref[...] = jnp.zeros_like(dh_ref)
        dh_ref[...] += lax.dot_general(dp_ref[...], w_ref[...], (((1,), (1,)), ((), ())), preferred_element_type=F32)

    dh = pl.pallas_call(
        mm_body, name="in_proj_bwd", grid=(t // tm, npc // tk),
        in_specs=[pl.BlockSpec((tm, tk), lambda i, k: (i, k)), pl.BlockSpec((d, tk), lambda i, k: (0, k))],
        out_specs=pl.BlockSpec((tm, d), lambda i, k: (i, 0)), out_shape=jax.ShapeDtypeStruct((t, d), F32),
        compiler_params=_cp(("parallel", "arbitrary")),
    )(dp, wp)

    tr = _tile(t, 256, 8)

    def norm_body(dh_ref, x_ref, dxn_ref, sh_ref, sc_ref, g_ref, dx_ref, dg_ref, dsc_ref, dsh_ref):
        @pl.when(pl.program_id(0) == 0)
        def _():
            for r in (dg_ref, dsc_ref, dsh_ref):
                r[...] = jnp.zeros_like(r)
        _, vj = jax.vjp(_modnorm, x_ref[...], g_ref[...], sc_ref[...], sh_ref[...])
        dx, dg, dsc, dsh = vj(dh_ref[...])
        dx_ref[...] = dxn_ref[...] + dx
        dg_ref[...] += dg
        dsc_ref[...] += dsc
        dsh_ref[...] += dsh

    vec = pl.BlockSpec((1, d), lambda i: (0, 0))
    row = pl.BlockSpec((tr, d), lambda i: (i, 0))
    return pl.pallas_call(
        norm_body, name="modnorm_bwd", grid=(t // tr,), in_specs=[row, row, row, vec, vec, vec],
        out_specs=[row, vec, vec, vec],
        out_shape=[jax.ShapeDtypeStruct((t, d), F32)] + [jax.ShapeDtypeStruct((1, d), F32)] * 3,
        compiler_params=_cp(("arbitrary",)),
    )(dh, x, dxn, shift, scale, ng)


def loss_head(x, tgt, fg):
    t, d = x.shape
    tm = _tile(t, 512, 8)

    def body(x_ref, t_ref, g_ref, l_ref, dx_ref, dg_ref):
        @pl.when(pl.program_id(0) == 0)
        def _():
            l_ref[...] = jnp.zeros_like(l_ref)
            dg_ref[...] = jnp.zeros_like(dg_ref)
        y, vj = jax.vjp(_rmsnorm, x_ref[...], g_ref[...])
        err = y - t_ref[...]
        part = 0.5 * jnp.sum(jnp.sum(err * err, axis=1, keepdims=True), axis=0, keepdims=True) / d
        l_ref[...] += jnp.broadcast_to(part, l_ref.shape)
        dx, dg = vj(err / d)
        dx_ref[...] = dx
        dg_ref[...] += dg

    row = pl.BlockSpec((tm, d), lambda i: (i, 0))
    vec = pl.BlockSpec((1, d), lambda i: (0, 0))
    return pl.pallas_call(
        body, name="loss_head", grid=(t // tm,), in_specs=[row, row, vec],
        out_specs=[pl.BlockSpec((1, LANES), lambda i: (0, 0)), row, vec],
        out_shape=[jax.ShapeDtypeStruct((1, LANES), F32), jax.ShapeDtypeStruct((t, d), F32),
                   jax.ShapeDtypeStruct((1, d), F32)],
        compiler_params=_cp(("arbitrary",)),
    )(x, tgt, fg)


def adamw(name, w, g, m, v):
    r, c = w.shape
    tr = _tile(r, 256, 8) if r % 8 == 0 else r

    def body(w_ref, g_ref, m_ref, v_ref, d_ref, mo_ref, vo_ref):
        d_ref[...], mo_ref[...], vo_ref[...] = _adamw_math(w_ref[...], g_ref[...], m_ref[...], v_ref[...])

    blk = pl.BlockSpec((tr, c), lambda i: (i, 0))
    return pl.pallas_call(
        body, name=name, grid=(r // tr,), in_specs=[blk] * 4, out_specs=[blk] * 3,
        out_shape=[jax.ShapeDtypeStruct((r, c), F32)] * 3, compiler_params=_cp(("parallel",)),
    )(w, g, m, v)


def ada_fwd(c_all, w_ada, b_loc):
    nl, d, cols = w_ada.shape
    nb = c_all.shape[0]
    tn = _tile(cols, 512, LANES)

    def body(c_ref, w_ref, b_ref, o_ref):
        ca = _silu(c_ref[...]).astype(BF16)
        o_ref[0] = jnp.dot(ca, w_ref[0].astype(BF16), preferred_element_type=F32) + b_ref[0]

    return pl.pallas_call(
        body, name="ada_fwd", grid=(nl, cols // tn),
        in_specs=[pl.BlockSpec((nb, d), lambda l, j: (0, 0)), pl.BlockSpec((1, d, tn), lambda l, j: (l, 0, j)),
                  pl.BlockSpec((1, 1, tn), lambda l, j: (l, 0, j))],
        out_specs=pl.BlockSpec((1, nb, tn), lambda l, j: (l, 0, j)),
        out_shape=jax.ShapeDtypeStruct((nl, nb, cols), F32),
        compiler_params=_cp(("parallel", "parallel")),
    )(c_all, w_ada, b_loc)


def ada_bwd(c_all_t, dmod_loc, w, m, v):
    nl, d, cols = w.shape
    nb = c_all_t.shape[1]
    tr = _tile(d, 256, 8)

    def body(c_ref, dm_ref, w_ref, m_ref, v_ref, g_ref, d_ref, mo_ref, vo_ref):
        ca = _silu(c_ref[...])
        dm = dm_ref[0]
        g = _lane_col(ca, 0) * dm[0:1, :]
        for b in range(1, nb):
            g = g + _lane_col(ca, b) * dm[b:b + 1, :]
        g_ref[0] = g
        d_ref[0], mo_ref[0], vo_ref[0] = _adamw_math(w_ref[0], g, m_ref[0], v_ref[0])

    blk = pl.BlockSpec((1, tr, cols), lambda l, i: (l, i, 0))
    return pl.pallas_call(
        body, name="ada_bwd", grid=(nl, d // tr),
        in_specs=[pl.BlockSpec((tr, nb), lambda l, i: (i, 0)), pl.BlockSpec((1, nb, cols), lambda l, i: (l, 0, 0)),
                  blk, blk, blk],
        out_specs=[blk] * 4, out_shape=[jax.ShapeDtypeStruct((nl, d, cols), F32)] * 4,
        compiler_params=_cp(("parallel", "parallel")),
    )(c_all_t, dmod_loc, w, m, v)


def sum8(g):
    _, r, c = g.shape
    tr = _tile(r, 256, 8)

    def body(g_ref, o_ref):
        acc = g_ref[0]
        for k in range(1, NDEV):
            acc = acc + g_ref[k]
        o_ref[...] = acc

    return pl.pallas_call(
        body, name="sum8", grid=(r // tr,), in_specs=[pl.BlockSpec((NDEV, tr, c), lambda i: (0, i, 0))],
        out_specs=pl.BlockSpec((tr, c), lambda i: (i, 0)), out_shape=jax.ShapeDtypeStruct((r, c), F32),
        compiler_params=_cp(("parallel",)),
    )(g)


def pair_sum(cflag, g, r1):
    _, ns, r, c = g.shape
    tr = _tile(r, 256, 8)

    def body(cf_ref, g0_ref, g1_ref, r_ref, ob_ref):
        keep = jnp.where(cf_ref[0:1, 0:1] == 0.0, g0_ref[0], g1_ref[0])
        ob_ref[...] = (keep + r_ref[...]).astype(BF16)

    blk = pl.BlockSpec((1, tr, c), lambda s, i: (s, i, 0))
    return pl.pallas_call(
        body, name="pair_sum", grid=(ns, r // tr),
        in_specs=[pl.BlockSpec((1, LANES), lambda s, i: (0, 0)), pl.BlockSpec((1, 1, tr, c), lambda s, i: (0, s, i, 0)),
                  pl.BlockSpec((1, 1, tr, c), lambda s, i: (1, s, i, 0)), blk], out_specs=blk,
        out_shape=jax.ShapeDtypeStruct((ns, r, c), BF16), compiler_params=_cp(("parallel", "parallel")),
    )(cflag, g, g, r1)


def relay_sum(px, py, rx, ry):
    r, c = px.shape
    nh = r // 2
    tr = _tile(nh, 256, 8)

    def body(px_ref, py_ref, rx_ref, ry_ref, qx_ref, qy_ref):
        upper = pl.program_id(0) == 0
        qx_ref[...] = (px_ref[...].astype(F32) + jnp.where(upper, 0.0, ry_ref[...].astype(F32))).astype(BF16)
        qy_ref[...] = (py_ref[...].astype(F32) + jnp.where(upper, rx_ref[...].astype(F32), 0.0)).astype(BF16)

    full = pl.BlockSpec((tr, c), lambda h, i: (h * (nh // tr) + i, 0))
    half = pl.BlockSpec((tr, c), lambda h, i: (i, 0))
    return pl.pallas_call(
        body, name="relay_sum", grid=(2, nh // tr), in_specs=[full, full, half, half], out_specs=[full, full],
        out_shape=[jax.ShapeDtypeStruct((r, c), BF16)] * 2, compiler_params=_cp(("parallel", "parallel")),
    )(px, py, rx, ry)


def chip_sum(own, r1, r2):
    r, c = own.shape
    tr = _tile(r, 256, 8)
    nslot = r2.shape[0]

    def body(g_ref, r1_ref, r2_ref, o_ref):
        acc = g_ref[...] + r1_ref[...]
        for k in range(nslot):
            acc = acc + r2_ref[k].astype(F32)
        o_ref[...] = acc

    blk = pl.BlockSpec((tr, c), lambda i: (i, 0))
    return pl.pallas_call(
        body, name="chip_sum", grid=(r // tr,),
        in_specs=[blk, blk, pl.BlockSpec((nslot, tr, c), lambda i: (0, i, 0))], out_specs=blk,
        out_shape=jax.ShapeDtypeStruct((r, c), F32), compiler_params=_cp(("parallel",)),
    )(own, r1, r2)


def adamw_halves(name, cflag, w, own, recv, m, v):
    _, r, c = w.shape
    tr = _tile(r, 256, 8)

    def body(cf_ref, w_ref, a_ref, b_ref, m_ref, v_ref, g_ref, d_ref, mo_ref, vo_ref):
        is_own = cf_ref[0:1, 0:1] == pl.program_id(0).astype(F32)
        g = jnp.where(is_own, a_ref[...], b_ref[...])
        g_ref[0] = g
        d_ref[0], mo_ref[0], vo_ref[0] = _adamw_math(w_ref[0], g, m_ref[0], v_ref[0])

    blk = pl.BlockSpec((1, tr, c), lambda h, i: (h, i, 0))
    hlf = pl.BlockSpec((tr, c), lambda h, i: (i, 0))
    return pl.pallas_call(
        body, name=name, grid=(2, r // tr),
        in_specs=[pl.BlockSpec((1, LANES), lambda h, i: (0, 0)), blk, hlf, hlf, blk, blk], out_specs=[blk] * 4,
        out_shape=[jax.ShapeDtypeStruct(w.shape, F32)] * 4, compiler_params=_cp(("parallel", "parallel")),
    )(cflag, w, own, recv, m, v)


def _me():
    return lax.axis_index("x"), lax.axis_index("y"), lax.axis_index("c")


_FLIPS = ((1, 0), (0, 1), (1, 1))


def all_gather8(v):
    m_per, n = v.shape

    def body(x_ref, out_ref, send_sems, recv_sems, local_sem):
        x, y, c = _me()
        me, sibling = (x, y, c), (x, y, 1 - c)
        chips = [(x ^ fx, y ^ fy) for fx, fy in _FLIPS]

        def rows(px, py, pc):
            return out_ref.at[pl.ds((4 * px + 2 * py + pc) * m_per, m_per), :]

        def copy(k, block, to, src=None):
            return pltpu.make_async_remote_copy(
                src_ref=rows(*block) if src is None else src, dst_ref=rows(*block),
                send_sem=send_sems.at[k], recv_sem=recv_sems.at[k], device_id=to, device_id_type=MESH)

        mine = pltpu.make_async_copy(x_ref, rows(*me), local_sem)
        mine.start()
        first = [copy(0, me, sibling, src=x_ref)]
        first += [copy(1 + j, me, (*chip, c), src=x_ref) for j, chip in enumerate(chips)]
        for cp in first:
            cp.start()
        passed = [copy(4 + j, (*chip, c), sibling) for j, chip in enumerate(chips)]
        for j, chip in enumerate(chips):
            copy(1 + j, (*chip, c), me).wait_recv()
            passed[j].start()
        copy(0, sibling, me).wait_recv()
        for j, chip in enumerate(chips):
            copy(4 + j, (*chip, 1 - c), me).wait_recv()
        for cp in first + passed:
            cp.wait_send()
        mine.wait()

    return pl.pallas_call(
        body, name="all_gather8", out_shape=jax.ShapeDtypeStruct((NDEV * m_per, n), v.dtype),
        in_specs=[pl.BlockSpec(memory_space=pltpu.VMEM)], out_specs=pl.BlockSpec(memory_space=pltpu.VMEM),
        scratch_shapes=[pltpu.SemaphoreType.DMA((7,)), pltpu.SemaphoreType.DMA((7,)), pltpu.SemaphoreType.DMA],
        compiler_params=pltpu.CompilerParams(vmem_limit_bytes=VMEM_LIMIT),
    )(v)


def gather_weights(ws):
    na = len(ws)
    ncp = 9

    def body(*refs):
        srcs, outs = refs[:na], refs[na:2 * na]
        send_sems, recv_sems = refs[2 * na:]
        x, y, c = _me()
        chip, cx, cy, cd = 2 * x + y, 2 * (1 - x) + y, 2 * x + 1 - y, 2 * (1 - x) + 1 - y
        nx, ny, sibling = (1 - x, y, c), (x, 1 - y, c), (x, y, 1 - c)

        def copy(a, k, src, dst, to):
            return pltpu.make_async_remote_copy(
                src_ref=src, dst_ref=dst, send_sem=send_sems.at[a * ncp + k], recv_sem=recv_sems.at[a * ncp + k],
                device_id=to, device_id_type=MESH)

        def land(a, k, ch, q):
            return copy(a, k, srcs[a].at[c, q], outs[a].at[ch, c, q], sibling)

        sends = []
        for a in range(na):
            for k, q, to in ((0, 0, nx), (2, 1, ny), (1, 1, nx), (3, 0, ny)):
                sends.append(copy(a, k, srcs[a].at[c, q], outs[a].at[chip, c, q], to))
                sends[-1].start()
        for a in range(na):
            land(a, 0, cx, 0).wait_recv()
            sends.append(copy(a, 4, outs[a].at[cx, c, 0], outs[a].at[cx, c, 0], ny))
            sends[-1].start()
            land(a, 2, cy, 1).wait_recv()
            sends.append(copy(a, 5, outs[a].at[cy, c, 1], outs[a].at[cy, c, 1], nx))
            sends[-1].start()
        for a in range(na):
            for k, ch, q in ((1, cx, 1), (3, cy, 0), (4, cd, 0), (5, cd, 1)):
                land(a, k, ch, q).wait_recv()
            for j, ch in enumerate((cx, cy, cd)):
                sends.append(copy(a, 6 + j, outs[a].at[ch, c], outs[a].at[ch, c], sibling))
                sends[-1].start()
        for a in range(na):
            for j, ch in enumerate((cx, cy, cd)):
                copy(a, 6 + j, outs[a].at[ch, c], outs[a].at[ch, 1 - c], sibling).wait_recv()
        for cp in sends:
            cp.wait_send()

    return pl.pallas_call(
        body, name="gather_weights",
        out_shape=[jax.ShapeDtypeStruct((NCHIP,) + w.shape, w.dtype) for w in ws],
        in_specs=[ANY] * na, out_specs=[ANY] * na,
        scratch_shapes=[pltpu.SemaphoreType.DMA((ncp * na,)), pltpu.SemaphoreType.DMA((ncp * na,))],
    )(*ws)


def sibling_swap(name, gs, other_half=False):
    na = len(gs)

    def body(*refs):
        srcs, outs = refs[:na], refs[na:2 * na]
        send_sems, recv_sems = refs[2 * na:]
        x, y, c = _me()
        cps = [pltpu.make_async_remote_copy(
            src_ref=srcs[a].at[1 - c] if other_half else srcs[a], dst_ref=outs[a], send_sem=send_sems.at[a],
            recv_sem=recv_sems.at[a], device_id=(x, y, 1 - c), device_id_type=MESH) for a in range(na)]
        for cp in cps:
            cp.start()
        for cp in cps:
            cp.wait()

    return pl.pallas_call(
        body, name=name, out_shape=[jax.ShapeDtypeStruct(g.shape[1:] if other_half else g.shape, g.dtype) for g in gs],
        in_specs=[ANY] * na, out_specs=[ANY] * na,
        scratch_shapes=[pltpu.SemaphoreType.DMA((na,)), pltpu.SemaphoreType.DMA((na,))],
    )(*gs)


def relay_out(ps):
    na = len(ps)

    def body(*refs):
        srcs, outs = refs[:na], refs[na:3 * na]
        send_sems, recv_sems = refs[3 * na:]
        x, y, c = _me()
        cd = 2 * (1 - x) + 1 - y
        cps = []
        for a in range(na):
            nh = srcs[a].shape[1] // 2
            for j, to in enumerate(((1 - x, y, c), (x, 1 - y, c))):
                cps.append(pltpu.make_async_remote_copy(
                    src_ref=srcs[a].at[cd, pl.ds(j * nh, nh)], dst_ref=outs[2 * a + j], send_sem=send_sems.at[2 * a + j],
                    recv_sem=recv_sems.at[2 * a + j], device_id=to, device_id_type=MESH))
        for cp in cps:
            cp.start()
        for cp in cps:
            cp.wait()

    shapes = [jax.ShapeDtypeStruct((p_.shape[1] // 2,) + p_.shape[2:], p_.dtype) for p_ in ps for _ in range(2)]
    return pl.pallas_call(
        body, name="relay_out", out_shape=shapes, in_specs=[ANY] * na, out_specs=[ANY] * (2 * na),
        scratch_shapes=[pltpu.SemaphoreType.DMA((2 * na,)), pltpu.SemaphoreType.DMA((2 * na,))],
    )(*ps)


def neighbour_exchange(qs):
    na = len(qs) // 2

    def body(*refs):
        srcs, outs = refs[:2 * na], refs[2 * na:3 * na]
        send_sems, recv_sems = refs[3 * na:]
        x, y, c = _me()
        cps = []
        for a in range(na):
            for j, to in enumerate(((1 - x, y, c), (x, 1 - y, c))):
                cps.append(pltpu.make_async_remote_copy(
                    src_ref=srcs[2 * a + j], dst_ref=outs[a].at[j], send_sem=send_sems.at[2 * a + j],
                    recv_sem=recv_sems.at[2 * a + j], device_id=to, device_id_type=MESH))
        for cp in cps:
            cp.start()
        for cp in cps:
            cp.wait()

    return pl.pallas_call(
        body, name="chip_exchange",
        out_shape=[jax.ShapeDtypeStruct((2,) + qs[2 * a].shape, qs[2 * a].dtype) for a in range(na)],
        in_specs=[ANY] * (2 * na), out_specs=[ANY] * na,
        scratch_shapes=[pltpu.SemaphoreType.DMA((2 * na,)), pltpu.SemaphoreType.DMA((2 * na,))],
    )(*qs)


class _Cfg:
    def __init__(self, x, a_log, sg_w, cv_ln_g, cv_w, conv_qkv):
        self.t, self.d = x.shape[1], x.shape[2]
        self.nl, self.h = a_log.shape
        self.dn = self.h * LANES
        self.g = sg_w.shape[1]
        self.sg = self.g * LANES
        self.cv = cv_ln_g.shape[1]
        self.kc = cv_w.shape[1]
        self.k4 = conv_qkv.shape[1]
        self.o_z = 3 * self.dn
        self.o_sg = 4 * self.dn
        self.o_cv = self.o_sg + 3 * self.sg
        self.o_ba = self.o_cv + 3 * self.cv
        self.npc = self.o_ba + LANES
        self.d_in = self.o_ba + 2 * self.h
        self.dmix = self.dn + self.sg + self.cv
        self.hb_fwd = _tile(self.h, 8, 1)
        self.hb_bwd = _tile(self.h, 8, 1)


def _runs(cfg):
    dn, h = cfg.dn, cfg.h
    runs = [(part * dn + hd * LANES, hd * 3 * LANES + part * LANES, LANES) for part in range(3) for hd in range(h)]
    return runs + [(3 * dn, 3 * dn, dn), (4 * dn, cfg.o_ba, 2 * h), (4 * dn + 2 * h, 4 * dn, cfg.o_ba - 4 * dn)]


def _assemble_perm(cfg, shards):
    cols = shards[0].shape[-1]
    pieces = []
    for nat, _, wdt in sorted(_runs(cfg), key=lambda r_: r_[1]):
        a = nat
        while a < nat + wdt:
            s = a // cols
            b = min(nat + wdt, (s + 1) * cols)
            pieces.append(shards[s][..., a - s * cols:b - s * cols])
            a = b
    pieces.append(jnp.zeros(shards[0].shape[:-1] + (cfg.npc - cfg.o_ba - 2 * cfg.h,), shards[0].dtype))
    return jnp.concatenate(pieces, axis=-1)


def _natural_pieces(cfg, s, cols):
    lo, hi = s * cols, (s + 1) * cols
    pieces = []
    for nat, perm, wdt in sorted(_runs(cfg)):
        a, b = max(nat, lo), min(nat + wdt, hi)
        if a < b:
            pieces.append((perm + a - nat, perm + b - nat))
    return pieces


def _layer_fwd(cfg, x, mod, lw):
    shift, scale, gate = mod
    p, ht = in_proj(x, shift, scale, lw["norm_g"], lw["wp"])
    qk_post = [_qk_post, _qk_post, _v_post]
    qkv = conv_fwd("dn_pre_fwd", cfg.k4, HALO4, lambda a: a, [(p, 0)], lw["conv_qkv"], qk_post, [], [],
                   3 * cfg.dn, 3 * LANES)
    wy = dn_wy_fwd(qkv, p, cfg.o_ba, lw["alog_b"], lw["dtb_b"], cfg.h)
    y_dn, ss = dn_seq_fwd(*wy[:6], p, cfg.o_z, lw["dn_norm_g"], cfg.h, cfg.hb_fwd)
    y_sg = sg_fwd(p, cfg.o_sg, cfg.sg, lw["sg_ln_g"], lw["sg_ln_b"], lw["sg_w"], lw["sg_bias_b"])
    cv_post = [_cv_post] * (cfg.cv // LANES)
    y_cv = conv_fwd("cv_fwd", cfg.kc, HALO31, _glu, [(p, cfg.o_cv), (p, cfg.o_cv + cfg.cv)], lw["cv_w"], cv_post,
                    [(p, cfg.o_cv + 2 * cfg.cv)], [lw["cv_b"], lw["cv_ln_g"], lw["cv_ln_b"]], cfg.cv, cfg.cv)
    xn, y, yt = out_proj(x, y_dn, y_sg, y_cv, lw["wo"], gate)
    return xn, dict(x=x, p=p, ht=ht, qkv=qkv, wy=wy, ss=ss, y=y, yt=yt)


def _layer_bwd(cfg, dxn, mod, lw, sv):
    shift, scale, gate = mod
    p = sv["p"]
    d_dn, d_sg, d_cv, dyb, dgate = out_proj_bwd(dxn, sv["y"], gate, lw["wo"], cfg.dn, cfg.sg, cfg.cv)
    g_wo = matmul_acc("w_out_grad", sv["yt"], dyb)
    cv_post = [_cv_post] * (cfg.cv // LANES)
    dcv, g_cvw, (g_cvb, g_cvlg, g_cvlb) = conv_bwd(
        "cv_bwd", cfg.kc, HALO31, _glu, [(p, cfg.o_cv), (p, cfg.o_cv + cfg.cv)], lw["cv_w"], cv_post,
        [(p, cfg.o_cv + 2 * cfg.cv)], [lw["cv_b"], lw["cv_ln_g"], lw["cv_ln_b"]], d_cv, cfg.cv, cfg.cv, tm_pref=256)
    dsg, g_sglg, g_sglb, g_sgw, g_sgb = sg_bwd(p, cfg.o_sg, cfg.sg, lw["sg_ln_g"], lw["sg_ln_b"], lw["sg_w"],
                                               lw["sg_bias_b"], d_sg)
    *dwy, dz, g_dng = dn_seq_bwd(*sv["wy"][:6], p, cfg.o_z, lw["dn_norm_g"], sv["ss"], d_dn, cfg.h, cfg.hb_bwd)
    dqkv, dba, g_al, g_dt = dn_wy_bwd(sv["qkv"], p, cfg.o_ba, lw["alog_b"], lw["dtb_b"], sv["wy"][6], *dwy, cfg.h)
    qk_post = [_qk_post, _qk_post, _v_post]
    dqkv_pre, g_cq, _ = conv_bwd("dn_pre_bwd", cfg.k4, HALO4, lambda a: a, [(p, 0)], lw["conv_qkv"], qk_post, [], [],
                                 dqkv, 3 * cfg.dn, 3 * LANES)
    dp = jnp.concatenate([dqkv_pre, dz.astype(BF16), dsg, dcv, dba.astype(BF16)], axis=1)
    g_wp = matmul_acc("w_in_grad", sv["ht"], dp)
    dx, g_ng, dscale, dshift = in_proj_bwd(dp, lw["wp"], sv["x"], dxn, shift, scale, lw["norm_g"])
    grads = dict(norm_g=g_ng, conv_qkv=g_cq, a_log=g_al[:cfg.h, 0], dt_bias=g_dt[:cfg.h, 0], dn_norm_g=g_dng,
                 sg_ln_g=g_sglg, sg_ln_b=g_sglb, sg_w=g_sgw, sg_b=g_sgb[:, :, 0], cv_w=g_cvw, cv_b=g_cvb,
                 cv_ln_g=g_cvlg, cv_ln_b=g_cvlb, wp=g_wp, wo=g_wo)
    return dx, grads, (dshift, dscale, dgate)


def _local_step(cfg, xs, tgt, mods, lws, fg):
    nl = len(lws)
    saved = []
    for l in range(nl):
        xs, sv = _layer_fwd(cfg, xs, mods[l], lws[l])
        saved.append(sv)
    loss_b, dx, g_fg = loss_head(xs, tgt, fg)
    lg = [None] * nl
    dmods = [None] * nl
    for l in reversed(range(nl)):
        dx, lg[l], dmods[l] = _layer_bwd(cfg, dx, mods[l], lws[l], saved[l])
    return loss_b, dx, g_fg, lg, dmods


SMALL = ("norm_g", "conv_qkv", "a_log", "dt_bias", "dn_norm_g", "sg_ln_g", "sg_ln_b", "sg_w", "sg_b", "cv_w",
         "cv_b", "cv_ln_g", "cv_ln_b", "final_g", "b_ada")
PACK_N = 1024


def _pack(arrs):
    flat = jnp.concatenate([a.reshape(-1).astype(F32) for a in arrs])
    rows = -(-flat.shape[0] // PACK_N)
    rows = -(-rows // 8) * 8
    return jnp.pad(flat, (0, rows * PACK_N - flat.shape[0])).reshape(rows, PACK_N)


def _unpack(buf, shapes):
    flat = buf.reshape(-1)
    out, o = [], 0
    for s in shapes:
        n = 1
        for d_ in s:
            n *= d_
        out.append(flat[o:o + n].reshape(s))
        o += n
    return out


def kernel(x, c, norm_g, w_ada, b_ada, w_in, conv_qkv, a_log, dt_bias, dn_norm_g, sg_ln_g, sg_ln_b, sg_w, sg_b, cv_w, cv_b, cv_ln_g, cv_ln_b, w_out, final_g, loss_target, m_norm_g, m_w_ada, m_b_ada, m_w_in, m_conv_qkv, m_a_log, m_dt_bias, m_dn_norm_g, m_sg_ln_g, m_sg_ln_b, m_sg_w, m_sg_b, m_cv_w, m_cv_b, m_cv_ln_g, m_cv_ln_b, m_w_out, m_final_g, v_norm_g, v_w_ada, v_b_ada, v_w_in, v_conv_qkv, v_a_log, v_dt_bias, v_dn_norm_g, v_sg_ln_g, v_sg_ln_b, v_sg_w, v_sg_b, v_cv_w, v_cv_b, v_cv_ln_g, v_cv_ln_b, v_w_out, v_final_g):
    cfg = _Cfg(x, a_log, sg_w, cv_ln_g, cv_w, conv_qkv)
    nl, d, t, h = cfg.nl, cfg.d, cfg.t, cfg.h
    lh = nl // 2
    ax, ay, ac = _me()
    chip = 2 * ax + ay
    dev = 2 * chip + ac
    wts = dict(norm_g=norm_g, w_ada=w_ada, b_ada=b_ada, w_in=w_in, conv_qkv=conv_qkv, a_log=a_log, dt_bias=dt_bias,
               dn_norm_g=dn_norm_g, sg_ln_g=sg_ln_g, sg_ln_b=sg_ln_b, sg_w=sg_w, sg_b=sg_b, cv_w=cv_w, cv_b=cv_b,
               cv_ln_g=cv_ln_g, cv_ln_b=cv_ln_b, w_out=w_out, final_g=final_g)
    mom = dict(norm_g=m_norm_g, w_ada=m_w_ada, b_ada=m_b_ada, w_in=m_w_in, conv_qkv=m_conv_qkv, a_log=m_a_log,
               dt_bias=m_dt_bias, dn_norm_g=m_dn_norm_g, sg_ln_g=m_sg_ln_g, sg_ln_b=m_sg_ln_b, sg_w=m_sg_w,
               sg_b=m_sg_b, cv_w=m_cv_w, cv_b=m_cv_b, cv_ln_g=m_cv_ln_g, cv_ln_b=m_cv_ln_b, w_out=m_w_out,
               final_g=m_final_g)
    vel = dict(norm_g=v_norm_g, w_ada=v_w_ada, b_ada=v_b_ada, w_in=v_w_in, conv_qkv=v_conv_qkv, a_log=v_a_log,
               dt_bias=v_dt_bias, dn_norm_g=v_dn_norm_g, sg_ln_g=v_sg_ln_g, sg_ln_b=v_sg_ln_b, sg_w=v_sg_w,
               sg_b=v_sg_b, cv_w=v_cv_w, cv_b=v_cv_b, cv_ln_g=v_cv_ln_g, cv_ln_b=v_cv_ln_b, w_out=v_w_out,
               final_g=v_final_g)
    ada_cols = w_ada.shape[2]
    in_cols = w_in.shape[2]
    out_rows = w_out.shape[1]
    cq_cols = conv_qkv.shape[2]
    cvw_cols = cv_w.shape[2]

    c_all = all_gather8(jnp.pad(c, ((0, 7), (0, 0)))).reshape(NDEV, 8, d)[:, 0, :]
    b_loc = lax.dynamic_slice_in_dim(b_ada, chip * ada_cols, ada_cols, axis=1)[:, None, :]
    mod_part = ada_fwd(c_all, w_ada, b_loc)
    mod_all = all_gather8(mod_part.reshape(nl * NDEV, ada_cols)).reshape(NDEV, nl, NDEV, ada_cols)
    mod_me = lax.dynamic_index_in_dim(mod_all[0::2], dev, axis=2, keepdims=False)
    mod_me = jnp.moveaxis(mod_me, 0, 1).reshape(nl, 3, 1, d)

    win_b = w_in.astype(BF16).reshape(2, 2, lh * d // 2, in_cols)
    wout_b = w_out.astype(BF16).reshape(2, 2, lh * out_rows // 2, d)
    win_all, wout_all = gather_weights([win_b, wout_b])
    win_all = lax.dynamic_update_index_in_dim(win_all, win_b, chip, axis=0)
    wout_all = lax.dynamic_update_index_in_dim(wout_all, wout_b, chip, axis=0)
    win_all = win_all.reshape(NCHIP, nl, d, in_cols)
    wp_all = [_assemble_perm(cfg, [win_all[s, l] for s in range(NCHIP)]) for l in range(nl)]
    wo_all = jnp.moveaxis(wout_all.reshape(NCHIP, nl, out_rows, d), 0, 1).reshape(nl, NCHIP * out_rows, d)

    cq_all = all_gather8(conv_qkv.reshape(nl * cfg.k4, cq_cols)).reshape(NDEV, nl, cfg.k4, cq_cols)[0::2]
    cq_full = jnp.moveaxis(cq_all, 0, 2).reshape(nl, cfg.k4, NCHIP * cq_cols)
    cq_perm = _perm_cols_qkv(cfg, cq_full)
    kcp = -(-cfg.kc // 8) * 8
    cvw_all = all_gather8(jnp.pad(cv_w, ((0, 0), (0, kcp - cfg.kc), (0, 0))).reshape(nl * kcp, cvw_cols))
    cvw_all = cvw_all.reshape(NDEV, nl, kcp, cvw_cols)[0::2]
    cvw_full = jnp.moveaxis(cvw_all, 0, 2).reshape(nl, kcp, NCHIP * cvw_cols)[:, :cfg.kc]

    hp = -(-h // 8) * 8
    lws = []
    for l in range(nl):
        lws.append(dict(
            norm_g=norm_g[l][None], wp=wp_all[l], wo=wo_all[l], conv_qkv=cq_perm[l],
            alog_b=jnp.pad(jnp.broadcast_to(a_log[l][:, None], (h, LANES)), ((0, hp - h), (0, 0))),
            dtb_b=jnp.pad(jnp.broadcast_to(dt_bias[l][:, None], (h, LANES)), ((0, hp - h), (0, 0))),
            dn_norm_g=dn_norm_g[l][None], sg_ln_g=sg_ln_g[l][None], sg_ln_b=sg_ln_b[l][None], sg_w=sg_w[l],
            sg_bias_b=jnp.broadcast_to(sg_b[l][:, :, None], (cfg.g, LANES, LANES)),
            cv_w=cvw_full[l], cv_b=cv_b[l][None], cv_ln_g=cv_ln_g[l][None], cv_ln_b=cv_ln_b[l][None]))

    mods = [(mod_me[l, 0], mod_me[l, 1], mod_me[l, 2]) for l in range(nl)]
    loss_b, dx, g_fg, lg, dmods = _local_step(cfg, x[0], loss_target[0], mods, lws, final_g[None])
    grad_x = dx[None]

    dmod = jnp.stack([jnp.concatenate(dm, axis=1)[0] for dm in dmods])
    stack = lambda k: jnp.stack([g_[k] for g_ in lg])
    small_local = [stack(k).reshape(wts_shape) for k, wts_shape in
                   (("norm_g", (nl, d)), ("conv_qkv", (nl, cfg.k4, 3 * cfg.dn)), ("a_log", (nl, h)),
                    ("dt_bias", (nl, h)), ("dn_norm_g", (nl, LANES)), ("sg_ln_g", (nl, cfg.sg)),
                    ("sg_ln_b", (nl, cfg.sg)), ("sg_w", (nl, cfg.g, LANES, LANES)), ("sg_b", (nl, cfg.g, LANES)),
                    ("cv_w", (nl, cfg.kc, cfg.cv)), ("cv_b", (nl, cfg.cv)), ("cv_ln_g", (nl, cfg.cv)),
                    ("cv_ln_b", (nl, cfg.cv)))]
    small_local[1] = _unperm_cols_qkv(cfg, small_local[1])
    small_local += [g_fg[0], dmod, loss_b[0, 0:1]]
    shapes = [a.shape for a in small_local]
    packed = _pack(small_local)
    rows = packed.shape[0]
    gathered = all_gather8(packed).reshape(NDEV, rows, PACK_N)
    summed = _unpack(sum8(gathered), shapes)
    sgrads = dict(zip(SMALL, summed[:15]))
    loss = summed[15][0]
    sgrads["conv_qkv"] = lax.dynamic_slice_in_dim(sgrads["conv_qkv"], chip * cq_cols, cq_cols, axis=2)
    sgrads["cv_w"] = lax.dynamic_slice_in_dim(sgrads["cv_w"], chip * cvw_cols, cvw_cols, axis=2)

    off = sum(math.prod(s) for s in shapes[:14])
    dmod_all = gathered.reshape(NDEV, rows * PACK_N)[:, off:off + nl * 3 * d].reshape(NDEV, nl, 3 * d)
    dmod_loc = jnp.moveaxis(lax.dynamic_slice_in_dim(dmod_all, chip * ada_cols, ada_cols, axis=2), 0, 1)
    g_wada, d_wada, nm_wada, nv_wada = ada_bwd(c_all.T, dmod_loc, w_ada, m_w_ada, v_w_ada)

    shard = lambda g_, s: jnp.concatenate([g_[:, a:b] for a, b in _natural_pieces(cfg, s, in_cols)], axis=-1)
    g_in = jnp.stack([jnp.stack([jnp.concatenate([shard(lg[hh * lh + j]["wp"], s) for j in range(lh)], axis=0)
                                 for s in range(NCHIP)]) for hh in range(2)])
    g_wo = jnp.stack([g_["wo"] for g_ in lg]).reshape(2, lh, NCHIP, out_rows, d)
    g_out = jnp.moveaxis(g_wo, 2, 1).reshape(2, NCHIP, lh * out_rows, d)
    cflag = jnp.full((1, LANES), ac, F32)
    r1_in, r1_out = sibling_swap("swap_halves", [g_in, g_out], other_half=True)
    p_in, p_out = pair_sum(cflag, g_in, r1_in), pair_sum(cflag, g_out, r1_out)
    rx_in, ry_in, rx_out, ry_out = relay_out([p_in, p_out])
    to_x = lambda p_: lax.dynamic_index_in_dim(p_, 2 * (1 - ax) + ay, axis=0, keepdims=False)
    to_y = lambda p_: lax.dynamic_index_in_dim(p_, 2 * ax + 1 - ay, axis=0, keepdims=False)
    r2_in, r2_out = neighbour_exchange([*relay_sum(to_x(p_in), to_y(p_in), rx_in, ry_in),
                                        *relay_sum(to_x(p_out), to_y(p_out), rx_out, ry_out)])
    mine = lambda g_: lax.dynamic_index_in_dim(g_, chip, axis=0, keepdims=False)
    keep = lambda g_: lax.dynamic_index_in_dim(g_, ac, axis=0, keepdims=False)
    h_in = chip_sum(mine(keep(g_in)), mine(r1_in), r2_in)
    h_out = chip_sum(mine(keep(g_out)), mine(r1_out), r2_out)
    o_in, o_out = sibling_swap("join_halves", [h_in, h_out])

    v3 = lambda a, r_, c_: a.reshape(2, lh * r_, c_)
    grad_w_in, d_in_, nm_in, nv_in = adamw_halves("adamw_w_in", cflag, v3(w_in, d, in_cols), h_in, o_in,
                                                  v3(m_w_in, d, in_cols), v3(v_w_in, d, in_cols))
    grad_w_out, d_out_, nm_out, nv_out = adamw_halves("adamw_w_out", cflag, v3(w_out, out_rows, d), h_out, o_out,
                                                      v3(m_w_out, out_rows, d), v3(v_w_out, out_rows, d))
    grad_w_in = grad_w_in.reshape(w_in.shape)
    grad_w_out = grad_w_out.reshape(w_out.shape)
    sshapes = [wts[k].shape for k in SMALL]
    pk = lambda dct: _pack([dct[k] for k in SMALL])
    d_s, m_s, v_s = adamw("adamw_small", pk(wts), pk(sgrads), pk(mom), pk(vel))
    d_small = dict(zip(SMALL, _unpack(d_s, sshapes)))
    m_small = dict(zip(SMALL, _unpack(m_s, sshapes)))
    v_small = dict(zip(SMALL, _unpack(v_s, sshapes)))

    grads = dict(sgrads, w_ada=g_wada, w_in=grad_w_in, w_out=grad_w_out)
    deltas = dict(d_small, w_ada=d_wada, w_in=d_in_.reshape(w_in.shape), w_out=d_out_.reshape(w_out.shape))
    new_m = dict(m_small, w_ada=nm_wada, w_in=nm_in.reshape(w_in.shape), w_out=nm_out.reshape(w_out.shape))
    new_v = dict(v_small, w_ada=nv_wada, w_in=nv_in.reshape(w_in.shape), w_out=nv_out.reshape(w_out.shape))
    order = ("norm_g", "w_ada", "b_ada", "w_in", "conv_qkv", "a_log", "dt_bias", "dn_norm_g", "sg_ln_g", "sg_ln_b",
             "sg_w", "sg_b", "cv_w", "cv_b", "cv_ln_g", "cv_ln_b", "w_out", "final_g")
    return (loss, grad_x, *[grads[k] for k in order], *[deltas[k] for k in order], *[new_m[k] for k in order],
            *[new_v[k] for k in order])


def _perm_cols_qkv(cfg, w):
    lead = w.shape[:-1]
    return jnp.moveaxis(w.reshape(lead + (3, cfg.h, LANES)), -3, -2).reshape(lead + (3 * cfg.dn,))


def _unperm_cols_qkv(cfg, w):
    lead = w.shape[:-1]
    return jnp.moveaxis(w.reshape(lead + (cfg.h, 3, LANES)), -3, -2).reshape(lead + (3 * cfg.dn,))
```

```python
import functools
import math

import jax
import jax.numpy as jnp
from jax import lax
from jax.experimental import pallas as pl
from jax.experimental.pallas import tpu as pltpu

F32 = jnp.float32
BF16 = jnp.bfloat16
EPS = 1e-6
LN_EPS = 1e-5
LANES = 128
CHUNK = 64
SUBLANES = 8
HALO4 = 8
HALO31 = 32
NCHIP = 4
NDEV = 8
VMEM_LIMIT = 56 * 2 ** 20
ADAM_LR, ADAM_B1, ADAM_B2, ADAM_EPS, ADAM_WD, ADAM_STEP = 0.001, 0.9, 0.999, 1e-08, 0.01, 10
MESH = pl.DeviceIdType.MESH
ANY = pl.BlockSpec(memory_space=pl.ANY)


def _cp(sem=None, vmem=VMEM_LIMIT):
    return pltpu.CompilerParams(dimension_semantics=sem, vmem_limit_bytes=vmem)


def _tile(n, pref, mult):
    t = min(n, pref) // mult * mult
    while t > 0 and n % t:
        t -= mult
    return t if t > 0 else n


def _split(a):
    hi = a.astype(BF16)
    return hi, (a - hi.astype(F32)).astype(BF16)


def _raw_dot(a, b, ca, cb, hi):
    dn = (((ca,), (cb,)), ((), ()))
    if hi:
        ah, al = _split(a.astype(F32))
        bh, bl = _split(b.astype(F32))
        d3 = lambda x, y: lax.dot_general(x, y, dn, preferred_element_type=F32)
        return d3(ah, bh) + (d3(al, bh) + d3(ah, bl))
    return lax.dot_general(a.astype(BF16), b.astype(BF16), dn, preferred_element_type=F32)


@functools.partial(jax.custom_vjp, nondiff_argnums=(2, 3, 4))
def bdot(a, b, ca, cb, hi):
    return _raw_dot(a, b, ca, cb, hi)


def _bdot_fwd(a, b, ca, cb, hi):
    return _raw_dot(a, b, ca, cb, hi), (a, b)


def _bdot_bwd(ca, cb, hi, res, ct):
    a, b = res
    fa, fb = 1 - ca, 1 - cb
    da = _raw_dot(ct, b, 1, fb, hi) if ca == 1 else _raw_dot(b, ct, fb, 1, hi)
    db = _raw_dot(a, ct, fa, 0, hi) if cb == 0 else _raw_dot(ct, a, 0, fa, hi)
    return da.astype(a.dtype), db.astype(b.dtype)


bdot.defvjp(_bdot_fwd, _bdot_bwd)


def _sigmoid(x):
    return jax.nn.sigmoid(x)


def _silu(x):
    return x * _sigmoid(x)


def _gelu(x):
    return 0.5 * x * (1.0 + lax.erf(x * (2.0 ** -0.5)))


def _softplus(x):
    return jnp.maximum(x, 0.0) + jnp.log(1.0 + jnp.exp(-jnp.abs(x)))


def _modnorm(x, g, scale, shift):
    y = x * lax.rsqrt(jnp.mean(x * x, axis=-1, keepdims=True) + EPS)
    return (y * g) * (1.0 + scale) + shift


def _rmsnorm(x, g):
    return x * lax.rsqrt(jnp.mean(x * x, axis=-1, keepdims=True) + EPS) * g


def _layernorm(x, g, b):
    mu = jnp.mean(x, axis=-1, keepdims=True)
    xc = x - mu
    var = jnp.mean(xc * xc, axis=-1, keepdims=True)
    return xc * lax.rsqrt(var + LN_EPS) * g + b


def _l2norm(t):
    return t * lax.rsqrt(jnp.sum(t * t, axis=-1, keepdims=True) + EPS)


def _adamw_math(w, g, m, v):
    mn = ADAM_B1 * m + (1.0 - ADAM_B1) * g
    vn = ADAM_B2 * v + (1.0 - ADAM_B2) * (g * g)
    mh = mn / (1.0 - ADAM_B1 ** ADAM_STEP)
    vh = vn / (1.0 - ADAM_B2 ** ADAM_STEP)
    delta = -ADAM_LR * (mh / (jnp.sqrt(vh) + ADAM_EPS) + ADAM_WD * w)
    return delta, mn, vn


def _each(f, *lists):
    return [f(*xs) for xs in zip(*lists)]


def _wy(q, k, v, bcol, acol, alog, dtb, tinv=None):
    c = CHUNK
    r = lax.broadcasted_iota(jnp.int32, (c, c), 0)
    cc = lax.broadcasted_iota(jnp.int32, (c, c), 1)
    rr = lax.broadcasted_iota(jnp.int32, (c, 1), 0)
    tri_incl, tri_strict, eye = r >= cc, r > cc, r == cc
    beta = _each(_sigmoid, bcol)
    g = _each(lambda al, a_, dt: -jnp.exp(al) * _softplus(a_ + dt), alog, acol, dtb)
    gb = [jnp.broadcast_to(g_, (c, c)) for g_ in g]
    g_row = [jnp.sum(jnp.where(eye, b_, 0.0), axis=0, keepdims=True) for b_ in gb]
    gc_col = [jnp.sum(jnp.where(tri_incl, jnp.broadcast_to(gr, (c, c)), 0.0), axis=1, keepdims=True) for gr in g_row]
    gc_row = [jnp.sum(jnp.where(r <= cc, b_, 0.0), axis=0, keepdims=True) for b_ in gb]
    decay = _each(lambda gcc, gcr: jnp.where(tri_incl, jnp.exp(jnp.where(tri_incl, gcc - gcr, 0.0)), 0.0),
                  gc_col, gc_row)
    qs = [q_ * (q_.shape[-1] ** -0.5) for q_ in q]
    kb = _each(lambda k_, b_: k_ * b_, k, beta)
    a = _each(lambda kb_, k_, d_: jnp.where(tri_strict, bdot(kb_, k_, 1, 1, False) * d_, 0.0), kb, k, decay)
    dv = v[0].shape[-1]
    x = _each(lambda v_, b_, kb_, gcc: jnp.concatenate([v_ * b_, kb_ * jnp.exp(gcc)], axis=1), v, beta, kb, gc_col)
    if tinv is None:
        inv = [jnp.where(eye, 1.0, 0.0) - a_ for a_ in a]
        p = a
        for _ in range(5):
            p = _each(lambda p_: bdot(p_, p_, 1, 0, True), p)
            inv = _each(lambda t_, p_: t_ + bdot(t_, p_, 1, 0, True), inv, p)
        x = _each(lambda t_, x_: bdot(t_, x_, 1, 0, True), inv, x)
    else:
        x = _each(_solve_given_inverse, a, x, tinv)
    xv = [x_[:, :dv] for x_ in x]
    xk = [x_[:, dv:] for x_ in x]
    qk = _each(lambda q_, k_, d_: bdot(q_, k_, 1, 1, False) * d_, qs, k, decay)
    g_last = [jnp.sum(jnp.where(rr == c - 1, gcc, 0.0), axis=0, keepdims=True) for gcc in gc_col]
    qg = _each(lambda q_, gcc: q_ * jnp.exp(gcc), qs, gc_col)
    kd = _each(lambda k_, gl, gcc: k_ * jnp.exp(gl - gcc), k, g_last, gc_col)
    outs = (xv, xk, qg, kd, qk, [jnp.exp(gl) for gl in g_last])
    return outs + (inv,) if tinv is None else outs


@jax.custom_vjp
def _solve_given_inverse(a, rhs, tinv):
    return _raw_dot(tinv, rhs, 1, 0, True)


def _solve_given_inverse_fwd(a, rhs, tinv):
    x = _raw_dot(tinv, rhs, 1, 0, True)
    return x, (x, tinv)


def _solve_given_inverse_bwd(res, dx):
    x, tinv = res
    drhs = _raw_dot(tinv, dx, 0, 0, True)
    return -_raw_dot(drhs, x, 1, 1, True), drhs, jnp.zeros_like(tinv)


_solve_given_inverse.defvjp(_solve_given_inverse_fwd, _solve_given_inverse_bwd)


def _seq(u, w, qg, kd, qk, e, z, s, ng):
    v_new = _each(lambda u_, w_, s_: u_ - bdot(w_, s_, 1, 0, False), u, w, s)
    o1 = _each(lambda q_, s_: bdot(q_, s_, 1, 0, False), qg, s)
    o2 = _each(lambda qk_, vn: bdot(qk_, vn, 1, 0, False), qk, v_new)
    ds = _each(lambda kd_, vn: bdot(kd_, vn, 0, 0, False), kd, v_new)
    s_next = _each(lambda s_, e_, d_: s_ * e_ + d_, s, e, ds)
    y = _each(lambda a_, b_, z_: _rmsnorm(a_ + b_, ng) * _silu(z_), o1, o2, z)
    return y, s_next


def _sg_block(u, v, gt, lg, lb, w, bias):
    n = w.shape[0]
    pr = lax.broadcasted_iota(jnp.int32, (n, n), 0) // CHUNK
    pc = lax.broadcasted_iota(jnp.int32, (n, n), 1) // CHUNK
    wm = jnp.where(pr >= pc, w, 0.0)
    vl = _layernorm(_gelu(v), lg, lb)
    mixed = bdot(wm, vl, 1, 0, False) + bias
    return _gelu(u) * mixed * _silu(gt)


def _glu(a, b):
    return a * _sigmoid(b)


def _cv_post(conv, gate, cb, lg, lb):
    return _silu(_layernorm(conv + cb, lg, lb)) * _silu(gate)


def _qk_post(conv):
    return _l2norm(_silu(conv))


def _v_post(conv):
    return _silu(conv)


def in_proj(x, shift, scale, ng, wp):
    t, d = x.shape
    npc = wp.shape[1]
    tm, tn = _tile(t, 512, LANES), _tile(npc, 2432, LANES)

    def body(x_ref, sh_ref, sc_ref, g_ref, w_ref, p_ref, ht_ref, h_scr):
        @pl.when(pl.program_id(1) == 0)
        def _():
            h = _modnorm(x_ref[...], g_ref[...], sc_ref[...], sh_ref[...])
            h_scr[...] = h.astype(BF16)
            ht_ref[...] = h.T.astype(BF16)
        p_ref[...] = jnp.dot(h_scr[...], w_ref[...], preferred_element_type=F32)

    vec = pl.BlockSpec((1, d), lambda i, j: (0, 0))
    return pl.pallas_call(
        body, name="in_proj", grid=(t // tm, npc // tn),
        in_specs=[pl.BlockSpec((tm, d), lambda i, j: (i, 0)), vec, vec, vec,
                  pl.BlockSpec((d, tn), lambda i, j: (0, j))],
        out_specs=[pl.BlockSpec((tm, tn), lambda i, j: (i, j)), pl.BlockSpec((d, tm), lambda i, j: (0, i))],
        out_shape=[jax.ShapeDtypeStruct((t, npc), F32), jax.ShapeDtypeStruct((d, t), BF16)],
        scratch_shapes=[pltpu.VMEM((tm, d), BF16)],
        compiler_params=_cp(("parallel", "arbitrary")),
    )(x, shift, scale, ng, wp)


def _roll_bank(x, bank_ref, offsets):
    rows = x.shape[0]
    residues = sorted({o % SUBLANES for o in offsets})
    for slot, b in enumerate(residues):
        bank_ref[slot] = x if b == 0 else pltpu.roll(x, rows - b, 0)
    return {o: (residues.index(o % SUBLANES), o - o % SUBLANES) for o in offsets}


def _n_residues(offsets):
    return len({o % SUBLANES for o in offsets})


def conv_fwd(name, k, halo, pre_fn, pre, w, post_fns, extras, params, c_total, tc, tm_pref=512):
    t = pre[0][0].shape[0]
    tm = _tile(t, tm_pref, halo)
    npre, nex, npar = len(pre), len(extras), len(params)
    ngr = tc // LANES
    taps = [halo - (k - 1) + j for j in range(k)]

    def body(*refs):
        prev = refs[:npre]
        cur = refs[npre:2 * npre]
        w_ref = refs[2 * npre]
        ex = refs[2 * npre + 1:2 * npre + 1 + nex]
        par = refs[2 * npre + 1 + nex:2 * npre + 1 + nex + npar]
        out_ref, buf, bank = refs[-3], refs[-2], refs[-1]
        i = pl.program_id(1)
        pv = pre_fn(*[r[...] for r in prev])
        buf[0:halo, :] = jnp.where(i > 0, pv, 0.0)
        buf[halo:, :] = pre_fn(*[r[...] for r in cur])
        where = _roll_bank(buf[...], bank, taps)
        acc = None
        for j, o in enumerate(taps):
            slot, st = where[o]
            term = w_ref[j:j + 1, :] * bank[slot, st:st + tm, :]
            acc = term if acc is None else acc + term
        for gi in range(ngr):
            sl = slice(gi * LANES, (gi + 1) * LANES)
            out_ref[:, sl] = post_fns[gi](acc[:, sl], *[e[:, sl] for e in ex], *[p_[:, sl] for p_ in par])

    hb = tm // halo
    in_specs = ([pl.BlockSpec((halo, tc), functools.partial(lambda j, i, o: (jnp.maximum(i * hb - 1, 0), o + j), o=col // tc))
                 for _, col in pre]
                + [pl.BlockSpec((tm, tc), functools.partial(lambda j, i, o: (i, o + j), o=col // tc)) for _, col in pre]
                + [pl.BlockSpec((k, tc), lambda j, i: (0, j))]
                + [pl.BlockSpec((tm, tc), functools.partial(lambda j, i, o: (i, o + j), o=col // tc)) for _, col in extras]
                + [pl.BlockSpec((1, tc), lambda j, i: (0, j)) for _ in params])
    args = [a for a, _ in pre] * 2 + [w] + [a for a, _ in extras] + list(params)
    return pl.pallas_call(
        body, name=name, grid=(c_total // tc, t // tm), in_specs=in_specs,
        out_specs=pl.BlockSpec((tm, tc), lambda j, i: (i, j)),
        out_shape=jax.ShapeDtypeStruct((t, c_total), F32),
        scratch_shapes=[pltpu.VMEM((halo + tm, tc), F32), pltpu.VMEM((_n_residues(taps), halo + tm, tc), F32)],
        compiler_params=_cp(("parallel", "arbitrary")),
    )(*args)


def conv_bwd(name, k, halo, pre_fn, pre, w, post_fns, extras, params, dout, c_total, tc, tm_pref=512):
    t = pre[0][0].shape[0]
    tm = _tile(t, tm_pref, halo)
    npre, nex, npar = len(pre), len(extras), len(params)
    ngr = tc // LANES
    nout = npre + nex
    assert nout == 1 or c_total == tc
    nblk = t // tm
    ext = tm + halo
    taps = [halo - (k - 1) + j for j in range(k)]
    back = [k - 1 - j for j in range(k)]

    def body(*refs):
        it = iter(refs)
        prev = [next(it) for _ in range(npre)]
        cur = [next(it) for _ in range(npre)]
        nxt = [next(it) for _ in range(npre)]
        w_ref = next(it)
        ex_c = [next(it) for _ in range(nex)]
        ex_n = [next(it) for _ in range(nex)]
        par = [next(it) for _ in range(npar)]
        do_c, do_n = next(it), next(it)
        din_ref, dw_ref = next(it), next(it)
        dpar = [next(it) for _ in range(npar)]
        buf, dbuf, bank, dbank = next(it), next(it), next(it), next(it)
        i = pl.program_id(1)

        @pl.when(i == 0)
        def _():
            dw_ref[...] = jnp.zeros_like(dw_ref)
            for r in dpar:
                r[...] = jnp.zeros_like(r)

        buf[0:halo, :] = jnp.where(i > 0, pre_fn(*[r[...] for r in prev]), 0.0)
        cur_vals = [r[...] for r in cur]
        buf[halo:halo + tm, :] = pre_fn(*cur_vals)
        buf[halo + tm:, :] = pre_fn(*[r[...] for r in nxt])
        where = _roll_bank(buf[...], bank, taps)
        conv = None
        for j, o in enumerate(taps):
            slot, st = where[o]
            term = w_ref[j:j + 1, :] * bank[slot, st:st + ext, :]
            conv = term if conv is None else conv + term
        don = jnp.where(i < nblk - 1, do_n[...], 0.0)
        for gi in range(ngr):
            sl = slice(gi * LANES, (gi + 1) * LANES)
            pv = [p_[:, sl] for p_ in par]
            _, vj = jax.vjp(post_fns[gi], conv[:tm, sl], *[e[:, sl] for e in ex_c], *pv)
            gr = vj(do_c[:, sl])
            dbuf[0:tm, sl] = gr[0]
            for e in range(nex):
                din_ref[:, (npre + e) * tc + gi * LANES:(npre + e) * tc + (gi + 1) * LANES] = gr[1 + e].astype(din_ref.dtype)
            for q_ in range(npar):
                dpar[q_][:, sl] += gr[1 + nex + q_]
            _, vjn = jax.vjp(post_fns[gi], conv[tm:, sl], *[e[:, sl] for e in ex_n], *pv)
            dbuf[tm:, sl] = vjn(don[:, sl])[0]
        dcur = dbuf[0:tm, :]
        dwhere = _roll_bank(dbuf[...], dbank, back)
        dpre = None
        for j in range(k):
            slot, st = dwhere[back[j]]
            term = w_ref[j:j + 1, :] * dbank[slot, st:st + tm, :]
            dpre = term if dpre is None else dpre + term
            slot, st = where[taps[j]]
            dw_ref[j:j + 1, :] += jnp.sum(dcur * bank[slot, st:st + tm, :], axis=0, keepdims=True)
        _, vjp_pre = jax.vjp(pre_fn, *cur_vals)
        for e, gval in enumerate(vjp_pre(dpre)):
            din_ref[:, e * tc:(e + 1) * tc] = gval.astype(din_ref.dtype)

    hb = tm // halo
    last_h = t // halo - 1

    def spec(kind, col):
        o = col // tc
        if kind == "prev":
            return pl.BlockSpec((halo, tc), lambda j, i: (jnp.maximum(i * hb - 1, 0), o + j))
        if kind == "next":
            return pl.BlockSpec((halo, tc), lambda j, i: (jnp.minimum((i + 1) * hb, last_h), o + j))
        return pl.BlockSpec((tm, tc), lambda j, i: (i, o + j))

    in_specs = ([spec("prev", col) for _, col in pre] + [spec("cur", col) for _, col in pre]
                + [spec("next", col) for _, col in pre] + [pl.BlockSpec((k, tc), lambda j, i: (0, j))]
                + [spec("cur", col) for _, col in extras] + [spec("next", col) for _, col in extras]
                + [pl.BlockSpec((1, tc), lambda j, i: (0, j)) for _ in params]
                + [spec("cur", 0), spec("next", 0)])
    args = [a for a, _ in pre] * 3 + [w] + [a for a, _ in extras] * 2 + list(params) + [dout, dout]
    out = pl.pallas_call(
        body, name=name, grid=(c_total // tc, nblk), in_specs=in_specs,
        out_specs=[pl.BlockSpec((tm, nout * tc), lambda j, i: (i, j)), pl.BlockSpec((k, tc), lambda j, i: (0, j))]
        + [pl.BlockSpec((1, tc), lambda j, i: (0, j)) for _ in params],
        out_shape=[jax.ShapeDtypeStruct((t, nout * c_total), BF16), jax.ShapeDtypeStruct((k, c_total), F32)]
        + [jax.ShapeDtypeStruct((1, c_total), F32) for _ in params],
        scratch_shapes=[pltpu.VMEM((2 * halo + tm, tc), F32), pltpu.VMEM((ext, tc), F32),
                        pltpu.VMEM((_n_residues(taps), 2 * halo + tm, tc), F32),
                        pltpu.VMEM((_n_residues(back), ext, tc), F32)],
        compiler_params=_cp(("parallel", "arbitrary")),
    )(*args)
    return out[0], out[1], out[2:]


def _head_pick(ref_val, row):
    rr = lax.broadcasted_iota(jnp.int32, ref_val.shape, 0)
    v = jnp.sum(jnp.where(rr == row, ref_val, 0.0), axis=0, keepdims=True)
    ll = lax.broadcasted_iota(jnp.int32, v.shape, 1)
    return jnp.sum(jnp.where(ll == 0, v, 0.0), axis=1, keepdims=True)


def _lane_col(blk, lane_idx):
    ll = lax.broadcasted_iota(jnp.int32, blk.shape, 1)
    return jnp.sum(jnp.where(ll == lane_idx, blk, 0.0), axis=1, keepdims=True)


WY_HEADS = 2
WY_UNROLL = 4
WY_UNROLL_BWD = 4


def dn_wy_fwd(qkv, p, ba_col, alog_b, dtb_b, nheads):
    t = qkv.shape[0]
    tm = _tile(t, 512, CHUNK * WY_UNROLL)
    nc = tm // CHUNK
    hb = WY_HEADS
    hp = alog_b.shape[0]
    w_ = hb * LANES

    def body(qkv_ref, ba_ref, al_ref, dt_ref, u_ref, w_ref, qg_ref, kd_ref, qk_ref, e_ref, ti_ref):
        hblk = pl.program_id(1)
        alv, dtv = al_ref[...], dt_ref[...]

        def trip(cj, carry):
            units = [(cj * WY_UNROLL + cu, hl) for cu in range(WY_UNROLL) for hl in range(hb)]
            args = [[] for _ in range(7)]
            for ci, hl in units:
                rows = pl.ds(pl.multiple_of(ci * CHUNK, CHUNK), CHUNK)
                ba = ba_ref[rows, :]
                h = hblk * hb + hl
                for lst, val in zip(args, (qkv_ref[rows, hl * 384:hl * 384 + 128],
                                           qkv_ref[rows, hl * 384 + 128:hl * 384 + 256],
                                           qkv_ref[rows, hl * 384 + 256:hl * 384 + 384],
                                           _lane_col(ba, h), _lane_col(ba, nheads + h),
                                           _head_pick(alv, h), _head_pick(dtv, h))):
                    lst.append(val)
            outs = _wy(*args)
            for n, (ci, hl) in enumerate(units):
                rows = pl.ds(pl.multiple_of(ci * CHUNK, CHUNK), CHUNK)
                u, w, qg, kd, qk, e, ti = [o[n] for o in outs]
                sl = slice(hl * LANES, (hl + 1) * LANES)
                u_ref[rows, sl] = u
                w_ref[rows, sl] = w.astype(BF16)
                qg_ref[rows, sl] = qg.astype(BF16)
                kd_ref[rows, sl] = kd.astype(BF16)
                qk_ref[rows, hl * LANES:hl * LANES + CHUNK] = qk.astype(BF16)
                qk_ref[rows, hl * LANES + CHUNK:(hl + 1) * LANES] = jnp.zeros((CHUNK, LANES - CHUNK), BF16)
                e_ref[ci, :, sl] = jnp.broadcast_to(e, (1, LANES))
                ti_ref[rows, hl * LANES:hl * LANES + CHUNK] = ti
                ti_ref[rows, hl * LANES + CHUNK:(hl + 1) * LANES] = jnp.zeros((CHUNK, LANES - CHUNK), F32)
            return carry

        lax.fori_loop(0, nc // WY_UNROLL, trip, 0)

    bc = ba_col // LANES
    blk = pl.BlockSpec((tm, w_), lambda i, h: (i, h))
    tab = pl.BlockSpec((hp, LANES), lambda i, h: (0, 0))
    wide = lambda dt: jax.ShapeDtypeStruct((t, nheads * LANES), dt)
    return pl.pallas_call(
        body, name="dn_wy_fwd", grid=(t // tm, nheads // hb),
        in_specs=[pl.BlockSpec((tm, hb * 384), lambda i, h: (i, h)), pl.BlockSpec((tm, LANES), lambda i, h: (i, bc)),
                  tab, tab],
        out_specs=[blk] * 5 + [pl.BlockSpec((nc, 1, w_), lambda i, h: (i, 0, h)), blk],
        out_shape=[wide(F32), wide(BF16), wide(BF16), wide(BF16), wide(BF16),
                   jax.ShapeDtypeStruct((t // CHUNK, 1, nheads * LANES), F32), wide(F32)],
        compiler_params=_cp(("parallel", "parallel")),
    )(qkv, p, alog_b, dtb_b)


def dn_seq_fwd(u, w, qg, kd, qk, e, p, z_col, ng, nheads, hb):
    t = u.shape[0]
    tm = _tile(t, 512, CHUNK)
    nc = tm // CHUNK
    w_ = hb * LANES

    def body(u_ref, w_ref, qg_ref, kd_ref, qk_ref, e_ref, z_ref, ng_ref, y_ref, ss_ref, s_scr):
        i, hblk = pl.program_id(0), pl.program_id(1)
        for hl in range(hb):
            @pl.when(i == 0)
            def _():
                s_scr[hblk * hb + hl] = jnp.zeros((LANES, LANES), F32)
        ngv = ng_ref[...]

        def chunk(ci, carry):
            rows = pl.ds(pl.multiple_of(ci * CHUNK, CHUNK), CHUNK)
            ev = e_ref[ci]
            sls = [slice(hl * LANES, (hl + 1) * LANES) for hl in range(hb)]
            s = [s_scr[hblk * hb + hl] for hl in range(hb)]
            for hl in range(hb):
                ss_ref[ci, sls[hl], :] = s[hl]
            y, sn = _seq([u_ref[rows, sl] for sl in sls], [w_ref[rows, sl].astype(F32) for sl in sls],
                         [qg_ref[rows, sl].astype(F32) for sl in sls], [kd_ref[rows, sl].astype(F32) for sl in sls],
                         [qk_ref[rows, sl][:, :CHUNK].astype(F32) for sl in sls], [ev[:, sl] for sl in sls],
                         [z_ref[rows, sl] for sl in sls], s, ngv)
            for hl in range(hb):
                y_ref[rows, sls[hl]] = y[hl]
                s_scr[hblk * hb + hl] = sn[hl]
            return carry

        lax.fori_loop(0, nc, chunk, 0)

    zc = z_col // w_
    blk = pl.BlockSpec((tm, w_), lambda i, h: (i, h))
    return pl.pallas_call(
        body, name="dn_seq_fwd", grid=(t // tm, nheads // hb),
        in_specs=[blk] * 5 + [pl.BlockSpec((nc, 1, w_), lambda i, h: (i, 0, h)),
                              pl.BlockSpec((tm, w_), lambda i, h: (i, zc + h)),
                              pl.BlockSpec((1, LANES), lambda i, h: (0, 0))],
        out_specs=[blk, pl.BlockSpec((nc, w_, LANES), lambda i, h: (i, h, 0))],
        out_shape=[jax.ShapeDtypeStruct((t, nheads * LANES), F32),
                   jax.ShapeDtypeStruct((t // CHUNK, nheads * LANES, LANES), F32)],
        scratch_shapes=[pltpu.VMEM((nheads, LANES, LANES), F32)],
        compiler_params=_cp(("arbitrary", "arbitrary")),
    )(u, w, qg, kd, qk, e, p, ng)


def dn_seq_bwd(u, w, qg, kd, qk, e, p, z_col, ng, ss, dy, nheads, hb):
    t = u.shape[0]
    tm = _tile(t, 2048 // hb, CHUNK)
    nc = tm // CHUNK
    nblk = t // tm
    w_ = hb * LANES

    def body(u_ref, w_ref, qg_ref, kd_ref, qk_ref, e_ref, z_ref, ng_ref, ss_ref, dy_ref,
             du_ref, dw_ref, dqg_ref, dkd_ref, dqk_ref, de_ref, dz_ref, dng_ref, ds_scr):
        i, hblk = pl.program_id(0), pl.program_id(1)

        @pl.when((i == 0) & (hblk == 0))
        def _():
            dng_ref[...] = jnp.zeros_like(dng_ref)

        for hl in range(hb):
            @pl.when(i == 0)
            def _():
                ds_scr[hblk * hb + hl] = jnp.zeros((LANES, LANES), F32)
        ngv = ng_ref[...]

        def chunk(cj, carry):
            ci = nc - 1 - cj
            rows = pl.ds(pl.multiple_of(ci * CHUNK, CHUNK), CHUNK)
            ev = e_ref[ci]
            sls = [slice(hl * LANES, (hl + 1) * LANES) for hl in range(hb)]
            _, vj = jax.vjp(_seq, [u_ref[rows, sl] for sl in sls], [w_ref[rows, sl].astype(F32) for sl in sls],
                            [qg_ref[rows, sl].astype(F32) for sl in sls], [kd_ref[rows, sl].astype(F32) for sl in sls],
                            [qk_ref[rows, sl][:, :CHUNK].astype(F32) for sl in sls], [ev[:, sl] for sl in sls],
                            [z_ref[rows, sl] for sl in sls], [ss_ref[ci, sl, :] for sl in sls], ngv)
            du, dw, dqg, dkd, dqk, de, dz, dsp, dng = vj(([dy_ref[rows, sl] for sl in sls],
                                                          [ds_scr[hblk * hb + hl] for hl in range(hb)]))
            for hl, sl in enumerate(sls):
                du_ref[rows, sl] = du[hl]
                dw_ref[rows, sl] = dw[hl]
                dqg_ref[rows, sl] = dqg[hl]
                dkd_ref[rows, sl] = dkd[hl]
                dqk_ref[rows, hl * LANES:hl * LANES + CHUNK] = dqk[hl]
                dqk_ref[rows, hl * LANES + CHUNK:(hl + 1) * LANES] = jnp.zeros((CHUNK, LANES - CHUNK), F32)
                de_ref[ci, :, sl] = de[hl]
                dz_ref[rows, sl] = dz[hl]
                ds_scr[hblk * hb + hl] = dsp[hl]
            dng_ref[...] += dng
            return carry

        lax.fori_loop(0, nc, chunk, 0)

    zc = z_col // w_
    rv = lambda i: nblk - 1 - i
    blk = pl.BlockSpec((tm, w_), lambda i, h: (rv(i), h))
    eblk = pl.BlockSpec((nc, 1, w_), lambda i, h: (rv(i), 0, h))
    one = pl.BlockSpec((1, LANES), lambda i, h: (0, 0))
    wide = jax.ShapeDtypeStruct((t, nheads * LANES), F32)
    return pl.pallas_call(
        body, name="dn_seq_bwd", grid=(nblk, nheads // hb),
        in_specs=[blk] * 5 + [eblk, pl.BlockSpec((tm, w_), lambda i, h: (rv(i), zc + h)), one,
                              pl.BlockSpec((nc, w_, LANES), lambda i, h: (rv(i), h, 0)), blk],
        out_specs=[blk] * 5 + [eblk, blk, one],
        out_shape=[wide] * 5 + [jax.ShapeDtypeStruct((t // CHUNK, 1, nheads * LANES), F32), wide,
                                jax.ShapeDtypeStruct((1, LANES), F32)],
        scratch_shapes=[pltpu.VMEM((nheads, LANES, LANES), F32)],
        compiler_params=_cp(("arbitrary", "arbitrary")),
    )(u, w, qg, kd, qk, e, p, ng, ss, dy)


def dn_wy_bwd(qkv, p, ba_col, alog_b, dtb_b, ti, du, dw, dqg, dkd, dqk, de, nheads):
    t = qkv.shape[0]
    tm = _tile(t, 512, CHUNK * WY_UNROLL_BWD)
    nc = tm // CHUNK
    hb = WY_HEADS
    hp = alog_b.shape[0]
    w_ = hb * LANES

    def body(qkv_ref, ba_ref, al_ref, dt_ref, du_ref, dw_ref, dqg_ref, dkd_ref, dqk_ref, de_ref, ti_ref,
             dqkv_ref, dba_ref, dal_ref, ddt_ref):
        i, hblk = pl.program_id(0), pl.program_id(1)

        @pl.when((i == 0) & (hblk == 0))
        def _():
            dal_ref[...] = jnp.zeros_like(dal_ref)
            ddt_ref[...] = jnp.zeros_like(ddt_ref)

        @pl.when(hblk == 0)
        def _():
            dba_ref[...] = jnp.zeros_like(dba_ref)

        alv, dtv = al_ref[...], dt_ref[...]
        lane = lax.broadcasted_iota(jnp.int32, (CHUNK, LANES), 1)
        rowp = lax.broadcasted_iota(jnp.int32, (hp, LANES), 0)

        def trip(cj, carry):
            units = [(cj * WY_UNROLL_BWD + cu, hl) for cu in range(WY_UNROLL_BWD) for hl in range(hb)]
            args = [[] for _ in range(8)]
            cts = [[] for _ in range(6)]
            for ci, hl in units:
                rows = pl.ds(pl.multiple_of(ci * CHUNK, CHUNK), CHUNK)
                ba = ba_ref[rows, :]
                h = hblk * hb + hl
                sl = slice(hl * LANES, (hl + 1) * LANES)
                for lst, val in zip(args, (qkv_ref[rows, hl * 384:hl * 384 + 128],
                                           qkv_ref[rows, hl * 384 + 128:hl * 384 + 256],
                                           qkv_ref[rows, hl * 384 + 256:hl * 384 + 384],
                                           _lane_col(ba, h), _lane_col(ba, nheads + h),
                                           _head_pick(alv, h), _head_pick(dtv, h), ti_ref[rows, sl][:, :CHUNK])):
                    lst.append(val)
                de11 = jnp.sum(de_ref[ci][:, sl], axis=1, keepdims=True)
                for lst, val in zip(cts, (du_ref[rows, sl], dw_ref[rows, sl], dqg_ref[rows, sl], dkd_ref[rows, sl],
                                          dqk_ref[rows, sl][:, :CHUNK], de11)):
                    lst.append(val)
            _, vj = jax.vjp(_wy, *args)
            grads = vj(tuple(cts))
            for n, (ci, hl) in enumerate(units):
                rows = pl.ds(pl.multiple_of(ci * CHUNK, CHUNK), CHUNK)
                h = hblk * hb + hl
                dq, dk, dv, dbc, dac, dal, ddt = [g_[n] for g_ in grads[:7]]
                dqkv_ref[rows, hl * 384:hl * 384 + 128] = dq
                dqkv_ref[rows, hl * 384 + 128:hl * 384 + 256] = dk
                dqkv_ref[rows, hl * 384 + 256:hl * 384 + 384] = dv
                dba_ref[rows, :] += jnp.where(lane == h, dbc, 0.0) + jnp.where(lane == nheads + h, dac, 0.0)
                dal_ref[...] += jnp.where(rowp == h, dal, 0.0)
                ddt_ref[...] += jnp.where(rowp == h, ddt, 0.0)
            return carry

        lax.fori_loop(0, nc // WY_UNROLL_BWD, trip, 0)

    bc = ba_col // LANES
    blk = pl.BlockSpec((tm, w_), lambda i, h: (i, h))
    tab = pl.BlockSpec((hp, LANES), lambda i, h: (0, 0))
    return pl.pallas_call(
        body, name="dn_wy_bwd", grid=(t // tm, nheads // hb),
        in_specs=[pl.BlockSpec((tm, hb * 384), lambda i, h: (i, h)), pl.BlockSpec((tm, LANES), lambda i, h: (i, bc)),
                  tab, tab] + [blk] * 5 + [pl.BlockSpec((nc, 1, w_), lambda i, h: (i, 0, h)), blk],
        out_specs=[pl.BlockSpec((tm, hb * 384), lambda i, h: (i, h)), pl.BlockSpec((tm, LANES), lambda i, h: (i, 0)),
                   tab, tab],
        out_shape=[jax.ShapeDtypeStruct((t, nheads * 384), F32), jax.ShapeDtypeStruct((t, LANES), F32),
                   jax.ShapeDtypeStruct((hp, LANES), F32), jax.ShapeDtypeStruct((hp, LANES), F32)],
        compiler_params=_cp(("arbitrary", "arbitrary")),
    )(qkv, p, alog_b, dtb_b, du, dw, dqg, dkd, dqk, de, ti)


def sg_fwd(p, col, sg, lg, lb, w, bias_b):
    t = p.shape[0]
    ng_ = sg // LANES
    tm = _tile(t, 256, LANES)
    cb = col // sg

    def body(u_ref, v_ref, g_ref, lg_ref, lb_ref, w_ref, b_ref, y_ref):
        for n in range(tm // LANES):
            rs = slice(n * LANES, (n + 1) * LANES)
            for gi in range(ng_):
                sl = slice(gi * LANES, (gi + 1) * LANES)
                y_ref[rs, sl] = _sg_block(u_ref[rs, sl], v_ref[rs, sl], g_ref[rs, sl], lg_ref[:, sl], lb_ref[:, sl],
                                          w_ref[gi], b_ref[gi])

    vec = pl.BlockSpec((1, sg), lambda i: (0, 0))
    full = pl.BlockSpec((ng_, LANES, LANES), lambda i: (0, 0, 0))
    return pl.pallas_call(
        body, name="sg_fwd", grid=(t // tm,),
        in_specs=[pl.BlockSpec((tm, sg), lambda i: (i, cb)), pl.BlockSpec((tm, sg), lambda i: (i, cb + 1)),
                  pl.BlockSpec((tm, sg), lambda i: (i, cb + 2)), vec, vec, full, full],
        out_specs=pl.BlockSpec((tm, sg), lambda i: (i, 0)),
        out_shape=jax.ShapeDtypeStruct((t, sg), F32),
        compiler_params=_cp(("parallel",)),
    )(p, p, p, lg, lb, w, bias_b)


def sg_bwd(p, col, sg, lg, lb, w, bias_b, dy):
    t = p.shape[0]
    ng_ = sg // LANES
    tm = _tile(t, 256, LANES)
    cb = col // sg

    def body(u_ref, v_ref, g_ref, lg_ref, lb_ref, w_ref, b_ref, dy_ref, d_ref, dlg_ref, dlb_ref, dw_ref, db_ref):
        @pl.when(pl.program_id(0) == 0)
        def _():
            for r in (dlg_ref, dlb_ref, dw_ref, db_ref):
                r[...] = jnp.zeros_like(r)

        for n in range(tm // LANES):
            rs = slice(n * LANES, (n + 1) * LANES)
            for gi in range(ng_):
                sl = slice(gi * LANES, (gi + 1) * LANES)
                _, vj = jax.vjp(_sg_block, u_ref[rs, sl], v_ref[rs, sl], g_ref[rs, sl], lg_ref[:, sl], lb_ref[:, sl],
                                w_ref[gi], b_ref[gi])
                du, dv, dg, dlg, dlb, dw, db = vj(dy_ref[rs, sl])
                d_ref[rs, gi * LANES:(gi + 1) * LANES] = du.astype(BF16)
                d_ref[rs, sg + gi * LANES:sg + (gi + 1) * LANES] = dv.astype(BF16)
                d_ref[rs, 2 * sg + gi * LANES:2 * sg + (gi + 1) * LANES] = dg.astype(BF16)
                dlg_ref[:, sl] += dlg
                dlb_ref[:, sl] += dlb
                dw_ref[gi] += dw
                db_ref[gi] += jnp.broadcast_to(jnp.sum(db, axis=1, keepdims=True), (LANES, LANES))

    vec = pl.BlockSpec((1, sg), lambda i: (0, 0))
    full = pl.BlockSpec((ng_, LANES, LANES), lambda i: (0, 0, 0))
    return pl.pallas_call(
        body, name="sg_bwd", grid=(t // tm,),
        in_specs=[pl.BlockSpec((tm, sg), lambda i: (i, cb)), pl.BlockSpec((tm, sg), lambda i: (i, cb + 1)),
                  pl.BlockSpec((tm, sg), lambda i: (i, cb + 2)), vec, vec, full, full,
                  pl.BlockSpec((tm, sg), lambda i: (i, 0))],
        out_specs=[pl.BlockSpec((tm, 3 * sg), lambda i: (i, 0)), vec, vec, full, full],
        out_shape=[jax.ShapeDtypeStruct((t, 3 * sg), BF16), jax.ShapeDtypeStruct((1, sg), F32),
                   jax.ShapeDtypeStruct((1, sg), F32), jax.ShapeDtypeStruct((ng_, LANES, LANES), F32),
                   jax.ShapeDtypeStruct((ng_, LANES, LANES), F32)],
        compiler_params=_cp(("arbitrary",)),
    )(p, p, p, lg, lb, w, bias_b, dy)


def out_proj(x, y_dn, y_sg, y_cv, wo, gate):
    t, d = x.shape
    dn, sg, cv = y_dn.shape[1], y_sg.shape[1], y_cv.shape[1]
    dmix = dn + sg + cv
    tm = _tile(t, 256, LANES)

    def body(x_ref, a_ref, b_ref, c_ref, w_ref, g_ref, xn_ref, y_ref, yt_ref):
        a, b, c = a_ref[...], b_ref[...], c_ref[...]
        y = (jnp.dot(a.astype(BF16), w_ref[0:dn, :], preferred_element_type=F32)
             + jnp.dot(b.astype(BF16), w_ref[dn:dn + sg, :], preferred_element_type=F32)
             + jnp.dot(c.astype(BF16), w_ref[dn + sg:, :], preferred_element_type=F32))
        y_ref[...] = y
        xn_ref[...] = x_ref[...] + g_ref[...] * y
        yt_ref[0:dn, :] = a.T.astype(BF16)
        yt_ref[dn:dn + sg, :] = b.T.astype(BF16)
        yt_ref[dn + sg:, :] = c.T.astype(BF16)

    row = lambda w_: pl.BlockSpec((tm, w_), lambda i: (i, 0))
    return pl.pallas_call(
        body, name="out_proj", grid=(t // tm,),
        in_specs=[row(d), row(dn), row(sg), row(cv), pl.BlockSpec((dmix, d), lambda i: (0, 0)),
                  pl.BlockSpec((1, d), lambda i: (0, 0))],
        out_specs=[row(d), row(d), pl.BlockSpec((dmix, tm), lambda i: (0, i))],
        out_shape=[jax.ShapeDtypeStruct((t, d), F32), jax.ShapeDtypeStruct((t, d), F32),
                   jax.ShapeDtypeStruct((dmix, t), BF16)],
        compiler_params=_cp(("parallel",)),
    )(x, y_dn, y_sg, y_cv, wo, gate)


def out_proj_bwd(dxn, y, gate, wo, dn, sg, cv):
    t, d = dxn.shape
    dmix = dn + sg + cv
    tm = _tile(t, 256, LANES)

    def body(dx_ref, y_ref, g_ref, w_ref, da_ref, db_ref, dc_ref, dyb_ref, dg_ref):
        @pl.when(pl.program_id(0) == 0)
        def _():
            dg_ref[...] = jnp.zeros_like(dg_ref)
        dx = dx_ref[...]
        dg_ref[...] += jnp.sum(dx * y_ref[...], axis=0, keepdims=True)
        dyb = (dx * g_ref[...]).astype(BF16)
        dyb_ref[...] = dyb
        dcat = lax.dot_general(dyb, w_ref[...], (((1,), (1,)), ((), ())), preferred_element_type=F32)
        da_ref[...] = dcat[:, 0:dn]
        db_ref[...] = dcat[:, dn:dn + sg]
        dc_ref[...] = dcat[:, dn + sg:]

    row = lambda w_: pl.BlockSpec((tm, w_), lambda i: (i, 0))
    vec = pl.BlockSpec((1, d), lambda i: (0, 0))
    return pl.pallas_call(
        body, name="out_proj_bwd", grid=(t // tm,),
        in_specs=[row(d), row(d), vec, pl.BlockSpec((dmix, d), lambda i: (0, 0))],
        out_specs=[row(dn), row(sg), row(cv), row(d), vec],
        out_shape=[jax.ShapeDtypeStruct((t, dn), F32), jax.ShapeDtypeStruct((t, sg), F32),
                   jax.ShapeDtypeStruct((t, cv), F32), jax.ShapeDtypeStruct((t, d), BF16),
                   jax.ShapeDtypeStruct((1, d), F32)],
        compiler_params=_cp(("arbitrary",)),
    )(dxn, y, gate, wo)


def matmul_acc(name, at, b):
    m, t = at.shape
    n = b.shape[1]
    tm, tn, tk = _tile(m, 1024, LANES), _tile(n, 2432, LANES), _tile(t, 1024, LANES)

    def body(a_ref, b_ref, o_ref):
        @pl.when(pl.program_id(2) == 0)
        def _():
            o_ref[...] = jnp.zeros_like(o_ref)
        o_ref[...] += jnp.dot(a_ref[...], b_ref[...], preferred_element_type=F32)

    return pl.pallas_call(
        body, name=name, grid=(m // tm, n // tn, t // tk),
        in_specs=[pl.BlockSpec((tm, tk), lambda i, j, k: (i, k)), pl.BlockSpec((tk, tn), lambda i, j, k: (k, j))],
        out_specs=pl.BlockSpec((tm, tn), lambda i, j, k: (i, j)),
        out_shape=jax.ShapeDtypeStruct((m, n), F32),
        compiler_params=_cp(("parallel", "parallel", "arbitrary")),
    )(at, b)


def in_proj_bwd(dp, wp, x, dxn, shift, scale, ng):
    t, d = x.shape
    npc = wp.shape[1]
    tm, tk = _tile(t, 512, LANES), _tile(npc, 2432, LANES)

    def mm_body(dp_ref, w_ref, dh_ref):
        @pl.when(pl.program_id(1) == 0)
        def _():
            dh_ref[...] = jnp.zeros_like(dh_ref)
        dh_ref[...] += lax.dot_general(dp_ref[...], w_ref[...], (((1,), (1,)), ((), ())), preferred_element_type=F32)

    dh = pl.pallas_call(
        mm_body, name="in_proj_bwd", grid=(t // tm, npc // tk),
        in_specs=[pl.BlockSpec((tm, tk), lambda i, k: (i, k)), pl.BlockSpec((d, tk), lambda i, k: (0, k))],
        out_specs=pl.BlockSpec((tm, d), lambda i, k: (i, 0)), out_shape=jax.ShapeDtypeStruct((t, d), F32),
        compiler_params=_cp(("parallel", "arbitrary")),
    )(dp, wp)

    tr = _tile(t, 256, 8)

    def norm_body(dh_ref, x_ref, dxn_ref, sh_ref, sc_ref, g_ref, dx_ref, dg_ref, dsc_ref, dsh_ref):
        @pl.when(pl.program_id(0) == 0)
        def _():
            for r in (dg_ref, dsc_ref, dsh_ref):
                r[...] = jnp.zeros_like(r)
        _, vj = jax.vjp(_modnorm, x_ref[...], g_ref[...], sc_ref[...], sh_ref[...])
        dx, dg, dsc, dsh = vj(dh_ref[...])
        dx_ref[...] = dxn_ref[...] + dx
        dg_ref[...] += dg
        dsc_ref[...] += dsc
        dsh_ref[...] += dsh

    vec = pl.BlockSpec((1, d), lambda i: (0, 0))
    row = pl.BlockSpec((tr, d), lambda i: (i, 0))
    return pl.pallas_call(
        norm_body, name="modnorm_bwd", grid=(t // tr,), in_specs=[row, row, row, vec, vec, vec],
        out_specs=[row, vec, vec, vec],
        out_shape=[jax.ShapeDtypeStruct((t, d), F32)] + [jax.ShapeDtypeStruct((1, d), F32)] * 3,
        compiler_params=_cp(("arbitrary",)),
    )(dh, x, dxn, shift, scale, ng)


def loss_head(x, tgt, fg):
    t, d = x.shape
    tm = _tile(t, 512, 8)

    def body(x_ref, t_ref, g_ref, l_ref, dx_ref, dg_ref):
        @pl.when(pl.program_id(0) == 0)
        def _():
            l_ref[...] = jnp.zeros_like(l_ref)
            dg_ref[...] = jnp.zeros_like(dg_ref)
        y, vj = jax.vjp(_rmsnorm, x_ref[...], g_ref[...])
        err = y - t_ref[...]
        part = 0.5 * jnp.sum(jnp.sum(err * err, axis=1, keepdims=True), axis=0, keepdims=True) / d
        l_ref[...] += jnp.broadcast_to(part, l_ref.shape)
        dx, dg = vj(err / d)
        dx_ref[...] = dx
        dg_ref[...] += dg

    row = pl.BlockSpec((tm, d), lambda i: (i, 0))
    vec = pl.BlockSpec((1, d), lambda i: (0, 0))
    return pl.pallas_call(
        body, name="loss_head", grid=(t // tm,), in_specs=[row, row, vec],
        out_specs=[pl.BlockSpec((1, LANES), lambda i: (0, 0)), row, vec],
        out_shape=[jax.ShapeDtypeStruct((1, LANES), F32), jax.ShapeDtypeStruct((t, d), F32),
                   jax.ShapeDtypeStruct((1, d), F32)],
        compiler_params=_cp(("arbitrary",)),
    )(x, tgt, fg)


def adamw(name, w, g, m, v):
    r, c = w.shape
    tr = _tile(r, 256, 8) if r % 8 == 0 else r

    def body(w_ref, g_ref, m_ref, v_ref, d_ref, mo_ref, vo_ref):
        d_ref[...], mo_ref[...], vo_ref[...] = _adamw_math(w_ref[...], g_ref[...], m_ref[...], v_ref[...])

    blk = pl.BlockSpec((tr, c), lambda i: (i, 0))
    return pl.pallas_call(
        body, name=name, grid=(r // tr,), in_specs=[blk] * 4, out_specs=[blk] * 3,
        out_shape=[jax.ShapeDtypeStruct((r, c), F32)] * 3, compiler_params=_cp(("parallel",)),
    )(w, g, m, v)


def ada_fwd(c_all, w_ada, b_loc):
    nl, d, cols = w_ada.shape
    nb = c_all.shape[0]
    tn = _tile(cols, 512, LANES)

    def body(c_ref, w_ref, b_ref, o_ref):
        ca = _silu(c_ref[...]).astype(BF16)
        o_ref[0] = jnp.dot(ca, w_ref[0].astype(BF16), preferred_element_type=F32) + b_ref[0]

    return pl.pallas_call(
        body, name="ada_fwd", grid=(nl, cols // tn),
        in_specs=[pl.BlockSpec((nb, d), lambda l, j: (0, 0)), pl.BlockSpec((1, d, tn), lambda l, j: (l, 0, j)),
                  pl.BlockSpec((1, 1, tn), lambda l, j: (l, 0, j))],
        out_specs=pl.BlockSpec((1, nb, tn), lambda l, j: (l, 0, j)),
        out_shape=jax.ShapeDtypeStruct((nl, nb, cols), F32),
        compiler_params=_cp(("parallel", "parallel")),
    )(c_all, w_ada, b_loc)


def ada_bwd(c_all_t, dmod_loc, w, m, v):
    nl, d, cols = w.shape
    nb = c_all_t.shape[1]
    tr = _tile(d, 256, 8)

    def body(c_ref, dm_ref, w_ref, m_ref, v_ref, g_ref, d_ref, mo_ref, vo_ref):
        ca = _silu(c_ref[...])
        dm = dm_ref[0]
        g = _lane_col(ca, 0) * dm[0:1, :]
        for b in range(1, nb):
            g = g + _lane_col(ca, b) * dm[b:b + 1, :]
        g_ref[0] = g
        d_ref[0], mo_ref[0], vo_ref[0] = _adamw_math(w_ref[0], g, m_ref[0], v_ref[0])

    blk = pl.BlockSpec((1, tr, cols), lambda l, i: (l, i, 0))
    return pl.pallas_call(
        body, name="ada_bwd", grid=(nl, d // tr),
        in_specs=[pl.BlockSpec((tr, nb), lambda l, i: (i, 0)), pl.BlockSpec((1, nb, cols), lambda l, i: (l, 0, 0)),
                  blk, blk, blk],
        out_specs=[blk] * 4, out_shape=[jax.ShapeDtypeStruct((nl, d, cols), F32)] * 4,
        compiler_params=_cp(("parallel", "parallel")),
    )(c_all_t, dmod_loc, w, m, v)


def sum8(g):
    _, r, c = g.shape
    tr = _tile(r, 256, 8)

    def body(g_ref, o_ref):
        acc = g_ref[0]
        for k in range(1, NDEV):
            acc = acc + g_ref[k]
        o_ref[...] = acc

    return pl.pallas_call(
        body, name="sum8", grid=(r // tr,), in_specs=[pl.BlockSpec((NDEV, tr, c), lambda i: (0, i, 0))],
        out_specs=pl.BlockSpec((tr, c), lambda i: (i, 0)), out_shape=jax.ShapeDtypeStruct((r, c), F32),
        compiler_params=_cp(("parallel",)),
    )(g)


def pair_sum(cflag, g, r1):
    _, ns, r, c = g.shape
    tr = _tile(r, 256, 8)

    def body(cf_ref, g0_ref, g1_ref, r_ref, ob_ref):
        keep = jnp.where(cf_ref[0:1, 0:1] == 0.0, g0_ref[0], g1_ref[0])
        ob_ref[...] = (keep + r_ref[...]).astype(BF16)

    blk = pl.BlockSpec((1, tr, c), lambda s, i: (s, i, 0))
    return pl.pallas_call(
        body, name="pair_sum", grid=(ns, r // tr),
        in_specs=[pl.BlockSpec((1, LANES), lambda s, i: (0, 0)), pl.BlockSpec((1, 1, tr, c), lambda s, i: (0, s, i, 0)),
                  pl.BlockSpec((1, 1, tr, c), lambda s, i: (1, s, i, 0)), blk], out_specs=blk,
        out_shape=jax.ShapeDtypeStruct((ns, r, c), BF16), compiler_params=_cp(("parallel", "parallel")),
    )(cflag, g, g, r1)


def relay_sum(px, py, rx, ry):
    r, c = px.shape
    nh = r // 2
    tr = _tile(nh, 256, 8)

    def body(px_ref, py_ref, rx_ref, ry_ref, qx_ref, qy_ref):
        upper = pl.program_id(0) == 0
        qx_ref[...] = (px_ref[...].astype(F32) + jnp.where(upper, 0.0, ry_ref[...].astype(F32))).astype(BF16)
        qy_ref[...] = (py_ref[...].astype(F32) + jnp.where(upper, rx_ref[...].astype(F32), 0.0)).astype(BF16)

    full = pl.BlockSpec((tr, c), lambda h, i: (h * (nh // tr) + i, 0))
    half = pl.BlockSpec((tr, c), lambda h, i: (i, 0))
    return pl.pallas_call(
        body, name="relay_sum", grid=(2, nh // tr), in_specs=[full, full, half, half], out_specs=[full, full],
        out_shape=[jax.ShapeDtypeStruct((r, c), BF16)] * 2, compiler_params=_cp(("parallel", "parallel")),
    )(px, py, rx, ry)


def chip_sum(own, r1, r2):
    r, c = own.shape
    tr = _tile(r, 256, 8)
    nslot = r2.shape[0]

    def body(g_ref, r1_ref, r2_ref, o_ref):
        acc = g_ref[...] + r1_ref[...]
        for k in range(nslot):
            acc = acc + r2_ref[k].astype(F32)
        o_ref[...] = acc

    blk = pl.BlockSpec((tr, c), lambda i: (i, 0))
    return pl.pallas_call(
        body, name="chip_sum", grid=(r // tr,),
        in_specs=[blk, blk, pl.BlockSpec((nslot, tr, c), lambda i: (0, i, 0))], out_specs=blk,
        out_shape=jax.ShapeDtypeStruct((r, c), F32), compiler_params=_cp(("parallel",)),
    )(own, r1, r2)


def adamw_halves(name, cflag, w, own, recv, m, v):
    _, r, c = w.shape
    tr = _tile(r, 256, 8)

    def body(cf_ref, w_ref, a_ref, b_ref, m_ref, v_ref, g_ref, d_ref, mo_ref, vo_ref):
        is_own = cf_ref[0:1, 0:1] == pl.program_id(0).astype(F32)
        g = jnp.where(is_own, a_ref[...], b_ref[...])
        g_ref[0] = g
        d_ref[0], mo_ref[0], vo_ref[0] = _adamw_math(w_ref[0], g, m_ref[0], v_ref[0])

    blk = pl.BlockSpec((1, tr, c), lambda h, i: (h, i, 0))
    hlf = pl.BlockSpec((tr, c), lambda h, i: (i, 0))
    return pl.pallas_call(
        body, name=name, grid=(2, r // tr),
        in_specs=[pl.BlockSpec((1, LANES), lambda h, i: (0, 0)), blk, hlf, hlf, blk, blk], out_specs=[blk] * 4,
        out_shape=[jax.ShapeDtypeStruct(w.shape, F32)] * 4, compiler_params=_cp(("parallel", "parallel")),
    )(cflag, w, own, recv, m, v)


def _me():
    return lax.axis_index("x"), lax.axis_index("y"), lax.axis_index("c")


_FLIPS = ((1, 0), (0, 1), (1, 1))


def all_gather8(v):
    m_per, n = v.shape

    def body(x_ref, out_ref, send_sems, recv_sems, local_sem):
        x, y, c = _me()
        me, sibling = (x, y, c), (x, y, 1 - c)
        chips = [(x ^ fx, y ^ fy) for fx, fy in _FLIPS]

        def rows(px, py, pc):
            return out_ref.at[pl.ds((4 * px + 2 * py + pc) * m_per, m_per), :]

        def copy(k, block, to, src=None):
            return pltpu.make_async_remote_copy(
                src_ref=rows(*block) if src is None else src, dst_ref=rows(*block),
                send_sem=send_sems.at[k], recv_sem=recv_sems.at[k], device_id=to, device_id_type=MESH)

        mine = pltpu.make_async_copy(x_ref, rows(*me), local_sem)
        mine.start()
        first = [copy(0, me, sibling, src=x_ref)]
        first += [copy(1 + j, me, (*chip, c), src=x_ref) for j, chip in enumerate(chips)]
        for cp in first:
            cp.start()
        passed = [copy(4 + j, (*chip, c), sibling) for j, chip in enumerate(chips)]
        for j, chip in enumerate(chips):
            copy(1 + j, (*chip, c), me).wait_recv()
            passed[j].start()
        copy(0, sibling, me).wait_recv()
        for j, chip in enumerate(chips):
            copy(4 + j, (*chip, 1 - c), me).wait_recv()
        for cp in first + passed:
            cp.wait_send()
        mine.wait()

    return pl.pallas_call(
        body, name="all_gather8", out_shape=jax.ShapeDtypeStruct((NDEV * m_per, n), v.dtype),
        in_specs=[pl.BlockSpec(memory_space=pltpu.VMEM)], out_specs=pl.BlockSpec(memory_space=pltpu.VMEM),
        scratch_shapes=[pltpu.SemaphoreType.DMA((7,)), pltpu.SemaphoreType.DMA((7,)), pltpu.SemaphoreType.DMA],
        compiler_params=pltpu.CompilerParams(vmem_limit_bytes=VMEM_LIMIT),
    )(v)


def gather_weights(ws):
    na = len(ws)
    ncp = 9

    def body(*refs):
        srcs, outs = refs[:na], refs[na:2 * na]
        send_sems, recv_sems = refs[2 * na:]
        x, y, c = _me()
        chip, cx, cy, cd = 2 * x + y, 2 * (1 - x) + y, 2 * x + 1 - y, 2 * (1 - x) + 1 - y
        nx, ny, sibling = (1 - x, y, c), (x, 1 - y, c), (x, y, 1 - c)

        def copy(a, k, src, dst, to):
            return pltpu.make_async_remote_copy(
                src_ref=src, dst_ref=dst, send_sem=send_sems.at[a * ncp + k], recv_sem=recv_sems.at[a * ncp + k],
                device_id=to, device_id_type=MESH)

        def land(a, k, ch, q):
            return copy(a, k, srcs[a].at[c, q], outs[a].at[ch, c, q], sibling)

        sends = []
        for a in range(na):
            for k, q, to in ((0, 0, nx), (2, 1, ny), (1, 1, nx), (3, 0, ny)):
                sends.append(copy(a, k, srcs[a].at[c, q], outs[a].at[chip, c, q], to))
                sends[-1].start()
        for a in range(na):
            land(a, 0, cx, 0).wait_recv()
            sends.append(copy(a, 4, outs[a].at[cx, c, 0], outs[a].at[cx, c, 0], ny))
            sends[-1].start()
            land(a, 2, cy, 1).wait_recv()
            sends.append(copy(a, 5, outs[a].at[cy, c, 1], outs[a].at[cy, c, 1], nx))
            sends[-1].start()
        for a in range(na):
            for k, ch, q in ((1, cx, 1), (3, cy, 0), (4, cd, 0), (5, cd, 1)):
                land(a, k, ch, q).wait_recv()
            for j, ch in enumerate((cx, cy, cd)):
                sends.append(copy(a, 6 + j, outs[a].at[ch, c], outs[a].at[ch, c], sibling))
                sends[-1].start()
        for a in range(na):
            for j, ch in enumerate((cx, cy, cd)):
                copy(a, 6 + j, outs[a].at[ch, c], outs[a].at[ch, 1 - c], sibling).wait_recv()
        for cp in sends:
            cp.wait_send()

    return pl.pallas_call(
        body, name="gather_weights",
        out_shape=[jax.ShapeDtypeStruct((NCHIP,) + w.shape, w.dtype) for w in ws],
        in_specs=[ANY] * na, out_specs=[ANY] * na,
        scratch_shapes=[pltpu.SemaphoreType.DMA((ncp * na,)), pltpu.SemaphoreType.DMA((ncp * na,))],
    )(*ws)


def sibling_swap(name, gs, other_half=False):
    na = len(gs)

    def body(*refs):
        srcs, outs = refs[:na], refs[na:2 * na]
        send_sems, recv_sems = refs[2 * na:]
        x, y, c = _me()
        cps = [pltpu.make_async_remote_copy(
            src_ref=srcs[a].at[1 - c] if other_half else srcs[a], dst_ref=outs[a], send_sem=send_sems.at[a],
            recv_sem=recv_sems.at[a], device_id=(x, y, 1 - c), device_id_type=MESH) for a in range(na)]
        for cp in cps:
            cp.start()
        for cp in cps:
            cp.wait()

    return pl.pallas_call(
        body, name=name, out_shape=[jax.ShapeDtypeStruct(g.shape[1:] if other_half else g.shape, g.dtype) for g in gs],
        in_specs=[ANY] * na, out_specs=[ANY] * na,
        scratch_shapes=[pltpu.SemaphoreType.DMA((na,)), pltpu.SemaphoreType.DMA((na,))],
    )(*gs)


def relay_out(ps):
    na = len(ps)

    def body(*refs):
        srcs, outs = refs[:na], refs[na:3 * na]
        send_sems, recv_sems = refs[3 * na:]
        x, y, c = _me()
        cd = 2 * (1 - x) + 1 - y
        cps = []
        for a in range(na):
            nh = srcs[a].shape[1] // 2
            for j, to in enumerate(((1 - x, y, c), (x, 1 - y, c))):
                cps.append(pltpu.make_async_remote_copy(
                    src_ref=srcs[a].at[cd, pl.ds(j * nh, nh)], dst_ref=outs[2 * a + j], send_sem=send_sems.at[2 * a + j],
                    recv_sem=recv_sems.at[2 * a + j], device_id=to, device_id_type=MESH))
        for cp in cps:
            cp.start()
        for cp in cps:
            cp.wait()

    shapes = [jax.ShapeDtypeStruct((p_.shape[1] // 2,) + p_.shape[2:], p_.dtype) for p_ in ps for _ in range(2)]
    return pl.pallas_call(
        body, name="relay_out", out_shape=shapes, in_specs=[ANY] * na, out_specs=[ANY] * (2 * na),
        scratch_shapes=[pltpu.SemaphoreType.DMA((2 * na,)), pltpu.SemaphoreType.DMA((2 * na,))],
    )(*ps)


def neighbour_exchange(qs):
    na = len(qs) // 2

    def body(*refs):
        srcs, outs = refs[:2 * na], refs[2 * na:3 * na]
        send_sems, recv_sems = refs[3 * na:]
        x, y, c = _me()
        cps = []
        for a in range(na):
            for j, to in enumerate(((1 - x, y, c), (x, 1 - y, c))):
                cps.append(pltpu.make_async_remote_copy(
                    src_ref=srcs[2 * a + j], dst_ref=outs[a].at[j], send_sem=send_sems.at[2 * a + j],
                    recv_sem=recv_sems.at[2 * a + j], device_id=to, device_id_type=MESH))
        for cp in cps:
            cp.start()
        for cp in cps:
            cp.wait()

    return pl.pallas_call(
        body, name="chip_exchange",
        out_shape=[jax.ShapeDtypeStruct((2,) + qs[2 * a].shape, qs[2 * a].dtype) for a in range(na)],
        in_specs=[ANY] * (2 * na), out_specs=[ANY] * na,
        scratch_shapes=[pltpu.SemaphoreType.DMA((2 * na,)), pltpu.SemaphoreType.DMA((2 * na,))],
    )(*qs)


class _Cfg:
    def __init__(self, x, a_log, sg_w, cv_ln_g, cv_w, conv_qkv):
        self.t, self.d = x.shape[1], x.shape[2]
        self.nl, self.h = a_log.shape
        self.dn = self.h * LANES
        self.g = sg_w.shape[1]
        self.sg = self.g * LANES
        self.cv = cv_ln_g.shape[1]
        self.kc = cv_w.shape[1]
        self.k4 = conv_qkv.shape[1]
        self.o_z = 3 * self.dn
        self.o_sg = 4 * self.dn
        self.o_cv = self.o_sg + 3 * self.sg
        self.o_ba = self.o_cv + 3 * self.cv
        self.npc = self.o_ba + LANES
        self.d_in = self.o_ba + 2 * self.h
        self.dmix = self.dn + self.sg + self.cv
        self.hb_fwd = _tile(self.h, 8, 1)
        self.hb_bwd = _tile(self.h, 8, 1)


def _runs(cfg):
    dn, h = cfg.dn, cfg.h
    runs = [(part * dn + hd * LANES, hd * 3 * LANES + part * LANES, LANES) for part in range(3) for hd in range(h)]
    return runs + [(3 * dn, 3 * dn, dn), (4 * dn, cfg.o_ba, 2 * h), (4 * dn + 2 * h, 4 * dn, cfg.o_ba - 4 * dn)]


def _assemble_perm(cfg, shards):
    cols = shards[0].shape[-1]
    pieces = []
    for nat, _, wdt in sorted(_runs(cfg), key=lambda r_: r_[1]):
        a = nat
        while a < nat + wdt:
            s = a // cols
            b = min(nat + wdt, (s + 1) * cols)
            pieces.append(shards[s][..., a - s * cols:b - s * cols])
            a = b
    pieces.append(jnp.zeros(shards[0].shape[:-1] + (cfg.npc - cfg.o_ba - 2 * cfg.h,), shards[0].dtype))
    return jnp.concatenate(pieces, axis=-1)


def _natural_pieces(cfg, s, cols):
    lo, hi = s * cols, (s + 1) * cols
    pieces = []
    for nat, perm, wdt in sorted(_runs(cfg)):
        a, b = max(nat, lo), min(nat + wdt, hi)
        if a < b:
            pieces.append((perm + a - nat, perm + b - nat))
    return pieces


def _layer_fwd(cfg, x, mod, lw):
    shift, scale, gate = mod
    p, ht = in_proj(x, shift, scale, lw["norm_g"], lw["wp"])
    qk_post = [_qk_post, _qk_post, _v_post]
    qkv = conv_fwd("dn_pre_fwd", cfg.k4, HALO4, lambda a: a, [(p, 0)], lw["conv_qkv"], qk_post, [], [],
                   3 * cfg.dn, 3 * LANES)
    wy = dn_wy_fwd(qkv, p, cfg.o_ba, lw["alog_b"], lw["dtb_b"], cfg.h)
    y_dn, ss = dn_seq_fwd(*wy[:6], p, cfg.o_z, lw["dn_norm_g"], cfg.h, cfg.hb_fwd)
    y_sg = sg_fwd(p, cfg.o_sg, cfg.sg, lw["sg_ln_g"], lw["sg_ln_b"], lw["sg_w"], lw["sg_bias_b"])
    cv_post = [_cv_post] * (cfg.cv // LANES)
    y_cv = conv_fwd("cv_fwd", cfg.kc, HALO31, _glu, [(p, cfg.o_cv), (p, cfg.o_cv + cfg.cv)], lw["cv_w"], cv_post,
                    [(p, cfg.o_cv + 2 * cfg.cv)], [lw["cv_b"], lw["cv_ln_g"], lw["cv_ln_b"]], cfg.cv, cfg.cv)
    xn, y, yt = out_proj(x, y_dn, y_sg, y_cv, lw["wo"], gate)
    return xn, dict(x=x, p=p, ht=ht, qkv=qkv, wy=wy, ss=ss, y=y, yt=yt)


def _layer_bwd(cfg, dxn, mod, lw, sv):
    shift, scale, gate = mod
    p = sv["p"]
    d_dn, d_sg, d_cv, dyb, dgate = out_proj_bwd(dxn, sv["y"], gate, lw["wo"], cfg.dn, cfg.sg, cfg.cv)
    g_wo = matmul_acc("w_out_grad", sv["yt"], dyb)
    cv_post = [_cv_post] * (cfg.cv // LANES)
    dcv, g_cvw, (g_cvb, g_cvlg, g_cvlb) = conv_bwd(
        "cv_bwd", cfg.kc, HALO31, _glu, [(p, cfg.o_cv), (p, cfg.o_cv + cfg.cv)], lw["cv_w"], cv_post,
        [(p, cfg.o_cv + 2 * cfg.cv)], [lw["cv_b"], lw["cv_ln_g"], lw["cv_ln_b"]], d_cv, cfg.cv, cfg.cv, tm_pref=256)
    dsg, g_sglg, g_sglb, g_sgw, g_sgb = sg_bwd(p, cfg.o_sg, cfg.sg, lw["sg_ln_g"], lw["sg_ln_b"], lw["sg_w"],
                                               lw["sg_bias_b"], d_sg)
    *dwy, dz, g_dng = dn_seq_bwd(*sv["wy"][:6], p, cfg.o_z, lw["dn_norm_g"], sv["ss"], d_dn, cfg.h, cfg.hb_bwd)
    dqkv, dba, g_al, g_dt = dn_wy_bwd(sv["qkv"], p, cfg.o_ba, lw["alog_b"], lw["dtb_b"], sv["wy"][6], *dwy, cfg.h)
    qk_post = [_qk_post, _qk_post, _v_post]
    dqkv_pre, g_cq, _ = conv_bwd("dn_pre_bwd", cfg.k4, HALO4, lambda a: a, [(p, 0)], lw["conv_qkv"], qk_post, [], [],
                                 dqkv, 3 * cfg.dn, 3 * LANES)
    dp = jnp.concatenate([dqkv_pre, dz.astype(BF16), dsg, dcv, dba.astype(BF16)], axis=1)
    g_wp = matmul_acc("w_in_grad", sv["ht"], dp)
    dx, g_ng, dscale, dshift = in_proj_bwd(dp, lw["wp"], sv["x"], dxn, shift, scale, lw["norm_g"])
    grads = dict(norm_g=g_ng, conv_qkv=g_cq, a_log=g_al[:cfg.h, 0], dt_bias=g_dt[:cfg.h, 0], dn_norm_g=g_dng,
                 sg_ln_g=g_sglg, sg_ln_b=g_sglb, sg_w=g_sgw, sg_b=g_sgb[:, :, 0], cv_w=g_cvw, cv_b=g_cvb,
                 cv_ln_g=g_cvlg, cv_ln_b=g_cvlb, wp=g_wp, wo=g_wo)
    return dx, grads, (dshift, dscale, dgate)


def _local_step(cfg, xs, tgt, mods, lws, fg):
    nl = len(lws)
    saved = []
    for l in range(nl):
        xs, sv = _layer_fwd(cfg, xs, mods[l], lws[l])
        saved.append(sv)
    loss_b, dx, g_fg = loss_head(xs, tgt, fg)
    lg = [None] * nl
    dmods = [None] * nl
    for l in reversed(range(nl)):
        dx, lg[l], dmods[l] = _layer_bwd(cfg, dx, mods[l], lws[l], saved[l])
    return loss_b, dx, g_fg, lg, dmods


SMALL = ("norm_g", "conv_qkv", "a_log", "dt_bias", "dn_norm_g", "sg_ln_g", "sg_ln_b", "sg_w", "sg_b", "cv_w",
         "cv_b", "cv_ln_g", "cv_ln_b", "final_g", "b_ada")
PACK_N = 1024


def _pack(arrs):
    flat = jnp.concatenate([a.reshape(-1).astype(F32) for a in arrs])
    rows = -(-flat.shape[0] // PACK_N)
    rows = -(-rows // 8) * 8
    return jnp.pad(flat, (0, rows * PACK_N - flat.shape[0])).reshape(rows, PACK_N)


def _unpack(buf, shapes):
    flat = buf.reshape(-1)
    out, o = [], 0
    for s in shapes:
        n = 1
        for d_ in s:
            n *= d_
        out.append(flat[o:o + n].reshape(s))
        o += n
    return out


def kernel(x, c, norm_g, w_ada, b_ada, w_in, conv_qkv, a_log, dt_bias, dn_norm_g, sg_ln_g, sg_ln_b, sg_w, sg_b, cv_w, cv_b, cv_ln_g, cv_ln_b, w_out, final_g, loss_target, m_norm_g, m_w_ada, m_b_ada, m_w_in, m_conv_qkv, m_a_log, m_dt_bias, m_dn_norm_g, m_sg_ln_g, m_sg_ln_b, m_sg_w, m_sg_b, m_cv_w, m_cv_b, m_cv_ln_g, m_cv_ln_b, m_w_out, m_final_g, v_norm_g, v_w_ada, v_b_ada, v_w_in, v_conv_qkv, v_a_log, v_dt_bias, v_dn_norm_g, v_sg_ln_g, v_sg_ln_b, v_sg_w, v_sg_b, v_cv_w, v_cv_b, v_cv_ln_g, v_cv_ln_b, v_w_out, v_final_g):
    cfg = _Cfg(x, a_log, sg_w, cv_ln_g, cv_w, conv_qkv)
    nl, d, t, h = cfg.nl, cfg.d, cfg.t, cfg.h
    lh = nl // 2
    ax, ay, ac = _me()
    chip = 2 * ax + ay
    dev = 2 * chip + ac
    wts = dict(norm_g=norm_g, w_ada=w_ada, b_ada=b_ada, w_in=w_in, conv_qkv=conv_qkv, a_log=a_log, dt_bias=dt_bias,
               dn_norm_g=dn_norm_g, sg_ln_g=sg_ln_g, sg_ln_b=sg_ln_b, sg_w=sg_w, sg_b=sg_b, cv_w=cv_w, cv_b=cv_b,
               cv_ln_g=cv_ln_g, cv_ln_b=cv_ln_b, w_out=w_out, final_g=final_g)
    mom = dict(norm_g=m_norm_g, w_ada=m_w_ada, b_ada=m_b_ada, w_in=m_w_in, conv_qkv=m_conv_qkv, a_log=m_a_log,
               dt_bias=m_dt_bias, dn_norm_g=m_dn_norm_g, sg_ln_g=m_sg_ln_g, sg_ln_b=m_sg_ln_b, sg_w=m_sg_w,
               sg_b=m_sg_b, cv_w=m_cv_w, cv_b=m_cv_b, cv_ln_g=m_cv_ln_g, cv_ln_b=m_cv_ln_b, w_out=m_w_out,
               final_g=m_final_g)
    vel = dict(norm_g=v_norm_g, w_ada=v_w_ada, b_ada=v_b_ada, w_in=v_w_in, conv_qkv=v_conv_qkv, a_log=v_a_log,
               dt_bias=v_dt_bias, dn_norm_g=v_dn_norm_g, sg_ln_g=v_sg_ln_g, sg_ln_b=v_sg_ln_b, sg_w=v_sg_w,
               sg_b=v_sg_b, cv_w=v_cv_w, cv_b=v_cv_b, cv_ln_g=v_cv_ln_g, cv_ln_b=v_cv_ln_b, w_out=v_w_out,
               final_g=v_final_g)
    ada_cols = w_ada.shape[2]
    in_cols = w_in.shape[2]
    out_rows = w_out.shape[1]
    cq_cols = conv_qkv.shape[2]
    cvw_cols = cv_w.shape[2]

    c_all = all_gather8(jnp.pad(c, ((0, 7), (0, 0)))).reshape(NDEV, 8, d)[:, 0, :]
    b_loc = lax.dynamic_slice_in_dim(b_ada, chip * ada_cols, ada_cols, axis=1)[:, None, :]
    mod_part = ada_fwd(c_all, w_ada, b_loc)
    mod_all = all_gather8(mod_part.reshape(nl * NDEV, ada_cols)).reshape(NDEV, nl, NDEV, ada_cols)
    mod_me = lax.dynamic_index_in_dim(mod_all[0::2], dev, axis=2, keepdims=False)
    mod_me = jnp.moveaxis(mod_me, 0, 1).reshape(nl, 3, 1, d)

    win_b = w_in.astype(BF16).reshape(2, 2, lh * d // 2, in_cols)
    wout_b = w_out.astype(BF16).reshape(2, 2, lh * out_rows // 2, d)
    win_all, wout_all = gather_weights([win_b, wout_b])
    win_all = lax.dynamic_update_index_in_dim(win_all, win_b, chip, axis=0)
    wout_all = lax.dynamic_update_index_in_dim(wout_all, wout_b, chip, axis=0)
    win_all = win_all.reshape(NCHIP, nl, d, in_cols)
    wp_all = [_assemble_perm(cfg, [win_all[s, l] for s in range(NCHIP)]) for l in range(nl)]
    wo_all = jnp.moveaxis(wout_all.reshape(NCHIP, nl, out_rows, d), 0, 1).reshape(nl, NCHIP * out_rows, d)

    cq_all = all_gather8(conv_qkv.reshape(nl * cfg.k4, cq_cols)).reshape(NDEV, nl, cfg.k4, cq_cols)[0::2]
    cq_full = jnp.moveaxis(cq_all, 0, 2).reshape(nl, cfg.k4, NCHIP * cq_cols)
    cq_perm = _perm_cols_qkv(cfg, cq_full)
    kcp = -(-cfg.kc // 8) * 8
    cvw_all = all_gather8(jnp.pad(cv_w, ((0, 0), (0, kcp - cfg.kc), (0, 0))).reshape(nl * kcp, cvw_cols))
    cvw_all = cvw_all.reshape(NDEV, nl, kcp, cvw_cols)[0::2]
    cvw_full = jnp.moveaxis(cvw_all, 0, 2).reshape(nl, kcp, NCHIP * cvw_cols)[:, :cfg.kc]

    hp = -(-h // 8) * 8
    lws = []
    for l in range(nl):
        lws.append(dict(
            norm_g=norm_g[l][None], wp=wp_all[l], wo=wo_all[l], conv_qkv=cq_perm[l],
            alog_b=jnp.pad(jnp.broadcast_to(a_log[l][:, None], (h, LANES)), ((0, hp - h), (0, 0))),
            dtb_b=jnp.pad(jnp.broadcast_to(dt_bias[l][:, None], (h, LANES)), ((0, hp - h), (0, 0))),
            dn_norm_g=dn_norm_g[l][None], sg_ln_g=sg_ln_g[l][None], sg_ln_b=sg_ln_b[l][None], sg_w=sg_w[l],
            sg_bias_b=jnp.broadcast_to(sg_b[l][:, :, None], (cfg.g, LANES, LANES)),
            cv_w=cvw_full[l], cv_b=cv_b[l][None], cv_ln_g=cv_ln_g[l][None], cv_ln_b=cv_ln_b[l][None]))

    mods = [(mod_me[l, 0], mod_me[l, 1], mod_me[l, 2]) for l in range(nl)]
    loss_b, dx, g_fg, lg, dmods = _local_step(cfg, x[0], loss_target[0], mods, lws, final_g[None])
    grad_x = dx[None]

    dmod = jnp.stack([jnp.concatenate(dm, axis=1)[0] for dm in dmods])
    stack = lambda k: jnp.stack([g_[k] for g_ in lg])
    small_local = [stack(k).reshape(wts_shape) for k, wts_shape in
                   (("norm_g", (nl, d)), ("conv_qkv", (nl, cfg.k4, 3 * cfg.dn)), ("a_log", (nl, h)),
                    ("dt_bias", (nl, h)), ("dn_norm_g", (nl, LANES)), ("sg_ln_g", (nl, cfg.sg)),
                    ("sg_ln_b", (nl, cfg.sg)), ("sg_w", (nl, cfg.g, LANES, LANES)), ("sg_b", (nl, cfg.g, LANES)),
                    ("cv_w", (nl, cfg.kc, cfg.cv)), ("cv_b", (nl, cfg.cv)), ("cv_ln_g", (nl, cfg.cv)),
                    ("cv_ln_b", (nl, cfg.cv)))]
    small_local[1] = _unperm_cols_qkv(cfg, small_local[1])
    small_local += [g_fg[0], dmod, loss_b[0, 0:1]]
    shapes = [a.shape for a in small_local]
    packed = _pack(small_local)
    rows = packed.shape[0]
    gathered = all_gather8(packed).reshape(NDEV, rows, PACK_N)
    summed = _unpack(sum8(gathered), shapes)
    sgrads = dict(zip(SMALL, summed[:15]))
    loss = summed[15][0]
    sgrads["conv_qkv"] = lax.dynamic_slice_in_dim(sgrads["conv_qkv"], chip * cq_cols, cq_cols, axis=2)
    sgrads["cv_w"] = lax.dynamic_slice_in_dim(sgrads["cv_w"], chip * cvw_cols, cvw_cols, axis=2)

    off = sum(math.prod(s) for s in shapes[:14])
    dmod_all = gathered.reshape(NDEV, rows * PACK_N)[:, off:off + nl * 3 * d].reshape(NDEV, nl, 3 * d)
    dmod_loc = jnp.moveaxis(lax.dynamic_slice_in_dim(dmod_all, chip * ada_cols, ada_cols, axis=2), 0, 1)
    g_wada, d_wada, nm_wada, nv_wada = ada_bwd(c_all.T, dmod_loc, w_ada, m_w_ada, v_w_ada)

    shard = lambda g_, s: jnp.concatenate([g_[:, a:b] for a, b in _natural_pieces(cfg, s, in_cols)], axis=-1)
    g_in = jnp.stack([shard(lg[hh * lh + j]["wp"], s) for hh in range(2) for s in range(NCHIP) for j in range(lh)])
    g_in = g_in.reshape(2, NCHIP, lh * d, in_cols)
    g_wo = jnp.stack([g_["wo"] for g_ in lg]).reshape(2, lh, NCHIP, out_rows, d)
    g_out = jnp.moveaxis(g_wo, 2, 1).reshape(2, NCHIP, lh * out_rows, d)
    cflag = jnp.full((1, LANES), ac, F32)
    r1_in, r1_out = sibling_swap("swap_halves", [g_in, g_out], other_half=True)
    p_in, p_out = pair_sum(cflag, g_in, r1_in), pair_sum(cflag, g_out, r1_out)
    rx_in, ry_in, rx_out, ry_out = relay_out([p_in, p_out])
    to_x = lambda p_: lax.dynamic_index_in_dim(p_, 2 * (1 - ax) + ay, axis=0, keepdims=False)
    to_y = lambda p_: lax.dynamic_index_in_dim(p_, 2 * ax + 1 - ay, axis=0, keepdims=False)
    r2_in, r2_out = neighbour_exchange([*relay_sum(to_x(p_in), to_y(p_in), rx_in, ry_in),
                                        *relay_sum(to_x(p_out), to_y(p_out), rx_out, ry_out)])
    mine = lambda g_: lax.dynamic_index_in_dim(g_, chip, axis=0, keepdims=False)
    keep = lambda g_: lax.dynamic_index_in_dim(g_, ac, axis=0, keepdims=False)
    h_in = chip_sum(mine(keep(g_in)), mine(r1_in), r2_in)
    h_out = chip_sum(mine(keep(g_out)), mine(r1_out), r2_out)
    o_in, o_out = sibling_swap("join_halves", [h_in, h_out])

    v3 = lambda a, r_, c_: a.reshape(2, lh * r_, c_)
    grad_w_in, d_in_, nm_in, nv_in = adamw_halves("adamw_w_in", cflag, v3(w_in, d, in_cols), h_in, o_in,
                                                  v3(m_w_in, d, in_cols), v3(v_w_in, d, in_cols))
    grad_w_out, d_out_, nm_out, nv_out = adamw_halves("adamw_w_out", cflag, v3(w_out, out_rows, d), h_out, o_out,
                                                      v3(m_w_out, out_rows, d), v3(v_w_out, out_rows, d))
    grad_w_in = grad_w_in.reshape(w_in.shape)
    grad_w_out = grad_w_out.reshape(w_out.shape)
    sshapes = [wts[k].shape for k in SMALL]
    pk = lambda dct: _pack([dct[k] for k in SMALL])
    d_s, m_s, v_s = adamw("adamw_small", pk(wts), pk(sgrads), pk(mom), pk(vel))
    d_small = dict(zip(SMALL, _unpack(d_s, sshapes)))
    m_small = dict(zip(SMALL, _unpack(m_s, sshapes)))
    v_small = dict(zip(SMALL, _unpack(v_s, sshapes)))

    grads = dict(sgrads, w_ada=g_wada, w_in=grad_w_in, w_out=grad_w_out)
    deltas = dict(d_small, w_ada=d_wada, w_in=d_in_.reshape(w_in.shape), w_out=d_out_.reshape(w_out.shape))
    new_m = dict(m_small, w_ada=nm_wada, w_in=nm_in.reshape(w_in.shape), w_out=nm_out.reshape(w_out.shape))
    new_v = dict(v_small, w_ada=nv_wada, w_in=nv_in.reshape(w_in.shape), w_out=nv_out.reshape(w_out.shape))
    order = ("norm_g", "w_ada", "b_ada", "w_in", "conv_qkv", "a_log", "dt_bias", "dn_norm_g", "sg_ln_g", "sg_ln_b",
             "sg_w", "sg_b", "cv_w", "cv_b", "cv_ln_g", "cv_ln_b", "w_out", "final_g")
    return (loss, grad_x, *[grads[k] for k in order], *[deltas[k] for k in order], *[new_m[k] for k in order],
            *[new_v[k] for k in order])


def _perm_cols_qkv(cfg, w):
    lead = w.shape[:-1]
    return jnp.moveaxis(w.reshape(lead + (3, cfg.h, LANES)), -3, -2).reshape(lead + (3 * cfg.dn,))


def _unperm_cols_qkv(cfg, w):
    lead = w.shape[:-1]
    return jnp.moveaxis(w.reshape(lead + (cfg.h, 3, LANES)), -3, -2).reshape(lead + (3 * cfg.dn,))
```

```python
import functools
import math

import jax
import jax.numpy as jnp
from jax import lax
from jax.experimental import pallas as pl
from jax.experimental.pallas import tpu as pltpu

F32 = jnp.float32
BF16 = jnp.bfloat16
EPS = 1e-6
LN_EPS = 1e-5
LANES = 128
CHUNK = 64
SUBLANES = 8
HALO4 = 8
HALO31 = 32
NCHIP = 4
NDEV = 8
VMEM_LIMIT = 56 * 2 ** 20
ADAM_LR, ADAM_B1, ADAM_B2, ADAM_EPS, ADAM_WD, ADAM_STEP = 0.001, 0.9, 0.999, 1e-08, 0.01, 10
MESH = pl.DeviceIdType.MESH
ANY = pl.BlockSpec(memory_space=pl.ANY)


def _cp(sem=None, vmem=VMEM_LIMIT):
    return pltpu.CompilerParams(dimension_semantics=sem, vmem_limit_bytes=vmem)


def _tile(n, pref, mult):
    t = min(n, pref) // mult * mult
    while t > 0 and n % t:
        t -= mult
    return t if t > 0 else n


def _split(a):
    hi = a.astype(BF16)
    return hi, (a - hi.astype(F32)).astype(BF16)


def _raw_dot(a, b, ca, cb, hi):
    dn = (((ca,), (cb,)), ((), ()))
    if hi:
        ah, al = _split(a.astype(F32))
        bh, bl = _split(b.astype(F32))
        d3 = lambda x, y: lax.dot_general(x, y, dn, preferred_element_type=F32)
        return d3(ah, bh) + (d3(al, bh) + d3(ah, bl))
    return lax.dot_general(a.astype(BF16), b.astype(BF16), dn, preferred_element_type=F32)


@functools.partial(jax.custom_vjp, nondiff_argnums=(2, 3, 4))
def bdot(a, b, ca, cb, hi):
    return _raw_dot(a, b, ca, cb, hi)


def _bdot_fwd(a, b, ca, cb, hi):
    return _raw_dot(a, b, ca, cb, hi), (a, b)


def _bdot_bwd(ca, cb, hi, res, ct):
    a, b = res
    fa, fb = 1 - ca, 1 - cb
    da = _raw_dot(ct, b, 1, fb, hi) if ca == 1 else _raw_dot(b, ct, fb, 1, hi)
    db = _raw_dot(a, ct, fa, 0, hi) if cb == 0 else _raw_dot(ct, a, 0, fa, hi)
    return da.astype(a.dtype), db.astype(b.dtype)


bdot.defvjp(_bdot_fwd, _bdot_bwd)


def _sigmoid(x):
    return jax.nn.sigmoid(x)


def _silu(x):
    return x * _sigmoid(x)


def _gelu(x):
    return 0.5 * x * (1.0 + lax.erf(x * (2.0 ** -0.5)))


def _softplus(x):
    return jnp.maximum(x, 0.0) + jnp.log(1.0 + jnp.exp(-jnp.abs(x)))


def _modnorm(x, g, scale, shift):
    y = x * lax.rsqrt(jnp.mean(x * x, axis=-1, keepdims=True) + EPS)
    return (y * g) * (1.0 + scale) + shift


def _rmsnorm(x, g):
    return x * lax.rsqrt(jnp.mean(x * x, axis=-1, keepdims=True) + EPS) * g


def _layernorm(x, g, b):
    mu = jnp.mean(x, axis=-1, keepdims=True)
    xc = x - mu
    var = jnp.mean(xc * xc, axis=-1, keepdims=True)
    return xc * lax.rsqrt(var + LN_EPS) * g + b


def _l2norm(t):
    return t * lax.rsqrt(jnp.sum(t * t, axis=-1, keepdims=True) + EPS)


def _adamw_math(w, g, m, v):
    mn = ADAM_B1 * m + (1.0 - ADAM_B1) * g
    vn = ADAM_B2 * v + (1.0 - ADAM_B2) * (g * g)
    mh = mn / (1.0 - ADAM_B1 ** ADAM_STEP)
    vh = vn / (1.0 - ADAM_B2 ** ADAM_STEP)
    delta = -ADAM_LR * (mh / (jnp.sqrt(vh) + ADAM_EPS) + ADAM_WD * w)
    return delta, mn, vn


def _each(f, *lists):
    return [f(*xs) for xs in zip(*lists)]


def _wy(q, k, v, bcol, acol, alog, dtb, tinv=None):
    c = CHUNK
    r = lax.broadcasted_iota(jnp.int32, (c, c), 0)
    cc = lax.broadcasted_iota(jnp.int32, (c, c), 1)
    rr = lax.broadcasted_iota(jnp.int32, (c, 1), 0)
    tri_incl, tri_strict, eye = r >= cc, r > cc, r == cc
    beta = _each(_sigmoid, bcol)
    g = _each(lambda al, a_, dt: -jnp.exp(al) * _softplus(a_ + dt), alog, acol, dtb)
    gb = [jnp.broadcast_to(g_, (c, c)) for g_ in g]
    g_row = [jnp.sum(jnp.where(eye, b_, 0.0), axis=0, keepdims=True) for b_ in gb]
    gc_col = [jnp.sum(jnp.where(tri_incl, jnp.broadcast_to(gr, (c, c)), 0.0), axis=1, keepdims=True) for gr in g_row]
    gc_row = [jnp.sum(jnp.where(r <= cc, b_, 0.0), axis=0, keepdims=True) for b_ in gb]
    decay = _each(lambda gcc, gcr: jnp.where(tri_incl, jnp.exp(jnp.where(tri_incl, gcc - gcr, 0.0)), 0.0),
                  gc_col, gc_row)
    qs = [q_ * (q_.shape[-1] ** -0.5) for q_ in q]
    kb = _each(lambda k_, b_: k_ * b_, k, beta)
    a = _each(lambda kb_, k_, d_: jnp.where(tri_strict, bdot(kb_, k_, 1, 1, False) * d_, 0.0), kb, k, decay)
    dv = v[0].shape[-1]
    x = _each(lambda v_, b_, kb_, gcc: jnp.concatenate([v_ * b_, kb_ * jnp.exp(gcc)], axis=1), v, beta, kb, gc_col)
    if tinv is None:
        inv = [jnp.where(eye, 1.0, 0.0) - a_ for a_ in a]
        p = a
        for _ in range(5):
            p = _each(lambda p_: bdot(p_, p_, 1, 0, True), p)
            inv = _each(lambda t_, p_: t_ + bdot(t_, p_, 1, 0, True), inv, p)
        x = _each(lambda t_, x_: bdot(t_, x_, 1, 0, True), inv, x)
    else:
        x = _each(_solve_given_inverse, a, x, tinv)
    xv = [x_[:, :dv] for x_ in x]
    xk = [x_[:, dv:] for x_ in x]
    qk = _each(lambda q_, k_, d_: bdot(q_, k_, 1, 1, False) * d_, qs, k, decay)
    g_last = [jnp.sum(jnp.where(rr == c - 1, gcc, 0.0), axis=0, keepdims=True) for gcc in gc_col]
    qg = _each(lambda q_, gcc: q_ * jnp.exp(gcc), qs, gc_col)
    kd = _each(lambda k_, gl, gcc: k_ * jnp.exp(gl - gcc), k, g_last, gc_col)
    outs = (xv, xk, qg, kd, qk, [jnp.exp(gl) for gl in g_last])
    return outs + (inv,) if tinv is None else outs


@jax.custom_vjp
def _solve_given_inverse(a, rhs, tinv):
    return _raw_dot(tinv, rhs, 1, 0, True)


def _solve_given_inverse_fwd(a, rhs, tinv):
    x = _raw_dot(tinv, rhs, 1, 0, True)
    return x, (x, tinv)


def _solve_given_inverse_bwd(res, dx):
    x, tinv = res
    drhs = _raw_dot(tinv, dx, 0, 0, True)
    return -_raw_dot(drhs, x, 1, 1, True), drhs, jnp.zeros_like(tinv)


_solve_given_inverse.defvjp(_solve_given_inverse_fwd, _solve_given_inverse_bwd)


def _seq(u, w, qg, kd, qk, e, z, s, ng):
    v_new = _each(lambda u_, w_, s_: u_ - bdot(w_, s_, 1, 0, False), u, w, s)
    o1 = _each(lambda q_, s_: bdot(q_, s_, 1, 0, False), qg, s)
    o2 = _each(lambda qk_, vn: bdot(qk_, vn, 1, 0, False), qk, v_new)
    ds = _each(lambda kd_, vn: bdot(kd_, vn, 0, 0, False), kd, v_new)
    s_next = _each(lambda s_, e_, d_: s_ * e_ + d_, s, e, ds)
    y = _each(lambda a_, b_, z_: _rmsnorm(a_ + b_, ng) * _silu(z_), o1, o2, z)
    return y, s_next


def _sg_block(u, v, gt, lg, lb, w, bias):
    n = w.shape[0]
    pr = lax.broadcasted_iota(jnp.int32, (n, n), 0) // CHUNK
    pc = lax.broadcasted_iota(jnp.int32, (n, n), 1) // CHUNK
    wm = jnp.where(pr >= pc, w, 0.0)
    vl = _layernorm(_gelu(v), lg, lb)
    mixed = bdot(wm, vl, 1, 0, False) + bias
    return _gelu(u) * mixed * _silu(gt)


def _glu(a, b):
    return a * _sigmoid(b)


def _cv_post(conv, gate, cb, lg, lb):
    return _silu(_layernorm(conv + cb, lg, lb)) * _silu(gate)


def _qk_post(conv):
    return _l2norm(_silu(conv))


def _v_post(conv):
    return _silu(conv)


def in_proj(x, shift, scale, ng, wp):
    t, d = x.shape
    npc = wp.shape[1]
    tm, tn = _tile(t, 512, LANES), _tile(npc, 2432, LANES)

    def body(x_ref, sh_ref, sc_ref, g_ref, w_ref, p_ref, ht_ref, h_scr):
        @pl.when(pl.program_id(1) == 0)
        def _():
            h = _modnorm(x_ref[...], g_ref[...], sc_ref[...], sh_ref[...])
            h_scr[...] = h.astype(BF16)
            ht_ref[...] = h.T.astype(BF16)
        p_ref[...] = jnp.dot(h_scr[...], w_ref[...], preferred_element_type=F32)

    vec = pl.BlockSpec((1, d), lambda i, j: (0, 0))
    return pl.pallas_call(
        body, name="in_proj", grid=(t // tm, npc // tn),
        in_specs=[pl.BlockSpec((tm, d), lambda i, j: (i, 0)), vec, vec, vec,
                  pl.BlockSpec((d, tn), lambda i, j: (0, j))],
        out_specs=[pl.BlockSpec((tm, tn), lambda i, j: (i, j)), pl.BlockSpec((d, tm), lambda i, j: (0, i))],
        out_shape=[jax.ShapeDtypeStruct((t, npc), F32), jax.ShapeDtypeStruct((d, t), BF16)],
        scratch_shapes=[pltpu.VMEM((tm, d), BF16)],
        compiler_params=_cp(("parallel", "arbitrary")),
    )(x, shift, scale, ng, wp)


def _roll_bank(x, bank_ref, offsets):
    rows = x.shape[0]
    residues = sorted({o % SUBLANES for o in offsets})
    for slot, b in enumerate(residues):
        bank_ref[slot] = x if b == 0 else pltpu.roll(x, rows - b, 0)
    return {o: (residues.index(o % SUBLANES), o - o % SUBLANES) for o in offsets}


def _n_residues(offsets):
    return len({o % SUBLANES for o in offsets})


def conv_fwd(name, k, halo, pre_fn, pre, w, post_fns, extras, params, c_total, tc, tm_pref=512):
    t = pre[0][0].shape[0]
    tm = _tile(t, tm_pref, halo)
    npre, nex, npar = len(pre), len(extras), len(params)
    ngr = tc // LANES
    taps = [halo - (k - 1) + j for j in range(k)]

    def body(*refs):
        prev = refs[:npre]
        cur = refs[npre:2 * npre]
        w_ref = refs[2 * npre]
        ex = refs[2 * npre + 1:2 * npre + 1 + nex]
        par = refs[2 * npre + 1 + nex:2 * npre + 1 + nex + npar]
        out_ref, buf, bank = refs[-3], refs[-2], refs[-1]
        i = pl.program_id(1)
        pv = pre_fn(*[r[...] for r in prev])
        buf[0:halo, :] = jnp.where(i > 0, pv, 0.0)
        buf[halo:, :] = pre_fn(*[r[...] for r in cur])
        where = _roll_bank(buf[...], bank, taps)
        acc = None
        for j, o in enumerate(taps):
            slot, st = where[o]
            term = w_ref[j:j + 1, :] * bank[slot, st:st + tm, :]
            acc = term if acc is None else acc + term
        for gi in range(ngr):
            sl = slice(gi * LANES, (gi + 1) * LANES)
            out_ref[:, sl] = post_fns[gi](acc[:, sl], *[e[:, sl] for e in ex], *[p_[:, sl] for p_ in par])

    hb = tm // halo
    in_specs = ([pl.BlockSpec((halo, tc), functools.partial(lambda j, i, o: (jnp.maximum(i * hb - 1, 0), o + j), o=col // tc))
                 for _, col in pre]
                + [pl.BlockSpec((tm, tc), functools.partial(lambda j, i, o: (i, o + j), o=col // tc)) for _, col in pre]
                + [pl.BlockSpec((k, tc), lambda j, i: (0, j))]
                + [pl.BlockSpec((tm, tc), functools.partial(lambda j, i, o: (i, o + j), o=col // tc)) for _, col in extras]
                + [pl.BlockSpec((1, tc), lambda j, i: (0, j)) for _ in params])
    args = [a for a, _ in pre] * 2 + [w] + [a for a, _ in extras] + list(params)
    return pl.pallas_call(
        body, name=name, grid=(c_total // tc, t // tm), in_specs=in_specs,
        out_specs=pl.BlockSpec((tm, tc), lambda j, i: (i, j)),
        out_shape=jax.ShapeDtypeStruct((t, c_total), F32),
        scratch_shapes=[pltpu.VMEM((halo + tm, tc), F32), pltpu.VMEM((_n_residues(taps), halo + tm, tc), F32)],
        compiler_params=_cp(("parallel", "arbitrary")),
    )(*args)


def conv_bwd(name, k, halo, pre_fn, pre, w, post_fns, extras, params, dout, c_total, tc, tm_pref=512):
    t = pre[0][0].shape[0]
    tm = _tile(t, tm_pref, halo)
    npre, nex, npar = len(pre), len(extras), len(params)
    ngr = tc // LANES
    nout = npre + nex
    assert nout == 1 or c_total == tc
    nblk = t // tm
    ext = tm + halo
    taps = [halo - (k - 1) + j for j in range(k)]
    back = [k - 1 - j for j in range(k)]

    def body(*refs):
        it = iter(refs)
        prev = [next(it) for _ in range(npre)]
        cur = [next(it) for _ in range(npre)]
        nxt = [next(it) for _ in range(npre)]
        w_ref = next(it)
        ex_c = [next(it) for _ in range(nex)]
        ex_n = [next(it) for _ in range(nex)]
        par = [next(it) for _ in range(npar)]
        do_c, do_n = next(it), next(it)
        din_ref, dw_ref = next(it), next(it)
        dpar = [next(it) for _ in range(npar)]
        buf, dbuf, bank, dbank = next(it), next(it), next(it), next(it)
        i = pl.program_id(1)

        @pl.when(i == 0)
        def _():
            dw_ref[...] = jnp.zeros_like(dw_ref)
            for r in dpar:
                r[...] = jnp.zeros_like(r)

        buf[0:halo, :] = jnp.where(i > 0, pre_fn(*[r[...] for r in prev]), 0.0)
        cur_vals = [r[...] for r in cur]
        buf[halo:halo + tm, :] = pre_fn(*cur_vals)
        buf[halo + tm:, :] = pre_fn(*[r[...] for r in nxt])
        where = _roll_bank(buf[...], bank, taps)
        conv = None
        for j, o in enumerate(taps):
            slot, st = where[o]
            term = w_ref[j:j + 1, :] * bank[slot, st:st + ext, :]
            conv = term if conv is None else conv + term
        don = jnp.where(i < nblk - 1, do_n[...], 0.0)
        for gi in range(ngr):
            sl = slice(gi * LANES, (gi + 1) * LANES)
            pv = [p_[:, sl] for p_ in par]
            _, vj = jax.vjp(post_fns[gi], conv[:tm, sl], *[e[:, sl] for e in ex_c], *pv)
            gr = vj(do_c[:, sl])
            dbuf[0:tm, sl] = gr[0]
            for e in range(nex):
                din_ref[:, (npre + e) * tc + gi * LANES:(npre + e) * tc + (gi + 1) * LANES] = gr[1 + e].astype(din_ref.dtype)
            for q_ in range(npar):
                dpar[q_][:, sl] += gr[1 + nex + q_]
            _, vjn = jax.vjp(post_fns[gi], conv[tm:, sl], *[e[:, sl] for e in ex_n], *pv)
            dbuf[tm:, sl] = vjn(don[:, sl])[0]
        dcur = dbuf[0:tm, :]
        dwhere = _roll_bank(dbuf[...], dbank, back)
        dpre = None
        for j in range(k):
            slot, st = dwhere[back[j]]
            term = w_ref[j:j + 1, :] * dbank[slot, st:st + tm, :]
            dpre = term if dpre is None else dpre + term
            slot, st = where[taps[j]]
            dw_ref[j:j + 1, :] += jnp.sum(dcur * bank[slot, st:st + tm, :], axis=0, keepdims=True)
        _, vjp_pre = jax.vjp(pre_fn, *cur_vals)
        for e, gval in enumerate(vjp_pre(dpre)):
            din_ref[:, e * tc:(e + 1) * tc] = gval.astype(din_ref.dtype)

    hb = tm // halo
    last_h = t // halo - 1

    def spec(kind, col):
        o = col // tc
        if kind == "prev":
            return pl.BlockSpec((halo, tc), lambda j, i: (jnp.maximum(i * hb - 1, 0), o + j))
        if kind == "next":
            return pl.BlockSpec((halo, tc), lambda j, i: (jnp.minimum((i + 1) * hb, last_h), o + j))
        return pl.BlockSpec((tm, tc), lambda j, i: (i, o + j))

    in_specs = ([spec("prev", col) for _, col in pre] + [spec("cur", col) for _, col in pre]
                + [spec("next", col) for _, col in pre] + [pl.BlockSpec((k, tc), lambda j, i: (0, j))]
                + [spec("cur", col) for _, col in extras] + [spec("next", col) for _, col in extras]
                + [pl.BlockSpec((1, tc), lambda j, i: (0, j)) for _ in params]
                + [spec("cur", 0), spec("next", 0)])
    args = [a for a, _ in pre] * 3 + [w] + [a for a, _ in extras] * 2 + list(params) + [dout, dout]
    out = pl.pallas_call(
        body, name=name, grid=(c_total // tc, nblk), in_specs=in_specs,
        out_specs=[pl.BlockSpec((tm, nout * tc), lambda j, i: (i, j)), pl.BlockSpec((k, tc), lambda j, i: (0, j))]
        + [pl.BlockSpec((1, tc), lambda j, i: (0, j)) for _ in params],
        out_shape=[jax.ShapeDtypeStruct((t, nout * c_total), BF16), jax.ShapeDtypeStruct((k, c_total), F32)]
        + [jax.ShapeDtypeStruct((1, c_total), F32) for _ in params],
        scratch_shapes=[pltpu.VMEM((2 * halo + tm, tc), F32), pltpu.VMEM((ext, tc), F32),
                        pltpu.VMEM((_n_residues(taps), 2 * halo + tm, tc), F32),
                        pltpu.VMEM((_n_residues(back), ext, tc), F32)],
        compiler_params=_cp(("parallel", "arbitrary")),
    )(*args)
    return out[0], out[1], out[2:]


def _head_pick(ref_val, row):
    rr = lax.broadcasted_iota(jnp.int32, ref_val.shape, 0)
    v = jnp.sum(jnp.where(rr == row, ref_val, 0.0), axis=0, keepdims=True)
    ll = lax.broadcasted_iota(jnp.int32, v.shape, 1)
    return jnp.sum(jnp.where(ll == 0, v, 0.0), axis=1, keepdims=True)


def _lane_col(blk, lane_idx):
    ll = lax.broadcasted_iota(jnp.int32, blk.shape, 1)
    return jnp.sum(jnp.where(ll == lane_idx, blk, 0.0), axis=1, keepdims=True)


WY_HEADS = 2
WY_UNROLL = 4
WY_UNROLL_BWD = 4


def dn_wy_fwd(qkv, p, ba_col, alog_b, dtb_b, nheads):
    t = qkv.shape[0]
    tm = _tile(t, 512, CHUNK * WY_UNROLL)
    nc = tm // CHUNK
    hb = WY_HEADS
    hp = alog_b.shape[0]
    w_ = hb * LANES

    def body(qkv_ref, ba_ref, al_ref, dt_ref, u_ref, w_ref, qg_ref, kd_ref, qk_ref, e_ref, ti_ref):
        hblk = pl.program_id(1)
        alv, dtv = al_ref[...], dt_ref[...]

        def trip(cj, carry):
            units = [(cj * WY_UNROLL + cu, hl) for cu in range(WY_UNROLL) for hl in range(hb)]
            args = [[] for _ in range(7)]
            for ci, hl in units:
                rows = pl.ds(pl.multiple_of(ci * CHUNK, CHUNK), CHUNK)
                ba = ba_ref[rows, :]
                h = hblk * hb + hl
                for lst, val in zip(args, (qkv_ref[rows, hl * 384:hl * 384 + 128],
                                           qkv_ref[rows, hl * 384 + 128:hl * 384 + 256],
                                           qkv_ref[rows, hl * 384 + 256:hl * 384 + 384],
                                           _lane_col(ba, h), _lane_col(ba, nheads + h),
                                           _head_pick(alv, h), _head_pick(dtv, h))):
                    lst.append(val)
            outs = _wy(*args)
            for n, (ci, hl) in enumerate(units):
                rows = pl.ds(pl.multiple_of(ci * CHUNK, CHUNK), CHUNK)
                u, w, qg, kd, qk, e, ti = [o[n] for o in outs]
                sl = slice(hl * LANES, (hl + 1) * LANES)
                u_ref[rows, sl] = u
                w_ref[rows, sl] = w.astype(BF16)
                qg_ref[rows, sl] = qg.astype(BF16)
                kd_ref[rows, sl] = kd.astype(BF16)
                qk_ref[rows, hl * LANES:hl * LANES + CHUNK] = qk.astype(BF16)
                qk_ref[rows, hl * LANES + CHUNK:(hl + 1) * LANES] = jnp.zeros((CHUNK, LANES - CHUNK), BF16)
                e_ref[ci, :, sl] = jnp.broadcast_to(e, (1, LANES))
                ti_ref[rows, hl * LANES:hl * LANES + CHUNK] = ti
                ti_ref[rows, hl * LANES + CHUNK:(hl + 1) * LANES] = jnp.zeros((CHUNK, LANES - CHUNK), F32)
            return carry

        lax.fori_loop(0, nc // WY_UNROLL, trip, 0)

    bc = ba_col // LANES
    blk = pl.BlockSpec((tm, w_), lambda i, h: (i, h))
    tab = pl.BlockSpec((hp, LANES), lambda i, h: (0, 0))
    wide = lambda dt: jax.ShapeDtypeStruct((t, nheads * LANES), dt)
    return pl.pallas_call(
        body, name="dn_wy_fwd", grid=(t // tm, nheads // hb),
        in_specs=[pl.BlockSpec((tm, hb * 384), lambda i, h: (i, h)), pl.BlockSpec((tm, LANES), lambda i, h: (i, bc)),
                  tab, tab],
        out_specs=[blk] * 5 + [pl.BlockSpec((nc, 1, w_), lambda i, h: (i, 0, h)), blk],
        out_shape=[wide(F32), wide(BF16), wide(BF16), wide(BF16), wide(BF16),
                   jax.ShapeDtypeStruct((t // CHUNK, 1, nheads * LANES), F32), wide(F32)],
        compiler_params=_cp(("parallel", "parallel")),
    )(qkv, p, alog_b, dtb_b)


def dn_seq_fwd(u, w, qg, kd, qk, e, p, z_col, ng, nheads, hb):
    t = u.shape[0]
    tm = _tile(t, 512, CHUNK)
    nc = tm // CHUNK
    w_ = hb * LANES

    def body(u_ref, w_ref, qg_ref, kd_ref, qk_ref, e_ref, z_ref, ng_ref, y_ref, ss_ref, s_scr):
        i, hblk = pl.program_id(0), pl.program_id(1)
        for hl in range(hb):
            @pl.when(i == 0)
            def _():
                s_scr[hblk * hb + hl] = jnp.zeros((LANES, LANES), F32)
        ngv = ng_ref[...]

        def chunk(ci, carry):
            rows = pl.ds(pl.multiple_of(ci * CHUNK, CHUNK), CHUNK)
            ev = e_ref[ci]
            sls = [slice(hl * LANES, (hl + 1) * LANES) for hl in range(hb)]
            s = [s_scr[hblk * hb + hl] for hl in range(hb)]
            for hl in range(hb):
                ss_ref[ci, sls[hl], :] = s[hl]
            y, sn = _seq([u_ref[rows, sl] for sl in sls], [w_ref[rows, sl].astype(F32) for sl in sls],
                         [qg_ref[rows, sl].astype(F32) for sl in sls], [kd_ref[rows, sl].astype(F32) for sl in sls],
                         [qk_ref[rows, sl][:, :CHUNK].astype(F32) for sl in sls], [ev[:, sl] for sl in sls],
                         [z_ref[rows, sl] for sl in sls], s, ngv)
            for hl in range(hb):
                y_ref[rows, sls[hl]] = y[hl]
                s_scr[hblk * hb + hl] = sn[hl]
            return carry

        lax.fori_loop(0, nc, chunk, 0)

    zc = z_col // w_
    blk = pl.BlockSpec((tm, w_), lambda i, h: (i, h))
    return pl.pallas_call(
        body, name="dn_seq_fwd", grid=(t // tm, nheads // hb),
        in_specs=[blk] * 5 + [pl.BlockSpec((nc, 1, w_), lambda i, h: (i, 0, h)),
                              pl.BlockSpec((tm, w_), lambda i, h: (i, zc + h)),
                              pl.BlockSpec((1, LANES), lambda i, h: (0, 0))],
        out_specs=[blk, pl.BlockSpec((nc, w_, LANES), lambda i, h: (i, h, 0))],
        out_shape=[jax.ShapeDtypeStruct((t, nheads * LANES), F32),
                   jax.ShapeDtypeStruct((t // CHUNK, nheads * LANES, LANES), F32)],
        scratch_shapes=[pltpu.VMEM((nheads, LANES, LANES), F32)],
        compiler_params=_cp(("arbitrary", "arbitrary")),
    )(u, w, qg, kd, qk, e, p, ng)


def dn_seq_bwd(u, w, qg, kd, qk, e, p, z_col, ng, ss, dy, nheads, hb):
    t = u.shape[0]
    tm = _tile(t, 2048 // hb, CHUNK)
    nc = tm // CHUNK
    nblk = t // tm
    w_ = hb * LANES

    def body(u_ref, w_ref, qg_ref, kd_ref, qk_ref, e_ref, z_ref, ng_ref, ss_ref, dy_ref,
             du_ref, dw_ref, dqg_ref, dkd_ref, dqk_ref, de_ref, dz_ref, dng_ref, ds_scr):
        i, hblk = pl.program_id(0), pl.program_id(1)

        @pl.when((i == 0) & (hblk == 0))
        def _():
            dng_ref[...] = jnp.zeros_like(dng_ref)

        for hl in range(hb):
            @pl.when(i == 0)
            def _():
                ds_scr[hblk * hb + hl] = jnp.zeros((LANES, LANES), F32)
        ngv = ng_ref[...]

        def chunk(cj, carry):
            ci = nc - 1 - cj
            rows = pl.ds(pl.multiple_of(ci * CHUNK, CHUNK), CHUNK)
            ev = e_ref[ci]
            sls = [slice(hl * LANES, (hl + 1) * LANES) for hl in range(hb)]
            _, vj = jax.vjp(_seq, [u_ref[rows, sl] for sl in sls], [w_ref[rows, sl].astype(F32) for sl in sls],
                            [qg_ref[rows, sl].astype(F32) for sl in sls], [kd_ref[rows, sl].astype(F32) for sl in sls],
                            [qk_ref[rows, sl][:, :CHUNK].astype(F32) for sl in sls], [ev[:, sl] for sl in sls],
                            [z_ref[rows, sl] for sl in sls], [ss_ref[ci, sl, :] for sl in sls], ngv)
            du, dw, dqg, dkd, dqk, de, dz, dsp, dng = vj(([dy_ref[rows, sl] for sl in sls],
                                                          [ds_scr[hblk * hb + hl] for hl in range(hb)]))
            for hl, sl in enumerate(sls):
                du_ref[rows, sl] = du[hl]
                dw_ref[rows, sl] = dw[hl]
                dqg_ref[rows, sl] = dqg[hl]
                dkd_ref[rows, sl] = dkd[hl]
                dqk_ref[rows, hl * LANES:hl * LANES + CHUNK] = dqk[hl]
                dqk_ref[rows, hl * LANES + CHUNK:(hl + 1) * LANES] = jnp.zeros((CHUNK, LANES - CHUNK), F32)
                de_ref[ci, :, sl] = de[hl]
                dz_ref[rows, sl] = dz[hl]
                ds_scr[hblk * hb + hl] = dsp[hl]
            dng_ref[...] += dng
            return carry

        lax.fori_loop(0, nc, chunk, 0)

    zc = z_col // w_
    rv = lambda i: nblk - 1 - i
    blk = pl.BlockSpec((tm, w_), lambda i, h: (rv(i), h))
    eblk = pl.BlockSpec((nc, 1, w_), lambda i, h: (rv(i), 0, h))
    one = pl.BlockSpec((1, LANES), lambda i, h: (0, 0))
    wide = jax.ShapeDtypeStruct((t, nheads * LANES), F32)
    return pl.pallas_call(
        body, name="dn_seq_bwd", grid=(nblk, nheads // hb),
        in_specs=[blk] * 5 + [eblk, pl.BlockSpec((tm, w_), lambda i, h: (rv(i), zc + h)), one,
                              pl.BlockSpec((nc, w_, LANES), lambda i, h: (rv(i), h, 0)), blk],
        out_specs=[blk] * 5 + [eblk, blk, one],
        out_shape=[wide] * 5 + [jax.ShapeDtypeStruct((t // CHUNK, 1, nheads * LANES), F32), wide,
                                jax.ShapeDtypeStruct((1, LANES), F32)],
        scratch_shapes=[pltpu.VMEM((nheads, LANES, LANES), F32)],
        compiler_params=_cp(("arbitrary", "arbitrary")),
    )(u, w, qg, kd, qk, e, p, ng, ss, dy)


def dn_wy_bwd(qkv, p, ba_col, alog_b, dtb_b, ti, du, dw, dqg, dkd, dqk, de, nheads):
    t = qkv.shape[0]
    tm = _tile(t, 512, CHUNK * WY_UNROLL_BWD)
    nc = tm // CHUNK
    hb = WY_HEADS
    hp = alog_b.shape[0]
    w_ = hb * LANES

    def body(qkv_ref, ba_ref, al_ref, dt_ref, du_ref, dw_ref, dqg_ref, dkd_ref, dqk_ref, de_ref, ti_ref,
             dqkv_ref, dba_ref, dal_ref, ddt_ref):
        i, hblk = pl.program_id(0), pl.program_id(1)

        @pl.when((i == 0) & (hblk == 0))
        def _():
            dal_ref[...] = jnp.zeros_like(dal_ref)
            ddt_ref[...] = jnp.zeros_like(ddt_ref)

        @pl.when(hblk == 0)
        def _():
            dba_ref[...] = jnp.zeros_like(dba_ref)

        alv, dtv = al_ref[...], dt_ref[...]
        lane = lax.broadcasted_iota(jnp.int32, (CHUNK, LANES), 1)
        rowp = lax.broadcasted_iota(jnp.int32, (hp, LANES), 0)

        def trip(cj, carry):
            units = [(cj * WY_UNROLL_BWD + cu, hl) for cu in range(WY_UNROLL_BWD) for hl in range(hb)]
            args = [[] for _ in range(8)]
            cts = [[] for _ in range(6)]
            for ci, hl in units:
                rows = pl.ds(pl.multiple_of(ci * CHUNK, CHUNK), CHUNK)
                ba = ba_ref[rows, :]
                h = hblk * hb + hl
                sl = slice(hl * LANES, (hl + 1) * LANES)
                for lst, val in zip(args, (qkv_ref[rows, hl * 384:hl * 384 + 128],
                                           qkv_ref[rows, hl * 384 + 128:hl * 384 + 256],
                                           qkv_ref[rows, hl * 384 + 256:hl * 384 + 384],
                                           _lane_col(ba, h), _lane_col(ba, nheads + h),
                                           _head_pick(alv, h), _head_pick(dtv, h), ti_ref[rows, sl][:, :CHUNK])):
                    lst.append(val)
                de11 = jnp.sum(de_ref[ci][:, sl], axis=1, keepdims=True)
                for lst, val in zip(cts, (du_ref[rows, sl], dw_ref[rows, sl], dqg_ref[rows, sl], dkd_ref[rows, sl],
                                          dqk_ref[rows, sl][:, :CHUNK], de11)):
                    lst.append(val)
            _, vj = jax.vjp(_wy, *args)
            grads = vj(tuple(cts))
            for n, (ci, hl) in enumerate(units):
                rows = pl.ds(pl.multiple_of(ci * CHUNK, CHUNK), CHUNK)
                h = hblk * hb + hl
                dq, dk, dv, dbc, dac, dal, ddt = [g_[n] for g_ in grads[:7]]
                dqkv_ref[rows, hl * 384:hl * 384 + 128] = dq
                dqkv_ref[rows, hl * 384 + 128:hl * 384 + 256] = dk
                dqkv_ref[rows, hl * 384 + 256:hl * 384 + 384] = dv
                dba_ref[rows, :] += jnp.where(lane == h, dbc, 0.0) + jnp.where(lane == nheads + h, dac, 0.0)
                dal_ref[...] += jnp.where(rowp == h, dal, 0.0)
                ddt_ref[...] += jnp.where(rowp == h, ddt, 0.0)
            return carry

        lax.fori_loop(0, nc // WY_UNROLL_BWD, trip, 0)

    bc = ba_col // LANES
    blk = pl.BlockSpec((tm, w_), lambda i, h: (i, h))
    tab = pl.BlockSpec((hp, LANES), lambda i, h: (0, 0))
    return pl.pallas_call(
        body, name="dn_wy_bwd", grid=(t // tm, nheads // hb),
        in_specs=[pl.BlockSpec((tm, hb * 384), lambda i, h: (i, h)), pl.BlockSpec((tm, LANES), lambda i, h: (i, bc)),
                  tab, tab] + [blk] * 5 + [pl.BlockSpec((nc, 1, w_), lambda i, h: (i, 0, h)), blk],
        out_specs=[pl.BlockSpec((tm, hb * 384), lambda i, h: (i, h)), pl.BlockSpec((tm, LANES), lambda i, h: (i, 0)),
                   tab, tab],
        out_shape=[jax.ShapeDtypeStruct((t, nheads * 384), F32), jax.ShapeDtypeStruct((t, LANES), F32),
                   jax.ShapeDtypeStruct((hp, LANES), F32), jax.ShapeDtypeStruct((hp, LANES), F32)],
        compiler_params=_cp(("arbitrary", "arbitrary")),
    )(qkv, p, alog_b, dtb_b, du, dw, dqg, dkd, dqk, de, ti)


def sg_fwd(p, col, sg, lg, lb, w, bias_b):
    t = p.shape[0]
    ng_ = sg // LANES
    tm = _tile(t, 256, LANES)
    cb = col // sg

    def body(u_ref, v_ref, g_ref, lg_ref, lb_ref, w_ref, b_ref, y_ref):
        for n in range(tm // LANES):
            rs = slice(n * LANES, (n + 1) * LANES)
            for gi in range(ng_):
                sl = slice(gi * LANES, (gi + 1) * LANES)
                y_ref[rs, sl] = _sg_block(u_ref[rs, sl], v_ref[rs, sl], g_ref[rs, sl], lg_ref[:, sl], lb_ref[:, sl],
                                          w_ref[gi], b_ref[gi])

    vec = pl.BlockSpec((1, sg), lambda i: (0, 0))
    full = pl.BlockSpec((ng_, LANES, LANES), lambda i: (0, 0, 0))
    return pl.pallas_call(
        body, name="sg_fwd", grid=(t // tm,),
        in_specs=[pl.BlockSpec((tm, sg), lambda i: (i, cb)), pl.BlockSpec((tm, sg), lambda i: (i, cb + 1)),
                  pl.BlockSpec((tm, sg), lambda i: (i, cb + 2)), vec, vec, full, full],
        out_specs=pl.BlockSpec((tm, sg), lambda i: (i, 0)),
        out_shape=jax.ShapeDtypeStruct((t, sg), F32),
        compiler_params=_cp(("parallel",)),
    )(p, p, p, lg, lb, w, bias_b)


def sg_bwd(p, col, sg, lg, lb, w, bias_b, dy):
    t = p.shape[0]
    ng_ = sg // LANES
    tm = _tile(t, 256, LANES)
    cb = col // sg

    def body(u_ref, v_ref, g_ref, lg_ref, lb_ref, w_ref, b_ref, dy_ref, d_ref, dlg_ref, dlb_ref, dw_ref, db_ref):
        @pl.when(pl.program_id(0) == 0)
        def _():
            for r in (dlg_ref, dlb_ref, dw_ref, db_ref):
                r[...] = jnp.zeros_like(r)

        for n in range(tm // LANES):
            rs = slice(n * LANES, (n + 1) * LANES)
            for gi in range(ng_):
                sl = slice(gi * LANES, (gi + 1) * LANES)
                _, vj = jax.vjp(_sg_block, u_ref[rs, sl], v_ref[rs, sl], g_ref[rs, sl], lg_ref[:, sl], lb_ref[:, sl],
                                w_ref[gi], b_ref[gi])
                du, dv, dg, dlg, dlb, dw, db = vj(dy_ref[rs, sl])
                d_ref[rs, gi * LANES:(gi + 1) * LANES] = du.astype(BF16)
                d_ref[rs, sg + gi * LANES:sg + (gi + 1) * LANES] = dv.astype(BF16)
                d_ref[rs, 2 * sg + gi * LANES:2 * sg + (gi + 1) * LANES] = dg.astype(BF16)
                dlg_ref[:, sl] += dlg
                dlb_ref[:, sl] += dlb
                dw_ref[gi] += dw
                db_ref[gi] += jnp.broadcast_to(jnp.sum(db, axis=1, keepdims=True), (LANES, LANES))

    vec = pl.BlockSpec((1, sg), lambda i: (0, 0))
    full = pl.BlockSpec((ng_, LANES, LANES), lambda i: (0, 0, 0))
    return pl.pallas_call(
        body, name="sg_bwd", grid=(t // tm,),
        in_specs=[pl.BlockSpec((tm, sg), lambda i: (i, cb)), pl.BlockSpec((tm, sg), lambda i: (i, cb + 1)),
                  pl.BlockSpec((tm, sg), lambda i: (i, cb + 2)), vec, vec, full, full,
                  pl.BlockSpec((tm, sg), lambda i: (i, 0))],
        out_specs=[pl.BlockSpec((tm, 3 * sg), lambda i: (i, 0)), vec, vec, full, full],
        out_shape=[jax.ShapeDtypeStruct((t, 3 * sg), BF16), jax.ShapeDtypeStruct((1, sg), F32),
                   jax.ShapeDtypeStruct((1, sg), F32), jax.ShapeDtypeStruct((ng_, LANES, LANES), F32),
                   jax.ShapeDtypeStruct((ng_, LANES, LANES), F32)],
        compiler_params=_cp(("arbitrary",)),
    )(p, p, p, lg, lb, w, bias_b, dy)


def out_proj(x, y_dn, y_sg, y_cv, wo, gate):
    t, d = x.shape
    dn, sg, cv = y_dn.shape[1], y_sg.shape[1], y_cv.shape[1]
    dmix = dn + sg + cv
    tm = _tile(t, 256, LANES)

    def body(x_ref, a_ref, b_ref, c_ref, w_ref, g_ref, xn_ref, y_ref, yt_ref):
        a, b, c = a_ref[...], b_ref[...], c_ref[...]
        y = (jnp.dot(a.astype(BF16), w_ref[0:dn, :], preferred_element_type=F32)
             + jnp.dot(b.astype(BF16), w_ref[dn:dn + sg, :], preferred_element_type=F32)
             + jnp.dot(c.astype(BF16), w_ref[dn + sg:, :], preferred_element_type=F32))
        y_ref[...] = y
        xn_ref[...] = x_ref[...] + g_ref[...] * y
        yt_ref[0:dn, :] = a.T.astype(BF16)
        yt_ref[dn:dn + sg, :] = b.T.astype(BF16)
        yt_ref[dn + sg:, :] = c.T.astype(BF16)

    row = lambda w_: pl.BlockSpec((tm, w_), lambda i: (i, 0))
    return pl.pallas_call(
        body, name="out_proj", grid=(t // tm,),
        in_specs=[row(d), row(dn), row(sg), row(cv), pl.BlockSpec((dmix, d), lambda i: (0, 0)),
                  pl.BlockSpec((1, d), lambda i: (0, 0))],
        out_specs=[row(d), row(d), pl.BlockSpec((dmix, tm), lambda i: (0, i))],
        out_shape=[jax.ShapeDtypeStruct((t, d), F32), jax.ShapeDtypeStruct((t, d), F32),
                   jax.ShapeDtypeStruct((dmix, t), BF16)],
        compiler_params=_cp(("parallel",)),
    )(x, y_dn, y_sg, y_cv, wo, gate)


def out_proj_bwd(dxn, y, gate, wo, dn, sg, cv):
    t, d = dxn.shape
    dmix = dn + sg + cv
    tm = _tile(t, 256, LANES)

    def body(dx_ref, y_ref, g_ref, w_ref, da_ref, db_ref, dc_ref, dyb_ref, dg_ref):
        @pl.when(pl.program_id(0) == 0)
        def _():
            dg_ref[...] = jnp.zeros_like(dg_ref)
        dx = dx_ref[...]
        dg_ref[...] += jnp.sum(dx * y_ref[...], axis=0, keepdims=True)
        dyb = (dx * g_ref[...]).astype(BF16)
        dyb_ref[...] = dyb
        dcat = lax.dot_general(dyb, w_ref[...], (((1,), (1,)), ((), ())), preferred_element_type=F32)
        da_ref[...] = dcat[:, 0:dn]
        db_ref[...] = dcat[:, dn:dn + sg]
        dc_ref[...] = dcat[:, dn + sg:]

    row = lambda w_: pl.BlockSpec((tm, w_), lambda i: (i, 0))
    vec = pl.BlockSpec((1, d), lambda i: (0, 0))
    return pl.pallas_call(
        body, name="out_proj_bwd", grid=(t // tm,),
        in_specs=[row(d), row(d), vec, pl.BlockSpec((dmix, d), lambda i: (0, 0))],
        out_specs=[row(dn), row(sg), row(cv), row(d), vec],
        out_shape=[jax.ShapeDtypeStruct((t, dn), F32), jax.ShapeDtypeStruct((t, sg), F32),
                   jax.ShapeDtypeStruct((t, cv), F32), jax.ShapeDtypeStruct((t, d), BF16),
                   jax.ShapeDtypeStruct((1, d), F32)],
        compiler_params=_cp(("arbitrary",)),
    )(dxn, y, gate, wo)


def matmul_acc(name, at, b):
    m, t = at.shape
    n = b.shape[1]
    tm, tn, tk = _tile(m, 1024, LANES), _tile(n, 2432, LANES), _tile(t, 1024, LANES)

    def body(a_ref, b_ref, o_ref):
        @pl.when(pl.program_id(2) == 0)
        def _():
            o_ref[...] = jnp.zeros_like(o_ref)
        o_ref[...] += jnp.dot(a_ref[...], b_ref[...], preferred_element_type=F32)

    return pl.pallas_call(
        body, name=name, grid=(m // tm, n // tn, t // tk),
        in_specs=[pl.BlockSpec((tm, tk), lambda i, j, k: (i, k)), pl.BlockSpec((tk, tn), lambda i, j, k: (k, j))],
        out_specs=pl.BlockSpec((tm, tn), lambda i, j, k: (i, j)),
        out_shape=jax.ShapeDtypeStruct((m, n), F32),
        compiler_params=_cp(("parallel", "parallel", "arbitrary")),
    )(at, b)


def in_proj_bwd(dp, wp, x, dxn, shift, scale, ng):
    t, d = x.shape
    npc = wp.shape[1]
    tm, tk = _tile(t, 512, LANES), _tile(npc, 2432, LANES)

    def mm_body(dp_ref, w_ref, dh_ref):
        @pl.when(pl.program_id(1) == 0)
        def _():
            dh_ref[...] = jnp.zeros_like(dh_ref)
        dh_ref[...] += lax.dot_general(dp_ref[...], w_ref[...], (((1,), (1,)), ((), ())), preferred_element_type=F32)

    dh = pl.pallas_call(
        mm_body, name="in_proj_bwd", grid=(t // tm, npc // tk),
        in_specs=[pl.BlockSpec((tm, tk), lambda i, k: (i, k)), pl.BlockSpec((d, tk), lambda i, k: (0, k))],
        out_specs=pl.BlockSpec((tm, d), lambda i, k: (i, 0)), out_shape=jax.ShapeDtypeStruct((t, d), F32),
        compiler_params=_cp(("parallel", "arbitrary")),
    )(dp, wp)

    tr = _tile(t, 256, 8)

    def norm_body(dh_ref, x_ref, dxn_ref, sh_ref, sc_ref, g_ref, dx_ref, dg_ref, dsc_ref, dsh_ref):
        @pl.when(pl.program_id(0) == 0)
        def _():
            for r in (dg_ref, dsc_ref, dsh_ref):
                r[...] = jnp.zeros_like(r)
        _, vj = jax.vjp(_modnorm, x_ref[...], g_ref[...], sc_ref[...], sh_ref[...])
        dx, dg, dsc, dsh = vj(dh_ref[...])
        dx_ref[...] = dxn_ref[...] + dx
        dg_ref[...] += dg
        dsc_ref[...] += dsc
        dsh_ref[...] += dsh

    vec = pl.BlockSpec((1, d), lambda i: (0, 0))
    row = pl.BlockSpec((tr, d), lambda i: (i, 0))
    return pl.pallas_call(
        norm_body, name="modnorm_bwd", grid=(t // tr,), in_specs=[row, row, row, vec, vec, vec],
        out_specs=[row, vec, vec, vec],
        out_shape=[jax.ShapeDtypeStruct((t, d), F32)] + [jax.ShapeDtypeStruct((1, d), F32)] * 3,
        compiler_params=_cp(("arbitrary",)),
    )(dh, x, dxn, shift, scale, ng)


def loss_head(x, tgt, fg):
    t, d = x.shape
    tm = _tile(t, 512, 8)

    def body(x_ref, t_ref, g_ref, l_ref, dx_ref, dg_ref):
        @pl.when(pl.program_id(0) == 0)
        def _():
            l_ref[...] = jnp.zeros_like(l_ref)
            dg_ref[...] = jnp.zeros_like(dg_ref)
        y, vj = jax.vjp(_rmsnorm, x_ref[...], g_ref[...])
        err = y - t_ref[...]
        part = 0.5 * jnp.sum(jnp.sum(err * err, axis=1, keepdims=True), axis=0, keepdims=True) / d
        l_ref[...] += jnp.broadcast_to(part, l_ref.shape)
        dx, dg = vj(err / d)
        dx_ref[...] = dx
        dg_ref[...] += dg

    row = pl.BlockSpec((tm, d), lambda i: (i, 0))
    vec = pl.BlockSpec((1, d), lambda i: (0, 0))
    return pl.pallas_call(
        body, name="loss_head", grid=(t // tm,), in_specs=[row, row, vec],
        out_specs=[pl.BlockSpec((1, LANES), lambda i: (0, 0)), row, vec],
        out_shape=[jax.ShapeDtypeStruct((1, LANES), F32), jax.ShapeDtypeStruct((t, d), F32),
                   jax.ShapeDtypeStruct((1, d), F32)],
        compiler_params=_cp(("arbitrary",)),
    )(x, tgt, fg)


def adamw(name, w, g, m, v):
    r, c = w.shape
    tr = _tile(r, 256, 8) if r % 8 == 0 else r

    def body(w_ref, g_ref, m_ref, v_ref, d_ref, mo_ref, vo_ref):
        d_ref[...], mo_ref[...], vo_ref[...] = _adamw_math(w_ref[...], g_ref[...], m_ref[...], v_ref[...])

    blk = pl.BlockSpec((tr, c), lambda i: (i, 0))
    return pl.pallas_call(
        body, name=name, grid=(r // tr,), in_specs=[blk] * 4, out_specs=[blk] * 3,
        out_shape=[jax.ShapeDtypeStruct((r, c), F32)] * 3, compiler_params=_cp(("parallel",)),
    )(w, g, m, v)


def ada_fwd(c_all, w_ada, b_loc):
    nl, d, cols = w_ada.shape
    nb = c_all.shape[0]
    tn = _tile(cols, 512, LANES)

    def body(c_ref, w_ref, b_ref, o_ref):
        ca = _silu(c_ref[...]).astype(BF16)
        o_ref[0] = jnp.dot(ca, w_ref[0].astype(BF16), preferred_element_type=F32) + b_ref[0]

    return pl.pallas_call(
        body, name="ada_fwd", grid=(nl, cols // tn),
        in_specs=[pl.BlockSpec((nb, d), lambda l, j: (0, 0)), pl.BlockSpec((1, d, tn), lambda l, j: (l, 0, j)),
                  pl.BlockSpec((1, 1, tn), lambda l, j: (l, 0, j))],
        out_specs=pl.BlockSpec((1, nb, tn), lambda l, j: (l, 0, j)),
        out_shape=jax.ShapeDtypeStruct((nl, nb, cols), F32),
        compiler_params=_cp(("parallel", "parallel")),
    )(c_all, w_ada, b_loc)


def ada_bwd(c_all_t, dmod_loc, w, m, v):
    nl, d, cols = w.shape
    nb = c_all_t.shape[1]
    tr = _tile(d, 256, 8)

    def body(c_ref, dm_ref, w_ref, m_ref, v_ref, g_ref, d_ref, mo_ref, vo_ref):
        ca = _silu(c_ref[...])
        dm = dm_ref[0]
        g = _lane_col(ca, 0) * dm[0:1, :]
        for b in range(1, nb):
            g = g + _lane_col(ca, b) * dm[b:b + 1, :]
        g_ref[0] = g
        d_ref[0], mo_ref[0], vo_ref[0] = _adamw_math(w_ref[0], g, m_ref[0], v_ref[0])

    blk = pl.BlockSpec((1, tr, cols), lambda l, i: (l, i, 0))
    return pl.pallas_call(
        body, name="ada_bwd", grid=(nl, d // tr),
        in_specs=[pl.BlockSpec((tr, nb), lambda l, i: (i, 0)), pl.BlockSpec((1, nb, cols), lambda l, i: (l, 0, 0)),
                  blk, blk, blk],
        out_specs=[blk] * 4, out_shape=[jax.ShapeDtypeStruct((nl, d, cols), F32)] * 4,
        compiler_params=_cp(("parallel", "parallel")),
    )(c_all_t, dmod_loc, w, m, v)


def sum8(g):
    _, r, c = g.shape
    tr = _tile(r, 256, 8)

    def body(g_ref, o_ref):
        acc = g_ref[0]
        for k in range(1, NDEV):
            acc = acc + g_ref[k]
        o_ref[...] = acc

    return pl.pallas_call(
        body, name="sum8", grid=(r // tr,), in_specs=[pl.BlockSpec((NDEV, tr, c), lambda i: (0, i, 0))],
        out_specs=pl.BlockSpec((tr, c), lambda i: (i, 0)), out_shape=jax.ShapeDtypeStruct((r, c), F32),
        compiler_params=_cp(("parallel",)),
    )(g)


def pair_sum(cflag, g, r1):
    _, ns, r, c = g.shape
    tr = _tile(r, 256, 8)

    def body(cf_ref, g0_ref, g1_ref, r_ref, ob_ref):
        keep = jnp.where(cf_ref[0:1, 0:1] == 0.0, g0_ref[0], g1_ref[0])
        ob_ref[...] = (keep + r_ref[...]).astype(BF16)

    blk = pl.BlockSpec((1, tr, c), lambda s, i: (s, i, 0))
    return pl.pallas_call(
        body, name="pair_sum", grid=(ns, r // tr),
        in_specs=[pl.BlockSpec((1, LANES), lambda s, i: (0, 0)), pl.BlockSpec((1, 1, tr, c), lambda s, i: (0, s, i, 0)),
                  pl.BlockSpec((1, 1, tr, c), lambda s, i: (1, s, i, 0)), blk], out_specs=blk,
        out_shape=jax.ShapeDtypeStruct((ns, r, c), BF16), compiler_params=_cp(("parallel", "parallel")),
    )(cflag, g, g, r1)


def relay_sum(px, py, rx, ry):
    r, c = px.shape
    nh = r // 2
    tr = _tile(nh, 256, 8)

    def body(px_ref, py_ref, rx_ref, ry_ref, qx_ref, qy_ref):
        upper = pl.program_id(0) == 0
        qx_ref[...] = (px_ref[...].astype(F32) + jnp.where(upper, 0.0, ry_ref[...].astype(F32))).astype(BF16)
        qy_ref[...] = (py_ref[...].astype(F32) + jnp.where(upper, rx_ref[...].astype(F32), 0.0)).astype(BF16)

    full = pl.BlockSpec((tr, c), lambda h, i: (h * (nh // tr) + i, 0))
    half = pl.BlockSpec((tr, c), lambda h, i: (i, 0))
    return pl.pallas_call(
        body, name="relay_sum", grid=(2, nh // tr), in_specs=[full, full, half, half], out_specs=[full, full],
        out_shape=[jax.ShapeDtypeStruct((r, c), BF16)] * 2, compiler_params=_cp(("parallel", "parallel")),
    )(px, py, rx, ry)


def chip_sum(own, r1, r2):
    r, c = own.shape
    tr = _tile(r, 256, 8)
    nslot = r2.shape[0]

    def body(g_ref, r1_ref, r2_ref, o_ref):
        acc = g_ref[...] + r1_ref[...]
        for k in range(nslot):
            acc = acc + r2_ref[k].astype(F32)
        o_ref[...] = acc

    blk = pl.BlockSpec((tr, c), lambda i: (i, 0))
    return pl.pallas_call(
        body, name="chip_sum", grid=(r // tr,),
        in_specs=[blk, blk, pl.BlockSpec((nslot, tr, c), lambda i: (0, i, 0))], out_specs=blk,
        out_shape=jax.ShapeDtypeStruct((r, c), F32), compiler_params=_cp(("parallel",)),
    )(own, r1, r2)


def adamw_halves(name, cflag, w, own, recv, m, v):
    _, r, c = w.shape
    tr = _tile(r, 256, 8)

    def body(cf_ref, w_ref, a_ref, b_ref, m_ref, v_ref, g_ref, d_ref, mo_ref, vo_ref):
        is_own = cf_ref[0:1, 0:1] == pl.program_id(0).astype(F32)
        g = jnp.where(is_own, a_ref[...], b_ref[...])
        g_ref[0] = g
        d_ref[0], mo_ref[0], vo_ref[0] = _adamw_math(w_ref[0], g, m_ref[0], v_ref[0])

    blk = pl.BlockSpec((1, tr, c), lambda h, i: (h, i, 0))
    hlf = pl.BlockSpec((tr, c), lambda h, i: (i, 0))
    return pl.pallas_call(
        body, name=name, grid=(2, r // tr),
        in_specs=[pl.BlockSpec((1, LANES), lambda h, i: (0, 0)), blk, hlf, hlf, blk, blk], out_specs=[blk] * 4,
        out_shape=[jax.ShapeDtypeStruct(w.shape, F32)] * 4, compiler_params=_cp(("parallel", "parallel")),
    )(cflag, w, own, recv, m, v)


def _me():
    return lax.axis_index("x"), lax.axis_index("y"), lax.axis_index("c")


_FLIPS = ((1, 0), (0, 1), (1, 1))


def all_gather8(v):
    m_per, n = v.shape

    def body(x_ref, out_ref, send_sems, recv_sems, local_sem):
        x, y, c = _me()
        me, sibling = (x, y, c), (x, y, 1 - c)
        chips = [(x ^ fx, y ^ fy) for fx, fy in _FLIPS]

        def rows(px, py, pc):
            return out_ref.at[pl.ds((4 * px + 2 * py + pc) * m_per, m_per), :]

        def copy(k, block, to, src=None):
            return pltpu.make_async_remote_copy(
                src_ref=rows(*block) if src is None else src, dst_ref=rows(*block),
                send_sem=send_sems.at[k], recv_sem=recv_sems.at[k], device_id=to, device_id_type=MESH)

        mine = pltpu.make_async_copy(x_ref, rows(*me), local_sem)
        mine.start()
        first = [copy(0, me, sibling, src=x_ref)]
        first += [copy(1 + j, me, (*chip, c), src=x_ref) for j, chip in enumerate(chips)]
        for cp in first:
            cp.start()
        passed = [copy(4 + j, (*chip, c), sibling) for j, chip in enumerate(chips)]
        for j, chip in enumerate(chips):
            copy(1 + j, (*chip, c), me).wait_recv()
            passed[j].start()
        copy(0, sibling, me).wait_recv()
        for j, chip in enumerate(chips):
            copy(4 + j, (*chip, 1 - c), me).wait_recv()
        for cp in first + passed:
            cp.wait_send()
        mine.wait()

    return pl.pallas_call(
        body, name="all_gather8", out_shape=jax.ShapeDtypeStruct((NDEV * m_per, n), v.dtype),
        in_specs=[pl.BlockSpec(memory_space=pltpu.VMEM)], out_specs=pl.BlockSpec(memory_space=pltpu.VMEM),
        scratch_shapes=[pltpu.SemaphoreType.DMA((7,)), pltpu.SemaphoreType.DMA((7,)), pltpu.SemaphoreType.DMA],
        compiler_params=pltpu.CompilerParams(vmem_limit_bytes=VMEM_LIMIT),
    )(v)


def gather_weights(ws):
    na = len(ws)
    ncp = 9

    def body(*refs):
        srcs, outs = refs[:na], refs[na:2 * na]
        send_sems, recv_sems = refs[2 * na:]
        x, y, c = _me()
        chip, cx, cy, cd = 2 * x + y, 2 * (1 - x) + y, 2 * x + 1 - y, 2 * (1 - x) + 1 - y
        nx, ny, sibling = (1 - x, y, c), (x, 1 - y, c), (x, y, 1 - c)

        def copy(a, k, src, dst, to):
            return pltpu.make_async_remote_copy(
                src_ref=src, dst_ref=dst, send_sem=send_sems.at[a * ncp + k], recv_sem=recv_sems.at[a * ncp + k],
                device_id=to, device_id_type=MESH)

        def land(a, k, ch, q):
            return copy(a, k, srcs[a].at[c, q], outs[a].at[ch, c, q], sibling)

        sends = []
        for a in range(na):
            for k, q, to in ((0, 0, nx), (2, 1, ny), (1, 1, nx), (3, 0, ny)):
                sends.append(copy(a, k, srcs[a].at[c, q], outs[a].at[chip, c, q], to))
                sends[-1].start()
        for a in range(na):
            land(a, 0, cx, 0).wait_recv()
            sends.append(copy(a, 4, outs[a].at[cx, c, 0], outs[a].at[cx, c, 0], ny))
            sends[-1].start()
            land(a, 2, cy, 1).wait_recv()
            sends.append(copy(a, 5, outs[a].at[cy, c, 1], outs[a].at[cy, c, 1], nx))
            sends[-1].start()
        for a in range(na):
            for k, ch, q in ((1, cx, 1), (3, cy, 0), (4, cd, 0), (5, cd, 1)):
                land(a, k, ch, q).wait_recv()
            for j, ch in enumerate((cx, cy, cd)):
                sends.append(copy(a, 6 + j, outs[a].at[ch, c], outs[a].at[ch, c], sibling))
                sends[-1].start()
        for a in range(na):
            for j, ch in enumerate((cx, cy, cd)):
                copy(a, 6 + j, outs[a].at[ch, c], outs[a].at[ch, 1 - c], sibling).wait_recv()
        for cp in sends:
            cp.wait_send()

    return pl.pallas_call(
        body, name="gather_weights",
        out_shape=[jax.ShapeDtypeStruct((NCHIP,) + w.shape, w.dtype) for w in ws],
        in_specs=[ANY] * na, out_specs=[ANY] * na,
        scratch_shapes=[pltpu.SemaphoreType.DMA((ncp * na,)), pltpu.SemaphoreType.DMA((ncp * na,))],
    )(*ws)


def sibling_swap(name, gs, other_half=False):
    na = len(gs)

    def body(*refs):
        srcs, outs = refs[:na], refs[na:2 * na]
        send_sems, recv_sems = refs[2 * na:]
        x, y, c = _me()
        cps = [pltpu.make_async_remote_copy(
            src_ref=srcs[a].at[1 - c] if other_half else srcs[a], dst_ref=outs[a], send_sem=send_sems.at[a],
            recv_sem=recv_sems.at[a], device_id=(x, y, 1 - c), device_id_type=MESH) for a in range(na)]
        for cp in cps:
            cp.start()
        for cp in cps:
            cp.wait()

    return pl.pallas_call(
        body, name=name, out_shape=[jax.ShapeDtypeStruct(g.shape[1:] if other_half else g.shape, g.dtype) for g in gs],
        in_specs=[ANY] * na, out_specs=[ANY] * na,
        scratch_shapes=[pltpu.SemaphoreType.DMA((na,)), pltpu.SemaphoreType.DMA((na,))],
    )(*gs)


def relay_out(ps):
    na = len(ps)

    def body(*refs):
        srcs, outs = refs[:na], refs[na:3 * na]
        send_sems, recv_sems = refs[3 * na:]
        x, y, c = _me()
        cd = 2 * (1 - x) + 1 - y
        cps = []
        for a in range(na):
            nh = srcs[a].shape[1] // 2
            for j, to in enumerate(((1 - x, y, c), (x, 1 - y, c))):
                cps.append(pltpu.make_async_remote_copy(
                    src_ref=srcs[a].at[cd, pl.ds(j * nh, nh)], dst_ref=outs[2 * a + j], send_sem=send_sems.at[2 * a + j],
                    recv_sem=recv_sems.at[2 * a + j], device_id=to, device_id_type=MESH))
        for cp in cps:
            cp.start()
        for cp in cps:
            cp.wait()

    shapes = [jax.ShapeDtypeStruct((p_.shape[1] // 2,) + p_.shape[2:], p_.dtype) for p_ in ps for _ in range(2)]
    return pl.pallas_call(
        body, name="relay_out", out_shape=shapes, in_specs=[ANY] * na, out_specs=[ANY] * (2 * na),
        scratch_shapes=[pltpu.SemaphoreType.DMA((2 * na,)), pltpu.SemaphoreType.DMA((2 * na,))],
    )(*ps)


def neighbour_exchange(qs):
    na = len(qs) // 2

    def body(*refs):
        srcs, outs = refs[:2 * na], refs[2 * na:3 * na]
        send_sems, recv_sems = refs[3 * na:]
        x, y, c = _me()
        cps = []
        for a in range(na):
            for j, to in enumerate(((1 - x, y, c), (x, 1 - y, c))):
                cps.append(pltpu.make_async_remote_copy(
                    src_ref=srcs[2 * a + j], dst_ref=outs[a].at[j], send_sem=send_sems.at[2 * a + j],
                    recv_sem=recv_sems.at[2 * a + j], device_id=to, device_id_type=MESH))
        for cp in cps:
            cp.start()
        for cp in cps:
            cp.wait()

    return pl.pallas_call(
        body, name="chip_exchange",
        out_shape=[jax.ShapeDtypeStruct((2,) + qs[2 * a].shape, qs[2 * a].dtype) for a in range(na)],
        in_specs=[ANY] * (2 * na), out_specs=[ANY] * na,
        scratch_shapes=[pltpu.SemaphoreType.DMA((2 * na,)), pltpu.SemaphoreType.DMA((2 * na,))],
    )(*qs)


class _Cfg:
    def __init__(self, x, a_log, sg_w, cv_ln_g, cv_w, conv_qkv):
        self.t, self.d = x.shape[1], x.shape[2]
        self.nl, self.h = a_log.shape
        self.dn = self.h * LANES
        self.g = sg_w.shape[1]
        self.sg = self.g * LANES
        self.cv = cv_ln_g.shape[1]
        self.kc = cv_w.shape[1]
        self.k4 = conv_qkv.shape[1]
        self.o_z = 3 * self.dn
        self.o_sg = 4 * self.dn
        self.o_cv = self.o_sg + 3 * self.sg
        self.o_ba = self.o_cv + 3 * self.cv
        self.npc = self.o_ba + LANES
        self.d_in = self.o_ba + 2 * self.h
        self.dmix = self.dn + self.sg + self.cv
        self.hb_fwd = _tile(self.h, 8, 1)
        self.hb_bwd = _tile(self.h, 8, 1)


def _runs(cfg):
    dn, h = cfg.dn, cfg.h
    runs = [(part * dn + hd * LANES, hd * 3 * LANES + part * LANES, LANES) for part in range(3) for hd in range(h)]
    return runs + [(3 * dn, 3 * dn, dn), (4 * dn, cfg.o_ba, 2 * h), (4 * dn + 2 * h, 4 * dn, cfg.o_ba - 4 * dn)]


def _assemble_perm(cfg, shards):
    cols = shards[0].shape[-1]
    pieces = []
    for nat, _, wdt in sorted(_runs(cfg), key=lambda r_: r_[1]):
        a = nat
        while a < nat + wdt:
            s = a // cols
            b = min(nat + wdt, (s + 1) * cols)
            pieces.append(shards[s][..., a - s * cols:b - s * cols])
            a = b
    pieces.append(jnp.zeros(shards[0].shape[:-1] + (cfg.npc - cfg.o_ba - 2 * cfg.h,), shards[0].dtype))
    return jnp.concatenate(pieces, axis=-1)


def _natural_pieces(cfg, s, cols):
    lo, hi = s * cols, (s + 1) * cols
    pieces = []
    for nat, perm, wdt in sorted(_runs(cfg)):
        a, b = max(nat, lo), min(nat + wdt, hi)
        if a < b:
            pieces.append((perm + a - nat, perm + b - nat))
    return pieces


def _layer_fwd(cfg, x, mod, lw):
    shift, scale, gate = mod
    p, ht = in_proj(x, shift, scale, lw["norm_g"], lw["wp"])
    qk_post = [_qk_post, _qk_post, _v_post]
    qkv = conv_fwd("dn_pre_fwd", cfg.k4, HALO4, lambda a: a, [(p, 0)], lw["conv_qkv"], qk_post, [], [],
                   3 * cfg.dn, 3 * LANES)
    wy = dn_wy_fwd(qkv, p, cfg.o_ba, lw["alog_b"], lw["dtb_b"], cfg.h)
    y_dn, ss = dn_seq_fwd(*wy[:6], p, cfg.o_z, lw["dn_norm_g"], cfg.h, cfg.hb_fwd)
    y_sg = sg_fwd(p, cfg.o_sg, cfg.sg, lw["sg_ln_g"], lw["sg_ln_b"], lw["sg_w"], lw["sg_bias_b"])
    cv_post = [_cv_post] * (cfg.cv // LANES)
    y_cv = conv_fwd("cv_fwd", cfg.kc, HALO31, _glu, [(p, cfg.o_cv), (p, cfg.o_cv + cfg.cv)], lw["cv_w"], cv_post,
                    [(p, cfg.o_cv + 2 * cfg.cv)], [lw["cv_b"], lw["cv_ln_g"], lw["cv_ln_b"]], cfg.cv, cfg.cv)
    xn, y, yt = out_proj(x, y_dn, y_sg, y_cv, lw["wo"], gate)
    return xn, dict(x=x, p=p, ht=ht, qkv=qkv, wy=wy, ss=ss, y=y, yt=yt)


def _layer_bwd(cfg, dxn, mod, lw, sv):
    shift, scale, gate = mod
    p = sv["p"]
    d_dn, d_sg, d_cv, dyb, dgate = out_proj_bwd(dxn, sv["y"], gate, lw["wo"], cfg.dn, cfg.sg, cfg.cv)
    g_wo = matmul_acc("w_out_grad", sv["yt"], dyb)
    cv_post = [_cv_post] * (cfg.cv // LANES)
    dcv, g_cvw, (g_cvb, g_cvlg, g_cvlb) = conv_bwd(
        "cv_bwd", cfg.kc, HALO31, _glu, [(p, cfg.o_cv), (p, cfg.o_cv + cfg.cv)], lw["cv_w"], cv_post,
        [(p, cfg.o_cv + 2 * cfg.cv)], [lw["cv_b"], lw["cv_ln_g"], lw["cv_ln_b"]], d_cv, cfg.cv, cfg.cv, tm_pref=256)
    dsg, g_sglg, g_sglb, g_sgw, g_sgb = sg_bwd(p, cfg.o_sg, cfg.sg, lw["sg_ln_g"], lw["sg_ln_b"], lw["sg_w"],
                                               lw["sg_bias_b"], d_sg)
    *dwy, dz, g_dng = dn_seq_bwd(*sv["wy"][:6], p, cfg.o_z, lw["dn_norm_g"], sv["ss"], d_dn, cfg.h, cfg.hb_bwd)
    dqkv, dba, g_al, g_dt = dn_wy_bwd(sv["qkv"], p, cfg.o_ba, lw["alog_b"], lw["dtb_b"], sv["wy"][6], *dwy, cfg.h)
    qk_post = [_qk_post, _qk_post, _v_post]
    dqkv_pre, g_cq, _ = conv_bwd("dn_pre_bwd", cfg.k4, HALO4, lambda a: a, [(p, 0)], lw["conv_qkv"], qk_post, [], [],
                                 dqkv, 3 * cfg.dn, 3 * LANES)
    dp = jnp.concatenate([dqkv_pre, dz.astype(BF16), dsg, dcv, dba.astype(BF16)], axis=1)
    g_wp = matmul_acc("w_in_grad", sv["ht"], dp)
    dx, g_ng, dscale, dshift = in_proj_bwd(dp, lw["wp"], sv["x"], dxn, shift, scale, lw["norm_g"])
    grads = dict(norm_g=g_ng, conv_qkv=g_cq, a_log=g_al[:cfg.h, 0], dt_bias=g_dt[:cfg.h, 0], dn_norm_g=g_dng,
                 sg_ln_g=g_sglg, sg_ln_b=g_sglb, sg_w=g_sgw, sg_b=g_sgb[:, :, 0], cv_w=g_cvw, cv_b=g_cvb,
                 cv_ln_g=g_cvlg, cv_ln_b=g_cvlb, wp=g_wp, wo=g_wo)
    return dx, grads, (dshift, dscale, dgate)


def _local_step(cfg, xs, tgt, mods, lws, fg):
    nl = len(lws)
    saved = []
    for l in range(nl):
        xs, sv = _layer_fwd(cfg, xs, mods[l], lws[l])
        saved.append(sv)
    loss_b, dx, g_fg = loss_head(xs, tgt, fg)
    lg = [None] * nl
    dmods = [None] * nl
    for l in reversed(range(nl)):
        dx, lg[l], dmods[l] = _layer_bwd(cfg, dx, mods[l], lws[l], saved[l])
    return loss_b, dx, g_fg, lg, dmods


SMALL = ("norm_g", "conv_qkv", "a_log", "dt_bias", "dn_norm_g", "sg_ln_g", "sg_ln_b", "sg_w", "sg_b", "cv_w",
         "cv_b", "cv_ln_g", "cv_ln_b", "final_g", "b_ada")
PACK_N = 1024


def _pack(arrs):
    flat = jnp.concatenate([a.reshape(-1).astype(F32) for a in arrs])
    rows = -(-flat.shape[0] // PACK_N)
    rows = -(-rows // 8) * 8
    return jnp.pad(flat, (0, rows * PACK_N - flat.shape[0])).reshape(rows, PACK_N)


def _unpack(buf, shapes):
    flat = buf.reshape(-1)
    out, o = [], 0
    for s in shapes:
        n = 1
        for d_ in s:
            n *= d_
        out.append(flat[o:o + n].reshape(s))
        o += n
    return out


def kernel(x, c, norm_g, w_ada, b_ada, w_in, conv_qkv, a_log, dt_bias, dn_norm_g, sg_ln_g, sg_ln_b, sg_w, sg_b, cv_w, cv_b, cv_ln_g, cv_ln_b, w_out, final_g, loss_target, m_norm_g, m_w_ada, m_b_ada, m_w_in, m_conv_qkv, m_a_log, m_dt_bias, m_dn_norm_g, m_sg_ln_g, m_sg_ln_b, m_sg_w, m_sg_b, m_cv_w, m_cv_b, m_cv_ln_g, m_cv_ln_b, m_w_out, m_final_g, v_norm_g, v_w_ada, v_b_ada, v_w_in, v_conv_qkv, v_a_log, v_dt_bias, v_dn_norm_g, v_sg_ln_g, v_sg_ln_b, v_sg_w, v_sg_b, v_cv_w, v_cv_b, v_cv_ln_g, v_cv_ln_b, v_w_out, v_final_g):
    cfg = _Cfg(x, a_log, sg_w, cv_ln_g, cv_w, conv_qkv)
    nl, d, t, h = cfg.nl, cfg.d, cfg.t, cfg.h
    lh = nl // 2
    ax, ay, ac = _me()
    chip = 2 * ax + ay
    dev = 2 * chip + ac
    wts = dict(norm_g=norm_g, w_ada=w_ada, b_ada=b_ada, w_in=w_in, conv_qkv=conv_qkv, a_log=a_log, dt_bias=dt_bias,
               dn_norm_g=dn_norm_g, sg_ln_g=sg_ln_g, sg_ln_b=sg_ln_b, sg_w=sg_w, sg_b=sg_b, cv_w=cv_w, cv_b=cv_b,
               cv_ln_g=cv_ln_g, cv_ln_b=cv_ln_b, w_out=w_out, final_g=final_g)
    mom = dict(norm_g=m_norm_g, w_ada=m_w_ada, b_ada=m_b_ada, w_in=m_w_in, conv_qkv=m_conv_qkv, a_log=m_a_log,
               dt_bias=m_dt_bias, dn_norm_g=m_dn_norm_g, sg_ln_g=m_sg_ln_g, sg_ln_b=m_sg_ln_b, sg_w=m_sg_w,
               sg_b=m_sg_b, cv_w=m_cv_w, cv_b=m_cv_b, cv_ln_g=m_cv_ln_g, cv_ln_b=m_cv_ln_b, w_out=m_w_out,
               final_g=m_final_g)
    vel = dict(norm_g=v_norm_g, w_ada=v_w_ada, b_ada=v_b_ada, w_in=v_w_in, conv_qkv=v_conv_qkv, a_log=v_a_log,
               dt_bias=v_dt_bias, dn_norm_g=v_dn_norm_g, sg_ln_g=v_sg_ln_g, sg_ln_b=v_sg_ln_b, sg_w=v_sg_w,
               sg_b=v_sg_b, cv_w=v_cv_w, cv_b=v_cv_b, cv_ln_g=v_cv_ln_g, cv_ln_b=v_cv_ln_b, w_out=v_w_out,
               final_g=v_final_g)
    ada_cols = w_ada.shape[2]
    in_cols = w_in.shape[2]
    out_rows = w_out.shape[1]
    cq_cols = conv_qkv.shape[2]
    cvw_cols = cv_w.shape[2]

    c_all = all_gather8(jnp.pad(c, ((0, 7), (0, 0)))).reshape(NDEV, 8, d)[:, 0, :]
    b_loc = lax.dynamic_slice_in_dim(b_ada, chip * ada_cols, ada_cols, axis=1)[:, None, :]
    mod_part = ada_fwd(c_all, w_ada, b_loc)
    mod_all = all_gather8(mod_part.reshape(nl * NDEV, ada_cols)).reshape(NDEV, nl, NDEV, ada_cols)
    mod_me = lax.dynamic_index_in_dim(mod_all[0::2], dev, axis=2, keepdims=False)
    mod_me = jnp.moveaxis(mod_me, 0, 1).reshape(nl, 3, 1, d)

    win_b = w_in.astype(BF16).reshape(2, 2, lh * d // 2, in_cols)
    wout_b = w_out.astype(BF16).reshape(2, 2, lh * out_rows // 2, d)
    win_all, wout_all = gather_weights([win_b, wout_b])
    win_all = lax.dynamic_update_index_in_dim(win_all, win_b, chip, axis=0)
    wout_all = lax.dynamic_update_index_in_dim(wout_all, wout_b, chip, axis=0)
    win_all = win_all.reshape(NCHIP, nl, d, in_cols)
    wp_all = [_assemble_perm(cfg, [win_all[s, l] for s in range(NCHIP)]) for l in range(nl)]
    wout_all = wout_all.reshape(NCHIP, nl, out_rows, d)
    wo_all = [jnp.concatenate([wout_all[s, l] for s in range(NCHIP)], axis=0) for l in range(nl)]

    cq_all = all_gather8(conv_qkv.reshape(nl * cfg.k4, cq_cols)).reshape(NDEV, nl, cfg.k4, cq_cols)[0::2]
    cq_full = jnp.moveaxis(cq_all, 0, 2).reshape(nl, cfg.k4, NCHIP * cq_cols)
    cq_perm = _perm_cols_qkv(cfg, cq_full)
    kcp = -(-cfg.kc // 8) * 8
    cvw_all = all_gather8(jnp.pad(cv_w, ((0, 0), (0, kcp - cfg.kc), (0, 0))).reshape(nl * kcp, cvw_cols))
    cvw_all = cvw_all.reshape(NDEV, nl, kcp, cvw_cols)[0::2]
    cvw_full = jnp.moveaxis(cvw_all, 0, 2).reshape(nl, kcp, NCHIP * cvw_cols)[:, :cfg.kc]

    hp = -(-h // 8) * 8
    lws = []
    for l in range(nl):
        lws.append(dict(
            norm_g=norm_g[l][None], wp=wp_all[l], wo=wo_all[l], conv_qkv=cq_perm[l],
            alog_b=jnp.pad(jnp.broadcast_to(a_log[l][:, None], (h, LANES)), ((0, hp - h), (0, 0))),
            dtb_b=jnp.pad(jnp.broadcast_to(dt_bias[l][:, None], (h, LANES)), ((0, hp - h), (0, 0))),
            dn_norm_g=dn_norm_g[l][None], sg_ln_g=sg_ln_g[l][None], sg_ln_b=sg_ln_b[l][None], sg_w=sg_w[l],
            sg_bias_b=jnp.broadcast_to(sg_b[l][:, :, None], (cfg.g, LANES, LANES)),
            cv_w=cvw_full[l], cv_b=cv_b[l][None], cv_ln_g=cv_ln_g[l][None], cv_ln_b=cv_ln_b[l][None]))

    mods = [(mod_me[l, 0], mod_me[l, 1], mod_me[l, 2]) for l in range(nl)]
    loss_b, dx, g_fg, lg, dmods = _local_step(cfg, x[0], loss_target[0], mods, lws, final_g[None])
    grad_x = dx[None]

    dmod = jnp.stack([jnp.concatenate(dm, axis=1)[0] for dm in dmods])
    stack = lambda k: jnp.stack([g_[k] for g_ in lg])
    small_local = [stack(k).reshape(wts_shape) for k, wts_shape in
                   (("norm_g", (nl, d)), ("conv_qkv", (nl, cfg.k4, 3 * cfg.dn)), ("a_log", (nl, h)),
                    ("dt_bias", (nl, h)), ("dn_norm_g", (nl, LANES)), ("sg_ln_g", (nl, cfg.sg)),
                    ("sg_ln_b", (nl, cfg.sg)), ("sg_w", (nl, cfg.g, LANES, LANES)), ("sg_b", (nl, cfg.g, LANES)),
                    ("cv_w", (nl, cfg.kc, cfg.cv)), ("cv_b", (nl, cfg.cv)), ("cv_ln_g", (nl, cfg.cv)),
                    ("cv_ln_b", (nl, cfg.cv)))]
    small_local[1] = _unperm_cols_qkv(cfg, small_local[1])
    small_local += [g_fg[0], dmod, loss_b[0, 0:1]]
    shapes = [a.shape for a in small_local]
    packed = _pack(small_local)
    rows = packed.shape[0]
    gathered = all_gather8(packed).reshape(NDEV, rows, PACK_N)
    summed = _unpack(sum8(gathered), shapes)
    sgrads = dict(zip(SMALL, summed[:15]))
    loss = summed[15][0]
    sgrads["conv_qkv"] = lax.dynamic_slice_in_dim(sgrads["conv_qkv"], chip * cq_cols, cq_cols, axis=2)
    sgrads["cv_w"] = lax.dynamic_slice_in_dim(sgrads["cv_w"], chip * cvw_cols, cvw_cols, axis=2)

    off = sum(math.prod(s) for s in shapes[:14])
    dmod_all = gathered.reshape(NDEV, rows * PACK_N)[:, off:off + nl * 3 * d].reshape(NDEV, nl, 3 * d)
    dmod_loc = jnp.moveaxis(lax.dynamic_slice_in_dim(dmod_all, chip * ada_cols, ada_cols, axis=2), 0, 1)
    g_wada, d_wada, nm_wada, nv_wada = ada_bwd(c_all.T, dmod_loc, w_ada, m_w_ada, v_w_ada)

    shard = lambda g_, s: jnp.concatenate([g_[:, a:b] for a, b in _natural_pieces(cfg, s, in_cols)], axis=-1)
    g_in = jnp.stack([shard(lg[hh * lh + j]["wp"], s) for hh in range(2) for s in range(NCHIP) for j in range(lh)])
    g_in = g_in.reshape(2, NCHIP, lh * d, in_cols)
    g_out = jnp.stack([lg[hh * lh + j]["wo"][s * out_rows:(s + 1) * out_rows] for hh in range(2) for s in range(NCHIP)
                       for j in range(lh)]).reshape(2, NCHIP, lh * out_rows, d)
    cflag = jnp.full((1, LANES), ac, F32)
    r1_in, r1_out = sibling_swap("swap_halves", [g_in, g_out], other_half=True)
    p_in, p_out = pair_sum(cflag, g_in, r1_in), pair_sum(cflag, g_out, r1_out)
    rx_in, ry_in, rx_out, ry_out = relay_out([p_in, p_out])
    to_x = lambda p_: lax.dynamic_index_in_dim(p_, 2 * (1 - ax) + ay, axis=0, keepdims=False)
    to_y = lambda p_: lax.dynamic_index_in_dim(p_, 2 * ax + 1 - ay, axis=0, keepdims=False)
    r2_in, r2_out = neighbour_exchange([*relay_sum(to_x(p_in), to_y(p_in), rx_in, ry_in),
                                        *relay_sum(to_x(p_out), to_y(p_out), rx_out, ry_out)])
    mine = lambda g_: lax.dynamic_index_in_dim(g_, chip, axis=0, keepdims=False)
    keep = lambda g_: lax.dynamic_index_in_dim(g_, ac, axis=0, keepdims=False)
    h_in = chip_sum(mine(keep(g_in)), mine(r1_in), r2_in)
    h_out = chip_sum(mine(keep(g_out)), mine(r1_out), r2_out)
    o_in, o_out = sibling_swap("join_halves", [h_in, h_out])

    v3 = lambda a, r_, c_: a.reshape(2, lh * r_, c_)
    grad_w_in, d_in_, nm_in, nv_in = adamw_halves("adamw_w_in", cflag, v3(w_in, d, in_cols), h_in, o_in,
                                                  v3(m_w_in, d, in_cols), v3(v_w_in, d, in_cols))
    grad_w_out, d_out_, nm_out, nv_out = adamw_halves("adamw_w_out", cflag, v3(w_out, out_rows, d), h_out, o_out,
                                                      v3(m_w_out, out_rows, d), v3(v_w_out, out_rows, d))
    grad_w_in = grad_w_in.reshape(w_in.shape)
    grad_w_out = grad_w_out.reshape(w_out.shape)
    sshapes = [wts[k].shape for k in SMALL]
    pk = lambda dct: _pack([dct[k] for k in SMALL])
    d_s, m_s, v_s = adamw("adamw_small", pk(wts), pk(sgrads), pk(mom), pk(vel))
    d_small = dict(zip(SMALL, _unpack(d_s, sshapes)))
    m_small = dict(zip(SMALL, _unpack(m_s, sshapes)))
    v_small = dict(zip(SMALL, _unpack(v_s, sshapes)))

    grads = dict(sgrads, w_ada=g_wada, w_in=grad_w_in, w_out=grad_w_out)
    deltas = dict(d_small, w_ada=d_wada, w_in=d_in_.reshape(w_in.shape), w_out=d_out_.reshape(w_out.shape))
    new_m = dict(m_small, w_ada=nm_wada, w_in=nm_in.reshape(w_in.shape), w_out=nm_out.reshape(w_out.shape))
    new_v = dict(v_small, w_ada=nv_wada, w_in=nv_in.reshape(w_in.shape), w_out=nv_out.reshape(w_out.shape))
    order = ("norm_g", "w_ada", "b_ada", "w_in", "conv_qkv", "a_log", "dt_bias", "dn_norm_g", "sg_ln_g", "sg_ln_b",
             "sg_w", "sg_b", "cv_w", "cv_b", "cv_ln_g", "cv_ln_b", "w_out", "final_g")
    return (loss, grad_x, *[grads[k] for k in order], *[deltas[k] for k in order], *[new_m[k] for k in order],
            *[new_v[k] for k in order])


def _perm_cols_qkv(cfg, w):
    lead = w.shape[:-1]
    return jnp.moveaxis(w.reshape(lead + (3, cfg.h, LANES)), -3, -2).reshape(lead + (3 * cfg.dn,))


def _unperm_cols_qkv(cfg, w):
    lead = w.shape[:-1]
    return jnp.moveaxis(w.reshape(lead + (cfg.h, 3, LANES)), -3, -2).reshape(lead + (3 * cfg.dn,))
```

```python
import functools
import math

import jax
import jax.numpy as jnp
from jax import lax
from jax.experimental import pallas as pl
from jax.experimental.pallas import tpu as pltpu

F32 = jnp.float32
BF16 = jnp.bfloat16
EPS = 1e-6
LN_EPS = 1e-5
LANES = 128
CHUNK = 64
SUBLANES = 8
HALO4 = 8
HALO31 = 32
NCHIP = 4
NDEV = 8
VMEM_LIMIT = 56 * 2 ** 20
ADAM_LR, ADAM_B1, ADAM_B2, ADAM_EPS, ADAM_WD, ADAM_STEP = 0.001, 0.9, 0.999, 1e-08, 0.01, 10
MESH = pl.DeviceIdType.MESH
ANY = pl.BlockSpec(memory_space=pl.ANY)


def _cp(sem=None, vmem=VMEM_LIMIT):
    return pltpu.CompilerParams(dimension_semantics=sem, vmem_limit_bytes=vmem)


def _tile(n, pref, mult):
    t = min(n, pref) // mult * mult
    while t > 0 and n % t:
        t -= mult
    return t if t > 0 else n


def _split(a):
    hi = a.astype(BF16)
    return hi, (a - hi.astype(F32)).astype(BF16)


def _raw_dot(a, b, ca, cb, hi):
    dn = (((ca,), (cb,)), ((), ()))
    if hi:
        ah, al = _split(a.astype(F32))
        bh, bl = _split(b.astype(F32))
        d3 = lambda x, y: lax.dot_general(x, y, dn, preferred_element_type=F32)
        return d3(ah, bh) + (d3(al, bh) + d3(ah, bl))
    return lax.dot_general(a.astype(BF16), b.astype(BF16), dn, preferred_element_type=F32)


@functools.partial(jax.custom_vjp, nondiff_argnums=(2, 3, 4))
def bdot(a, b, ca, cb, hi):
    return _raw_dot(a, b, ca, cb, hi)


def _bdot_fwd(a, b, ca, cb, hi):
    return _raw_dot(a, b, ca, cb, hi), (a, b)


def _bdot_bwd(ca, cb, hi, res, ct):
    a, b = res
    fa, fb = 1 - ca, 1 - cb
    da = _raw_dot(ct, b, 1, fb, hi) if ca == 1 else _raw_dot(b, ct, fb, 1, hi)
    db = _raw_dot(a, ct, fa, 0, hi) if cb == 0 else _raw_dot(ct, a, 0, fa, hi)
    return da.astype(a.dtype), db.astype(b.dtype)


bdot.defvjp(_bdot_fwd, _bdot_bwd)


def _sigmoid(x):
    return jax.nn.sigmoid(x)


def _silu(x):
    return x * _sigmoid(x)


def _gelu(x):
    return 0.5 * x * (1.0 + lax.erf(x * (2.0 ** -0.5)))


def _softplus(x):
    return jnp.maximum(x, 0.0) + jnp.log(1.0 + jnp.exp(-jnp.abs(x)))


def _modnorm(x, g, scale, shift):
    y = x * lax.rsqrt(jnp.mean(x * x, axis=-1, keepdims=True) + EPS)
    return (y * g) * (1.0 + scale) + shift


def _rmsnorm(x, g):
    return x * lax.rsqrt(jnp.mean(x * x, axis=-1, keepdims=True) + EPS) * g


def _layernorm(x, g, b):
    mu = jnp.mean(x, axis=-1, keepdims=True)
    xc = x - mu
    var = jnp.mean(xc * xc, axis=-1, keepdims=True)
    return xc * lax.rsqrt(var + LN_EPS) * g + b


def _l2norm(t):
    return t * lax.rsqrt(jnp.sum(t * t, axis=-1, keepdims=True) + EPS)


def _adamw_math(w, g, m, v):
    mn = ADAM_B1 * m + (1.0 - ADAM_B1) * g
    vn = ADAM_B2 * v + (1.0 - ADAM_B2) * (g * g)
    mh = mn / (1.0 - ADAM_B1 ** ADAM_STEP)
    vh = vn / (1.0 - ADAM_B2 ** ADAM_STEP)
    delta = -ADAM_LR * (mh / (jnp.sqrt(vh) + ADAM_EPS) + ADAM_WD * w)
    return delta, mn, vn


def _each(f, *lists):
    return [f(*xs) for xs in zip(*lists)]


def _wy(q, k, v, bcol, acol, alog, dtb, tinv=None):
    c = CHUNK
    r = lax.broadcasted_iota(jnp.int32, (c, c), 0)
    cc = lax.broadcasted_iota(jnp.int32, (c, c), 1)
    rr = lax.broadcasted_iota(jnp.int32, (c, 1), 0)
    tri_incl, tri_strict, eye = r >= cc, r > cc, r == cc
    beta = _each(_sigmoid, bcol)
    g = _each(lambda al, a_, dt: -jnp.exp(al) * _softplus(a_ + dt), alog, acol, dtb)
    gb = [jnp.broadcast_to(g_, (c, c)) for g_ in g]
    g_row = [jnp.sum(jnp.where(eye, b_, 0.0), axis=0, keepdims=True) for b_ in gb]
    gc_col = [jnp.sum(jnp.where(tri_incl, jnp.broadcast_to(gr, (c, c)), 0.0), axis=1, keepdims=True) for gr in g_row]
    gc_row = [jnp.sum(jnp.where(r <= cc, b_, 0.0), axis=0, keepdims=True) for b_ in gb]
    decay = _each(lambda gcc, gcr: jnp.where(tri_incl, jnp.exp(jnp.where(tri_incl, gcc - gcr, 0.0)), 0.0),
                  gc_col, gc_row)
    qs = [q_ * (q_.shape[-1] ** -0.5) for q_ in q]
    kb = _each(lambda k_, b_: k_ * b_, k, beta)
    a = _each(lambda kb_, k_, d_: jnp.where(tri_strict, bdot(kb_, k_, 1, 1, False) * d_, 0.0), kb, k, decay)
    dv = v[0].shape[-1]
    x = _each(lambda v_, b_, kb_, gcc: jnp.concatenate([v_ * b_, kb_ * jnp.exp(gcc)], axis=1), v, beta, kb, gc_col)
    if tinv is None:
        inv = [jnp.where(eye, 1.0, 0.0) - a_ for a_ in a]
        p = a
        for _ in range(5):
            p = _each(lambda p_: bdot(p_, p_, 1, 0, True), p)
            inv = _each(lambda t_, p_: t_ + bdot(t_, p_, 1, 0, True), inv, p)
        x = _each(lambda t_, x_: bdot(t_, x_, 1, 0, True), inv, x)
    else:
        x = _each(_solve_given_inverse, a, x, tinv)
    xv = [x_[:, :dv] for x_ in x]
    xk = [x_[:, dv:] for x_ in x]
    qk = _each(lambda q_, k_, d_: bdot(q_, k_, 1, 1, False) * d_, qs, k, decay)
    g_last = [jnp.sum(jnp.where(rr == c - 1, gcc, 0.0), axis=0, keepdims=True) for gcc in gc_col]
    qg = _each(lambda q_, gcc: q_ * jnp.exp(gcc), qs, gc_col)
    kd = _each(lambda k_, gl, gcc: k_ * jnp.exp(gl - gcc), k, g_last, gc_col)
    outs = (xv, xk, qg, kd, qk, [jnp.exp(gl) for gl in g_last])
    return outs + (inv,) if tinv is None else outs


@jax.custom_vjp
def _solve_given_inverse(a, rhs, tinv):
    return _raw_dot(tinv, rhs, 1, 0, True)


def _solve_given_inverse_fwd(a, rhs, tinv):
    x = _raw_dot(tinv, rhs, 1, 0, True)
    return x, (x, tinv)


def _solve_given_inverse_bwd(res, dx):
    x, tinv = res
    drhs = _raw_dot(tinv, dx, 0, 0, True)
    return -_raw_dot(drhs, x, 1, 1, True), drhs, jnp.zeros_like(tinv)


_solve_given_inverse.defvjp(_solve_given_inverse_fwd, _solve_given_inverse_bwd)


def _seq(u, w, qg, kd, qk, e, z, s, ng):
    v_new = _each(lambda u_, w_, s_: u_ - bdot(w_, s_, 1, 0, False), u, w, s)
    o1 = _each(lambda q_, s_: bdot(q_, s_, 1, 0, False), qg, s)
    o2 = _each(lambda qk_, vn: bdot(qk_, vn, 1, 0, False), qk, v_new)
    ds = _each(lambda kd_, vn: bdot(kd_, vn, 0, 0, False), kd, v_new)
    s_next = _each(lambda s_, e_, d_: s_ * e_ + d_, s, e, ds)
    y = _each(lambda a_, b_, z_: _rmsnorm(a_ + b_, ng) * _silu(z_), o1, o2, z)
    return y, s_next


def _sg_block(u, v, gt, lg, lb, w, bias):
    n = w.shape[0]
    pr = lax.broadcasted_iota(jnp.int32, (n, n), 0) // CHUNK
    pc = lax.broadcasted_iota(jnp.int32, (n, n), 1) // CHUNK
    wm = jnp.where(pr >= pc, w, 0.0)
    vl = _layernorm(_gelu(v), lg, lb)
    mixed = bdot(wm, vl, 1, 0, False) + bias
    return _gelu(u) * mixed * _silu(gt)


def _glu(a, b):
    return a * _sigmoid(b)


def _cv_post(conv, gate, cb, lg, lb):
    return _silu(_layernorm(conv + cb, lg, lb)) * _silu(gate)


def _qk_post(conv):
    return _l2norm(_silu(conv))


def _v_post(conv):
    return _silu(conv)


def in_proj(x, shift, scale, ng, wp):
    t, d = x.shape
    npc = wp.shape[1]
    tm, tn = _tile(t, 512, LANES), _tile(npc, 2432, LANES)

    def body(x_ref, sh_ref, sc_ref, g_ref, w_ref, p_ref, ht_ref, h_scr):
        @pl.when(pl.program_id(1) == 0)
        def _():
            h = _modnorm(x_ref[...], g_ref[...], sc_ref[...], sh_ref[...])
            h_scr[...] = h.astype(BF16)
            ht_ref[...] = h.T.astype(BF16)
        p_ref[...] = jnp.dot(h_scr[...], w_ref[...], preferred_element_type=F32)

    vec = pl.BlockSpec((1, d), lambda i, j: (0, 0))
    return pl.pallas_call(
        body, name="in_proj", grid=(t // tm, npc // tn),
        in_specs=[pl.BlockSpec((tm, d), lambda i, j: (i, 0)), vec, vec, vec,
                  pl.BlockSpec((d, tn), lambda i, j: (0, j))],
        out_specs=[pl.BlockSpec((tm, tn), lambda i, j: (i, j)), pl.BlockSpec((d, tm), lambda i, j: (0, i))],
        out_shape=[jax.ShapeDtypeStruct((t, npc), F32), jax.ShapeDtypeStruct((d, t), BF16)],
        scratch_shapes=[pltpu.VMEM((tm, d), BF16)],
        compiler_params=_cp(("parallel", "arbitrary")),
    )(x, shift, scale, ng, wp)


def _roll_bank(x, bank_ref, offsets):
    rows = x.shape[0]
    residues = sorted({o % SUBLANES for o in offsets})
    for slot, b in enumerate(residues):
        bank_ref[slot] = x if b == 0 else pltpu.roll(x, rows - b, 0)
    return {o: (residues.index(o % SUBLANES), o - o % SUBLANES) for o in offsets}


def _n_residues(offsets):
    return len({o % SUBLANES for o in offsets})


def conv_fwd(name, k, halo, pre_fn, pre, w, post_fns, extras, params, c_total, tc, tm_pref=512):
    t = pre[0][0].shape[0]
    tm = _tile(t, tm_pref, halo)
    npre, nex, npar = len(pre), len(extras), len(params)
    ngr = tc // LANES
    taps = [halo - (k - 1) + j for j in range(k)]

    def body(*refs):
        prev = refs[:npre]
        cur = refs[npre:2 * npre]
        w_ref = refs[2 * npre]
        ex = refs[2 * npre + 1:2 * npre + 1 + nex]
        par = refs[2 * npre + 1 + nex:2 * npre + 1 + nex + npar]
        out_ref, buf, bank = refs[-3], refs[-2], refs[-1]
        i = pl.program_id(1)
        pv = pre_fn(*[r[...] for r in prev])
        buf[0:halo, :] = jnp.where(i > 0, pv, 0.0)
        buf[halo:, :] = pre_fn(*[r[...] for r in cur])
        where = _roll_bank(buf[...], bank, taps)
        acc = None
        for j, o in enumerate(taps):
            slot, st = where[o]
            term = w_ref[j:j + 1, :] * bank[slot, st:st + tm, :]
            acc = term if acc is None else acc + term
        for gi in range(ngr):
            sl = slice(gi * LANES, (gi + 1) * LANES)
            out_ref[:, sl] = post_fns[gi](acc[:, sl], *[e[:, sl] for e in ex], *[p_[:, sl] for p_ in par])

    hb = tm // halo
    in_specs = ([pl.BlockSpec((halo, tc), functools.partial(lambda j, i, o: (jnp.maximum(i * hb - 1, 0), o + j), o=col // tc))
                 for _, col in pre]
                + [pl.BlockSpec((tm, tc), functools.partial(lambda j, i, o: (i, o + j), o=col // tc)) for _, col in pre]
                + [pl.BlockSpec((k, tc), lambda j, i: (0, j))]
                + [pl.BlockSpec((tm, tc), functools.partial(lambda j, i, o: (i, o + j), o=col // tc)) for _, col in extras]
                + [pl.BlockSpec((1, tc), lambda j, i: (0, j)) for _ in params])
    args = [a for a, _ in pre] * 2 + [w] + [a for a, _ in extras] + list(params)
    return pl.pallas_call(
        body, name=name, grid=(c_total // tc, t // tm), in_specs=in_specs,
        out_specs=pl.BlockSpec((tm, tc), lambda j, i: (i, j)),
        out_shape=jax.ShapeDtypeStruct((t, c_total), F32),
        scratch_shapes=[pltpu.VMEM((halo + tm, tc), F32), pltpu.VMEM((_n_residues(taps), halo + tm, tc), F32)],
        compiler_params=_cp(("parallel", "arbitrary")),
    )(*args)


def conv_bwd(name, k, halo, pre_fn, pre, w, post_fns, extras, params, dout, c_total, tc, tm_pref=512):
    t = pre[0][0].shape[0]
    tm = _tile(t, tm_pref, halo)
    npre, nex, npar = len(pre), len(extras), len(params)
    ngr = tc // LANES
    nout = npre + nex
    assert nout == 1 or c_total == tc
    nblk = t // tm
    ext = tm + halo
    taps = [halo - (k - 1) + j for j in range(k)]
    back = [k - 1 - j for j in range(k)]

    def body(*refs):
        it = iter(refs)
        prev = [next(it) for _ in range(npre)]
        cur = [next(it) for _ in range(npre)]
        nxt = [next(it) for _ in range(npre)]
        w_ref = next(it)
        ex_c = [next(it) for _ in range(nex)]
        ex_n = [next(it) for _ in range(nex)]
        par = [next(it) for _ in range(npar)]
        do_c, do_n = next(it), next(it)
        din_ref, dw_ref = next(it), next(it)
        dpar = [next(it) for _ in range(npar)]
        buf, dbuf, bank, dbank = next(it), next(it), next(it), next(it)
        i = pl.program_id(1)

        @pl.when(i == 0)
        def _():
            dw_ref[...] = jnp.zeros_like(dw_ref)
            for r in dpar:
                r[...] = jnp.zeros_like(r)

        buf[0:halo, :] = jnp.where(i > 0, pre_fn(*[r[...] for r in prev]), 0.0)
        cur_vals = [r[...] for r in cur]
        buf[halo:halo + tm, :] = pre_fn(*cur_vals)
        buf[halo + tm:, :] = pre_fn(*[r[...] for r in nxt])
        where = _roll_bank(buf[...], bank, taps)
        conv = None
        for j, o in enumerate(taps):
            slot, st = where[o]
            term = w_ref[j:j + 1, :] * bank[slot, st:st + ext, :]
            conv = term if conv is None else conv + term
        don = jnp.where(i < nblk - 1, do_n[...], 0.0)
        for gi in range(ngr):
            sl = slice(gi * LANES, (gi + 1) * LANES)
            pv = [p_[:, sl] for p_ in par]
            _, vj = jax.vjp(post_fns[gi], conv[:tm, sl], *[e[:, sl] for e in ex_c], *pv)
            gr = vj(do_c[:, sl])
            dbuf[0:tm, sl] = gr[0]
            for e in range(nex):
                din_ref[:, (npre + e) * tc + gi * LANES:(npre + e) * tc + (gi + 1) * LANES] = gr[1 + e].astype(din_ref.dtype)
            for q_ in range(npar):
                dpar[q_][:, sl] += gr[1 + nex + q_]
            _, vjn = jax.vjp(post_fns[gi], conv[tm:, sl], *[e[:, sl] for e in ex_n], *pv)
            dbuf[tm:, sl] = vjn(don[:, sl])[0]
        dcur = dbuf[0:tm, :]
        dwhere = _roll_bank(dbuf[...], dbank, back)
        dpre = None
        for j in range(k):
            slot, st = dwhere[back[j]]
            term = w_ref[j:j + 1, :] * dbank[slot, st:st + tm, :]
            dpre = term if dpre is None else dpre + term
            slot, st = where[taps[j]]
            dw_ref[j:j + 1, :] += jnp.sum(dcur * bank[slot, st:st + tm, :], axis=0, keepdims=True)
        _, vjp_pre = jax.vjp(pre_fn, *cur_vals)
        for e, gval in enumerate(vjp_pre(dpre)):
            din_ref[:, e * tc:(e + 1) * tc] = gval.astype(din_ref.dtype)

    hb = tm // halo
    last_h = t // halo - 1

    def spec(kind, col):
        o = col // tc
        if kind == "prev":
            return pl.BlockSpec((halo, tc), lambda j, i: (jnp.maximum(i * hb - 1, 0), o + j))
        if kind == "next":
            return pl.BlockSpec((halo, tc), lambda j, i: (jnp.minimum((i + 1) * hb, last_h), o + j))
        return pl.BlockSpec((tm, tc), lambda j, i: (i, o + j))

    in_specs = ([spec("prev", col) for _, col in pre] + [spec("cur", col) for _, col in pre]
                + [spec("next", col) for _, col in pre] + [pl.BlockSpec((k, tc), lambda j, i: (0, j))]
                + [spec("cur", col) for _, col in extras] + [spec("next", col) for _, col in extras]
                + [pl.BlockSpec((1, tc), lambda j, i: (0, j)) for _ in params]
                + [spec("cur", 0), spec("next", 0)])
    args = [a for a, _ in pre] * 3 + [w] + [a for a, _ in extras] * 2 + list(params) + [dout, dout]
    out = pl.pallas_call(
        body, name=name, grid=(c_total // tc, nblk), in_specs=in_specs,
        out_specs=[pl.BlockSpec((tm, nout * tc), lambda j, i: (i, j)), pl.BlockSpec((k, tc), lambda j, i: (0, j))]
        + [pl.BlockSpec((1, tc), lambda j, i: (0, j)) for _ in params],
        out_shape=[jax.ShapeDtypeStruct((t, nout * c_total), BF16), jax.ShapeDtypeStruct((k, c_total), F32)]
        + [jax.ShapeDtypeStruct((1, c_total), F32) for _ in params],
        scratch_shapes=[pltpu.VMEM((2 * halo + tm, tc), F32), pltpu.VMEM((ext, tc), F32),
                        pltpu.VMEM((_n_residues(taps), 2 * halo + tm, tc), F32),
                        pltpu.VMEM((_n_residues(back), ext, tc), F32)],
        compiler_params=_cp(("parallel", "arbitrary")),
    )(*args)
    return out[0], out[1], out[2:]


def _head_pick(ref_val, row):
    rr = lax.broadcasted_iota(jnp.int32, ref_val.shape, 0)
    v = jnp.sum(jnp.where(rr == row, ref_val, 0.0), axis=0, keepdims=True)
    ll = lax.broadcasted_iota(jnp.int32, v.shape, 1)
    return jnp.sum(jnp.where(ll == 0, v, 0.0), axis=1, keepdims=True)


def _lane_col(blk, lane_idx):
    ll = lax.broadcasted_iota(jnp.int32, blk.shape, 1)
    return jnp.sum(jnp.where(ll == lane_idx, blk, 0.0), axis=1, keepdims=True)


WY_HEADS = 2
WY_UNROLL = 4
WY_UNROLL_BWD = 4


def dn_wy_fwd(qkv, p, ba_col, alog_b, dtb_b, nheads):
    t = qkv.shape[0]
    tm = _tile(t, 512, CHUNK * WY_UNROLL)
    nc = tm // CHUNK
    hb = WY_HEADS
    hp = alog_b.shape[0]
    w_ = hb * LANES

    def body(qkv_ref, ba_ref, al_ref, dt_ref, u_ref, w_ref, qg_ref, kd_ref, qk_ref, e_ref, ti_ref):
        hblk = pl.program_id(1)
        alv, dtv = al_ref[...], dt_ref[...]

        def trip(cj, carry):
            units = [(cj * WY_UNROLL + cu, hl) for cu in range(WY_UNROLL) for hl in range(hb)]
            args = [[] for _ in range(7)]
            for ci, hl in units:
                rows = pl.ds(pl.multiple_of(ci * CHUNK, CHUNK), CHUNK)
                ba = ba_ref[rows, :]
                h = hblk * hb + hl
                for lst, val in zip(args, (qkv_ref[rows, hl * 384:hl * 384 + 128],
                                           qkv_ref[rows, hl * 384 + 128:hl * 384 + 256],
                                           qkv_ref[rows, hl * 384 + 256:hl * 384 + 384],
                                           _lane_col(ba, h), _lane_col(ba, nheads + h),
                                           _head_pick(alv, h), _head_pick(dtv, h))):
                    lst.append(val)
            outs = _wy(*args)
            for n, (ci, hl) in enumerate(units):
                rows = pl.ds(pl.multiple_of(ci * CHUNK, CHUNK), CHUNK)
                u, w, qg, kd, qk, e, ti = [o[n] for o in outs]
                sl = slice(hl * LANES, (hl + 1) * LANES)
                u_ref[rows, sl] = u
                w_ref[rows, sl] = w.astype(BF16)
                qg_ref[rows, sl] = qg.astype(BF16)
                kd_ref[rows, sl] = kd.astype(BF16)
                qk_ref[rows, hl * LANES:hl * LANES + CHUNK] = qk.astype(BF16)
                qk_ref[rows, hl * LANES + CHUNK:(hl + 1) * LANES] = jnp.zeros((CHUNK, LANES - CHUNK), BF16)
                e_ref[ci, :, sl] = jnp.broadcast_to(e, (1, LANES))
                ti_ref[rows, hl * LANES:hl * LANES + CHUNK] = ti
                ti_ref[rows, hl * LANES + CHUNK:(hl + 1) * LANES] = jnp.zeros((CHUNK, LANES - CHUNK), F32)
            return carry

        lax.fori_loop(0, nc // WY_UNROLL, trip, 0)

    bc = ba_col // LANES
    blk = pl.BlockSpec((tm, w_), lambda i, h: (i, h))
    tab = pl.BlockSpec((hp, LANES), lambda i, h: (0, 0))
    wide = lambda dt: jax.ShapeDtypeStruct((t, nheads * LANES), dt)
    return pl.pallas_call(
        body, name="dn_wy_fwd", grid=(t // tm, nheads // hb),
        in_specs=[pl.BlockSpec((tm, hb * 384), lambda i, h: (i, h)), pl.BlockSpec((tm, LANES), lambda i, h: (i, bc)),
                  tab, tab],
        out_specs=[blk] * 5 + [pl.BlockSpec((nc, 1, w_), lambda i, h: (i, 0, h)), blk],
        out_shape=[wide(F32), wide(BF16), wide(BF16), wide(BF16), wide(BF16),
                   jax.ShapeDtypeStruct((t // CHUNK, 1, nheads * LANES), F32), wide(F32)],
        compiler_params=_cp(("parallel", "parallel")),
    )(qkv, p, alog_b, dtb_b)


def dn_seq_fwd(u, w, qg, kd, qk, e, p, z_col, ng, nheads, hb):
    t = u.shape[0]
    tm = _tile(t, 512, CHUNK)
    nc = tm // CHUNK
    w_ = hb * LANES

    def body(u_ref, w_ref, qg_ref, kd_ref, qk_ref, e_ref, z_ref, ng_ref, y_ref, ss_ref, s_scr):
        i, hblk = pl.program_id(0), pl.program_id(1)
        for hl in range(hb):
            @pl.when(i == 0)
            def _():
                s_scr[hblk * hb + hl] = jnp.zeros((LANES, LANES), F32)
        ngv = ng_ref[...]

        def chunk(ci, carry):
            rows = pl.ds(pl.multiple_of(ci * CHUNK, CHUNK), CHUNK)
            ev = e_ref[ci]
            sls = [slice(hl * LANES, (hl + 1) * LANES) for hl in range(hb)]
            s = [s_scr[hblk * hb + hl] for hl in range(hb)]
            for hl in range(hb):
                ss_ref[ci, sls[hl], :] = s[hl]
            y, sn = _seq([u_ref[rows, sl] for sl in sls], [w_ref[rows, sl].astype(F32) for sl in sls],
                         [qg_ref[rows, sl].astype(F32) for sl in sls], [kd_ref[rows, sl].astype(F32) for sl in sls],
                         [qk_ref[rows, sl][:, :CHUNK].astype(F32) for sl in sls], [ev[:, sl] for sl in sls],
                         [z_ref[rows, sl] for sl in sls], s, ngv)
            for hl in range(hb):
                y_ref[rows, sls[hl]] = y[hl]
                s_scr[hblk * hb + hl] = sn[hl]
            return carry

        lax.fori_loop(0, nc, chunk, 0)

    zc = z_col // w_
    blk = pl.BlockSpec((tm, w_), lambda i, h: (i, h))
    return pl.pallas_call(
        body, name="dn_seq_fwd", grid=(t // tm, nheads // hb),
        in_specs=[blk] * 5 + [pl.BlockSpec((nc, 1, w_), lambda i, h: (i, 0, h)),
                              pl.BlockSpec((tm, w_), lambda i, h: (i, zc + h)),
                              pl.BlockSpec((1, LANES), lambda i, h: (0, 0))],
        out_specs=[blk, pl.BlockSpec((nc, w_, LANES), lambda i, h: (i, h, 0))],
        out_shape=[jax.ShapeDtypeStruct((t, nheads * LANES), F32),
                   jax.ShapeDtypeStruct((t // CHUNK, nheads * LANES, LANES), F32)],
        scratch_shapes=[pltpu.VMEM((nheads, LANES, LANES), F32)],
        compiler_params=_cp(("arbitrary", "arbitrary")),
    )(u, w, qg, kd, qk, e, p, ng)


def dn_seq_bwd(u, w, qg, kd, qk, e, p, z_col, ng, ss, dy, nheads, hb):
    t = u.shape[0]
    tm = _tile(t, 2048 // hb, CHUNK)
    nc = tm // CHUNK
    nblk = t // tm
    w_ = hb * LANES

    def body(u_ref, w_ref, qg_ref, kd_ref, qk_ref, e_ref, z_ref, ng_ref, ss_ref, dy_ref,
             du_ref, dw_ref, dqg_ref, dkd_ref, dqk_ref, de_ref, dz_ref, dng_ref, ds_scr):
        i, hblk = pl.program_id(0), pl.program_id(1)

        @pl.when((i == 0) & (hblk == 0))
        def _():
            dng_ref[...] = jnp.zeros_like(dng_ref)

        for hl in range(hb):
            @pl.when(i == 0)
            def _():
                ds_scr[hblk * hb + hl] = jnp.zeros((LANES, LANES), F32)
        ngv = ng_ref[...]

        def chunk(cj, carry):
            ci = nc - 1 - cj
            rows = pl.ds(pl.multiple_of(ci * CHUNK, CHUNK), CHUNK)
            ev = e_ref[ci]
            sls = [slice(hl * LANES, (hl + 1) * LANES) for hl in range(hb)]
            _, vj = jax.vjp(_seq, [u_ref[rows, sl] for sl in sls], [w_ref[rows, sl].astype(F32) for sl in sls],
                            [qg_ref[rows, sl].astype(F32) for sl in sls], [kd_ref[rows, sl].astype(F32) for sl in sls],
                            [qk_ref[rows, sl][:, :CHUNK].astype(F32) for sl in sls], [ev[:, sl] for sl in sls],
                            [z_ref[rows, sl] for sl in sls], [ss_ref[ci, sl, :] for sl in sls], ngv)
            du, dw, dqg, dkd, dqk, de, dz, dsp, dng = vj(([dy_ref[rows, sl] for sl in sls],
                                                          [ds_scr[hblk * hb + hl] for hl in range(hb)]))
            for hl, sl in enumerate(sls):
                du_ref[rows, sl] = du[hl]
                dw_ref[rows, sl] = dw[hl]
                dqg_ref[rows, sl] = dqg[hl]
                dkd_ref[rows, sl] = dkd[hl]
                dqk_ref[rows, hl * LANES:hl * LANES + CHUNK] = dqk[hl]
                dqk_ref[rows, hl * LANES + CHUNK:(hl + 1) * LANES] = jnp.zeros((CHUNK, LANES - CHUNK), F32)
                de_ref[ci, :, sl] = de[hl]
                dz_ref[rows, sl] = dz[hl]
                ds_scr[hblk * hb + hl] = dsp[hl]
            dng_ref[...] += dng
            return carry

        lax.fori_loop(0, nc, chunk, 0)

    zc = z_col // w_
    rv = lambda i: nblk - 1 - i
    blk = pl.BlockSpec((tm, w_), lambda i, h: (rv(i), h))
    eblk = pl.BlockSpec((nc, 1, w_), lambda i, h: (rv(i), 0, h))
    one = pl.BlockSpec((1, LANES), lambda i, h: (0, 0))
    wide = jax.ShapeDtypeStruct((t, nheads * LANES), F32)
    return pl.pallas_call(
        body, name="dn_seq_bwd", grid=(nblk, nheads // hb),
        in_specs=[blk] * 5 + [eblk, pl.BlockSpec((tm, w_), lambda i, h: (rv(i), zc + h)), one,
                              pl.BlockSpec((nc, w_, LANES), lambda i, h: (rv(i), h, 0)), blk],
        out_specs=[blk] * 5 + [eblk, blk, one],
        out_shape=[wide] * 5 + [jax.ShapeDtypeStruct((t // CHUNK, 1, nheads * LANES), F32), wide,
                                jax.ShapeDtypeStruct((1, LANES), F32)],
        scratch_shapes=[pltpu.VMEM((nheads, LANES, LANES), F32)],
        compiler_params=_cp(("arbitrary", "arbitrary")),
    )(u, w, qg, kd, qk, e, p, ng, ss, dy)


def dn_wy_bwd(qkv, p, ba_col, alog_b, dtb_b, ti, du, dw, dqg, dkd, dqk, de, nheads):
    t = qkv.shape[0]
    tm = _tile(t, 512, CHUNK * WY_UNROLL_BWD)
    nc = tm // CHUNK
    hb = WY_HEADS
    hp = alog_b.shape[0]
    w_ = hb * LANES

    def body(qkv_ref, ba_ref, al_ref, dt_ref, du_ref, dw_ref, dqg_ref, dkd_ref, dqk_ref, de_ref, ti_ref,
             dqkv_ref, dba_ref, dal_ref, ddt_ref):
        i, hblk = pl.program_id(0), pl.program_id(1)

        @pl.when((i == 0) & (hblk == 0))
        def _():
            dal_ref[...] = jnp.zeros_like(dal_ref)
            ddt_ref[...] = jnp.zeros_like(ddt_ref)

        @pl.when(hblk == 0)
        def _():
            dba_ref[...] = jnp.zeros_like(dba_ref)

        alv, dtv = al_ref[...], dt_ref[...]
        lane = lax.broadcasted_iota(jnp.int32, (CHUNK, LANES), 1)
        rowp = lax.broadcasted_iota(jnp.int32, (hp, LANES), 0)

        def trip(cj, carry):
            units = [(cj * WY_UNROLL_BWD + cu, hl) for cu in range(WY_UNROLL_BWD) for hl in range(hb)]
            args = [[] for _ in range(8)]
            cts = [[] for _ in range(6)]
            for ci, hl in units:
                rows = pl.ds(pl.multiple_of(ci * CHUNK, CHUNK), CHUNK)
                ba = ba_ref[rows, :]
                h = hblk * hb + hl
                sl = slice(hl * LANES, (hl + 1) * LANES)
                for lst, val in zip(args, (qkv_ref[rows, hl * 384:hl * 384 + 128],
                                           qkv_ref[rows, hl * 384 + 128:hl * 384 + 256],
                                           qkv_ref[rows, hl * 384 + 256:hl * 384 + 384],
                                           _lane_col(ba, h), _lane_col(ba, nheads + h),
                                           _head_pick(alv, h), _head_pick(dtv, h), ti_ref[rows, sl][:, :CHUNK])):
                    lst.append(val)
                de11 = jnp.sum(de_ref[ci][:, sl], axis=1, keepdims=True)
                for lst, val in zip(cts, (du_ref[rows, sl], dw_ref[rows, sl], dqg_ref[rows, sl], dkd_ref[rows, sl],
                                          dqk_ref[rows, sl][:, :CHUNK], de11)):
                    lst.append(val)
            _, vj = jax.vjp(_wy, *args)
            grads = vj(tuple(cts))
            for n, (ci, hl) in enumerate(units):
                rows = pl.ds(pl.multiple_of(ci * CHUNK, CHUNK), CHUNK)
                h = hblk * hb + hl
                dq, dk, dv, dbc, dac, dal, ddt = [g_[n] for g_ in grads[:7]]
                dqkv_ref[rows, hl * 384:hl * 384 + 128] = dq
                dqkv_ref[rows, hl * 384 + 128:hl * 384 + 256] = dk
                dqkv_ref[rows, hl * 384 + 256:hl * 384 + 384] = dv
                dba_ref[rows, :] += jnp.where(lane == h, dbc, 0.0) + jnp.where(lane == nheads + h, dac, 0.0)
                dal_ref[...] += jnp.where(rowp == h, dal, 0.0)
                ddt_ref[...] += jnp.where(rowp == h, ddt, 0.0)
            return carry

        lax.fori_loop(0, nc // WY_UNROLL_BWD, trip, 0)

    bc = ba_col // LANES
    blk = pl.BlockSpec((tm, w_), lambda i, h: (i, h))
    tab = pl.BlockSpec((hp, LANES), lambda i, h: (0, 0))
    return pl.pallas_call(
        body, name="dn_wy_bwd", grid=(t // tm, nheads // hb),
        in_specs=[pl.BlockSpec((tm, hb * 384), lambda i, h: (i, h)), pl.BlockSpec((tm, LANES), lambda i, h: (i, bc)),
                  tab, tab] + [blk] * 5 + [pl.BlockSpec((nc, 1, w_), lambda i, h: (i, 0, h)), blk],
        out_specs=[pl.BlockSpec((tm, hb * 384), lambda i, h: (i, h)), pl.BlockSpec((tm, LANES), lambda i, h: (i, 0)),
                   tab, tab],
        out_shape=[jax.ShapeDtypeStruct((t, nheads * 384), F32), jax.ShapeDtypeStruct((t, LANES), F32),
                   jax.ShapeDtypeStruct((hp, LANES), F32), jax.ShapeDtypeStruct((hp, LANES), F32)],
        compiler_params=_cp(("arbitrary", "arbitrary")),
    )(qkv, p, alog_b, dtb_b, du, dw, dqg, dkd, dqk, de, ti)


def sg_fwd(p, col, sg, lg, lb, w, bias_b):
    t = p.shape[0]
    ng_ = sg // LANES
    tm = _tile(t, 256, LANES)
    cb = col // sg

    def body(u_ref, v_ref, g_ref, lg_ref, lb_ref, w_ref, b_ref, y_ref):
        for n in range(tm // LANES):
            rs = slice(n * LANES, (n + 1) * LANES)
            for gi in range(ng_):
                sl = slice(gi * LANES, (gi + 1) * LANES)
                y_ref[rs, sl] = _sg_block(u_ref[rs, sl], v_ref[rs, sl], g_ref[rs, sl], lg_ref[:, sl], lb_ref[:, sl],
                                          w_ref[gi], b_ref[gi])

    vec = pl.BlockSpec((1, sg), lambda i: (0, 0))
    full = pl.BlockSpec((ng_, LANES, LANES), lambda i: (0, 0, 0))
    return pl.pallas_call(
        body, name="sg_fwd", grid=(t // tm,),
        in_specs=[pl.BlockSpec((tm, sg), lambda i: (i, cb)), pl.BlockSpec((tm, sg), lambda i: (i, cb + 1)),
                  pl.BlockSpec((tm, sg), lambda i: (i, cb + 2)), vec, vec, full, full],
        out_specs=pl.BlockSpec((tm, sg), lambda i: (i, 0)),
        out_shape=jax.ShapeDtypeStruct((t, sg), F32),
        compiler_params=_cp(("parallel",)),
    )(p, p, p, lg, lb, w, bias_b)


def sg_bwd(p, col, sg, lg, lb, w, bias_b, dy):
    t = p.shape[0]
    ng_ = sg // LANES
    tm = _tile(t, 256, LANES)
    cb = col // sg

    def body(u_ref, v_ref, g_ref, lg_ref, lb_ref, w_ref, b_ref, dy_ref, d_ref, dlg_ref, dlb_ref, dw_ref, db_ref):
        @pl.when(pl.program_id(0) == 0)
        def _():
            for r in (dlg_ref, dlb_ref, dw_ref, db_ref):
                r[...] = jnp.zeros_like(r)

        for n in range(tm // LANES):
            rs = slice(n * LANES, (n + 1) * LANES)
            for gi in range(ng_):
                sl = slice(gi * LANES, (gi + 1) * LANES)
                _, vj = jax.vjp(_sg_block, u_ref[rs, sl], v_ref[rs, sl], g_ref[rs, sl], lg_ref[:, sl], lb_ref[:, sl],
                                w_ref[gi], b_ref[gi])
                du, dv, dg, dlg, dlb, dw, db = vj(dy_ref[rs, sl])
                d_ref[rs, gi * LANES:(gi + 1) * LANES] = du.astype(BF16)
                d_ref[rs, sg + gi * LANES:sg + (gi + 1) * LANES] = dv.astype(BF16)
                d_ref[rs, 2 * sg + gi * LANES:2 * sg + (gi + 1) * LANES] = dg.astype(BF16)
                dlg_ref[:, sl] += dlg
                dlb_ref[:, sl] += dlb
                dw_ref[gi] += dw
                db_ref[gi] += jnp.broadcast_to(jnp.sum(db, axis=1, keepdims=True), (LANES, LANES))

    vec = pl.BlockSpec((1, sg), lambda i: (0, 0))
    full = pl.BlockSpec((ng_, LANES, LANES), lambda i: (0, 0, 0))
    return pl.pallas_call(
        body, name="sg_bwd", grid=(t // tm,),
        in_specs=[pl.BlockSpec((tm, sg), lambda i: (i, cb)), pl.BlockSpec((tm, sg), lambda i: (i, cb + 1)),
                  pl.BlockSpec((tm, sg), lambda i: (i, cb + 2)), vec, vec, full, full,
                  pl.BlockSpec((tm, sg), lambda i: (i, 0))],
        out_specs=[pl.BlockSpec((tm, 3 * sg), lambda i: (i, 0)), vec, vec, full, full],
        out_shape=[jax.ShapeDtypeStruct((t, 3 * sg), BF16), jax.ShapeDtypeStruct((1, sg), F32),
                   jax.ShapeDtypeStruct((1, sg), F32), jax.ShapeDtypeStruct((ng_, LANES, LANES), F32),
                   jax.ShapeDtypeStruct((ng_, LANES, LANES), F32)],
        compiler_params=_cp(("arbitrary",)),
    )(p, p, p, lg, lb, w, bias_b, dy)


def out_proj(x, y_dn, y_sg, y_cv, wo, gate):
    t, d = x.shape
    dn, sg, cv = y_dn.shape[1], y_sg.shape[1], y_cv.shape[1]
    dmix = dn + sg + cv
    tm = _tile(t, 256, LANES)

    def body(x_ref, a_ref, b_ref, c_ref, w_ref, g_ref, xn_ref, y_ref, yt_ref):
        a, b, c = a_ref[...], b_ref[...], c_ref[...]
        y = (jnp.dot(a.astype(BF16), w_ref[0:dn, :], preferred_element_type=F32)
             + jnp.dot(b.astype(BF16), w_ref[dn:dn + sg, :], preferred_element_type=F32)
             + jnp.dot(c.astype(BF16), w_ref[dn + sg:, :], preferred_element_type=F32))
        y_ref[...] = y
        xn_ref[...] = x_ref[...] + g_ref[...] * y
        yt_ref[0:dn, :] = a.T.astype(BF16)
        yt_ref[dn:dn + sg, :] = b.T.astype(BF16)
        yt_ref[dn + sg:, :] = c.T.astype(BF16)

    row = lambda w_: pl.BlockSpec((tm, w_), lambda i: (i, 0))
    return pl.pallas_call(
        body, name="out_proj", grid=(t // tm,),
        in_specs=[row(d), row(dn), row(sg), row(cv), pl.BlockSpec((dmix, d), lambda i: (0, 0)),
                  pl.BlockSpec((1, d), lambda i: (0, 0))],
        out_specs=[row(d), row(d), pl.BlockSpec((dmix, tm), lambda i: (0, i))],
        out_shape=[jax.ShapeDtypeStruct((t, d), F32), jax.ShapeDtypeStruct((t, d), F32),
                   jax.ShapeDtypeStruct((dmix, t), BF16)],
        compiler_params=_cp(("parallel",)),
    )(x, y_dn, y_sg, y_cv, wo, gate)


def out_proj_bwd(dxn, y, gate, wo, dn, sg, cv):
    t, d = dxn.shape
    dmix = dn + sg + cv
    tm = _tile(t, 256, LANES)

    def body(dx_ref, y_ref, g_ref, w_ref, da_ref, db_ref, dc_ref, dyb_ref, dg_ref):
        @pl.when(pl.program_id(0) == 0)
        def _():
            dg_ref[...] = jnp.zeros_like(dg_ref)
        dx = dx_ref[...]
        dg_ref[...] += jnp.sum(dx * y_ref[...], axis=0, keepdims=True)
        dyb = (dx * g_ref[...]).astype(BF16)
        dyb_ref[...] = dyb
        dcat = lax.dot_general(dyb, w_ref[...], (((1,), (1,)), ((), ())), preferred_element_type=F32)
        da_ref[...] = dcat[:, 0:dn]
        db_ref[...] = dcat[:, dn:dn + sg]
        dc_ref[...] = dcat[:, dn + sg:]

    row = lambda w_: pl.BlockSpec((tm, w_), lambda i: (i, 0))
    vec = pl.BlockSpec((1, d), lambda i: (0, 0))
    return pl.pallas_call(
        body, name="out_proj_bwd", grid=(t // tm,),
        in_specs=[row(d), row(d), vec, pl.BlockSpec((dmix, d), lambda i: (0, 0))],
        out_specs=[row(dn), row(sg), row(cv), row(d), vec],
        out_shape=[jax.ShapeDtypeStruct((t, dn), F32), jax.ShapeDtypeStruct((t, sg), F32),
                   jax.ShapeDtypeStruct((t, cv), F32), jax.ShapeDtypeStruct((t, d), BF16),
                   jax.ShapeDtypeStruct((1, d), F32)],
        compiler_params=_cp(("arbitrary",)),
    )(dxn, y, gate, wo)


def matmul_acc(name, at, b):
    m, t = at.shape
    n = b.shape[1]
    tm, tn, tk = _tile(m, 1024, LANES), _tile(n, 2432, LANES), _tile(t, 1024, LANES)

    def body(a_ref, b_ref, o_ref):
        @pl.when(pl.program_id(2) == 0)
        def _():
            o_ref[...] = jnp.zeros_like(o_ref)
        o_ref[...] += jnp.dot(a_ref[...], b_ref[...], preferred_element_type=F32)

    return pl.pallas_call(
        body, name=name, grid=(m // tm, n // tn, t // tk),
        in_specs=[pl.BlockSpec((tm, tk), lambda i, j, k: (i, k)), pl.BlockSpec((tk, tn), lambda i, j, k: (k, j))],
        out_specs=pl.BlockSpec((tm, tn), lambda i, j, k: (i, j)),
        out_shape=jax.ShapeDtypeStruct((m, n), F32),
        compiler_params=_cp(("parallel", "parallel", "arbitrary")),
    )(at, b)


def in_proj_bwd(dp, wp, x, dxn, shift, scale, ng):
    t, d = x.shape
    npc = wp.shape[1]
    tm, tk = _tile(t, 512, LANES), _tile(npc, 2432, LANES)

    def mm_body(dp_ref, w_ref, dh_ref):
        @pl.when(pl.program_id(1) == 0)
        def _():
            dh_ref[...] = jnp.zeros_like(dh_ref)
        dh_ref[...] += lax.dot_general(dp_ref[...], w_ref[...], (((1,), (1,)), ((), ())), preferred_element_type=F32)

    dh = pl.pallas_call(
        mm_body, name="in_proj_bwd", grid=(t // tm, npc // tk),
        in_specs=[pl.BlockSpec((tm, tk), lambda i, k: (i, k)), pl.BlockSpec((d, tk), lambda i, k: (0, k))],
        out_specs=pl.BlockSpec((tm, d), lambda i, k: (i, 0)), out_shape=jax.ShapeDtypeStruct((t, d), F32),
        compiler_params=_cp(("parallel", "arbitrary")),
    )(dp, wp)

    tr = _tile(t, 256, 8)

    def norm_body(dh_ref, x_ref, dxn_ref, sh_ref, sc_ref, g_ref, dx_ref, dg_ref, dsc_ref, dsh_ref):
        @pl.when(pl.program_id(0) == 0)
        def _():
            for r in (dg_ref, dsc_ref, dsh_ref):
                r[...] = jnp.zeros_like(r)
        _, vj = jax.vjp(_modnorm, x_ref[...], g_ref[...], sc_ref[...], sh_ref[...])
        dx, dg, dsc, dsh = vj(dh_ref[...])
        dx_ref[...] = dxn_ref[...] + dx
        dg_ref[...] += dg
        dsc_ref[...] += dsc
        dsh_ref[...] += dsh

    vec = pl.BlockSpec((1, d), lambda i: (0, 0))
    row = pl.BlockSpec((tr, d), lambda i: (i, 0))
    return pl.pallas_call(
        norm_body, name="modnorm_bwd", grid=(t // tr,), in_specs=[row, row, row, vec, vec, vec],
        out_specs=[row, vec, vec, vec],
        out_shape=[jax.ShapeDtypeStruct((t, d), F32)] + [jax.ShapeDtypeStruct((1, d), F32)] * 3,
        compiler_params=_cp(("arbitrary",)),
    )(dh, x, dxn, shift, scale, ng)


def loss_head(x, tgt, fg):
    t, d = x.shape
    tm = _tile(t, 512, 8)

    def body(x_ref, t_ref, g_ref, l_ref, dx_ref, dg_ref):
        @pl.when(pl.program_id(0) == 0)
        def _():
            l_ref[...] = jnp.zeros_like(l_ref)
            dg_ref[...] = jnp.zeros_like(dg_ref)
        y, vj = jax.vjp(_rmsnorm, x_ref[...], g_ref[...])
        err = y - t_ref[...]
        part = 0.5 * jnp.sum(jnp.sum(err * err, axis=1, keepdims=True), axis=0, keepdims=True) / d
        l_ref[...] += jnp.broadcast_to(part, l_ref.shape)
        dx, dg = vj(err / d)
        dx_ref[...] = dx
        dg_ref[...] += dg

    row = pl.BlockSpec((tm, d), lambda i: (i, 0))
    vec = pl.BlockSpec((1, d), lambda i: (0, 0))
    return pl.pallas_call(
        body, name="loss_head", grid=(t // tm,), in_specs=[row, row, vec],
        out_specs=[pl.BlockSpec((1, LANES), lambda i: (0, 0)), row, vec],
        out_shape=[jax.ShapeDtypeStruct((1, LANES), F32), jax.ShapeDtypeStruct((t, d), F32),
                   jax.ShapeDtypeStruct((1, d), F32)],
        compiler_params=_cp(("arbitrary",)),
    )(x, tgt, fg)


def adamw(name, w, g, m, v):
    r, c = w.shape
    tr = _tile(r, 256, 8) if r % 8 == 0 else r

    def body(w_ref, g_ref, m_ref, v_ref, d_ref, mo_ref, vo_ref):
        d_ref[...], mo_ref[...], vo_ref[...] = _adamw_math(w_ref[...], g_ref[...], m_ref[...], v_ref[...])

    blk = pl.BlockSpec((tr, c), lambda i: (i, 0))
    return pl.pallas_call(
        body, name=name, grid=(r // tr,), in_specs=[blk] * 4, out_specs=[blk] * 3,
        out_shape=[jax.ShapeDtypeStruct((r, c), F32)] * 3, compiler_params=_cp(("parallel",)),
    )(w, g, m, v)


def ada_fwd(c_all, w_ada, b_loc):
    nl, d, cols = w_ada.shape
    nb = c_all.shape[0]
    tn = _tile(cols, 512, LANES)

    def body(c_ref, w_ref, b_ref, o_ref):
        ca = _silu(c_ref[...]).astype(BF16)
        o_ref[0] = jnp.dot(ca, w_ref[0].astype(BF16), preferred_element_type=F32) + b_ref[0]

    return pl.pallas_call(
        body, name="ada_fwd", grid=(nl, cols // tn),
        in_specs=[pl.BlockSpec((nb, d), lambda l, j: (0, 0)), pl.BlockSpec((1, d, tn), lambda l, j: (l, 0, j)),
                  pl.BlockSpec((1, 1, tn), lambda l, j: (l, 0, j))],
        out_specs=pl.BlockSpec((1, nb, tn), lambda l, j: (l, 0, j)),
        out_shape=jax.ShapeDtypeStruct((nl, nb, cols), F32),
        compiler_params=_cp(("parallel", "parallel")),
    )(c_all, w_ada, b_loc)


def ada_bwd(c_all_t, dmod_loc, w, m, v, qs):
    nl, d, cols = w.shape
    nb = c_all_t.shape[1]
    tr = _tile(d, 256, 8)
    na = len(qs) // 2
    ni = d // tr

    def body(c_ref, dm_ref, w_ref, m_ref, v_ref, *rest):
        q_refs = rest[:2 * na]
        g_ref, d_ref, mo_ref, vo_ref = rest[2 * na:2 * na + 4]
        outs = rest[2 * na + 4:3 * na + 4]
        send_sems, recv_sems = rest[3 * na + 4:]
        x, y, c = _me()
        l, i = pl.program_id(0), pl.program_id(1)

        def copies():
            return [pltpu.make_async_remote_copy(
                src_ref=q_refs[2 * a + j], dst_ref=outs[a].at[j], send_sem=send_sems.at[2 * a + j],
                recv_sem=recv_sems.at[2 * a + j], device_id=to, device_id_type=MESH)
                for a in range(na) for j, to in enumerate(((1 - x, y, c), (x, 1 - y, c)))]

        @pl.when((l == 0) & (i == 0))
        def _():
            for cp in copies():
                cp.start()

        ca = _silu(c_ref[...])
        dm = dm_ref[0]
        g = _lane_col(ca, 0) * dm[0:1, :]
        for b in range(1, nb):
            g = g + _lane_col(ca, b) * dm[b:b + 1, :]
        g_ref[0] = g
        d_ref[0], mo_ref[0], vo_ref[0] = _adamw_math(w_ref[0], g, m_ref[0], v_ref[0])

        @pl.when((l == nl - 1) & (i == ni - 1))
        def _():
            for cp in copies():
                cp.wait()

    blk = pl.BlockSpec((1, tr, cols), lambda l, i: (l, i, 0))
    return pl.pallas_call(
        body, name="ada_bwd", grid=(nl, ni),
        in_specs=[pl.BlockSpec((tr, nb), lambda l, i: (i, 0)), pl.BlockSpec((1, nb, cols), lambda l, i: (l, 0, 0)),
                  blk, blk, blk] + [ANY] * (2 * na),
        out_specs=[blk] * 4 + [ANY] * na,
        out_shape=[jax.ShapeDtypeStruct((nl, d, cols), F32)] * 4
        + [jax.ShapeDtypeStruct((2,) + qs[2 * a].shape, qs[2 * a].dtype) for a in range(na)],
        scratch_shapes=[pltpu.SemaphoreType.DMA((2 * na,)), pltpu.SemaphoreType.DMA((2 * na,))],
        compiler_params=_cp(("arbitrary", "arbitrary")),
    )(c_all_t, dmod_loc, w, m, v, *qs)


def sum8(g):
    _, r, c = g.shape
    tr = _tile(r, 256, 8)

    def body(g_ref, o_ref):
        acc = g_ref[0]
        for k in range(1, NDEV):
            acc = acc + g_ref[k]
        o_ref[...] = acc

    return pl.pallas_call(
        body, name="sum8", grid=(r // tr,), in_specs=[pl.BlockSpec((NDEV, tr, c), lambda i: (0, i, 0))],
        out_specs=pl.BlockSpec((tr, c), lambda i: (i, 0)), out_shape=jax.ShapeDtypeStruct((r, c), F32),
        compiler_params=_cp(("parallel",)),
    )(g)


def pair_sum(cflag, g, r1):
    _, ns, r, c = g.shape
    tr = _tile(r, 256, 8)

    def body(cf_ref, g0_ref, g1_ref, r_ref, ob_ref):
        keep = jnp.where(cf_ref[0:1, 0:1] == 0.0, g0_ref[0], g1_ref[0])
        ob_ref[...] = (keep + r_ref[...]).astype(BF16)

    blk = pl.BlockSpec((1, tr, c), lambda s, i: (s, i, 0))
    return pl.pallas_call(
        body, name="pair_sum", grid=(ns, r // tr),
        in_specs=[pl.BlockSpec((1, LANES), lambda s, i: (0, 0)), pl.BlockSpec((1, 1, tr, c), lambda s, i: (0, s, i, 0)),
                  pl.BlockSpec((1, 1, tr, c), lambda s, i: (1, s, i, 0)), blk], out_specs=blk,
        out_shape=jax.ShapeDtypeStruct((ns, r, c), BF16), compiler_params=_cp(("parallel", "parallel")),
    )(cflag, g, g, r1)


def relay_sum(px, py, rx, ry):
    r, c = px.shape
    nh = r // 2
    tr = _tile(nh, 256, 8)

    def body(px_ref, py_ref, rx_ref, ry_ref, qx_ref, qy_ref):
        upper = pl.program_id(0) == 0
        qx_ref[...] = (px_ref[...].astype(F32) + jnp.where(upper, 0.0, ry_ref[...].astype(F32))).astype(BF16)
        qy_ref[...] = (py_ref[...].astype(F32) + jnp.where(upper, rx_ref[...].astype(F32), 0.0)).astype(BF16)

    full = pl.BlockSpec((tr, c), lambda h, i: (h * (nh // tr) + i, 0))
    half = pl.BlockSpec((tr, c), lambda h, i: (i, 0))
    return pl.pallas_call(
        body, name="relay_sum", grid=(2, nh // tr), in_specs=[full, full, half, half], out_specs=[full, full],
        out_shape=[jax.ShapeDtypeStruct((r, c), BF16)] * 2, compiler_params=_cp(("parallel", "parallel")),
    )(px, py, rx, ry)


def chip_sum(own, r1, r2):
    r, c = own.shape
    tr = _tile(r, 256, 8)
    nslot = r2.shape[0]

    def body(g_ref, r1_ref, r2_ref, o_ref):
        acc = g_ref[...] + r1_ref[...]
        for k in range(nslot):
            acc = acc + r2_ref[k].astype(F32)
        o_ref[...] = acc

    blk = pl.BlockSpec((tr, c), lambda i: (i, 0))
    return pl.pallas_call(
        body, name="chip_sum", grid=(r // tr,),
        in_specs=[blk, blk, pl.BlockSpec((nslot, tr, c), lambda i: (0, i, 0))], out_specs=blk,
        out_shape=jax.ShapeDtypeStruct((r, c), F32), compiler_params=_cp(("parallel",)),
    )(own, r1, r2)


def adamw_halves(name, cflag, w, own, recv, m, v):
    _, r, c = w.shape
    tr = _tile(r, 256, 8)

    def body(cf_ref, w_ref, a_ref, b_ref, m_ref, v_ref, g_ref, d_ref, mo_ref, vo_ref):
        is_own = cf_ref[0:1, 0:1] == pl.program_id(0).astype(F32)
        g = jnp.where(is_own, a_ref[...], b_ref[...])
        g_ref[0] = g
        d_ref[0], mo_ref[0], vo_ref[0] = _adamw_math(w_ref[0], g, m_ref[0], v_ref[0])

    blk = pl.BlockSpec((1, tr, c), lambda h, i: (h, i, 0))
    hlf = pl.BlockSpec((tr, c), lambda h, i: (i, 0))
    return pl.pallas_call(
        body, name=name, grid=(2, r // tr),
        in_specs=[pl.BlockSpec((1, LANES), lambda h, i: (0, 0)), blk, hlf, hlf, blk, blk], out_specs=[blk] * 4,
        out_shape=[jax.ShapeDtypeStruct(w.shape, F32)] * 4, compiler_params=_cp(("parallel", "parallel")),
    )(cflag, w, own, recv, m, v)


def _me():
    return lax.axis_index("x"), lax.axis_index("y"), lax.axis_index("c")


_FLIPS = ((1, 0), (0, 1), (1, 1))


def all_gather8(v):
    m_per, n = v.shape

    def body(x_ref, out_ref, send_sems, recv_sems, local_sem):
        x, y, c = _me()
        me, sibling = (x, y, c), (x, y, 1 - c)
        chips = [(x ^ fx, y ^ fy) for fx, fy in _FLIPS]

        def rows(px, py, pc):
            return out_ref.at[pl.ds((4 * px + 2 * py + pc) * m_per, m_per), :]

        def copy(k, block, to, src=None):
            return pltpu.make_async_remote_copy(
                src_ref=rows(*block) if src is None else src, dst_ref=rows(*block),
                send_sem=send_sems.at[k], recv_sem=recv_sems.at[k], device_id=to, device_id_type=MESH)

        mine = pltpu.make_async_copy(x_ref, rows(*me), local_sem)
        mine.start()
        first = [copy(0, me, sibling, src=x_ref)]
        first += [copy(1 + j, me, (*chip, c), src=x_ref) for j, chip in enumerate(chips)]
        for cp in first:
            cp.start()
        passed = [copy(4 + j, (*chip, c), sibling) for j, chip in enumerate(chips)]
        for j, chip in enumerate(chips):
            copy(1 + j, (*chip, c), me).wait_recv()
            passed[j].start()
        copy(0, sibling, me).wait_recv()
        for j, chip in enumerate(chips):
            copy(4 + j, (*chip, 1 - c), me).wait_recv()
        for cp in first + passed:
            cp.wait_send()
        mine.wait()

    return pl.pallas_call(
        body, name="all_gather8", out_shape=jax.ShapeDtypeStruct((NDEV * m_per, n), v.dtype),
        in_specs=[pl.BlockSpec(memory_space=pltpu.VMEM)], out_specs=pl.BlockSpec(memory_space=pltpu.VMEM),
        scratch_shapes=[pltpu.SemaphoreType.DMA((7,)), pltpu.SemaphoreType.DMA((7,)), pltpu.SemaphoreType.DMA],
        compiler_params=pltpu.CompilerParams(vmem_limit_bytes=VMEM_LIMIT),
    )(v)


def gather_weights(ws):
    na = len(ws)
    ncp = 9

    def body(*refs):
        srcs, outs = refs[:na], refs[na:2 * na]
        send_sems, recv_sems = refs[2 * na:]
        x, y, c = _me()
        chip, cx, cy, cd = 2 * x + y, 2 * (1 - x) + y, 2 * x + 1 - y, 2 * (1 - x) + 1 - y
        nx, ny, sibling = (1 - x, y, c), (x, 1 - y, c), (x, y, 1 - c)

        def copy(a, k, src, dst, to):
            return pltpu.make_async_remote_copy(
                src_ref=src, dst_ref=dst, send_sem=send_sems.at[a * ncp + k], recv_sem=recv_sems.at[a * ncp + k],
                device_id=to, device_id_type=MESH)

        def land(a, k, ch, q):
            return copy(a, k, srcs[a].at[c, q], outs[a].at[ch, c, q], sibling)

        sends = []
        for a in range(na):
            for k, q, to in ((0, 0, nx), (2, 1, ny), (1, 1, nx), (3, 0, ny)):
                sends.append(copy(a, k, srcs[a].at[c, q], outs[a].at[chip, c, q], to))
                sends[-1].start()
        for a in range(na):
            land(a, 0, cx, 0).wait_recv()
            sends.append(copy(a, 4, outs[a].at[cx, c, 0], outs[a].at[cx, c, 0], ny))
            sends[-1].start()
            land(a, 2, cy, 1).wait_recv()
            sends.append(copy(a, 5, outs[a].at[cy, c, 1], outs[a].at[cy, c, 1], nx))
            sends[-1].start()
        for a in range(na):
            for k, ch, q in ((1, cx, 1), (3, cy, 0), (4, cd, 0), (5, cd, 1)):
                land(a, k, ch, q).wait_recv()
            for j, ch in enumerate((cx, cy, cd)):
                sends.append(copy(a, 6 + j, outs[a].at[ch, c], outs[a].at[ch, c], sibling))
                sends[-1].start()
        for a in range(na):
            for j, ch in enumerate((cx, cy, cd)):
                copy(a, 6 + j, outs[a].at[ch, c], outs[a].at[ch, 1 - c], sibling).wait_recv()
        for cp in sends:
            cp.wait_send()

    return pl.pallas_call(
        body, name="gather_weights",
        out_shape=[jax.ShapeDtypeStruct((NCHIP,) + w.shape, w.dtype) for w in ws],
        in_specs=[ANY] * na, out_specs=[ANY] * na,
        scratch_shapes=[pltpu.SemaphoreType.DMA((ncp * na,)), pltpu.SemaphoreType.DMA((ncp * na,))],
    )(*ws)


def sibling_swap(name, gs, other_half=False):
    na = len(gs)

    def body(*refs):
        srcs, outs = refs[:na], refs[na:2 * na]
        send_sems, recv_sems = refs[2 * na:]
        x, y, c = _me()
        cps = [pltpu.make_async_remote_copy(
            src_ref=srcs[a].at[1 - c] if other_half else srcs[a], dst_ref=outs[a], send_sem=send_sems.at[a],
            recv_sem=recv_sems.at[a], device_id=(x, y, 1 - c), device_id_type=MESH) for a in range(na)]
        for cp in cps:
            cp.start()
        for cp in cps:
            cp.wait()

    return pl.pallas_call(
        body, name=name, out_shape=[jax.ShapeDtypeStruct(g.shape[1:] if other_half else g.shape, g.dtype) for g in gs],
        in_specs=[ANY] * na, out_specs=[ANY] * na,
        scratch_shapes=[pltpu.SemaphoreType.DMA((na,)), pltpu.SemaphoreType.DMA((na,))],
    )(*gs)


def relay_out(ps):
    na = len(ps)

    def body(*refs):
        srcs, outs = refs[:na], refs[na:3 * na]
        send_sems, recv_sems = refs[3 * na:]
        x, y, c = _me()
        cd = 2 * (1 - x) + 1 - y
        cps = []
        for a in range(na):
            nh = srcs[a].shape[1] // 2
            for j, to in enumerate(((1 - x, y, c), (x, 1 - y, c))):
                cps.append(pltpu.make_async_remote_copy(
                    src_ref=srcs[a].at[cd, pl.ds(j * nh, nh)], dst_ref=outs[2 * a + j], send_sem=send_sems.at[2 * a + j],
                    recv_sem=recv_sems.at[2 * a + j], device_id=to, device_id_type=MESH))
        for cp in cps:
            cp.start()
        for cp in cps:
            cp.wait()

    shapes = [jax.ShapeDtypeStruct((p_.shape[1] // 2,) + p_.shape[2:], p_.dtype) for p_ in ps for _ in range(2)]
    return pl.pallas_call(
        body, name="relay_out", out_shape=shapes, in_specs=[ANY] * na, out_specs=[ANY] * (2 * na),
        scratch_shapes=[pltpu.SemaphoreType.DMA((2 * na,)), pltpu.SemaphoreType.DMA((2 * na,))],
    )(*ps)


def neighbour_exchange(qs):
    na = len(qs) // 2

    def body(*refs):
        srcs, outs = refs[:2 * na], refs[2 * na:3 * na]
        send_sems, recv_sems = refs[3 * na:]
        x, y, c = _me()
        cps = []
        for a in range(na):
            for j, to in enumerate(((1 - x, y, c), (x, 1 - y, c))):
                cps.append(pltpu.make_async_remote_copy(
                    src_ref=srcs[2 * a + j], dst_ref=outs[a].at[j], send_sem=send_sems.at[2 * a + j],
                    recv_sem=recv_sems.at[2 * a + j], device_id=to, device_id_type=MESH))
        for cp in cps:
            cp.start()
        for cp in cps:
            cp.wait()

    return pl.pallas_call(
        body, name="chip_exchange",
        out_shape=[jax.ShapeDtypeStruct((2,) + qs[2 * a].shape, qs[2 * a].dtype) for a in range(na)],
        in_specs=[ANY] * (2 * na), out_specs=[ANY] * na,
        scratch_shapes=[pltpu.SemaphoreType.DMA((2 * na,)), pltpu.SemaphoreType.DMA((2 * na,))],
    )(*qs)


class _Cfg:
    def __init__(self, x, a_log, sg_w, cv_ln_g, cv_w, conv_qkv):
        self.t, self.d = x.shape[1], x.shape[2]
        self.nl, self.h = a_log.shape
        self.dn = self.h * LANES
        self.g = sg_w.shape[1]
        self.sg = self.g * LANES
        self.cv = cv_ln_g.shape[1]
        self.kc = cv_w.shape[1]
        self.k4 = conv_qkv.shape[1]
        self.o_z = 3 * self.dn
        self.o_sg = 4 * self.dn
        self.o_cv = self.o_sg + 3 * self.sg
        self.o_ba = self.o_cv + 3 * self.cv
        self.npc = self.o_ba + LANES
        self.d_in = self.o_ba + 2 * self.h
        self.dmix = self.dn + self.sg + self.cv
        self.hb_fwd = _tile(self.h, 8, 1)
        self.hb_bwd = _tile(self.h, 8, 1)


def _runs(cfg):
    dn, h = cfg.dn, cfg.h
    runs = [(part * dn + hd * LANES, hd * 3 * LANES + part * LANES, LANES) for part in range(3) for hd in range(h)]
    return runs + [(3 * dn, 3 * dn, dn), (4 * dn, cfg.o_ba, 2 * h), (4 * dn + 2 * h, 4 * dn, cfg.o_ba - 4 * dn)]


def _assemble_perm(cfg, shards):
    cols = shards[0].shape[-1]
    pieces = []
    for nat, _, wdt in sorted(_runs(cfg), key=lambda r_: r_[1]):
        a = nat
        while a < nat + wdt:
            s = a // cols
            b = min(nat + wdt, (s + 1) * cols)
            pieces.append(shards[s][..., a - s * cols:b - s * cols])
            a = b
    pieces.append(jnp.zeros(shards[0].shape[:-1] + (cfg.npc - cfg.o_ba - 2 * cfg.h,), shards[0].dtype))
    return jnp.concatenate(pieces, axis=-1)


def _natural_pieces(cfg, s, cols):
    lo, hi = s * cols, (s + 1) * cols
    pieces = []
    for nat, perm, wdt in sorted(_runs(cfg)):
        a, b = max(nat, lo), min(nat + wdt, hi)
        if a < b:
            pieces.append((perm + a - nat, perm + b - nat))
    return pieces


def _layer_fwd(cfg, x, mod, lw):
    shift, scale, gate = mod
    p, ht = in_proj(x, shift, scale, lw["norm_g"], lw["wp"])
    qk_post = [_qk_post, _qk_post, _v_post]
    qkv = conv_fwd("dn_pre_fwd", cfg.k4, HALO4, lambda a: a, [(p, 0)], lw["conv_qkv"], qk_post, [], [],
                   3 * cfg.dn, 3 * LANES)
    wy = dn_wy_fwd(qkv, p, cfg.o_ba, lw["alog_b"], lw["dtb_b"], cfg.h)
    y_dn, ss = dn_seq_fwd(*wy[:6], p, cfg.o_z, lw["dn_norm_g"], cfg.h, cfg.hb_fwd)
    y_sg = sg_fwd(p, cfg.o_sg, cfg.sg, lw["sg_ln_g"], lw["sg_ln_b"], lw["sg_w"], lw["sg_bias_b"])
    cv_post = [_cv_post] * (cfg.cv // LANES)
    y_cv = conv_fwd("cv_fwd", cfg.kc, HALO31, _glu, [(p, cfg.o_cv), (p, cfg.o_cv + cfg.cv)], lw["cv_w"], cv_post,
                    [(p, cfg.o_cv + 2 * cfg.cv)], [lw["cv_b"], lw["cv_ln_g"], lw["cv_ln_b"]], cfg.cv, cfg.cv)
    xn, y, yt = out_proj(x, y_dn, y_sg, y_cv, lw["wo"], gate)
    return xn, dict(x=x, p=p, ht=ht, qkv=qkv, wy=wy, ss=ss, y=y, yt=yt)


def _layer_bwd(cfg, dxn, mod, lw, sv):
    shift, scale, gate = mod
    p = sv["p"]
    d_dn, d_sg, d_cv, dyb, dgate = out_proj_bwd(dxn, sv["y"], gate, lw["wo"], cfg.dn, cfg.sg, cfg.cv)
    g_wo = matmul_acc("w_out_grad", sv["yt"], dyb)
    cv_post = [_cv_post] * (cfg.cv // LANES)
    dcv, g_cvw, (g_cvb, g_cvlg, g_cvlb) = conv_bwd(
        "cv_bwd", cfg.kc, HALO31, _glu, [(p, cfg.o_cv), (p, cfg.o_cv + cfg.cv)], lw["cv_w"], cv_post,
        [(p, cfg.o_cv + 2 * cfg.cv)], [lw["cv_b"], lw["cv_ln_g"], lw["cv_ln_b"]], d_cv, cfg.cv, cfg.cv, tm_pref=256)
    dsg, g_sglg, g_sglb, g_sgw, g_sgb = sg_bwd(p, cfg.o_sg, cfg.sg, lw["sg_ln_g"], lw["sg_ln_b"], lw["sg_w"],
                                               lw["sg_bias_b"], d_sg)
    *dwy, dz, g_dng = dn_seq_bwd(*sv["wy"][:6], p, cfg.o_z, lw["dn_norm_g"], sv["ss"], d_dn, cfg.h, cfg.hb_bwd)
    dqkv, dba, g_al, g_dt = dn_wy_bwd(sv["qkv"], p, cfg.o_ba, lw["alog_b"], lw["dtb_b"], sv["wy"][6], *dwy, cfg.h)
    qk_post = [_qk_post, _qk_post, _v_post]
    dqkv_pre, g_cq, _ = conv_bwd("dn_pre_bwd", cfg.k4, HALO4, lambda a: a, [(p, 0)], lw["conv_qkv"], qk_post, [], [],
                                 dqkv, 3 * cfg.dn, 3 * LANES)
    dp = jnp.concatenate([dqkv_pre, dz.astype(BF16), dsg, dcv, dba.astype(BF16)], axis=1)
    g_wp = matmul_acc("w_in_grad", sv["ht"], dp)
    dx, g_ng, dscale, dshift = in_proj_bwd(dp, lw["wp"], sv["x"], dxn, shift, scale, lw["norm_g"])
    grads = dict(norm_g=g_ng, conv_qkv=g_cq, a_log=g_al[:cfg.h, 0], dt_bias=g_dt[:cfg.h, 0], dn_norm_g=g_dng,
                 sg_ln_g=g_sglg, sg_ln_b=g_sglb, sg_w=g_sgw, sg_b=g_sgb[:, :, 0], cv_w=g_cvw, cv_b=g_cvb,
                 cv_ln_g=g_cvlg, cv_ln_b=g_cvlb, wp=g_wp, wo=g_wo)
    return dx, grads, (dshift, dscale, dgate)


def _local_step(cfg, xs, tgt, mods, lws, fg):
    nl = len(lws)
    saved = []
    for l in range(nl):
        xs, sv = _layer_fwd(cfg, xs, mods[l], lws[l])
        saved.append(sv)
    loss_b, dx, g_fg = loss_head(xs, tgt, fg)
    lg = [None] * nl
    dmods = [None] * nl
    for l in reversed(range(nl)):
        dx, lg[l], dmods[l] = _layer_bwd(cfg, dx, mods[l], lws[l], saved[l])
    return loss_b, dx, g_fg, lg, dmods


SMALL = ("norm_g", "conv_qkv", "a_log", "dt_bias", "dn_norm_g", "sg_ln_g", "sg_ln_b", "sg_w", "sg_b", "cv_w",
         "cv_b", "cv_ln_g", "cv_ln_b", "final_g", "b_ada")
PACK_N = 1024


def _pack(arrs):
    flat = jnp.concatenate([a.reshape(-1).astype(F32) for a in arrs])
    rows = -(-flat.shape[0] // PACK_N)
    rows = -(-rows // 8) * 8
    return jnp.pad(flat, (0, rows * PACK_N - flat.shape[0])).reshape(rows, PACK_N)


def _unpack(buf, shapes):
    flat = buf.reshape(-1)
    out, o = [], 0
    for s in shapes:
        n = 1
        for d_ in s:
            n *= d_
        out.append(flat[o:o + n].reshape(s))
        o += n
    return out


def kernel(x, c, norm_g, w_ada, b_ada, w_in, conv_qkv, a_log, dt_bias, dn_norm_g, sg_ln_g, sg_ln_b, sg_w, sg_b, cv_w, cv_b, cv_ln_g, cv_ln_b, w_out, final_g, loss_target, m_norm_g, m_w_ada, m_b_ada, m_w_in, m_conv_qkv, m_a_log, m_dt_bias, m_dn_norm_g, m_sg_ln_g, m_sg_ln_b, m_sg_w, m_sg_b, m_cv_w, m_cv_b, m_cv_ln_g, m_cv_ln_b, m_w_out, m_final_g, v_norm_g, v_w_ada, v_b_ada, v_w_in, v_conv_qkv, v_a_log, v_dt_bias, v_dn_norm_g, v_sg_ln_g, v_sg_ln_b, v_sg_w, v_sg_b, v_cv_w, v_cv_b, v_cv_ln_g, v_cv_ln_b, v_w_out, v_final_g):
    cfg = _Cfg(x, a_log, sg_w, cv_ln_g, cv_w, conv_qkv)
    nl, d, t, h = cfg.nl, cfg.d, cfg.t, cfg.h
    lh = nl // 2
    ax, ay, ac = _me()
    chip = 2 * ax + ay
    dev = 2 * chip + ac
    wts = dict(norm_g=norm_g, w_ada=w_ada, b_ada=b_ada, w_in=w_in, conv_qkv=conv_qkv, a_log=a_log, dt_bias=dt_bias,
               dn_norm_g=dn_norm_g, sg_ln_g=sg_ln_g, sg_ln_b=sg_ln_b, sg_w=sg_w, sg_b=sg_b, cv_w=cv_w, cv_b=cv_b,
               cv_ln_g=cv_ln_g, cv_ln_b=cv_ln_b, w_out=w_out, final_g=final_g)
    mom = dict(norm_g=m_norm_g, w_ada=m_w_ada, b_ada=m_b_ada, w_in=m_w_in, conv_qkv=m_conv_qkv, a_log=m_a_log,
               dt_bias=m_dt_bias, dn_norm_g=m_dn_norm_g, sg_ln_g=m_sg_ln_g, sg_ln_b=m_sg_ln_b, sg_w=m_sg_w,
               sg_b=m_sg_b, cv_w=m_cv_w, cv_b=m_cv_b, cv_ln_g=m_cv_ln_g, cv_ln_b=m_cv_ln_b, w_out=m_w_out,
               final_g=m_final_g)
    vel = dict(norm_g=v_norm_g, w_ada=v_w_ada, b_ada=v_b_ada, w_in=v_w_in, conv_qkv=v_conv_qkv, a_log=v_a_log,
               dt_bias=v_dt_bias, dn_norm_g=v_dn_norm_g, sg_ln_g=v_sg_ln_g, sg_ln_b=v_sg_ln_b, sg_w=v_sg_w,
               sg_b=v_sg_b, cv_w=v_cv_w, cv_b=v_cv_b, cv_ln_g=v_cv_ln_g, cv_ln_b=v_cv_ln_b, w_out=v_w_out,
               final_g=v_final_g)
    ada_cols = w_ada.shape[2]
    in_cols = w_in.shape[2]
    out_rows = w_out.shape[1]
    cq_cols = conv_qkv.shape[2]
    cvw_cols = cv_w.shape[2]

    c_all = all_gather8(jnp.pad(c, ((0, 7), (0, 0)))).reshape(NDEV, 8, d)[:, 0, :]
    b_loc = lax.dynamic_slice_in_dim(b_ada, chip * ada_cols, ada_cols, axis=1)[:, None, :]
    mod_part = ada_fwd(c_all, w_ada, b_loc)
    mod_all = all_gather8(mod_part.reshape(nl * NDEV, ada_cols)).reshape(NDEV, nl, NDEV, ada_cols)
    mod_me = lax.dynamic_index_in_dim(mod_all[0::2], dev, axis=2, keepdims=False)
    mod_me = jnp.moveaxis(mod_me, 0, 1).reshape(nl, 3, 1, d)

    win_b = w_in.astype(BF16).reshape(2, 2, lh * d // 2, in_cols)
    wout_b = w_out.astype(BF16).reshape(2, 2, lh * out_rows // 2, d)
    win_all, wout_all = gather_weights([win_b, wout_b])
    win_all = lax.dynamic_update_index_in_dim(win_all, win_b, chip, axis=0)
    wout_all = lax.dynamic_update_index_in_dim(wout_all, wout_b, chip, axis=0)
    win_all = win_all.reshape(NCHIP, nl, d, in_cols)
    wp_all = [_assemble_perm(cfg, [win_all[s, l] for s in range(NCHIP)]) for l in range(nl)]
    wout_all = wout_all.reshape(NCHIP, nl, out_rows, d)
    wo_all = [jnp.concatenate([wout_all[s, l] for s in range(NCHIP)], axis=0) for l in range(nl)]

    cq_all = all_gather8(conv_qkv.reshape(nl * cfg.k4, cq_cols)).reshape(NDEV, nl, cfg.k4, cq_cols)[0::2]
    cq_full = jnp.moveaxis(cq_all, 0, 2).reshape(nl, cfg.k4, NCHIP * cq_cols)
    cq_perm = _perm_cols_qkv(cfg, cq_full)
    kcp = -(-cfg.kc // 8) * 8
    cvw_all = all_gather8(jnp.pad(cv_w, ((0, 0), (0, kcp - cfg.kc), (0, 0))).reshape(nl * kcp, cvw_cols))
    cvw_all = cvw_all.reshape(NDEV, nl, kcp, cvw_cols)[0::2]
    cvw_full = jnp.moveaxis(cvw_all, 0, 2).reshape(nl, kcp, NCHIP * cvw_cols)[:, :cfg.kc]

    hp = -(-h // 8) * 8
    lws = []
    for l in range(nl):
        lws.append(dict(
            norm_g=norm_g[l][None], wp=wp_all[l], wo=wo_all[l], conv_qkv=cq_perm[l],
            alog_b=jnp.pad(jnp.broadcast_to(a_log[l][:, None], (h, LANES)), ((0, hp - h), (0, 0))),
            dtb_b=jnp.pad(jnp.broadcast_to(dt_bias[l][:, None], (h, LANES)), ((0, hp - h), (0, 0))),
            dn_norm_g=dn_norm_g[l][None], sg_ln_g=sg_ln_g[l][None], sg_ln_b=sg_ln_b[l][None], sg_w=sg_w[l],
            sg_bias_b=jnp.broadcast_to(sg_b[l][:, :, None], (cfg.g, LANES, LANES)),
            cv_w=cvw_full[l], cv_b=cv_b[l][None], cv_ln_g=cv_ln_g[l][None], cv_ln_b=cv_ln_b[l][None]))

    mods = [(mod_me[l, 0], mod_me[l, 1], mod_me[l, 2]) for l in range(nl)]
    loss_b, dx, g_fg, lg, dmods = _local_step(cfg, x[0], loss_target[0], mods, lws, final_g[None])
    grad_x = dx[None]

    dmod = jnp.stack([jnp.concatenate(dm, axis=1)[0] for dm in dmods])
    stack = lambda k: jnp.stack([g_[k] for g_ in lg])
    small_local = [stack(k).reshape(wts_shape) for k, wts_shape in
                   (("norm_g", (nl, d)), ("conv_qkv", (nl, cfg.k4, 3 * cfg.dn)), ("a_log", (nl, h)),
                    ("dt_bias", (nl, h)), ("dn_norm_g", (nl, LANES)), ("sg_ln_g", (nl, cfg.sg)),
                    ("sg_ln_b", (nl, cfg.sg)), ("sg_w", (nl, cfg.g, LANES, LANES)), ("sg_b", (nl, cfg.g, LANES)),
                    ("cv_w", (nl, cfg.kc, cfg.cv)), ("cv_b", (nl, cfg.cv)), ("cv_ln_g", (nl, cfg.cv)),
                    ("cv_ln_b", (nl, cfg.cv)))]
    small_local[1] = _unperm_cols_qkv(cfg, small_local[1])
    small_local += [g_fg[0], dmod, loss_b[0, 0:1]]
    shapes = [a.shape for a in small_local]
    packed = _pack(small_local)
    rows = packed.shape[0]
    gathered = all_gather8(packed).reshape(NDEV, rows, PACK_N)
    summed = _unpack(sum8(gathered), shapes)
    sgrads = dict(zip(SMALL, summed[:15]))
    loss = summed[15][0]
    sgrads["conv_qkv"] = lax.dynamic_slice_in_dim(sgrads["conv_qkv"], chip * cq_cols, cq_cols, axis=2)
    sgrads["cv_w"] = lax.dynamic_slice_in_dim(sgrads["cv_w"], chip * cvw_cols, cvw_cols, axis=2)

    off = sum(math.prod(s) for s in shapes[:14])
    dmod_all = gathered.reshape(NDEV, rows * PACK_N)[:, off:off + nl * 3 * d].reshape(NDEV, nl, 3 * d)
    dmod_loc = jnp.moveaxis(lax.dynamic_slice_in_dim(dmod_all, chip * ada_cols, ada_cols, axis=2), 0, 1)

    shard = lambda g_, s: jnp.concatenate([g_[:, a:b] for a, b in _natural_pieces(cfg, s, in_cols)], axis=-1)
    g_in = jnp.stack([shard(lg[hh * lh + j]["wp"], s) for hh in range(2) for s in range(NCHIP) for j in range(lh)])
    g_in = g_in.reshape(2, NCHIP, lh * d, in_cols)
    g_out = jnp.stack([lg[hh * lh + j]["wo"][s * out_rows:(s + 1) * out_rows] for hh in range(2) for s in range(NCHIP)
                       for j in range(lh)]).reshape(2, NCHIP, lh * out_rows, d)
    cflag = jnp.full((1, LANES), ac, F32)
    r1_in, r1_out = sibling_swap("swap_halves", [g_in, g_out], other_half=True)
    p_in, p_out = pair_sum(cflag, g_in, r1_in), pair_sum(cflag, g_out, r1_out)
    rx_in, ry_in, rx_out, ry_out = relay_out([p_in, p_out])
    to_x = lambda p_: lax.dynamic_index_in_dim(p_, 2 * (1 - ax) + ay, axis=0, keepdims=False)
    to_y = lambda p_: lax.dynamic_index_in_dim(p_, 2 * ax + 1 - ay, axis=0, keepdims=False)
    g_wada, d_wada, nm_wada, nv_wada, r2_in, r2_out = ada_bwd(
        c_all.T, dmod_loc, w_ada, m_w_ada, v_w_ada,
        [*relay_sum(to_x(p_in), to_y(p_in), rx_in, ry_in), *relay_sum(to_x(p_out), to_y(p_out), rx_out, ry_out)])
    mine = lambda g_: lax.dynamic_index_in_dim(g_, chip, axis=0, keepdims=False)
    keep = lambda g_: lax.dynamic_index_in_dim(g_, ac, axis=0, keepdims=False)
    h_in = chip_sum(mine(keep(g_in)), mine(r1_in), r2_in)
    h_out = chip_sum(mine(keep(g_out)), mine(r1_out), r2_out)
    o_in, o_out = sibling_swap("join_halves", [h_in, h_out])

    v3 = lambda a, r_, c_: a.reshape(2, lh * r_, c_)
    grad_w_in, d_in_, nm_in, nv_in = adamw_halves("adamw_w_in", cflag, v3(w_in, d, in_cols), h_in, o_in,
                                                  v3(m_w_in, d, in_cols), v3(v_w_in, d, in_cols))
    grad_w_out, d_out_, nm_out, nv_out = adamw_halves("adamw_w_out", cflag, v3(w_out, out_rows, d), h_out, o_out,
                                                      v3(m_w_out, out_rows, d), v3(v_w_out, out_rows, d))
    grad_w_in = grad_w_in.reshape(w_in.shape)
    grad_w_out = grad_w_out.reshape(w_out.shape)
    sshapes = [wts[k].shape for k in SMALL]
    pk = lambda dct: _pack([dct[k] for k in SMALL])
    d_s, m_s, v_s = adamw("adamw_small", pk(wts), pk(sgrads), pk(mom), pk(vel))
    d_small = dict(zip(SMALL, _unpack(d_s, sshapes)))
    m_small = dict(zip(SMALL, _unpack(m_s, sshapes)))
    v_small = dict(zip(SMALL, _unpack(v_s, sshapes)))

    grads = dict(sgrads, w_ada=g_wada, w_in=grad_w_in, w_out=grad_w_out)
    deltas = dict(d_small, w_ada=d_wada, w_in=d_in_.reshape(w_in.shape), w_out=d_out_.reshape(w_out.shape))
    new_m = dict(m_small, w_ada=nm_wada, w_in=nm_in.reshape(w_in.shape), w_out=nm_out.reshape(w_out.shape))
    new_v = dict(v_small, w_ada=nv_wada, w_in=nv_in.reshape(w_in.shape), w_out=nv_out.reshape(w_out.shape))
    order = ("norm_g", "w_ada", "b_ada", "w_in", "conv_qkv", "a_log", "dt_bias", "dn_norm_g", "sg_ln_g", "sg_ln_b",
             "sg_w", "sg_b", "cv_w", "cv_b", "cv_ln_g", "cv_ln_b", "w_out", "final_g")
    return (loss, grad_x, *[grads[k] for k in order], *[deltas[k] for k in order], *[new_m[k] for k in order],
            *[new_v[k] for k in order])


def _perm_cols_qkv(cfg, w):
    lead = w.shape[:-1]
    return jnp.moveaxis(w.reshape(lead + (3, cfg.h, LANES)), -3, -2).reshape(lead + (3 * cfg.dn,))


def _unperm_cols_qkv(cfg, w):
    lead = w.shape[:-1]
    return jnp.moveaxis(w.reshape(lead + (cfg.h, 3, LANES)), -3, -2).reshape(lead + (3 * cfg.dn,))
```
